```python
import math
import jax, jax.numpy as jnp
from jax import lax
import numpy as np

D_MODEL = 1024
BATCH = 8
SEQ = 8192
DEPTH = 2

N_META = 16
MLA_HEADS = 8
Q_LORA = 768
KV_LORA = 256
QK_NOPE = 128
QK_ROPE = 64
V_HEAD = 128
ROPE_THETA = 10000.0
Q_BLOCK = 128
NEG_INF = -1e30
SSD_INNER = 2 * D_MODEL
SSD_HEAD_DIM = 64
SSD_HEADS = SSD_INNER // SSD_HEAD_DIM
SSD_GROUPS = 4
SSD_HEADS_PER_GROUP = SSD_HEADS // SSD_GROUPS
SSD_STATE = 128
SSD_CONV = 4
SSD_CONV_DIM = SSD_INNER + 2 * SSD_GROUPS * SSD_STATE
CHUNK = 128
DT_MIN = 0.001
DT_MAX = 0.1
D_FF = 2816
FFN_CONV = 3
LN_EPS = 1e-5
RMS_EPS = 1e-6
DEEPNORM_ALPHA = (2 * DEPTH) ** 0.25
DEEPNORM_BETA = (8 * DEPTH) ** -0.25
IN_SIZES = (Q_LORA, KV_LORA, QK_ROPE, SSD_INNER, SSD_CONV_DIM, SSD_HEADS, D_MODEL, D_MODEL)
IN_COLS = sum(IN_SIZES)

kernel_name = "hybrid_mla_ssd_gated_deepnorm"


def layer_norm(x, g, b):
    xf = x.astype(jnp.float32)
    mu = jnp.mean(xf, axis=-1, keepdims=True)
    var = jnp.mean(jnp.square(xf - mu), axis=-1, keepdims=True)
    y = (xf - mu) * lax.rsqrt(var + LN_EPS) * g.astype(jnp.float32) + b.astype(jnp.float32)
    return y.astype(x.dtype)


def rms_norm(x, g):
    xf = x.astype(jnp.float32)
    y = xf * lax.rsqrt(jnp.mean(xf * xf, axis=-1, keepdims=True) + RMS_EPS) * g.astype(jnp.float32)
    return y.astype(x.dtype)


def causal_dwconv(x, w, b):
    k = w.shape[0]
    y = lax.conv_general_dilated(
        x, w[:, None, :].astype(x.dtype), window_strides=(1,), padding=((k - 1, 0),),
        dimension_numbers=("NWC", "WIO", "NWC"), feature_group_count=x.shape[-1])
    return y + b.astype(x.dtype)


def front_pad(t, pad):
    return jnp.pad(t, [(0, 0), (pad, 0)] + [(0, 0)] * (t.ndim - 2))


def rope_tables(length):
    inv_freq = 1.0 / (ROPE_THETA ** (jnp.arange(0, QK_ROPE, 2, dtype=jnp.float32) / QK_ROPE))
    ang = jnp.arange(length, dtype=jnp.float32)[:, None] * inv_freq[None, :]
    ang = jnp.concatenate([ang, ang], axis=-1)
    return jnp.cos(ang), jnp.sin(ang)


def apply_rope(x, cos, sin):
    xf = x.astype(jnp.float32)
    x1, x2 = jnp.split(xf, 2, axis=-1)
    rot = jnp.concatenate([-x2, x1], axis=-1)
    return (xf * cos + rot * sin).astype(x.dtype)


def mla_branch(q_lat, kv_lat, k_pe, cos, sin, q_norm_g, w_q_b, kv_norm_g, w_kv_b, w_o):
    b, l, _ = q_lat.shape
    q = (rms_norm(q_lat, q_norm_g) @ w_q_b).reshape(b, l, MLA_HEADS, QK_NOPE + QK_ROPE)
    q_nope = q[..., :QK_NOPE]
    q_pe = apply_rope(q[..., QK_NOPE:], cos[:, None, :], sin[:, None, :])
    kv = (rms_norm(kv_lat, kv_norm_g) @ w_kv_b).reshape(b, l, MLA_HEADS, QK_NOPE + V_HEAD)
    k_nope, v = kv[..., :QK_NOPE], kv[..., QK_NOPE:]
    k_pe = apply_rope(k_pe, cos, sin)
    pad = (-l) % Q_BLOCK
    q_nope, q_pe, k_nope, k_pe, v = [front_pad(t, pad) for t in (q_nope, q_pe, k_nope, k_pe, v)]
    lp = l + pad
    scale = (QK_NOPE + QK_ROPE) ** -0.5
    key_pos = jnp.arange(lp)

    def attend_block(blk):
        start = blk * Q_BLOCK
        qn = lax.dynamic_slice_in_dim(q_nope, start, Q_BLOCK, axis=1)
        qr = lax.dynamic_slice_in_dim(q_pe, start, Q_BLOCK, axis=1)
        s = (jnp.einsum("bqhd,bkhd->bhqk", qn, k_nope).astype(jnp.float32)
             + jnp.einsum("bqhr,bkr->bhqk", qr, k_pe).astype(jnp.float32))
        q_pos = start + jnp.arange(Q_BLOCK)
        visible = (key_pos[None, :] <= q_pos[:, None]) & (key_pos[None, :] >= pad)
        p = jax.nn.softmax(jnp.where(visible, s * scale, NEG_INF), axis=-1)
        return jnp.einsum("bhqk,bkhd->bqhd", p.astype(v.dtype), v)

    o = lax.map(attend_block, jnp.arange(lp // Q_BLOCK))
    o = jnp.moveaxis(o, 0, 1).reshape(b, lp, MLA_HEADS * V_HEAD)[:, pad:]
    return o @ w_o


def ssd_branch(z, xbc, dt_raw, conv_w, conv_b, dt_bias, a_log, d_skip, norm_g, w_o):
    b, l, _ = xbc.shape
    dtype = xbc.dtype
    f32 = jnp.float32
    xbc = jax.nn.silu(causal_dwconv(xbc, conv_w, conv_b)).astype(f32)
    xs, bm, cm = jnp.split(xbc, [SSD_INNER, SSD_INNER + SSD_GROUPS * SSD_STATE], axis=-1)
    dt = jax.nn.softplus(dt_raw.astype(f32) + dt_bias.astype(f32))
    a = -jnp.exp(a_log.astype(f32))
    pad = (-l) % CHUNK
    lp = l + pad
    nc = lp // CHUNK
    g, e = SSD_GROUPS, SSD_HEADS_PER_GROUP
    dt_c = front_pad(dt, pad).reshape(b, nc, CHUNK, g, e)
    x_c = front_pad(xs, pad).reshape(b, nc, CHUNK, g, e, SSD_HEAD_DIM) * dt_c[..., None]
    b_c = front_pad(bm, pad).reshape(b, nc, CHUNK, g, SSD_STATE)
    c_c = front_pad(cm, pad).reshape(b, nc, CHUNK, g, SSD_STATE)
    a_c = jnp.transpose(dt_c * a.reshape(g, e), (0, 3, 4, 1, 2))
    a_cs = jnp.cumsum(a_c, axis=-1)
    causal = jnp.tril(jnp.ones((CHUNK, CHUNK), dtype=bool))
    decay = jnp.exp(jnp.where(causal, a_cs[..., :, None] - a_cs[..., None, :], -jnp.inf))
    cb = jnp.einsum("bclgn,bcsgn->bgcls", c_c, b_c)
    y_diag = jnp.einsum("bgecls,bcsgep->bclgep", cb[:, :, None] * decay, x_c)
    decay_states = jnp.exp(a_cs[..., -1:] - a_cs)
    states = jnp.einsum("bclgn,bgecl,bclgep->cbgepn", b_c, decay_states, x_c)
    chunk_decay = jnp.moveaxis(jnp.exp(a_cs[..., -1]), -1, 0)

    def carry_state(h, inp):
        s_c, d_c = inp
        return d_c[..., None, None] * h + s_c, h

    h0 = jnp.zeros(states.shape[1:], f32)
    _, prev = lax.scan(carry_state, h0, (states, chunk_decay))
    y_off = jnp.einsum("bclgn,cbgepn,bgecl->bclgep", c_c, prev, jnp.exp(a_cs))
    y = (y_diag + y_off).reshape(b, lp, SSD_HEADS, SSD_HEAD_DIM)[:, pad:]
    y = y + xs.reshape(b, l, SSD_HEADS, SSD_HEAD_DIM) * d_skip.astype(f32)[:, None]
    gs = SSD_INNER // SSD_GROUPS
    y = y.reshape(b, l, SSD_GROUPS, gs) * jax.nn.silu(z.astype(f32)).reshape(b, l, SSD_GROUPS, gs)
    y = y * lax.rsqrt(jnp.mean(y * y, axis=-1, keepdims=True) + RMS_EPS)
    y = y.reshape(b, l, SSD_INNER) * norm_g.astype(f32)
    return y.astype(dtype) @ w_o


def conv_glu_ffn(h, w_up, conv_w, conv_b, w_down):
    u = causal_dwconv(h @ w_up, conv_w, conv_b)
    gate, val = jnp.split(u, 2, axis=-1)
    return (jax.nn.silu(gate) * val) @ w_down


def _fwd_setup_inputs(seed: int = 0) -> dict:
    key = jax.random.key(seed)
    ks = iter(jax.random.split(key, 40))
    nrm = lambda shape, scale: jax.random.normal(next(ks), shape, jnp.float32) * scale
    gain = lambda shape: 1.0 + nrm(shape, 0.02)
    bias = lambda shape: nrm(shape, 0.02)
    L = DEPTH
    x = nrm((BATCH, SEQ, D_MODEL), 1.0)
    meta_tokens = nrm((N_META, D_MODEL), 1.0)
    emb_ln_g = gain((D_MODEL,))
    emb_ln_b = bias((D_MODEL,))
    w_in = nrm((L, D_MODEL, IN_COLS), D_MODEL ** -0.5)
    q_norm_g = gain((L, Q_LORA))
    w_q_b = nrm((L, Q_LORA, MLA_HEADS * (QK_NOPE + QK_ROPE)), Q_LORA ** -0.5)
    kv_norm_g = gain((L, KV_LORA))
    w_kv_b = nrm((L, KV_LORA, MLA_HEADS * (QK_NOPE + V_HEAD)), KV_LORA ** -0.5)
    w_o_attn = nrm((L, MLA_HEADS * V_HEAD, D_MODEL), (MLA_HEADS * V_HEAD) ** -0.5)
    ssd_conv_w = nrm((L, SSD_CONV, SSD_CONV_DIM), SSD_CONV ** -0.5)
    ssd_conv_b = bias((L, SSD_CONV_DIM))
    u = jax.random.uniform(next(ks), (L, SSD_HEADS), jnp.float32)
    dt0 = jnp.exp(u * (math.log(DT_MAX) - math.log(DT_MIN)) + math.log(DT_MIN))
    dt_bias = dt0 + jnp.log(-jnp.expm1(-dt0))
    a_log = jnp.log(jax.random.uniform(next(ks), (L, SSD_HEADS), jnp.float32, 1.0, 16.0))
    d_skip = gain((L, SSD_HEADS))
    ssd_norm_g = gain((L, SSD_INNER))
    w_o_ssd = nrm((L, SSD_INNER, D_MODEL), SSD_INNER ** -0.5)
    w_out = nrm((L, D_MODEL, D_MODEL), DEEPNORM_BETA * D_MODEL ** -0.5)
    ln1_g = gain((L, D_MODEL))
    ln1_b = bias((L, D_MODEL))
    w_up = nrm((L, D_MODEL, 2 * D_FF), D_MODEL ** -0.5)
    ffn_conv_w = nrm((L, FFN_CONV, 2 * D_FF), FFN_CONV ** -0.5)
    ffn_conv_b = bias((L, 2 * D_FF))
    w_down = nrm((L, D_FF, D_MODEL), DEEPNORM_BETA * D_FF ** -0.5)
    ln2_g = gain((L, D_MODEL))
    ln2_b = bias((L, D_MODEL))
    return {"x": x, "meta_tokens": meta_tokens, "emb_ln_g": emb_ln_g, "emb_ln_b": emb_ln_b,
            "w_in": w_in, "q_norm_g": q_norm_g, "w_q_b": w_q_b, "kv_norm_g": kv_norm_g,
            "w_kv_b": w_kv_b, "w_o_attn": w_o_attn, "ssd_conv_w": ssd_conv_w, "ssd_conv_b": ssd_conv_b,
            "dt_bias": dt_bias, "a_log": a_log, "d_skip": d_skip, "ssd_norm_g": ssd_norm_g,
            "w_o_ssd": w_o_ssd, "w_out": w_out, "ln1_g": ln1_g, "ln1_b": ln1_b, "w_up": w_up,
            "ffn_conv_w": ffn_conv_w, "ffn_conv_b": ffn_conv_b, "w_down": w_down,
            "ln2_g": ln2_g, "ln2_b": ln2_b}


def _fwd_reference(x, meta_tokens, emb_ln_g, emb_ln_b, w_in, q_norm_g, w_q_b, kv_norm_g, w_kv_b,
              w_o_attn, ssd_conv_w, ssd_conv_b, dt_bias, a_log, d_skip, ssd_norm_g, w_o_ssd,
              w_out, ln1_g, ln1_b, w_up, ffn_conv_w, ffn_conv_b, w_down, ln2_g, ln2_b):
    b = x.shape[0]
    meta = jnp.broadcast_to(meta_tokens.astype(x.dtype)[None], (b, N_META, D_MODEL))
    h = layer_norm(jnp.concatenate([meta, x], axis=1), emb_ln_g, emb_ln_b)
    cos, sin = rope_tables(h.shape[1])
    splits = np.cumsum(IN_SIZES)[:-1].tolist()
    for i in range(DEPTH):
        proj = h @ w_in[i]
        q_lat, kv_lat, k_pe, z, xbc, dt_raw, g_attn, g_ssd = jnp.split(proj, splits, axis=-1)
        y_attn = mla_branch(q_lat, kv_lat, k_pe, cos, sin, q_norm_g[i], w_q_b[i],
                            kv_norm_g[i], w_kv_b[i], w_o_attn[i])
        y_ssd = ssd_branch(z, xbc, dt_raw, ssd_conv_w[i], ssd_conv_b[i], dt_bias[i], a_log[i],
                           d_skip[i], ssd_norm_g[i], w_o_ssd[i])
        mixed = jax.nn.sigmoid(g_attn) * y_attn + jax.nn.sigmoid(g_ssd) * y_ssd
        h = layer_norm(DEEPNORM_ALPHA * h + mixed @ w_out[i], ln1_g[i], ln1_b[i])
        ffn = conv_glu_ffn(h, w_up[i], ffn_conv_w[i], ffn_conv_b[i], w_down[i])
        h = layer_norm(DEEPNORM_ALPHA * h + ffn, ln2_g[i], ln2_b[i])
    return h[:, N_META:]


import jax as _jax
import jax.numpy as _jnp

TWIN_FORMAT = 'train_step'
FWD_PARAMS = ['x', 'meta_tokens', 'emb_ln_g', 'emb_ln_b', 'w_in', 'q_norm_g', 'w_q_b', 'kv_norm_g', 'w_kv_b', 'w_o_attn', 'ssd_conv_w', 'ssd_conv_b', 'dt_bias', 'a_log', 'd_skip', 'ssd_norm_g', 'w_o_ssd', 'w_out', 'ln1_g', 'ln1_b', 'w_up', 'ffn_conv_w', 'ffn_conv_b', 'w_down', 'ln2_g', 'ln2_b']
TWIN_WEIGHTS = ['meta_tokens', 'emb_ln_g', 'emb_ln_b', 'w_in', 'q_norm_g', 'w_q_b', 'kv_norm_g', 'w_kv_b', 'w_o_attn', 'ssd_conv_w', 'ssd_conv_b', 'dt_bias', 'a_log', 'd_skip', 'ssd_norm_g', 'w_o_ssd', 'w_out', 'ln1_g', 'ln1_b', 'w_up', 'ffn_conv_w', 'ffn_conv_b', 'w_down', 'ln2_g', 'ln2_b']
TWIN_DIFF_INPUT = 'x'
TWIN_INPUTS = ['x', 'meta_tokens', 'emb_ln_g', 'emb_ln_b', 'w_in', 'q_norm_g', 'w_q_b', 'kv_norm_g', 'w_kv_b', 'w_o_attn', 'ssd_conv_w', 'ssd_conv_b', 'dt_bias', 'a_log', 'd_skip', 'ssd_norm_g', 'w_o_ssd', 'w_out', 'ln1_g', 'ln1_b', 'w_up', 'ffn_conv_w', 'ffn_conv_b', 'w_down', 'ln2_g', 'ln2_b', 'loss_target', 'm_meta_tokens', 'm_emb_ln_g', 'm_emb_ln_b', 'm_w_in', 'm_q_norm_g', 'm_w_q_b', 'm_kv_norm_g', 'm_w_kv_b', 'm_w_o_attn', 'm_ssd_conv_w', 'm_ssd_conv_b', 'm_dt_bias', 'm_a_log', 'm_d_skip', 'm_ssd_norm_g', 'm_w_o_ssd', 'm_w_out', 'm_ln1_g', 'm_ln1_b', 'm_w_up', 'm_ffn_conv_w', 'm_ffn_conv_b', 'm_w_down', 'm_ln2_g', 'm_ln2_b', 'v_meta_tokens', 'v_emb_ln_g', 'v_emb_ln_b', 'v_w_in', 'v_q_norm_g', 'v_w_q_b', 'v_kv_norm_g', 'v_w_kv_b', 'v_w_o_attn', 'v_ssd_conv_w', 'v_ssd_conv_b', 'v_dt_bias', 'v_a_log', 'v_d_skip', 'v_ssd_norm_g', 'v_w_o_ssd', 'v_w_out', 'v_ln1_g', 'v_ln1_b', 'v_w_up', 'v_ffn_conv_w', 'v_ffn_conv_b', 'v_w_down', 'v_ln2_g', 'v_ln2_b']
TWIN_OUTPUTS = ['loss', 'grad_x', 'grad_meta_tokens', 'grad_emb_ln_g', 'grad_emb_ln_b', 'grad_w_in', 'grad_q_norm_g', 'grad_w_q_b', 'grad_kv_norm_g', 'grad_w_kv_b', 'grad_w_o_attn', 'grad_ssd_conv_w', 'grad_ssd_conv_b', 'grad_dt_bias', 'grad_a_log', 'grad_d_skip', 'grad_ssd_norm_g', 'grad_w_o_ssd', 'grad_w_out', 'grad_ln1_g', 'grad_ln1_b', 'grad_w_up', 'grad_ffn_conv_w', 'grad_ffn_conv_b', 'grad_w_down', 'grad_ln2_g', 'grad_ln2_b', 'delta_meta_tokens', 'delta_emb_ln_g', 'delta_emb_ln_b', 'delta_w_in', 'delta_q_norm_g', 'delta_w_q_b', 'delta_kv_norm_g', 'delta_w_kv_b', 'delta_w_o_attn', 'delta_ssd_conv_w', 'delta_ssd_conv_b', 'delta_dt_bias', 'delta_a_log', 'delta_d_skip', 'delta_ssd_norm_g', 'delta_w_o_ssd', 'delta_w_out', 'delta_ln1_g', 'delta_ln1_b', 'delta_w_up', 'delta_ffn_conv_w', 'delta_ffn_conv_b', 'delta_w_down', 'delta_ln2_g', 'delta_ln2_b', 'new_m_meta_tokens', 'new_m_emb_ln_g', 'new_m_emb_ln_b', 'new_m_w_in', 'new_m_q_norm_g', 'new_m_w_q_b', 'new_m_kv_norm_g', 'new_m_w_kv_b', 'new_m_w_o_attn', 'new_m_ssd_conv_w', 'new_m_ssd_conv_b', 'new_m_dt_bias', 'new_m_a_log', 'new_m_d_skip', 'new_m_ssd_norm_g', 'new_m_w_o_ssd', 'new_m_w_out', 'new_m_ln1_g', 'new_m_ln1_b', 'new_m_w_up', 'new_m_ffn_conv_w', 'new_m_ffn_conv_b', 'new_m_w_down', 'new_m_ln2_g', 'new_m_ln2_b', 'new_v_meta_tokens', 'new_v_emb_ln_g', 'new_v_emb_ln_b', 'new_v_w_in', 'new_v_q_norm_g', 'new_v_w_q_b', 'new_v_kv_norm_g', 'new_v_w_kv_b', 'new_v_w_o_attn', 'new_v_ssd_conv_w', 'new_v_ssd_conv_b', 'new_v_dt_bias', 'new_v_a_log', 'new_v_d_skip', 'new_v_ssd_norm_g', 'new_v_w_o_ssd', 'new_v_w_out', 'new_v_ln1_g', 'new_v_ln1_b', 'new_v_w_up', 'new_v_ffn_conv_w', 'new_v_ffn_conv_b', 'new_v_w_down', 'new_v_ln2_g', 'new_v_ln2_b']
TWIN_LEAF_KINDS = {'loss': 'loss', 'grad_x': 'grad_x', 'grad_meta_tokens': 'grad_w', 'grad_emb_ln_g': 'grad_w', 'grad_emb_ln_b': 'grad_w', 'grad_w_in': 'grad_w', 'grad_q_norm_g': 'grad_w', 'grad_w_q_b': 'grad_w', 'grad_kv_norm_g': 'grad_w', 'grad_w_kv_b': 'grad_w', 'grad_w_o_attn': 'grad_w', 'grad_ssd_conv_w': 'grad_w', 'grad_ssd_conv_b': 'grad_w', 'grad_dt_bias': 'grad_w', 'grad_a_log': 'grad_w', 'grad_d_skip': 'grad_w', 'grad_ssd_norm_g': 'grad_w', 'grad_w_o_ssd': 'grad_w', 'grad_w_out': 'grad_w', 'grad_ln1_g': 'grad_w', 'grad_ln1_b': 'grad_w', 'grad_w_up': 'grad_w', 'grad_ffn_conv_w': 'grad_w', 'grad_ffn_conv_b': 'grad_w', 'grad_w_down': 'grad_w', 'grad_ln2_g': 'grad_w', 'grad_ln2_b': 'grad_w', 'delta_meta_tokens': 'delta_w', 'delta_emb_ln_g': 'delta_w', 'delta_emb_ln_b': 'delta_w', 'delta_w_in': 'delta_w', 'delta_q_norm_g': 'delta_w', 'delta_w_q_b': 'delta_w', 'delta_kv_norm_g': 'delta_w', 'delta_w_kv_b': 'delta_w', 'delta_w_o_attn': 'delta_w', 'delta_ssd_conv_w': 'delta_w', 'delta_ssd_conv_b': 'delta_w', 'delta_dt_bias': 'delta_w', 'delta_a_log': 'delta_w', 'delta_d_skip': 'delta_w', 'delta_ssd_norm_g': 'delta_w', 'delta_w_o_ssd': 'delta_w', 'delta_w_out': 'delta_w', 'delta_ln1_g': 'delta_w', 'delta_ln1_b': 'delta_w', 'delta_w_up': 'delta_w', 'delta_ffn_conv_w': 'delta_w', 'delta_ffn_conv_b': 'delta_w', 'delta_w_down': 'delta_w', 'delta_ln2_g': 'delta_w', 'delta_ln2_b': 'delta_w', 'new_m_meta_tokens': 'new_m', 'new_m_emb_ln_g': 'new_m', 'new_m_emb_ln_b': 'new_m', 'new_m_w_in': 'new_m', 'new_m_q_norm_g': 'new_m', 'new_m_w_q_b': 'new_m', 'new_m_kv_norm_g': 'new_m', 'new_m_w_kv_b': 'new_m', 'new_m_w_o_attn': 'new_m', 'new_m_ssd_conv_w': 'new_m', 'new_m_ssd_conv_b': 'new_m', 'new_m_dt_bias': 'new_m', 'new_m_a_log': 'new_m', 'new_m_d_skip': 'new_m', 'new_m_ssd_norm_g': 'new_m', 'new_m_w_o_ssd': 'new_m', 'new_m_w_out': 'new_m', 'new_m_ln1_g': 'new_m', 'new_m_ln1_b': 'new_m', 'new_m_w_up': 'new_m', 'new_m_ffn_conv_w': 'new_m', 'new_m_ffn_conv_b': 'new_m', 'new_m_w_down': 'new_m', 'new_m_ln2_g': 'new_m', 'new_m_ln2_b': 'new_m', 'new_v_meta_tokens': 'new_v', 'new_v_emb_ln_g': 'new_v', 'new_v_emb_ln_b': 'new_v', 'new_v_w_in': 'new_v', 'new_v_q_norm_g': 'new_v', 'new_v_w_q_b': 'new_v', 'new_v_kv_norm_g': 'new_v', 'new_v_w_kv_b': 'new_v', 'new_v_w_o_attn': 'new_v', 'new_v_ssd_conv_w': 'new_v', 'new_v_ssd_conv_b': 'new_v', 'new_v_dt_bias': 'new_v', 'new_v_a_log': 'new_v', 'new_v_d_skip': 'new_v', 'new_v_ssd_norm_g': 'new_v', 'new_v_w_o_ssd': 'new_v', 'new_v_w_out': 'new_v', 'new_v_ln1_g': 'new_v', 'new_v_ln1_b': 'new_v', 'new_v_w_up': 'new_v', 'new_v_ffn_conv_w': 'new_v', 'new_v_ffn_conv_b': 'new_v', 'new_v_w_down': 'new_v', 'new_v_ln2_g': 'new_v', 'new_v_ln2_b': 'new_v'}


def _forward(args):
    return _fwd_reference(*[args[k] for k in FWD_PARAMS])


def _output_shape():
    def fwd():
        inp = _fwd_setup_inputs(0)
        return _fwd_reference(*[inp[k] for k in FWD_PARAMS])
    out = _jax.eval_shape(fwd)
    return out.shape, out.dtype

N_MICROBATCH = 1
ADAM_LR = 0.001
ADAM_B1 = 0.9
ADAM_B2 = 0.999
ADAM_EPS = 1e-08
ADAM_WD = 0.01
ADAM_STEP = 10
PER_EXAMPLE_BATCH_AXIS = {'x': 0, 'loss_target': 0}
SHARED_INPUTS = []
_WEIGHT_DTYPES = {'meta_tokens': _jnp.float32, 'emb_ln_g': _jnp.float32, 'emb_ln_b': _jnp.float32, 'w_in': _jnp.float32, 'q_norm_g': _jnp.float32, 'w_q_b': _jnp.float32, 'kv_norm_g': _jnp.float32, 'w_kv_b': _jnp.float32, 'w_o_attn': _jnp.float32, 'ssd_conv_w': _jnp.float32, 'ssd_conv_b': _jnp.float32, 'dt_bias': _jnp.float32, 'a_log': _jnp.float32, 'd_skip': _jnp.float32, 'ssd_norm_g': _jnp.float32, 'w_o_ssd': _jnp.float32, 'w_out': _jnp.float32, 'ln1_g': _jnp.float32, 'ln1_b': _jnp.float32, 'w_up': _jnp.float32, 'ffn_conv_w': _jnp.float32, 'ffn_conv_b': _jnp.float32, 'w_down': _jnp.float32, 'ln2_g': _jnp.float32, 'ln2_b': _jnp.float32}
MOMENT_SCALE = {'meta_tokens': 3.433945e-03, 'emb_ln_g': 1.928366e+00, 'emb_ln_b': 1.019925e+00, 'w_in': 2.754016e-02, 'q_norm_g': 9.265487e-03, 'w_q_b': 6.534022e-03, 'kv_norm_g': 2.343855e-02, 'w_kv_b': 8.019673e-03, 'w_o_attn': 9.213294e-03, 'ssd_conv_w': 3.206991e-02, 'ssd_conv_b': 5.363574e-02, 'dt_bias': 5.814203e-02, 'a_log': 1.149291e-01, 'd_skip': 1.955803e-01, 'ssd_norm_g': 3.856808e-02, 'w_o_ssd': 5.477569e-02, 'w_out': 1.089879e-01, 'ln1_g': 2.161153e+00, 'ln1_b': 1.028097e+00, 'w_up': 3.374513e-02, 'ffn_conv_w': 3.345199e-02, 'ffn_conv_b': 4.350349e-02, 'w_down': 1.101208e-01, 'ln2_g': 4.541847e+01, 'ln2_b': 2.594296e+00}


def _to_microbatches(a, axis):
    t = _jnp.moveaxis(a, axis, 0)
    t = t.reshape((N_MICROBATCH, t.shape[0] // N_MICROBATCH) + t.shape[1:])
    return _jnp.moveaxis(t, 1, axis + 1)


def setup_inputs(seed: int = 0) -> dict:
    inp = _fwd_setup_inputs(seed)
    key = _jax.random.fold_in(_jax.random.key(seed), 7919)
    shape, _ = _output_shape()
    out = dict(inp)
    out["loss_target"] = _jax.random.normal(_jax.random.fold_in(key, 0), shape, _jnp.float32)
    for i, name in enumerate(TWIN_WEIGHTS):
        w = inp[name].astype(_jnp.float32)
        if MOMENT_SCALE is None:
            s = _jnp.sqrt(_jnp.mean(_jnp.square(w)) + 1e-30)
        else:
            s = MOMENT_SCALE[name]
        km, kv = _jax.random.split(_jax.random.fold_in(key, i + 1))
        out[name] = w
        out["m_" + name] = s * _jax.random.normal(km, w.shape, _jnp.float32)
        out["v_" + name] = (s * s) * _jax.random.uniform(kv, w.shape, _jnp.float32, 0.5, 1.5)
    if N_MICROBATCH > 1:
        for name, axis in PER_EXAMPLE_BATCH_AXIS.items():
            out[name] = _to_microbatches(out[name], axis)
    return {'x': out['x'], 'meta_tokens': out['meta_tokens'], 'emb_ln_g': out['emb_ln_g'], 'emb_ln_b': out['emb_ln_b'], 'w_in': out['w_in'], 'q_norm_g': out['q_norm_g'], 'w_q_b': out['w_q_b'], 'kv_norm_g': out['kv_norm_g'], 'w_kv_b': out['w_kv_b'], 'w_o_attn': out['w_o_attn'], 'ssd_conv_w': out['ssd_conv_w'], 'ssd_conv_b': out['ssd_conv_b'], 'dt_bias': out['dt_bias'], 'a_log': out['a_log'], 'd_skip': out['d_skip'], 'ssd_norm_g': out['ssd_norm_g'], 'w_o_ssd': out['w_o_ssd'], 'w_out': out['w_out'], 'ln1_g': out['ln1_g'], 'ln1_b': out['ln1_b'], 'w_up': out['w_up'], 'ffn_conv_w': out['ffn_conv_w'], 'ffn_conv_b': out['ffn_conv_b'], 'w_down': out['w_down'], 'ln2_g': out['ln2_g'], 'ln2_b': out['ln2_b'], 'loss_target': out['loss_target'], 'm_meta_tokens': out['m_meta_tokens'], 'm_emb_ln_g': out['m_emb_ln_g'], 'm_emb_ln_b': out['m_emb_ln_b'], 'm_w_in': out['m_w_in'], 'm_q_norm_g': out['m_q_norm_g'], 'm_w_q_b': out['m_w_q_b'], 'm_kv_norm_g': out['m_kv_norm_g'], 'm_w_kv_b': out['m_w_kv_b'], 'm_w_o_attn': out['m_w_o_attn'], 'm_ssd_conv_w': out['m_ssd_conv_w'], 'm_ssd_conv_b': out['m_ssd_conv_b'], 'm_dt_bias': out['m_dt_bias'], 'm_a_log': out['m_a_log'], 'm_d_skip': out['m_d_skip'], 'm_ssd_norm_g': out['m_ssd_norm_g'], 'm_w_o_ssd': out['m_w_o_ssd'], 'm_w_out': out['m_w_out'], 'm_ln1_g': out['m_ln1_g'], 'm_ln1_b': out['m_ln1_b'], 'm_w_up': out['m_w_up'], 'm_ffn_conv_w': out['m_ffn_conv_w'], 'm_ffn_conv_b': out['m_ffn_conv_b'], 'm_w_down': out['m_w_down'], 'm_ln2_g': out['m_ln2_g'], 'm_ln2_b': out['m_ln2_b'], 'v_meta_tokens': out['v_meta_tokens'], 'v_emb_ln_g': out['v_emb_ln_g'], 'v_emb_ln_b': out['v_emb_ln_b'], 'v_w_in': out['v_w_in'], 'v_q_norm_g': out['v_q_norm_g'], 'v_w_q_b': out['v_w_q_b'], 'v_kv_norm_g': out['v_kv_norm_g'], 'v_w_kv_b': out['v_w_kv_b'], 'v_w_o_attn': out['v_w_o_attn'], 'v_ssd_conv_w': out['v_ssd_conv_w'], 'v_ssd_conv_b': out['v_ssd_conv_b'], 'v_dt_bias': out['v_dt_bias'], 'v_a_log': out['v_a_log'], 'v_d_skip': out['v_d_skip'], 'v_ssd_norm_g': out['v_ssd_norm_g'], 'v_w_o_ssd': out['v_w_o_ssd'], 'v_w_out': out['v_w_out'], 'v_ln1_g': out['v_ln1_g'], 'v_ln1_b': out['v_ln1_b'], 'v_w_up': out['v_w_up'], 'v_ffn_conv_w': out['v_ffn_conv_w'], 'v_ffn_conv_b': out['v_ffn_conv_b'], 'v_w_down': out['v_w_down'], 'v_ln2_g': out['v_ln2_g'], 'v_ln2_b': out['v_ln2_b']}


def _loss(weights, diff, rest, loss_target):
    with _jax.named_scope("forward"):
        args = {**rest, TWIN_DIFF_INPUT: diff, **{k: w.astype(_WEIGHT_DTYPES[k]) for k, w in weights.items()}}
        y = _forward(args)
    with _jax.named_scope("loss_head"):
        err = _jnp.square(y.astype(_jnp.float32) - loss_target)
        return 0.5 * _jnp.sum(_jnp.mean(err, axis=-1)) if err.ndim else 0.5 * err


def _adamw(w, g, m, v):
    m = ADAM_B1 * m + (1.0 - ADAM_B1) * g
    v = ADAM_B2 * v + (1.0 - ADAM_B2) * _jnp.square(g)
    m_hat = m / (1.0 - ADAM_B1 ** ADAM_STEP)
    v_hat = v / (1.0 - ADAM_B2 ** ADAM_STEP)
    delta = -ADAM_LR * (m_hat / (_jnp.sqrt(v_hat) + ADAM_EPS) + ADAM_WD * w)
    return delta, m, v


def reference(x, meta_tokens, emb_ln_g, emb_ln_b, w_in, q_norm_g, w_q_b, kv_norm_g, w_kv_b, w_o_attn, ssd_conv_w, ssd_conv_b, dt_bias, a_log, d_skip, ssd_norm_g, w_o_ssd, w_out, ln1_g, ln1_b, w_up, ffn_conv_w, ffn_conv_b, w_down, ln2_g, ln2_b, loss_target, m_meta_tokens, m_emb_ln_g, m_emb_ln_b, m_w_in, m_q_norm_g, m_w_q_b, m_kv_norm_g, m_w_kv_b, m_w_o_attn, m_ssd_conv_w, m_ssd_conv_b, m_dt_bias, m_a_log, m_d_skip, m_ssd_norm_g, m_w_o_ssd, m_w_out, m_ln1_g, m_ln1_b, m_w_up, m_ffn_conv_w, m_ffn_conv_b, m_w_down, m_ln2_g, m_ln2_b, v_meta_tokens, v_emb_ln_g, v_emb_ln_b, v_w_in, v_q_norm_g, v_w_q_b, v_kv_norm_g, v_w_kv_b, v_w_o_attn, v_ssd_conv_w, v_ssd_conv_b, v_dt_bias, v_a_log, v_d_skip, v_ssd_norm_g, v_w_o_ssd, v_w_out, v_ln1_g, v_ln1_b, v_w_up, v_ffn_conv_w, v_ffn_conv_b, v_w_down, v_ln2_g, v_ln2_b):
    given = dict(x=x, meta_tokens=meta_tokens, emb_ln_g=emb_ln_g, emb_ln_b=emb_ln_b, w_in=w_in, q_norm_g=q_norm_g, w_q_b=w_q_b, kv_norm_g=kv_norm_g, w_kv_b=w_kv_b, w_o_attn=w_o_attn, ssd_conv_w=ssd_conv_w, ssd_conv_b=ssd_conv_b, dt_bias=dt_bias, a_log=a_log, d_skip=d_skip, ssd_norm_g=ssd_norm_g, w_o_ssd=w_o_ssd, w_out=w_out, ln1_g=ln1_g, ln1_b=ln1_b, w_up=w_up, ffn_conv_w=ffn_conv_w, ffn_conv_b=ffn_conv_b, w_down=w_down, ln2_g=ln2_g, ln2_b=ln2_b, loss_target=loss_target, m_meta_tokens=m_meta_tokens, m_emb_ln_g=m_emb_ln_g, m_emb_ln_b=m_emb_ln_b, m_w_in=m_w_in, m_q_norm_g=m_q_norm_g, m_w_q_b=m_w_q_b, m_kv_norm_g=m_kv_norm_g, m_w_kv_b=m_w_kv_b, m_w_o_attn=m_w_o_attn, m_ssd_conv_w=m_ssd_conv_w, m_ssd_conv_b=m_ssd_conv_b, m_dt_bias=m_dt_bias, m_a_log=m_a_log, m_d_skip=m_d_skip, m_ssd_norm_g=m_ssd_norm_g, m_w_o_ssd=m_w_o_ssd, m_w_out=m_w_out, m_ln1_g=m_ln1_g, m_ln1_b=m_ln1_b, m_w_up=m_w_up, m_ffn_conv_w=m_ffn_conv_w, m_ffn_conv_b=m_ffn_conv_b, m_w_down=m_w_down, m_ln2_g=m_ln2_g, m_ln2_b=m_ln2_b, v_meta_tokens=v_meta_tokens, v_emb_ln_g=v_emb_ln_g, v_emb_ln_b=v_emb_ln_b, v_w_in=v_w_in, v_q_norm_g=v_q_norm_g, v_w_q_b=v_w_q_b, v_kv_norm_g=v_kv_norm_g, v_w_kv_b=v_w_kv_b, v_w_o_attn=v_w_o_attn, v_ssd_conv_w=v_ssd_conv_w, v_ssd_conv_b=v_ssd_conv_b, v_dt_bias=v_dt_bias, v_a_log=v_a_log, v_d_skip=v_d_skip, v_ssd_norm_g=v_ssd_norm_g, v_w_o_ssd=v_w_o_ssd, v_w_out=v_w_out, v_ln1_g=v_ln1_g, v_ln1_b=v_ln1_b, v_w_up=v_w_up, v_ffn_conv_w=v_ffn_conv_w, v_ffn_conv_b=v_ffn_conv_b, v_w_down=v_w_down, v_ln2_g=v_ln2_g, v_ln2_b=v_ln2_b)
    weights = {n: given[n] for n in TWIN_WEIGHTS}
    shared = {n: given[n] for n in SHARED_INPUTS}
    per_example = {n: given[n] for n in ['x']}
    grad_fn = _jax.value_and_grad(_loss, argnums=(0, 1))

    def one_microbatch(ex, loss_target):
        ex = dict(ex)
        diff = ex.pop(TWIN_DIFF_INPUT)
        return grad_fn(weights, diff, {**shared, **ex}, loss_target)

    if N_MICROBATCH == 1:
        loss, (grad_w, grad_x) = one_microbatch(per_example, given["loss_target"])
    else:
        def body(carry, xs):
            loss_sum, grad_sum = carry
            l_k, (gw_k, gx_k) = one_microbatch(xs[0], xs[1])
            with _jax.named_scope("update"):
                return (loss_sum + l_k, _jax.tree.map(_jnp.add, grad_sum, gw_k)), gx_k

        init = (_jnp.zeros((), _jnp.float32), _jax.tree.map(_jnp.zeros_like, weights))
        (loss, grad_w), grad_x = _jax.lax.scan(body, init, (per_example, given["loss_target"]))
    with _jax.named_scope("update"):
        delta_w, new_m, new_v = {}, {}, {}
        for n in TWIN_WEIGHTS:
            delta_w[n], new_m[n], new_v[n] = _adamw(weights[n], grad_w[n], given["m_" + n], given["v_" + n])
    return (loss, grad_x, *[grad_w[n] for n in TWIN_WEIGHTS], *[delta_w[n] for n in TWIN_WEIGHTS],
            *[new_m[n] for n in TWIN_WEIGHTS], *[new_v[n] for n in TWIN_WEIGHTS])
```

```python
import functools
import math

import numpy as np
import jax
import jax.numpy as jnp
from jax import lax
from jax.experimental import pallas as pl
from jax.experimental.pallas import tpu as pltpu

F32 = jnp.float32
BF16 = jnp.bfloat16

D_MODEL = 1024
N_META = 16
DEPTH = 2
MLA_HEADS = 8
Q_LORA = 768
KV_LORA = 256
QK_NOPE = 128
QK_ROPE = 64
V_HEAD = 128
ROPE_THETA = 10000.0
NEG_INF = -1e30
SSD_INNER = 2048
SSD_HEAD_DIM = 64
SSD_HEADS = 32
SSD_GROUPS = 4
SSD_STATE = 128
SSD_CONV = 4
SSD_CONV_DIM = 3072
CHUNK = 128
D_FF = 2816
FFN_CONV = 3
LN_EPS = 1e-5
RMS_EPS = 1e-6
ALPHA = (2 * DEPTH) ** 0.25
ATTN_SCALE = (QK_NOPE + QK_ROPE) ** -0.5
ADAM_LR = 0.001
ADAM_B1 = 0.9
ADAM_B2 = 0.999
ADAM_EPS = 1e-08
ADAM_WD = 0.01
ADAM_STEP = 10

LANES = 128
PAD = 112
ROW0 = PAD + N_META
QHEAD = 256
GROUP_W = SSD_INNER // SSD_GROUPS
HALO = 8
VMEM_LIMIT_BYTES = 56 * 1024 * 1024
N_CHIPS = 4

OFF_Q, OFF_KV, OFF_Z, OFF_XBC, OFF_GA, OFF_GS, OFF_KPE, OFF_DT = 0, 768, 1024, 3072, 6144, 7168, 8192, 8320
IN_COLS_P = 8448

NT_DIMS = (((1,), (1,)), ((), ()))
NN_DIMS = (((1,), (0,)), ((), ()))
TN_DIMS = (((0,), (0,)), ((), ()))

SHARDED = (("meta_tokens", "col"), ("w_in", "col"), ("w_q_b", "col"), ("w_kv_b", "col"), ("w_o_attn", "row"),
           ("ssd_conv_w", "col"), ("w_o_ssd", "row"), ("w_out", "row"), ("w_up", "col"), ("ffn_conv_w", "col"),
           ("w_down", "row"))
REPLICATED = ("emb_ln_g", "emb_ln_b", "q_norm_g", "kv_norm_g", "ssd_conv_b", "dt_bias", "a_log", "d_skip",
              "ssd_norm_g", "ln1_g", "ln1_b", "ffn_conv_b", "ln2_g", "ln2_b")
WEIGHTS = ("meta_tokens", "emb_ln_g", "emb_ln_b", "w_in", "q_norm_g", "w_q_b", "kv_norm_g", "w_kv_b", "w_o_attn",
           "ssd_conv_w", "ssd_conv_b", "dt_bias", "a_log", "d_skip", "ssd_norm_g", "w_o_ssd", "w_out", "ln1_g",
           "ln1_b", "w_up", "ffn_conv_w", "ffn_conv_b", "w_down", "ln2_g", "ln2_b")
GATHER_BF16 = ("w_in", "w_q_b", "w_kv_b", "w_o_attn", "w_o_ssd", "w_out", "w_up", "w_down")
GATHER_F32 = ("meta_tokens", "ssd_conv_w", "ffn_conv_w")
REDUCE_ROW_ALIGN = 2048


def _tile(n, target, base=LANES):
    best = None
    d = base
    while d <= min(n, target):
        if n % d == 0:
            best = d
        d += base
    return n if best is None else best


def _cp(*sem):
    return pltpu.CompilerParams(dimension_semantics=sem, vmem_limit_bytes=VMEM_LIMIT_BYTES)


def _sds(shape, dtype):
    return jax.ShapeDtypeStruct(shape, dtype)


def _row_ids(i, tr, shape):
    return i * tr + lax.broadcasted_iota(jnp.int32, shape, 0)


def _sigmoid(x):
    return 1.0 / (1.0 + jnp.exp(-x))


def _mm(a, b, *, name, trans_b=False, out_dtype=F32, add=None, add_scale=1.0, tm=640, tn=1024, tk=1408):
    m, k_dim = a.shape
    n = b.shape[0] if trans_b else b.shape[1]
    tm, tn, tk = _tile(m, tm), _tile(n, tn), _tile(k_dim, tk)
    nk = k_dim // tk
    has_add = add is not None
    dims = NT_DIMS if trans_b else NN_DIMS

    def body(*refs):
        a_ref, b_ref = refs[0], refs[1]
        r_ref = refs[2] if has_add else None
        o_ref = refs[3] if has_add else refs[2]
        part = lax.dot_general(a_ref[...].astype(BF16), b_ref[...].astype(BF16), dims, preferred_element_type=F32)

        def finish(r):
            if has_add:
                r = r + add_scale * r_ref[...].astype(F32)
            o_ref[...] = r.astype(out_dtype)

        if nk == 1:
            finish(part)
        else:
            acc = refs[-1]
            kk = pl.program_id(2)

            @pl.when(kk == 0)
            def _():
                acc[...] = part

            @pl.when(kk > 0)
            def _():
                acc[...] += part

            @pl.when(kk == nk - 1)
            def _():
                finish(acc[...])

    in_specs = [pl.BlockSpec((tm, tk), lambda i, j, kk: (i, kk)),
                pl.BlockSpec((tn, tk), lambda i, j, kk: (j, kk)) if trans_b
                else pl.BlockSpec((tk, tn), lambda i, j, kk: (kk, j))]
    args = [a, b]
    if has_add:
        in_specs.append(pl.BlockSpec((tm, tn), lambda i, j, kk: (i, j)))
        args.append(add)
    return pl.pallas_call(
        body, name=name, grid=(m // tm, n // tn, nk), in_specs=in_specs,
        out_specs=pl.BlockSpec((tm, tn), lambda i, j, kk: (i, j)),
        out_shape=_sds((m, n), out_dtype),
        scratch_shapes=[pltpu.VMEM((tm, tn), F32)] if nk > 1 else [],
        compiler_params=_cp("parallel", "parallel", "arbitrary"),
    )(*args)


def _mm_tn(a, b, *, name, tko=512, tn=1024, tt=640):
    t, k_dim = a.shape
    n = b.shape[1]
    tko, tn, tt = _tile(k_dim, tko), _tile(n, tn), _tile(t, tt)

    def body(a_ref, b_ref, o_ref):
        part = lax.dot_general(a_ref[...].astype(BF16), b_ref[...].astype(BF16), TN_DIMS, preferred_element_type=F32)
        tt_i = pl.program_id(2)

        @pl.when(tt_i == 0)
        def _():
            o_ref[...] = part

        @pl.when(tt_i > 0)
        def _():
            o_ref[...] += part

    return pl.pallas_call(
        body, name=name, grid=(k_dim // tko, n // tn, t // tt),
        in_specs=[pl.BlockSpec((tt, tko), lambda i, j, s: (s, i)), pl.BlockSpec((tt, tn), lambda i, j, s: (s, j))],
        out_specs=pl.BlockSpec((tko, tn), lambda i, j, s: (i, j)),
        out_shape=_sds((k_dim, n), F32),
        compiler_params=_cp("parallel", "parallel", "arbitrary"),
    )(a, b)


def _ln_fwd(h, branch, g, b, *, name):
    t, d = h.shape
    tr = _tile(t, 640)
    has_branch = branch is not None

    def body(*refs):
        if has_branch:
            h_ref, br_ref, g_ref, b_ref, pre_ref, o_ref = refs
            pre = ALPHA * h_ref[...] + br_ref[...]
            pre_ref[...] = pre
        else:
            h_ref, g_ref, b_ref, o_ref = refs
            pre = h_ref[...]
        mu = jnp.mean(pre, axis=1, keepdims=True)
        xc = pre - mu
        var = jnp.mean(xc * xc, axis=1, keepdims=True)
        y = xc * lax.rsqrt(var + LN_EPS) * g_ref[...] + b_ref[...]
        rows = _row_ids(pl.program_id(0), tr, (tr, 1))
        o_ref[...] = jnp.where(rows >= PAD, y, 0.0)

    row_spec = pl.BlockSpec((tr, d), lambda i: (i, 0))
    vec_spec = pl.BlockSpec((1, d), lambda i: (0, 0))
    if has_branch:
        return pl.pallas_call(
            body, name=name, grid=(t // tr,), in_specs=[row_spec, row_spec, vec_spec, vec_spec],
            out_specs=[row_spec, row_spec], out_shape=[_sds((t, d), F32), _sds((t, d), F32)],
            compiler_params=_cp("parallel"))(h, branch, g, b)
    out = pl.pallas_call(
        body, name=name, grid=(t // tr,), in_specs=[row_spec, vec_spec, vec_spec],
        out_specs=row_spec, out_shape=_sds((t, d), F32), compiler_params=_cp("parallel"))(h, g, b)
    return h, out


def _ln_bwd(dy, pre, g, *, name):
    t, d = pre.shape
    tr = _tile(t, 640)

    def body(dy_ref, pre_ref, g_ref, dpre_ref, dg_ref, db_ref):
        i = pl.program_id(0)
        pre_v = pre_ref[...]
        mu = jnp.mean(pre_v, axis=1, keepdims=True)
        xc = pre_v - mu
        var = jnp.mean(xc * xc, axis=1, keepdims=True)
        rstd = lax.rsqrt(var + LN_EPS)
        xhat = xc * rstd
        rows = _row_ids(i, tr, (tr, 1))
        dym = jnp.where(rows >= PAD, dy_ref[...], 0.0)
        gdy = dym * g_ref[...]
        m1 = jnp.mean(gdy, axis=1, keepdims=True)
        m2 = jnp.mean(gdy * xhat, axis=1, keepdims=True)
        dpre_ref[...] = rstd * (gdy - m1 - xhat * m2)
        pg = jnp.sum(dym * xhat, axis=0, keepdims=True)
        pb = jnp.sum(dym, axis=0, keepdims=True)

        @pl.when(i == 0)
        def _():
            dg_ref[...] = pg
            db_ref[...] = pb

        @pl.when(i > 0)
        def _():
            dg_ref[...] += pg
            db_ref[...] += pb

    row_spec = pl.BlockSpec((tr, d), lambda i: (i, 0))
    vec_spec = pl.BlockSpec((1, d), lambda i: (0, 0))
    return pl.pallas_call(
        body, name=name, grid=(t // tr,), in_specs=[row_spec, row_spec, vec_spec],
        out_specs=[row_spec, vec_spec, vec_spec],
        out_shape=[_sds((t, d), F32), _sds((1, d), F32), _sds((1, d), F32)],
        compiler_params=_cp("arbitrary"))(dy, pre, g)


def _rms_fwd(proj, col_off, width, g, *, name):
    t = proj.shape[0]
    tr = _tile(t, 640)
    cb = col_off // width

    def body(x_ref, g_ref, o_ref):
        x = x_ref[...]
        r = lax.rsqrt(jnp.mean(x * x, axis=1, keepdims=True) + RMS_EPS)
        o_ref[...] = (x * r * g_ref[...]).astype(BF16)

    return pl.pallas_call(
        body, name=name, grid=(t // tr,),
        in_specs=[pl.BlockSpec((tr, width), lambda i: (i, cb)), pl.BlockSpec((1, width), lambda i: (0, 0))],
        out_specs=pl.BlockSpec((tr, width), lambda i: (i, 0)), out_shape=_sds((t, width), BF16),
        compiler_params=_cp("parallel"))(proj, g)


def _rms_bwd(dy, proj, col_off, width, g, *, name):
    t = proj.shape[0]
    tr = _tile(t, 640)
    cb = col_off // width

    def body(dy_ref, x_ref, g_ref, dx_ref, dg_ref):
        i = pl.program_id(0)
        x = x_ref[...]
        dyv = dy_ref[...].astype(F32)
        r = lax.rsqrt(jnp.mean(x * x, axis=1, keepdims=True) + RMS_EPS)
        gdy = dyv * g_ref[...]
        m = jnp.mean(x * gdy, axis=1, keepdims=True)
        dx_ref[...] = (r * gdy - x * (r * r * r) * m).astype(BF16)
        pg = jnp.sum(dyv * x * r, axis=0, keepdims=True)

        @pl.when(i == 0)
        def _():
            dg_ref[...] = pg

        @pl.when(i > 0)
        def _():
            dg_ref[...] += pg

    return pl.pallas_call(
        body, name=name, grid=(t // tr,),
        in_specs=[pl.BlockSpec((tr, width), lambda i: (i, 0)), pl.BlockSpec((tr, width), lambda i: (i, cb)),
                  pl.BlockSpec((1, width), lambda i: (0, 0))],
        out_specs=[pl.BlockSpec((tr, width), lambda i: (i, 0)), pl.BlockSpec((1, width), lambda i: (0, 0))],
        out_shape=[_sds((t, width), BF16), _sds((1, width), F32)],
        compiler_params=_cp("arbitrary"))(dy, proj, g)


def _rope_apply(r, cos, sin_a, sin_b):
    return r * cos + pltpu.roll(r, 96, 1) * sin_a + pltpu.roll(r, 32, 1) * sin_b


def _rope_apply_t(dr, cos, sin_a, sin_b):
    return dr * cos + pltpu.roll(dr * sin_a, 32, 1) + pltpu.roll(dr * sin_b, 96, 1)


def _rope_q_fwd(q, cos, sin_a, sin_b, *, name):
    t, w = q.shape
    tr = _tile(t, 128)

    def body(q_ref, c_ref, sa_ref, sb_ref, o_ref):
        c, sa, sb = c_ref[...], sa_ref[...], sb_ref[...]
        for h in range(MLA_HEADS):
            base = h * QHEAD
            o_ref[:, base:base + LANES] = (q_ref[:, base:base + LANES] * ATTN_SCALE).astype(BF16)
            rot = _rope_apply(q_ref[:, base + LANES:base + QHEAD], c, sa, sb)
            o_ref[:, base + LANES:base + QHEAD] = (rot * ATTN_SCALE).astype(BF16)

    tab = pl.BlockSpec((tr, LANES), lambda i: (i, 0))
    row = pl.BlockSpec((tr, w), lambda i: (i, 0))
    return pl.pallas_call(body, name=name, grid=(t // tr,), in_specs=[row, tab, tab, tab], out_specs=row,
                          out_shape=_sds((t, w), BF16), compiler_params=_cp("parallel"))(q, cos, sin_a, sin_b)


def _rope_q_bwd(dq, cos, sin_a, sin_b, *, name):
    t, w = dq.shape
    tr = _tile(t, 128)

    def body(dq_ref, c_ref, sa_ref, sb_ref, o_ref):
        c, sa, sb = c_ref[...], sa_ref[...], sb_ref[...]
        for h in range(MLA_HEADS):
            base = h * QHEAD
            o_ref[:, base:base + LANES] = (dq_ref[:, base:base + LANES] * ATTN_SCALE).astype(BF16)
            d_rot = _rope_apply_t(dq_ref[:, base + LANES:base + QHEAD], c, sa, sb)
            o_ref[:, base + LANES:base + QHEAD] = (d_rot * ATTN_SCALE).astype(BF16)

    tab = pl.BlockSpec((tr, LANES), lambda i: (i, 0))
    row = pl.BlockSpec((tr, w), lambda i: (i, 0))
    return pl.pallas_call(body, name=name, grid=(t // tr,), in_specs=[row, tab, tab, tab], out_specs=row,
                          out_shape=_sds((t, w), BF16), compiler_params=_cp("parallel"))(dq, cos, sin_a, sin_b)


def _rope_k_fwd(proj, cos, sin_a, sin_b, *, name):
    t = proj.shape[0]
    tr = _tile(t, 640)
    cb = OFF_KPE // LANES

    def body(x_ref, c_ref, sa_ref, sb_ref, o_ref):
        o_ref[...] = _rope_apply(x_ref[...], c_ref[...], sa_ref[...], sb_ref[...]).astype(BF16)

    tab = pl.BlockSpec((tr, LANES), lambda i: (i, 0))
    return pl.pallas_call(body, name=name, grid=(t // tr,),
                          in_specs=[pl.BlockSpec((tr, LANES), lambda i: (i, cb)), tab, tab, tab], out_specs=tab,
                          out_shape=_sds((t, LANES), BF16), compiler_params=_cp("parallel"))(proj, cos, sin_a, sin_b)


def _rope_k_bwd(dkp, cos, sin_a, sin_b, *, name):
    nh, t, _ = dkp.shape
    tr = _tile(t, 640)

    def body(d_ref, c_ref, sa_ref, sb_ref, o_ref):
        tot = d_ref[0]
        for h in range(1, nh):
            tot = tot + d_ref[h]
        o_ref[...] = _rope_apply_t(tot, c_ref[...], sa_ref[...], sb_ref[...]).astype(BF16)

    tab = pl.BlockSpec((tr, LANES), lambda i: (i, 0))
    return pl.pallas_call(body, name=name, grid=(t // tr,),
                          in_specs=[pl.BlockSpec((nh, tr, LANES), lambda i: (0, i, 0)), tab, tab, tab], out_specs=tab,
                          out_shape=_sds((t, LANES), BF16), compiler_params=_cp("parallel"))(dkp, cos, sin_a, sin_b)


def _visible(i, j, tb):
    row = i * tb + lax.broadcasted_iota(jnp.int32, (tb, tb), 0)
    col = j * tb + lax.broadcasted_iota(jnp.int32, (tb, tb), 1)
    return (col <= row) & (col >= PAD)


def _flash_fwd(q, kv, kpe, *, name):
    t = q.shape[0]
    nh = MLA_HEADS
    tb = _tile(t, 640)
    nb = t // tb

    def body(q_ref, kn_ref, v_ref, kp_ref, o_ref, lse_ref, m_s, l_s, acc_s):
        i, j = pl.program_id(1), pl.program_id(2)

        @pl.when(j == 0)
        def _():
            m_s[...] = jnp.full((tb, 1), NEG_INF, F32)
            l_s[...] = jnp.zeros((tb, 1), F32)
            acc_s[...] = jnp.zeros((tb, V_HEAD), F32)

        @pl.when(j <= i)
        def _():
            k = jnp.concatenate([kn_ref[...], kp_ref[...]], axis=1)
            s = lax.dot_general(q_ref[...], k, NT_DIMS, preferred_element_type=F32)
            s = jnp.where(_visible(i, j, tb), s, NEG_INF)
            m_prev = m_s[...]
            m_new = jnp.maximum(m_prev, jnp.max(s, axis=1, keepdims=True))
            p = jnp.exp(s - m_new)
            corr = jnp.exp(m_prev - m_new)
            l_s[...] = corr * l_s[...] + jnp.sum(p, axis=1, keepdims=True)
            acc_s[...] = corr * acc_s[...] + jnp.dot(p.astype(BF16), v_ref[...], preferred_element_type=F32)
            m_s[...] = m_new

        @pl.when(j == i)
        def _():
            l = l_s[...]
            o_ref[...] = (acc_s[...] / l).astype(BF16)
            lse_ref[0] = m_s[...] + jnp.log(l)

    return pl.pallas_call(
        body, name=name, grid=(nh, nb, nb),
        in_specs=[pl.BlockSpec((tb, QHEAD), lambda h, i, j: (i, h)),
                  pl.BlockSpec((tb, LANES), lambda h, i, j: (jnp.minimum(j, i), h)),
                  pl.BlockSpec((tb, LANES), lambda h, i, j: (jnp.minimum(j, i), nh + h)),
                  pl.BlockSpec((tb, LANES), lambda h, i, j: (jnp.minimum(j, i), 0))],
        out_specs=[pl.BlockSpec((tb, V_HEAD), lambda h, i, j: (i, h)),
                   pl.BlockSpec((1, tb, 1), lambda h, i, j: (h, i, 0))],
        out_shape=[_sds((t, nh * V_HEAD), BF16), _sds((nh, t, 1), F32)],
        scratch_shapes=[pltpu.VMEM((tb, 1), F32), pltpu.VMEM((tb, 1), F32), pltpu.VMEM((tb, V_HEAD), F32)],
        compiler_params=_cp("parallel", "parallel", "arbitrary"))(q, kv, kv, kpe)


def _attn_delta(do, o, *, name):
    t = o.shape[0]
    nh = MLA_HEADS
    tr = _tile(t, 640)

    def body(do_ref, o_ref, d_ref):
        d_ref[0] = jnp.sum(do_ref[...].astype(F32) * o_ref[...].astype(F32), axis=1, keepdims=True)

    blk = pl.BlockSpec((tr, V_HEAD), lambda h, i: (i, h))
    return pl.pallas_call(body, name=name, grid=(nh, t // tr), in_specs=[blk, blk],
                          out_specs=pl.BlockSpec((1, tr, 1), lambda h, i: (h, i, 0)),
                          out_shape=_sds((nh, t, 1), F32), compiler_params=_cp("parallel", "parallel"))(do, o)


def _flash_bwd(q, kv, kpe, do, lse, delta, *, name):
    t = q.shape[0]
    nh = MLA_HEADS
    tb = _tile(t, 640)
    nb = t // tb

    def body(q_ref, kn_ref, v_ref, kp_ref, do_ref, lse_ref, dl_ref, dq_ref, dkn_ref, dkp_ref, dv_ref, dk_s, dv_s):
        j, i = pl.program_id(1), pl.program_id(2)

        @pl.when((j == 0) & (i == 0))
        def _():
            dq_ref[...] = jnp.zeros((t, QHEAD), F32)

        @pl.when(i == 0)
        def _():
            dk_s[...] = jnp.zeros((tb, QHEAD), F32)
            dv_s[...] = jnp.zeros((tb, V_HEAD), F32)

        @pl.when(i >= j)
        def _():
            qv = q_ref[...]
            k = jnp.concatenate([kn_ref[...], kp_ref[...]], axis=1)
            s = lax.dot_general(qv, k, NT_DIMS, preferred_element_type=F32)
            s = jnp.where(_visible(i, j, tb), s, NEG_INF)
            p = jnp.exp(s - lse_ref[0])
            dov = do_ref[...]
            dv_s[...] += lax.dot_general(p.astype(BF16), dov, TN_DIMS, preferred_element_type=F32)
            dp = lax.dot_general(dov, v_ref[...], NT_DIMS, preferred_element_type=F32)
            ds = (p * (dp - dl_ref[0])).astype(BF16)
            dk_s[...] += lax.dot_general(ds, qv, TN_DIMS, preferred_element_type=F32)
            r0 = pl.multiple_of(i * tb, tb)
            dq_ref[pl.ds(r0, tb), :] += jnp.dot(ds, k, preferred_element_type=F32)

        @pl.when(i == nb - 1)
        def _():
            dkn_ref[...] = dk_s[:, :LANES].astype(BF16)
            dkp_ref[0] = dk_s[:, LANES:]
            dv_ref[...] = dv_s[...].astype(BF16)

    qi = lambda h, j, i: (jnp.maximum(i, j), h)
    return pl.pallas_call(
        body, name=name, grid=(nh, nb, nb),
        in_specs=[pl.BlockSpec((tb, QHEAD), qi),
                  pl.BlockSpec((tb, LANES), lambda h, j, i: (j, h)),
                  pl.BlockSpec((tb, LANES), lambda h, j, i: (j, nh + h)),
                  pl.BlockSpec((tb, LANES), lambda h, j, i: (j, 0)),
                  pl.BlockSpec((tb, V_HEAD), qi),
                  pl.BlockSpec((1, tb, 1), lambda h, j, i: (h, jnp.maximum(i, j), 0)),
                  pl.BlockSpec((1, tb, 1), lambda h, j, i: (h, jnp.maximum(i, j), 0))],
        out_specs=[pl.BlockSpec((t, QHEAD), lambda h, j, i: (0, h)),
                   pl.BlockSpec((tb, LANES), lambda h, j, i: (j, h)),
                   pl.BlockSpec((1, tb, LANES), lambda h, j, i: (h, j, 0)),
                   pl.BlockSpec((tb, V_HEAD), lambda h, j, i: (j, h))],
        out_shape=[_sds((t, nh * QHEAD), F32), _sds((t, nh * LANES), BF16), _sds((nh, t, LANES), F32),
                   _sds((t, nh * V_HEAD), BF16)],
        scratch_shapes=[pltpu.VMEM((tb, QHEAD), F32), pltpu.VMEM((tb, V_HEAD), F32)],
        compiler_params=_cp("arbitrary", "arbitrary", "arbitrary"))(q, kv, kv, kpe, do, lse, delta)


def _fill_prev(buf, x_ref, halo_ref, i, tr):
    buf[pl.ds(0, HALO), :] = jnp.where(i > 0, halo_ref[...], 0.0)
    buf[pl.ds(HALO, tr), :] = x_ref[...]


def _conv_prev(buf, w_ref, kw, tr):
    acc = w_ref[kw - 1:kw, :] * buf[pl.ds(HALO, tr), :]
    for k in range(kw - 1):
        acc = acc + w_ref[k:k + 1, :] * buf[pl.ds(HALO - kw + 1 + k, tr), :]
    return acc


def _conv_dw(buf, dc, kw, tr):
    rows = [jnp.sum(dc * buf[pl.ds(HALO - kw + 1 + k, tr), :], axis=0, keepdims=True) for k in range(kw)]
    return jnp.concatenate(rows, axis=0)


def _conv_next(buf, dc_ref, halo_ref, w_ref, kw, i, n_tiles, tr):
    buf[pl.ds(0, tr), :] = dc_ref[...]
    buf[pl.ds(tr, HALO), :] = jnp.where(i < n_tiles - 1, halo_ref[...], 0.0)
    acc = w_ref[kw - 1:kw, :] * buf[pl.ds(0, tr), :]
    for k in range(kw - 1):
        acc = acc + w_ref[k:k + 1, :] * buf[pl.ds(kw - 1 - k, tr), :]
    return acc


def _split3(x):
    x1 = x.astype(BF16)
    r1 = x - x1.astype(F32)
    x2 = r1.astype(BF16)
    x3 = (r1 - x2.astype(F32)).astype(BF16)
    return x1, x2, x3


def _dot3(parts, m, left):
    tot = None
    for p in parts:
        r = jnp.dot(m, p, preferred_element_type=F32) if left else jnp.dot(p, m, preferred_element_type=F32)
        tot = r if tot is None else tot + r
    return tot


def _ssd_prep_fwd(proj, conv_w, conv_b, dt_bias, expand, *, name):
    t = proj.shape[0]
    tr = _tile(t, 128)
    nt = t // tr
    hb = tr // HALO
    cw = SSD_CONV_DIM
    cb_x = OFF_XBC // cw
    cb_dt = OFF_DT // LANES

    def body(x_ref, halo_ref, dtr_ref, w_ref, b_ref, dtb_ref, e_ref, xs_ref, bm_ref, cm_ref, dtx_ref, buf):
        i = pl.program_id(0)
        _fill_prev(buf, x_ref, halo_ref, i, tr)
        conv = _conv_prev(buf, w_ref, SSD_CONV, tr) + b_ref[...]
        rows = _row_ids(i, tr, (tr, 1))
        live = rows >= PAD
        act = jnp.where(live, conv * _sigmoid(conv), 0.0)
        xs_ref[...] = act[:, :SSD_INNER]
        bm_ref[...] = act[:, SSD_INNER:SSD_INNER + GROUP_W]
        cm_ref[...] = act[:, SSD_INNER + GROUP_W:]
        dt = jnp.where(live, jax.nn.softplus(dtr_ref[...] + dtb_ref[...]), 0.0)
        dtx_ref[...] = _dot3(_split3(dt), e_ref[...], left=False)

    return pl.pallas_call(
        body, name=name, grid=(nt,),
        in_specs=[pl.BlockSpec((tr, cw), lambda i: (i, cb_x)),
                  pl.BlockSpec((HALO, cw), lambda i: (jnp.maximum(i * hb - 1, 0), cb_x)),
                  pl.BlockSpec((tr, LANES), lambda i: (i, cb_dt)),
                  pl.BlockSpec((SSD_CONV, cw), lambda i: (0, 0)),
                  pl.BlockSpec((1, cw), lambda i: (0, 0)),
                  pl.BlockSpec((1, LANES), lambda i: (0, 0)),
                  pl.BlockSpec((LANES, SSD_INNER), lambda i: (0, 0))],
        out_specs=[pl.BlockSpec((tr, SSD_INNER), lambda i: (i, 0)), pl.BlockSpec((tr, GROUP_W), lambda i: (i, 0)),
                   pl.BlockSpec((tr, GROUP_W), lambda i: (i, 0)), pl.BlockSpec((tr, SSD_INNER), lambda i: (i, 0))],
        out_shape=[_sds((t, SSD_INNER), F32), _sds((t, GROUP_W), F32), _sds((t, GROUP_W), F32),
                   _sds((t, SSD_INNER), F32)],
        scratch_shapes=[pltpu.VMEM((tr + HALO, cw), F32)],
        compiler_params=_cp("parallel"))(proj, proj, proj, conv_w, conv_b, dt_bias, expand)


def _ssd_prep_bwd_a(proj, dxs, dbm, dcm, ddtx, conv_w, conv_b, dt_bias, reduce_m, *, name):
    t = proj.shape[0]
    tr = _tile(t, 128)
    nt = t // tr
    hb = tr // HALO
    cw = SSD_CONV_DIM
    cb_x = OFF_XBC // cw
    cb_dt = OFF_DT // LANES

    def body(x_ref, halo_ref, dtr_ref, dxs_ref, dbm_ref, dcm_ref, ddtx_ref, w_ref, b_ref, dtb_ref, r_ref,
             dconv_ref, ddtr_ref, dw_ref, db_ref, ddtb_ref, buf):
        i = pl.program_id(0)
        _fill_prev(buf, x_ref, halo_ref, i, tr)
        conv = _conv_prev(buf, w_ref, SSD_CONV, tr) + b_ref[...]
        rows = _row_ids(i, tr, (tr, 1))
        live = rows >= PAD
        sg = _sigmoid(conv)
        dact = jnp.concatenate([dxs_ref[...], dbm_ref[...], dcm_ref[...]], axis=1)
        dconv = jnp.where(live, dact * (sg * (1.0 + conv * (1.0 - sg))), 0.0)
        dconv_ref[...] = dconv
        pw = _conv_dw(buf, dconv, SSD_CONV, tr)
        pb = jnp.sum(dconv, axis=0, keepdims=True)
        ddt = _dot3(_split3(ddtx_ref[...]), r_ref[...], left=False)
        ddtr = jnp.where(live, ddt * _sigmoid(dtr_ref[...] + dtb_ref[...]), 0.0)
        ddtr_ref[...] = ddtr.astype(BF16)
        pdb = jnp.sum(ddtr, axis=0, keepdims=True)

        @pl.when(i == 0)
        def _():
            dw_ref[...] = pw
            db_ref[...] = pb
            ddtb_ref[...] = pdb

        @pl.when(i > 0)
        def _():
            dw_ref[...] += pw
            db_ref[...] += pb
            ddtb_ref[...] += pdb

    return pl.pallas_call(
        body, name=name, grid=(nt,),
        in_specs=[pl.BlockSpec((tr, cw), lambda i: (i, cb_x)),
                  pl.BlockSpec((HALO, cw), lambda i: (jnp.maximum(i * hb - 1, 0), cb_x)),
                  pl.BlockSpec((tr, LANES), lambda i: (i, cb_dt)),
                  pl.BlockSpec((tr, SSD_INNER), lambda i: (i, 0)),
                  pl.BlockSpec((tr, GROUP_W), lambda i: (i, 0)),
                  pl.BlockSpec((tr, GROUP_W), lambda i: (i, 0)),
                  pl.BlockSpec((tr, SSD_INNER), lambda i: (i, 0)),
                  pl.BlockSpec((SSD_CONV, cw), lambda i: (0, 0)),
                  pl.BlockSpec((1, cw), lambda i: (0, 0)),
                  pl.BlockSpec((1, LANES), lambda i: (0, 0)),
                  pl.BlockSpec((SSD_INNER, LANES), lambda i: (0, 0))],
        out_specs=[pl.BlockSpec((tr, cw), lambda i: (i, 0)), pl.BlockSpec((tr, LANES), lambda i: (i, 0)),
                   pl.BlockSpec((SSD_CONV, cw), lambda i: (0, 0)), pl.BlockSpec((1, cw), lambda i: (0, 0)),
                   pl.BlockSpec((1, LANES), lambda i: (0, 0))],
        out_shape=[_sds((t, cw), F32), _sds((t, LANES), BF16), _sds((SSD_CONV, cw), F32), _sds((1, cw), F32),
                   _sds((1, LANES), F32)],
        scratch_shapes=[pltpu.VMEM((tr + HALO, cw), F32)],
        compiler_params=_cp("arbitrary"))(proj, proj, proj, dxs, dbm, dcm, ddtx, conv_w, conv_b, dt_bias, reduce_m)


def _conv_bwd_input(dconv, w, kw, *, name, out_dtype=BF16, tc=None):
    t, c = dconv.shape
    tr = _tile(t, 128)
    nt = t // tr
    hb = tr // HALO
    tc = _tile(c, tc or c)
    last_hb = t // HALO - 1

    def body(dc_ref, halo_ref, w_ref, o_ref, buf):
        i = pl.program_id(0)
        o_ref[...] = _conv_next(buf, dc_ref, halo_ref, w_ref, kw, i, nt, tr).astype(out_dtype)

    return pl.pallas_call(
        body, name=name, grid=(nt, c // tc),
        in_specs=[pl.BlockSpec((tr, tc), lambda i, j: (i, j)),
                  pl.BlockSpec((HALO, tc), lambda i, j: (jnp.minimum((i + 1) * hb, last_hb), j)),
                  pl.BlockSpec((kw, tc), lambda i, j: (0, j))],
        out_specs=pl.BlockSpec((tr, tc), lambda i, j: (i, j)), out_shape=_sds((t, c), out_dtype),
        scratch_shapes=[pltpu.VMEM((tr + HALO, tc), F32)],
        compiler_params=_cp("parallel", "parallel"))(dconv, dconv, w)


def _ffn_act_fwd(ug, uv, wg, wv, bg, bv, *, name):
    t, c = ug.shape
    tr = _tile(t, 128)
    hb = tr // HALO
    tc = _tile(c, 1408)

    def body(ug_ref, hg_ref, uv_ref, hv_ref, wg_ref, wv_ref, bg_ref, bv_ref, o_ref, bufg, bufv):
        i = pl.program_id(0)
        _fill_prev(bufg, ug_ref, hg_ref, i, tr)
        _fill_prev(bufv, uv_ref, hv_ref, i, tr)
        cg = _conv_prev(bufg, wg_ref, FFN_CONV, tr) + bg_ref[...]
        cv = _conv_prev(bufv, wv_ref, FFN_CONV, tr) + bv_ref[...]
        o_ref[...] = (cg * _sigmoid(cg) * cv).astype(BF16)

    blk = pl.BlockSpec((tr, tc), lambda i, j: (i, j))
    halo = pl.BlockSpec((HALO, tc), lambda i, j: (jnp.maximum(i * hb - 1, 0), j))
    wsp = pl.BlockSpec((FFN_CONV, tc), lambda i, j: (0, j))
    bsp = pl.BlockSpec((1, tc), lambda i, j: (0, j))
    return pl.pallas_call(
        body, name=name, grid=(t // tr, c // tc), in_specs=[blk, halo, blk, halo, wsp, wsp, bsp, bsp],
        out_specs=blk, out_shape=_sds((t, c), BF16),
        scratch_shapes=[pltpu.VMEM((tr + HALO, tc), F32), pltpu.VMEM((tr + HALO, tc), F32)],
        compiler_params=_cp("parallel", "parallel"))(ug, ug, uv, uv, wg, wv, bg, bv)


def _ffn_act_bwd(ug, uv, dact, wg, wv, bg, bv, *, name):
    t, c = ug.shape
    tr = _tile(t, 128)
    hb = tr // HALO
    tc = _tile(c, 1408)

    def body(ug_ref, hg_ref, uv_ref, hv_ref, da_ref, wg_ref, wv_ref, bg_ref, bv_ref,
             dcg_ref, dcv_ref, dwg_ref, dwv_ref, dbg_ref, dbv_ref, bufg, bufv):
        i = pl.program_id(1)
        _fill_prev(bufg, ug_ref, hg_ref, i, tr)
        _fill_prev(bufv, uv_ref, hv_ref, i, tr)
        cg = _conv_prev(bufg, wg_ref, FFN_CONV, tr) + bg_ref[...]
        cv = _conv_prev(bufv, wv_ref, FFN_CONV, tr) + bv_ref[...]
        sg = _sigmoid(cg)
        da = da_ref[...]
        dcg = da * cv * (sg * (1.0 + cg * (1.0 - sg)))
        dcv = da * (cg * sg)
        dcg_ref[...] = dcg
        dcv_ref[...] = dcv
        pwg = _conv_dw(bufg, dcg, FFN_CONV, tr)
        pwv = _conv_dw(bufv, dcv, FFN_CONV, tr)
        pbg = jnp.sum(dcg, axis=0, keepdims=True)
        pbv = jnp.sum(dcv, axis=0, keepdims=True)

        @pl.when(i == 0)
        def _():
            dwg_ref[...] = pwg
            dwv_ref[...] = pwv
            dbg_ref[...] = pbg
            dbv_ref[...] = pbv

        @pl.when(i > 0)
        def _():
            dwg_ref[...] += pwg
            dwv_ref[...] += pwv
            dbg_ref[...] += pbg
            dbv_ref[...] += pbv

    blk = pl.BlockSpec((tr, tc), lambda j, i: (i, j))
    halo = pl.BlockSpec((HALO, tc), lambda j, i: (jnp.maximum(i * hb - 1, 0), j))
    wsp = pl.BlockSpec((FFN_CONV, tc), lambda j, i: (0, j))
    bsp = pl.BlockSpec((1, tc), lambda j, i: (0, j))
    return pl.pallas_call(
        body, name=name, grid=(c // tc, t // tr), in_specs=[blk, halo, blk, halo, blk, wsp, wsp, bsp, bsp],
        out_specs=[blk, blk, wsp, wsp, bsp, bsp],
        out_shape=[_sds((t, c), F32), _sds((t, c), F32), _sds((FFN_CONV, c), F32), _sds((FFN_CONV, c), F32),
                   _sds((1, c), F32), _sds((1, c), F32)],
        scratch_shapes=[pltpu.VMEM((tr + HALO, tc), F32), pltpu.VMEM((tr + HALO, tc), F32)],
        compiler_params=_cp("parallel", "arbitrary"))(ug, ug, uv, uv, dact, wg, wv, bg, bv)


def _tri(lower):
    li = lax.broadcasted_iota(jnp.int32, (CHUNK, CHUNK), 0)
    si = lax.broadcasted_iota(jnp.int32, (CHUNK, CHUNK), 1)
    return li >= si if lower else li <= si


def _tri_ones(lower):
    return jnp.where(_tri(lower), 1.0, 0.0).astype(BF16)


def _decay_pair(acs, acs_t, lane0):
    col = acs[:, lane0:lane0 + 1]
    row = acs_t[lane0:lane0 + 1, :]
    low = jnp.where(_tri(True), jnp.exp(jnp.minimum(col - row, 0.0)), 0.0)
    upp = jnp.where(_tri(False), jnp.exp(jnp.minimum(row - col, 0.0)), 0.0)
    return low, upp


def _ssd_fwd(xs, dtx, bm, cm, bm_t, a_x, d_x, *, name):
    t = xs.shape[0]
    nc = t // CHUNK
    gw = GROUP_W

    def body(xs_ref, dt_ref, b_ref, c_ref, bt_ref, a_ref, d_ref, y_ref, prev_ref, h_s):
        @pl.when(pl.program_id(1) == 0)
        def _():
            h_s[...] = jnp.zeros((SSD_STATE, gw), F32)

        x = xs_ref[...]
        dt = dt_ref[...]
        acs = _dot3(_split3(dt * a_ref[...]), _tri_ones(True), left=True)
        acs_t = acs.T
        xc = x * dt
        bv = b_ref[...].astype(BF16)
        cv = c_ref[...].astype(BF16)
        cb = lax.dot_general(cv, bv, NT_DIMS, preferred_element_type=F32)
        lane = lax.broadcasted_iota(jnp.int32, (CHUNK, LANES), 1)
        pieces = []
        for pp in range(gw // LANES):
            xcp = xc[:, pp * LANES:(pp + 1) * LANES]
            acc = jnp.zeros((CHUNK, LANES), F32)
            for e in range(2):
                low, _ = _decay_pair(acs, acs_t, pp * LANES + e * SSD_HEAD_DIM)
                mine = (lane >= e * SSD_HEAD_DIM) & (lane < (e + 1) * SSD_HEAD_DIM)
                xm = jnp.where(mine, xcp, 0.0).astype(BF16)
                acc = acc + jnp.dot((cb * low).astype(BF16), xm, preferred_element_type=F32)
            pieces.append(acc)
        y_diag = jnp.concatenate(pieces, axis=1)
        h_prev = h_s[...]
        y_off = jnp.dot(cv, h_prev.astype(BF16), preferred_element_type=F32) * jnp.exp(acs)
        y_ref[...] = y_diag + y_off + d_ref[...] * x
        prev_ref[0] = h_prev
        last = acs[CHUNK - 1:CHUNK, :]
        w = jnp.exp(last - acs)
        st = jnp.dot(bt_ref[...].astype(BF16), (xc * w).astype(BF16), preferred_element_type=F32)
        h_s[...] = h_prev * jnp.exp(last) + st

    tok = pl.BlockSpec((CHUNK, gw), lambda g, c: (c, g))
    grp = pl.BlockSpec((CHUNK, SSD_STATE), lambda g, c: (c, g))
    vec = pl.BlockSpec((1, gw), lambda g, c: (0, g))
    return pl.pallas_call(
        body, name=name, grid=(SSD_GROUPS, nc),
        in_specs=[tok, tok, grp, grp, pl.BlockSpec((SSD_STATE, CHUNK), lambda g, c: (g, c)), vec, vec],
        out_specs=[tok, pl.BlockSpec((1, SSD_STATE, gw), lambda g, c: (c, 0, g))],
        out_shape=[_sds((t, SSD_INNER), F32), _sds((nc, SSD_STATE, SSD_INNER), F32)],
        scratch_shapes=[pltpu.VMEM((SSD_STATE, gw), F32)],
        compiler_params=_cp("parallel", "arbitrary"))(xs, dtx, bm, cm, bm_t, a_x, d_x)


def _ssd_bwd(xs, dtx, bm, cm, cm_t, prev, dy, a_x, d_x, *, name):
    t = xs.shape[0]
    nc = t // CHUNK
    gw = GROUP_W

    def body(xs_ref, dt_ref, b_ref, c_ref, ct_ref, prev_ref, dy_ref, a_ref, d_ref,
             dxs_ref, ddt_ref, db_ref, dc_ref, da_ref, dd_ref, g_s):
        first = pl.program_id(1) == 0

        @pl.when(first)
        def _():
            g_s[...] = jnp.zeros((SSD_STATE, gw), F32)

        x = xs_ref[...]
        dt = dt_ref[...]
        a = a_ref[...]
        dyv = dy_ref[...]
        acs = _dot3(_split3(dt * a), _tri_ones(True), left=True)
        acs_t = acs.T
        xc = x * dt
        bv = b_ref[...].astype(BF16)
        cv = c_ref[...].astype(BF16)
        cb = lax.dot_general(cv, bv, NT_DIMS, preferred_element_type=F32)
        cb_t = lax.dot_general(bv, cv, NT_DIMS, preferred_element_type=F32)
        last = acs[CHUNK - 1:CHUNK, :]
        w = jnp.exp(last - acs)
        cd = jnp.exp(last)
        p_in = prev_ref[0]
        p_b = p_in.astype(BF16)
        g_out = g_s[...]
        g_b = g_out.astype(BF16)
        dy_e = dyv * jnp.exp(acs)
        dy_eb = dy_e.astype(BF16)
        y_off_raw = jnp.dot(cv, p_b, preferred_element_type=F32)
        dacs = dy_e * y_off_raw
        d_c = lax.dot_general(dy_eb, p_b, NT_DIMS, preferred_element_type=F32)
        d_prev = jnp.dot(ct_ref[...].astype(BF16), dy_eb, preferred_element_type=F32)
        q_l = jnp.dot(bv, g_b, preferred_element_type=F32)
        dxc = w * q_l
        tw = xc * q_l * w
        dacs = dacs - tw
        d_b = lax.dot_general((xc * w).astype(BF16), g_b, NT_DIMS, preferred_element_type=F32)
        last_add = jnp.sum(tw, axis=0, keepdims=True) + cd * jnp.sum(g_out * p_in, axis=0, keepdims=True)
        g_s[...] = cd * g_out + d_prev
        lane = lax.broadcasted_iota(jnp.int32, (CHUNK, LANES), 1)
        d_cb = jnp.zeros((CHUNK, CHUNK), F32)
        d_cb_t = jnp.zeros((CHUNK, CHUNK), F32)
        dxc_pieces, dacs_pieces = [], []
        for pp in range(gw // LANES):
            xcp = xc[:, pp * LANES:(pp + 1) * LANES]
            dyp = dyv[:, pp * LANES:(pp + 1) * LANES]
            dxcp = jnp.zeros((CHUNK, LANES), F32)
            dacsp = jnp.zeros((CHUNK, LANES), F32)
            for e in range(2):
                low, upp = _decay_pair(acs, acs_t, pp * LANES + e * SSD_HEAD_DIM)
                mine = (lane >= e * SSD_HEAD_DIM) & (lane < (e + 1) * SSD_HEAD_DIM)
                m_low = cb * low
                m_upp = cb_t * upp
                dym = jnp.where(mine, dyp, 0.0).astype(BF16)
                xm = jnp.where(mine, xcp, 0.0).astype(BF16)
                dxcp = dxcp + jnp.dot(m_upp.astype(BF16), dym, preferred_element_type=F32)
                d_m = lax.dot_general(dym, xm, NT_DIMS, preferred_element_type=F32)
                d_m_t = lax.dot_general(xm, dym, NT_DIMS, preferred_element_type=F32)
                rs = jnp.sum(d_m * m_low, axis=1, keepdims=True)
                cs = jnp.sum(d_m_t * m_upp, axis=1, keepdims=True)
                dacsp = dacsp + jnp.where(lane == e * SSD_HEAD_DIM, rs - cs, 0.0)
                d_cb = d_cb + d_m * low
                d_cb_t = d_cb_t + d_m_t * upp
            dxc_pieces.append(dxcp)
            dacs_pieces.append(dacsp)
        dxc = dxc + jnp.concatenate(dxc_pieces, axis=1)
        dacs = dacs + jnp.concatenate(dacs_pieces, axis=1)
        rowi = lax.broadcasted_iota(jnp.int32, (CHUNK, gw), 0)
        dacs = dacs + jnp.where(rowi == CHUNK - 1, last_add, 0.0)
        dc_ref[...] = d_c + jnp.dot(d_cb.astype(BF16), bv, preferred_element_type=F32)
        db_ref[...] = d_b + jnp.dot(d_cb_t.astype(BF16), cv, preferred_element_type=F32)
        dda = _dot3(_split3(dacs), _tri_ones(False), left=True)
        ddt_ref[...] = dda * a + dxc * x
        dxs_ref[...] = dxc * dt + d_ref[...] * dyv
        pa = jnp.sum(dda * dt, axis=0, keepdims=True)
        pd = jnp.sum(dyv * x, axis=0, keepdims=True)

        @pl.when(first)
        def _():
            da_ref[...] = pa
            dd_ref[...] = pd

        @pl.when(jnp.logical_not(first))
        def _():
            da_ref[...] += pa
            dd_ref[...] += pd

    rc = lambda c: nc - 1 - c
    tok = pl.BlockSpec((CHUNK, gw), lambda g, c: (rc(c), g))
    grp = pl.BlockSpec((CHUNK, SSD_STATE), lambda g, c: (rc(c), g))
    vec = pl.BlockSpec((1, gw), lambda g, c: (0, g))
    return pl.pallas_call(
        body, name=name, grid=(SSD_GROUPS, nc),
        in_specs=[tok, tok, grp, grp, pl.BlockSpec((SSD_STATE, CHUNK), lambda g, c: (g, rc(c))),
                  pl.BlockSpec((1, SSD_STATE, gw), lambda g, c: (rc(c), 0, g)), tok, vec, vec],
        out_specs=[tok, tok, grp, grp, vec, vec],
        out_shape=[_sds((t, SSD_INNER), F32), _sds((t, SSD_INNER), F32), _sds((t, gw), F32), _sds((t, gw), F32),
                   _sds((1, SSD_INNER), F32), _sds((1, SSD_INNER), F32)],
        scratch_shapes=[pltpu.VMEM((SSD_STATE, gw), F32)],
        compiler_params=_cp("parallel", "arbitrary"))(xs, dtx, bm, cm, cm_t, prev, dy, a_x, d_x)


def _gnorm_fwd(y, proj, g, *, name):
    t = y.shape[0]
    tr = _tile(t, 640)
    zb = OFF_Z // GROUP_W

    def body(y_ref, z_ref, g_ref, o_ref):
        z = z_ref[...]
        v = y_ref[...] * (z * _sigmoid(z))
        r = lax.rsqrt(jnp.mean(v * v, axis=1, keepdims=True) + RMS_EPS)
        o_ref[...] = (v * r * g_ref[...]).astype(BF16)

    blk = pl.BlockSpec((tr, GROUP_W), lambda i, j: (i, j))
    return pl.pallas_call(
        body, name=name, grid=(t // tr, SSD_GROUPS),
        in_specs=[blk, pl.BlockSpec((tr, GROUP_W), lambda i, j: (i, zb + j)),
                  pl.BlockSpec((1, GROUP_W), lambda i, j: (0, j))],
        out_specs=blk, out_shape=_sds((t, SSD_INNER), BF16),
        compiler_params=_cp("parallel", "parallel"))(y, proj, g)


def _gnorm_bwd(dout, y, proj, g, *, name):
    t = y.shape[0]
    tr = _tile(t, 640)
    zb = OFF_Z // GROUP_W

    def body(do_ref, y_ref, z_ref, g_ref, dy_ref, dz_ref, dg_ref):
        i = pl.program_id(1)
        z = z_ref[...]
        yv = y_ref[...]
        sg = _sigmoid(z)
        sz = z * sg
        v = yv * sz
        r = lax.rsqrt(jnp.mean(v * v, axis=1, keepdims=True) + RMS_EPS)
        dov = do_ref[...].astype(F32)
        gdo = dov * g_ref[...]
        m = jnp.mean(v * gdo, axis=1, keepdims=True)
        dv = r * gdo - v * (r * r * r) * m
        dy_ref[...] = dv * sz
        dz_ref[...] = (dv * yv * (sg * (1.0 + z * (1.0 - sg)))).astype(BF16)
        pg = jnp.sum(dov * v * r, axis=0, keepdims=True)

        @pl.when(i == 0)
        def _():
            dg_ref[...] = pg

        @pl.when(i > 0)
        def _():
            dg_ref[...] += pg

    blk = pl.BlockSpec((tr, GROUP_W), lambda j, i: (i, j))
    vec = pl.BlockSpec((1, GROUP_W), lambda j, i: (0, j))
    return pl.pallas_call(
        body, name=name, grid=(SSD_GROUPS, t // tr),
        in_specs=[blk, blk, pl.BlockSpec((tr, GROUP_W), lambda j, i: (i, zb + j)), vec],
        out_specs=[blk, blk, vec],
        out_shape=[_sds((t, SSD_INNER), F32), _sds((t, SSD_INNER), BF16), _sds((1, SSD_INNER), F32)],
        compiler_params=_cp("parallel", "arbitrary"))(dout, y, proj, g)


def _mix_fwd(proj, ya, ys, *, name):
    t, d = ya.shape
    tr = _tile(t, 640)
    ba, bs = OFF_GA // d, OFF_GS // d

    def body(ga_ref, gs_ref, ya_ref, ys_ref, o_ref):
        o_ref[...] = (_sigmoid(ga_ref[...]) * ya_ref[...] + _sigmoid(gs_ref[...]) * ys_ref[...]).astype(BF16)

    blk = pl.BlockSpec((tr, d), lambda i: (i, 0))
    return pl.pallas_call(
        body, name=name, grid=(t // tr,),
        in_specs=[pl.BlockSpec((tr, d), lambda i: (i, ba)), pl.BlockSpec((tr, d), lambda i: (i, bs)), blk, blk],
        out_specs=blk, out_shape=_sds((t, d), BF16), compiler_params=_cp("parallel"))(proj, proj, ya, ys)


def _mix_bwd(dmix, proj, ya, ys, *, name):
    t, d = ya.shape
    tr = _tile(t, 640)
    ba, bs = OFF_GA // d, OFF_GS // d

    def body(dm_ref, ga_ref, gs_ref, ya_ref, ys_ref, dya_ref, dys_ref, dga_ref, dgs_ref):
        dm = dm_ref[...]
        sa = _sigmoid(ga_ref[...])
        ss = _sigmoid(gs_ref[...])
        dya_ref[...] = (sa * dm).astype(BF16)
        dys_ref[...] = (ss * dm).astype(BF16)
        dga_ref[...] = (dm * ya_ref[...] * sa * (1.0 - sa)).astype(BF16)
        dgs_ref[...] = (dm * ys_ref[...] * ss * (1.0 - ss)).astype(BF16)

    blk = pl.BlockSpec((tr, d), lambda i: (i, 0))
    return pl.pallas_call(
        body, name=name, grid=(t // tr,),
        in_specs=[blk, pl.BlockSpec((tr, d), lambda i: (i, ba)), pl.BlockSpec((tr, d), lambda i: (i, bs)), blk, blk],
        out_specs=[blk] * 4, out_shape=[_sds((t, d), BF16)] * 4,
        compiler_params=_cp("parallel"))(dmix, proj, proj, ya, ys)


def _loss_grad(h, target, *, name):
    t, d = h.shape
    tr = LANES
    assert ROW0 == tr

    def body(h_ref, t_ref, dh_ref, loss_ref):
        i = pl.program_id(0)

        @pl.when(i == 0)
        def _():
            dh_ref[...] = jnp.zeros((tr, d), F32)
            loss_ref[...] = jnp.zeros((1, LANES), F32)

        @pl.when(i > 0)
        def _():
            err = h_ref[...] - t_ref[...]
            dh_ref[...] = err * (1.0 / d)
            part = jnp.sum(jnp.sum(err * err, axis=1, keepdims=True), axis=0, keepdims=True)
            loss_ref[...] += jnp.broadcast_to(part * (0.5 / d), (1, LANES))

    blk = pl.BlockSpec((tr, d), lambda i: (i, 0))
    return pl.pallas_call(
        body, name=name, grid=(t // tr,),
        in_specs=[blk, pl.BlockSpec((tr, d), lambda i: (jnp.maximum(i - 1, 0), 0))],
        out_specs=[blk, pl.BlockSpec((1, LANES), lambda i: (0, 0))],
        out_shape=[_sds((t, d), F32), _sds((1, LANES), F32)],
        compiler_params=_cp("arbitrary"))(h, target)


def _adamw(g, w, m, v, *, name):
    r = g.shape[0]
    tr = _tile(r, 1024, base=8)
    c1 = 1.0 - ADAM_B1 ** ADAM_STEP
    c2 = 1.0 - ADAM_B2 ** ADAM_STEP

    def body(g_ref, w_ref, m_ref, v_ref, d_ref, nm_ref, nv_ref):
        gv = g_ref[...]
        nm = ADAM_B1 * m_ref[...] + (1.0 - ADAM_B1) * gv
        nv = ADAM_B2 * v_ref[...] + (1.0 - ADAM_B2) * (gv * gv)
        nm_ref[...] = nm
        nv_ref[...] = nv
        d_ref[...] = -ADAM_LR * ((nm / c1) / (jnp.sqrt(nv / c2) + ADAM_EPS) + ADAM_WD * w_ref[...])

    blk = pl.BlockSpec((tr, LANES), lambda i: (i, 0))
    return pl.pallas_call(body, name=name, grid=(r // tr,), in_specs=[blk] * 4, out_specs=[blk] * 3,
                          out_shape=[_sds((r, LANES), F32)] * 3, compiler_params=_cp("parallel"))(g, w, m, v)


def _add2(a, b, *, name):
    n, r, _ = a.shape
    tr = _tile(r, 1024, base=8)

    def body(a_ref, b_ref, o_ref):
        o_ref[...] = a_ref[...] + b_ref[...]

    blk = pl.BlockSpec((1, tr, LANES), lambda s, i: (s, i, 0))
    return pl.pallas_call(body, name=name, grid=(n, r // tr), in_specs=[blk, blk], out_specs=blk,
                          out_shape=_sds(a.shape, F32), compiler_params=_cp("parallel", "parallel"))(a, b)


def _sum_chips(q, *, name):
    n, r, _ = q.shape
    tr = _tile(r, 1024, base=8)

    def body(q_ref, o_ref):
        tot = q_ref[0]
        for s in range(1, n):
            tot = tot + q_ref[s]
        o_ref[...] = tot

    return pl.pallas_call(body, name=name, grid=(r // tr,),
                          in_specs=[pl.BlockSpec((n, tr, LANES), lambda i: (0, i, 0))],
                          out_specs=pl.BlockSpec((tr, LANES), lambda i: (i, 0)),
                          out_shape=_sds((r, LANES), F32), compiler_params=_cp("parallel"))(q)


_ANY = pl.BlockSpec(memory_space=pl.ANY)
_MESH = pl.DeviceIdType.MESH


def _place():
    x, y, c = lax.axis_index("x"), lax.axis_index("y"), lax.axis_index("c")
    return x, y, c, [(1 - x, y), (x, 1 - y), (1 - x, 1 - y)]


def _chip_allgather(mine, *, name):
    r = mine.shape[0]
    half = r // 2

    def body(x_ref, o_ref, send_sems, recv_sems, local_sem):
        x, y, c, chips = _place()
        k = 2 * x + y

        def rows(kk, hh):
            return o_ref.at[kk, pl.ds(hh * half, half)]

        def copy(n, src, dst, to):
            return pltpu.make_async_remote_copy(src_ref=src, dst_ref=dst, send_sem=send_sems.at[n],
                                                recv_sem=recv_sems.at[n], device_id=to, device_id_type=_MESH)

        local = pltpu.make_async_copy(x_ref, o_ref.at[k], local_sem)
        local.start()
        sends = [copy(n, x_ref.at[pl.ds(c * half, half)], rows(k, c), (cx, cy, c)) for n, (cx, cy) in enumerate(chips)]
        for cp in sends:
            cp.start()
        passed = []
        for n, (cx, cy) in enumerate(chips):
            kk = 2 * cx + cy
            copy(n, rows(kk, c), rows(kk, c), (cx, cy, c)).wait_recv()
            fw = copy(3 + n, rows(kk, c), rows(kk, c), (x, y, 1 - c))
            fw.start()
            passed.append(fw)
        for n, (cx, cy) in enumerate(chips):
            kk = 2 * cx + cy
            copy(3 + n, rows(kk, 1 - c), rows(kk, 1 - c), (x, y, 1 - c)).wait_recv()
        for cp in sends + passed:
            cp.wait_send()
        local.wait()

    return pl.pallas_call(
        body, name=name, in_specs=[_ANY], out_specs=_ANY, out_shape=_sds((N_CHIPS, r, LANES), mine.dtype),
        scratch_shapes=[pltpu.SemaphoreType.DMA((6,)), pltpu.SemaphoreType.DMA((6,)), pltpu.SemaphoreType.DMA])(mine)


def _sibling_swap(mine, *, name):
    n = mine.shape[0]

    def body(x_ref, o_ref, send_sems, recv_sems):
        x, y, c, _ = _place()
        cps = [pltpu.make_async_remote_copy(src_ref=x_ref.at[s], dst_ref=o_ref.at[s], send_sem=send_sems.at[s],
                                            recv_sem=recv_sems.at[s], device_id=(x, y, 1 - c), device_id_type=_MESH)
               for s in range(n)]
        for cp in cps:
            cp.start()
        for cp in cps:
            cp.wait()

    return pl.pallas_call(
        body, name=name, in_specs=[_ANY], out_specs=_ANY, out_shape=_sds(mine.shape, mine.dtype),
        scratch_shapes=[pltpu.SemaphoreType.DMA((n,)), pltpu.SemaphoreType.DMA((n,))])(mine)


def _chip_exchange(h, *, name):
    def body(h_ref, q_ref, send_sems, recv_sems, local_sem):
        x, y, c, chips = _place()
        k = 2 * x + y
        local = pltpu.make_async_copy(h_ref.at[k], q_ref.at[k], local_sem)
        local.start()
        cps = []
        for n, (cx, cy) in enumerate(chips):
            kk = 2 * cx + cy
            cps.append(pltpu.make_async_remote_copy(src_ref=h_ref.at[kk], dst_ref=q_ref.at[k], send_sem=send_sems.at[n],
                                                    recv_sem=recv_sems.at[n], device_id=(cx, cy, c),
                                                    device_id_type=_MESH))
        for cp in cps:
            cp.start()
        for n, (cx, cy) in enumerate(chips):
            kk = 2 * cx + cy
            pltpu.make_async_remote_copy(src_ref=h_ref.at[kk], dst_ref=q_ref.at[kk], send_sem=send_sems.at[n],
                                         recv_sem=recv_sems.at[n], device_id=(cx, cy, c),
                                         device_id_type=_MESH).wait_recv()
        for cp in cps:
            cp.wait_send()
        local.wait()

    return pl.pallas_call(
        body, name=name, in_specs=[_ANY], out_specs=_ANY, out_shape=_sds(h.shape, h.dtype),
        scratch_shapes=[pltpu.SemaphoreType.DMA((3,)), pltpu.SemaphoreType.DMA((3,)), pltpu.SemaphoreType.DMA])(h)


def _sibling_allgather(mine, *, name):
    def body(x_ref, o_ref, send_sem, recv_sem, local_sem):
        x, y, c, _ = _place()
        local = pltpu.make_async_copy(x_ref, o_ref.at[c], local_sem)
        local.start()
        cp = pltpu.make_async_remote_copy(src_ref=x_ref, dst_ref=o_ref.at[c], send_sem=send_sem, recv_sem=recv_sem,
                                          device_id=(x, y, 1 - c), device_id_type=_MESH)
        cp.start()
        pltpu.make_async_remote_copy(src_ref=x_ref, dst_ref=o_ref.at[1 - c], send_sem=send_sem, recv_sem=recv_sem,
                                     device_id=(x, y, 1 - c), device_id_type=_MESH).wait_recv()
        cp.wait_send()
        local.wait()

    return pl.pallas_call(
        body, name=name, in_specs=[_ANY], out_specs=_ANY, out_shape=_sds((2,) + mine.shape, mine.dtype),
        scratch_shapes=[pltpu.SemaphoreType.DMA, pltpu.SemaphoreType.DMA, pltpu.SemaphoreType.DMA])(mine)


def _rows_of(flat, align):
    n = flat.shape[-1]
    rows = -(-n // LANES)
    rows = -(-rows // align) * align
    pad = rows * LANES - n
    if pad:
        flat = jnp.pad(flat, [(0, 0)] * (flat.ndim - 1) + [(0, pad)])
    return flat.reshape(flat.shape[:-1] + (rows, LANES))


def _pack_local(named, names, dtype, align):
    flat = jnp.concatenate([named[n].astype(dtype).reshape(-1) for n in names])
    return _rows_of(flat, align)


def _chip_segments(g, kind):
    if kind == "col":
        n = g.shape[-1] // N_CHIPS
        s = g.reshape(g.shape[:-1] + (N_CHIPS, n))
        return jnp.moveaxis(s, -2, 0).reshape(N_CHIPS, -1)
    k = g.shape[-2] // N_CHIPS
    s = g.reshape(g.shape[:-2] + (N_CHIPS, k, g.shape[-1]))
    return jnp.moveaxis(s, -3, 0).reshape(N_CHIPS, -1)


def _unshard(gathered, local_shape, kind):
    s = gathered.reshape((N_CHIPS,) + tuple(local_shape))
    if kind == "col":
        s = jnp.moveaxis(s, 0, -2)
        return s.reshape(s.shape[:-2] + (s.shape[-2] * s.shape[-1],))
    s = jnp.moveaxis(s, 0, -3)
    return s.reshape(s.shape[:-3] + (s.shape[-3] * s.shape[-2], s.shape[-1]))


def _gather_weights(local):
    kinds = dict(SHARDED)
    full = {}
    for names, dtype, align in ((GATHER_BF16, BF16, 32), (GATHER_F32, F32, 16)):
        packed = _pack_local(local, names, dtype, align)
        got = _chip_allgather(packed, name="gather_" + ("bf16" if dtype == BF16 else "f32"))
        got = got.reshape(N_CHIPS, -1)
        off = 0
        for n in names:
            size = int(np.prod(local[n].shape))
            full[n] = _unshard(got[:, off:off + size], local[n].shape, kinds[n])
            off += size
    return full


def _rope_tables(t):
    half = QK_ROPE // 2
    inv_freq = 1.0 / (ROPE_THETA ** (jnp.arange(0, QK_ROPE, 2, dtype=F32) / QK_ROPE))
    pos = jnp.maximum(jnp.arange(t, dtype=F32) - PAD, 0.0)
    ang = pos[:, None] * inv_freq[None, :]
    cos, sin = jnp.cos(ang), jnp.sin(ang)
    z = jnp.zeros((t, half), F32)
    z2 = jnp.zeros((t, LANES - QK_ROPE), F32)
    return (jnp.concatenate([cos, cos, z2], axis=1), jnp.concatenate([-sin, z, z2], axis=1),
            jnp.concatenate([z, sin, z2], axis=1))


def _expand_matrix():
    lane = np.arange(SSD_INNER) // SSD_HEAD_DIM
    e = (np.arange(LANES)[:, None] == lane[None, :]).astype(np.float32)
    return jnp.asarray(e, BF16)


def _layer_weights(full, rep, i):
    w = {}
    w_in = full["w_in"][i]
    zc = lambda n: jnp.zeros((D_MODEL, n), BF16)
    w["w_in"] = jnp.concatenate(
        [w_in[:, 0:1024], w_in[:, 1088:3136], w_in[:, 3136:6208], w_in[:, 6240:7264], w_in[:, 7264:8288],
         w_in[:, 1024:1088], zc(LANES - QK_ROPE), w_in[:, 6208:6240], zc(LANES - SSD_HEADS)], axis=1)
    wq = full["w_q_b"][i].reshape(Q_LORA, MLA_HEADS, QK_NOPE + QK_ROPE)
    w["w_q_b"] = jnp.pad(wq, ((0, 0), (0, 0), (0, QHEAD - QK_NOPE - QK_ROPE))).reshape(Q_LORA, MLA_HEADS * QHEAD)
    wkv = full["w_kv_b"][i].reshape(KV_LORA, MLA_HEADS, 2, QK_NOPE)
    w["w_kv_b"] = jnp.swapaxes(wkv, 1, 2).reshape(KV_LORA, 2 * MLA_HEADS * QK_NOPE)
    for n in ("w_o_attn", "w_o_ssd", "w_out", "w_down"):
        w[n] = full[n][i]
    w["w_up_g"] = full["w_up"][i][:, :D_FF]
    w["w_up_v"] = full["w_up"][i][:, D_FF:]
    w["ssd_conv_w"] = full["ssd_conv_w"][i]
    w["ffn_conv_wg"] = full["ffn_conv_w"][i][:, :D_FF]
    w["ffn_conv_wv"] = full["ffn_conv_w"][i][:, D_FF:]
    row = lambda v: v.reshape(1, -1)
    w["q_norm_g"] = row(rep["q_norm_g"][i])
    w["kv_norm_g"] = row(rep["kv_norm_g"][i])
    w["ssd_conv_b"] = row(rep["ssd_conv_b"][i])
    w["dt_bias"] = row(jnp.pad(rep["dt_bias"][i], (0, LANES - SSD_HEADS)))
    a = -jnp.exp(rep["a_log"][i])
    w["a"] = a
    w["a_x"] = row(jnp.repeat(a, SSD_HEAD_DIM))
    w["d_x"] = row(jnp.repeat(rep["d_skip"][i], SSD_HEAD_DIM))
    w["ssd_norm_g"] = row(rep["ssd_norm_g"][i])
    w["ffn_conv_bg"] = row(rep["ffn_conv_b"][i][:D_FF])
    w["ffn_conv_bv"] = row(rep["ffn_conv_b"][i][D_FF:])
    for n in ("ln1_g", "ln1_b", "ln2_g", "ln2_b"):
        w[n] = row(rep[n][i])
    return w


def _layer_fwd(h, w, tabs, expand, tag):
    cos, sin_a, sin_b = tabs
    s = {"h": h}
    proj = _mm(h, w["w_in"], name=tag + "in_proj", tn=768)
    s["proj"] = proj
    qn = _rms_fwd(proj, OFF_Q, Q_LORA, w["q_norm_g"], name=tag + "q_norm")
    q_raw = _mm(qn, w["w_q_b"], name=tag + "q_up")
    q = _rope_q_fwd(q_raw, cos, sin_a, sin_b, name=tag + "q_rope")
    kvn = _rms_fwd(proj, OFF_KV, KV_LORA, w["kv_norm_g"], name=tag + "kv_norm")
    kv = _mm(kvn, w["w_kv_b"], name=tag + "kv_up", out_dtype=BF16)
    kpe = _rope_k_fwd(proj, cos, sin_a, sin_b, name=tag + "k_rope")
    o, lse = _flash_fwd(q, kv, kpe, name=tag + "attn")
    ya = _mm(o, w["w_o_attn"], name=tag + "attn_out")
    s.update(qn=qn, q=q, kvn=kvn, kv=kv, kpe=kpe, o=o, lse=lse, ya=ya)
    xs, bm, cm, dtx = _ssd_prep_fwd(proj, w["ssd_conv_w"], w["ssd_conv_b"], w["dt_bias"], expand, name=tag + "ssd_prep")
    y, prev = _ssd_fwd(xs, dtx, bm, cm, bm.T, w["a_x"], w["d_x"], name=tag + "ssd_scan")
    yn = _gnorm_fwd(y, proj, w["ssd_norm_g"], name=tag + "ssd_norm")
    ys = _mm(yn, w["w_o_ssd"], name=tag + "ssd_out")
    s.update(xs=xs, bm=bm, cm=cm, dtx=dtx, y=y, prev=prev, yn=yn, ys=ys)
    mixed = _mix_fwd(proj, ya, ys, name=tag + "mix")
    br = _mm(mixed, w["w_out"], name=tag + "mix_out")
    pre1, h1 = _ln_fwd(h, br, w["ln1_g"], w["ln1_b"], name=tag + "ln1")
    s.update(mixed=mixed, pre1=pre1, h1=h1)
    ug = _mm(h1, w["w_up_g"], name=tag + "up_g", tn=1408)
    uv = _mm(h1, w["w_up_v"], name=tag + "up_v", tn=1408)
    act = _ffn_act_fwd(ug, uv, w["ffn_conv_wg"], w["ffn_conv_wv"], w["ffn_conv_bg"], w["ffn_conv_bv"],
                       name=tag + "ffn_act")
    ffn = _mm(act, w["w_down"], name=tag + "down")
    pre2, h2 = _ln_fwd(h1, ffn, w["ln2_g"], w["ln2_b"], name=tag + "ln2")
    s.update(ug=ug, uv=uv, act=act, pre2=pre2)
    return h2, s


def _layer_bwd(dh2, w, s, tabs, reduce_m, tag):
    cos, sin_a, sin_b = tabs
    g = {}
    proj = s["proj"]
    dpre2, g["ln2_g"], g["ln2_b"] = _ln_bwd(dh2, s["pre2"], w["ln2_g"], name=tag + "ln2_bwd")
    g["w_down"] = _mm_tn(s["act"], dpre2, name=tag + "down_dw")
    dact = _mm(dpre2, w["w_down"], trans_b=True, name=tag + "down_dx", tn=1408)
    dcg, dcv, dwg, dwv, dbg, dbv = _ffn_act_bwd(s["ug"], s["uv"], dact, w["ffn_conv_wg"], w["ffn_conv_wv"],
                                                w["ffn_conv_bg"], w["ffn_conv_bv"], name=tag + "ffn_act_bwd")
    g["ffn_conv_w"] = jnp.concatenate([dwg, dwv], axis=1)
    g["ffn_conv_b"] = jnp.concatenate([dbg, dbv], axis=1).reshape(-1)
    dug = _conv_bwd_input(dcg, w["ffn_conv_wg"], FFN_CONV, name=tag + "ffn_conv_bwd_g", tc=1408)
    duv = _conv_bwd_input(dcv, w["ffn_conv_wv"], FFN_CONV, name=tag + "ffn_conv_bwd_v", tc=1408)
    g["w_up"] = jnp.concatenate([_mm_tn(s["h1"], dug, name=tag + "up_g_dw", tn=1408),
                                 _mm_tn(s["h1"], duv, name=tag + "up_v_dw", tn=1408)], axis=1)
    dh1 = _mm(dug, w["w_up_g"], trans_b=True, add=dpre2, add_scale=ALPHA, name=tag + "up_g_dx")
    dh1 = _mm(duv, w["w_up_v"], trans_b=True, add=dh1, name=tag + "up_v_dx")
    dpre1, g["ln1_g"], g["ln1_b"] = _ln_bwd(dh1, s["pre1"], w["ln1_g"], name=tag + "ln1_bwd")
    g["w_out"] = _mm_tn(s["mixed"], dpre1, name=tag + "mix_out_dw")
    dmix = _mm(dpre1, w["w_out"], trans_b=True, name=tag + "mix_out_dx")
    dya, dys, dga, dgs = _mix_bwd(dmix, proj, s["ya"], s["ys"], name=tag + "mix_bwd")
    g["w_o_attn"] = _mm_tn(s["o"], dya, name=tag + "attn_out_dw")
    do = _mm(dya, w["w_o_attn"], trans_b=True, out_dtype=BF16, name=tag + "attn_out_dx")
    delta = _attn_delta(do, s["o"], name=tag + "attn_delta")
    dq, dkn, dkp, dv = _flash_bwd(s["q"], s["kv"], s["kpe"], do, s["lse"], delta, name=tag + "attn_bwd")
    dq_raw = _rope_q_bwd(dq, cos, sin_a, sin_b, name=tag + "q_rope_bwd")
    dwq = _mm_tn(s["qn"], dq_raw, name=tag + "q_up_dw")
    g["w_q_b"] = dwq.reshape(Q_LORA, MLA_HEADS, QHEAD)[:, :, :QK_NOPE + QK_ROPE].reshape(Q_LORA, -1)
    dqn = _mm(dq_raw, w["w_q_b"], trans_b=True, out_dtype=BF16, name=tag + "q_up_dx")
    dqlat, dgq = _rms_bwd(dqn, proj, OFF_Q, Q_LORA, w["q_norm_g"], name=tag + "q_norm_bwd")
    g["q_norm_g"] = dgq.reshape(-1)
    dkv = jnp.concatenate([dkn, dv], axis=1)
    dwkv = _mm_tn(s["kvn"], dkv, name=tag + "kv_up_dw")
    g["w_kv_b"] = jnp.swapaxes(dwkv.reshape(KV_LORA, 2, MLA_HEADS, QK_NOPE), 1, 2).reshape(KV_LORA, -1)
    dkvn = _mm(dkv, w["w_kv_b"], trans_b=True, out_dtype=BF16, name=tag + "kv_up_dx")
    dkvlat, dgkv = _rms_bwd(dkvn, proj, OFF_KV, KV_LORA, w["kv_norm_g"], name=tag + "kv_norm_bwd")
    g["kv_norm_g"] = dgkv.reshape(-1)
    dkpe = _rope_k_bwd(dkp, cos, sin_a, sin_b, name=tag + "k_rope_bwd")
    g["w_o_ssd"] = _mm_tn(s["yn"], dys, name=tag + "ssd_out_dw")
    dyn = _mm(dys, w["w_o_ssd"], trans_b=True, out_dtype=BF16, name=tag + "ssd_out_dx")
    dy, dz, dgn = _gnorm_bwd(dyn, s["y"], proj, w["ssd_norm_g"], name=tag + "ssd_norm_bwd")
    g["ssd_norm_g"] = dgn.reshape(-1)
    dxs, ddtx, dbm, dcm, da_x, dd_x = _ssd_bwd(s["xs"], s["dtx"], s["bm"], s["cm"], s["cm"].T, s["prev"], dy,
                                               w["a_x"], w["d_x"], name=tag + "ssd_scan_bwd")
    g["a_log"] = da_x.reshape(SSD_HEADS, SSD_HEAD_DIM).sum(axis=1) * w["a"]
    g["d_skip"] = dd_x.reshape(SSD_HEADS, SSD_HEAD_DIM).sum(axis=1)
    dconv, ddtr, dcw, dcb, ddtb = _ssd_prep_bwd_a(proj, dxs, dbm, dcm, ddtx, w["ssd_conv_w"], w["ssd_conv_b"],
                                                  w["dt_bias"], reduce_m, name=tag + "ssd_prep_bwd")
    g["ssd_conv_w"] = dcw
    g["ssd_conv_b"] = dcb.reshape(-1)
    g["dt_bias"] = ddtb.reshape(-1)[:SSD_HEADS]
    dxbc = _conv_bwd_input(dconv, w["ssd_conv_w"], SSD_CONV, name=tag + "ssd_conv_bwd", tc=1024)
    h = s["h"]
    comps = ((dqlat, OFF_Q), (dkvlat, OFF_KV), (dz, OFF_Z), (dxbc, OFF_XBC), (dga, OFF_GA), (dgs, OFF_GS),
             (dkpe, OFF_KPE), (ddtr, OFF_DT))
    dws = {}
    dh = None
    for n, (dc, off) in enumerate(comps):
        width = dc.shape[1]
        dws[off] = _mm_tn(h, dc, name=f"{tag}in_dw{n}")
        w_c = w["w_in"][:, off:off + width]
        if dh is None:
            dh = _mm(dc, w_c, trans_b=True, add=dpre1, add_scale=ALPHA, name=f"{tag}in_dx{n}")
        else:
            dh = _mm(dc, w_c, trans_b=True, add=dh, name=f"{tag}in_dx{n}")
    g["w_in"] = jnp.concatenate([dws[OFF_Q], dws[OFF_KV], dws[OFF_KPE][:, :QK_ROPE], dws[OFF_Z], dws[OFF_XBC],
                                 dws[OFF_DT][:, :SSD_HEADS], dws[OFF_GA], dws[OFF_GS]], axis=1)
    return dh, g


def _local_step(x, target, full, rep):
    seq = x.shape[0]
    t = seq + ROW0
    tabs = _rope_tables(t)
    expand = _expand_matrix()
    reduce_m = expand.T
    xin = jnp.concatenate([jnp.zeros((PAD, D_MODEL), F32), full["meta_tokens"], x], axis=0)
    row = lambda v: v.reshape(1, -1)
    _, h = _ln_fwd(xin, None, row(rep["emb_ln_g"]), row(rep["emb_ln_b"]), name="emb_ln")
    ws, saved = [], []
    for i in range(DEPTH):
        w = _layer_weights(full, rep, i)
        h, s = _layer_fwd(h, w, tabs, expand, f"l{i}_")
        ws.append(w)
        saved.append(s)
    dh, loss = _loss_grad(h, target, name="loss")
    layer_grads = [None] * DEPTH
    for i in reversed(range(DEPTH)):
        dh, layer_grads[i] = _layer_bwd(dh, ws[i], saved[i], tabs, reduce_m, f"l{i}_")
    dxin, dg, db = _ln_bwd(dh, xin, row(rep["emb_ln_g"]), name="emb_ln_bwd")
    grads = {n: jnp.stack([layer_grads[i][n] for i in range(DEPTH)]) for n in layer_grads[0]}
    grads["emb_ln_g"] = dg.reshape(-1)
    grads["emb_ln_b"] = db.reshape(-1)
    grads["meta_tokens"] = dxin[PAD:ROW0]
    return loss, dxin[ROW0:], grads


def _reduce_grads(grads):
    segs = [_chip_segments(grads[n], kind) for n, kind in SHARDED]
    small = jnp.concatenate([grads[n].reshape(-1) for n in REPLICATED])
    segs.append(jnp.broadcast_to(small[None], (N_CHIPS, small.shape[0])))
    g4 = _rows_of(jnp.concatenate(segs, axis=1), REDUCE_ROW_ALIGN)
    r = g4.shape[1]
    g5 = g4.reshape(N_CHIPS, 2, r // 2, LANES)
    c = lax.axis_index("c")
    mine = lax.dynamic_index_in_dim(g5, c, axis=1, keepdims=False)
    theirs = lax.dynamic_index_in_dim(g5, 1 - c, axis=1, keepdims=False)
    got = _sibling_swap(theirs, name="reduce_pair_swap")
    pair = _add2(mine, got, name="reduce_pair_add")
    parts = _chip_exchange(pair, name="reduce_chip_exchange")
    half = _sum_chips(parts, name="reduce_chip_sum")
    both = _sibling_allgather(half, name="reduce_pair_gather")
    return both.reshape(r, LANES)


def kernel(x, meta_tokens, emb_ln_g, emb_ln_b, w_in, q_norm_g, w_q_b, kv_norm_g, w_kv_b, w_o_attn, ssd_conv_w, ssd_conv_b, dt_bias, a_log, d_skip, ssd_norm_g, w_o_ssd, w_out, ln1_g, ln1_b, w_up, ffn_conv_w, ffn_conv_b, w_down, ln2_g, ln2_b, loss_target, m_meta_tokens, m_emb_ln_g, m_emb_ln_b, m_w_in, m_q_norm_g, m_w_q_b, m_kv_norm_g, m_w_kv_b, m_w_o_attn, m_ssd_conv_w, m_ssd_conv_b, m_dt_bias, m_a_log, m_d_skip, m_ssd_norm_g, m_w_o_ssd, m_w_out, m_ln1_g, m_ln1_b, m_w_up, m_ffn_conv_w, m_ffn_conv_b, m_w_down, m_ln2_g, m_ln2_b, v_meta_tokens, v_emb_ln_g, v_emb_ln_b, v_w_in, v_q_norm_g, v_w_q_b, v_kv_norm_g, v_w_kv_b, v_w_o_attn, v_ssd_conv_w, v_ssd_conv_b, v_dt_bias, v_a_log, v_d_skip, v_ssd_norm_g, v_w_o_ssd, v_w_out, v_ln1_g, v_ln1_b, v_w_up, v_ffn_conv_w, v_ffn_conv_b, v_w_down, v_ln2_g, v_ln2_b):
    given = dict(locals())
    local_w = {n: given[n] for n in WEIGHTS}
    local_m = {n: given["m_" + n] for n in WEIGHTS}
    local_v = {n: given["v_" + n] for n in WEIGHTS}
    full = _gather_weights(local_w)
    rep = {n: local_w[n] for n in REPLICATED}
    loss, grad_x, grads = _local_step(x[0], loss_target[0], full, rep)
    g_flat = _reduce_grads(grads)
    order = [n for n, _ in SHARDED] + list(REPLICATED)
    rows = g_flat.shape[0]

    def pack(named):
        flat = jnp.concatenate([named[n].astype(F32).reshape(-1) for n in order])
        return jnp.pad(flat, (0, rows * LANES - flat.shape[0])).reshape(rows, LANES)

    delta, new_m, new_v = _adamw(g_flat, pack(local_w), pack(local_m), pack(local_v), name="adamw")

    def unpack(flat2d):
        flat = flat2d.reshape(-1)
        out, off = {}, 0
        for n in order:
            size = int(np.prod(local_w[n].shape))
            out[n] = flat[off:off + size].reshape(local_w[n].shape)
            off += size
        return out

    total = lax.psum(loss[0, 0], ("x", "y", "c"))
    outs = [total, grad_x[None]]
    for flat2d in (g_flat, delta, new_m, new_v):
        named = unpack(flat2d)
        outs.extend(named[n] for n in WEIGHTS)
    return tuple(outs)
```

```python
import functools
import math

import numpy as np
import jax
import jax.numpy as jnp
from jax import lax
from jax.experimental import pallas as pl
from jax.experimental.pallas import tpu as pltpu

F32 = jnp.float32
BF16 = jnp.bfloat16

D_MODEL = 1024
N_META = 16
DEPTH = 2
MLA_HEADS = 8
Q_LORA = 768
KV_LORA = 256
QK_NOPE = 128
QK_ROPE = 64
V_HEAD = 128
ROPE_THETA = 10000.0
NEG_INF = -1e30
SSD_INNER = 2048
SSD_HEAD_DIM = 64
SSD_HEADS = 32
SSD_GROUPS = 4
SSD_STATE = 128
SSD_CONV = 4
SSD_CONV_DIM = 3072
CHUNK = 128
D_FF = 2816
FFN_CONV = 3
LN_EPS = 1e-5
RMS_EPS = 1e-6
ALPHA = (2 * DEPTH) ** 0.25
ATTN_SCALE = (QK_NOPE + QK_ROPE) ** -0.5
ADAM_LR = 0.001
ADAM_B1 = 0.9
ADAM_B2 = 0.999
ADAM_EPS = 1e-08
ADAM_WD = 0.01
ADAM_STEP = 10

LANES = 128
PAD = 112
ROW0 = PAD + N_META
QHEAD = 256
GROUP_W = SSD_INNER // SSD_GROUPS
HALO = 8
VMEM_LIMIT_BYTES = 56 * 1024 * 1024
N_CHIPS = 4

OFF_Q, OFF_KV, OFF_Z, OFF_XBC, OFF_GA, OFF_GS, OFF_KPE, OFF_DT = 0, 768, 1024, 3072, 6144, 7168, 8192, 8320
IN_COLS_P = 8448

NT_DIMS = (((1,), (1,)), ((), ()))
NN_DIMS = (((1,), (0,)), ((), ()))
TN_DIMS = (((0,), (0,)), ((), ()))

SHARDED = (("meta_tokens", "col"), ("w_in", "col"), ("w_q_b", "col"), ("w_kv_b", "col"), ("w_o_attn", "row"),
           ("ssd_conv_w", "col"), ("w_o_ssd", "row"), ("w_out", "row"), ("w_up", "col"), ("ffn_conv_w", "col"),
           ("w_down", "row"))
REPLICATED = ("emb_ln_g", "emb_ln_b", "q_norm_g", "kv_norm_g", "ssd_conv_b", "dt_bias", "a_log", "d_skip",
              "ssd_norm_g", "ln1_g", "ln1_b", "ffn_conv_b", "ln2_g", "ln2_b")
WEIGHTS = ("meta_tokens", "emb_ln_g", "emb_ln_b", "w_in", "q_norm_g", "w_q_b", "kv_norm_g", "w_kv_b", "w_o_attn",
           "ssd_conv_w", "ssd_conv_b", "dt_bias", "a_log", "d_skip", "ssd_norm_g", "w_o_ssd", "w_out", "ln1_g",
           "ln1_b", "w_up", "ffn_conv_w", "ffn_conv_b", "w_down", "ln2_g", "ln2_b")
GATHER_BF16 = ("w_in", "w_q_b", "w_kv_b", "w_o_attn", "w_o_ssd", "w_out", "w_up", "w_down")
GATHER_F32 = ("meta_tokens", "ssd_conv_w", "ffn_conv_w")
REDUCE_ROW_ALIGN = 2048


def _tile(n, target, base=LANES):
    best = None
    d = base
    while d <= min(n, target):
        if n % d == 0:
            best = d
        d += base
    return n if best is None else best


def _cp(*sem):
    return pltpu.CompilerParams(dimension_semantics=sem, vmem_limit_bytes=VMEM_LIMIT_BYTES)


def _sds(shape, dtype):
    return jax.ShapeDtypeStruct(shape, dtype)


def _row_ids(i, tr, shape):
    return i * tr + lax.broadcasted_iota(jnp.int32, shape, 0)


def _sigmoid(x):
    return 1.0 / (1.0 + jnp.exp(-x))


def _mm(a, b, *, name, trans_b=False, out_dtype=F32, add=None, add_scale=1.0, tm=640, tn=1024, tk=1408):
    m, k_dim = a.shape
    n = b.shape[0] if trans_b else b.shape[1]
    tm, tn, tk = _tile(m, tm), _tile(n, tn), _tile(k_dim, tk)
    nk = k_dim // tk
    has_add = add is not None
    dims = NT_DIMS if trans_b else NN_DIMS

    def body(*refs):
        a_ref, b_ref = refs[0], refs[1]
        r_ref = refs[2] if has_add else None
        o_ref = refs[3] if has_add else refs[2]
        part = lax.dot_general(a_ref[...].astype(BF16), b_ref[...].astype(BF16), dims, preferred_element_type=F32)

        def finish(r):
            if has_add:
                r = r + add_scale * r_ref[...].astype(F32)
            o_ref[...] = r.astype(out_dtype)

        if nk == 1:
            finish(part)
        else:
            acc = refs[-1]
            kk = pl.program_id(2)

            @pl.when(kk == 0)
            def _():
                acc[...] = part

            @pl.when(kk > 0)
            def _():
                acc[...] += part

            @pl.when(kk == nk - 1)
            def _():
                finish(acc[...])

    in_specs = [pl.BlockSpec((tm, tk), lambda i, j, kk: (i, kk)),
                pl.BlockSpec((tn, tk), lambda i, j, kk: (j, kk)) if trans_b
                else pl.BlockSpec((tk, tn), lambda i, j, kk: (kk, j))]
    args = [a, b]
    if has_add:
        in_specs.append(pl.BlockSpec((tm, tn), lambda i, j, kk: (i, j)))
        args.append(add)
    return pl.pallas_call(
        body, name=name, grid=(m // tm, n // tn, nk), in_specs=in_specs,
        out_specs=pl.BlockSpec((tm, tn), lambda i, j, kk: (i, j)),
        out_shape=_sds((m, n), out_dtype),
        scratch_shapes=[pltpu.VMEM((tm, tn), F32)] if nk > 1 else [],
        compiler_params=_cp("parallel", "parallel", "arbitrary"),
    )(*args)


def _mm_tn(a, b, *, name, tko=1408, tn=1024, tt=640):
    t, k_dim = a.shape
    n = b.shape[1]
    tko, tn, tt = _tile(k_dim, tko), _tile(n, tn), _tile(t, tt)

    def body(a_ref, b_ref, o_ref):
        part = lax.dot_general(a_ref[...].astype(BF16), b_ref[...].astype(BF16), TN_DIMS, preferred_element_type=F32)
        tt_i = pl.program_id(2)

        @pl.when(tt_i == 0)
        def _():
            o_ref[...] = part

        @pl.when(tt_i > 0)
        def _():
            o_ref[...] += part

    return pl.pallas_call(
        body, name=name, grid=(k_dim // tko, n // tn, t // tt),
        in_specs=[pl.BlockSpec((tt, tko), lambda i, j, s: (s, i)), pl.BlockSpec((tt, tn), lambda i, j, s: (s, j))],
        out_specs=pl.BlockSpec((tko, tn), lambda i, j, s: (i, j)),
        out_shape=_sds((k_dim, n), F32),
        compiler_params=_cp("parallel", "parallel", "arbitrary"),
    )(a, b)


def _ln_fwd(h, branch, g, b, *, name):
    t, d = h.shape
    tr = _tile(t, 640)
    has_branch = branch is not None

    def body(*refs):
        if has_branch:
            h_ref, br_ref, g_ref, b_ref, pre_ref, o_ref = refs
            pre = ALPHA * h_ref[...] + br_ref[...]
            pre_ref[...] = pre
        else:
            h_ref, g_ref, b_ref, o_ref = refs
            pre = h_ref[...]
        mu = jnp.mean(pre, axis=1, keepdims=True)
        xc = pre - mu
        var = jnp.mean(xc * xc, axis=1, keepdims=True)
        y = xc * lax.rsqrt(var + LN_EPS) * g_ref[...] + b_ref[...]
        rows = _row_ids(pl.program_id(0), tr, (tr, 1))
        o_ref[...] = jnp.where(rows >= PAD, y, 0.0)

    row_spec = pl.BlockSpec((tr, d), lambda i: (i, 0))
    vec_spec = pl.BlockSpec((1, d), lambda i: (0, 0))
    if has_branch:
        return pl.pallas_call(
            body, name=name, grid=(t // tr,), in_specs=[row_spec, row_spec, vec_spec, vec_spec],
            out_specs=[row_spec, row_spec], out_shape=[_sds((t, d), F32), _sds((t, d), F32)],
            compiler_params=_cp("parallel"))(h, branch, g, b)
    out = pl.pallas_call(
        body, name=name, grid=(t // tr,), in_specs=[row_spec, vec_spec, vec_spec],
        out_specs=row_spec, out_shape=_sds((t, d), F32), compiler_params=_cp("parallel"))(h, g, b)
    return h, out


def _ln_bwd(dy, pre, g, *, name):
    t, d = pre.shape
    tr = _tile(t, 640)

    def body(dy_ref, pre_ref, g_ref, dpre_ref, dg_ref, db_ref):
        i = pl.program_id(0)
        pre_v = pre_ref[...]
        mu = jnp.mean(pre_v, axis=1, keepdims=True)
        xc = pre_v - mu
        var = jnp.mean(xc * xc, axis=1, keepdims=True)
        rstd = lax.rsqrt(var + LN_EPS)
        xhat = xc * rstd
        rows = _row_ids(i, tr, (tr, 1))
        dym = jnp.where(rows >= PAD, dy_ref[...], 0.0)
        gdy = dym * g_ref[...]
        m1 = jnp.mean(gdy, axis=1, keepdims=True)
        m2 = jnp.mean(gdy * xhat, axis=1, keepdims=True)
        dpre_ref[...] = rstd * (gdy - m1 - xhat * m2)
        pg = jnp.sum(dym * xhat, axis=0, keepdims=True)
        pb = jnp.sum(dym, axis=0, keepdims=True)

        @pl.when(i == 0)
        def _():
            dg_ref[...] = pg
            db_ref[...] = pb

        @pl.when(i > 0)
        def _():
            dg_ref[...] += pg
            db_ref[...] += pb

    row_spec = pl.BlockSpec((tr, d), lambda i: (i, 0))
    vec_spec = pl.BlockSpec((1, d), lambda i: (0, 0))
    return pl.pallas_call(
        body, name=name, grid=(t // tr,), in_specs=[row_spec, row_spec, vec_spec],
        out_specs=[row_spec, vec_spec, vec_spec],
        out_shape=[_sds((t, d), F32), _sds((1, d), F32), _sds((1, d), F32)],
        compiler_params=_cp("arbitrary"))(dy, pre, g)


def _rms_fwd(proj, col_off, width, g, *, name):
    t = proj.shape[0]
    tr = _tile(t, 640)
    cb = col_off // width

    def body(x_ref, g_ref, o_ref):
        x = x_ref[...]
        r = lax.rsqrt(jnp.mean(x * x, axis=1, keepdims=True) + RMS_EPS)
        o_ref[...] = (x * r * g_ref[...]).astype(BF16)

    return pl.pallas_call(
        body, name=name, grid=(t // tr,),
        in_specs=[pl.BlockSpec((tr, width), lambda i: (i, cb)), pl.BlockSpec((1, width), lambda i: (0, 0))],
        out_specs=pl.BlockSpec((tr, width), lambda i: (i, 0)), out_shape=_sds((t, width), BF16),
        compiler_params=_cp("parallel"))(proj, g)


def _rms_bwd(dy, proj, col_off, width, g, *, name):
    t = proj.shape[0]
    tr = _tile(t, 640)
    cb = col_off // width

    def body(dy_ref, x_ref, g_ref, dx_ref, dg_ref):
        i = pl.program_id(0)
        x = x_ref[...]
        dyv = dy_ref[...].astype(F32)
        r = lax.rsqrt(jnp.mean(x * x, axis=1, keepdims=True) + RMS_EPS)
        gdy = dyv * g_ref[...]
        m = jnp.mean(x * gdy, axis=1, keepdims=True)
        dx_ref[...] = (r * gdy - x * (r * r * r) * m).astype(BF16)
        pg = jnp.sum(dyv * x * r, axis=0, keepdims=True)

        @pl.when(i == 0)
        def _():
            dg_ref[...] = pg

        @pl.when(i > 0)
        def _():
            dg_ref[...] += pg

    return pl.pallas_call(
        body, name=name, grid=(t // tr,),
        in_specs=[pl.BlockSpec((tr, width), lambda i: (i, 0)), pl.BlockSpec((tr, width), lambda i: (i, cb)),
                  pl.BlockSpec((1, width), lambda i: (0, 0))],
        out_specs=[pl.BlockSpec((tr, width), lambda i: (i, 0)), pl.BlockSpec((1, width), lambda i: (0, 0))],
        out_shape=[_sds((t, width), BF16), _sds((1, width), F32)],
        compiler_params=_cp("arbitrary"))(dy, proj, g)


def _rope_apply(r, cos, sin_a, sin_b):
    return r * cos + pltpu.roll(r, 96, 1) * sin_a + pltpu.roll(r, 32, 1) * sin_b


def _rope_apply_t(dr, cos, sin_a, sin_b):
    return dr * cos + pltpu.roll(dr * sin_a, 32, 1) + pltpu.roll(dr * sin_b, 96, 1)


def _rope_q_fwd(q, cos, sin_a, sin_b, *, name):
    t, w = q.shape
    tr = _tile(t, 128)

    def body(q_ref, c_ref, sa_ref, sb_ref, o_ref):
        c, sa, sb = c_ref[...], sa_ref[...], sb_ref[...]
        for h in range(MLA_HEADS):
            base = h * QHEAD
            o_ref[:, base:base + LANES] = (q_ref[:, base:base + LANES] * ATTN_SCALE).astype(BF16)
            rot = _rope_apply(q_ref[:, base + LANES:base + QHEAD], c, sa, sb)
            o_ref[:, base + LANES:base + QHEAD] = (rot * ATTN_SCALE).astype(BF16)

    tab = pl.BlockSpec((tr, LANES), lambda i: (i, 0))
    row = pl.BlockSpec((tr, w), lambda i: (i, 0))
    return pl.pallas_call(body, name=name, grid=(t // tr,), in_specs=[row, tab, tab, tab], out_specs=row,
                          out_shape=_sds((t, w), BF16), compiler_params=_cp("parallel"))(q, cos, sin_a, sin_b)


def _rope_q_bwd(dq, cos, sin_a, sin_b, *, name):
    t, w = dq.shape
    tr = _tile(t, 128)

    def body(dq_ref, c_ref, sa_ref, sb_ref, o_ref):
        c, sa, sb = c_ref[...], sa_ref[...], sb_ref[...]
        for h in range(MLA_HEADS):
            base = h * QHEAD
            o_ref[:, base:base + LANES] = (dq_ref[:, base:base + LANES] * ATTN_SCALE).astype(BF16)
            d_rot = _rope_apply_t(dq_ref[:, base + LANES:base + QHEAD], c, sa, sb)
            o_ref[:, base + LANES:base + QHEAD] = (d_rot * ATTN_SCALE).astype(BF16)

    tab = pl.BlockSpec((tr, LANES), lambda i: (i, 0))
    row = pl.BlockSpec((tr, w), lambda i: (i, 0))
    return pl.pallas_call(body, name=name, grid=(t // tr,), in_specs=[row, tab, tab, tab], out_specs=row,
                          out_shape=_sds((t, w), BF16), compiler_params=_cp("parallel"))(dq, cos, sin_a, sin_b)


def _rope_k_fwd(proj, cos, sin_a, sin_b, *, name):
    t = proj.shape[0]
    tr = _tile(t, 640)
    cb = OFF_KPE // LANES

    def body(x_ref, c_ref, sa_ref, sb_ref, o_ref):
        o_ref[...] = _rope_apply(x_ref[...], c_ref[...], sa_ref[...], sb_ref[...]).astype(BF16)

    tab = pl.BlockSpec((tr, LANES), lambda i: (i, 0))
    return pl.pallas_call(body, name=name, grid=(t // tr,),
                          in_specs=[pl.BlockSpec((tr, LANES), lambda i: (i, cb)), tab, tab, tab], out_specs=tab,
                          out_shape=_sds((t, LANES), BF16), compiler_params=_cp("parallel"))(proj, cos, sin_a, sin_b)


def _rope_k_bwd(dkp, cos, sin_a, sin_b, *, name):
    nh, t, _ = dkp.shape
    tr = _tile(t, 640)

    def body(d_ref, c_ref, sa_ref, sb_ref, o_ref):
        tot = d_ref[0]
        for h in range(1, nh):
            tot = tot + d_ref[h]
        o_ref[...] = _rope_apply_t(tot, c_ref[...], sa_ref[...], sb_ref[...]).astype(BF16)

    tab = pl.BlockSpec((tr, LANES), lambda i: (i, 0))
    return pl.pallas_call(body, name=name, grid=(t // tr,),
                          in_specs=[pl.BlockSpec((nh, tr, LANES), lambda i: (0, i, 0)), tab, tab, tab], out_specs=tab,
                          out_shape=_sds((t, LANES), BF16), compiler_params=_cp("parallel"))(dkp, cos, sin_a, sin_b)


def _visible(i, j, tb):
    row = i * tb + lax.broadcasted_iota(jnp.int32, (tb, tb), 0)
    col = j * tb + lax.broadcasted_iota(jnp.int32, (tb, tb), 1)
    return (col <= row) & (col >= PAD)


def _flash_fwd(q, kv, kpe, *, name):
    t = q.shape[0]
    nh = MLA_HEADS
    tb = _tile(t, 640)
    nb = t // tb

    def body(q_ref, kn_ref, v_ref, kp_ref, o_ref, lse_ref, m_s, l_s, acc_s):
        i, j = pl.program_id(1), pl.program_id(2)

        @pl.when(j == 0)
        def _():
            m_s[...] = jnp.full((tb, 1), NEG_INF, F32)
            l_s[...] = jnp.zeros((tb, 1), F32)
            acc_s[...] = jnp.zeros((tb, V_HEAD), F32)

        def step(masked):
            k = jnp.concatenate([kn_ref[...], kp_ref[...]], axis=1)
            s = lax.dot_general(q_ref[...], k, NT_DIMS, preferred_element_type=F32)
            if masked:
                s = jnp.where(_visible(i, j, tb), s, NEG_INF)
            m_prev = m_s[...]
            m_new = jnp.maximum(m_prev, jnp.max(s, axis=1, keepdims=True))
            p = jnp.exp(s - m_new)
            corr = jnp.exp(m_prev - m_new)
            l_s[...] = corr * l_s[...] + jnp.sum(p, axis=1, keepdims=True)
            acc_s[...] = corr * acc_s[...] + jnp.dot(p.astype(BF16), v_ref[...], preferred_element_type=F32)
            m_s[...] = m_new

        edge = (j == i) | (j == 0)

        @pl.when((j <= i) & edge)
        def _():
            step(True)

        @pl.when((j < i) & jnp.logical_not(edge))
        def _():
            step(False)

        @pl.when(j == i)
        def _():
            l = l_s[...]
            o_ref[...] = (acc_s[...] / l).astype(BF16)
            lse_ref[0] = m_s[...] + jnp.log(l)

    return pl.pallas_call(
        body, name=name, grid=(nh, nb, nb),
        in_specs=[pl.BlockSpec((tb, QHEAD), lambda h, i, j: (i, h)),
                  pl.BlockSpec((tb, LANES), lambda h, i, j: (jnp.minimum(j, i), h)),
                  pl.BlockSpec((tb, LANES), lambda h, i, j: (jnp.minimum(j, i), nh + h)),
                  pl.BlockSpec((tb, LANES), lambda h, i, j: (jnp.minimum(j, i), 0))],
        out_specs=[pl.BlockSpec((tb, V_HEAD), lambda h, i, j: (i, h)),
                   pl.BlockSpec((1, tb, 1), lambda h, i, j: (h, i, 0))],
        out_shape=[_sds((t, nh * V_HEAD), BF16), _sds((nh, t, 1), F32)],
        scratch_shapes=[pltpu.VMEM((tb, 1), F32), pltpu.VMEM((tb, 1), F32), pltpu.VMEM((tb, V_HEAD), F32)],
        compiler_params=_cp("parallel", "parallel", "arbitrary"))(q, kv, kv, kpe)


def _attn_delta(do, o, *, name):
    t = o.shape[0]
    nh = MLA_HEADS
    tr = _tile(t, 640)

    def body(do_ref, o_ref, d_ref):
        d_ref[0] = jnp.sum(do_ref[...].astype(F32) * o_ref[...].astype(F32), axis=1, keepdims=True)

    blk = pl.BlockSpec((tr, V_HEAD), lambda h, i: (i, h))
    return pl.pallas_call(body, name=name, grid=(nh, t // tr), in_specs=[blk, blk],
                          out_specs=pl.BlockSpec((1, tr, 1), lambda h, i: (h, i, 0)),
                          out_shape=_sds((nh, t, 1), F32), compiler_params=_cp("parallel", "parallel"))(do, o)


def _flash_bwd(q, kv, kpe, do, lse, delta, *, name):
    t = q.shape[0]
    nh = MLA_HEADS
    tb = _tile(t, 640)
    nb = t // tb

    def body(q_ref, kn_ref, v_ref, kp_ref, do_ref, lse_ref, dl_ref, dq_ref, dkn_ref, dkp_ref, dv_ref, dk_s, dv_s):
        j, i = pl.program_id(1), pl.program_id(2)

        @pl.when((j == 0) & (i == 0))
        def _():
            dq_ref[...] = jnp.zeros((t, QHEAD), F32)

        @pl.when(i == 0)
        def _():
            dk_s[...] = jnp.zeros((tb, QHEAD), F32)
            dv_s[...] = jnp.zeros((tb, V_HEAD), F32)

        def step(masked):
            qv = q_ref[...]
            k = jnp.concatenate([kn_ref[...], kp_ref[...]], axis=1)
            s = lax.dot_general(qv, k, NT_DIMS, preferred_element_type=F32)
            if masked:
                s = jnp.where(_visible(i, j, tb), s, NEG_INF)
            p = jnp.exp(s - lse_ref[0])
            dov = do_ref[...]
            dv_s[...] += lax.dot_general(p.astype(BF16), dov, TN_DIMS, preferred_element_type=F32)
            dp = lax.dot_general(dov, v_ref[...], NT_DIMS, preferred_element_type=F32)
            ds = (p * (dp - dl_ref[0])).astype(BF16)
            dk_s[...] += lax.dot_general(ds, qv, TN_DIMS, preferred_element_type=F32)
            r0 = pl.multiple_of(i * tb, tb)
            dq_ref[pl.ds(r0, tb), :] += jnp.dot(ds, k, preferred_element_type=F32)

        edge = (j == i) | (j == 0)

        @pl.when((i >= j) & edge)
        def _():
            step(True)

        @pl.when((i > j) & jnp.logical_not(edge))
        def _():
            step(False)

        @pl.when(i == nb - 1)
        def _():
            dkn_ref[...] = dk_s[:, :LANES].astype(BF16)
            dkp_ref[0] = dk_s[:, LANES:]
            dv_ref[...] = dv_s[...].astype(BF16)

    qi = lambda h, j, i: (jnp.maximum(i, j), h)
    return pl.pallas_call(
        body, name=name, grid=(nh, nb, nb),
        in_specs=[pl.BlockSpec((tb, QHEAD), qi),
                  pl.BlockSpec((tb, LANES), lambda h, j, i: (j, h)),
                  pl.BlockSpec((tb, LANES), lambda h, j, i: (j, nh + h)),
                  pl.BlockSpec((tb, LANES), lambda h, j, i: (j, 0)),
                  pl.BlockSpec((tb, V_HEAD), qi),
                  pl.BlockSpec((1, tb, 1), lambda h, j, i: (h, jnp.maximum(i, j), 0)),
                  pl.BlockSpec((1, tb, 1), lambda h, j, i: (h, jnp.maximum(i, j), 0))],
        out_specs=[pl.BlockSpec((t, QHEAD), lambda h, j, i: (0, h)),
                   pl.BlockSpec((tb, LANES), lambda h, j, i: (j, h)),
                   pl.BlockSpec((1, tb, LANES), lambda h, j, i: (h, j, 0)),
                   pl.BlockSpec((tb, V_HEAD), lambda h, j, i: (j, h))],
        out_shape=[_sds((t, nh * QHEAD), F32), _sds((t, nh * LANES), BF16), _sds((nh, t, LANES), F32),
                   _sds((t, nh * V_HEAD), BF16)],
        scratch_shapes=[pltpu.VMEM((tb, QHEAD), F32), pltpu.VMEM((tb, V_HEAD), F32)],
        compiler_params=_cp("arbitrary", "arbitrary", "arbitrary"))(q, kv, kv, kpe, do, lse, delta)


def _fill_prev(buf, x_ref, halo_ref, i, tr):
    buf[pl.ds(0, HALO), :] = jnp.where(i > 0, halo_ref[...], 0.0)
    buf[pl.ds(HALO, tr), :] = x_ref[...]


def _conv_prev(buf, w_ref, kw, tr):
    acc = w_ref[kw - 1:kw, :] * buf[pl.ds(HALO, tr), :]
    for k in range(kw - 1):
        acc = acc + w_ref[k:k + 1, :] * buf[pl.ds(HALO - kw + 1 + k, tr), :]
    return acc


def _conv_dw(buf, dc, kw, tr):
    rows = [jnp.sum(dc * buf[pl.ds(HALO - kw + 1 + k, tr), :], axis=0, keepdims=True) for k in range(kw)]
    return jnp.concatenate(rows, axis=0)


def _conv_next(buf, dc_ref, halo_ref, w_ref, kw, i, n_tiles, tr):
    buf[pl.ds(0, tr), :] = dc_ref[...]
    buf[pl.ds(tr, HALO), :] = jnp.where(i < n_tiles - 1, halo_ref[...], 0.0)
    acc = w_ref[kw - 1:kw, :] * buf[pl.ds(0, tr), :]
    for k in range(kw - 1):
        acc = acc + w_ref[k:k + 1, :] * buf[pl.ds(kw - 1 - k, tr), :]
    return acc


def _split3(x):
    x1 = x.astype(BF16)
    r1 = x - x1.astype(F32)
    x2 = r1.astype(BF16)
    x3 = (r1 - x2.astype(F32)).astype(BF16)
    return x1, x2, x3


def _dot3(parts, m, left):
    tot = None
    for p in parts:
        r = jnp.dot(m, p, preferred_element_type=F32) if left else jnp.dot(p, m, preferred_element_type=F32)
        tot = r if tot is None else tot + r
    return tot


def _ssd_prep_fwd(proj, conv_w, conv_b, dt_bias, expand, *, name):
    t = proj.shape[0]
    tr = _tile(t, 128)
    nt = t // tr
    hb = tr // HALO
    cw = SSD_CONV_DIM
    cb_x = OFF_XBC // cw
    cb_dt = OFF_DT // LANES

    def body(x_ref, halo_ref, dtr_ref, w_ref, b_ref, dtb_ref, e_ref, xs_ref, bm_ref, cm_ref, dtx_ref, buf):
        i = pl.program_id(0)
        _fill_prev(buf, x_ref, halo_ref, i, tr)
        conv = _conv_prev(buf, w_ref, SSD_CONV, tr) + b_ref[...]
        rows = _row_ids(i, tr, (tr, 1))
        live = rows >= PAD
        act = jnp.where(live, conv * _sigmoid(conv), 0.0)
        xs_ref[...] = act[:, :SSD_INNER]
        bm_ref[...] = act[:, SSD_INNER:SSD_INNER + GROUP_W]
        cm_ref[...] = act[:, SSD_INNER + GROUP_W:]
        dt = jnp.where(live, jax.nn.softplus(dtr_ref[...] + dtb_ref[...]), 0.0)
        dtx_ref[...] = _dot3(_split3(dt), e_ref[...], left=False)

    return pl.pallas_call(
        body, name=name, grid=(nt,),
        in_specs=[pl.BlockSpec((tr, cw), lambda i: (i, cb_x)),
                  pl.BlockSpec((HALO, cw), lambda i: (jnp.maximum(i * hb - 1, 0), cb_x)),
                  pl.BlockSpec((tr, LANES), lambda i: (i, cb_dt)),
                  pl.BlockSpec((SSD_CONV, cw), lambda i: (0, 0)),
                  pl.BlockSpec((1, cw), lambda i: (0, 0)),
                  pl.BlockSpec((1, LANES), lambda i: (0, 0)),
                  pl.BlockSpec((LANES, SSD_INNER), lambda i: (0, 0))],
        out_specs=[pl.BlockSpec((tr, SSD_INNER), lambda i: (i, 0)), pl.BlockSpec((tr, GROUP_W), lambda i: (i, 0)),
                   pl.BlockSpec((tr, GROUP_W), lambda i: (i, 0)), pl.BlockSpec((tr, SSD_INNER), lambda i: (i, 0))],
        out_shape=[_sds((t, SSD_INNER), F32), _sds((t, GROUP_W), F32), _sds((t, GROUP_W), F32),
                   _sds((t, SSD_INNER), F32)],
        scratch_shapes=[pltpu.VMEM((tr + HALO, cw), F32)],
        compiler_params=_cp("parallel"))(proj, proj, proj, conv_w, conv_b, dt_bias, expand)


def _ssd_prep_bwd_a(proj, dxs, dbm, dcm, ddtx, conv_w, conv_b, dt_bias, reduce_m, *, name):
    t = proj.shape[0]
    tr = _tile(t, 128)
    nt = t // tr
    hb = tr // HALO
    cw = SSD_CONV_DIM
    cb_x = OFF_XBC // cw
    cb_dt = OFF_DT // LANES

    def body(x_ref, halo_ref, dtr_ref, dxs_ref, dbm_ref, dcm_ref, ddtx_ref, w_ref, b_ref, dtb_ref, r_ref,
             dconv_ref, ddtr_ref, dw_ref, db_ref, ddtb_ref, buf):
        i = pl.program_id(0)
        _fill_prev(buf, x_ref, halo_ref, i, tr)
        conv = _conv_prev(buf, w_ref, SSD_CONV, tr) + b_ref[...]
        rows = _row_ids(i, tr, (tr, 1))
        live = rows >= PAD
        sg = _sigmoid(conv)
        dact = jnp.concatenate([dxs_ref[...], dbm_ref[...], dcm_ref[...]], axis=1)
        dconv = jnp.where(live, dact * (sg * (1.0 + conv * (1.0 - sg))), 0.0)
        dconv_ref[...] = dconv
        pw = _conv_dw(buf, dconv, SSD_CONV, tr)
        pb = jnp.sum(dconv, axis=0, keepdims=True)
        ddt = _dot3(_split3(ddtx_ref[...]), r_ref[...], left=False)
        ddtr = jnp.where(live, ddt * _sigmoid(dtr_ref[...] + dtb_ref[...]), 0.0)
        ddtr_ref[...] = ddtr.astype(BF16)
        pdb = jnp.sum(ddtr, axis=0, keepdims=True)

        @pl.when(i == 0)
        def _():
            dw_ref[...] = pw
            db_ref[...] = pb
            ddtb_ref[...] = pdb

        @pl.when(i > 0)
        def _():
            dw_ref[...] += pw
            db_ref[...] += pb
            ddtb_ref[...] += pdb

    return pl.pallas_call(
        body, name=name, grid=(nt,),
        in_specs=[pl.BlockSpec((tr, cw), lambda i: (i, cb_x)),
                  pl.BlockSpec((HALO, cw), lambda i: (jnp.maximum(i * hb - 1, 0), cb_x)),
                  pl.BlockSpec((tr, LANES), lambda i: (i, cb_dt)),
                  pl.BlockSpec((tr, SSD_INNER), lambda i: (i, 0)),
                  pl.BlockSpec((tr, GROUP_W), lambda i: (i, 0)),
                  pl.BlockSpec((tr, GROUP_W), lambda i: (i, 0)),
                  pl.BlockSpec((tr, SSD_INNER), lambda i: (i, 0)),
                  pl.BlockSpec((SSD_CONV, cw), lambda i: (0, 0)),
                  pl.BlockSpec((1, cw), lambda i: (0, 0)),
                  pl.BlockSpec((1, LANES), lambda i: (0, 0)),
                  pl.BlockSpec((SSD_INNER, LANES), lambda i: (0, 0))],
        out_specs=[pl.BlockSpec((tr, cw), lambda i: (i, 0)), pl.BlockSpec((tr, LANES), lambda i: (i, 0)),
                   pl.BlockSpec((SSD_CONV, cw), lambda i: (0, 0)), pl.BlockSpec((1, cw), lambda i: (0, 0)),
                   pl.BlockSpec((1, LANES), lambda i: (0, 0))],
        out_shape=[_sds((t, cw), F32), _sds((t, LANES), BF16), _sds((SSD_CONV, cw), F32), _sds((1, cw), F32),
                   _sds((1, LANES), F32)],
        scratch_shapes=[pltpu.VMEM((tr + HALO, cw), F32)],
        compiler_params=_cp("arbitrary"))(proj, proj, proj, dxs, dbm, dcm, ddtx, conv_w, conv_b, dt_bias, reduce_m)


def _conv_bwd_input(dconv, w, kw, *, name, out_dtype=BF16, tc=None):
    t, c = dconv.shape
    tr = _tile(t, 128)
    nt = t // tr
    hb = tr // HALO
    tc = _tile(c, tc or c)
    last_hb = t // HALO - 1

    def body(dc_ref, halo_ref, w_ref, o_ref, buf):
        i = pl.program_id(0)
        o_ref[...] = _conv_next(buf, dc_ref, halo_ref, w_ref, kw, i, nt, tr).astype(out_dtype)

    return pl.pallas_call(
        body, name=name, grid=(nt, c // tc),
        in_specs=[pl.BlockSpec((tr, tc), lambda i, j: (i, j)),
                  pl.BlockSpec((HALO, tc), lambda i, j: (jnp.minimum((i + 1) * hb, last_hb), j)),
                  pl.BlockSpec((kw, tc), lambda i, j: (0, j))],
        out_specs=pl.BlockSpec((tr, tc), lambda i, j: (i, j)), out_shape=_sds((t, c), out_dtype),
        scratch_shapes=[pltpu.VMEM((tr + HALO, tc), F32)],
        compiler_params=_cp("parallel", "parallel"))(dconv, dconv, w)


def _ffn_act_fwd(ug, uv, wg, wv, bg, bv, *, name):
    t, c = ug.shape
    tr = _tile(t, 128)
    hb = tr // HALO
    tc = _tile(c, 1408)

    def body(ug_ref, hg_ref, uv_ref, hv_ref, wg_ref, wv_ref, bg_ref, bv_ref, o_ref, bufg, bufv):
        i = pl.program_id(0)
        _fill_prev(bufg, ug_ref, hg_ref, i, tr)
        _fill_prev(bufv, uv_ref, hv_ref, i, tr)
        cg = _conv_prev(bufg, wg_ref, FFN_CONV, tr) + bg_ref[...]
        cv = _conv_prev(bufv, wv_ref, FFN_CONV, tr) + bv_ref[...]
        o_ref[...] = (cg * _sigmoid(cg) * cv).astype(BF16)

    blk = pl.BlockSpec((tr, tc), lambda i, j: (i, j))
    halo = pl.BlockSpec((HALO, tc), lambda i, j: (jnp.maximum(i * hb - 1, 0), j))
    wsp = pl.BlockSpec((FFN_CONV, tc), lambda i, j: (0, j))
    bsp = pl.BlockSpec((1, tc), lambda i, j: (0, j))
    return pl.pallas_call(
        body, name=name, grid=(t // tr, c // tc), in_specs=[blk, halo, blk, halo, wsp, wsp, bsp, bsp],
        out_specs=blk, out_shape=_sds((t, c), BF16),
        scratch_shapes=[pltpu.VMEM((tr + HALO, tc), F32), pltpu.VMEM((tr + HALO, tc), F32)],
        compiler_params=_cp("parallel", "parallel"))(ug, ug, uv, uv, wg, wv, bg, bv)


def _ffn_act_bwd(ug, uv, dact, wg, wv, bg, bv, *, name):
    t, c = ug.shape
    tr = _tile(t, 128)
    hb = tr // HALO
    tc = _tile(c, 1408)

    def body(ug_ref, hg_ref, uv_ref, hv_ref, da_ref, wg_ref, wv_ref, bg_ref, bv_ref,
             dcg_ref, dcv_ref, dwg_ref, dwv_ref, dbg_ref, dbv_ref, bufg, bufv):
        i = pl.program_id(1)
        _fill_prev(bufg, ug_ref, hg_ref, i, tr)
        _fill_prev(bufv, uv_ref, hv_ref, i, tr)
        cg = _conv_prev(bufg, wg_ref, FFN_CONV, tr) + bg_ref[...]
        cv = _conv_prev(bufv, wv_ref, FFN_CONV, tr) + bv_ref[...]
        sg = _sigmoid(cg)
        da = da_ref[...]
        dcg = da * cv * (sg * (1.0 + cg * (1.0 - sg)))
        dcv = da * (cg * sg)
        dcg_ref[...] = dcg
        dcv_ref[...] = dcv
        pwg = _conv_dw(bufg, dcg, FFN_CONV, tr)
        pwv = _conv_dw(bufv, dcv, FFN_CONV, tr)
        pbg = jnp.sum(dcg, axis=0, keepdims=True)
        pbv = jnp.sum(dcv, axis=0, keepdims=True)

        @pl.when(i == 0)
        def _():
            dwg_ref[...] = pwg
            dwv_ref[...] = pwv
            dbg_ref[...] = pbg
            dbv_ref[...] = pbv

        @pl.when(i > 0)
        def _():
            dwg_ref[...] += pwg
            dwv_ref[...] += pwv
            dbg_ref[...] += pbg
            dbv_ref[...] += pbv

    blk = pl.BlockSpec((tr, tc), lambda j, i: (i, j))
    halo = pl.BlockSpec((HALO, tc), lambda j, i: (jnp.maximum(i * hb - 1, 0), j))
    wsp = pl.BlockSpec((FFN_CONV, tc), lambda j, i: (0, j))
    bsp = pl.BlockSpec((1, tc), lambda j, i: (0, j))
    return pl.pallas_call(
        body, name=name, grid=(c // tc, t // tr), in_specs=[blk, halo, blk, halo, blk, wsp, wsp, bsp, bsp],
        out_specs=[blk, blk, wsp, wsp, bsp, bsp],
        out_shape=[_sds((t, c), F32), _sds((t, c), F32), _sds((FFN_CONV, c), F32), _sds((FFN_CONV, c), F32),
                   _sds((1, c), F32), _sds((1, c), F32)],
        scratch_shapes=[pltpu.VMEM((tr + HALO, tc), F32), pltpu.VMEM((tr + HALO, tc), F32)],
        compiler_params=_cp("parallel", "arbitrary"))(ug, ug, uv, uv, dact, wg, wv, bg, bv)


def _tri(lower):
    li = lax.broadcasted_iota(jnp.int32, (CHUNK, CHUNK), 0)
    si = lax.broadcasted_iota(jnp.int32, (CHUNK, CHUNK), 1)
    return li >= si if lower else li <= si


def _tri_ones(lower):
    return jnp.where(_tri(lower), 1.0, 0.0).astype(BF16)


def _decay_pair(acs, acs_t, lane0):
    col = acs[:, lane0:lane0 + 1]
    row = acs_t[lane0:lane0 + 1, :]
    low = jnp.where(_tri(True), jnp.exp(jnp.minimum(col - row, 0.0)), 0.0)
    upp = jnp.where(_tri(False), jnp.exp(jnp.minimum(row - col, 0.0)), 0.0)
    return low, upp


def _ssd_fwd(xs, dtx, bm, cm, bm_t, a_x, d_x, *, name):
    t = xs.shape[0]
    nc = t // CHUNK
    gw = GROUP_W

    def body(xs_ref, dt_ref, b_ref, c_ref, bt_ref, a_ref, d_ref, y_ref, prev_ref, h_s):
        @pl.when(pl.program_id(1) == 0)
        def _():
            h_s[...] = jnp.zeros((SSD_STATE, gw), F32)

        x = xs_ref[...]
        dt = dt_ref[...]
        acs = _dot3(_split3(dt * a_ref[...]), _tri_ones(True), left=True)
        acs_t = acs.T
        xc = x * dt
        bv = b_ref[...].astype(BF16)
        cv = c_ref[...].astype(BF16)
        cb = lax.dot_general(cv, bv, NT_DIMS, preferred_element_type=F32)
        lane = lax.broadcasted_iota(jnp.int32, (CHUNK, LANES), 1)
        pieces = []
        for pp in range(gw // LANES):
            xcp = xc[:, pp * LANES:(pp + 1) * LANES]
            acc = jnp.zeros((CHUNK, LANES), F32)
            for e in range(2):
                low, _ = _decay_pair(acs, acs_t, pp * LANES + e * SSD_HEAD_DIM)
                mine = (lane >= e * SSD_HEAD_DIM) & (lane < (e + 1) * SSD_HEAD_DIM)
                xm = jnp.where(mine, xcp, 0.0).astype(BF16)
                acc = acc + jnp.dot((cb * low).astype(BF16), xm, preferred_element_type=F32)
            pieces.append(acc)
        y_diag = jnp.concatenate(pieces, axis=1)
        h_prev = h_s[...]
        y_off = jnp.dot(cv, h_prev.astype(BF16), preferred_element_type=F32) * jnp.exp(acs)
        y_ref[...] = y_diag + y_off + d_ref[...] * x
        prev_ref[0] = h_prev
        last = acs[CHUNK - 1:CHUNK, :]
        w = jnp.exp(last - acs)
        st = jnp.dot(bt_ref[...].astype(BF16), (xc * w).astype(BF16), preferred_element_type=F32)
        h_s[...] = h_prev * jnp.exp(last) + st

    tok = pl.BlockSpec((CHUNK, gw), lambda g, c: (c, g))
    grp = pl.BlockSpec((CHUNK, SSD_STATE), lambda g, c: (c, g))
    vec = pl.BlockSpec((1, gw), lambda g, c: (0, g))
    return pl.pallas_call(
        body, name=name, grid=(SSD_GROUPS, nc),
        in_specs=[tok, tok, grp, grp, pl.BlockSpec((SSD_STATE, CHUNK), lambda g, c: (g, c)), vec, vec],
        out_specs=[tok, pl.BlockSpec((1, SSD_STATE, gw), lambda g, c: (c, 0, g))],
        out_shape=[_sds((t, SSD_INNER), F32), _sds((nc, SSD_STATE, SSD_INNER), F32)],
        scratch_shapes=[pltpu.VMEM((SSD_STATE, gw), F32)],
        compiler_params=_cp("parallel", "arbitrary"))(xs, dtx, bm, cm, bm_t, a_x, d_x)


def _ssd_bwd(xs, dtx, bm, cm, cm_t, prev, dy, a_x, d_x, *, name):
    t = xs.shape[0]
    nc = t // CHUNK
    gw = GROUP_W

    def body(xs_ref, dt_ref, b_ref, c_ref, ct_ref, prev_ref, dy_ref, a_ref, d_ref,
             dxs_ref, ddt_ref, db_ref, dc_ref, da_ref, dd_ref, g_s):
        first = pl.program_id(1) == 0

        @pl.when(first)
        def _():
            g_s[...] = jnp.zeros((SSD_STATE, gw), F32)

        x = xs_ref[...]
        dt = dt_ref[...]
        a = a_ref[...]
        dyv = dy_ref[...]
        acs = _dot3(_split3(dt * a), _tri_ones(True), left=True)
        acs_t = acs.T
        xc = x * dt
        bv = b_ref[...].astype(BF16)
        cv = c_ref[...].astype(BF16)
        cb = lax.dot_general(cv, bv, NT_DIMS, preferred_element_type=F32)
        cb_t = lax.dot_general(bv, cv, NT_DIMS, preferred_element_type=F32)
        last = acs[CHUNK - 1:CHUNK, :]
        w = jnp.exp(last - acs)
        cd = jnp.exp(last)
        p_in = prev_ref[0]
        p_b = p_in.astype(BF16)
        g_out = g_s[...]
        g_b = g_out.astype(BF16)
        dy_e = dyv * jnp.exp(acs)
        dy_eb = dy_e.astype(BF16)
        y_off_raw = jnp.dot(cv, p_b, preferred_element_type=F32)
        dacs = dy_e * y_off_raw
        d_c = lax.dot_general(dy_eb, p_b, NT_DIMS, preferred_element_type=F32)
        d_prev = jnp.dot(ct_ref[...].astype(BF16), dy_eb, preferred_element_type=F32)
        q_l = jnp.dot(bv, g_b, preferred_element_type=F32)
        dxc = w * q_l
        tw = xc * q_l * w
        dacs = dacs - tw
        d_b = lax.dot_general((xc * w).astype(BF16), g_b, NT_DIMS, preferred_element_type=F32)
        last_add = jnp.sum(tw, axis=0, keepdims=True) + cd * jnp.sum(g_out * p_in, axis=0, keepdims=True)
        g_s[...] = cd * g_out + d_prev
        lane = lax.broadcasted_iota(jnp.int32, (CHUNK, LANES), 1)
        d_cb = jnp.zeros((CHUNK, CHUNK), F32)
        d_cb_t = jnp.zeros((CHUNK, CHUNK), F32)
        dxc_pieces, dacs_pieces = [], []
        for pp in range(gw // LANES):
            xcp = xc[:, pp * LANES:(pp + 1) * LANES]
            dyp = dyv[:, pp * LANES:(pp + 1) * LANES]
            dxcp = jnp.zeros((CHUNK, LANES), F32)
            dacsp = jnp.zeros((CHUNK, LANES), F32)
            for e in range(2):
                low, upp = _decay_pair(acs, acs_t, pp * LANES + e * SSD_HEAD_DIM)
                mine = (lane >= e * SSD_HEAD_DIM) & (lane < (e + 1) * SSD_HEAD_DIM)
                m_low = cb * low
                m_upp = cb_t * upp
                dym = jnp.where(mine, dyp, 0.0).astype(BF16)
                xm = jnp.where(mine, xcp, 0.0).astype(BF16)
                dxcp = dxcp + jnp.dot(m_upp.astype(BF16), dym, preferred_element_type=F32)
                d_m = lax.dot_general(dym, xm, NT_DIMS, preferred_element_type=F32)
                d_m_t = lax.dot_general(xm, dym, NT_DIMS, preferred_element_type=F32)
                rs = jnp.sum(d_m * m_low, axis=1, keepdims=True)
                cs = jnp.sum(d_m_t * m_upp, axis=1, keepdims=True)
                dacsp = dacsp + jnp.where(lane == e * SSD_HEAD_DIM, rs - cs, 0.0)
                d_cb = d_cb + d_m * low
                d_cb_t = d_cb_t + d_m_t * upp
            dxc_pieces.append(dxcp)
            dacs_pieces.append(dacsp)
        dxc = dxc + jnp.concatenate(dxc_pieces, axis=1)
        dacs = dacs + jnp.concatenate(dacs_pieces, axis=1)
        rowi = lax.broadcasted_iota(jnp.int32, (CHUNK, gw), 0)
        dacs = dacs + jnp.where(rowi == CHUNK - 1, last_add, 0.0)
        dc_ref[...] = d_c + jnp.dot(d_cb.astype(BF16), bv, preferred_element_type=F32)
        db_ref[...] = d_b + jnp.dot(d_cb_t.astype(BF16), cv, preferred_element_type=F32)
        dda = _dot3(_split3(dacs), _tri_ones(False), left=True)
        ddt_ref[...] = dda * a + dxc * x
        dxs_ref[...] = dxc * dt + d_ref[...] * dyv
        pa = jnp.sum(dda * dt, axis=0, keepdims=True)
        pd = jnp.sum(dyv * x, axis=0, keepdims=True)

        @pl.when(first)
        def _():
            da_ref[...] = pa
            dd_ref[...] = pd

        @pl.when(jnp.logical_not(first))
        def _():
            da_ref[...] += pa
            dd_ref[...] += pd

    rc = lambda c: nc - 1 - c
    tok = pl.BlockSpec((CHUNK, gw), lambda g, c: (rc(c), g))
    grp = pl.BlockSpec((CHUNK, SSD_STATE), lambda g, c: (rc(c), g))
    vec = pl.BlockSpec((1, gw), lambda g, c: (0, g))
    return pl.pallas_call(
        body, name=name, grid=(SSD_GROUPS, nc),
        in_specs=[tok, tok, grp, grp, pl.BlockSpec((SSD_STATE, CHUNK), lambda g, c: (g, rc(c))),
                  pl.BlockSpec((1, SSD_STATE, gw), lambda g, c: (rc(c), 0, g)), tok, vec, vec],
        out_specs=[tok, tok, grp, grp, vec, vec],
        out_shape=[_sds((t, SSD_INNER), F32), _sds((t, SSD_INNER), F32), _sds((t, gw), F32), _sds((t, gw), F32),
                   _sds((1, SSD_INNER), F32), _sds((1, SSD_INNER), F32)],
        scratch_shapes=[pltpu.VMEM((SSD_STATE, gw), F32)],
        compiler_params=_cp("parallel", "arbitrary"))(xs, dtx, bm, cm, cm_t, prev, dy, a_x, d_x)


def _gnorm_fwd(y, proj, g, *, name):
    t = y.shape[0]
    tr = _tile(t, 640)
    zb = OFF_Z // GROUP_W

    def body(y_ref, z_ref, g_ref, o_ref):
        z = z_ref[...]
        v = y_ref[...] * (z * _sigmoid(z))
        r = lax.rsqrt(jnp.mean(v * v, axis=1, keepdims=True) + RMS_EPS)
        o_ref[...] = (v * r * g_ref[...]).astype(BF16)

    blk = pl.BlockSpec((tr, GROUP_W), lambda i, j: (i, j))
    return pl.pallas_call(
        body, name=name, grid=(t // tr, SSD_GROUPS),
        in_specs=[blk, pl.BlockSpec((tr, GROUP_W), lambda i, j: (i, zb + j)),
                  pl.BlockSpec((1, GROUP_W), lambda i, j: (0, j))],
        out_specs=blk, out_shape=_sds((t, SSD_INNER), BF16),
        compiler_params=_cp("parallel", "parallel"))(y, proj, g)


def _gnorm_bwd(dout, y, proj, g, *, name):
    t = y.shape[0]
    tr = _tile(t, 640)
    zb = OFF_Z // GROUP_W

    def body(do_ref, y_ref, z_ref, g_ref, dy_ref, dz_ref, dg_ref):
        i = pl.program_id(1)
        z = z_ref[...]
        yv = y_ref[...]
        sg = _sigmoid(z)
        sz = z * sg
        v = yv * sz
        r = lax.rsqrt(jnp.mean(v * v, axis=1, keepdims=True) + RMS_EPS)
        dov = do_ref[...].astype(F32)
        gdo = dov * g_ref[...]
        m = jnp.mean(v * gdo, axis=1, keepdims=True)
        dv = r * gdo - v * (r * r * r) * m
        dy_ref[...] = dv * sz
        dz_ref[...] = (dv * yv * (sg * (1.0 + z * (1.0 - sg)))).astype(BF16)
        pg = jnp.sum(dov * v * r, axis=0, keepdims=True)

        @pl.when(i == 0)
        def _():
            dg_ref[...] = pg

        @pl.when(i > 0)
        def _():
            dg_ref[...] += pg

    blk = pl.BlockSpec((tr, GROUP_W), lambda j, i: (i, j))
    vec = pl.BlockSpec((1, GROUP_W), lambda j, i: (0, j))
    return pl.pallas_call(
        body, name=name, grid=(SSD_GROUPS, t // tr),
        in_specs=[blk, blk, pl.BlockSpec((tr, GROUP_W), lambda j, i: (i, zb + j)), vec],
        out_specs=[blk, blk, vec],
        out_shape=[_sds((t, SSD_INNER), F32), _sds((t, SSD_INNER), BF16), _sds((1, SSD_INNER), F32)],
        compiler_params=_cp("parallel", "arbitrary"))(dout, y, proj, g)


def _mix_fwd(proj, ya, ys, *, name):
    t, d = ya.shape
    tr = _tile(t, 640)
    ba, bs = OFF_GA // d, OFF_GS // d

    def body(ga_ref, gs_ref, ya_ref, ys_ref, o_ref):
        o_ref[...] = (_sigmoid(ga_ref[...]) * ya_ref[...] + _sigmoid(gs_ref[...]) * ys_ref[...]).astype(BF16)

    blk = pl.BlockSpec((tr, d), lambda i: (i, 0))
    return pl.pallas_call(
        body, name=name, grid=(t // tr,),
        in_specs=[pl.BlockSpec((tr, d), lambda i: (i, ba)), pl.BlockSpec((tr, d), lambda i: (i, bs)), blk, blk],
        out_specs=blk, out_shape=_sds((t, d), BF16), compiler_params=_cp("parallel"))(proj, proj, ya, ys)


def _mix_bwd(dmix, proj, ya, ys, *, name):
    t, d = ya.shape
    tr = _tile(t, 640)
    ba, bs = OFF_GA // d, OFF_GS // d

    def body(dm_ref, ga_ref, gs_ref, ya_ref, ys_ref, dya_ref, dys_ref, dga_ref, dgs_ref):
        dm = dm_ref[...]
        sa = _sigmoid(ga_ref[...])
        ss = _sigmoid(gs_ref[...])
        dya_ref[...] = (sa * dm).astype(BF16)
        dys_ref[...] = (ss * dm).astype(BF16)
        dga_ref[...] = (dm * ya_ref[...] * sa * (1.0 - sa)).astype(BF16)
        dgs_ref[...] = (dm * ys_ref[...] * ss * (1.0 - ss)).astype(BF16)

    blk = pl.BlockSpec((tr, d), lambda i: (i, 0))
    return pl.pallas_call(
        body, name=name, grid=(t // tr,),
        in_specs=[blk, pl.BlockSpec((tr, d), lambda i: (i, ba)), pl.BlockSpec((tr, d), lambda i: (i, bs)), blk, blk],
        out_specs=[blk] * 4, out_shape=[_sds((t, d), BF16)] * 4,
        compiler_params=_cp("parallel"))(dmix, proj, proj, ya, ys)


def _loss_grad(h, target, *, name):
    t, d = h.shape
    tr = LANES
    assert ROW0 == tr

    def body(h_ref, t_ref, dh_ref, loss_ref):
        i = pl.program_id(0)

        @pl.when(i == 0)
        def _():
            dh_ref[...] = jnp.zeros((tr, d), F32)
            loss_ref[...] = jnp.zeros((1, LANES), F32)

        @pl.when(i > 0)
        def _():
            err = h_ref[...] - t_ref[...]
            dh_ref[...] = err * (1.0 / d)
            part = jnp.sum(jnp.sum(err * err, axis=1, keepdims=True), axis=0, keepdims=True)
            loss_ref[...] += jnp.broadcast_to(part * (0.5 / d), (1, LANES))

    blk = pl.BlockSpec((tr, d), lambda i: (i, 0))
    return pl.pallas_call(
        body, name=name, grid=(t // tr,),
        in_specs=[blk, pl.BlockSpec((tr, d), lambda i: (jnp.maximum(i - 1, 0), 0))],
        out_specs=[blk, pl.BlockSpec((1, LANES), lambda i: (0, 0))],
        out_shape=[_sds((t, d), F32), _sds((1, LANES), F32)],
        compiler_params=_cp("arbitrary"))(h, target)


def _adamw_update(gv, wv, mv, vv):
    c1 = 1.0 - ADAM_B1 ** ADAM_STEP
    c2 = 1.0 - ADAM_B2 ** ADAM_STEP
    nm = ADAM_B1 * mv + (1.0 - ADAM_B1) * gv
    nv = ADAM_B2 * vv + (1.0 - ADAM_B2) * (gv * gv)
    return -ADAM_LR * ((nm / c1) / (jnp.sqrt(nv / c2) + ADAM_EPS) + ADAM_WD * wv), nm, nv


def _as_2d(a):
    return a.reshape(1, -1) if a.ndim == 1 else a.reshape(-1, a.shape[-1])


def _adamw(g, w, m, v, *, name):
    shape = w.shape
    g2, w2, m2, v2 = (_as_2d(a) for a in (g, w, m, v))
    r, c = w2.shape
    tr = _tile(r, 256, base=8)

    def body(g_ref, w_ref, m_ref, v_ref, d_ref, nm_ref, nv_ref):
        d_ref[...], nm_ref[...], nv_ref[...] = _adamw_update(g_ref[...], w_ref[...], m_ref[...], v_ref[...])

    blk = pl.BlockSpec((tr, c), lambda i: (i, 0))
    outs = pl.pallas_call(body, name=name, grid=(r // tr,), in_specs=[blk] * 4, out_specs=[blk] * 3,
                          out_shape=[_sds((r, c), F32)] * 3, compiler_params=_cp("parallel"))(g2, w2, m2, v2)
    return [o.reshape(shape) for o in outs]


def _adamw_small(items, *, name):
    n = len(items)
    shapes = [it[1].shape for it in items]
    flat = [_as_2d(a) for it in items for a in it]

    def body(*refs):
        ins, outs = refs[:4 * n], refs[4 * n:]
        for k in range(n):
            g_ref, w_ref, m_ref, v_ref = ins[4 * k:4 * k + 4]
            d_ref, nm_ref, nv_ref = outs[3 * k:3 * k + 3]
            d_ref[...], nm_ref[...], nv_ref[...] = _adamw_update(g_ref[...], w_ref[...], m_ref[...], v_ref[...])

    out_shape = [_sds(flat[4 * k + 1].shape, F32) for k in range(n) for _ in range(3)]
    outs = pl.pallas_call(body, name=name, out_shape=out_shape,
                          compiler_params=pltpu.CompilerParams(vmem_limit_bytes=VMEM_LIMIT_BYTES))(*flat)
    return [[outs[3 * k + q].reshape(shapes[k]) for q in range(3)] for k in range(n)]


def _pair_add(g5, got, core, *, name):
    n, _, r, _ = g5.shape
    tr = _tile(r, 1024, base=8)

    def body(c_ref, a_ref, b_ref, o_ref):
        o_ref[...] = a_ref[0] + b_ref[...]

    grid_spec = pltpu.PrefetchScalarGridSpec(
        num_scalar_prefetch=1, grid=(n, r // tr),
        in_specs=[pl.BlockSpec((1, 1, tr, LANES), lambda s, i, c_ref: (s, c_ref[0], i, 0)),
                  pl.BlockSpec((1, tr, LANES), lambda s, i, c_ref: (s, i, 0))],
        out_specs=pl.BlockSpec((1, tr, LANES), lambda s, i, c_ref: (s, i, 0)))
    return pl.pallas_call(body, name=name, grid_spec=grid_spec, out_shape=_sds(got.shape, F32),
                          compiler_params=_cp("parallel", "parallel"))(core, g5, got)


def _sum_chips(q, *, name):
    n, r, _ = q.shape
    tr = _tile(r, 1024, base=8)

    def body(q_ref, o_ref):
        tot = q_ref[0]
        for s in range(1, n):
            tot = tot + q_ref[s]
        o_ref[...] = tot

    return pl.pallas_call(body, name=name, grid=(r // tr,),
                          in_specs=[pl.BlockSpec((n, tr, LANES), lambda i: (0, i, 0))],
                          out_specs=pl.BlockSpec((tr, LANES), lambda i: (i, 0)),
                          out_shape=_sds((r, LANES), F32), compiler_params=_cp("parallel"))(q)


_ANY = pl.BlockSpec(memory_space=pl.ANY)
_MESH = pl.DeviceIdType.MESH


def _place():
    x, y, c = lax.axis_index("x"), lax.axis_index("y"), lax.axis_index("c")
    return x, y, c, [(1 - x, y), (x, 1 - y), (1 - x, 1 - y)]


def _chip_allgather(mine, *, name):
    na = len(mine)

    def body(*refs):
        x_refs, o_refs = refs[:na], refs[na:2 * na]
        send_sems, recv_sems, local_sems = refs[2 * na:]
        x, y, c, chips = _place()
        k = 2 * x + y

        def copy(a, n, src, dst, to):
            return pltpu.make_async_remote_copy(src_ref=src, dst_ref=dst, send_sem=send_sems.at[6 * a + n],
                                                recv_sem=recv_sems.at[6 * a + n], device_id=to, device_id_type=_MESH)

        locals_ = [pltpu.make_async_copy(x_refs[a], o_refs[a].at[k], local_sems.at[a]) for a in range(na)]
        for cp in locals_:
            cp.start()
        sends = [copy(a, n, x_refs[a].at[c], o_refs[a].at[k, c], (cx, cy, c))
                 for a in range(na) for n, (cx, cy) in enumerate(chips)]
        for cp in sends:
            cp.start()
        passed = []
        for a in range(na):
            for n, (cx, cy) in enumerate(chips):
                slab = o_refs[a].at[2 * cx + cy, c]
                copy(a, n, slab, slab, (cx, cy, c)).wait_recv()
                fw = copy(a, 3 + n, slab, slab, (x, y, 1 - c))
                fw.start()
                passed.append(fw)
        for a in range(na):
            for n, (cx, cy) in enumerate(chips):
                slab = o_refs[a].at[2 * cx + cy, 1 - c]
                copy(a, 3 + n, slab, slab, (x, y, 1 - c)).wait_recv()
        for cp in sends + passed:
            cp.wait_send()
        for cp in locals_:
            cp.wait()

    return pl.pallas_call(
        body, name=name, in_specs=[_ANY] * na, out_specs=[_ANY] * na,
        out_shape=[_sds((N_CHIPS,) + a.shape, a.dtype) for a in mine],
        scratch_shapes=[pltpu.SemaphoreType.DMA((6 * na,)), pltpu.SemaphoreType.DMA((6 * na,)),
                        pltpu.SemaphoreType.DMA((na,))])(*mine)


def _sibling_swap(g5, *, name):
    n, _, r, _ = g5.shape

    def body(x_ref, o_ref, send_sems, recv_sems):
        x, y, c, _ = _place()
        cps = [pltpu.make_async_remote_copy(src_ref=x_ref.at[s, 1 - c], dst_ref=o_ref.at[s], send_sem=send_sems.at[s],
                                            recv_sem=recv_sems.at[s], device_id=(x, y, 1 - c), device_id_type=_MESH)
               for s in range(n)]
        for cp in cps:
            cp.start()
        for cp in cps:
            cp.wait()

    return pl.pallas_call(
        body, name=name, in_specs=[_ANY], out_specs=_ANY, out_shape=_sds((n, r, LANES), g5.dtype),
        scratch_shapes=[pltpu.SemaphoreType.DMA((n,)), pltpu.SemaphoreType.DMA((n,))])(g5)


def _chip_exchange(h, *, name):
    def body(h_ref, q_ref, send_sems, recv_sems, local_sem):
        x, y, c, chips = _place()
        k = 2 * x + y
        local = pltpu.make_async_copy(h_ref.at[k], q_ref.at[k], local_sem)
        local.start()
        cps = []
        for n, (cx, cy) in enumerate(chips):
            kk = 2 * cx + cy
            cps.append(pltpu.make_async_remote_copy(src_ref=h_ref.at[kk], dst_ref=q_ref.at[k], send_sem=send_sems.at[n],
                                                    recv_sem=recv_sems.at[n], device_id=(cx, cy, c),
                                                    device_id_type=_MESH))
        for cp in cps:
            cp.start()
        for n, (cx, cy) in enumerate(chips):
            kk = 2 * cx + cy
            pltpu.make_async_remote_copy(src_ref=h_ref.at[kk], dst_ref=q_ref.at[kk], send_sem=send_sems.at[n],
                                         recv_sem=recv_sems.at[n], device_id=(cx, cy, c),
                                         device_id_type=_MESH).wait_recv()
        for cp in cps:
            cp.wait_send()
        local.wait()

    return pl.pallas_call(
        body, name=name, in_specs=[_ANY], out_specs=_ANY, out_shape=_sds(h.shape, h.dtype),
        scratch_shapes=[pltpu.SemaphoreType.DMA((3,)), pltpu.SemaphoreType.DMA((3,)), pltpu.SemaphoreType.DMA])(h)


def _sibling_allgather(mine, *, name):
    def body(x_ref, o_ref, send_sem, recv_sem, local_sem):
        x, y, c, _ = _place()
        local = pltpu.make_async_copy(x_ref, o_ref.at[c], local_sem)
        local.start()
        cp = pltpu.make_async_remote_copy(src_ref=x_ref, dst_ref=o_ref.at[c], send_sem=send_sem, recv_sem=recv_sem,
                                          device_id=(x, y, 1 - c), device_id_type=_MESH)
        cp.start()
        pltpu.make_async_remote_copy(src_ref=x_ref, dst_ref=o_ref.at[1 - c], send_sem=send_sem, recv_sem=recv_sem,
                                     device_id=(x, y, 1 - c), device_id_type=_MESH).wait_recv()
        cp.wait_send()
        local.wait()

    return pl.pallas_call(
        body, name=name, in_specs=[_ANY], out_specs=_ANY, out_shape=_sds((2,) + mine.shape, mine.dtype),
        scratch_shapes=[pltpu.SemaphoreType.DMA, pltpu.SemaphoreType.DMA, pltpu.SemaphoreType.DMA])(mine)


def _rows_of(flat, align):
    n = flat.shape[-1]
    rows = -(-n // LANES)
    rows = -(-rows // align) * align
    pad = rows * LANES - n
    if pad:
        flat = jnp.pad(flat, [(0, 0)] * (flat.ndim - 1) + [(0, pad)])
    return flat.reshape(flat.shape[:-1] + (rows, LANES))


def _chip_segments(g, kind):
    if kind == "col":
        n = g.shape[-1] // N_CHIPS
        s = g.reshape(g.shape[:-1] + (N_CHIPS, n))
        return jnp.moveaxis(s, -2, 0).reshape(N_CHIPS, -1)
    k = g.shape[-2] // N_CHIPS
    s = g.reshape(g.shape[:-2] + (N_CHIPS, k, g.shape[-1]))
    return jnp.moveaxis(s, -3, 0).reshape(N_CHIPS, -1)


def _join(blocks, kind):
    if kind == "col":
        s = jnp.moveaxis(blocks, 0, -2)
        return s.reshape(s.shape[:-2] + (s.shape[-2] * s.shape[-1],))
    return blocks.reshape((blocks.shape[0] * blocks.shape[1],) + blocks.shape[2:])


def _gather_weights(local):
    mine = []
    for n, _ in SHARDED:
        a = local[n]
        if n == "w_in":
            a = jnp.swapaxes(a, 1, 2)
        if n == "meta_tokens":
            a = a.reshape(2, N_META // 2, a.shape[-1])
        mine.append(a.astype(BF16) if n in GATHER_BF16 else a)
    got = _chip_allgather(mine, name="gather_weights")
    return {n: g for (n, _), g in zip(SHARDED, got)}


def _rope_tables(t):
    half = QK_ROPE // 2
    inv_freq = 1.0 / (ROPE_THETA ** (jnp.arange(0, QK_ROPE, 2, dtype=F32) / QK_ROPE))
    pos = jnp.maximum(jnp.arange(t, dtype=F32) - PAD, 0.0)
    ang = pos[:, None] * inv_freq[None, :]
    cos, sin = jnp.cos(ang), jnp.sin(ang)
    z = jnp.zeros((t, half), F32)
    z2 = jnp.zeros((t, LANES - QK_ROPE), F32)
    return (jnp.concatenate([cos, cos, z2], axis=1), jnp.concatenate([-sin, z, z2], axis=1),
            jnp.concatenate([z, sin, z2], axis=1))


def _expand_matrix():
    lane = np.arange(SSD_INNER) // SSD_HEAD_DIM
    e = (np.arange(LANES)[:, None] == lane[None, :]).astype(np.float32)
    return jnp.asarray(e, BF16)


def _layer_weights(full, rep, i):
    w = {}
    kinds = dict(SHARDED)
    whole = lambda n: _join(full[n][:, i], kinds[n])
    wt = _join(full["w_in"][:, i], "row")
    zr = lambda n: jnp.zeros((n, D_MODEL), BF16)
    w["w_in_t"] = jnp.concatenate(
        [wt[0:1024], wt[1088:3136], wt[3136:6208], wt[6240:7264], wt[7264:8288],
         wt[1024:1088], zr(LANES - QK_ROPE), wt[6208:6240], zr(LANES - SSD_HEADS)], axis=0)
    wq = whole("w_q_b").reshape(Q_LORA, MLA_HEADS, QK_NOPE + QK_ROPE)
    w["w_q_b"] = jnp.pad(wq, ((0, 0), (0, 0), (0, QHEAD - QK_NOPE - QK_ROPE))).reshape(Q_LORA, MLA_HEADS * QHEAD)
    wkv = whole("w_kv_b").reshape(KV_LORA, MLA_HEADS, 2, QK_NOPE)
    w["w_kv_b"] = jnp.swapaxes(wkv, 1, 2).reshape(KV_LORA, 2 * MLA_HEADS * QK_NOPE)
    for n in ("w_o_attn", "w_o_ssd", "w_out", "w_down", "ssd_conv_w"):
        w[n] = whole(n)
    w_up = whole("w_up")
    w["w_up_g"] = w_up[:, :D_FF]
    w["w_up_v"] = w_up[:, D_FF:]
    ffn_w = whole("ffn_conv_w")
    w["ffn_conv_wg"] = ffn_w[:, :D_FF]
    w["ffn_conv_wv"] = ffn_w[:, D_FF:]
    row = lambda v: v.reshape(1, -1)
    w["q_norm_g"] = row(rep["q_norm_g"][i])
    w["kv_norm_g"] = row(rep["kv_norm_g"][i])
    w["ssd_conv_b"] = row(rep["ssd_conv_b"][i])
    w["dt_bias"] = row(jnp.pad(rep["dt_bias"][i], (0, LANES - SSD_HEADS)))
    a = -jnp.exp(rep["a_log"][i])
    w["a"] = a
    w["a_x"] = row(jnp.repeat(a, SSD_HEAD_DIM))
    w["d_x"] = row(jnp.repeat(rep["d_skip"][i], SSD_HEAD_DIM))
    w["ssd_norm_g"] = row(rep["ssd_norm_g"][i])
    w["ffn_conv_bg"] = row(rep["ffn_conv_b"][i][:D_FF])
    w["ffn_conv_bv"] = row(rep["ffn_conv_b"][i][D_FF:])
    for n in ("ln1_g", "ln1_b", "ln2_g", "ln2_b"):
        w[n] = row(rep[n][i])
    return w


def _layer_fwd(h, w, tabs, expand, tag):
    cos, sin_a, sin_b = tabs
    s = {"h": h}
    proj = _mm(h, w["w_in_t"], trans_b=True, name=tag + "in_proj", tn=768)
    s["proj"] = proj
    qn = _rms_fwd(proj, OFF_Q, Q_LORA, w["q_norm_g"], name=tag + "q_norm")
    q_raw = _mm(qn, w["w_q_b"], name=tag + "q_up")
    q = _rope_q_fwd(q_raw, cos, sin_a, sin_b, name=tag + "q_rope")
    kvn = _rms_fwd(proj, OFF_KV, KV_LORA, w["kv_norm_g"], name=tag + "kv_norm")
    kv = _mm(kvn, w["w_kv_b"], name=tag + "kv_up", out_dtype=BF16)
    kpe = _rope_k_fwd(proj, cos, sin_a, sin_b, name=tag + "k_rope")
    o, lse = _flash_fwd(q, kv, kpe, name=tag + "attn")
    ya = _mm(o, w["w_o_attn"], name=tag + "attn_out")
    s.update(qn=qn, q=q, kvn=kvn, kv=kv, kpe=kpe, o=o, lse=lse, ya=ya)
    xs, bm, cm, dtx = _ssd_prep_fwd(proj, w["ssd_conv_w"], w["ssd_conv_b"], w["dt_bias"], expand, name=tag + "ssd_prep")
    y, prev = _ssd_fwd(xs, dtx, bm, cm, bm.T, w["a_x"], w["d_x"], name=tag + "ssd_scan")
    yn = _gnorm_fwd(y, proj, w["ssd_norm_g"], name=tag + "ssd_norm")
    ys = _mm(yn, w["w_o_ssd"], name=tag + "ssd_out")
    s.update(xs=xs, bm=bm, cm=cm, dtx=dtx, y=y, prev=prev, yn=yn, ys=ys)
    mixed = _mix_fwd(proj, ya, ys, name=tag + "mix")
    br = _mm(mixed, w["w_out"], name=tag + "mix_out")
    pre1, h1 = _ln_fwd(h, br, w["ln1_g"], w["ln1_b"], name=tag + "ln1")
    s.update(mixed=mixed, pre1=pre1, h1=h1)
    ug = _mm(h1, w["w_up_g"], name=tag + "up_g", tn=1408)
    uv = _mm(h1, w["w_up_v"], name=tag + "up_v", tn=1408)
    act = _ffn_act_fwd(ug, uv, w["ffn_conv_wg"], w["ffn_conv_wv"], w["ffn_conv_bg"], w["ffn_conv_bv"],
                       name=tag + "ffn_act")
    ffn = _mm(act, w["w_down"], name=tag + "down")
    pre2, h2 = _ln_fwd(h1, ffn, w["ln2_g"], w["ln2_b"], name=tag + "ln2")
    s.update(ug=ug, uv=uv, act=act, pre2=pre2)
    return h2, s


def _layer_bwd(dh2, w, s, tabs, reduce_m, tag):
    cos, sin_a, sin_b = tabs
    g = {}
    proj = s["proj"]
    dpre2, g["ln2_g"], g["ln2_b"] = _ln_bwd(dh2, s["pre2"], w["ln2_g"], name=tag + "ln2_bwd")
    g["w_down"] = _mm_tn(s["act"], dpre2, name=tag + "down_dw")
    dact = _mm(dpre2, w["w_down"], trans_b=True, name=tag + "down_dx", tn=1408)
    dcg, dcv, dwg, dwv, dbg, dbv = _ffn_act_bwd(s["ug"], s["uv"], dact, w["ffn_conv_wg"], w["ffn_conv_wv"],
                                                w["ffn_conv_bg"], w["ffn_conv_bv"], name=tag + "ffn_act_bwd")
    g["ffn_conv_w"] = jnp.concatenate([dwg, dwv], axis=1)
    g["ffn_conv_b"] = jnp.concatenate([dbg, dbv], axis=1).reshape(-1)
    dug = _conv_bwd_input(dcg, w["ffn_conv_wg"], FFN_CONV, name=tag + "ffn_conv_bwd_g", tc=1408)
    duv = _conv_bwd_input(dcv, w["ffn_conv_wv"], FFN_CONV, name=tag + "ffn_conv_bwd_v", tc=1408)
    g["w_up"] = jnp.concatenate([_mm_tn(s["h1"], dug, name=tag + "up_g_dw", tn=1408),
                                 _mm_tn(s["h1"], duv, name=tag + "up_v_dw", tn=1408)], axis=1)
    dh1 = _mm(dug, w["w_up_g"], trans_b=True, add=dpre2, add_scale=ALPHA, name=tag + "up_g_dx")
    dh1 = _mm(duv, w["w_up_v"], trans_b=True, add=dh1, name=tag + "up_v_dx")
    dpre1, g["ln1_g"], g["ln1_b"] = _ln_bwd(dh1, s["pre1"], w["ln1_g"], name=tag + "ln1_bwd")
    g["w_out"] = _mm_tn(s["mixed"], dpre1, name=tag + "mix_out_dw")
    dmix = _mm(dpre1, w["w_out"], trans_b=True, name=tag + "mix_out_dx")
    dya, dys, dga, dgs = _mix_bwd(dmix, proj, s["ya"], s["ys"], name=tag + "mix_bwd")
    g["w_o_attn"] = _mm_tn(s["o"], dya, name=tag + "attn_out_dw")
    do = _mm(dya, w["w_o_attn"], trans_b=True, out_dtype=BF16, name=tag + "attn_out_dx")
    delta = _attn_delta(do, s["o"], name=tag + "attn_delta")
    dq, dkn, dkp, dv = _flash_bwd(s["q"], s["kv"], s["kpe"], do, s["lse"], delta, name=tag + "attn_bwd")
    dq_raw = _rope_q_bwd(dq, cos, sin_a, sin_b, name=tag + "q_rope_bwd")
    dwq = _mm_tn(s["qn"], dq_raw, name=tag + "q_up_dw")
    g["w_q_b"] = dwq.reshape(Q_LORA, MLA_HEADS, QHEAD)[:, :, :QK_NOPE + QK_ROPE].reshape(Q_LORA, -1)
    dqn = _mm(dq_raw, w["w_q_b"], trans_b=True, out_dtype=BF16, name=tag + "q_up_dx")
    dqlat, dgq = _rms_bwd(dqn, proj, OFF_Q, Q_LORA, w["q_norm_g"], name=tag + "q_norm_bwd")
    g["q_norm_g"] = dgq.reshape(-1)
    dkv = jnp.concatenate([dkn, dv], axis=1)
    dwkv = _mm_tn(s["kvn"], dkv, name=tag + "kv_up_dw")
    g["w_kv_b"] = jnp.swapaxes(dwkv.reshape(KV_LORA, 2, MLA_HEADS, QK_NOPE), 1, 2).reshape(KV_LORA, -1)
    dkvn = _mm(dkv, w["w_kv_b"], trans_b=True, out_dtype=BF16, name=tag + "kv_up_dx")
    dkvlat, dgkv = _rms_bwd(dkvn, proj, OFF_KV, KV_LORA, w["kv_norm_g"], name=tag + "kv_norm_bwd")
    g["kv_norm_g"] = dgkv.reshape(-1)
    dkpe = _rope_k_bwd(dkp, cos, sin_a, sin_b, name=tag + "k_rope_bwd")
    g["w_o_ssd"] = _mm_tn(s["yn"], dys, name=tag + "ssd_out_dw")
    dyn = _mm(dys, w["w_o_ssd"], trans_b=True, out_dtype=BF16, name=tag + "ssd_out_dx")
    dy, dz, dgn = _gnorm_bwd(dyn, s["y"], proj, w["ssd_norm_g"], name=tag + "ssd_norm_bwd")
    g["ssd_norm_g"] = dgn.reshape(-1)
    dxs, ddtx, dbm, dcm, da_x, dd_x = _ssd_bwd(s["xs"], s["dtx"], s["bm"], s["cm"], s["cm"].T, s["prev"], dy,
                                               w["a_x"], w["d_x"], name=tag + "ssd_scan_bwd")
    g["a_log"] = da_x.reshape(SSD_HEADS, SSD_HEAD_DIM).sum(axis=1) * w["a"]
    g["d_skip"] = dd_x.reshape(SSD_HEADS, SSD_HEAD_DIM).sum(axis=1)
    dconv, ddtr, dcw, dcb, ddtb = _ssd_prep_bwd_a(proj, dxs, dbm, dcm, ddtx, w["ssd_conv_w"], w["ssd_conv_b"],
                                                  w["dt_bias"], reduce_m, name=tag + "ssd_prep_bwd")
    g["ssd_conv_w"] = dcw
    g["ssd_conv_b"] = dcb.reshape(-1)
    g["dt_bias"] = ddtb.reshape(-1)[:SSD_HEADS]
    dxbc = _conv_bwd_input(dconv, w["ssd_conv_w"], SSD_CONV, name=tag + "ssd_conv_bwd", tc=1024)
    h = s["h"]
    comps = ((dqlat, OFF_Q), (dkvlat, OFF_KV), (dz, OFF_Z), (dxbc, OFF_XBC), (dga, OFF_GA), (dgs, OFF_GS),
             (dkpe, OFF_KPE), (ddtr, OFF_DT))
    dws = {}
    dh = None
    for n, (dc, off) in enumerate(comps):
        width = dc.shape[1]
        dws[off] = _mm_tn(dc, h, name=f"{tag}in_dw{n}")
        w_c = w["w_in_t"][off:off + width]
        if dh is None:
            dh = _mm(dc, w_c, add=dpre1, add_scale=ALPHA, name=f"{tag}in_dx{n}")
        else:
            dh = _mm(dc, w_c, add=dh, name=f"{tag}in_dx{n}")
    g["w_in"] = jnp.concatenate([dws[OFF_Q], dws[OFF_KV], dws[OFF_KPE][:QK_ROPE], dws[OFF_Z], dws[OFF_XBC],
                                 dws[OFF_DT][:SSD_HEADS], dws[OFF_GA], dws[OFF_GS]], axis=0)
    return dh, g


def _local_step(x, target, full, rep):
    seq = x.shape[0]
    t = seq + ROW0
    tabs = _rope_tables(t)
    expand = _expand_matrix()
    reduce_m = expand.T
    meta = _join(full["meta_tokens"].reshape(N_CHIPS, N_META, -1), "col")
    xin = jnp.concatenate([jnp.zeros((PAD, D_MODEL), F32), meta, x], axis=0)
    row = lambda v: v.reshape(1, -1)
    _, h = _ln_fwd(xin, None, row(rep["emb_ln_g"]), row(rep["emb_ln_b"]), name="emb_ln")
    ws, saved = [], []
    for i in range(DEPTH):
        w = _layer_weights(full, rep, i)
        h, s = _layer_fwd(h, w, tabs, expand, f"l{i}_")
        ws.append(w)
        saved.append(s)
    dh, loss = _loss_grad(h, target, name="loss")
    layer_grads = [None] * DEPTH
    for i in reversed(range(DEPTH)):
        dh, layer_grads[i] = _layer_bwd(dh, ws[i], saved[i], tabs, reduce_m, f"l{i}_")
    dxin, dg, db = _ln_bwd(dh, xin, row(rep["emb_ln_g"]), name="emb_ln_bwd")
    grads = {n: jnp.stack([layer_grads[i][n] for i in range(DEPTH)]) for n in layer_grads[0]}
    grads["emb_ln_g"] = dg.reshape(-1)
    grads["emb_ln_b"] = db.reshape(-1)
    grads["meta_tokens"] = dxin[PAD:ROW0]
    return loss, dxin[ROW0:], grads


def _reduce_grads(grads):
    segs = [_chip_segments(grads[n], "row" if n == "w_in" else kind) for n, kind in SHARDED]
    small = jnp.concatenate([grads[n].reshape(-1) for n in REPLICATED])
    segs.append(jnp.broadcast_to(small[None], (N_CHIPS, small.shape[0])))
    g4 = _rows_of(jnp.concatenate(segs, axis=1), REDUCE_ROW_ALIGN)
    r = g4.shape[1]
    g5 = g4.reshape(N_CHIPS, 2, r // 2, LANES)
    core = lax.axis_index("c").astype(jnp.int32).reshape(1)
    got = _sibling_swap(g5, name="reduce_pair_swap")
    pair = _pair_add(g5, got, core, name="reduce_pair_add")
    parts = _chip_exchange(pair, name="reduce_chip_exchange")
    half = _sum_chips(parts, name="reduce_chip_sum")
    both = _sibling_allgather(half, name="reduce_pair_gather")
    return both.reshape(r, LANES)


def kernel(x, meta_tokens, emb_ln_g, emb_ln_b, w_in, q_norm_g, w_q_b, kv_norm_g, w_kv_b, w_o_attn, ssd_conv_w, ssd_conv_b, dt_bias, a_log, d_skip, ssd_norm_g, w_o_ssd, w_out, ln1_g, ln1_b, w_up, ffn_conv_w, ffn_conv_b, w_down, ln2_g, ln2_b, loss_target, m_meta_tokens, m_emb_ln_g, m_emb_ln_b, m_w_in, m_q_norm_g, m_w_q_b, m_kv_norm_g, m_w_kv_b, m_w_o_attn, m_ssd_conv_w, m_ssd_conv_b, m_dt_bias, m_a_log, m_d_skip, m_ssd_norm_g, m_w_o_ssd, m_w_out, m_ln1_g, m_ln1_b, m_w_up, m_ffn_conv_w, m_ffn_conv_b, m_w_down, m_ln2_g, m_ln2_b, v_meta_tokens, v_emb_ln_g, v_emb_ln_b, v_w_in, v_q_norm_g, v_w_q_b, v_kv_norm_g, v_w_kv_b, v_w_o_attn, v_ssd_conv_w, v_ssd_conv_b, v_dt_bias, v_a_log, v_d_skip, v_ssd_norm_g, v_w_o_ssd, v_w_out, v_ln1_g, v_ln1_b, v_w_up, v_ffn_conv_w, v_ffn_conv_b, v_w_down, v_ln2_g, v_ln2_b):
    given = dict(locals())
    local_w = {n: given[n] for n in WEIGHTS}
    local_m = {n: given["m_" + n] for n in WEIGHTS}
    local_v = {n: given["v_" + n] for n in WEIGHTS}
    full = _gather_weights(local_w)
    rep = {n: local_w[n] for n in REPLICATED}
    loss, grad_x, grads = _local_step(x[0], loss_target[0], full, rep)
    g_flat = _reduce_grads(grads)
    flat = g_flat.reshape(-1)
    grad, off = {}, 0
    for n in [n for n, _ in SHARDED] + list(REPLICATED):
        shape = local_w[n].shape
        size = int(np.prod(shape))
        piece = flat[off:off + size]
        if n == "w_in":
            grad[n] = jnp.swapaxes(piece.reshape(shape[0], shape[2], shape[1]), 1, 2)
        else:
            grad[n] = piece.reshape(shape)
        off += size
    upd = {}
    small = [n for n in WEIGHTS if n not in GATHER_BF16]
    for n in GATHER_BF16:
        upd[n] = _adamw(grad[n], local_w[n], local_m[n], local_v[n], name="adamw_" + n)
    res = _adamw_small([(grad[n], local_w[n], local_m[n], local_v[n]) for n in small], name="adamw_small")
    upd.update(zip(small, res))
    total = lax.psum(loss[0, 0], ("x", "y", "c"))
    outs = [total, grad_x[None]] + [grad[n] for n in WEIGHTS]
    for q in range(3):
        outs.extend(upd[n][q] for n in WEIGHTS)
    return tuple(outs)
```

```python
import functools
import math

import numpy as np
import jax
import jax.numpy as jnp
from jax import lax
from jax.experimental import pallas as pl
from jax.experimental.pallas import tpu as pltpu

F32 = jnp.float32
BF16 = jnp.bfloat16

D_MODEL = 1024
N_META = 16
DEPTH = 2
MLA_HEADS = 8
Q_LORA = 768
KV_LORA = 256
QK_NOPE = 128
QK_ROPE = 64
V_HEAD = 128
ROPE_THETA = 10000.0
NEG_INF = -1e30
SSD_INNER = 2048
SSD_HEAD_DIM = 64
SSD_HEADS = 32
SSD_GROUPS = 4
SSD_STATE = 128
SSD_CONV = 4
SSD_CONV_DIM = 3072
CHUNK = 128
D_FF = 2816
FFN_CONV = 3
LN_EPS = 1e-5
RMS_EPS = 1e-6
ALPHA = (2 * DEPTH) ** 0.25
ATTN_SCALE = (QK_NOPE + QK_ROPE) ** -0.5
LOG2E = math.log2(math.e)
LN2 = math.log(2.0)
Q_SCALE = ATTN_SCALE * LOG2E
ADAM_LR = 0.001
ADAM_B1 = 0.9
ADAM_B2 = 0.999
ADAM_EPS = 1e-08
ADAM_WD = 0.01
ADAM_STEP = 10

LANES = 128
PAD = 112
ROW0 = PAD + N_META
QHEAD = 256
GROUP_W = SSD_INNER // SSD_GROUPS
HALO = 8
VMEM_LIMIT_BYTES = 56 * 1024 * 1024
N_CHIPS = 4

OFF_Q, OFF_KV, OFF_Z, OFF_XBC, OFF_GA, OFF_GS, OFF_KPE, OFF_DT = 0, 768, 1024, 3072, 6144, 7168, 8192, 8320
IN_COLS_P = 8448

NT_DIMS = (((1,), (1,)), ((), ()))
NN_DIMS = (((1,), (0,)), ((), ()))
TN_DIMS = (((0,), (0,)), ((), ()))

SHARDED = (("meta_tokens", "col"), ("w_in", "col"), ("w_q_b", "col"), ("w_kv_b", "col"), ("w_o_attn", "row"),
           ("ssd_conv_w", "col"), ("w_o_ssd", "row"), ("w_out", "row"), ("w_up", "col"), ("ffn_conv_w", "col"),
           ("w_down", "row"))
REPLICATED = ("emb_ln_g", "emb_ln_b", "q_norm_g", "kv_norm_g", "ssd_conv_b", "dt_bias", "a_log", "d_skip",
              "ssd_norm_g", "ln1_g", "ln1_b", "ffn_conv_b", "ln2_g", "ln2_b")
WEIGHTS = ("meta_tokens", "emb_ln_g", "emb_ln_b", "w_in", "q_norm_g", "w_q_b", "kv_norm_g", "w_kv_b", "w_o_attn",
           "ssd_conv_w", "ssd_conv_b", "dt_bias", "a_log", "d_skip", "ssd_norm_g", "w_o_ssd", "w_out", "ln1_g",
           "ln1_b", "w_up", "ffn_conv_w", "ffn_conv_b", "w_down", "ln2_g", "ln2_b")
GATHER_BF16 = ("w_in", "w_q_b", "w_kv_b", "w_o_attn", "w_o_ssd", "w_out", "w_up", "w_down")
GATHER_F32 = ("meta_tokens", "ssd_conv_w", "ffn_conv_w")
REDUCE_ROW_ALIGN = 2048


def _tile(n, target, base=LANES):
    best = None
    d = base
    while d <= min(n, target):
        if n % d == 0:
            best = d
        d += base
    return n if best is None else best


def _cp(*sem):
    return pltpu.CompilerParams(dimension_semantics=sem, vmem_limit_bytes=VMEM_LIMIT_BYTES)


def _sds(shape, dtype):
    return jax.ShapeDtypeStruct(shape, dtype)


def _row_ids(i, tr, shape):
    return i * tr + lax.broadcasted_iota(jnp.int32, shape, 0)


def _sigmoid(x):
    return 1.0 / (1.0 + jnp.exp(-x))


def _mm(a, b, *, name, trans_b=False, out_dtype=F32, add=None, add_scale=1.0, tm=640, tn=1024, tk=1408):
    m, k_dim = a.shape
    n = b.shape[0] if trans_b else b.shape[1]
    tm, tn, tk = _tile(m, tm), _tile(n, tn), _tile(k_dim, tk)
    nk = k_dim // tk
    has_add = add is not None
    dims = NT_DIMS if trans_b else NN_DIMS

    def body(*refs):
        a_ref, b_ref = refs[0], refs[1]
        r_ref = refs[2] if has_add else None
        o_ref = refs[3] if has_add else refs[2]
        part = lax.dot_general(a_ref[...].astype(BF16), b_ref[...].astype(BF16), dims, preferred_element_type=F32)

        def finish(r):
            if has_add:
                r = r + add_scale * r_ref[...].astype(F32)
            o_ref[...] = r.astype(out_dtype)

        if nk == 1:
            finish(part)
        else:
            acc = refs[-1]
            kk = pl.program_id(2)

            @pl.when(kk == 0)
            def _():
                acc[...] = part

            @pl.when(kk > 0)
            def _():
                acc[...] += part

            @pl.when(kk == nk - 1)
            def _():
                finish(acc[...])

    in_specs = [pl.BlockSpec((tm, tk), lambda i, j, kk: (i, kk)),
                pl.BlockSpec((tn, tk), lambda i, j, kk: (j, kk)) if trans_b
                else pl.BlockSpec((tk, tn), lambda i, j, kk: (kk, j))]
    args = [a, b]
    if has_add:
        in_specs.append(pl.BlockSpec((tm, tn), lambda i, j, kk: (i, j)))
        args.append(add)
    return pl.pallas_call(
        body, name=name, grid=(m // tm, n // tn, nk), in_specs=in_specs,
        out_specs=pl.BlockSpec((tm, tn), lambda i, j, kk: (i, j)),
        out_shape=_sds((m, n), out_dtype),
        scratch_shapes=[pltpu.VMEM((tm, tn), F32)] if nk > 1 else [],
        compiler_params=_cp("parallel", "parallel", "arbitrary"),
    )(*args)


def _mm_sum(pairs, add, *, name, add_scale=1.0, tm=640):
    m, n = add.shape
    tm = _tile(m, tm)
    npairs = len(pairs)

    def body(*refs):
        a_refs, b_refs = refs[:npairs], refs[npairs:2 * npairs]
        r_ref, o_ref = refs[2 * npairs], refs[2 * npairs + 1]
        acc = add_scale * r_ref[...]
        for a_ref, b_ref in zip(a_refs, b_refs):
            acc = acc + jnp.dot(a_ref[...].astype(BF16), b_ref[...].astype(BF16), preferred_element_type=F32)
        o_ref[...] = acc

    in_specs = ([pl.BlockSpec((tm, a.shape[1]), lambda i: (i, 0)) for a, _ in pairs]
                + [pl.BlockSpec(b.shape, lambda i: (0, 0)) for _, b in pairs]
                + [pl.BlockSpec((tm, n), lambda i: (i, 0))])
    return pl.pallas_call(
        body, name=name, grid=(m // tm,), in_specs=in_specs, out_specs=pl.BlockSpec((tm, n), lambda i: (i, 0)),
        out_shape=_sds((m, n), F32), compiler_params=_cp("parallel"),
    )(*[a for a, _ in pairs], *[b for _, b in pairs], add)


def _mm_tn(a, b, *, name, tko=1408, tn=1024, tt=640):
    t, k_dim = a.shape
    n = b.shape[1]
    tko, tn, tt = _tile(k_dim, tko), _tile(n, tn), _tile(t, tt)

    def body(a_ref, b_ref, o_ref):
        part = lax.dot_general(a_ref[...].astype(BF16), b_ref[...].astype(BF16), TN_DIMS, preferred_element_type=F32)
        tt_i = pl.program_id(2)

        @pl.when(tt_i == 0)
        def _():
            o_ref[...] = part

        @pl.when(tt_i > 0)
        def _():
            o_ref[...] += part

    return pl.pallas_call(
        body, name=name, grid=(k_dim // tko, n // tn, t // tt),
        in_specs=[pl.BlockSpec((tt, tko), lambda i, j, s: (s, i)), pl.BlockSpec((tt, tn), lambda i, j, s: (s, j))],
        out_specs=pl.BlockSpec((tko, tn), lambda i, j, s: (i, j)),
        out_shape=_sds((k_dim, n), F32),
        compiler_params=_cp("parallel", "parallel", "arbitrary"),
    )(a, b)


def _ln_fwd(h, branch, g, b, *, name):
    t, d = h.shape
    tr = _tile(t, 640)
    has_branch = branch is not None

    def body(*refs):
        if has_branch:
            h_ref, br_ref, g_ref, b_ref, pre_ref, o_ref = refs
            pre = ALPHA * h_ref[...] + br_ref[...]
            pre_ref[...] = pre
        else:
            h_ref, g_ref, b_ref, o_ref = refs
            pre = h_ref[...]
        mu = jnp.mean(pre, axis=1, keepdims=True)
        xc = pre - mu
        var = jnp.mean(xc * xc, axis=1, keepdims=True)
        y = xc * lax.rsqrt(var + LN_EPS) * g_ref[...] + b_ref[...]
        rows = _row_ids(pl.program_id(0), tr, (tr, 1))
        o_ref[...] = jnp.where(rows >= PAD, y, 0.0)

    row_spec = pl.BlockSpec((tr, d), lambda i: (i, 0))
    vec_spec = pl.BlockSpec((1, d), lambda i: (0, 0))
    if has_branch:
        return pl.pallas_call(
            body, name=name, grid=(t // tr,), in_specs=[row_spec, row_spec, vec_spec, vec_spec],
            out_specs=[row_spec, row_spec], out_shape=[_sds((t, d), F32), _sds((t, d), F32)],
            compiler_params=_cp("parallel"))(h, branch, g, b)
    out = pl.pallas_call(
        body, name=name, grid=(t // tr,), in_specs=[row_spec, vec_spec, vec_spec],
        out_specs=row_spec, out_shape=_sds((t, d), F32), compiler_params=_cp("parallel"))(h, g, b)
    return h, out


def _ln_bwd(dy, pre, g, *, name):
    t, d = pre.shape
    tr = _tile(t, 640)

    def body(dy_ref, pre_ref, g_ref, dpre_ref, dg_ref, db_ref):
        i = pl.program_id(0)
        pre_v = pre_ref[...]
        mu = jnp.mean(pre_v, axis=1, keepdims=True)
        xc = pre_v - mu
        var = jnp.mean(xc * xc, axis=1, keepdims=True)
        rstd = lax.rsqrt(var + LN_EPS)
        xhat = xc * rstd
        rows = _row_ids(i, tr, (tr, 1))
        dym = jnp.where(rows >= PAD, dy_ref[...], 0.0)
        gdy = dym * g_ref[...]
        m1 = jnp.mean(gdy, axis=1, keepdims=True)
        m2 = jnp.mean(gdy * xhat, axis=1, keepdims=True)
        dpre_ref[...] = rstd * (gdy - m1 - xhat * m2)
        pg = jnp.sum(dym * xhat, axis=0, keepdims=True)
        pb = jnp.sum(dym, axis=0, keepdims=True)

        @pl.when(i == 0)
        def _():
            dg_ref[...] = pg
            db_ref[...] = pb

        @pl.when(i > 0)
        def _():
            dg_ref[...] += pg
            db_ref[...] += pb

    row_spec = pl.BlockSpec((tr, d), lambda i: (i, 0))
    vec_spec = pl.BlockSpec((1, d), lambda i: (0, 0))
    return pl.pallas_call(
        body, name=name, grid=(t // tr,), in_specs=[row_spec, row_spec, vec_spec],
        out_specs=[row_spec, vec_spec, vec_spec],
        out_shape=[_sds((t, d), F32), _sds((1, d), F32), _sds((1, d), F32)],
        compiler_params=_cp("arbitrary"))(dy, pre, g)


def _rms_fwd(proj, col_off, width, g, *, name):
    t = proj.shape[0]
    tr = _tile(t, 640)
    cb = col_off // width

    def body(x_ref, g_ref, o_ref):
        x = x_ref[...]
        r = lax.rsqrt(jnp.mean(x * x, axis=1, keepdims=True) + RMS_EPS)
        o_ref[...] = (x * r * g_ref[...]).astype(BF16)

    return pl.pallas_call(
        body, name=name, grid=(t // tr,),
        in_specs=[pl.BlockSpec((tr, width), lambda i: (i, cb)), pl.BlockSpec((1, width), lambda i: (0, 0))],
        out_specs=pl.BlockSpec((tr, width), lambda i: (i, 0)), out_shape=_sds((t, width), BF16),
        compiler_params=_cp("parallel"))(proj, g)


def _rms_bwd(dy, proj, col_off, width, g, *, name):
    t = proj.shape[0]
    tr = _tile(t, 640)
    cb = col_off // width

    def body(dy_ref, x_ref, g_ref, dx_ref, dg_ref):
        i = pl.program_id(0)
        x = x_ref[...]
        dyv = dy_ref[...].astype(F32)
        r = lax.rsqrt(jnp.mean(x * x, axis=1, keepdims=True) + RMS_EPS)
        gdy = dyv * g_ref[...]
        m = jnp.mean(x * gdy, axis=1, keepdims=True)
        dx_ref[...] = (r * gdy - x * (r * r * r) * m).astype(BF16)
        pg = jnp.sum(dyv * x * r, axis=0, keepdims=True)

        @pl.when(i == 0)
        def _():
            dg_ref[...] = pg

        @pl.when(i > 0)
        def _():
            dg_ref[...] += pg

    return pl.pallas_call(
        body, name=name, grid=(t // tr,),
        in_specs=[pl.BlockSpec((tr, width), lambda i: (i, 0)), pl.BlockSpec((tr, width), lambda i: (i, cb)),
                  pl.BlockSpec((1, width), lambda i: (0, 0))],
        out_specs=[pl.BlockSpec((tr, width), lambda i: (i, 0)), pl.BlockSpec((1, width), lambda i: (0, 0))],
        out_shape=[_sds((t, width), BF16), _sds((1, width), F32)],
        compiler_params=_cp("arbitrary"))(dy, proj, g)


def _rope_apply(r, cos, sin_a, sin_b):
    return r * cos + pltpu.roll(r, 96, 1) * sin_a + pltpu.roll(r, 32, 1) * sin_b


def _rope_apply_t(dr, cos, sin_a, sin_b):
    return dr * cos + pltpu.roll(dr * sin_a, 32, 1) + pltpu.roll(dr * sin_b, 96, 1)


def _rope_q_fwd(q, cos, sin_a, sin_b, *, name):
    t, w = q.shape
    tr = _tile(t, 128)

    def body(q_ref, c_ref, sa_ref, sb_ref, o_ref):
        c, sa, sb = c_ref[...], sa_ref[...], sb_ref[...]
        for h in range(MLA_HEADS):
            base = h * QHEAD
            o_ref[:, base:base + LANES] = (q_ref[:, base:base + LANES] * Q_SCALE).astype(BF16)
            rot = _rope_apply(q_ref[:, base + LANES:base + QHEAD], c, sa, sb)
            o_ref[:, base + LANES:base + QHEAD] = (rot * Q_SCALE).astype(BF16)

    tab = pl.BlockSpec((tr, LANES), lambda i: (i, 0))
    row = pl.BlockSpec((tr, w), lambda i: (i, 0))
    return pl.pallas_call(body, name=name, grid=(t // tr,), in_specs=[row, tab, tab, tab], out_specs=row,
                          out_shape=_sds((t, w), BF16), compiler_params=_cp("parallel"))(q, cos, sin_a, sin_b)


def _rope_q_bwd(dq, cos, sin_a, sin_b, *, name):
    t, w = dq.shape
    tr = _tile(t, 128)

    def body(dq_ref, c_ref, sa_ref, sb_ref, o_ref):
        c, sa, sb = c_ref[...], sa_ref[...], sb_ref[...]
        for h in range(MLA_HEADS):
            base = h * QHEAD
            o_ref[:, base:base + LANES] = (dq_ref[:, base:base + LANES] * ATTN_SCALE).astype(BF16)
            d_rot = _rope_apply_t(dq_ref[:, base + LANES:base + QHEAD], c, sa, sb)
            o_ref[:, base + LANES:base + QHEAD] = (d_rot * ATTN_SCALE).astype(BF16)

    tab = pl.BlockSpec((tr, LANES), lambda i: (i, 0))
    row = pl.BlockSpec((tr, w), lambda i: (i, 0))
    return pl.pallas_call(body, name=name, grid=(t // tr,), in_specs=[row, tab, tab, tab], out_specs=row,
                          out_shape=_sds((t, w), BF16), compiler_params=_cp("parallel"))(dq, cos, sin_a, sin_b)


def _rope_k_fwd(proj, cos, sin_a, sin_b, *, name):
    t = proj.shape[0]
    tr = _tile(t, 640)
    cb = OFF_KPE // LANES

    def body(x_ref, c_ref, sa_ref, sb_ref, o_ref):
        o_ref[...] = _rope_apply(x_ref[...], c_ref[...], sa_ref[...], sb_ref[...]).astype(BF16)

    tab = pl.BlockSpec((tr, LANES), lambda i: (i, 0))
    return pl.pallas_call(body, name=name, grid=(t // tr,),
                          in_specs=[pl.BlockSpec((tr, LANES), lambda i: (i, cb)), tab, tab, tab], out_specs=tab,
                          out_shape=_sds((t, LANES), BF16), compiler_params=_cp("parallel"))(proj, cos, sin_a, sin_b)


def _rope_k_bwd(dkp, cos, sin_a, sin_b, *, name):
    nh, t, _ = dkp.shape
    tr = _tile(t, 640)

    def body(d_ref, c_ref, sa_ref, sb_ref, o_ref):
        tot = d_ref[0]
        for h in range(1, nh):
            tot = tot + d_ref[h]
        o_ref[...] = _rope_apply_t(tot, c_ref[...], sa_ref[...], sb_ref[...]).astype(BF16)

    tab = pl.BlockSpec((tr, LANES), lambda i: (i, 0))
    return pl.pallas_call(body, name=name, grid=(t // tr,),
                          in_specs=[pl.BlockSpec((nh, tr, LANES), lambda i: (0, i, 0)), tab, tab, tab], out_specs=tab,
                          out_shape=_sds((t, LANES), BF16), compiler_params=_cp("parallel"))(dkp, cos, sin_a, sin_b)


def _visible(i, j, tb):
    row = i * tb + lax.broadcasted_iota(jnp.int32, (tb, tb), 0)
    col = j * tb + lax.broadcasted_iota(jnp.int32, (tb, tb), 1)
    return (col <= row) & (col >= PAD)


def _flash_fwd(q, kv, kpe, *, name):
    t = q.shape[0]
    nh = MLA_HEADS
    tb = _tile(t, 640)
    nb = t // tb

    qt = np.asarray([i for i in range(nb) for j in range(i + 1)], np.int32)
    kt = np.asarray([j for i in range(nb) for j in range(i + 1)], np.int32)

    def body(qt_ref, kt_ref, q_ref, kn_ref, v_ref, kp_ref, o_ref, lse_ref, m_s, l_s, acc_s):
        pair = pl.program_id(1)
        i, j = qt_ref[pair], kt_ref[pair]

        @pl.when(j == 0)
        def _():
            m_s[...] = jnp.full((tb, 1), NEG_INF, F32)
            l_s[...] = jnp.zeros((tb, 1), F32)
            acc_s[...] = jnp.zeros((tb, V_HEAD), F32)

        def step(masked):
            k = jnp.concatenate([kn_ref[...], kp_ref[...]], axis=1)
            s = lax.dot_general(q_ref[...], k, NT_DIMS, preferred_element_type=F32)
            if masked:
                s = jnp.where(_visible(i, j, tb), s, NEG_INF)
            m_prev = m_s[...]
            m_new = jnp.maximum(m_prev, jnp.max(s, axis=1, keepdims=True))
            p = jnp.exp2(s - m_new)
            corr = jnp.exp2(m_prev - m_new)
            l_s[...] = corr * l_s[...] + jnp.sum(p, axis=1, keepdims=True)
            acc_s[...] = corr * acc_s[...] + jnp.dot(p.astype(BF16), v_ref[...], preferred_element_type=F32)
            m_s[...] = m_new

        edge = (j == i) | (j == 0)

        @pl.when(edge)
        def _():
            step(True)

        @pl.when(jnp.logical_not(edge))
        def _():
            step(False)

        @pl.when(j == i)
        def _():
            l = l_s[...]
            o_ref[...] = (acc_s[...] / l).astype(BF16)
            lse_ref[0] = m_s[...] + jnp.log2(l)

    grid_spec = pltpu.PrefetchScalarGridSpec(
        num_scalar_prefetch=2, grid=(nh, len(qt)),
        in_specs=[pl.BlockSpec((tb, QHEAD), lambda h, p, qt_r, kt_r: (qt_r[p], h)),
                  pl.BlockSpec((tb, LANES), lambda h, p, qt_r, kt_r: (kt_r[p], h)),
                  pl.BlockSpec((tb, LANES), lambda h, p, qt_r, kt_r: (kt_r[p], nh + h)),
                  pl.BlockSpec((tb, LANES), lambda h, p, qt_r, kt_r: (kt_r[p], 0))],
        out_specs=[pl.BlockSpec((tb, V_HEAD), lambda h, p, qt_r, kt_r: (qt_r[p], h)),
                   pl.BlockSpec((1, tb, 1), lambda h, p, qt_r, kt_r: (h, qt_r[p], 0))],
        scratch_shapes=[pltpu.VMEM((tb, 1), F32), pltpu.VMEM((tb, 1), F32), pltpu.VMEM((tb, V_HEAD), F32)])
    return pl.pallas_call(
        body, name=name, grid_spec=grid_spec,
        out_shape=[_sds((t, nh * V_HEAD), BF16), _sds((nh, t, 1), F32)],
        compiler_params=_cp("parallel", "arbitrary"))(jnp.asarray(qt), jnp.asarray(kt), q, kv, kv, kpe)


def _attn_delta(do, o, *, name):
    t = o.shape[0]
    nh = MLA_HEADS
    tr = _tile(t, 640)

    def body(do_ref, o_ref, d_ref):
        d_ref[0] = jnp.sum(do_ref[...].astype(F32) * o_ref[...].astype(F32), axis=1, keepdims=True)

    blk = pl.BlockSpec((tr, V_HEAD), lambda h, i: (i, h))
    return pl.pallas_call(body, name=name, grid=(nh, t // tr), in_specs=[blk, blk],
                          out_specs=pl.BlockSpec((1, tr, 1), lambda h, i: (h, i, 0)),
                          out_shape=_sds((nh, t, 1), F32), compiler_params=_cp("parallel", "parallel"))(do, o)


def _flash_bwd(q, kv, kpe, do, lse, delta, *, name):
    t = q.shape[0]
    nh = MLA_HEADS
    tb = _tile(t, 640)
    nb = t // tb

    kt = np.asarray([j for j in range(nb) for i in range(j, nb)], np.int32)
    qt = np.asarray([i for j in range(nb) for i in range(j, nb)], np.int32)

    def body(kt_ref, qt_ref, q_ref, kn_ref, v_ref, kp_ref, do_ref, lse_ref, dl_ref,
             dq_ref, dkn_ref, dkp_ref, dv_ref, dk_s, dv_s):
        pair = pl.program_id(1)
        j, i = kt_ref[pair], qt_ref[pair]

        @pl.when(pair == 0)
        def _():
            dq_ref[...] = jnp.zeros((t, QHEAD), F32)

        @pl.when(i == j)
        def _():
            dk_s[...] = jnp.zeros((tb, QHEAD), F32)
            dv_s[...] = jnp.zeros((tb, V_HEAD), F32)

        def step(masked):
            qv = q_ref[...]
            k = jnp.concatenate([kn_ref[...], kp_ref[...]], axis=1)
            s = lax.dot_general(qv, k, NT_DIMS, preferred_element_type=F32)
            if masked:
                s = jnp.where(_visible(i, j, tb), s, NEG_INF)
            p = jnp.exp2(s - lse_ref[0])
            dov = do_ref[...]
            dv_s[...] += lax.dot_general(p.astype(BF16), dov, TN_DIMS, preferred_element_type=F32)
            dp = lax.dot_general(dov, v_ref[...], NT_DIMS, preferred_element_type=F32)
            ds = (p * (dp - dl_ref[0])).astype(BF16)
            dk_s[...] += lax.dot_general(ds, qv, TN_DIMS, preferred_element_type=F32)
            r0 = pl.multiple_of(i * tb, tb)
            dq_ref[pl.ds(r0, tb), :] += jnp.dot(ds, k, preferred_element_type=F32)

        edge = (j == i) | (j == 0)

        @pl.when(edge)
        def _():
            step(True)

        @pl.when(jnp.logical_not(edge))
        def _():
            step(False)

        @pl.when(i == nb - 1)
        def _():
            dkn_ref[...] = (dk_s[:, :LANES] * LN2).astype(BF16)
            dkp_ref[0] = dk_s[:, LANES:] * LN2
            dv_ref[...] = dv_s[...].astype(BF16)

    kj = lambda c: (lambda h, p, kt_r, qt_r: (kt_r[p], c(h)))
    qi = lambda h, p, kt_r, qt_r: (qt_r[p], h)
    qvec = lambda h, p, kt_r, qt_r: (h, qt_r[p], 0)
    grid_spec = pltpu.PrefetchScalarGridSpec(
        num_scalar_prefetch=2, grid=(nh, len(kt)),
        in_specs=[pl.BlockSpec((tb, QHEAD), qi),
                  pl.BlockSpec((tb, LANES), kj(lambda h: h)),
                  pl.BlockSpec((tb, LANES), kj(lambda h: nh + h)),
                  pl.BlockSpec((tb, LANES), kj(lambda h: 0)),
                  pl.BlockSpec((tb, V_HEAD), qi),
                  pl.BlockSpec((1, tb, 1), qvec),
                  pl.BlockSpec((1, tb, 1), qvec)],
        out_specs=[pl.BlockSpec((t, QHEAD), lambda h, p, kt_r, qt_r: (0, h)),
                   pl.BlockSpec((tb, LANES), kj(lambda h: h)),
                   pl.BlockSpec((1, tb, LANES), lambda h, p, kt_r, qt_r: (h, kt_r[p], 0)),
                   pl.BlockSpec((tb, V_HEAD), kj(lambda h: h))],
        scratch_shapes=[pltpu.VMEM((tb, QHEAD), F32), pltpu.VMEM((tb, V_HEAD), F32)])
    return pl.pallas_call(
        body, name=name, grid_spec=grid_spec,
        out_shape=[_sds((t, nh * QHEAD), F32), _sds((t, nh * LANES), BF16), _sds((nh, t, LANES), F32),
                   _sds((t, nh * V_HEAD), BF16)],
        compiler_params=_cp("arbitrary", "arbitrary"))(jnp.asarray(kt), jnp.asarray(qt), q, kv, kv, kpe, do, lse, delta)


def _fill_prev(buf, x_ref, halo_ref, i, tr):
    buf[pl.ds(0, HALO), :] = jnp.where(i > 0, halo_ref[...], 0.0)
    buf[pl.ds(HALO, tr), :] = x_ref[...]


def _conv_prev(buf, w_ref, kw, tr):
    acc = w_ref[kw - 1:kw, :] * buf[pl.ds(HALO, tr), :]
    for k in range(kw - 1):
        acc = acc + w_ref[k:k + 1, :] * buf[pl.ds(HALO - kw + 1 + k, tr), :]
    return acc


def _conv_dw(buf, dc, kw, tr):
    rows = [jnp.sum(dc * buf[pl.ds(HALO - kw + 1 + k, tr), :], axis=0, keepdims=True) for k in range(kw)]
    return jnp.concatenate(rows, axis=0)


def _conv_next(buf, dc_ref, halo_ref, w_ref, kw, i, n_tiles, tr):
    buf[pl.ds(0, tr), :] = dc_ref[...]
    buf[pl.ds(tr, HALO), :] = jnp.where(i < n_tiles - 1, halo_ref[...], 0.0)
    acc = w_ref[kw - 1:kw, :] * buf[pl.ds(0, tr), :]
    for k in range(kw - 1):
        acc = acc + w_ref[k:k + 1, :] * buf[pl.ds(kw - 1 - k, tr), :]
    return acc


def _split3(x):
    x1 = x.astype(BF16)
    r1 = x - x1.astype(F32)
    x2 = r1.astype(BF16)
    x3 = (r1 - x2.astype(F32)).astype(BF16)
    return x1, x2, x3


def _dot3(parts, m, left):
    tot = None
    for p in parts:
        r = jnp.dot(m, p, preferred_element_type=F32) if left else jnp.dot(p, m, preferred_element_type=F32)
        tot = r if tot is None else tot + r
    return tot


def _ssd_prep_fwd(proj, conv_w, conv_b, dt_bias, expand, *, name):
    t = proj.shape[0]
    tr = _tile(t, 128)
    nt = t // tr
    hb = tr // HALO
    cw = SSD_CONV_DIM
    cb_x = OFF_XBC // cw
    cb_dt = OFF_DT // LANES

    def body(x_ref, halo_ref, dtr_ref, w_ref, b_ref, dtb_ref, e_ref, xs_ref, bm_ref, cm_ref, dtx_ref, buf):
        i = pl.program_id(0)
        _fill_prev(buf, x_ref, halo_ref, i, tr)
        conv = _conv_prev(buf, w_ref, SSD_CONV, tr) + b_ref[...]
        rows = _row_ids(i, tr, (tr, 1))
        live = rows >= PAD
        act = jnp.where(live, conv * _sigmoid(conv), 0.0)
        xs_ref[...] = act[:, :SSD_INNER]
        bm_ref[...] = act[:, SSD_INNER:SSD_INNER + GROUP_W]
        cm_ref[...] = act[:, SSD_INNER + GROUP_W:]
        dt = jnp.where(live, jax.nn.softplus(dtr_ref[...] + dtb_ref[...]), 0.0)
        dtx_ref[...] = _dot3(_split3(dt), e_ref[...], left=False)

    return pl.pallas_call(
        body, name=name, grid=(nt,),
        in_specs=[pl.BlockSpec((tr, cw), lambda i: (i, cb_x)),
                  pl.BlockSpec((HALO, cw), lambda i: (jnp.maximum(i * hb - 1, 0), cb_x)),
                  pl.BlockSpec((tr, LANES), lambda i: (i, cb_dt)),
                  pl.BlockSpec((SSD_CONV, cw), lambda i: (0, 0)),
                  pl.BlockSpec((1, cw), lambda i: (0, 0)),
                  pl.BlockSpec((1, LANES), lambda i: (0, 0)),
                  pl.BlockSpec((LANES, SSD_INNER), lambda i: (0, 0))],
        out_specs=[pl.BlockSpec((tr, SSD_INNER), lambda i: (i, 0)), pl.BlockSpec((tr, GROUP_W), lambda i: (i, 0)),
                   pl.BlockSpec((tr, GROUP_W), lambda i: (i, 0)), pl.BlockSpec((tr, SSD_INNER), lambda i: (i, 0))],
        out_shape=[_sds((t, SSD_INNER), F32), _sds((t, GROUP_W), F32), _sds((t, GROUP_W), F32),
                   _sds((t, SSD_INNER), F32)],
        scratch_shapes=[pltpu.VMEM((tr + HALO, cw), F32)],
        compiler_params=_cp("parallel"))(proj, proj, proj, conv_w, conv_b, dt_bias, expand)


def _ssd_prep_bwd_a(proj, dxs, dbm, dcm, ddtx, conv_w, conv_b, dt_bias, reduce_m, *, name):
    t = proj.shape[0]
    tr = _tile(t, 128)
    nt = t // tr
    hb = tr // HALO
    cw = SSD_CONV_DIM
    cb_x = OFF_XBC // cw
    cb_dt = OFF_DT // LANES

    def body(x_ref, halo_ref, dtr_ref, dxs_ref, dbm_ref, dcm_ref, ddtx_ref, w_ref, b_ref, dtb_ref, r_ref,
             dconv_ref, ddtr_ref, dw_ref, db_ref, ddtb_ref, buf):
        i = pl.program_id(0)
        _fill_prev(buf, x_ref, halo_ref, i, tr)
        conv = _conv_prev(buf, w_ref, SSD_CONV, tr) + b_ref[...]
        rows = _row_ids(i, tr, (tr, 1))
        live = rows >= PAD
        sg = _sigmoid(conv)
        dact = jnp.concatenate([dxs_ref[...], dbm_ref[...], dcm_ref[...]], axis=1)
        dconv = jnp.where(live, dact * (sg * (1.0 + conv * (1.0 - sg))), 0.0)
        dconv_ref[...] = dconv
        pw = _conv_dw(buf, dconv, SSD_CONV, tr)
        pb = jnp.sum(dconv, axis=0, keepdims=True)
        ddt = _dot3(_split3(ddtx_ref[...]), r_ref[...], left=False)
        ddtr = jnp.where(live, ddt * _sigmoid(dtr_ref[...] + dtb_ref[...]), 0.0)
        ddtr_ref[...] = ddtr.astype(BF16)
        pdb = jnp.sum(ddtr, axis=0, keepdims=True)

        @pl.when(i == 0)
        def _():
            dw_ref[...] = pw
            db_ref[...] = pb
            ddtb_ref[...] = pdb

        @pl.when(i > 0)
        def _():
            dw_ref[...] += pw
            db_ref[...] += pb
            ddtb_ref[...] += pdb

    return pl.pallas_call(
        body, name=name, grid=(nt,),
        in_specs=[pl.BlockSpec((tr, cw), lambda i: (i, cb_x)),
                  pl.BlockSpec((HALO, cw), lambda i: (jnp.maximum(i * hb - 1, 0), cb_x)),
                  pl.BlockSpec((tr, LANES), lambda i: (i, cb_dt)),
                  pl.BlockSpec((tr, SSD_INNER), lambda i: (i, 0)),
                  pl.BlockSpec((tr, GROUP_W), lambda i: (i, 0)),
                  pl.BlockSpec((tr, GROUP_W), lambda i: (i, 0)),
                  pl.BlockSpec((tr, SSD_INNER), lambda i: (i, 0)),
                  pl.BlockSpec((SSD_CONV, cw), lambda i: (0, 0)),
                  pl.BlockSpec((1, cw), lambda i: (0, 0)),
                  pl.BlockSpec((1, LANES), lambda i: (0, 0)),
                  pl.BlockSpec((SSD_INNER, LANES), lambda i: (0, 0))],
        out_specs=[pl.BlockSpec((tr, cw), lambda i: (i, 0)), pl.BlockSpec((tr, LANES), lambda i: (i, 0)),
                   pl.BlockSpec((SSD_CONV, cw), lambda i: (0, 0)), pl.BlockSpec((1, cw), lambda i: (0, 0)),
                   pl.BlockSpec((1, LANES), lambda i: (0, 0))],
        out_shape=[_sds((t, cw), F32), _sds((t, LANES), BF16), _sds((SSD_CONV, cw), F32), _sds((1, cw), F32),
                   _sds((1, LANES), F32)],
        scratch_shapes=[pltpu.VMEM((tr + HALO, cw), F32)],
        compiler_params=_cp("arbitrary"))(proj, proj, proj, dxs, dbm, dcm, ddtx, conv_w, conv_b, dt_bias, reduce_m)


def _conv_bwd_input(dconv, w, kw, *, name, out_dtype=BF16, tc=None):
    t, c = dconv.shape
    tr = _tile(t, 128)
    nt = t // tr
    hb = tr // HALO
    tc = _tile(c, tc or c)
    last_hb = t // HALO - 1

    def body(dc_ref, halo_ref, w_ref, o_ref, buf):
        i = pl.program_id(0)
        o_ref[...] = _conv_next(buf, dc_ref, halo_ref, w_ref, kw, i, nt, tr).astype(out_dtype)

    return pl.pallas_call(
        body, name=name, grid=(nt, c // tc),
        in_specs=[pl.BlockSpec((tr, tc), lambda i, j: (i, j)),
                  pl.BlockSpec((HALO, tc), lambda i, j: (jnp.minimum((i + 1) * hb, last_hb), j)),
                  pl.BlockSpec((kw, tc), lambda i, j: (0, j))],
        out_specs=pl.BlockSpec((tr, tc), lambda i, j: (i, j)), out_shape=_sds((t, c), out_dtype),
        scratch_shapes=[pltpu.VMEM((tr + HALO, tc), F32)],
        compiler_params=_cp("parallel", "parallel"))(dconv, dconv, w)


def _ffn_act_fwd(ug, uv, wg, wv, bg, bv, *, name):
    t, c = ug.shape
    tr = _tile(t, 128)
    hb = tr // HALO
    tc = _tile(c, 1408)

    def body(ug_ref, hg_ref, uv_ref, hv_ref, wg_ref, wv_ref, bg_ref, bv_ref, o_ref, bufg, bufv):
        i = pl.program_id(0)
        _fill_prev(bufg, ug_ref, hg_ref, i, tr)
        _fill_prev(bufv, uv_ref, hv_ref, i, tr)
        cg = _conv_prev(bufg, wg_ref, FFN_CONV, tr) + bg_ref[...]
        cv = _conv_prev(bufv, wv_ref, FFN_CONV, tr) + bv_ref[...]
        o_ref[...] = (cg * _sigmoid(cg) * cv).astype(BF16)

    blk = pl.BlockSpec((tr, tc), lambda i, j: (i, j))
    halo = pl.BlockSpec((HALO, tc), lambda i, j: (jnp.maximum(i * hb - 1, 0), j))
    wsp = pl.BlockSpec((FFN_CONV, tc), lambda i, j: (0, j))
    bsp = pl.BlockSpec((1, tc), lambda i, j: (0, j))
    return pl.pallas_call(
        body, name=name, grid=(t // tr, c // tc), in_specs=[blk, halo, blk, halo, wsp, wsp, bsp, bsp],
        out_specs=blk, out_shape=_sds((t, c), BF16),
        scratch_shapes=[pltpu.VMEM((tr + HALO, tc), F32), pltpu.VMEM((tr + HALO, tc), F32)],
        compiler_params=_cp("parallel", "parallel"))(ug, ug, uv, uv, wg, wv, bg, bv)


def _ffn_act_bwd(ug, uv, dact, wg, wv, bg, bv, *, name):
    t, c = ug.shape
    tr = _tile(t, 128)
    hb = tr // HALO
    tc = _tile(c, 1408)

    def body(ug_ref, hg_ref, uv_ref, hv_ref, da_ref, wg_ref, wv_ref, bg_ref, bv_ref,
             dcg_ref, dcv_ref, dwg_ref, dwv_ref, dbg_ref, dbv_ref, bufg, bufv):
        i = pl.program_id(1)
        _fill_prev(bufg, ug_ref, hg_ref, i, tr)
        _fill_prev(bufv, uv_ref, hv_ref, i, tr)
        cg = _conv_prev(bufg, wg_ref, FFN_CONV, tr) + bg_ref[...]
        cv = _conv_prev(bufv, wv_ref, FFN_CONV, tr) + bv_ref[...]
        sg = _sigmoid(cg)
        da = da_ref[...]
        dcg = da * cv * (sg * (1.0 + cg * (1.0 - sg)))
        dcv = da * (cg * sg)
        dcg_ref[...] = dcg
        dcv_ref[...] = dcv
        pwg = _conv_dw(bufg, dcg, FFN_CONV, tr)
        pwv = _conv_dw(bufv, dcv, FFN_CONV, tr)
        pbg = jnp.sum(dcg, axis=0, keepdims=True)
        pbv = jnp.sum(dcv, axis=0, keepdims=True)

        @pl.when(i == 0)
        def _():
            dwg_ref[...] = pwg
            dwv_ref[...] = pwv
            dbg_ref[...] = pbg
            dbv_ref[...] = pbv

        @pl.when(i > 0)
        def _():
            dwg_ref[...] += pwg
            dwv_ref[...] += pwv
            dbg_ref[...] += pbg
            dbv_ref[...] += pbv

    blk = pl.BlockSpec((tr, tc), lambda j, i: (i, j))
    halo = pl.BlockSpec((HALO, tc), lambda j, i: (jnp.maximum(i * hb - 1, 0), j))
    wsp = pl.BlockSpec((FFN_CONV, tc), lambda j, i: (0, j))
    bsp = pl.BlockSpec((1, tc), lambda j, i: (0, j))
    return pl.pallas_call(
        body, name=name, grid=(c // tc, t // tr), in_specs=[blk, halo, blk, halo, blk, wsp, wsp, bsp, bsp],
        out_specs=[blk, blk, wsp, wsp, bsp, bsp],
        out_shape=[_sds((t, c), F32), _sds((t, c), F32), _sds((FFN_CONV, c), F32), _sds((FFN_CONV, c), F32),
                   _sds((1, c), F32), _sds((1, c), F32)],
        scratch_shapes=[pltpu.VMEM((tr + HALO, tc), F32), pltpu.VMEM((tr + HALO, tc), F32)],
        compiler_params=_cp("parallel", "arbitrary"))(ug, ug, uv, uv, dact, wg, wv, bg, bv)


def _tri(lower):
    li = lax.broadcasted_iota(jnp.int32, (CHUNK, CHUNK), 0)
    si = lax.broadcasted_iota(jnp.int32, (CHUNK, CHUNK), 1)
    return li >= si if lower else li <= si


def _tri_ones(lower):
    return jnp.where(_tri(lower), 1.0, 0.0).astype(BF16)


def _decay_pair(acs, acs_t, lane0):
    col = acs[:, lane0:lane0 + 1]
    row = acs_t[lane0:lane0 + 1, :]
    low = jnp.where(_tri(True), jnp.exp(jnp.minimum(col - row, 0.0)), 0.0)
    upp = jnp.where(_tri(False), jnp.exp(jnp.minimum(row - col, 0.0)), 0.0)
    return low, upp


def _ssd_fwd(xs, dtx, bm, cm, bm_t, a_x, d_x, *, name):
    t = xs.shape[0]
    nc = t // CHUNK
    gw = GROUP_W

    def body(xs_ref, dt_ref, b_ref, c_ref, bt_ref, a_ref, d_ref, y_ref, prev_ref, h_s):
        @pl.when(pl.program_id(1) == 0)
        def _():
            h_s[...] = jnp.zeros((SSD_STATE, gw), F32)

        x = xs_ref[...]
        dt = dt_ref[...]
        acs = _dot3(_split3(dt * a_ref[...]), _tri_ones(True), left=True)
        acs_t = acs.T
        xc = x * dt
        bv = b_ref[...].astype(BF16)
        cv = c_ref[...].astype(BF16)
        cb = lax.dot_general(cv, bv, NT_DIMS, preferred_element_type=F32)
        lane = lax.broadcasted_iota(jnp.int32, (CHUNK, LANES), 1)
        pieces = []
        for pp in range(gw // LANES):
            xcp = xc[:, pp * LANES:(pp + 1) * LANES]
            acc = jnp.zeros((CHUNK, LANES), F32)
            for e in range(2):
                low, _ = _decay_pair(acs, acs_t, pp * LANES + e * SSD_HEAD_DIM)
                mine = (lane >= e * SSD_HEAD_DIM) & (lane < (e + 1) * SSD_HEAD_DIM)
                xm = jnp.where(mine, xcp, 0.0).astype(BF16)
                acc = acc + jnp.dot((cb * low).astype(BF16), xm, preferred_element_type=F32)
            pieces.append(acc)
        y_diag = jnp.concatenate(pieces, axis=1)
        h_prev = h_s[...]
        y_off = jnp.dot(cv, h_prev.astype(BF16), preferred_element_type=F32) * jnp.exp(acs)
        y_ref[...] = y_diag + y_off + d_ref[...] * x
        prev_ref[0] = h_prev
        last = acs[CHUNK - 1:CHUNK, :]
        w = jnp.exp(last - acs)
        st = jnp.dot(bt_ref[...].astype(BF16), (xc * w).astype(BF16), preferred_element_type=F32)
        h_s[...] = h_prev * jnp.exp(last) + st

    tok = pl.BlockSpec((CHUNK, gw), lambda g, c: (c, g))
    grp = pl.BlockSpec((CHUNK, SSD_STATE), lambda g, c: (c, g))
    vec = pl.BlockSpec((1, gw), lambda g, c: (0, g))
    return pl.pallas_call(
        body, name=name, grid=(SSD_GROUPS, nc),
        in_specs=[tok, tok, grp, grp, pl.BlockSpec((SSD_STATE, CHUNK), lambda g, c: (g, c)), vec, vec],
        out_specs=[tok, pl.BlockSpec((1, SSD_STATE, gw), lambda g, c: (c, 0, g))],
        out_shape=[_sds((t, SSD_INNER), F32), _sds((nc, SSD_STATE, SSD_INNER), F32)],
        scratch_shapes=[pltpu.VMEM((SSD_STATE, gw), F32)],
        compiler_params=_cp("parallel", "arbitrary"))(xs, dtx, bm, cm, bm_t, a_x, d_x)


def _ssd_bwd(xs, dtx, bm, cm, cm_t, prev, dy, a_x, d_x, *, name):
    t = xs.shape[0]
    nc = t // CHUNK
    gw = GROUP_W

    def body(xs_ref, dt_ref, b_ref, c_ref, ct_ref, prev_ref, dy_ref, a_ref, d_ref,
             dxs_ref, ddt_ref, db_ref, dc_ref, da_ref, dd_ref, g_s):
        first = pl.program_id(1) == 0

        @pl.when(first)
        def _():
            g_s[...] = jnp.zeros((SSD_STATE, gw), F32)

        x = xs_ref[...]
        dt = dt_ref[...]
        a = a_ref[...]
        dyv = dy_ref[...]
        acs = _dot3(_split3(dt * a), _tri_ones(True), left=True)
        acs_t = acs.T
        xc = x * dt
        bv = b_ref[...].astype(BF16)
        cv = c_ref[...].astype(BF16)
        cb = lax.dot_general(cv, bv, NT_DIMS, preferred_element_type=F32)
        cb_t = lax.dot_general(bv, cv, NT_DIMS, preferred_element_type=F32)
        last = acs[CHUNK - 1:CHUNK, :]
        w = jnp.exp(last - acs)
        cd = jnp.exp(last)
        p_in = prev_ref[0]
        p_b = p_in.astype(BF16)
        g_out = g_s[...]
        g_b = g_out.astype(BF16)
        dy_e = dyv * jnp.exp(acs)
        dy_eb = dy_e.astype(BF16)
        y_off_raw = jnp.dot(cv, p_b, preferred_element_type=F32)
        dacs = dy_e * y_off_raw
        d_c = lax.dot_general(dy_eb, p_b, NT_DIMS, preferred_element_type=F32)
        d_prev = jnp.dot(ct_ref[...].astype(BF16), dy_eb, preferred_element_type=F32)
        q_l = jnp.dot(bv, g_b, preferred_element_type=F32)
        dxc = w * q_l
        tw = xc * q_l * w
        dacs = dacs - tw
        d_b = lax.dot_general((xc * w).astype(BF16), g_b, NT_DIMS, preferred_element_type=F32)
        last_add = jnp.sum(tw, axis=0, keepdims=True) + cd * jnp.sum(g_out * p_in, axis=0, keepdims=True)
        g_s[...] = cd * g_out + d_prev
        lane = lax.broadcasted_iota(jnp.int32, (CHUNK, LANES), 1)
        d_cb = jnp.zeros((CHUNK, CHUNK), F32)
        d_cb_t = jnp.zeros((CHUNK, CHUNK), F32)
        dxc_pieces, dacs_pieces = [], []
        for pp in range(gw // LANES):
            xcp = xc[:, pp * LANES:(pp + 1) * LANES]
            dyp = dyv[:, pp * LANES:(pp + 1) * LANES]
            dxcp = jnp.zeros((CHUNK, LANES), F32)
            dacsp = jnp.zeros((CHUNK, LANES), F32)
            for e in range(2):
                low, upp = _decay_pair(acs, acs_t, pp * LANES + e * SSD_HEAD_DIM)
                mine = (lane >= e * SSD_HEAD_DIM) & (lane < (e + 1) * SSD_HEAD_DIM)
                m_low = cb * low
                m_upp = cb_t * upp
                dym = jnp.where(mine, dyp, 0.0).astype(BF16)
                xm = jnp.where(mine, xcp, 0.0).astype(BF16)
                dxcp = dxcp + jnp.dot(m_upp.astype(BF16), dym, preferred_element_type=F32)
                d_m = lax.dot_general(dym, xm, NT_DIMS, preferred_element_type=F32)
                d_m_t = lax.dot_general(xm, dym, NT_DIMS, preferred_element_type=F32)
                rs = jnp.sum(d_m * m_low, axis=1, keepdims=True)
                cs = jnp.sum(d_m_t * m_upp, axis=1, keepdims=True)
                dacsp = dacsp + jnp.where(lane == e * SSD_HEAD_DIM, rs - cs, 0.0)
                d_cb = d_cb + d_m * low
                d_cb_t = d_cb_t + d_m_t * upp
            dxc_pieces.append(dxcp)
            dacs_pieces.append(dacsp)
        dxc = dxc + jnp.concatenate(dxc_pieces, axis=1)
        dacs = dacs + jnp.concatenate(dacs_pieces, axis=1)
        rowi = lax.broadcasted_iota(jnp.int32, (CHUNK, gw), 0)
        dacs = dacs + jnp.where(rowi == CHUNK - 1, last_add, 0.0)
        dc_ref[...] = d_c + jnp.dot(d_cb.astype(BF16), bv, preferred_element_type=F32)
        db_ref[...] = d_b + jnp.dot(d_cb_t.astype(BF16), cv, preferred_element_type=F32)
        dda = _dot3(_split3(dacs), _tri_ones(False), left=True)
        ddt_ref[...] = dda * a + dxc * x
        dxs_ref[...] = dxc * dt + d_ref[...] * dyv
        pa = jnp.sum(dda * dt, axis=0, keepdims=True)
        pd = jnp.sum(dyv * x, axis=0, keepdims=True)

        @pl.when(first)
        def _():
            da_ref[...] = pa
            dd_ref[...] = pd

        @pl.when(jnp.logical_not(first))
        def _():
            da_ref[...] += pa
            dd_ref[...] += pd

    rc = lambda c: nc - 1 - c
    tok = pl.BlockSpec((CHUNK, gw), lambda g, c: (rc(c), g))
    grp = pl.BlockSpec((CHUNK, SSD_STATE), lambda g, c: (rc(c), g))
    vec = pl.BlockSpec((1, gw), lambda g, c: (0, g))
    return pl.pallas_call(
        body, name=name, grid=(SSD_GROUPS, nc),
        in_specs=[tok, tok, grp, grp, pl.BlockSpec((SSD_STATE, CHUNK), lambda g, c: (g, rc(c))),
                  pl.BlockSpec((1, SSD_STATE, gw), lambda g, c: (rc(c), 0, g)), tok, vec, vec],
        out_specs=[tok, tok, grp, grp, vec, vec],
        out_shape=[_sds((t, SSD_INNER), F32), _sds((t, SSD_INNER), F32), _sds((t, gw), F32), _sds((t, gw), F32),
                   _sds((1, SSD_INNER), F32), _sds((1, SSD_INNER), F32)],
        scratch_shapes=[pltpu.VMEM((SSD_STATE, gw), F32)],
        compiler_params=_cp("parallel", "arbitrary"))(xs, dtx, bm, cm, cm_t, prev, dy, a_x, d_x)


def _gnorm_fwd(y, proj, g, *, name):
    t = y.shape[0]
    tr = _tile(t, 640)
    zb = OFF_Z // GROUP_W

    def body(y_ref, z_ref, g_ref, o_ref):
        z = z_ref[...]
        v = y_ref[...] * (z * _sigmoid(z))
        r = lax.rsqrt(jnp.mean(v * v, axis=1, keepdims=True) + RMS_EPS)
        o_ref[...] = (v * r * g_ref[...]).astype(BF16)

    blk = pl.BlockSpec((tr, GROUP_W), lambda i, j: (i, j))
    return pl.pallas_call(
        body, name=name, grid=(t // tr, SSD_GROUPS),
        in_specs=[blk, pl.BlockSpec((tr, GROUP_W), lambda i, j: (i, zb + j)),
                  pl.BlockSpec((1, GROUP_W), lambda i, j: (0, j))],
        out_specs=blk, out_shape=_sds((t, SSD_INNER), BF16),
        compiler_params=_cp("parallel", "parallel"))(y, proj, g)


def _gnorm_bwd(dout, y, proj, g, *, name):
    t = y.shape[0]
    tr = _tile(t, 640)
    zb = OFF_Z // GROUP_W

    def body(do_ref, y_ref, z_ref, g_ref, dy_ref, dz_ref, dg_ref):
        i = pl.program_id(1)
        z = z_ref[...]
        yv = y_ref[...]
        sg = _sigmoid(z)
        sz = z * sg
        v = yv * sz
        r = lax.rsqrt(jnp.mean(v * v, axis=1, keepdims=True) + RMS_EPS)
        dov = do_ref[...].astype(F32)
        gdo = dov * g_ref[...]
        m = jnp.mean(v * gdo, axis=1, keepdims=True)
        dv = r * gdo - v * (r * r * r) * m
        dy_ref[...] = dv * sz
        dz_ref[...] = (dv * yv * (sg * (1.0 + z * (1.0 - sg)))).astype(BF16)
        pg = jnp.sum(dov * v * r, axis=0, keepdims=True)

        @pl.when(i == 0)
        def _():
            dg_ref[...] = pg

        @pl.when(i > 0)
        def _():
            dg_ref[...] += pg

    blk = pl.BlockSpec((tr, GROUP_W), lambda j, i: (i, j))
    vec = pl.BlockSpec((1, GROUP_W), lambda j, i: (0, j))
    return pl.pallas_call(
        body, name=name, grid=(SSD_GROUPS, t // tr),
        in_specs=[blk, blk, pl.BlockSpec((tr, GROUP_W), lambda j, i: (i, zb + j)), vec],
        out_specs=[blk, blk, vec],
        out_shape=[_sds((t, SSD_INNER), F32), _sds((t, SSD_INNER), BF16), _sds((1, SSD_INNER), F32)],
        compiler_params=_cp("parallel", "arbitrary"))(dout, y, proj, g)


def _mix_fwd(proj, ya, ys, *, name):
    t, d = ya.shape
    tr = _tile(t, 640)
    ba, bs = OFF_GA // d, OFF_GS // d

    def body(ga_ref, gs_ref, ya_ref, ys_ref, o_ref):
        o_ref[...] = (_sigmoid(ga_ref[...]) * ya_ref[...] + _sigmoid(gs_ref[...]) * ys_ref[...]).astype(BF16)

    blk = pl.BlockSpec((tr, d), lambda i: (i, 0))
    return pl.pallas_call(
        body, name=name, grid=(t // tr,),
        in_specs=[pl.BlockSpec((tr, d), lambda i: (i, ba)), pl.BlockSpec((tr, d), lambda i: (i, bs)), blk, blk],
        out_specs=blk, out_shape=_sds((t, d), BF16), compiler_params=_cp("parallel"))(proj, proj, ya, ys)


def _mix_bwd(dmix, proj, ya, ys, *, name):
    t, d = ya.shape
    tr = _tile(t, 640)
    ba, bs = OFF_GA // d, OFF_GS // d

    def body(dm_ref, ga_ref, gs_ref, ya_ref, ys_ref, dya_ref, dys_ref, dga_ref, dgs_ref):
        dm = dm_ref[...]
        sa = _sigmoid(ga_ref[...])
        ss = _sigmoid(gs_ref[...])
        dya_ref[...] = (sa * dm).astype(BF16)
        dys_ref[...] = (ss * dm).astype(BF16)
        dga_ref[...] = (dm * ya_ref[...] * sa * (1.0 - sa)).astype(BF16)
        dgs_ref[...] = (dm * ys_ref[...] * ss * (1.0 - ss)).astype(BF16)

    blk = pl.BlockSpec((tr, d), lambda i: (i, 0))
    return pl.pallas_call(
        body, name=name, grid=(t // tr,),
        in_specs=[blk, pl.BlockSpec((tr, d), lambda i: (i, ba)), pl.BlockSpec((tr, d), lambda i: (i, bs)), blk, blk],
        out_specs=[blk] * 4, out_shape=[_sds((t, d), BF16)] * 4,
        compiler_params=_cp("parallel"))(dmix, proj, proj, ya, ys)


def _loss_grad(h, target, *, name):
    t, d = h.shape
    tr = LANES
    assert ROW0 == tr

    def body(h_ref, t_ref, dh_ref, loss_ref):
        i = pl.program_id(0)

        @pl.when(i == 0)
        def _():
            dh_ref[...] = jnp.zeros((tr, d), F32)
            loss_ref[...] = jnp.zeros((1, LANES), F32)

        @pl.when(i > 0)
        def _():
            err = h_ref[...] - t_ref[...]
            dh_ref[...] = err * (1.0 / d)
            part = jnp.sum(jnp.sum(err * err, axis=1, keepdims=True), axis=0, keepdims=True)
            loss_ref[...] += jnp.broadcast_to(part * (0.5 / d), (1, LANES))

    blk = pl.BlockSpec((tr, d), lambda i: (i, 0))
    return pl.pallas_call(
        body, name=name, grid=(t // tr,),
        in_specs=[blk, pl.BlockSpec((tr, d), lambda i: (jnp.maximum(i - 1, 0), 0))],
        out_specs=[blk, pl.BlockSpec((1, LANES), lambda i: (0, 0))],
        out_shape=[_sds((t, d), F32), _sds((1, LANES), F32)],
        compiler_params=_cp("arbitrary"))(h, target)


def _adamw_update(gv, wv, mv, vv):
    c1 = 1.0 - ADAM_B1 ** ADAM_STEP
    c2 = 1.0 - ADAM_B2 ** ADAM_STEP
    nm = ADAM_B1 * mv + (1.0 - ADAM_B1) * gv
    nv = ADAM_B2 * vv + (1.0 - ADAM_B2) * (gv * gv)
    return -ADAM_LR * ((nm / c1) / (jnp.sqrt(nv / c2) + ADAM_EPS) + ADAM_WD * wv), nm, nv


def _as_2d(a):
    return a.reshape(1, -1) if a.ndim == 1 else a.reshape(-1, a.shape[-1])


def _adamw(g, w, m, v, *, name):
    shape = w.shape
    g2, w2, m2, v2 = (_as_2d(a) for a in (g, w, m, v))
    r, c = w2.shape
    tr = _tile(r, 256, base=8)

    def body(g_ref, w_ref, m_ref, v_ref, d_ref, nm_ref, nv_ref):
        d_ref[...], nm_ref[...], nv_ref[...] = _adamw_update(g_ref[...], w_ref[...], m_ref[...], v_ref[...])

    blk = pl.BlockSpec((tr, c), lambda i: (i, 0))
    outs = pl.pallas_call(body, name=name, grid=(r // tr,), in_specs=[blk] * 4, out_specs=[blk] * 3,
                          out_shape=[_sds((r, c), F32)] * 3, compiler_params=_cp("parallel"))(g2, w2, m2, v2)
    return [o.reshape(shape) for o in outs]


def _adamw_small(items, *, name):
    n = len(items)
    shapes = [it[1].shape for it in items]
    flat = [_as_2d(a) for it in items for a in it]

    def body(*refs):
        ins, outs = refs[:4 * n], refs[4 * n:]
        for k in range(n):
            g_ref, w_ref, m_ref, v_ref = ins[4 * k:4 * k + 4]
            d_ref, nm_ref, nv_ref = outs[3 * k:3 * k + 3]
            d_ref[...], nm_ref[...], nv_ref[...] = _adamw_update(g_ref[...], w_ref[...], m_ref[...], v_ref[...])

    out_shape = [_sds(flat[4 * k + 1].shape, F32) for k in range(n) for _ in range(3)]
    outs = pl.pallas_call(body, name=name, out_shape=out_shape,
                          compiler_params=pltpu.CompilerParams(vmem_limit_bytes=VMEM_LIMIT_BYTES))(*flat)
    return [[outs[3 * k + q].reshape(shapes[k]) for q in range(3)] for k in range(n)]


def _pair_add(g5, got, core, *, name):
    n, _, r, _ = g5.shape
    tr = _tile(r, 1024, base=8)

    def body(c_ref, a_ref, b_ref, o_ref):
        o_ref[...] = a_ref[0] + b_ref[...]

    grid_spec = pltpu.PrefetchScalarGridSpec(
        num_scalar_prefetch=1, grid=(n, r // tr),
        in_specs=[pl.BlockSpec((1, 1, tr, LANES), lambda s, i, c_ref: (s, c_ref[0], i, 0)),
                  pl.BlockSpec((1, tr, LANES), lambda s, i, c_ref: (s, i, 0))],
        out_specs=pl.BlockSpec((1, tr, LANES), lambda s, i, c_ref: (s, i, 0)))
    return pl.pallas_call(body, name=name, grid_spec=grid_spec, out_shape=_sds(got.shape, F32),
                          compiler_params=_cp("parallel", "parallel"))(core, g5, got)


def _sum_chips(q, core, *, name):
    n, r, _ = q.shape
    tr = _tile(r, 1024, base=8)

    def body(c_ref, q_ref, o_ref):
        tot = q_ref[0]
        for s in range(1, n):
            tot = tot + q_ref[s]
        o_ref[0] = tot

    grid_spec = pltpu.PrefetchScalarGridSpec(
        num_scalar_prefetch=1, grid=(r // tr,),
        in_specs=[pl.BlockSpec((n, tr, LANES), lambda i, c_ref: (0, i, 0))],
        out_specs=pl.BlockSpec((1, tr, LANES), lambda i, c_ref: (c_ref[0], i, 0)))
    return pl.pallas_call(body, name=name, grid_spec=grid_spec, out_shape=_sds((2, r, LANES), F32),
                          compiler_params=_cp("parallel"))(core, q)


_ANY = pl.BlockSpec(memory_space=pl.ANY)
_MESH = pl.DeviceIdType.MESH


def _place():
    x, y, c = lax.axis_index("x"), lax.axis_index("y"), lax.axis_index("c")
    return x, y, c, [(1 - x, y), (x, 1 - y), (1 - x, 1 - y)]


def _chip_allgather(mine, *, name):
    na = len(mine)

    def body(*refs):
        x_refs, o_refs = refs[:na], refs[na:2 * na]
        send_sems, recv_sems, local_sems = refs[2 * na:]
        x, y, c, chips = _place()
        k = 2 * x + y

        def copy(a, n, src, dst, to):
            return pltpu.make_async_remote_copy(src_ref=src, dst_ref=dst, send_sem=send_sems.at[6 * a + n],
                                                recv_sem=recv_sems.at[6 * a + n], device_id=to, device_id_type=_MESH)

        locals_ = [pltpu.make_async_copy(x_refs[a], o_refs[a].at[k], local_sems.at[a]) for a in range(na)]
        for cp in locals_:
            cp.start()
        sends = [copy(a, n, x_refs[a].at[c], o_refs[a].at[k, c], (cx, cy, c))
                 for a in range(na) for n, (cx, cy) in enumerate(chips)]
        for cp in sends:
            cp.start()
        passed = []
        for a in range(na):
            for n, (cx, cy) in enumerate(chips):
                slab = o_refs[a].at[2 * cx + cy, c]
                copy(a, n, slab, slab, (cx, cy, c)).wait_recv()
                fw = copy(a, 3 + n, slab, slab, (x, y, 1 - c))
                fw.start()
                passed.append(fw)
        for a in range(na):
            for n, (cx, cy) in enumerate(chips):
                slab = o_refs[a].at[2 * cx + cy, 1 - c]
                copy(a, 3 + n, slab, slab, (x, y, 1 - c)).wait_recv()
        for cp in sends + passed:
            cp.wait_send()
        for cp in locals_:
            cp.wait()

    return pl.pallas_call(
        body, name=name, in_specs=[_ANY] * na, out_specs=[_ANY] * na,
        out_shape=[_sds((N_CHIPS,) + a.shape, a.dtype) for a in mine],
        scratch_shapes=[pltpu.SemaphoreType.DMA((6 * na,)), pltpu.SemaphoreType.DMA((6 * na,)),
                        pltpu.SemaphoreType.DMA((na,))])(*mine)


def _sibling_swap(g5, *, name):
    n, _, r, _ = g5.shape

    def body(x_ref, o_ref, send_sems, recv_sems):
        x, y, c, _ = _place()
        cps = [pltpu.make_async_remote_copy(src_ref=x_ref.at[s, 1 - c], dst_ref=o_ref.at[s], send_sem=send_sems.at[s],
                                            recv_sem=recv_sems.at[s], device_id=(x, y, 1 - c), device_id_type=_MESH)
               for s in range(n)]
        for cp in cps:
            cp.start()
        for cp in cps:
            cp.wait()

    return pl.pallas_call(
        body, name=name, in_specs=[_ANY], out_specs=_ANY, out_shape=_sds((n, r, LANES), g5.dtype),
        scratch_shapes=[pltpu.SemaphoreType.DMA((n,)), pltpu.SemaphoreType.DMA((n,))])(g5)


def _chip_exchange(h, *, name):
    def body(h_ref, q_ref, send_sems, recv_sems, local_sem):
        x, y, c, chips = _place()
        k = 2 * x + y
        local = pltpu.make_async_copy(h_ref.at[k], q_ref.at[k], local_sem)
        local.start()
        cps = []
        for n, (cx, cy) in enumerate(chips):
            kk = 2 * cx + cy
            cps.append(pltpu.make_async_remote_copy(src_ref=h_ref.at[kk], dst_ref=q_ref.at[k], send_sem=send_sems.at[n],
                                                    recv_sem=recv_sems.at[n], device_id=(cx, cy, c),
                                                    device_id_type=_MESH))
        for cp in cps:
            cp.start()
        for n, (cx, cy) in enumerate(chips):
            kk = 2 * cx + cy
            pltpu.make_async_remote_copy(src_ref=h_ref.at[kk], dst_ref=q_ref.at[kk], send_sem=send_sems.at[n],
                                         recv_sem=recv_sems.at[n], device_id=(cx, cy, c),
                                         device_id_type=_MESH).wait_recv()
        for cp in cps:
            cp.wait_send()
        local.wait()

    return pl.pallas_call(
        body, name=name, in_specs=[_ANY], out_specs=_ANY, out_shape=_sds(h.shape, h.dtype),
        scratch_shapes=[pltpu.SemaphoreType.DMA((3,)), pltpu.SemaphoreType.DMA((3,)), pltpu.SemaphoreType.DMA])(h)


def _sibling_allgather(buf, *, name):
    def body(x_ref, o_ref, send_sem, recv_sem):
        x, y, c, _ = _place()
        cp = pltpu.make_async_remote_copy(src_ref=x_ref.at[c], dst_ref=o_ref.at[c], send_sem=send_sem,
                                          recv_sem=recv_sem, device_id=(x, y, 1 - c), device_id_type=_MESH)
        cp.start()
        pltpu.make_async_remote_copy(src_ref=x_ref.at[c], dst_ref=o_ref.at[1 - c], send_sem=send_sem,
                                     recv_sem=recv_sem, device_id=(x, y, 1 - c), device_id_type=_MESH).wait_recv()
        cp.wait_send()

    return pl.pallas_call(
        body, name=name, in_specs=[_ANY], out_specs=_ANY, out_shape=_sds(buf.shape, buf.dtype),
        input_output_aliases={0: 0},
        scratch_shapes=[pltpu.SemaphoreType.DMA, pltpu.SemaphoreType.DMA])(buf)


def _rows_of(flat, align):
    n = flat.shape[-1]
    rows = -(-n // LANES)
    rows = -(-rows // align) * align
    pad = rows * LANES - n
    if pad:
        flat = jnp.pad(flat, [(0, 0)] * (flat.ndim - 1) + [(0, pad)])
    return flat.reshape(flat.shape[:-1] + (rows, LANES))


def _chip_segments(g, kind):
    if kind == "col":
        n = g.shape[-1] // N_CHIPS
        s = g.reshape(g.shape[:-1] + (N_CHIPS, n))
        return jnp.moveaxis(s, -2, 0).reshape(N_CHIPS, -1)
    k = g.shape[-2] // N_CHIPS
    s = g.reshape(g.shape[:-2] + (N_CHIPS, k, g.shape[-1]))
    return jnp.moveaxis(s, -3, 0).reshape(N_CHIPS, -1)


def _join(blocks, kind):
    if kind == "col":
        s = jnp.moveaxis(blocks, 0, -2)
        return s.reshape(s.shape[:-2] + (s.shape[-2] * s.shape[-1],))
    return blocks.reshape((blocks.shape[0] * blocks.shape[1],) + blocks.shape[2:])


def _gather_weights(local):
    mine = []
    for n, _ in SHARDED:
        a = local[n]
        if n == "w_in":
            a = jnp.swapaxes(a, 1, 2)
        if n == "meta_tokens":
            a = a.reshape(2, N_META // 2, a.shape[-1])
        mine.append(a.astype(BF16) if n in GATHER_BF16 else a)
    got = _chip_allgather(mine, name="gather_weights")
    return {n: g for (n, _), g in zip(SHARDED, got)}


def _rope_tables(t):
    half = QK_ROPE // 2
    inv_freq = 1.0 / (ROPE_THETA ** (jnp.arange(0, QK_ROPE, 2, dtype=F32) / QK_ROPE))
    pos = jnp.maximum(jnp.arange(t, dtype=F32) - PAD, 0.0)
    ang = pos[:, None] * inv_freq[None, :]
    cos, sin = jnp.cos(ang), jnp.sin(ang)
    z = jnp.zeros((t, half), F32)
    z2 = jnp.zeros((t, LANES - QK_ROPE), F32)
    return (jnp.concatenate([cos, cos, z2], axis=1), jnp.concatenate([-sin, z, z2], axis=1),
            jnp.concatenate([z, sin, z2], axis=1))


def _expand_matrix():
    lane = np.arange(SSD_INNER) // SSD_HEAD_DIM
    e = (np.arange(LANES)[:, None] == lane[None, :]).astype(np.float32)
    return jnp.asarray(e, BF16)


def _layer_weights(full, rep, i):
    w = {}
    kinds = dict(SHARDED)
    whole = lambda n: _join(full[n][:, i], kinds[n])
    wt = _join(full["w_in"][:, i], "row")
    zr = lambda n: jnp.zeros((n, D_MODEL), BF16)
    w["w_in_t"] = jnp.concatenate(
        [wt[0:1024], wt[1088:3136], wt[3136:6208], wt[6240:7264], wt[7264:8288],
         wt[1024:1088], zr(LANES - QK_ROPE), wt[6208:6240], zr(LANES - SSD_HEADS)], axis=0)
    wq = whole("w_q_b").reshape(Q_LORA, MLA_HEADS, QK_NOPE + QK_ROPE)
    w["w_q_b"] = jnp.pad(wq, ((0, 0), (0, 0), (0, QHEAD - QK_NOPE - QK_ROPE))).reshape(Q_LORA, MLA_HEADS * QHEAD)
    wkv = whole("w_kv_b").reshape(KV_LORA, MLA_HEADS, 2, QK_NOPE)
    w["w_kv_b"] = jnp.swapaxes(wkv, 1, 2).reshape(KV_LORA, 2 * MLA_HEADS * QK_NOPE)
    for n in ("w_o_attn", "w_o_ssd", "w_out", "w_down", "ssd_conv_w"):
        w[n] = whole(n)
    w_up = whole("w_up")
    w["w_up_g"] = w_up[:, :D_FF]
    w["w_up_v"] = w_up[:, D_FF:]
    ffn_w = whole("ffn_conv_w")
    w["ffn_conv_wg"] = ffn_w[:, :D_FF]
    w["ffn_conv_wv"] = ffn_w[:, D_FF:]
    row = lambda v: v.reshape(1, -1)
    w["q_norm_g"] = row(rep["q_norm_g"][i])
    w["kv_norm_g"] = row(rep["kv_norm_g"][i])
    w["ssd_conv_b"] = row(rep["ssd_conv_b"][i])
    w["dt_bias"] = row(jnp.pad(rep["dt_bias"][i], (0, LANES - SSD_HEADS)))
    a = -jnp.exp(rep["a_log"][i])
    w["a"] = a
    w["a_x"] = row(jnp.repeat(a, SSD_HEAD_DIM))
    w["d_x"] = row(jnp.repeat(rep["d_skip"][i], SSD_HEAD_DIM))
    w["ssd_norm_g"] = row(rep["ssd_norm_g"][i])
    w["ffn_conv_bg"] = row(rep["ffn_conv_b"][i][:D_FF])
    w["ffn_conv_bv"] = row(rep["ffn_conv_b"][i][D_FF:])
    for n in ("ln1_g", "ln1_b", "ln2_g", "ln2_b"):
        w[n] = row(rep[n][i])
    return w


def _layer_fwd(h, w, tabs, expand, tag):
    cos, sin_a, sin_b = tabs
    s = {"h": h}
    proj = _mm(h, w["w_in_t"], trans_b=True, name=tag + "in_proj", tn=768)
    s["proj"] = proj
    qn = _rms_fwd(proj, OFF_Q, Q_LORA, w["q_norm_g"], name=tag + "q_norm")
    q_raw = _mm(qn, w["w_q_b"], name=tag + "q_up")
    q = _rope_q_fwd(q_raw, cos, sin_a, sin_b, name=tag + "q_rope")
    kvn = _rms_fwd(proj, OFF_KV, KV_LORA, w["kv_norm_g"], name=tag + "kv_norm")
    kv = _mm(kvn, w["w_kv_b"], name=tag + "kv_up", out_dtype=BF16)
    kpe = _rope_k_fwd(proj, cos, sin_a, sin_b, name=tag + "k_rope")
    o, lse = _flash_fwd(q, kv, kpe, name=tag + "attn")
    ya = _mm(o, w["w_o_attn"], name=tag + "attn_out")
    s.update(qn=qn, q=q, kvn=kvn, kv=kv, kpe=kpe, o=o, lse=lse, ya=ya)
    xs, bm, cm, dtx = _ssd_prep_fwd(proj, w["ssd_conv_w"], w["ssd_conv_b"], w["dt_bias"], expand, name=tag + "ssd_prep")
    y, prev = _ssd_fwd(xs, dtx, bm, cm, bm.T, w["a_x"], w["d_x"], name=tag + "ssd_scan")
    yn = _gnorm_fwd(y, proj, w["ssd_norm_g"], name=tag + "ssd_norm")
    ys = _mm(yn, w["w_o_ssd"], name=tag + "ssd_out")
    s.update(xs=xs, bm=bm, cm=cm, dtx=dtx, y=y, prev=prev, yn=yn, ys=ys)
    mixed = _mix_fwd(proj, ya, ys, name=tag + "mix")
    br = _mm(mixed, w["w_out"], name=tag + "mix_out")
    pre1, h1 = _ln_fwd(h, br, w["ln1_g"], w["ln1_b"], name=tag + "ln1")
    s.update(mixed=mixed, pre1=pre1, h1=h1)
    ug = _mm(h1, w["w_up_g"], name=tag + "up_g", tn=1408)
    uv = _mm(h1, w["w_up_v"], name=tag + "up_v", tn=1408)
    act = _ffn_act_fwd(ug, uv, w["ffn_conv_wg"], w["ffn_conv_wv"], w["ffn_conv_bg"], w["ffn_conv_bv"],
                       name=tag + "ffn_act")
    ffn = _mm(act, w["w_down"], name=tag + "down")
    pre2, h2 = _ln_fwd(h1, ffn, w["ln2_g"], w["ln2_b"], name=tag + "ln2")
    s.update(ug=ug, uv=uv, act=act, pre2=pre2)
    return h2, s


def _layer_bwd(dh2, w, s, tabs, reduce_m, tag):
    cos, sin_a, sin_b = tabs
    g = {}
    proj = s["proj"]
    dpre2, g["ln2_g"], g["ln2_b"] = _ln_bwd(dh2, s["pre2"], w["ln2_g"], name=tag + "ln2_bwd")
    g["w_down"] = _mm_tn(s["act"], dpre2, name=tag + "down_dw")
    dact = _mm(dpre2, w["w_down"], trans_b=True, name=tag + "down_dx", tn=1408)
    dcg, dcv, dwg, dwv, dbg, dbv = _ffn_act_bwd(s["ug"], s["uv"], dact, w["ffn_conv_wg"], w["ffn_conv_wv"],
                                                w["ffn_conv_bg"], w["ffn_conv_bv"], name=tag + "ffn_act_bwd")
    g["ffn_conv_w"] = jnp.concatenate([dwg, dwv], axis=1)
    g["ffn_conv_b"] = jnp.concatenate([dbg, dbv], axis=1).reshape(-1)
    dug = _conv_bwd_input(dcg, w["ffn_conv_wg"], FFN_CONV, name=tag + "ffn_conv_bwd_g", tc=1408)
    duv = _conv_bwd_input(dcv, w["ffn_conv_wv"], FFN_CONV, name=tag + "ffn_conv_bwd_v", tc=1408)
    g["w_up"] = jnp.concatenate([_mm_tn(s["h1"], dug, name=tag + "up_g_dw", tn=1408),
                                 _mm_tn(s["h1"], duv, name=tag + "up_v_dw", tn=1408)], axis=1)
    dh1 = _mm(dug, w["w_up_g"], trans_b=True, add=dpre2, add_scale=ALPHA, name=tag + "up_g_dx")
    dh1 = _mm(duv, w["w_up_v"], trans_b=True, add=dh1, name=tag + "up_v_dx")
    dpre1, g["ln1_g"], g["ln1_b"] = _ln_bwd(dh1, s["pre1"], w["ln1_g"], name=tag + "ln1_bwd")
    g["w_out"] = _mm_tn(s["mixed"], dpre1, name=tag + "mix_out_dw")
    dmix = _mm(dpre1, w["w_out"], trans_b=True, name=tag + "mix_out_dx")
    dya, dys, dga, dgs = _mix_bwd(dmix, proj, s["ya"], s["ys"], name=tag + "mix_bwd")
    g["w_o_attn"] = _mm_tn(s["o"], dya, name=tag + "attn_out_dw")
    do = _mm(dya, w["w_o_attn"], trans_b=True, out_dtype=BF16, name=tag + "attn_out_dx")
    delta = _attn_delta(do, s["o"], name=tag + "attn_delta")
    dq, dkn, dkp, dv = _flash_bwd(s["q"], s["kv"], s["kpe"], do, s["lse"], delta, name=tag + "attn_bwd")
    dq_raw = _rope_q_bwd(dq, cos, sin_a, sin_b, name=tag + "q_rope_bwd")
    dwq = _mm_tn(s["qn"], dq_raw, name=tag + "q_up_dw")
    g["w_q_b"] = dwq.reshape(Q_LORA, MLA_HEADS, QHEAD)[:, :, :QK_NOPE + QK_ROPE].reshape(Q_LORA, -1)
    dqn = _mm(dq_raw, w["w_q_b"], trans_b=True, out_dtype=BF16, name=tag + "q_up_dx")
    dqlat, dgq = _rms_bwd(dqn, proj, OFF_Q, Q_LORA, w["q_norm_g"], name=tag + "q_norm_bwd")
    g["q_norm_g"] = dgq.reshape(-1)
    dkv = jnp.concatenate([dkn, dv], axis=1)
    dwkv = _mm_tn(s["kvn"], dkv, name=tag + "kv_up_dw")
    g["w_kv_b"] = jnp.swapaxes(dwkv.reshape(KV_LORA, 2, MLA_HEADS, QK_NOPE), 1, 2).reshape(KV_LORA, -1)
    dkvn = _mm(dkv, w["w_kv_b"], trans_b=True, out_dtype=BF16, name=tag + "kv_up_dx")
    dkvlat, dgkv = _rms_bwd(dkvn, proj, OFF_KV, KV_LORA, w["kv_norm_g"], name=tag + "kv_norm_bwd")
    g["kv_norm_g"] = dgkv.reshape(-1)
    dkpe = _rope_k_bwd(dkp, cos, sin_a, sin_b, name=tag + "k_rope_bwd")
    g["w_o_ssd"] = _mm_tn(s["yn"], dys, name=tag + "ssd_out_dw")
    dyn = _mm(dys, w["w_o_ssd"], trans_b=True, out_dtype=BF16, name=tag + "ssd_out_dx")
    dy, dz, dgn = _gnorm_bwd(dyn, s["y"], proj, w["ssd_norm_g"], name=tag + "ssd_norm_bwd")
    g["ssd_norm_g"] = dgn.reshape(-1)
    dxs, ddtx, dbm, dcm, da_x, dd_x = _ssd_bwd(s["xs"], s["dtx"], s["bm"], s["cm"], s["cm"].T, s["prev"], dy,
                                               w["a_x"], w["d_x"], name=tag + "ssd_scan_bwd")
    g["a_log"] = da_x.reshape(SSD_HEADS, SSD_HEAD_DIM).sum(axis=1) * w["a"]
    g["d_skip"] = dd_x.reshape(SSD_HEADS, SSD_HEAD_DIM).sum(axis=1)
    dconv, ddtr, dcw, dcb, ddtb = _ssd_prep_bwd_a(proj, dxs, dbm, dcm, ddtx, w["ssd_conv_w"], w["ssd_conv_b"],
                                                  w["dt_bias"], reduce_m, name=tag + "ssd_prep_bwd")
    g["ssd_conv_w"] = dcw
    g["ssd_conv_b"] = dcb.reshape(-1)
    g["dt_bias"] = ddtb.reshape(-1)[:SSD_HEADS]
    dxbc = _conv_bwd_input(dconv, w["ssd_conv_w"], SSD_CONV, name=tag + "ssd_conv_bwd", tc=1024)
    h = s["h"]
    comps = ((dqlat, OFF_Q), (dkvlat, OFF_KV), (dz, OFF_Z), (dxbc, OFF_XBC), (dga, OFF_GA), (dgs, OFF_GS),
             (dkpe, OFF_KPE), (ddtr, OFF_DT))
    dws = {off: _mm_tn(dc, h, name=f"{tag}in_dw{n}") for n, (dc, off) in enumerate(comps)}
    with_w = lambda group: [(dc, w["w_in_t"][off:off + dc.shape[1]]) for dc, off in group]
    wide = [c for c in comps if c[1] in (OFF_Z, OFF_XBC)]
    rest = [c for c in comps if c[1] not in (OFF_Z, OFF_XBC)]
    dh = _mm_sum(with_w(wide), dpre1, add_scale=ALPHA, name=tag + "in_dx_wide")
    dh = _mm_sum(with_w(rest), dh, name=tag + "in_dx_rest")
    g["w_in"] = jnp.concatenate([dws[OFF_Q], dws[OFF_KV], dws[OFF_KPE][:QK_ROPE], dws[OFF_Z], dws[OFF_XBC],
                                 dws[OFF_DT][:SSD_HEADS], dws[OFF_GA], dws[OFF_GS]], axis=0)
    return dh, g


def _local_step(x, target, full, rep):
    seq = x.shape[0]
    t = seq + ROW0
    tabs = _rope_tables(t)
    expand = _expand_matrix()
    reduce_m = expand.T
    meta = _join(full["meta_tokens"].reshape(N_CHIPS, N_META, -1), "col")
    xin = jnp.concatenate([jnp.zeros((PAD, D_MODEL), F32), meta, x], axis=0)
    row = lambda v: v.reshape(1, -1)
    _, h = _ln_fwd(xin, None, row(rep["emb_ln_g"]), row(rep["emb_ln_b"]), name="emb_ln")
    ws, saved = [], []
    for i in range(DEPTH):
        w = _layer_weights(full, rep, i)
        h, s = _layer_fwd(h, w, tabs, expand, f"l{i}_")
        ws.append(w)
        saved.append(s)
    dh, loss = _loss_grad(h, target, name="loss")
    layer_grads = [None] * DEPTH
    for i in reversed(range(DEPTH)):
        dh, layer_grads[i] = _layer_bwd(dh, ws[i], saved[i], tabs, reduce_m, f"l{i}_")
    dxin, dg, db = _ln_bwd(dh, xin, row(rep["emb_ln_g"]), name="emb_ln_bwd")
    grads = {n: jnp.stack([layer_grads[i][n] for i in range(DEPTH)]) for n in layer_grads[0]}
    grads["emb_ln_g"] = dg.reshape(-1)
    grads["emb_ln_b"] = db.reshape(-1)
    grads["meta_tokens"] = dxin[PAD:ROW0]
    return loss, dxin[ROW0:], grads


def _reduce_grads(grads):
    segs = [_chip_segments(grads[n], "row" if n == "w_in" else kind) for n, kind in SHARDED]
    small = jnp.concatenate([grads[n].reshape(-1) for n in REPLICATED])
    segs.append(jnp.broadcast_to(small[None], (N_CHIPS, small.shape[0])))
    g4 = _rows_of(jnp.concatenate(segs, axis=1), REDUCE_ROW_ALIGN)
    r = g4.shape[1]
    g5 = g4.reshape(N_CHIPS, 2, r // 2, LANES)
    core = lax.axis_index("c").astype(jnp.int32).reshape(1)
    got = _sibling_swap(g5, name="reduce_pair_swap")
    pair = _pair_add(g5, got, core, name="reduce_pair_add")
    parts = _chip_exchange(pair, name="reduce_chip_exchange")
    half = _sum_chips(parts, core, name="reduce_chip_sum")
    both = _sibling_allgather(half, name="reduce_pair_gather")
    return both.reshape(r, LANES)


def kernel(x, meta_tokens, emb_ln_g, emb_ln_b, w_in, q_norm_g, w_q_b, kv_norm_g, w_kv_b, w_o_attn, ssd_conv_w, ssd_conv_b, dt_bias, a_log, d_skip, ssd_norm_g, w_o_ssd, w_out, ln1_g, ln1_b, w_up, ffn_conv_w, ffn_conv_b, w_down, ln2_g, ln2_b, loss_target, m_meta_tokens, m_emb_ln_g, m_emb_ln_b, m_w_in, m_q_norm_g, m_w_q_b, m_kv_norm_g, m_w_kv_b, m_w_o_attn, m_ssd_conv_w, m_ssd_conv_b, m_dt_bias, m_a_log, m_d_skip, m_ssd_norm_g, m_w_o_ssd, m_w_out, m_ln1_g, m_ln1_b, m_w_up, m_ffn_conv_w, m_ffn_conv_b, m_w_down, m_ln2_g, m_ln2_b, v_meta_tokens, v_emb_ln_g, v_emb_ln_b, v_w_in, v_q_norm_g, v_w_q_b, v_kv_norm_g, v_w_kv_b, v_w_o_attn, v_ssd_conv_w, v_ssd_conv_b, v_dt_bias, v_a_log, v_d_skip, v_ssd_norm_g, v_w_o_ssd, v_w_out, v_ln1_g, v_ln1_b, v_w_up, v_ffn_conv_w, v_ffn_conv_b, v_w_down, v_ln2_g, v_ln2_b):
    given = dict(locals())
    local_w = {n: given[n] for n in WEIGHTS}
    local_m = {n: given["m_" + n] for n in WEIGHTS}
    local_v = {n: given["v_" + n] for n in WEIGHTS}
    full = _gather_weights(local_w)
    rep = {n: local_w[n] for n in REPLICATED}
    loss, grad_x, grads = _local_step(x[0], loss_target[0], full, rep)
    g_flat = _reduce_grads(grads)
    flat = g_flat.reshape(-1)
    grad, off = {}, 0
    for n in [n for n, _ in SHARDED] + list(REPLICATED):
        shape = local_w[n].shape
        size = int(np.prod(shape))
        piece = flat[off:off + size]
        if n == "w_in":
            grad[n] = jnp.swapaxes(piece.reshape(shape[0], shape[2], shape[1]), 1, 2)
        else:
            grad[n] = piece.reshape(shape)
        off += size
    upd = {}
    small = [n for n in WEIGHTS if n not in GATHER_BF16]
    for n in GATHER_BF16:
        upd[n] = _adamw(grad[n], local_w[n], local_m[n], local_v[n], name="adamw_" + n)
    res = _adamw_small([(grad[n], local_w[n], local_m[n], local_v[n]) for n in small], name="adamw_small")
    upd.update(zip(small, res))
    total = lax.psum(loss[0, 0], ("x", "y", "c"))
    outs = [total, grad_x[None]] + [grad[n] for n in WEIGHTS]
    for q in range(3):
        outs.extend(upd[n][q] for n in WEIGHTS)
    return tuple(outs)
```

```python
import functools
import math

import numpy as np
import jax
import jax.numpy as jnp
from jax import lax
from jax.experimental import pallas as pl
from jax.experimental.pallas import tpu as pltpu

F32 = jnp.float32
BF16 = jnp.bfloat16

D_MODEL = 1024
N_META = 16
DEPTH = 2
MLA_HEADS = 8
Q_LORA = 768
KV_LORA = 256
QK_NOPE = 128
QK_ROPE = 64
V_HEAD = 128
ROPE_THETA = 10000.0
NEG_INF = -1e30
PAD_KEY_SCORE = -1e30
SSD_INNER = 2048
SSD_HEAD_DIM = 64
SSD_HEADS = 32
SSD_GROUPS = 4
SSD_STATE = 128
SSD_CONV = 4
SSD_CONV_DIM = 3072
CHUNK = 128
D_FF = 2816
FFN_CONV = 3
LN_EPS = 1e-5
RMS_EPS = 1e-6
ALPHA = (2 * DEPTH) ** 0.25
ATTN_SCALE = (QK_NOPE + QK_ROPE) ** -0.5
LOG2E = math.log2(math.e)
LN2 = math.log(2.0)
Q_SCALE = ATTN_SCALE * LOG2E
ADAM_LR = 0.001
ADAM_B1 = 0.9
ADAM_B2 = 0.999
ADAM_EPS = 1e-08
ADAM_WD = 0.01
ADAM_STEP = 10

LANES = 128
PAD = 112
ROW0 = PAD + N_META
QHEAD = 256
GROUP_W = SSD_INNER // SSD_GROUPS
HALO = 8
VMEM_LIMIT_BYTES = 56 * 1024 * 1024
N_CHIPS = 4

OFF_Q, OFF_KV, OFF_Z, OFF_XBC, OFF_GA, OFF_GS, OFF_KPE, OFF_DT = 0, 768, 1024, 3072, 6144, 7168, 8192, 8320
IN_COLS_P = 8448

NT_DIMS = (((1,), (1,)), ((), ()))
NN_DIMS = (((1,), (0,)), ((), ()))
TN_DIMS = (((0,), (0,)), ((), ()))

SHARDED = (("meta_tokens", "col"), ("w_in", "col"), ("w_q_b", "col"), ("w_kv_b", "col"), ("w_o_attn", "row"),
           ("ssd_conv_w", "col"), ("w_o_ssd", "row"), ("w_out", "row"), ("w_up", "col"), ("ffn_conv_w", "col"),
           ("w_down", "row"))
REPLICATED = ("emb_ln_g", "emb_ln_b", "q_norm_g", "kv_norm_g", "ssd_conv_b", "dt_bias", "a_log", "d_skip",
              "ssd_norm_g", "ln1_g", "ln1_b", "ffn_conv_b", "ln2_g", "ln2_b")
WEIGHTS = ("meta_tokens", "emb_ln_g", "emb_ln_b", "w_in", "q_norm_g", "w_q_b", "kv_norm_g", "w_kv_b", "w_o_attn",
           "ssd_conv_w", "ssd_conv_b", "dt_bias", "a_log", "d_skip", "ssd_norm_g", "w_o_ssd", "w_out", "ln1_g",
           "ln1_b", "w_up", "ffn_conv_w", "ffn_conv_b", "w_down", "ln2_g", "ln2_b")
GATHER_BF16 = ("w_in", "w_q_b", "w_kv_b", "w_o_attn", "w_o_ssd", "w_out", "w_up", "w_down")
GATHER_F32 = ("meta_tokens", "ssd_conv_w", "ffn_conv_w")
REDUCE_ROW_ALIGN = 2048


def _tile(n, target, base=LANES):
    best = None
    d = base
    while d <= min(n, target):
        if n % d == 0:
            best = d
        d += base
    return n if best is None else best


def _cp(*sem):
    return pltpu.CompilerParams(dimension_semantics=sem, vmem_limit_bytes=VMEM_LIMIT_BYTES)


def _sds(shape, dtype):
    return jax.ShapeDtypeStruct(shape, dtype)


def _row_ids(i, tr, shape):
    return i * tr + lax.broadcasted_iota(jnp.int32, shape, 0)


def _sigmoid(x):
    return 1.0 / (1.0 + jnp.exp(-x))


def _mm(a, b, *, name, trans_b=False, out_dtype=F32, add=None, add_scale=1.0, tm=640, tn=1024, tk=1408):
    m, k_dim = a.shape
    n = b.shape[0] if trans_b else b.shape[1]
    tm, tn, tk = _tile(m, tm), _tile(n, tn), _tile(k_dim, tk)
    nk = k_dim // tk
    has_add = add is not None
    dims = NT_DIMS if trans_b else NN_DIMS

    def body(*refs):
        a_ref, b_ref = refs[0], refs[1]
        r_ref = refs[2] if has_add else None
        o_ref = refs[3] if has_add else refs[2]
        part = lax.dot_general(a_ref[...].astype(BF16), b_ref[...].astype(BF16), dims, preferred_element_type=F32)

        def finish(r):
            if has_add:
                r = r + add_scale * r_ref[...].astype(F32)
            o_ref[...] = r.astype(out_dtype)

        if nk == 1:
            finish(part)
        else:
            acc = refs[-1]
            kk = pl.program_id(2)

            @pl.when(kk == 0)
            def _():
                acc[...] = part

            @pl.when(kk > 0)
            def _():
                acc[...] += part

            @pl.when(kk == nk - 1)
            def _():
                finish(acc[...])

    in_specs = [pl.BlockSpec((tm, tk), lambda i, j, kk: (i, kk)),
                pl.BlockSpec((tn, tk), lambda i, j, kk: (j, kk)) if trans_b
                else pl.BlockSpec((tk, tn), lambda i, j, kk: (kk, j))]
    args = [a, b]
    if has_add:
        in_specs.append(pl.BlockSpec((tm, tn), lambda i, j, kk: (i, j)))
        args.append(add)
    return pl.pallas_call(
        body, name=name, grid=(m // tm, n // tn, nk), in_specs=in_specs,
        out_specs=pl.BlockSpec((tm, tn), lambda i, j, kk: (i, j)),
        out_shape=_sds((m, n), out_dtype),
        scratch_shapes=[pltpu.VMEM((tm, tn), F32)] if nk > 1 else [],
        compiler_params=_cp("parallel", "parallel", "arbitrary"),
    )(*args)


def _mm_sum(pairs, add, *, name, add_scale=1.0, tm=640):
    m, n = add.shape
    tm = _tile(m, tm)
    npairs = len(pairs)

    def body(*refs):
        a_refs, b_refs = refs[:npairs], refs[npairs:2 * npairs]
        r_ref, o_ref = refs[2 * npairs], refs[2 * npairs + 1]
        acc = add_scale * r_ref[...]
        for a_ref, b_ref in zip(a_refs, b_refs):
            acc = acc + jnp.dot(a_ref[...].astype(BF16), b_ref[...].astype(BF16), preferred_element_type=F32)
        o_ref[...] = acc

    in_specs = ([pl.BlockSpec((tm, a.shape[1]), lambda i: (i, 0)) for a, _ in pairs]
                + [pl.BlockSpec(b.shape, lambda i: (0, 0)) for _, b in pairs]
                + [pl.BlockSpec((tm, n), lambda i: (i, 0))])
    return pl.pallas_call(
        body, name=name, grid=(m // tm,), in_specs=in_specs, out_specs=pl.BlockSpec((tm, n), lambda i: (i, 0)),
        out_shape=_sds((m, n), F32), compiler_params=_cp("parallel"),
    )(*[a for a, _ in pairs], *[b for _, b in pairs], add)


def _mm_tn(a, b, *, name, tko=1408, tn=1024, tt=640):
    t, k_dim = a.shape
    n = b.shape[1]
    tko, tn, tt = _tile(k_dim, tko), _tile(n, tn), _tile(t, tt)

    def body(a_ref, b_ref, o_ref):
        part = lax.dot_general(a_ref[...].astype(BF16), b_ref[...].astype(BF16), TN_DIMS, preferred_element_type=F32)
        tt_i = pl.program_id(2)

        @pl.when(tt_i == 0)
        def _():
            o_ref[...] = part

        @pl.when(tt_i > 0)
        def _():
            o_ref[...] += part

    return pl.pallas_call(
        body, name=name, grid=(k_dim // tko, n // tn, t // tt),
        in_specs=[pl.BlockSpec((tt, tko), lambda i, j, s: (s, i)), pl.BlockSpec((tt, tn), lambda i, j, s: (s, j))],
        out_specs=pl.BlockSpec((tko, tn), lambda i, j, s: (i, j)),
        out_shape=_sds((k_dim, n), F32),
        compiler_params=_cp("parallel", "parallel", "arbitrary"),
    )(a, b)


def _ln_fwd(h, branch, g, b, *, name):
    t, d = h.shape
    tr = _tile(t, 640)
    has_branch = branch is not None

    def body(*refs):
        if has_branch:
            h_ref, br_ref, g_ref, b_ref, pre_ref, o_ref = refs
            pre = ALPHA * h_ref[...] + br_ref[...]
            pre_ref[...] = pre
        else:
            h_ref, g_ref, b_ref, o_ref = refs
            pre = h_ref[...]
        mu = jnp.mean(pre, axis=1, keepdims=True)
        xc = pre - mu
        var = jnp.mean(xc * xc, axis=1, keepdims=True)
        y = xc * lax.rsqrt(var + LN_EPS) * g_ref[...] + b_ref[...]
        rows = _row_ids(pl.program_id(0), tr, (tr, 1))
        o_ref[...] = jnp.where(rows >= PAD, y, 0.0)

    row_spec = pl.BlockSpec((tr, d), lambda i: (i, 0))
    vec_spec = pl.BlockSpec((1, d), lambda i: (0, 0))
    if has_branch:
        return pl.pallas_call(
            body, name=name, grid=(t // tr,), in_specs=[row_spec, row_spec, vec_spec, vec_spec],
            out_specs=[row_spec, row_spec], out_shape=[_sds((t, d), F32), _sds((t, d), F32)],
            compiler_params=_cp("parallel"))(h, branch, g, b)
    out = pl.pallas_call(
        body, name=name, grid=(t // tr,), in_specs=[row_spec, vec_spec, vec_spec],
        out_specs=row_spec, out_shape=_sds((t, d), F32), compiler_params=_cp("parallel"))(h, g, b)
    return h, out


def _ln_bwd(dy, pre, g, *, name):
    t, d = pre.shape
    tr = _tile(t, 640)

    def body(dy_ref, pre_ref, g_ref, dpre_ref, dg_ref, db_ref):
        i = pl.program_id(0)
        pre_v = pre_ref[...]
        mu = jnp.mean(pre_v, axis=1, keepdims=True)
        xc = pre_v - mu
        var = jnp.mean(xc * xc, axis=1, keepdims=True)
        rstd = lax.rsqrt(var + LN_EPS)
        xhat = xc * rstd
        rows = _row_ids(i, tr, (tr, 1))
        dym = jnp.where(rows >= PAD, dy_ref[...], 0.0)
        gdy = dym * g_ref[...]
        m1 = jnp.mean(gdy, axis=1, keepdims=True)
        m2 = jnp.mean(gdy * xhat, axis=1, keepdims=True)
        dpre_ref[...] = rstd * (gdy - m1 - xhat * m2)
        pg = jnp.sum(dym * xhat, axis=0, keepdims=True)
        pb = jnp.sum(dym, axis=0, keepdims=True)

        @pl.when(i == 0)
        def _():
            dg_ref[...] = pg
            db_ref[...] = pb

        @pl.when(i > 0)
        def _():
            dg_ref[...] += pg
            db_ref[...] += pb

    row_spec = pl.BlockSpec((tr, d), lambda i: (i, 0))
    vec_spec = pl.BlockSpec((1, d), lambda i: (0, 0))
    return pl.pallas_call(
        body, name=name, grid=(t // tr,), in_specs=[row_spec, row_spec, vec_spec],
        out_specs=[row_spec, vec_spec, vec_spec],
        out_shape=[_sds((t, d), F32), _sds((1, d), F32), _sds((1, d), F32)],
        compiler_params=_cp("arbitrary"))(dy, pre, g)


def _rms_fwd(proj, col_off, width, g, *, name):
    t = proj.shape[0]
    tr = _tile(t, 640)
    cb = col_off // width

    def body(x_ref, g_ref, o_ref):
        x = x_ref[...]
        r = lax.rsqrt(jnp.mean(x * x, axis=1, keepdims=True) + RMS_EPS)
        o_ref[...] = (x * r * g_ref[...]).astype(BF16)

    return pl.pallas_call(
        body, name=name, grid=(t // tr,),
        in_specs=[pl.BlockSpec((tr, width), lambda i: (i, cb)), pl.BlockSpec((1, width), lambda i: (0, 0))],
        out_specs=pl.BlockSpec((tr, width), lambda i: (i, 0)), out_shape=_sds((t, width), BF16),
        compiler_params=_cp("parallel"))(proj, g)


def _rms_bwd(dy, proj, col_off, width, g, *, name):
    t = proj.shape[0]
    tr = _tile(t, 640)
    cb = col_off // width

    def body(dy_ref, x_ref, g_ref, dx_ref, dg_ref):
        i = pl.program_id(0)
        x = x_ref[...]
        dyv = dy_ref[...].astype(F32)
        r = lax.rsqrt(jnp.mean(x * x, axis=1, keepdims=True) + RMS_EPS)
        gdy = dyv * g_ref[...]
        m = jnp.mean(x * gdy, axis=1, keepdims=True)
        dx_ref[...] = (r * gdy - x * (r * r * r) * m).astype(BF16)
        pg = jnp.sum(dyv * x * r, axis=0, keepdims=True)

        @pl.when(i == 0)
        def _():
            dg_ref[...] = pg

        @pl.when(i > 0)
        def _():
            dg_ref[...] += pg

    return pl.pallas_call(
        body, name=name, grid=(t // tr,),
        in_specs=[pl.BlockSpec((tr, width), lambda i: (i, 0)), pl.BlockSpec((tr, width), lambda i: (i, cb)),
                  pl.BlockSpec((1, width), lambda i: (0, 0))],
        out_specs=[pl.BlockSpec((tr, width), lambda i: (i, 0)), pl.BlockSpec((1, width), lambda i: (0, 0))],
        out_shape=[_sds((t, width), BF16), _sds((1, width), F32)],
        compiler_params=_cp("arbitrary"))(dy, proj, g)


def _rope_apply(r, cos, sin_a, sin_b):
    return r * cos + pltpu.roll(r, 96, 1) * sin_a + pltpu.roll(r, 32, 1) * sin_b


def _rope_apply_t(dr, cos, sin_a, sin_b):
    return dr * cos + pltpu.roll(dr * sin_a, 32, 1) + pltpu.roll(dr * sin_b, 96, 1)


def _rope_q_fwd(q, cos, sin_a, sin_b, *, name):
    t, w = q.shape
    tr = _tile(t, 128)

    def body(q_ref, c_ref, sa_ref, sb_ref, o_ref):
        c, sa, sb = c_ref[...], sa_ref[...], sb_ref[...]
        flag = lax.broadcasted_iota(jnp.int32, (tr, LANES), 1) == QK_ROPE
        for h in range(MLA_HEADS):
            base = h * QHEAD
            o_ref[:, base:base + LANES] = (q_ref[:, base:base + LANES] * Q_SCALE).astype(BF16)
            rot = _rope_apply(q_ref[:, base + LANES:base + QHEAD], c, sa, sb)
            o_ref[:, base + LANES:base + QHEAD] = jnp.where(flag, 1.0, rot * Q_SCALE).astype(BF16)

    tab = pl.BlockSpec((tr, LANES), lambda i: (i, 0))
    row = pl.BlockSpec((tr, w), lambda i: (i, 0))
    return pl.pallas_call(body, name=name, grid=(t // tr,), in_specs=[row, tab, tab, tab], out_specs=row,
                          out_shape=_sds((t, w), BF16), compiler_params=_cp("parallel"))(q, cos, sin_a, sin_b)


def _rope_q_bwd(dq, cos, sin_a, sin_b, *, name):
    t, w = dq.shape
    tr = _tile(t, 128)

    def body(dq_ref, c_ref, sa_ref, sb_ref, o_ref):
        c, sa, sb = c_ref[...], sa_ref[...], sb_ref[...]
        for h in range(MLA_HEADS):
            base = h * QHEAD
            o_ref[:, base:base + LANES] = (dq_ref[:, base:base + LANES] * ATTN_SCALE).astype(BF16)
            d_rot = _rope_apply_t(dq_ref[:, base + LANES:base + QHEAD], c, sa, sb)
            o_ref[:, base + LANES:base + QHEAD] = (d_rot * ATTN_SCALE).astype(BF16)

    tab = pl.BlockSpec((tr, LANES), lambda i: (i, 0))
    row = pl.BlockSpec((tr, w), lambda i: (i, 0))
    return pl.pallas_call(body, name=name, grid=(t // tr,), in_specs=[row, tab, tab, tab], out_specs=row,
                          out_shape=_sds((t, w), BF16), compiler_params=_cp("parallel"))(dq, cos, sin_a, sin_b)


def _rope_k_fwd(proj, cos, sin_a, sin_b, *, name):
    t = proj.shape[0]
    tr = _tile(t, 640)
    cb = OFF_KPE // LANES

    def body(x_ref, c_ref, sa_ref, sb_ref, o_ref):
        rot = _rope_apply(x_ref[...], c_ref[...], sa_ref[...], sb_ref[...])
        rows = _row_ids(pl.program_id(0), tr, (tr, LANES))
        lane = lax.broadcasted_iota(jnp.int32, (tr, LANES), 1)
        o_ref[...] = jnp.where((lane == QK_ROPE) & (rows < PAD), PAD_KEY_SCORE, rot).astype(BF16)

    tab = pl.BlockSpec((tr, LANES), lambda i: (i, 0))
    return pl.pallas_call(body, name=name, grid=(t // tr,),
                          in_specs=[pl.BlockSpec((tr, LANES), lambda i: (i, cb)), tab, tab, tab], out_specs=tab,
                          out_shape=_sds((t, LANES), BF16), compiler_params=_cp("parallel"))(proj, cos, sin_a, sin_b)


def _rope_k_bwd(dkp, cos, sin_a, sin_b, *, name):
    nh, t, _ = dkp.shape
    tr = _tile(t, 640)

    def body(d_ref, c_ref, sa_ref, sb_ref, o_ref):
        tot = d_ref[0]
        for h in range(1, nh):
            tot = tot + d_ref[h]
        o_ref[...] = _rope_apply_t(tot, c_ref[...], sa_ref[...], sb_ref[...]).astype(BF16)

    tab = pl.BlockSpec((tr, LANES), lambda i: (i, 0))
    return pl.pallas_call(body, name=name, grid=(t // tr,),
                          in_specs=[pl.BlockSpec((nh, tr, LANES), lambda i: (0, i, 0)), tab, tab, tab], out_specs=tab,
                          out_shape=_sds((t, LANES), BF16), compiler_params=_cp("parallel"))(dkp, cos, sin_a, sin_b)


def _causal(tb, keys_first=False):
    a = lax.broadcasted_iota(jnp.int32, (tb, tb), 0)
    b = lax.broadcasted_iota(jnp.int32, (tb, tb), 1)
    return a <= b if keys_first else b <= a


def _flash_fwd(q, kv, kpe, *, name):
    t = q.shape[0]
    nh = MLA_HEADS
    tb = _tile(t, 640)
    nb = t // tb

    def body(q_ref, kn_ref, v_ref, kp_ref, o_ref, lse_ref):
        i = pl.program_id(1)
        qv = q_ref[...]

        def scores(j):
            r0 = pl.multiple_of(j * tb, tb)
            k = jnp.concatenate([kn_ref[pl.ds(r0, tb), :], kp_ref[pl.ds(r0, tb), :]], axis=1)
            return lax.dot_general(qv, k, NT_DIMS, preferred_element_type=F32)

        def update(s, j, state):
            m_prev, l_prev, acc = state
            m_new = jnp.maximum(m_prev, jnp.max(s, axis=1, keepdims=True))
            p = jnp.exp2(s - m_new)
            corr = jnp.exp2(m_prev - m_new)
            r0 = pl.multiple_of(j * tb, tb)
            pv = jnp.dot(p.astype(BF16), v_ref[pl.ds(r0, tb), :], preferred_element_type=F32)
            return m_new, corr * l_prev + jnp.sum(p, axis=1, keepdims=True), corr * acc + pv

        def loop(j, carry):
            s_cur, st = carry
            s_next = scores(j + 1)
            return s_next, update(s_cur, j, st)

        state = (jnp.full((tb, 1), NEG_INF, F32), jnp.zeros((tb, 1), F32), jnp.zeros((tb, V_HEAD), F32))
        s_diag, state = lax.fori_loop(0, i, loop, (scores(0), state))
        m, l, acc = update(jnp.where(_causal(tb), s_diag, NEG_INF), i, state)
        o_ref[...] = (acc / l).astype(BF16)
        lse_ref[0] = m + jnp.log2(l)

    return pl.pallas_call(
        body, name=name, grid=(nh, nb),
        in_specs=[pl.BlockSpec((tb, QHEAD), lambda h, i: (i, h)),
                  pl.BlockSpec((t, LANES), lambda h, i: (0, h)),
                  pl.BlockSpec((t, LANES), lambda h, i: (0, nh + h)),
                  pl.BlockSpec((t, LANES), lambda h, i: (0, 0))],
        out_specs=[pl.BlockSpec((tb, V_HEAD), lambda h, i: (i, h)),
                   pl.BlockSpec((1, tb, 1), lambda h, i: (h, i, 0))],
        out_shape=[_sds((t, nh * V_HEAD), BF16), _sds((nh, t, 1), F32)],
        compiler_params=_cp("parallel", "parallel"))(q, kv, kv, kpe)


def _attn_delta(do, o, *, name):
    t = o.shape[0]
    nh = MLA_HEADS
    tr = _tile(t, 640)

    def body(do_ref, o_ref, d_ref):
        d_ref[0] = jnp.sum(do_ref[...].astype(F32) * o_ref[...].astype(F32), axis=1, keepdims=True)

    blk = pl.BlockSpec((tr, V_HEAD), lambda h, i: (i, h))
    return pl.pallas_call(body, name=name, grid=(nh, t // tr), in_specs=[blk, blk],
                          out_specs=pl.BlockSpec((1, tr, 1), lambda h, i: (h, i, 0)),
                          out_shape=_sds((nh, t, 1), F32), compiler_params=_cp("parallel", "parallel"))(do, o)


def _flash_bwd(q, kv, kpe, do, lse, delta, *, name):
    t = q.shape[0]
    nh = MLA_HEADS
    tb = lse.shape[2]
    nb = t // tb

    def body(q_ref, do_ref, lse_ref, dl_ref, kn_ref, v_ref, kp_ref, dq_ref, dkn_ref, dkp_ref, dv_ref):
        j = pl.program_id(1)

        @pl.when(j == 0)
        def _():
            dq_ref[...] = jnp.zeros((t, QHEAD), F32)

        k = jnp.concatenate([kn_ref[...], kp_ref[...]], axis=1)
        v = v_ref[...]

        def tile(i, carry, masked):
            dk, dv = carry
            r0 = pl.multiple_of(i * tb, tb)
            qv = q_ref[pl.ds(r0, tb), :]
            dov = do_ref[pl.ds(r0, tb), :]
            st = lax.dot_general(k, qv, NT_DIMS, preferred_element_type=F32)
            if masked:
                st = jnp.where(_causal(tb, keys_first=True), st, NEG_INF)
            pt = jnp.exp2(st - lse_ref[0, pl.ds(i, 1), :])
            dpt = lax.dot_general(v, dov, NT_DIMS, preferred_element_type=F32)
            dst = (pt * (dpt - dl_ref[0, pl.ds(i, 1), :])).astype(BF16)
            dv = dv + jnp.dot(pt.astype(BF16), dov, preferred_element_type=F32)
            dk = dk + jnp.dot(dst, qv, preferred_element_type=F32)
            dq_ref[pl.ds(r0, tb), :] += lax.dot_general(dst, k, TN_DIMS, preferred_element_type=F32)
            return dk, dv

        carry = tile(j, (jnp.zeros((tb, QHEAD), F32), jnp.zeros((tb, V_HEAD), F32)), True)
        dk, dv = lax.fori_loop(j + 1, nb, lambda i, c: tile(i, c, False), carry)
        dkn_ref[...] = (dk[:, :LANES] * LN2).astype(BF16)
        dkp_ref[0] = dk[:, LANES:] * LN2
        dv_ref[...] = dv.astype(BF16)

    stat = pl.BlockSpec((1, nb, tb), lambda h, j: (h, 0, 0))
    return pl.pallas_call(
        body, name=name, grid=(nh, nb),
        in_specs=[pl.BlockSpec((t, QHEAD), lambda h, j: (0, h)),
                  pl.BlockSpec((t, V_HEAD), lambda h, j: (0, h)),
                  stat, stat,
                  pl.BlockSpec((tb, LANES), lambda h, j: (j, h)),
                  pl.BlockSpec((tb, LANES), lambda h, j: (j, nh + h)),
                  pl.BlockSpec((tb, LANES), lambda h, j: (j, 0))],
        out_specs=[pl.BlockSpec((t, QHEAD), lambda h, j: (0, h)),
                   pl.BlockSpec((tb, LANES), lambda h, j: (j, h)),
                   pl.BlockSpec((1, tb, LANES), lambda h, j: (h, j, 0)),
                   pl.BlockSpec((tb, V_HEAD), lambda h, j: (j, h))],
        out_shape=[_sds((t, nh * QHEAD), F32), _sds((t, nh * LANES), BF16), _sds((nh, t, LANES), F32),
                   _sds((t, nh * V_HEAD), BF16)],
        compiler_params=_cp("arbitrary", "arbitrary"))(q, do, lse, delta, kv, kv, kpe)


def _fill_prev(buf, x_ref, halo_ref, i, tr):
    buf[pl.ds(0, HALO), :] = jnp.where(i > 0, halo_ref[...], 0.0)
    buf[pl.ds(HALO, tr), :] = x_ref[...]


def _conv_prev(buf, w_ref, kw, tr):
    acc = w_ref[kw - 1:kw, :] * buf[pl.ds(HALO, tr), :]
    for k in range(kw - 1):
        acc = acc + w_ref[k:k + 1, :] * buf[pl.ds(HALO - kw + 1 + k, tr), :]
    return acc


def _conv_dw(buf, dc, kw, tr):
    rows = [jnp.sum(dc * buf[pl.ds(HALO - kw + 1 + k, tr), :], axis=0, keepdims=True) for k in range(kw)]
    return jnp.concatenate(rows, axis=0)


def _conv_next(buf, dc_ref, halo_ref, w_ref, kw, i, n_tiles, tr):
    buf[pl.ds(0, tr), :] = dc_ref[...]
    buf[pl.ds(tr, HALO), :] = jnp.where(i < n_tiles - 1, halo_ref[...], 0.0)
    acc = w_ref[kw - 1:kw, :] * buf[pl.ds(0, tr), :]
    for k in range(kw - 1):
        acc = acc + w_ref[k:k + 1, :] * buf[pl.ds(kw - 1 - k, tr), :]
    return acc


def _split3(x):
    x1 = x.astype(BF16)
    r1 = x - x1.astype(F32)
    x2 = r1.astype(BF16)
    x3 = (r1 - x2.astype(F32)).astype(BF16)
    return x1, x2, x3


def _dot3(parts, m, left):
    tot = None
    for p in parts:
        r = jnp.dot(m, p, preferred_element_type=F32) if left else jnp.dot(p, m, preferred_element_type=F32)
        tot = r if tot is None else tot + r
    return tot


def _ssd_prep_fwd(proj, conv_w, conv_b, dt_bias, expand, *, name):
    t = proj.shape[0]
    tr = _tile(t, 128)
    nt = t // tr
    hb = tr // HALO
    cw = SSD_CONV_DIM
    cb_x = OFF_XBC // cw
    cb_dt = OFF_DT // LANES

    def body(x_ref, halo_ref, dtr_ref, w_ref, b_ref, dtb_ref, e_ref, xs_ref, bm_ref, cm_ref, dtx_ref, buf):
        i = pl.program_id(0)
        _fill_prev(buf, x_ref, halo_ref, i, tr)
        conv = _conv_prev(buf, w_ref, SSD_CONV, tr) + b_ref[...]
        rows = _row_ids(i, tr, (tr, 1))
        live = rows >= PAD
        act = jnp.where(live, conv * _sigmoid(conv), 0.0)
        xs_ref[...] = act[:, :SSD_INNER]
        bm_ref[...] = act[:, SSD_INNER:SSD_INNER + GROUP_W]
        cm_ref[...] = act[:, SSD_INNER + GROUP_W:]
        dt = jnp.where(live, jax.nn.softplus(dtr_ref[...] + dtb_ref[...]), 0.0)
        dtx_ref[...] = _dot3(_split3(dt), e_ref[...], left=False)

    return pl.pallas_call(
        body, name=name, grid=(nt,),
        in_specs=[pl.BlockSpec((tr, cw), lambda i: (i, cb_x)),
                  pl.BlockSpec((HALO, cw), lambda i: (jnp.maximum(i * hb - 1, 0), cb_x)),
                  pl.BlockSpec((tr, LANES), lambda i: (i, cb_dt)),
                  pl.BlockSpec((SSD_CONV, cw), lambda i: (0, 0)),
                  pl.BlockSpec((1, cw), lambda i: (0, 0)),
                  pl.BlockSpec((1, LANES), lambda i: (0, 0)),
                  pl.BlockSpec((LANES, SSD_INNER), lambda i: (0, 0))],
        out_specs=[pl.BlockSpec((tr, SSD_INNER), lambda i: (i, 0)), pl.BlockSpec((tr, GROUP_W), lambda i: (i, 0)),
                   pl.BlockSpec((tr, GROUP_W), lambda i: (i, 0)), pl.BlockSpec((tr, SSD_INNER), lambda i: (i, 0))],
        out_shape=[_sds((t, SSD_INNER), F32), _sds((t, GROUP_W), F32), _sds((t, GROUP_W), F32),
                   _sds((t, SSD_INNER), F32)],
        scratch_shapes=[pltpu.VMEM((tr + HALO, cw), F32)],
        compiler_params=_cp("parallel"))(proj, proj, proj, conv_w, conv_b, dt_bias, expand)


def _ssd_prep_bwd_a(proj, dxs, dbm, dcm, ddtx, conv_w, conv_b, dt_bias, reduce_m, *, name):
    t = proj.shape[0]
    tr = _tile(t, 128)
    nt = t // tr
    hb = tr // HALO
    cw = SSD_CONV_DIM
    cb_x = OFF_XBC // cw
    cb_dt = OFF_DT // LANES

    def body(x_ref, halo_ref, dtr_ref, dxs_ref, dbm_ref, dcm_ref, ddtx_ref, w_ref, b_ref, dtb_ref, r_ref,
             dconv_ref, ddtr_ref, dw_ref, db_ref, ddtb_ref, buf):
        i = pl.program_id(0)
        _fill_prev(buf, x_ref, halo_ref, i, tr)
        conv = _conv_prev(buf, w_ref, SSD_CONV, tr) + b_ref[...]
        rows = _row_ids(i, tr, (tr, 1))
        live = rows >= PAD
        sg = _sigmoid(conv)
        dact = jnp.concatenate([dxs_ref[...], dbm_ref[...], dcm_ref[...]], axis=1)
        dconv = jnp.where(live, dact * (sg * (1.0 + conv * (1.0 - sg))), 0.0)
        dconv_ref[...] = dconv
        pw = _conv_dw(buf, dconv, SSD_CONV, tr)
        pb = jnp.sum(dconv, axis=0, keepdims=True)
        ddt = _dot3(_split3(ddtx_ref[...]), r_ref[...], left=False)
        ddtr = jnp.where(live, ddt * _sigmoid(dtr_ref[...] + dtb_ref[...]), 0.0)
        ddtr_ref[...] = ddtr.astype(BF16)
        pdb = jnp.sum(ddtr, axis=0, keepdims=True)

        @pl.when(i == 0)
        def _():
            dw_ref[...] = pw
            db_ref[...] = pb
            ddtb_ref[...] = pdb

        @pl.when(i > 0)
        def _():
            dw_ref[...] += pw
            db_ref[...] += pb
            ddtb_ref[...] += pdb

    return pl.pallas_call(
        body, name=name, grid=(nt,),
        in_specs=[pl.BlockSpec((tr, cw), lambda i: (i, cb_x)),
                  pl.BlockSpec((HALO, cw), lambda i: (jnp.maximum(i * hb - 1, 0), cb_x)),
                  pl.BlockSpec((tr, LANES), lambda i: (i, cb_dt)),
                  pl.BlockSpec((tr, SSD_INNER), lambda i: (i, 0)),
                  pl.BlockSpec((tr, GROUP_W), lambda i: (i, 0)),
                  pl.BlockSpec((tr, GROUP_W), lambda i: (i, 0)),
                  pl.BlockSpec((tr, SSD_INNER), lambda i: (i, 0)),
                  pl.BlockSpec((SSD_CONV, cw), lambda i: (0, 0)),
                  pl.BlockSpec((1, cw), lambda i: (0, 0)),
                  pl.BlockSpec((1, LANES), lambda i: (0, 0)),
                  pl.BlockSpec((SSD_INNER, LANES), lambda i: (0, 0))],
        out_specs=[pl.BlockSpec((tr, cw), lambda i: (i, 0)), pl.BlockSpec((tr, LANES), lambda i: (i, 0)),
                   pl.BlockSpec((SSD_CONV, cw), lambda i: (0, 0)), pl.BlockSpec((1, cw), lambda i: (0, 0)),
                   pl.BlockSpec((1, LANES), lambda i: (0, 0))],
        out_shape=[_sds((t, cw), F32), _sds((t, LANES), BF16), _sds((SSD_CONV, cw), F32), _sds((1, cw), F32),
                   _sds((1, LANES), F32)],
        scratch_shapes=[pltpu.VMEM((tr + HALO, cw), F32)],
        compiler_params=_cp("arbitrary"))(proj, proj, proj, dxs, dbm, dcm, ddtx, conv_w, conv_b, dt_bias, reduce_m)


def _conv_bwd_input(dconv, w, kw, *, name, out_dtype=BF16, tc=None):
    t, c = dconv.shape
    tr = _tile(t, 128)
    nt = t // tr
    hb = tr // HALO
    tc = _tile(c, tc or c)
    last_hb = t // HALO - 1

    def body(dc_ref, halo_ref, w_ref, o_ref, buf):
        i = pl.program_id(0)
        o_ref[...] = _conv_next(buf, dc_ref, halo_ref, w_ref, kw, i, nt, tr).astype(out_dtype)

    return pl.pallas_call(
        body, name=name, grid=(nt, c // tc),
        in_specs=[pl.BlockSpec((tr, tc), lambda i, j: (i, j)),
                  pl.BlockSpec((HALO, tc), lambda i, j: (jnp.minimum((i + 1) * hb, last_hb), j)),
                  pl.BlockSpec((kw, tc), lambda i, j: (0, j))],
        out_specs=pl.BlockSpec((tr, tc), lambda i, j: (i, j)), out_shape=_sds((t, c), out_dtype),
        scratch_shapes=[pltpu.VMEM((tr + HALO, tc), F32)],
        compiler_params=_cp("parallel", "parallel"))(dconv, dconv, w)


def _ffn_act_fwd(ug, uv, wg, wv, bg, bv, *, name):
    t, c = ug.shape
    tr = _tile(t, 128)
    hb = tr // HALO
    tc = _tile(c, 1408)

    def body(ug_ref, hg_ref, uv_ref, hv_ref, wg_ref, wv_ref, bg_ref, bv_ref, o_ref, bufg, bufv):
        i = pl.program_id(0)
        _fill_prev(bufg, ug_ref, hg_ref, i, tr)
        _fill_prev(bufv, uv_ref, hv_ref, i, tr)
        cg = _conv_prev(bufg, wg_ref, FFN_CONV, tr) + bg_ref[...]
        cv = _conv_prev(bufv, wv_ref, FFN_CONV, tr) + bv_ref[...]
        o_ref[...] = (cg * _sigmoid(cg) * cv).astype(BF16)

    blk = pl.BlockSpec((tr, tc), lambda i, j: (i, j))
    halo = pl.BlockSpec((HALO, tc), lambda i, j: (jnp.maximum(i * hb - 1, 0), j))
    wsp = pl.BlockSpec((FFN_CONV, tc), lambda i, j: (0, j))
    bsp = pl.BlockSpec((1, tc), lambda i, j: (0, j))
    return pl.pallas_call(
        body, name=name, grid=(t // tr, c // tc), in_specs=[blk, halo, blk, halo, wsp, wsp, bsp, bsp],
        out_specs=blk, out_shape=_sds((t, c), BF16),
        scratch_shapes=[pltpu.VMEM((tr + HALO, tc), F32), pltpu.VMEM((tr + HALO, tc), F32)],
        compiler_params=_cp("parallel", "parallel"))(ug, ug, uv, uv, wg, wv, bg, bv)


def _ffn_act_bwd(ug, uv, dact, wg, wv, bg, bv, *, name):
    t, c = ug.shape
    tr = _tile(t, 128)
    hb = tr // HALO
    tc = _tile(c, 1408)

    def body(ug_ref, hg_ref, uv_ref, hv_ref, da_ref, wg_ref, wv_ref, bg_ref, bv_ref,
             dcg_ref, dcv_ref, dwg_ref, dwv_ref, dbg_ref, dbv_ref, bufg, bufv):
        i = pl.program_id(1)
        _fill_prev(bufg, ug_ref, hg_ref, i, tr)
        _fill_prev(bufv, uv_ref, hv_ref, i, tr)
        cg = _conv_prev(bufg, wg_ref, FFN_CONV, tr) + bg_ref[...]
        cv = _conv_prev(bufv, wv_ref, FFN_CONV, tr) + bv_ref[...]
        sg = _sigmoid(cg)
        da = da_ref[...]
        dcg = da * cv * (sg * (1.0 + cg * (1.0 - sg)))
        dcv = da * (cg * sg)
        dcg_ref[...] = dcg
        dcv_ref[...] = dcv
        pwg = _conv_dw(bufg, dcg, FFN_CONV, tr)
        pwv = _conv_dw(bufv, dcv, FFN_CONV, tr)
        pbg = jnp.sum(dcg, axis=0, keepdims=True)
        pbv = jnp.sum(dcv, axis=0, keepdims=True)

        @pl.when(i == 0)
        def _():
            dwg_ref[...] = pwg
            dwv_ref[...] = pwv
            dbg_ref[...] = pbg
            dbv_ref[...] = pbv

        @pl.when(i > 0)
        def _():
            dwg_ref[...] += pwg
            dwv_ref[...] += pwv
            dbg_ref[...] += pbg
            dbv_ref[...] += pbv

    blk = pl.BlockSpec((tr, tc), lambda j, i: (i, j))
    halo = pl.BlockSpec((HALO, tc), lambda j, i: (jnp.maximum(i * hb - 1, 0), j))
    wsp = pl.BlockSpec((FFN_CONV, tc), lambda j, i: (0, j))
    bsp = pl.BlockSpec((1, tc), lambda j, i: (0, j))
    return pl.pallas_call(
        body, name=name, grid=(c // tc, t // tr), in_specs=[blk, halo, blk, halo, blk, wsp, wsp, bsp, bsp],
        out_specs=[blk, blk, wsp, wsp, bsp, bsp],
        out_shape=[_sds((t, c), F32), _sds((t, c), F32), _sds((FFN_CONV, c), F32), _sds((FFN_CONV, c), F32),
                   _sds((1, c), F32), _sds((1, c), F32)],
        scratch_shapes=[pltpu.VMEM((tr + HALO, tc), F32), pltpu.VMEM((tr + HALO, tc), F32)],
        compiler_params=_cp("parallel", "arbitrary"))(ug, ug, uv, uv, dact, wg, wv, bg, bv)


def _tri(lower):
    li = lax.broadcasted_iota(jnp.int32, (CHUNK, CHUNK), 0)
    si = lax.broadcasted_iota(jnp.int32, (CHUNK, CHUNK), 1)
    return li >= si if lower else li <= si


def _tri_ones(lower):
    return jnp.where(_tri(lower), 1.0, 0.0).astype(BF16)


def _decay_pair(acs, acs_t, lane0):
    col = acs[:, lane0:lane0 + 1]
    row = acs_t[lane0:lane0 + 1, :]
    low = jnp.where(_tri(True), jnp.exp(jnp.minimum(col - row, 0.0)), 0.0)
    upp = jnp.where(_tri(False), jnp.exp(jnp.minimum(row - col, 0.0)), 0.0)
    return low, upp


def _ssd_fwd(xs, dtx, bm, cm, bm_t, a_x, d_x, *, name):
    t = xs.shape[0]
    nc = t // CHUNK
    gw = GROUP_W

    def body(xs_ref, dt_ref, b_ref, c_ref, bt_ref, a_ref, d_ref, y_ref, prev_ref, h_s):
        @pl.when(pl.program_id(1) == 0)
        def _():
            h_s[...] = jnp.zeros((SSD_STATE, gw), F32)

        x = xs_ref[...]
        dt = dt_ref[...]
        acs = _dot3(_split3(dt * a_ref[...]), _tri_ones(True), left=True)
        acs_t = acs.T
        xc = x * dt
        bv = b_ref[...].astype(BF16)
        cv = c_ref[...].astype(BF16)
        cb = lax.dot_general(cv, bv, NT_DIMS, preferred_element_type=F32)
        lane = lax.broadcasted_iota(jnp.int32, (CHUNK, LANES), 1)
        pieces = []
        for pp in range(gw // LANES):
            xcp = xc[:, pp * LANES:(pp + 1) * LANES]
            acc = jnp.zeros((CHUNK, LANES), F32)
            for e in range(2):
                low, _ = _decay_pair(acs, acs_t, pp * LANES + e * SSD_HEAD_DIM)
                mine = (lane >= e * SSD_HEAD_DIM) & (lane < (e + 1) * SSD_HEAD_DIM)
                xm = jnp.where(mine, xcp, 0.0).astype(BF16)
                acc = acc + jnp.dot((cb * low).astype(BF16), xm, preferred_element_type=F32)
            pieces.append(acc)
        y_diag = jnp.concatenate(pieces, axis=1)
        h_prev = h_s[...]
        y_off = jnp.dot(cv, h_prev.astype(BF16), preferred_element_type=F32) * jnp.exp(acs)
        y_ref[...] = y_diag + y_off + d_ref[...] * x
        prev_ref[0] = h_prev
        last = acs[CHUNK - 1:CHUNK, :]
        w = jnp.exp(last - acs)
        st = jnp.dot(bt_ref[...].astype(BF16), (xc * w).astype(BF16), preferred_element_type=F32)
        h_s[...] = h_prev * jnp.exp(last) + st

    tok = pl.BlockSpec((CHUNK, gw), lambda g, c: (c, g))
    grp = pl.BlockSpec((CHUNK, SSD_STATE), lambda g, c: (c, g))
    vec = pl.BlockSpec((1, gw), lambda g, c: (0, g))
    return pl.pallas_call(
        body, name=name, grid=(SSD_GROUPS, nc),
        in_specs=[tok, tok, grp, grp, pl.BlockSpec((SSD_STATE, CHUNK), lambda g, c: (g, c)), vec, vec],
        out_specs=[tok, pl.BlockSpec((1, SSD_STATE, gw), lambda g, c: (c, 0, g))],
        out_shape=[_sds((t, SSD_INNER), F32), _sds((nc, SSD_STATE, SSD_INNER), F32)],
        scratch_shapes=[pltpu.VMEM((SSD_STATE, gw), F32)],
        compiler_params=_cp("parallel", "arbitrary"))(xs, dtx, bm, cm, bm_t, a_x, d_x)


def _ssd_bwd(xs, dtx, bm, cm, cm_t, prev, dy, a_x, d_x, *, name):
    t = xs.shape[0]
    nc = t // CHUNK
    gw = GROUP_W

    def body(xs_ref, dt_ref, b_ref, c_ref, ct_ref, prev_ref, dy_ref, a_ref, d_ref,
             dxs_ref, ddt_ref, db_ref, dc_ref, da_ref, dd_ref, g_s):
        first = pl.program_id(1) == 0

        @pl.when(first)
        def _():
            g_s[...] = jnp.zeros((SSD_STATE, gw), F32)

        x = xs_ref[...]
        dt = dt_ref[...]
        a = a_ref[...]
        dyv = dy_ref[...]
        acs = _dot3(_split3(dt * a), _tri_ones(True), left=True)
        acs_t = acs.T
        xc = x * dt
        bv = b_ref[...].astype(BF16)
        cv = c_ref[...].astype(BF16)
        cb = lax.dot_general(cv, bv, NT_DIMS, preferred_element_type=F32)
        cb_t = lax.dot_general(bv, cv, NT_DIMS, preferred_element_type=F32)
        last = acs[CHUNK - 1:CHUNK, :]
        w = jnp.exp(last - acs)
        cd = jnp.exp(last)
        p_in = prev_ref[0]
        p_b = p_in.astype(BF16)
        g_out = g_s[...]
        g_b = g_out.astype(BF16)
        dy_e = dyv * jnp.exp(acs)
        dy_eb = dy_e.astype(BF16)
        y_off_raw = jnp.dot(cv, p_b, preferred_element_type=F32)
        dacs = dy_e * y_off_raw
        d_c = lax.dot_general(dy_eb, p_b, NT_DIMS, preferred_element_type=F32)
        d_prev = jnp.dot(ct_ref[...].astype(BF16), dy_eb, preferred_element_type=F32)
        q_l = jnp.dot(bv, g_b, preferred_element_type=F32)
        dxc = w * q_l
        tw = xc * q_l * w
        dacs = dacs - tw
        d_b = lax.dot_general((xc * w).astype(BF16), g_b, NT_DIMS, preferred_element_type=F32)
        last_add = jnp.sum(tw, axis=0, keepdims=True) + cd * jnp.sum(g_out * p_in, axis=0, keepdims=True)
        g_s[...] = cd * g_out + d_prev
        lane = lax.broadcasted_iota(jnp.int32, (CHUNK, LANES), 1)
        d_cb = jnp.zeros((CHUNK, CHUNK), F32)
        d_cb_t = jnp.zeros((CHUNK, CHUNK), F32)
        dxc_pieces, dacs_pieces = [], []
        for pp in range(gw // LANES):
            xcp = xc[:, pp * LANES:(pp + 1) * LANES]
            dyp = dyv[:, pp * LANES:(pp + 1) * LANES]
            dxcp = jnp.zeros((CHUNK, LANES), F32)
            dacsp = jnp.zeros((CHUNK, LANES), F32)
            for e in range(2):
                low, upp = _decay_pair(acs, acs_t, pp * LANES + e * SSD_HEAD_DIM)
                mine = (lane >= e * SSD_HEAD_DIM) & (lane < (e + 1) * SSD_HEAD_DIM)
                m_low = cb * low
                m_upp = cb_t * upp
                dym = jnp.where(mine, dyp, 0.0).astype(BF16)
                xm = jnp.where(mine, xcp, 0.0).astype(BF16)
                dxcp = dxcp + jnp.dot(m_upp.astype(BF16), dym, preferred_element_type=F32)
                d_m = lax.dot_general(dym, xm, NT_DIMS, preferred_element_type=F32)
                d_m_t = lax.dot_general(xm, dym, NT_DIMS, preferred_element_type=F32)
                rs = jnp.sum(d_m * m_low, axis=1, keepdims=True)
                cs = jnp.sum(d_m_t * m_upp, axis=1, keepdims=True)
                dacsp = dacsp + jnp.where(lane == e * SSD_HEAD_DIM, rs - cs, 0.0)
                d_cb = d_cb + d_m * low
                d_cb_t = d_cb_t + d_m_t * upp
            dxc_pieces.append(dxcp)
            dacs_pieces.append(dacsp)
        dxc = dxc + jnp.concatenate(dxc_pieces, axis=1)
        dacs = dacs + jnp.concatenate(dacs_pieces, axis=1)
        rowi = lax.broadcasted_iota(jnp.int32, (CHUNK, gw), 0)
        dacs = dacs + jnp.where(rowi == CHUNK - 1, last_add, 0.0)
        dc_ref[...] = d_c + jnp.dot(d_cb.astype(BF16), bv, preferred_element_type=F32)
        db_ref[...] = d_b + jnp.dot(d_cb_t.astype(BF16), cv, preferred_element_type=F32)
        dda = _dot3(_split3(dacs), _tri_ones(False), left=True)
        ddt_ref[...] = dda * a + dxc * x
        dxs_ref[...] = dxc * dt + d_ref[...] * dyv
        pa = jnp.sum(dda * dt, axis=0, keepdims=True)
        pd = jnp.sum(dyv * x, axis=0, keepdims=True)

        @pl.when(first)
        def _():
            da_ref[...] = pa
            dd_ref[...] = pd

        @pl.when(jnp.logical_not(first))
        def _():
            da_ref[...] += pa
            dd_ref[...] += pd

    rc = lambda c: nc - 1 - c
    tok = pl.BlockSpec((CHUNK, gw), lambda g, c: (rc(c), g))
    grp = pl.BlockSpec((CHUNK, SSD_STATE), lambda g, c: (rc(c), g))
    vec = pl.BlockSpec((1, gw), lambda g, c: (0, g))
    return pl.pallas_call(
        body, name=name, grid=(SSD_GROUPS, nc),
        in_specs=[tok, tok, grp, grp, pl.BlockSpec((SSD_STATE, CHUNK), lambda g, c: (g, rc(c))),
                  pl.BlockSpec((1, SSD_STATE, gw), lambda g, c: (rc(c), 0, g)), tok, vec, vec],
        out_specs=[tok, tok, grp, grp, vec, vec],
        out_shape=[_sds((t, SSD_INNER), F32), _sds((t, SSD_INNER), F32), _sds((t, gw), F32), _sds((t, gw), F32),
                   _sds((1, SSD_INNER), F32), _sds((1, SSD_INNER), F32)],
        scratch_shapes=[pltpu.VMEM((SSD_STATE, gw), F32)],
        compiler_params=_cp("parallel", "arbitrary"))(xs, dtx, bm, cm, cm_t, prev, dy, a_x, d_x)


def _gnorm_fwd(y, proj, g, *, name):
    t = y.shape[0]
    tr = _tile(t, 640)
    zb = OFF_Z // GROUP_W

    def body(y_ref, z_ref, g_ref, o_ref):
        z = z_ref[...]
        v = y_ref[...] * (z * _sigmoid(z))
        r = lax.rsqrt(jnp.mean(v * v, axis=1, keepdims=True) + RMS_EPS)
        o_ref[...] = (v * r * g_ref[...]).astype(BF16)

    blk = pl.BlockSpec((tr, GROUP_W), lambda i, j: (i, j))
    return pl.pallas_call(
        body, name=name, grid=(t // tr, SSD_GROUPS),
        in_specs=[blk, pl.BlockSpec((tr, GROUP_W), lambda i, j: (i, zb + j)),
                  pl.BlockSpec((1, GROUP_W), lambda i, j: (0, j))],
        out_specs=blk, out_shape=_sds((t, SSD_INNER), BF16),
        compiler_params=_cp("parallel", "parallel"))(y, proj, g)


def _gnorm_bwd(dout, y, proj, g, *, name):
    t = y.shape[0]
    tr = _tile(t, 640)
    zb = OFF_Z // GROUP_W

    def body(do_ref, y_ref, z_ref, g_ref, dy_ref, dz_ref, dg_ref):
        i = pl.program_id(1)
        z = z_ref[...]
        yv = y_ref[...]
        sg = _sigmoid(z)
        sz = z * sg
        v = yv * sz
        r = lax.rsqrt(jnp.mean(v * v, axis=1, keepdims=True) + RMS_EPS)
        dov = do_ref[...].astype(F32)
        gdo = dov * g_ref[...]
        m = jnp.mean(v * gdo, axis=1, keepdims=True)
        dv = r * gdo - v * (r * r * r) * m
        dy_ref[...] = dv * sz
        dz_ref[...] = (dv * yv * (sg * (1.0 + z * (1.0 - sg)))).astype(BF16)
        pg = jnp.sum(dov * v * r, axis=0, keepdims=True)

        @pl.when(i == 0)
        def _():
            dg_ref[...] = pg

        @pl.when(i > 0)
        def _():
            dg_ref[...] += pg

    blk = pl.BlockSpec((tr, GROUP_W), lambda j, i: (i, j))
    vec = pl.BlockSpec((1, GROUP_W), lambda j, i: (0, j))
    return pl.pallas_call(
        body, name=name, grid=(SSD_GROUPS, t // tr),
        in_specs=[blk, blk, pl.BlockSpec((tr, GROUP_W), lambda j, i: (i, zb + j)), vec],
        out_specs=[blk, blk, vec],
        out_shape=[_sds((t, SSD_INNER), F32), _sds((t, SSD_INNER), BF16), _sds((1, SSD_INNER), F32)],
        compiler_params=_cp("parallel", "arbitrary"))(dout, y, proj, g)


def _mix_fwd(proj, ya, ys, *, name):
    t, d = ya.shape
    tr = _tile(t, 640)
    ba, bs = OFF_GA // d, OFF_GS // d

    def body(ga_ref, gs_ref, ya_ref, ys_ref, o_ref):
        o_ref[...] = (_sigmoid(ga_ref[...]) * ya_ref[...] + _sigmoid(gs_ref[...]) * ys_ref[...]).astype(BF16)

    blk = pl.BlockSpec((tr, d), lambda i: (i, 0))
    return pl.pallas_call(
        body, name=name, grid=(t // tr,),
        in_specs=[pl.BlockSpec((tr, d), lambda i: (i, ba)), pl.BlockSpec((tr, d), lambda i: (i, bs)), blk, blk],
        out_specs=blk, out_shape=_sds((t, d), BF16), compiler_params=_cp("parallel"))(proj, proj, ya, ys)


def _mix_bwd(dmix, proj, ya, ys, *, name):
    t, d = ya.shape
    tr = _tile(t, 640)
    ba, bs = OFF_GA // d, OFF_GS // d

    def body(dm_ref, ga_ref, gs_ref, ya_ref, ys_ref, dya_ref, dys_ref, dga_ref, dgs_ref):
        dm = dm_ref[...]
        sa = _sigmoid(ga_ref[...])
        ss = _sigmoid(gs_ref[...])
        dya_ref[...] = (sa * dm).astype(BF16)
        dys_ref[...] = (ss * dm).astype(BF16)
        dga_ref[...] = (dm * ya_ref[...] * sa * (1.0 - sa)).astype(BF16)
        dgs_ref[...] = (dm * ys_ref[...] * ss * (1.0 - ss)).astype(BF16)

    blk = pl.BlockSpec((tr, d), lambda i: (i, 0))
    return pl.pallas_call(
        body, name=name, grid=(t // tr,),
        in_specs=[blk, pl.BlockSpec((tr, d), lambda i: (i, ba)), pl.BlockSpec((tr, d), lambda i: (i, bs)), blk, blk],
        out_specs=[blk] * 4, out_shape=[_sds((t, d), BF16)] * 4,
        compiler_params=_cp("parallel"))(dmix, proj, proj, ya, ys)


def _loss_grad(h, target, *, name):
    t, d = h.shape
    tr = LANES
    assert ROW0 == tr

    def body(h_ref, t_ref, dh_ref, loss_ref):
        i = pl.program_id(0)

        @pl.when(i == 0)
        def _():
            dh_ref[...] = jnp.zeros((tr, d), F32)
            loss_ref[...] = jnp.zeros((1, LANES), F32)

        @pl.when(i > 0)
        def _():
            err = h_ref[...] - t_ref[...]
            dh_ref[...] = err * (1.0 / d)
            part = jnp.sum(jnp.sum(err * err, axis=1, keepdims=True), axis=0, keepdims=True)
            loss_ref[...] += jnp.broadcast_to(part * (0.5 / d), (1, LANES))

    blk = pl.BlockSpec((tr, d), lambda i: (i, 0))
    return pl.pallas_call(
        body, name=name, grid=(t // tr,),
        in_specs=[blk, pl.BlockSpec((tr, d), lambda i: (jnp.maximum(i - 1, 0), 0))],
        out_specs=[blk, pl.BlockSpec((1, LANES), lambda i: (0, 0))],
        out_shape=[_sds((t, d), F32), _sds((1, LANES), F32)],
        compiler_params=_cp("arbitrary"))(h, target)


def _adamw_update(gv, wv, mv, vv):
    c1 = 1.0 - ADAM_B1 ** ADAM_STEP
    c2 = 1.0 - ADAM_B2 ** ADAM_STEP
    nm = ADAM_B1 * mv + (1.0 - ADAM_B1) * gv
    nv = ADAM_B2 * vv + (1.0 - ADAM_B2) * (gv * gv)
    return -ADAM_LR * ((nm / c1) / (jnp.sqrt(nv / c2) + ADAM_EPS) + ADAM_WD * wv), nm, nv


def _as_2d(a):
    return a.reshape(1, -1) if a.ndim == 1 else a.reshape(-1, a.shape[-1])


def _adamw(g, w, m, v, *, name):
    shape = w.shape
    g2, w2, m2, v2 = (_as_2d(a) for a in (g, w, m, v))
    r, c = w2.shape
    tr = _tile(r, 256, base=8)

    def body(g_ref, w_ref, m_ref, v_ref, d_ref, nm_ref, nv_ref):
        d_ref[...], nm_ref[...], nv_ref[...] = _adamw_update(g_ref[...], w_ref[...], m_ref[...], v_ref[...])

    blk = pl.BlockSpec((tr, c), lambda i: (i, 0))
    outs = pl.pallas_call(body, name=name, grid=(r // tr,), in_specs=[blk] * 4, out_specs=[blk] * 3,
                          out_shape=[_sds((r, c), F32)] * 3, compiler_params=_cp("parallel"))(g2, w2, m2, v2)
    return [o.reshape(shape) for o in outs]


def _adamw_small(items, *, name):
    n = len(items)
    shapes = [it[1].shape for it in items]
    flat = [_as_2d(a) for it in items for a in it]

    def body(*refs):
        ins, outs = refs[:4 * n], refs[4 * n:]
        for k in range(n):
            g_ref, w_ref, m_ref, v_ref = ins[4 * k:4 * k + 4]
            d_ref, nm_ref, nv_ref = outs[3 * k:3 * k + 3]
            d_ref[...], nm_ref[...], nv_ref[...] = _adamw_update(g_ref[...], w_ref[...], m_ref[...], v_ref[...])

    out_shape = [_sds(flat[4 * k + 1].shape, F32) for k in range(n) for _ in range(3)]
    outs = pl.pallas_call(body, name=name, out_shape=out_shape,
                          compiler_params=pltpu.CompilerParams(vmem_limit_bytes=VMEM_LIMIT_BYTES))(*flat)
    return [[outs[3 * k + q].reshape(shapes[k]) for q in range(3)] for k in range(n)]


def _pair_add(g5, got, core, *, name):
    n, _, r, _ = g5.shape
    tr = _tile(r, 1024, base=8)

    def body(c_ref, a_ref, b_ref, o_ref):
        o_ref[...] = a_ref[0] + b_ref[...]

    grid_spec = pltpu.PrefetchScalarGridSpec(
        num_scalar_prefetch=1, grid=(n, r // tr),
        in_specs=[pl.BlockSpec((1, 1, tr, LANES), lambda s, i, c_ref: (s, c_ref[0], i, 0)),
                  pl.BlockSpec((1, tr, LANES), lambda s, i, c_ref: (s, i, 0))],
        out_specs=pl.BlockSpec((1, tr, LANES), lambda s, i, c_ref: (s, i, 0)))
    return pl.pallas_call(body, name=name, grid_spec=grid_spec, out_shape=_sds(got.shape, F32),
                          compiler_params=_cp("parallel", "parallel"))(core, g5, got)


def _sum_chips(q, core, *, name):
    n, r, _ = q.shape
    tr = _tile(r, 1024, base=8)

    def body(c_ref, q_ref, o_ref):
        tot = q_ref[0]
        for s in range(1, n):
            tot = tot + q_ref[s]
        o_ref[0] = tot

    grid_spec = pltpu.PrefetchScalarGridSpec(
        num_scalar_prefetch=1, grid=(r // tr,),
        in_specs=[pl.BlockSpec((n, tr, LANES), lambda i, c_ref: (0, i, 0))],
        out_specs=pl.BlockSpec((1, tr, LANES), lambda i, c_ref: (c_ref[0], i, 0)))
    return pl.pallas_call(body, name=name, grid_spec=grid_spec, out_shape=_sds((2, r, LANES), F32),
                          compiler_params=_cp("parallel"))(core, q)


_ANY = pl.BlockSpec(memory_space=pl.ANY)
_MESH = pl.DeviceIdType.MESH


def _place():
    x, y, c = lax.axis_index("x"), lax.axis_index("y"), lax.axis_index("c")
    return x, y, c, [(1 - x, y), (x, 1 - y), (1 - x, 1 - y)]


def _chip_allgather(mine, *, name):
    na = len(mine)

    def body(*refs):
        x_refs, o_refs = refs[:na], refs[na:2 * na]
        send_sems, recv_sems, local_sems = refs[2 * na:]
        x, y, c, chips = _place()
        k = 2 * x + y

        def copy(a, n, src, dst, to):
            return pltpu.make_async_remote_copy(src_ref=src, dst_ref=dst, send_sem=send_sems.at[6 * a + n],
                                                recv_sem=recv_sems.at[6 * a + n], device_id=to, device_id_type=_MESH)

        locals_ = [pltpu.make_async_copy(x_refs[a], o_refs[a].at[k], local_sems.at[a]) for a in range(na)]
        for cp in locals_:
            cp.start()
        sends = [copy(a, n, x_refs[a].at[c], o_refs[a].at[k, c], (cx, cy, c))
                 for a in range(na) for n, (cx, cy) in enumerate(chips)]
        for cp in sends:
            cp.start()
        passed = []
        for a in range(na):
            for n, (cx, cy) in enumerate(chips):
                slab = o_refs[a].at[2 * cx + cy, c]
                copy(a, n, slab, slab, (cx, cy, c)).wait_recv()
                fw = copy(a, 3 + n, slab, slab, (x, y, 1 - c))
                fw.start()
                passed.append(fw)
        for a in range(na):
            for n, (cx, cy) in enumerate(chips):
                slab = o_refs[a].at[2 * cx + cy, 1 - c]
                copy(a, 3 + n, slab, slab, (x, y, 1 - c)).wait_recv()
        for cp in sends + passed:
            cp.wait_send()
        for cp in locals_:
            cp.wait()

    return pl.pallas_call(
        body, name=name, in_specs=[_ANY] * na, out_specs=[_ANY] * na,
        out_shape=[_sds((N_CHIPS,) + a.shape, a.dtype) for a in mine],
        scratch_shapes=[pltpu.SemaphoreType.DMA((6 * na,)), pltpu.SemaphoreType.DMA((6 * na,)),
                        pltpu.SemaphoreType.DMA((na,))])(*mine)


def _sibling_swap(g5, *, name):
    n, _, r, _ = g5.shape

    def body(x_ref, o_ref, send_sems, recv_sems):
        x, y, c, _ = _place()
        cps = [pltpu.make_async_remote_copy(src_ref=x_ref.at[s, 1 - c], dst_ref=o_ref.at[s], send_sem=send_sems.at[s],
                                            recv_sem=recv_sems.at[s], device_id=(x, y, 1 - c), device_id_type=_MESH)
               for s in range(n)]
        for cp in cps:
            cp.start()
        for cp in cps:
            cp.wait()

    return pl.pallas_call(
        body, name=name, in_specs=[_ANY], out_specs=_ANY, out_shape=_sds((n, r, LANES), g5.dtype),
        scratch_shapes=[pltpu.SemaphoreType.DMA((n,)), pltpu.SemaphoreType.DMA((n,))])(g5)


def _chip_exchange(h, *, name):
    def body(h_ref, q_ref, send_sems, recv_sems, local_sem):
        x, y, c, chips = _place()
        k = 2 * x + y
        local = pltpu.make_async_copy(h_ref.at[k], q_ref.at[k], local_sem)
        local.start()
        cps = []
        for n, (cx, cy) in enumerate(chips):
            kk = 2 * cx + cy
            cps.append(pltpu.make_async_remote_copy(src_ref=h_ref.at[kk], dst_ref=q_ref.at[k], send_sem=send_sems.at[n],
                                                    recv_sem=recv_sems.at[n], device_id=(cx, cy, c),
                                                    device_id_type=_MESH))
        for cp in cps:
            cp.start()
        for n, (cx, cy) in enumerate(chips):
            kk = 2 * cx + cy
            pltpu.make_async_remote_copy(src_ref=h_ref.at[kk], dst_ref=q_ref.at[kk], send_sem=send_sems.at[n],
                                         recv_sem=recv_sems.at[n], device_id=(cx, cy, c),
                                         device_id_type=_MESH).wait_recv()
        for cp in cps:
            cp.wait_send()
        local.wait()

    return pl.pallas_call(
        body, name=name, in_specs=[_ANY], out_specs=_ANY, out_shape=_sds(h.shape, h.dtype),
        scratch_shapes=[pltpu.SemaphoreType.DMA((3,)), pltpu.SemaphoreType.DMA((3,)), pltpu.SemaphoreType.DMA])(h)


def _sibling_allgather(buf, *, name):
    def body(x_ref, o_ref, send_sem, recv_sem):
        x, y, c, _ = _place()
        cp = pltpu.make_async_remote_copy(src_ref=x_ref.at[c], dst_ref=o_ref.at[c], send_sem=send_sem,
                                          recv_sem=recv_sem, device_id=(x, y, 1 - c), device_id_type=_MESH)
        cp.start()
        pltpu.make_async_remote_copy(src_ref=x_ref.at[c], dst_ref=o_ref.at[1 - c], send_sem=send_sem,
                                     recv_sem=recv_sem, device_id=(x, y, 1 - c), device_id_type=_MESH).wait_recv()
        cp.wait_send()

    return pl.pallas_call(
        body, name=name, in_specs=[_ANY], out_specs=_ANY, out_shape=_sds(buf.shape, buf.dtype),
        input_output_aliases={0: 0},
        scratch_shapes=[pltpu.SemaphoreType.DMA, pltpu.SemaphoreType.DMA])(buf)


def _rows_of(flat, align):
    n = flat.shape[-1]
    rows = -(-n // LANES)
    rows = -(-rows // align) * align
    pad = rows * LANES - n
    if pad:
        flat = jnp.pad(flat, [(0, 0)] * (flat.ndim - 1) + [(0, pad)])
    return flat.reshape(flat.shape[:-1] + (rows, LANES))


def _chip_segments(g, kind):
    if kind == "col":
        n = g.shape[-1] // N_CHIPS
        s = g.reshape(g.shape[:-1] + (N_CHIPS, n))
        return jnp.moveaxis(s, -2, 0).reshape(N_CHIPS, -1)
    k = g.shape[-2] // N_CHIPS
    s = g.reshape(g.shape[:-2] + (N_CHIPS, k, g.shape[-1]))
    return jnp.moveaxis(s, -3, 0).reshape(N_CHIPS, -1)


def _join(blocks, kind):
    if kind == "col":
        s = jnp.moveaxis(blocks, 0, -2)
        return s.reshape(s.shape[:-2] + (s.shape[-2] * s.shape[-1],))
    return blocks.reshape((blocks.shape[0] * blocks.shape[1],) + blocks.shape[2:])


def _gather_weights(local):
    mine = []
    for n, _ in SHARDED:
        a = local[n]
        if n == "w_in":
            a = jnp.swapaxes(a, 1, 2)
        if n == "meta_tokens":
            a = a.reshape(2, N_META // 2, a.shape[-1])
        mine.append(a.astype(BF16) if n in GATHER_BF16 else a)
    got = _chip_allgather(mine, name="gather_weights")
    return {n: g for (n, _), g in zip(SHARDED, got)}


def _rope_tables(t):
    half = QK_ROPE // 2
    inv_freq = 1.0 / (ROPE_THETA ** (jnp.arange(0, QK_ROPE, 2, dtype=F32) / QK_ROPE))
    pos = jnp.maximum(jnp.arange(t, dtype=F32) - PAD, 0.0)
    ang = pos[:, None] * inv_freq[None, :]
    cos, sin = jnp.cos(ang), jnp.sin(ang)
    z = jnp.zeros((t, half), F32)
    z2 = jnp.zeros((t, LANES - QK_ROPE), F32)
    return (jnp.concatenate([cos, cos, z2], axis=1), jnp.concatenate([-sin, z, z2], axis=1),
            jnp.concatenate([z, sin, z2], axis=1))


def _expand_matrix():
    lane = np.arange(SSD_INNER) // SSD_HEAD_DIM
    e = (np.arange(LANES)[:, None] == lane[None, :]).astype(np.float32)
    return jnp.asarray(e, BF16)


def _layer_weights(full, rep, i):
    w = {}
    kinds = dict(SHARDED)
    whole = lambda n: _join(full[n][:, i], kinds[n])
    wt = _join(full["w_in"][:, i], "row")
    zr = lambda n: jnp.zeros((n, D_MODEL), BF16)
    w["w_in_t"] = jnp.concatenate(
        [wt[0:1024], wt[1088:3136], wt[3136:6208], wt[6240:7264], wt[7264:8288],
         wt[1024:1088], zr(LANES - QK_ROPE), wt[6208:6240], zr(LANES - SSD_HEADS)], axis=0)
    wq = whole("w_q_b").reshape(Q_LORA, MLA_HEADS, QK_NOPE + QK_ROPE)
    w["w_q_b"] = jnp.pad(wq, ((0, 0), (0, 0), (0, QHEAD - QK_NOPE - QK_ROPE))).reshape(Q_LORA, MLA_HEADS * QHEAD)
    wkv = whole("w_kv_b").reshape(KV_LORA, MLA_HEADS, 2, QK_NOPE)
    w["w_kv_b"] = jnp.swapaxes(wkv, 1, 2).reshape(KV_LORA, 2 * MLA_HEADS * QK_NOPE)
    for n in ("w_o_attn", "w_o_ssd", "w_out", "w_down", "ssd_conv_w"):
        w[n] = whole(n)
    w_up = whole("w_up")
    w["w_up_g"] = w_up[:, :D_FF]
    w["w_up_v"] = w_up[:, D_FF:]
    ffn_w = whole("ffn_conv_w")
    w["ffn_conv_wg"] = ffn_w[:, :D_FF]
    w["ffn_conv_wv"] = ffn_w[:, D_FF:]
    row = lambda v: v.reshape(1, -1)
    w["q_norm_g"] = row(rep["q_norm_g"][i])
    w["kv_norm_g"] = row(rep["kv_norm_g"][i])
    w["ssd_conv_b"] = row(rep["ssd_conv_b"][i])
    w["dt_bias"] = row(jnp.pad(rep["dt_bias"][i], (0, LANES - SSD_HEADS)))
    a = -jnp.exp(rep["a_log"][i])
    w["a"] = a
    w["a_x"] = row(jnp.repeat(a, SSD_HEAD_DIM))
    w["d_x"] = row(jnp.repeat(rep["d_skip"][i], SSD_HEAD_DIM))
    w["ssd_norm_g"] = row(rep["ssd_norm_g"][i])
    w["ffn_conv_bg"] = row(rep["ffn_conv_b"][i][:D_FF])
    w["ffn_conv_bv"] = row(rep["ffn_conv_b"][i][D_FF:])
    for n in ("ln1_g", "ln1_b", "ln2_g", "ln2_b"):
        w[n] = row(rep[n][i])
    return w


def _layer_fwd(h, w, tabs, expand, tag):
    cos, sin_a, sin_b = tabs
    s = {"h": h}
    proj = _mm(h, w["w_in_t"], trans_b=True, name=tag + "in_proj", tn=768)
    s["proj"] = proj
    qn = _rms_fwd(proj, OFF_Q, Q_LORA, w["q_norm_g"], name=tag + "q_norm")
    q_raw = _mm(qn, w["w_q_b"], name=tag + "q_up")
    q = _rope_q_fwd(q_raw, cos, sin_a, sin_b, name=tag + "q_rope")
    kvn = _rms_fwd(proj, OFF_KV, KV_LORA, w["kv_norm_g"], name=tag + "kv_norm")
    kv = _mm(kvn, w["w_kv_b"], name=tag + "kv_up", out_dtype=BF16)
    kpe = _rope_k_fwd(proj, cos, sin_a, sin_b, name=tag + "k_rope")
    o, lse = _flash_fwd(q, kv, kpe, name=tag + "attn")
    ya = _mm(o, w["w_o_attn"], name=tag + "attn_out")
    s.update(qn=qn, q=q, kvn=kvn, kv=kv, kpe=kpe, o=o, lse=lse, ya=ya)
    xs, bm, cm, dtx = _ssd_prep_fwd(proj, w["ssd_conv_w"], w["ssd_conv_b"], w["dt_bias"], expand, name=tag + "ssd_prep")
    y, prev = _ssd_fwd(xs, dtx, bm, cm, bm.T, w["a_x"], w["d_x"], name=tag + "ssd_scan")
    yn = _gnorm_fwd(y, proj, w["ssd_norm_g"], name=tag + "ssd_norm")
    ys = _mm(yn, w["w_o_ssd"], name=tag + "ssd_out")
    s.update(xs=xs, bm=bm, cm=cm, dtx=dtx, y=y, prev=prev, yn=yn, ys=ys)
    mixed = _mix_fwd(proj, ya, ys, name=tag + "mix")
    br = _mm(mixed, w["w_out"], name=tag + "mix_out")
    pre1, h1 = _ln_fwd(h, br, w["ln1_g"], w["ln1_b"], name=tag + "ln1")
    s.update(mixed=mixed, pre1=pre1, h1=h1)
    ug = _mm(h1, w["w_up_g"], name=tag + "up_g", tn=1408)
    uv = _mm(h1, w["w_up_v"], name=tag + "up_v", tn=1408)
    act = _ffn_act_fwd(ug, uv, w["ffn_conv_wg"], w["ffn_conv_wv"], w["ffn_conv_bg"], w["ffn_conv_bv"],
                       name=tag + "ffn_act")
    ffn = _mm(act, w["w_down"], name=tag + "down")
    pre2, h2 = _ln_fwd(h1, ffn, w["ln2_g"], w["ln2_b"], name=tag + "ln2")
    s.update(ug=ug, uv=uv, act=act, pre2=pre2)
    return h2, s


def _layer_bwd(dh2, w, s, tabs, reduce_m, tag):
    cos, sin_a, sin_b = tabs
    g = {}
    proj = s["proj"]
    dpre2, g["ln2_g"], g["ln2_b"] = _ln_bwd(dh2, s["pre2"], w["ln2_g"], name=tag + "ln2_bwd")
    g["w_down"] = _mm_tn(s["act"], dpre2, name=tag + "down_dw")
    dact = _mm(dpre2, w["w_down"], trans_b=True, name=tag + "down_dx", tn=1408)
    dcg, dcv, dwg, dwv, dbg, dbv = _ffn_act_bwd(s["ug"], s["uv"], dact, w["ffn_conv_wg"], w["ffn_conv_wv"],
                                                w["ffn_conv_bg"], w["ffn_conv_bv"], name=tag + "ffn_act_bwd")
    g["ffn_conv_w"] = jnp.concatenate([dwg, dwv], axis=1)
    g["ffn_conv_b"] = jnp.concatenate([dbg, dbv], axis=1).reshape(-1)
    dug = _conv_bwd_input(dcg, w["ffn_conv_wg"], FFN_CONV, name=tag + "ffn_conv_bwd_g", tc=1408)
    duv = _conv_bwd_input(dcv, w["ffn_conv_wv"], FFN_CONV, name=tag + "ffn_conv_bwd_v", tc=1408)
    g["w_up"] = jnp.concatenate([_mm_tn(s["h1"], dug, name=tag + "up_g_dw", tn=1408),
                                 _mm_tn(s["h1"], duv, name=tag + "up_v_dw", tn=1408)], axis=1)
    dh1 = _mm(dug, w["w_up_g"], trans_b=True, add=dpre2, add_scale=ALPHA, name=tag + "up_g_dx")
    dh1 = _mm(duv, w["w_up_v"], trans_b=True, add=dh1, name=tag + "up_v_dx")
    dpre1, g["ln1_g"], g["ln1_b"] = _ln_bwd(dh1, s["pre1"], w["ln1_g"], name=tag + "ln1_bwd")
    g["w_out"] = _mm_tn(s["mixed"], dpre1, name=tag + "mix_out_dw")
    dmix = _mm(dpre1, w["w_out"], trans_b=True, name=tag + "mix_out_dx")
    dya, dys, dga, dgs = _mix_bwd(dmix, proj, s["ya"], s["ys"], name=tag + "mix_bwd")
    g["w_o_attn"] = _mm_tn(s["o"], dya, name=tag + "attn_out_dw")
    do = _mm(dya, w["w_o_attn"], trans_b=True, out_dtype=BF16, name=tag + "attn_out_dx")
    delta = _attn_delta(do, s["o"], name=tag + "attn_delta")
    by_tile = lambda a: a.reshape(MLA_HEADS, -1, _tile(a.shape[1], 640))
    dq, dkn, dkp, dv = _flash_bwd(s["q"], s["kv"], s["kpe"], do, by_tile(s["lse"]), by_tile(delta),
                                  name=tag + "attn_bwd")
    dq_raw = _rope_q_bwd(dq, cos, sin_a, sin_b, name=tag + "q_rope_bwd")
    dwq = _mm_tn(s["qn"], dq_raw, name=tag + "q_up_dw")
    g["w_q_b"] = dwq.reshape(Q_LORA, MLA_HEADS, QHEAD)[:, :, :QK_NOPE + QK_ROPE].reshape(Q_LORA, -1)
    dqn = _mm(dq_raw, w["w_q_b"], trans_b=True, out_dtype=BF16, name=tag + "q_up_dx")
    dqlat, dgq = _rms_bwd(dqn, proj, OFF_Q, Q_LORA, w["q_norm_g"], name=tag + "q_norm_bwd")
    g["q_norm_g"] = dgq.reshape(-1)
    dkv = jnp.concatenate([dkn, dv], axis=1)
    dwkv = _mm_tn(s["kvn"], dkv, name=tag + "kv_up_dw")
    g["w_kv_b"] = jnp.swapaxes(dwkv.reshape(KV_LORA, 2, MLA_HEADS, QK_NOPE), 1, 2).reshape(KV_LORA, -1)
    dkvn = _mm(dkv, w["w_kv_b"], trans_b=True, out_dtype=BF16, name=tag + "kv_up_dx")
    dkvlat, dgkv = _rms_bwd(dkvn, proj, OFF_KV, KV_LORA, w["kv_norm_g"], name=tag + "kv_norm_bwd")
    g["kv_norm_g"] = dgkv.reshape(-1)
    dkpe = _rope_k_bwd(dkp, cos, sin_a, sin_b, name=tag + "k_rope_bwd")
    g["w_o_ssd"] = _mm_tn(s["yn"], dys, name=tag + "ssd_out_dw")
    dyn = _mm(dys, w["w_o_ssd"], trans_b=True, out_dtype=BF16, name=tag + "ssd_out_dx")
    dy, dz, dgn = _gnorm_bwd(dyn, s["y"], proj, w["ssd_norm_g"], name=tag + "ssd_norm_bwd")
    g["ssd_norm_g"] = dgn.reshape(-1)
    dxs, ddtx, dbm, dcm, da_x, dd_x = _ssd_bwd(s["xs"], s["dtx"], s["bm"], s["cm"], s["cm"].T, s["prev"], dy,
                                               w["a_x"], w["d_x"], name=tag + "ssd_scan_bwd")
    g["a_log"] = da_x.reshape(SSD_HEADS, SSD_HEAD_DIM).sum(axis=1) * w["a"]
    g["d_skip"] = dd_x.reshape(SSD_HEADS, SSD_HEAD_DIM).sum(axis=1)
    dconv, ddtr, dcw, dcb, ddtb = _ssd_prep_bwd_a(proj, dxs, dbm, dcm, ddtx, w["ssd_conv_w"], w["ssd_conv_b"],
                                                  w["dt_bias"], reduce_m, name=tag + "ssd_prep_bwd")
    g["ssd_conv_w"] = dcw
    g["ssd_conv_b"] = dcb.reshape(-1)
    g["dt_bias"] = ddtb.reshape(-1)[:SSD_HEADS]
    dxbc = _conv_bwd_input(dconv, w["ssd_conv_w"], SSD_CONV, name=tag + "ssd_conv_bwd", tc=1024)
    h = s["h"]
    comps = ((dqlat, OFF_Q), (dkvlat, OFF_KV), (dz, OFF_Z), (dxbc, OFF_XBC), (dga, OFF_GA), (dgs, OFF_GS),
             (dkpe, OFF_KPE), (ddtr, OFF_DT))
    dws = {off: _mm_tn(dc, h, name=f"{tag}in_dw{n}") for n, (dc, off) in enumerate(comps)}
    with_w = lambda group: [(dc, w["w_in_t"][off:off + dc.shape[1]]) for dc, off in group]
    wide = [c for c in comps if c[1] in (OFF_Z, OFF_XBC)]
    rest = [c for c in comps if c[1] not in (OFF_Z, OFF_XBC)]
    dh = _mm_sum(with_w(wide), dpre1, add_scale=ALPHA, name=tag + "in_dx_wide")
    dh = _mm_sum(with_w(rest), dh, name=tag + "in_dx_rest")
    g["w_in"] = jnp.concatenate([dws[OFF_Q], dws[OFF_KV], dws[OFF_KPE][:QK_ROPE], dws[OFF_Z], dws[OFF_XBC],
                                 dws[OFF_DT][:SSD_HEADS], dws[OFF_GA], dws[OFF_GS]], axis=0)
    return dh, g


def _local_step(x, target, full, rep):
    seq = x.shape[0]
    t = seq + ROW0
    tabs = _rope_tables(t)
    expand = _expand_matrix()
    reduce_m = expand.T
    meta = _join(full["meta_tokens"].reshape(N_CHIPS, N_META, -1), "col")
    xin = jnp.concatenate([jnp.zeros((PAD, D_MODEL), F32), meta, x], axis=0)
    row = lambda v: v.reshape(1, -1)
    _, h = _ln_fwd(xin, None, row(rep["emb_ln_g"]), row(rep["emb_ln_b"]), name="emb_ln")
    ws, saved = [], []
    for i in range(DEPTH):
        w = _layer_weights(full, rep, i)
        h, s = _layer_fwd(h, w, tabs, expand, f"l{i}_")
        ws.append(w)
        saved.append(s)
    dh, loss = _loss_grad(h, target, name="loss")
    layer_grads = [None] * DEPTH
    for i in reversed(range(DEPTH)):
        dh, layer_grads[i] = _layer_bwd(dh, ws[i], saved[i], tabs, reduce_m, f"l{i}_")
    dxin, dg, db = _ln_bwd(dh, xin, row(rep["emb_ln_g"]), name="emb_ln_bwd")
    grads = {n: jnp.stack([layer_grads[i][n] for i in range(DEPTH)]) for n in layer_grads[0]}
    grads["emb_ln_g"] = dg.reshape(-1)
    grads["emb_ln_b"] = db.reshape(-1)
    grads["meta_tokens"] = dxin[PAD:ROW0]
    return loss, dxin[ROW0:], grads


def _reduce_grads(grads):
    segs = [_chip_segments(grads[n], "row" if n == "w_in" else kind) for n, kind in SHARDED]
    small = jnp.concatenate([grads[n].reshape(-1) for n in REPLICATED])
    segs.append(jnp.broadcast_to(small[None], (N_CHIPS, small.shape[0])))
    g4 = _rows_of(jnp.concatenate(segs, axis=1), REDUCE_ROW_ALIGN)
    r = g4.shape[1]
    g5 = g4.reshape(N_CHIPS, 2, r // 2, LANES)
    core = lax.axis_index("c").astype(jnp.int32).reshape(1)
    got = _sibling_swap(g5, name="reduce_pair_swap")
    pair = _pair_add(g5, got, core, name="reduce_pair_add")
    parts = _chip_exchange(pair, name="reduce_chip_exchange")
    half = _sum_chips(parts, core, name="reduce_chip_sum")
    both = _sibling_allgather(half, name="reduce_pair_gather")
    return both.reshape(r, LANES)


def kernel(x, meta_tokens, emb_ln_g, emb_ln_b, w_in, q_norm_g, w_q_b, kv_norm_g, w_kv_b, w_o_attn, ssd_conv_w, ssd_conv_b, dt_bias, a_log, d_skip, ssd_norm_g, w_o_ssd, w_out, ln1_g, ln1_b, w_up, ffn_conv_w, ffn_conv_b, w_down, ln2_g, ln2_b, loss_target, m_meta_tokens, m_emb_ln_g, m_emb_ln_b, m_w_in, m_q_norm_g, m_w_q_b, m_kv_norm_g, m_w_kv_b, m_w_o_attn, m_ssd_conv_w, m_ssd_conv_b, m_dt_bias, m_a_log, m_d_skip, m_ssd_norm_g, m_w_o_ssd, m_w_out, m_ln1_g, m_ln1_b, m_w_up, m_ffn_conv_w, m_ffn_conv_b, m_w_down, m_ln2_g, m_ln2_b, v_meta_tokens, v_emb_ln_g, v_emb_ln_b, v_w_in, v_q_norm_g, v_w_q_b, v_kv_norm_g, v_w_kv_b, v_w_o_attn, v_ssd_conv_w, v_ssd_conv_b, v_dt_bias, v_a_log, v_d_skip, v_ssd_norm_g, v_w_o_ssd, v_w_out, v_ln1_g, v_ln1_b, v_w_up, v_ffn_conv_w, v_ffn_conv_b, v_w_down, v_ln2_g, v_ln2_b):
    given = dict(locals())
    local_w = {n: given[n] for n in WEIGHTS}
    local_m = {n: given["m_" + n] for n in WEIGHTS}
    local_v = {n: given["v_" + n] for n in WEIGHTS}
    full = _gather_weights(local_w)
    rep = {n: local_w[n] for n in REPLICATED}
    loss, grad_x, grads = _local_step(x[0], loss_target[0], full, rep)
    g_flat = _reduce_grads(grads)
    flat = g_flat.reshape(-1)
    grad, off = {}, 0
    for n in [n for n, _ in SHARDED] + list(REPLICATED):
        shape = local_w[n].shape
        size = int(np.prod(shape))
        piece = flat[off:off + size]
        if n == "w_in":
            grad[n] = jnp.swapaxes(piece.reshape(shape[0], shape[2], shape[1]), 1, 2)
        else:
            grad[n] = piece.reshape(shape)
        off += size
    upd = {}
    small = [n for n in WEIGHTS if n not in GATHER_BF16]
    for n in GATHER_BF16:
        upd[n] = _adamw(grad[n], local_w[n], local_m[n], local_v[n], name="adamw_" + n)
    res = _adamw_small([(grad[n], local_w[n], local_m[n], local_v[n]) for n in small], name="adamw_small")
    upd.update(zip(small, res))
    total = lax.psum(loss[0, 0], ("x", "y", "c"))
    outs = [total, grad_x[None]] + [grad[n] for n in WEIGHTS]
    for q in range(3):
        outs.extend(upd[n][q] for n in WEIGHTS)
    return tuple(outs)
```

```python
import functools
import math

import numpy as np
import jax
import jax.numpy as jnp
from jax import lax
from jax.experimental import pallas as pl
from jax.experimental.pallas import tpu as pltpu

F32 = jnp.float32
BF16 = jnp.bfloat16

D_MODEL = 1024
N_META = 16
DEPTH = 2
MLA_HEADS = 8
Q_LORA = 768
KV_LORA = 256
QK_NOPE = 128
QK_ROPE = 64
V_HEAD = 128
ROPE_THETA = 10000.0
NEG_INF = -1e30
PAD_KEY_SCORE = -1e30
SSD_INNER = 2048
SSD_HEAD_DIM = 64
SSD_HEADS = 32
SSD_GROUPS = 4
SSD_STATE = 128
SSD_CONV = 4
SSD_CONV_DIM = 3072
CHUNK = 128
D_FF = 2816
FFN_CONV = 3
LN_EPS = 1e-5
RMS_EPS = 1e-6
ALPHA = (2 * DEPTH) ** 0.25
ATTN_SCALE = (QK_NOPE + QK_ROPE) ** -0.5
LOG2E = math.log2(math.e)
LN2 = math.log(2.0)
Q_SCALE = ATTN_SCALE * LOG2E
ADAM_LR = 0.001
ADAM_B1 = 0.9
ADAM_B2 = 0.999
ADAM_EPS = 1e-08
ADAM_WD = 0.01
ADAM_STEP = 10

LANES = 128
PAD = 112
ROW0 = PAD + N_META
QHEAD = 256
GROUP_W = SSD_INNER // SSD_GROUPS
HALO = 8
VMEM_LIMIT_BYTES = 56 * 1024 * 1024
N_CHIPS = 4

OFF_Q, OFF_KV, OFF_Z, OFF_XBC, OFF_GA, OFF_GS, OFF_KPE, OFF_DT = 0, 768, 1024, 3072, 6144, 7168, 8192, 8320
IN_COLS_P = 8448

NT_DIMS = (((1,), (1,)), ((), ()))
NN_DIMS = (((1,), (0,)), ((), ()))
TN_DIMS = (((0,), (0,)), ((), ()))

SHARDED = (("meta_tokens", "col"), ("w_in", "col"), ("w_q_b", "col"), ("w_kv_b", "col"), ("w_o_attn", "row"),
           ("ssd_conv_w", "col"), ("w_o_ssd", "row"), ("w_out", "row"), ("w_up", "col"), ("ffn_conv_w", "col"),
           ("w_down", "row"))
REPLICATED = ("emb_ln_g", "emb_ln_b", "q_norm_g", "kv_norm_g", "ssd_conv_b", "dt_bias", "a_log", "d_skip",
              "ssd_norm_g", "ln1_g", "ln1_b", "ffn_conv_b", "ln2_g", "ln2_b")
WEIGHTS = ("meta_tokens", "emb_ln_g", "emb_ln_b", "w_in", "q_norm_g", "w_q_b", "kv_norm_g", "w_kv_b", "w_o_attn",
           "ssd_conv_w", "ssd_conv_b", "dt_bias", "a_log", "d_skip", "ssd_norm_g", "w_o_ssd", "w_out", "ln1_g",
           "ln1_b", "w_up", "ffn_conv_w", "ffn_conv_b", "w_down", "ln2_g", "ln2_b")
GATHER_BF16 = ("w_in", "w_q_b", "w_kv_b", "w_o_attn", "w_o_ssd", "w_out", "w_up", "w_down")
PART_SHARDED = tuple(item for item in SHARDED if item[0] != "meta_tokens")
PART_REPLICATED = ("q_norm_g", "kv_norm_g", "ssd_conv_b", "ssd_norm_g", "ln1_g", "ln1_b", "ffn_conv_b", "ln2_g", "ln2_b")
PART_TAIL = ("dt_bias", "a_log", "d_skip")
OUTSIDE_SHARDED = (("meta_tokens", "col"),)
OUTSIDE_REPLICATED = ("emb_ln_g", "emb_ln_b")
REDUCE_ROW_ALIGN = 512


def _tile(n, target, base=LANES):
    best = None
    d = base
    while d <= min(n, target):
        if n % d == 0:
            best = d
        d += base
    return n if best is None else best


def _cp(*sem):
    return pltpu.CompilerParams(dimension_semantics=sem, vmem_limit_bytes=VMEM_LIMIT_BYTES)


def _sds(shape, dtype):
    return jax.ShapeDtypeStruct(shape, dtype)


def _row_ids(i, tr, shape):
    return i * tr + lax.broadcasted_iota(jnp.int32, shape, 0)


def _sigmoid(x):
    return 1.0 / (1.0 + jnp.exp(-x))


def _mm(a, b, *, name, trans_b=False, out_dtype=F32, add=None, add_scale=1.0, tm=640, tn=1024, tk=1408):
    m, k_dim = a.shape
    n = b.shape[0] if trans_b else b.shape[1]
    tm, tn, tk = _tile(m, tm), _tile(n, tn), _tile(k_dim, tk)
    nk = k_dim // tk
    has_add = add is not None
    dims = NT_DIMS if trans_b else NN_DIMS

    def body(*refs):
        a_ref, b_ref = refs[0], refs[1]
        r_ref = refs[2] if has_add else None
        o_ref = refs[3] if has_add else refs[2]
        part = lax.dot_general(a_ref[...].astype(BF16), b_ref[...].astype(BF16), dims, preferred_element_type=F32)

        def finish(r):
            if has_add:
                r = r + add_scale * r_ref[...].astype(F32)
            o_ref[...] = r.astype(out_dtype)

        if nk == 1:
            finish(part)
        else:
            acc = refs[-1]
            kk = pl.program_id(2)

            @pl.when(kk == 0)
            def _():
                acc[...] = part

            @pl.when(kk > 0)
            def _():
                acc[...] += part

            @pl.when(kk == nk - 1)
            def _():
                finish(acc[...])

    in_specs = [pl.BlockSpec((tm, tk), lambda i, j, kk: (i, kk)),
                pl.BlockSpec((tn, tk), lambda i, j, kk: (j, kk)) if trans_b
                else pl.BlockSpec((tk, tn), lambda i, j, kk: (kk, j))]
    args = [a, b]
    if has_add:
        in_specs.append(pl.BlockSpec((tm, tn), lambda i, j, kk: (i, j)))
        args.append(add)
    return pl.pallas_call(
        body, name=name, grid=(m // tm, n // tn, nk), in_specs=in_specs,
        out_specs=pl.BlockSpec((tm, tn), lambda i, j, kk: (i, j)),
        out_shape=_sds((m, n), out_dtype),
        scratch_shapes=[pltpu.VMEM((tm, tn), F32)] if nk > 1 else [],
        compiler_params=_cp("parallel", "parallel", "arbitrary"),
    )(*args)


def _mm_sum(pairs, add, *, name, add_scale=1.0, tm=640):
    m, n = add.shape
    tm = _tile(m, tm)
    npairs = len(pairs)

    def body(*refs):
        a_refs, b_refs = refs[:npairs], refs[npairs:2 * npairs]
        r_ref, o_ref = refs[2 * npairs], refs[2 * npairs + 1]
        acc = add_scale * r_ref[...]
        for a_ref, b_ref in zip(a_refs, b_refs):
            acc = acc + jnp.dot(a_ref[...].astype(BF16), b_ref[...].astype(BF16), preferred_element_type=F32)
        o_ref[...] = acc

    in_specs = ([pl.BlockSpec((tm, a.shape[1]), lambda i: (i, 0)) for a, _ in pairs]
                + [pl.BlockSpec(b.shape, lambda i: (0, 0)) for _, b in pairs]
                + [pl.BlockSpec((tm, n), lambda i: (i, 0))])
    return pl.pallas_call(
        body, name=name, grid=(m // tm,), in_specs=in_specs, out_specs=pl.BlockSpec((tm, n), lambda i: (i, 0)),
        out_shape=_sds((m, n), F32), compiler_params=_cp("parallel"),
    )(*[a for a, _ in pairs], *[b for _, b in pairs], add)


def _mm_tn(a, b, *, name, tko=1408, tn=1024, tt=640):
    t, k_dim = a.shape
    n = b.shape[1]
    tko, tn, tt = _tile(k_dim, tko), _tile(n, tn), _tile(t, tt)

    def body(a_ref, b_ref, o_ref):
        part = lax.dot_general(a_ref[...].astype(BF16), b_ref[...].astype(BF16), TN_DIMS, preferred_element_type=F32)
        tt_i = pl.program_id(2)

        @pl.when(tt_i == 0)
        def _():
            o_ref[...] = part

        @pl.when(tt_i > 0)
        def _():
            o_ref[...] += part

    return pl.pallas_call(
        body, name=name, grid=(k_dim // tko, n // tn, t // tt),
        in_specs=[pl.BlockSpec((tt, tko), lambda i, j, s: (s, i)), pl.BlockSpec((tt, tn), lambda i, j, s: (s, j))],
        out_specs=pl.BlockSpec((tko, tn), lambda i, j, s: (i, j)),
        out_shape=_sds((k_dim, n), F32),
        compiler_params=_cp("parallel", "parallel", "arbitrary"),
    )(a, b)


def _ln_fwd(h, branch, g, b, *, name):
    t, d = h.shape
    tr = _tile(t, 640)
    has_branch = branch is not None

    def body(*refs):
        if has_branch:
            h_ref, br_ref, g_ref, b_ref, pre_ref, o_ref = refs
            pre = ALPHA * h_ref[...] + br_ref[...]
            pre_ref[...] = pre
        else:
            h_ref, g_ref, b_ref, o_ref = refs
            pre = h_ref[...]
        mu = jnp.mean(pre, axis=1, keepdims=True)
        xc = pre - mu
        var = jnp.mean(xc * xc, axis=1, keepdims=True)
        y = xc * lax.rsqrt(var + LN_EPS) * g_ref[...] + b_ref[...]
        rows = _row_ids(pl.program_id(0), tr, (tr, 1))
        o_ref[...] = jnp.where(rows >= PAD, y, 0.0)

    row_spec = pl.BlockSpec((tr, d), lambda i: (i, 0))
    vec_spec = pl.BlockSpec((1, d), lambda i: (0, 0))
    if has_branch:
        return pl.pallas_call(
            body, name=name, grid=(t // tr,), in_specs=[row_spec, row_spec, vec_spec, vec_spec],
            out_specs=[row_spec, row_spec], out_shape=[_sds((t, d), F32), _sds((t, d), F32)],
            compiler_params=_cp("parallel"))(h, branch, g, b)
    out = pl.pallas_call(
        body, name=name, grid=(t // tr,), in_specs=[row_spec, vec_spec, vec_spec],
        out_specs=row_spec, out_shape=_sds((t, d), F32), compiler_params=_cp("parallel"))(h, g, b)
    return h, out


def _ln_bwd(dy, pre, g, *, name):
    t, d = pre.shape
    tr = _tile(t, 640)

    def body(dy_ref, pre_ref, g_ref, dpre_ref, dg_ref, db_ref):
        i = pl.program_id(0)
        pre_v = pre_ref[...]
        mu = jnp.mean(pre_v, axis=1, keepdims=True)
        xc = pre_v - mu
        var = jnp.mean(xc * xc, axis=1, keepdims=True)
        rstd = lax.rsqrt(var + LN_EPS)
        xhat = xc * rstd
        rows = _row_ids(i, tr, (tr, 1))
        dym = jnp.where(rows >= PAD, dy_ref[...], 0.0)
        gdy = dym * g_ref[...]
        m1 = jnp.mean(gdy, axis=1, keepdims=True)
        m2 = jnp.mean(gdy * xhat, axis=1, keepdims=True)
        dpre_ref[...] = rstd * (gdy - m1 - xhat * m2)
        pg = jnp.sum(dym * xhat, axis=0, keepdims=True)
        pb = jnp.sum(dym, axis=0, keepdims=True)

        @pl.when(i == 0)
        def _():
            dg_ref[...] = pg
            db_ref[...] = pb

        @pl.when(i > 0)
        def _():
            dg_ref[...] += pg
            db_ref[...] += pb

    row_spec = pl.BlockSpec((tr, d), lambda i: (i, 0))
    vec_spec = pl.BlockSpec((1, d), lambda i: (0, 0))
    return pl.pallas_call(
        body, name=name, grid=(t // tr,), in_specs=[row_spec, row_spec, vec_spec],
        out_specs=[row_spec, vec_spec, vec_spec],
        out_shape=[_sds((t, d), F32), _sds((1, d), F32), _sds((1, d), F32)],
        compiler_params=_cp("arbitrary"))(dy, pre, g)


def _rms_fwd(proj, col_off, width, g, *, name):
    t = proj.shape[0]
    tr = _tile(t, 640)
    cb = col_off // width

    def body(x_ref, g_ref, o_ref):
        x = x_ref[...]
        r = lax.rsqrt(jnp.mean(x * x, axis=1, keepdims=True) + RMS_EPS)
        o_ref[...] = (x * r * g_ref[...]).astype(BF16)

    return pl.pallas_call(
        body, name=name, grid=(t // tr,),
        in_specs=[pl.BlockSpec((tr, width), lambda i: (i, cb)), pl.BlockSpec((1, width), lambda i: (0, 0))],
        out_specs=pl.BlockSpec((tr, width), lambda i: (i, 0)), out_shape=_sds((t, width), BF16),
        compiler_params=_cp("parallel"))(proj, g)


def _rms_bwd(dy, proj, col_off, width, g, *, name):
    t = proj.shape[0]
    tr = _tile(t, 640)
    cb = col_off // width

    def body(dy_ref, x_ref, g_ref, dx_ref, dg_ref):
        i = pl.program_id(0)
        x = x_ref[...]
        dyv = dy_ref[...].astype(F32)
        r = lax.rsqrt(jnp.mean(x * x, axis=1, keepdims=True) + RMS_EPS)
        gdy = dyv * g_ref[...]
        m = jnp.mean(x * gdy, axis=1, keepdims=True)
        dx_ref[...] = (r * gdy - x * (r * r * r) * m).astype(BF16)
        pg = jnp.sum(dyv * x * r, axis=0, keepdims=True)

        @pl.when(i == 0)
        def _():
            dg_ref[...] = pg

        @pl.when(i > 0)
        def _():
            dg_ref[...] += pg

    return pl.pallas_call(
        body, name=name, grid=(t // tr,),
        in_specs=[pl.BlockSpec((tr, width), lambda i: (i, 0)), pl.BlockSpec((tr, width), lambda i: (i, cb)),
                  pl.BlockSpec((1, width), lambda i: (0, 0))],
        out_specs=[pl.BlockSpec((tr, width), lambda i: (i, 0)), pl.BlockSpec((1, width), lambda i: (0, 0))],
        out_shape=[_sds((t, width), BF16), _sds((1, width), F32)],
        compiler_params=_cp("arbitrary"))(dy, proj, g)


def _rope_apply(r, cos, sin_a, sin_b):
    return r * cos + pltpu.roll(r, 96, 1) * sin_a + pltpu.roll(r, 32, 1) * sin_b


def _rope_apply_t(dr, cos, sin_a, sin_b):
    return dr * cos + pltpu.roll(dr * sin_a, 32, 1) + pltpu.roll(dr * sin_b, 96, 1)


def _rope_q_fwd(q, cos, sin_a, sin_b, *, name):
    t, w = q.shape
    tr = _tile(t, 128)

    def body(q_ref, c_ref, sa_ref, sb_ref, o_ref):
        c, sa, sb = c_ref[...], sa_ref[...], sb_ref[...]
        flag = lax.broadcasted_iota(jnp.int32, (tr, LANES), 1) == QK_ROPE
        for h in range(MLA_HEADS):
            base = h * QHEAD
            o_ref[:, base:base + LANES] = (q_ref[:, base:base + LANES] * Q_SCALE).astype(BF16)
            rot = _rope_apply(q_ref[:, base + LANES:base + QHEAD], c, sa, sb)
            o_ref[:, base + LANES:base + QHEAD] = jnp.where(flag, 1.0, rot * Q_SCALE).astype(BF16)

    tab = pl.BlockSpec((tr, LANES), lambda i: (i, 0))
    row = pl.BlockSpec((tr, w), lambda i: (i, 0))
    return pl.pallas_call(body, name=name, grid=(t // tr,), in_specs=[row, tab, tab, tab], out_specs=row,
                          out_shape=_sds((t, w), BF16), compiler_params=_cp("parallel"))(q, cos, sin_a, sin_b)


def _rope_q_bwd(dq, cos, sin_a, sin_b, *, name):
    t, w = dq.shape
    tr = _tile(t, 128)

    def body(dq_ref, c_ref, sa_ref, sb_ref, o_ref):
        c, sa, sb = c_ref[...], sa_ref[...], sb_ref[...]
        for h in range(MLA_HEADS):
            base = h * QHEAD
            o_ref[:, base:base + LANES] = (dq_ref[:, base:base + LANES] * ATTN_SCALE).astype(BF16)
            d_rot = _rope_apply_t(dq_ref[:, base + LANES:base + QHEAD], c, sa, sb)
            o_ref[:, base + LANES:base + QHEAD] = (d_rot * ATTN_SCALE).astype(BF16)

    tab = pl.BlockSpec((tr, LANES), lambda i: (i, 0))
    row = pl.BlockSpec((tr, w), lambda i: (i, 0))
    return pl.pallas_call(body, name=name, grid=(t // tr,), in_specs=[row, tab, tab, tab], out_specs=row,
                          out_shape=_sds((t, w), BF16), compiler_params=_cp("parallel"))(dq, cos, sin_a, sin_b)


def _rope_k_fwd(proj, cos, sin_a, sin_b, *, name):
    t = proj.shape[0]
    tr = _tile(t, 640)
    cb = OFF_KPE // LANES

    def body(x_ref, c_ref, sa_ref, sb_ref, o_ref):
        rot = _rope_apply(x_ref[...], c_ref[...], sa_ref[...], sb_ref[...])
        rows = _row_ids(pl.program_id(0), tr, (tr, LANES))
        lane = lax.broadcasted_iota(jnp.int32, (tr, LANES), 1)
        o_ref[...] = jnp.where((lane == QK_ROPE) & (rows < PAD), PAD_KEY_SCORE, rot).astype(BF16)

    tab = pl.BlockSpec((tr, LANES), lambda i: (i, 0))
    return pl.pallas_call(body, name=name, grid=(t // tr,),
                          in_specs=[pl.BlockSpec((tr, LANES), lambda i: (i, cb)), tab, tab, tab], out_specs=tab,
                          out_shape=_sds((t, LANES), BF16), compiler_params=_cp("parallel"))(proj, cos, sin_a, sin_b)


def _rope_k_bwd(dkp, cos, sin_a, sin_b, *, name):
    nh, t, _ = dkp.shape
    tr = _tile(t, 640)

    def body(d_ref, c_ref, sa_ref, sb_ref, o_ref):
        tot = d_ref[0]
        for h in range(1, nh):
            tot = tot + d_ref[h]
        o_ref[...] = _rope_apply_t(tot, c_ref[...], sa_ref[...], sb_ref[...]).astype(BF16)

    tab = pl.BlockSpec((tr, LANES), lambda i: (i, 0))
    return pl.pallas_call(body, name=name, grid=(t // tr,),
                          in_specs=[pl.BlockSpec((nh, tr, LANES), lambda i: (0, i, 0)), tab, tab, tab], out_specs=tab,
                          out_shape=_sds((t, LANES), BF16), compiler_params=_cp("parallel"))(dkp, cos, sin_a, sin_b)


def _causal(tb, keys_first=False):
    a = lax.broadcasted_iota(jnp.int32, (tb, tb), 0)
    b = lax.broadcasted_iota(jnp.int32, (tb, tb), 1)
    return a <= b if keys_first else b <= a


def _flash_fwd(q, kv, kpe, *, name):
    t = q.shape[0]
    nh = MLA_HEADS
    tb = _tile(t, 640)
    nb = t // tb

    def body(q_ref, kn_ref, v_ref, kp_ref, o_ref, lse_ref):
        i = pl.program_id(1)
        qv = q_ref[...]

        def scores(j):
            r0 = pl.multiple_of(j * tb, tb)
            k = jnp.concatenate([kn_ref[pl.ds(r0, tb), :], kp_ref[pl.ds(r0, tb), :]], axis=1)
            return lax.dot_general(qv, k, NT_DIMS, preferred_element_type=F32)

        def update(s, j, state):
            m_prev, l_prev, acc = state
            m_new = jnp.maximum(m_prev, jnp.max(s, axis=1, keepdims=True))
            p = jnp.exp2(s - m_new)
            corr = jnp.exp2(m_prev - m_new)
            r0 = pl.multiple_of(j * tb, tb)
            pv = jnp.dot(p.astype(BF16), v_ref[pl.ds(r0, tb), :], preferred_element_type=F32)
            return m_new, corr * l_prev + jnp.sum(p, axis=1, keepdims=True), corr * acc + pv

        def loop(j, carry):
            s_cur, st = carry
            s_next = scores(j + 1)
            return s_next, update(s_cur, j, st)

        state = (jnp.full((tb, 1), NEG_INF, F32), jnp.zeros((tb, 1), F32), jnp.zeros((tb, V_HEAD), F32))
        s_diag, state = lax.fori_loop(0, i, loop, (scores(0), state))
        m, l, acc = update(jnp.where(_causal(tb), s_diag, NEG_INF), i, state)
        o_ref[...] = (acc / l).astype(BF16)
        lse_ref[0] = m + jnp.log2(l)

    return pl.pallas_call(
        body, name=name, grid=(nh, nb),
        in_specs=[pl.BlockSpec((tb, QHEAD), lambda h, i: (i, h)),
                  pl.BlockSpec((t, LANES), lambda h, i: (0, h)),
                  pl.BlockSpec((t, LANES), lambda h, i: (0, nh + h)),
                  pl.BlockSpec((t, LANES), lambda h, i: (0, 0))],
        out_specs=[pl.BlockSpec((tb, V_HEAD), lambda h, i: (i, h)),
                   pl.BlockSpec((1, tb, 1), lambda h, i: (h, i, 0))],
        out_shape=[_sds((t, nh * V_HEAD), BF16), _sds((nh, t, 1), F32)],
        compiler_params=_cp("parallel", "parallel"))(q, kv, kv, kpe)


def _attn_delta(do, o, *, name):
    t = o.shape[0]
    nh = MLA_HEADS
    tr = _tile(t, 640)

    def body(do_ref, o_ref, d_ref):
        d_ref[0] = jnp.sum(do_ref[...].astype(F32) * o_ref[...].astype(F32), axis=1, keepdims=True)

    blk = pl.BlockSpec((tr, V_HEAD), lambda h, i: (i, h))
    return pl.pallas_call(body, name=name, grid=(nh, t // tr), in_specs=[blk, blk],
                          out_specs=pl.BlockSpec((1, tr, 1), lambda h, i: (h, i, 0)),
                          out_shape=_sds((nh, t, 1), F32), compiler_params=_cp("parallel", "parallel"))(do, o)


def _flash_bwd(q, kv, kpe, do, lse, delta, *, name, exchange=None):
    t = q.shape[0]
    nh = MLA_HEADS
    tb = lse.shape[2]
    nb = t // tb
    fused = exchange is not None

    def body(*refs):
        q_ref, do_ref, lse_ref, dl_ref, kn_ref, v_ref, kp_ref = refs[:7]
        dq_ref, dkn_ref, dkp_ref, dv_ref = refs[7 + fused:11 + fused]
        j = pl.program_id(1)

        if fused:
            copies = functools.partial(_chip_exchange_copies, refs[7], refs[11 + fused], *refs[12 + fused:])
            first = (pl.program_id(0) == 0) & (j == 0)
            last = (pl.program_id(0) == nh - 1) & (j == nb - 1)

            @pl.when(first)
            def _():
                _chip_exchange_start(copies())

        @pl.when(j == 0)
        def _():
            dq_ref[...] = jnp.zeros((t, QHEAD), F32)

        k = jnp.concatenate([kn_ref[...], kp_ref[...]], axis=1)
        v = v_ref[...]

        def tile(i, carry, masked):
            dk, dv = carry
            r0 = pl.multiple_of(i * tb, tb)
            qv = q_ref[pl.ds(r0, tb), :]
            dov = do_ref[pl.ds(r0, tb), :]
            st = lax.dot_general(k, qv, NT_DIMS, preferred_element_type=F32)
            if masked:
                st = jnp.where(_causal(tb, keys_first=True), st, NEG_INF)
            pt = jnp.exp2(st - lse_ref[0, pl.ds(i, 1), :])
            dpt = lax.dot_general(v, dov, NT_DIMS, preferred_element_type=F32)
            dst = (pt * (dpt - dl_ref[0, pl.ds(i, 1), :])).astype(BF16)
            dv = dv + jnp.dot(pt.astype(BF16), dov, preferred_element_type=F32)
            dk = dk + jnp.dot(dst, qv, preferred_element_type=F32)
            dq_ref[pl.ds(r0, tb), :] += lax.dot_general(dst, k, TN_DIMS, preferred_element_type=F32)
            return dk, dv

        carry = tile(j, (jnp.zeros((tb, QHEAD), F32), jnp.zeros((tb, V_HEAD), F32)), True)
        dk, dv = lax.fori_loop(j + 1, nb, lambda i, c: tile(i, c, False), carry)
        dkn_ref[...] = (dk[:, :LANES] * LN2).astype(BF16)
        dkp_ref[0] = dk[:, LANES:] * LN2
        dv_ref[...] = dv.astype(BF16)

        if fused:
            @pl.when(last)
            def _():
                _chip_exchange_wait(copies())

    stat = pl.BlockSpec((1, nb, tb), lambda h, j: (h, 0, 0))
    in_specs = [pl.BlockSpec((t, QHEAD), lambda h, j: (0, h)),
                pl.BlockSpec((t, V_HEAD), lambda h, j: (0, h)),
                stat, stat,
                pl.BlockSpec((tb, LANES), lambda h, j: (j, h)),
                pl.BlockSpec((tb, LANES), lambda h, j: (j, nh + h)),
                pl.BlockSpec((tb, LANES), lambda h, j: (j, 0))]
    out_specs = [pl.BlockSpec((t, QHEAD), lambda h, j: (0, h)),
                 pl.BlockSpec((tb, LANES), lambda h, j: (j, h)),
                 pl.BlockSpec((1, tb, LANES), lambda h, j: (h, j, 0)),
                 pl.BlockSpec((tb, V_HEAD), lambda h, j: (j, h))]
    out_shape = [_sds((t, nh * QHEAD), F32), _sds((t, nh * LANES), BF16), _sds((nh, t, LANES), F32),
                 _sds((t, nh * V_HEAD), BF16)]
    args = [q, do, lse, delta, kv, kv, kpe]
    scratch = []
    if fused:
        in_specs.append(_ANY)
        out_specs.append(_ANY)
        out_shape.append(_sds(exchange.shape, exchange.dtype))
        args.append(exchange)
        scratch = _CHIP_EXCHANGE_SEMS
    return pl.pallas_call(body, name=name, grid=(nh, nb), in_specs=in_specs, out_specs=out_specs, out_shape=out_shape,
                          scratch_shapes=scratch, compiler_params=_cp("arbitrary", "arbitrary"))(*args)


def _fill_prev(buf, x_ref, halo_ref, i, tr):
    buf[pl.ds(0, HALO), :] = jnp.where(i > 0, halo_ref[...], 0.0)
    buf[pl.ds(HALO, tr), :] = x_ref[...]


def _conv_prev(buf, w_ref, kw, tr):
    acc = w_ref[kw - 1:kw, :] * buf[pl.ds(HALO, tr), :]
    for k in range(kw - 1):
        acc = acc + w_ref[k:k + 1, :] * buf[pl.ds(HALO - kw + 1 + k, tr), :]
    return acc


def _conv_dw(buf, dc, kw, tr):
    rows = [jnp.sum(dc * buf[pl.ds(HALO - kw + 1 + k, tr), :], axis=0, keepdims=True) for k in range(kw)]
    return jnp.concatenate(rows, axis=0)


def _conv_next(buf, dc_ref, halo_ref, w_ref, kw, i, n_tiles, tr):
    buf[pl.ds(0, tr), :] = dc_ref[...]
    buf[pl.ds(tr, HALO), :] = jnp.where(i < n_tiles - 1, halo_ref[...], 0.0)
    acc = w_ref[kw - 1:kw, :] * buf[pl.ds(0, tr), :]
    for k in range(kw - 1):
        acc = acc + w_ref[k:k + 1, :] * buf[pl.ds(kw - 1 - k, tr), :]
    return acc


def _split3(x):
    x1 = x.astype(BF16)
    r1 = x - x1.astype(F32)
    x2 = r1.astype(BF16)
    x3 = (r1 - x2.astype(F32)).astype(BF16)
    return x1, x2, x3


def _dot3(parts, m, left):
    tot = None
    for p in parts:
        r = jnp.dot(m, p, preferred_element_type=F32) if left else jnp.dot(p, m, preferred_element_type=F32)
        tot = r if tot is None else tot + r
    return tot


def _ssd_prep_fwd(proj, conv_w, conv_b, dt_bias, expand, *, name):
    t = proj.shape[0]
    tr = _tile(t, 128)
    nt = t // tr
    hb = tr // HALO
    cw = SSD_CONV_DIM
    cb_x = OFF_XBC // cw
    cb_dt = OFF_DT // LANES

    def body(x_ref, halo_ref, dtr_ref, w_ref, b_ref, dtb_ref, e_ref, xs_ref, bm_ref, cm_ref, dtx_ref, buf):
        i = pl.program_id(0)
        _fill_prev(buf, x_ref, halo_ref, i, tr)
        conv = _conv_prev(buf, w_ref, SSD_CONV, tr) + b_ref[...]
        rows = _row_ids(i, tr, (tr, 1))
        live = rows >= PAD
        act = jnp.where(live, conv * _sigmoid(conv), 0.0)
        xs_ref[...] = act[:, :SSD_INNER]
        bm_ref[...] = act[:, SSD_INNER:SSD_INNER + GROUP_W]
        cm_ref[...] = act[:, SSD_INNER + GROUP_W:]
        dt = jnp.where(live, jax.nn.softplus(dtr_ref[...] + dtb_ref[...]), 0.0)
        dtx_ref[...] = _dot3(_split3(dt), e_ref[...], left=False)

    return pl.pallas_call(
        body, name=name, grid=(nt,),
        in_specs=[pl.BlockSpec((tr, cw), lambda i: (i, cb_x)),
                  pl.BlockSpec((HALO, cw), lambda i: (jnp.maximum(i * hb - 1, 0), cb_x)),
                  pl.BlockSpec((tr, LANES), lambda i: (i, cb_dt)),
                  pl.BlockSpec((SSD_CONV, cw), lambda i: (0, 0)),
                  pl.BlockSpec((1, cw), lambda i: (0, 0)),
                  pl.BlockSpec((1, LANES), lambda i: (0, 0)),
                  pl.BlockSpec((LANES, SSD_INNER), lambda i: (0, 0))],
        out_specs=[pl.BlockSpec((tr, SSD_INNER), lambda i: (i, 0)), pl.BlockSpec((tr, GROUP_W), lambda i: (i, 0)),
                   pl.BlockSpec((tr, GROUP_W), lambda i: (i, 0)), pl.BlockSpec((tr, SSD_INNER), lambda i: (i, 0))],
        out_shape=[_sds((t, SSD_INNER), F32), _sds((t, GROUP_W), F32), _sds((t, GROUP_W), F32),
                   _sds((t, SSD_INNER), F32)],
        scratch_shapes=[pltpu.VMEM((tr + HALO, cw), F32)],
        compiler_params=_cp("parallel"))(proj, proj, proj, conv_w, conv_b, dt_bias, expand)


def _ssd_prep_bwd_a(proj, dxs, dbm, dcm, ddtx, conv_w, conv_b, dt_bias, reduce_m, *, name):
    t = proj.shape[0]
    tr = _tile(t, 128)
    nt = t // tr
    hb = tr // HALO
    cw = SSD_CONV_DIM
    cb_x = OFF_XBC // cw
    cb_dt = OFF_DT // LANES

    def body(x_ref, halo_ref, dtr_ref, dxs_ref, dbm_ref, dcm_ref, ddtx_ref, w_ref, b_ref, dtb_ref, r_ref,
             dconv_ref, ddtr_ref, dw_ref, db_ref, ddtb_ref, buf):
        i = pl.program_id(0)
        _fill_prev(buf, x_ref, halo_ref, i, tr)
        conv = _conv_prev(buf, w_ref, SSD_CONV, tr) + b_ref[...]
        rows = _row_ids(i, tr, (tr, 1))
        live = rows >= PAD
        sg = _sigmoid(conv)
        dact = jnp.concatenate([dxs_ref[...], dbm_ref[...], dcm_ref[...]], axis=1)
        dconv = jnp.where(live, dact * (sg * (1.0 + conv * (1.0 - sg))), 0.0)
        dconv_ref[...] = dconv
        pw = _conv_dw(buf, dconv, SSD_CONV, tr)
        pb = jnp.sum(dconv, axis=0, keepdims=True)
        ddt = _dot3(_split3(ddtx_ref[...]), r_ref[...], left=False)
        ddtr = jnp.where(live, ddt * _sigmoid(dtr_ref[...] + dtb_ref[...]), 0.0)
        ddtr_ref[...] = ddtr.astype(BF16)
        pdb = jnp.sum(ddtr, axis=0, keepdims=True)

        @pl.when(i == 0)
        def _():
            dw_ref[...] = pw
            db_ref[...] = pb
            ddtb_ref[...] = pdb

        @pl.when(i > 0)
        def _():
            dw_ref[...] += pw
            db_ref[...] += pb
            ddtb_ref[...] += pdb

    return pl.pallas_call(
        body, name=name, grid=(nt,),
        in_specs=[pl.BlockSpec((tr, cw), lambda i: (i, cb_x)),
                  pl.BlockSpec((HALO, cw), lambda i: (jnp.maximum(i * hb - 1, 0), cb_x)),
                  pl.BlockSpec((tr, LANES), lambda i: (i, cb_dt)),
                  pl.BlockSpec((tr, SSD_INNER), lambda i: (i, 0)),
                  pl.BlockSpec((tr, GROUP_W), lambda i: (i, 0)),
                  pl.BlockSpec((tr, GROUP_W), lambda i: (i, 0)),
                  pl.BlockSpec((tr, SSD_INNER), lambda i: (i, 0)),
                  pl.BlockSpec((SSD_CONV, cw), lambda i: (0, 0)),
                  pl.BlockSpec((1, cw), lambda i: (0, 0)),
                  pl.BlockSpec((1, LANES), lambda i: (0, 0)),
                  pl.BlockSpec((SSD_INNER, LANES), lambda i: (0, 0))],
        out_specs=[pl.BlockSpec((tr, cw), lambda i: (i, 0)), pl.BlockSpec((tr, LANES), lambda i: (i, 0)),
                   pl.BlockSpec((SSD_CONV, cw), lambda i: (0, 0)), pl.BlockSpec((1, cw), lambda i: (0, 0)),
                   pl.BlockSpec((1, LANES), lambda i: (0, 0))],
        out_shape=[_sds((t, cw), F32), _sds((t, LANES), BF16), _sds((SSD_CONV, cw), F32), _sds((1, cw), F32),
                   _sds((1, LANES), F32)],
        scratch_shapes=[pltpu.VMEM((tr + HALO, cw), F32)],
        compiler_params=_cp("arbitrary"))(proj, proj, proj, dxs, dbm, dcm, ddtx, conv_w, conv_b, dt_bias, reduce_m)


def _conv_bwd_input(dconv, w, kw, *, name, out_dtype=BF16, tc=None):
    t, c = dconv.shape
    tr = _tile(t, 128)
    nt = t // tr
    hb = tr // HALO
    tc = _tile(c, tc or c)
    last_hb = t // HALO - 1

    def body(dc_ref, halo_ref, w_ref, o_ref, buf):
        i = pl.program_id(0)
        o_ref[...] = _conv_next(buf, dc_ref, halo_ref, w_ref, kw, i, nt, tr).astype(out_dtype)

    return pl.pallas_call(
        body, name=name, grid=(nt, c // tc),
        in_specs=[pl.BlockSpec((tr, tc), lambda i, j: (i, j)),
                  pl.BlockSpec((HALO, tc), lambda i, j: (jnp.minimum((i + 1) * hb, last_hb), j)),
                  pl.BlockSpec((kw, tc), lambda i, j: (0, j))],
        out_specs=pl.BlockSpec((tr, tc), lambda i, j: (i, j)), out_shape=_sds((t, c), out_dtype),
        scratch_shapes=[pltpu.VMEM((tr + HALO, tc), F32)],
        compiler_params=_cp("parallel", "parallel"))(dconv, dconv, w)


def _ffn_act_fwd(ug, uv, wg, wv, bg, bv, *, name):
    t, c = ug.shape
    tr = _tile(t, 128)
    hb = tr // HALO
    tc = _tile(c, 1408)

    def body(ug_ref, hg_ref, uv_ref, hv_ref, wg_ref, wv_ref, bg_ref, bv_ref, o_ref, bufg, bufv):
        i = pl.program_id(0)
        _fill_prev(bufg, ug_ref, hg_ref, i, tr)
        _fill_prev(bufv, uv_ref, hv_ref, i, tr)
        cg = _conv_prev(bufg, wg_ref, FFN_CONV, tr) + bg_ref[...]
        cv = _conv_prev(bufv, wv_ref, FFN_CONV, tr) + bv_ref[...]
        o_ref[...] = (cg * _sigmoid(cg) * cv).astype(BF16)

    blk = pl.BlockSpec((tr, tc), lambda i, j: (i, j))
    halo = pl.BlockSpec((HALO, tc), lambda i, j: (jnp.maximum(i * hb - 1, 0), j))
    wsp = pl.BlockSpec((FFN_CONV, tc), lambda i, j: (0, j))
    bsp = pl.BlockSpec((1, tc), lambda i, j: (0, j))
    return pl.pallas_call(
        body, name=name, grid=(t // tr, c // tc), in_specs=[blk, halo, blk, halo, wsp, wsp, bsp, bsp],
        out_specs=blk, out_shape=_sds((t, c), BF16),
        scratch_shapes=[pltpu.VMEM((tr + HALO, tc), F32), pltpu.VMEM((tr + HALO, tc), F32)],
        compiler_params=_cp("parallel", "parallel"))(ug, ug, uv, uv, wg, wv, bg, bv)


def _ffn_act_bwd(ug, uv, dact, wg, wv, bg, bv, *, name):
    t, c = ug.shape
    tr = _tile(t, 128)
    hb = tr // HALO
    tc = _tile(c, 1408)

    def body(ug_ref, hg_ref, uv_ref, hv_ref, da_ref, wg_ref, wv_ref, bg_ref, bv_ref,
             dcg_ref, dcv_ref, dwg_ref, dwv_ref, dbg_ref, dbv_ref, bufg, bufv):
        i = pl.program_id(1)
        _fill_prev(bufg, ug_ref, hg_ref, i, tr)
        _fill_prev(bufv, uv_ref, hv_ref, i, tr)
        cg = _conv_prev(bufg, wg_ref, FFN_CONV, tr) + bg_ref[...]
        cv = _conv_prev(bufv, wv_ref, FFN_CONV, tr) + bv_ref[...]
        sg = _sigmoid(cg)
        da = da_ref[...]
        dcg = da * cv * (sg * (1.0 + cg * (1.0 - sg)))
        dcv = da * (cg * sg)
        dcg_ref[...] = dcg
        dcv_ref[...] = dcv
        pwg = _conv_dw(bufg, dcg, FFN_CONV, tr)
        pwv = _conv_dw(bufv, dcv, FFN_CONV, tr)
        pbg = jnp.sum(dcg, axis=0, keepdims=True)
        pbv = jnp.sum(dcv, axis=0, keepdims=True)

        @pl.when(i == 0)
        def _():
            dwg_ref[...] = pwg
            dwv_ref[...] = pwv
            dbg_ref[...] = pbg
            dbv_ref[...] = pbv

        @pl.when(i > 0)
        def _():
            dwg_ref[...] += pwg
            dwv_ref[...] += pwv
            dbg_ref[...] += pbg
            dbv_ref[...] += pbv

    blk = pl.BlockSpec((tr, tc), lambda j, i: (i, j))
    halo = pl.BlockSpec((HALO, tc), lambda j, i: (jnp.maximum(i * hb - 1, 0), j))
    wsp = pl.BlockSpec((FFN_CONV, tc), lambda j, i: (0, j))
    bsp = pl.BlockSpec((1, tc), lambda j, i: (0, j))
    return pl.pallas_call(
        body, name=name, grid=(c // tc, t // tr), in_specs=[blk, halo, blk, halo, blk, wsp, wsp, bsp, bsp],
        out_specs=[blk, blk, wsp, wsp, bsp, bsp],
        out_shape=[_sds((t, c), F32), _sds((t, c), F32), _sds((FFN_CONV, c), F32), _sds((FFN_CONV, c), F32),
                   _sds((1, c), F32), _sds((1, c), F32)],
        scratch_shapes=[pltpu.VMEM((tr + HALO, tc), F32), pltpu.VMEM((tr + HALO, tc), F32)],
        compiler_params=_cp("parallel", "arbitrary"))(ug, ug, uv, uv, dact, wg, wv, bg, bv)


def _tri(lower):
    li = lax.broadcasted_iota(jnp.int32, (CHUNK, CHUNK), 0)
    si = lax.broadcasted_iota(jnp.int32, (CHUNK, CHUNK), 1)
    return li >= si if lower else li <= si


def _tri_ones(lower):
    return jnp.where(_tri(lower), 1.0, 0.0).astype(BF16)


def _decay_pair(acs, acs_t, lane0):
    col = acs[:, lane0:lane0 + 1]
    row = acs_t[lane0:lane0 + 1, :]
    low = jnp.where(_tri(True), jnp.exp(jnp.minimum(col - row, 0.0)), 0.0)
    upp = jnp.where(_tri(False), jnp.exp(jnp.minimum(row - col, 0.0)), 0.0)
    return low, upp


def _ssd_fwd(xs, dtx, bm, cm, bm_t, a_x, d_x, *, name):
    t = xs.shape[0]
    nc = t // CHUNK
    gw = GROUP_W

    def body(xs_ref, dt_ref, b_ref, c_ref, bt_ref, a_ref, d_ref, y_ref, prev_ref, h_s):
        @pl.when(pl.program_id(1) == 0)
        def _():
            h_s[...] = jnp.zeros((SSD_STATE, gw), F32)

        x = xs_ref[...]
        dt = dt_ref[...]
        acs = _dot3(_split3(dt * a_ref[...]), _tri_ones(True), left=True)
        acs_t = acs.T
        xc = x * dt
        bv = b_ref[...].astype(BF16)
        cv = c_ref[...].astype(BF16)
        cb = lax.dot_general(cv, bv, NT_DIMS, preferred_element_type=F32)
        lane = lax.broadcasted_iota(jnp.int32, (CHUNK, LANES), 1)
        pieces = []
        for pp in range(gw // LANES):
            xcp = xc[:, pp * LANES:(pp + 1) * LANES]
            acc = jnp.zeros((CHUNK, LANES), F32)
            for e in range(2):
                low, _ = _decay_pair(acs, acs_t, pp * LANES + e * SSD_HEAD_DIM)
                mine = (lane >= e * SSD_HEAD_DIM) & (lane < (e + 1) * SSD_HEAD_DIM)
                xm = jnp.where(mine, xcp, 0.0).astype(BF16)
                acc = acc + jnp.dot((cb * low).astype(BF16), xm, preferred_element_type=F32)
            pieces.append(acc)
        y_diag = jnp.concatenate(pieces, axis=1)
        h_prev = h_s[...]
        y_off = jnp.dot(cv, h_prev.astype(BF16), preferred_element_type=F32) * jnp.exp(acs)
        y_ref[...] = y_diag + y_off + d_ref[...] * x
        prev_ref[0] = h_prev
        last = acs[CHUNK - 1:CHUNK, :]
        w = jnp.exp(last - acs)
        st = jnp.dot(bt_ref[...].astype(BF16), (xc * w).astype(BF16), preferred_element_type=F32)
        h_s[...] = h_prev * jnp.exp(last) + st

    tok = pl.BlockSpec((CHUNK, gw), lambda g, c: (c, g))
    grp = pl.BlockSpec((CHUNK, SSD_STATE), lambda g, c: (c, g))
    vec = pl.BlockSpec((1, gw), lambda g, c: (0, g))
    return pl.pallas_call(
        body, name=name, grid=(SSD_GROUPS, nc),
        in_specs=[tok, tok, grp, grp, pl.BlockSpec((SSD_STATE, CHUNK), lambda g, c: (g, c)), vec, vec],
        out_specs=[tok, pl.BlockSpec((1, SSD_STATE, gw), lambda g, c: (c, 0, g))],
        out_shape=[_sds((t, SSD_INNER), F32), _sds((nc, SSD_STATE, SSD_INNER), F32)],
        scratch_shapes=[pltpu.VMEM((SSD_STATE, gw), F32)],
        compiler_params=_cp("parallel", "arbitrary"))(xs, dtx, bm, cm, bm_t, a_x, d_x)


def _ssd_bwd(xs, dtx, bm, cm, cm_t, prev, dy, a_x, d_x, *, name):
    t = xs.shape[0]
    nc = t // CHUNK
    gw = GROUP_W

    def body(xs_ref, dt_ref, b_ref, c_ref, ct_ref, prev_ref, dy_ref, a_ref, d_ref,
             dxs_ref, ddt_ref, db_ref, dc_ref, da_ref, dd_ref, g_s):
        first = pl.program_id(1) == 0

        @pl.when(first)
        def _():
            g_s[...] = jnp.zeros((SSD_STATE, gw), F32)

        x = xs_ref[...]
        dt = dt_ref[...]
        a = a_ref[...]
        dyv = dy_ref[...]
        acs = _dot3(_split3(dt * a), _tri_ones(True), left=True)
        acs_t = acs.T
        xc = x * dt
        bv = b_ref[...].astype(BF16)
        cv = c_ref[...].astype(BF16)
        cb = lax.dot_general(cv, bv, NT_DIMS, preferred_element_type=F32)
        cb_t = lax.dot_general(bv, cv, NT_DIMS, preferred_element_type=F32)
        last = acs[CHUNK - 1:CHUNK, :]
        w = jnp.exp(last - acs)
        cd = jnp.exp(last)
        p_in = prev_ref[0]
        p_b = p_in.astype(BF16)
        g_out = g_s[...]
        g_b = g_out.astype(BF16)
        dy_e = dyv * jnp.exp(acs)
        dy_eb = dy_e.astype(BF16)
        y_off_raw = jnp.dot(cv, p_b, preferred_element_type=F32)
        dacs = dy_e * y_off_raw
        d_c = lax.dot_general(dy_eb, p_b, NT_DIMS, preferred_element_type=F32)
        d_prev = jnp.dot(ct_ref[...].astype(BF16), dy_eb, preferred_element_type=F32)
        q_l = jnp.dot(bv, g_b, preferred_element_type=F32)
        dxc = w * q_l
        tw = xc * q_l * w
        dacs = dacs - tw
        d_b = lax.dot_general((xc * w).astype(BF16), g_b, NT_DIMS, preferred_element_type=F32)
        last_add = jnp.sum(tw, axis=0, keepdims=True) + cd * jnp.sum(g_out * p_in, axis=0, keepdims=True)
        g_s[...] = cd * g_out + d_prev
        lane = lax.broadcasted_iota(jnp.int32, (CHUNK, LANES), 1)
        d_cb = jnp.zeros((CHUNK, CHUNK), F32)
        d_cb_t = jnp.zeros((CHUNK, CHUNK), F32)
        dxc_pieces, dacs_pieces = [], []
        for pp in range(gw // LANES):
            xcp = xc[:, pp * LANES:(pp + 1) * LANES]
            dyp = dyv[:, pp * LANES:(pp + 1) * LANES]
            dxcp = jnp.zeros((CHUNK, LANES), F32)
            dacsp = jnp.zeros((CHUNK, LANES), F32)
            for e in range(2):
                low, upp = _decay_pair(acs, acs_t, pp * LANES + e * SSD_HEAD_DIM)
                mine = (lane >= e * SSD_HEAD_DIM) & (lane < (e + 1) * SSD_HEAD_DIM)
                m_low = cb * low
                m_upp = cb_t * upp
                dym = jnp.where(mine, dyp, 0.0).astype(BF16)
                xm = jnp.where(mine, xcp, 0.0).astype(BF16)
                dxcp = dxcp + jnp.dot(m_upp.astype(BF16), dym, preferred_element_type=F32)
                d_m = lax.dot_general(dym, xm, NT_DIMS, preferred_element_type=F32)
                d_m_t = lax.dot_general(xm, dym, NT_DIMS, preferred_element_type=F32)
                rs = jnp.sum(d_m * m_low, axis=1, keepdims=True)
                cs = jnp.sum(d_m_t * m_upp, axis=1, keepdims=True)
                dacsp = dacsp + jnp.where(lane == e * SSD_HEAD_DIM, rs - cs, 0.0)
                d_cb = d_cb + d_m * low
                d_cb_t = d_cb_t + d_m_t * upp
            dxc_pieces.append(dxcp)
            dacs_pieces.append(dacsp)
        dxc = dxc + jnp.concatenate(dxc_pieces, axis=1)
        dacs = dacs + jnp.concatenate(dacs_pieces, axis=1)
        rowi = lax.broadcasted_iota(jnp.int32, (CHUNK, gw), 0)
        dacs = dacs + jnp.where(rowi == CHUNK - 1, last_add, 0.0)
        dc_ref[...] = d_c + jnp.dot(d_cb.astype(BF16), bv, preferred_element_type=F32)
        db_ref[...] = d_b + jnp.dot(d_cb_t.astype(BF16), cv, preferred_element_type=F32)
        dda = _dot3(_split3(dacs), _tri_ones(False), left=True)
        ddt_ref[...] = dda * a + dxc * x
        dxs_ref[...] = dxc * dt + d_ref[...] * dyv
        pa = jnp.sum(dda * dt, axis=0, keepdims=True)
        pd = jnp.sum(dyv * x, axis=0, keepdims=True)

        @pl.when(first)
        def _():
            da_ref[...] = pa
            dd_ref[...] = pd

        @pl.when(jnp.logical_not(first))
        def _():
            da_ref[...] += pa
            dd_ref[...] += pd

    rc = lambda c: nc - 1 - c
    tok = pl.BlockSpec((CHUNK, gw), lambda g, c: (rc(c), g))
    grp = pl.BlockSpec((CHUNK, SSD_STATE), lambda g, c: (rc(c), g))
    vec = pl.BlockSpec((1, gw), lambda g, c: (0, g))
    return pl.pallas_call(
        body, name=name, grid=(SSD_GROUPS, nc),
        in_specs=[tok, tok, grp, grp, pl.BlockSpec((SSD_STATE, CHUNK), lambda g, c: (g, rc(c))),
                  pl.BlockSpec((1, SSD_STATE, gw), lambda g, c: (rc(c), 0, g)), tok, vec, vec],
        out_specs=[tok, tok, grp, grp, vec, vec],
        out_shape=[_sds((t, SSD_INNER), F32), _sds((t, SSD_INNER), F32), _sds((t, gw), F32), _sds((t, gw), F32),
                   _sds((1, SSD_INNER), F32), _sds((1, SSD_INNER), F32)],
        scratch_shapes=[pltpu.VMEM((SSD_STATE, gw), F32)],
        compiler_params=_cp("parallel", "arbitrary"))(xs, dtx, bm, cm, cm_t, prev, dy, a_x, d_x)


def _gnorm_fwd(y, proj, g, *, name):
    t = y.shape[0]
    tr = _tile(t, 640)
    zb = OFF_Z // GROUP_W

    def body(y_ref, z_ref, g_ref, o_ref):
        z = z_ref[...]
        v = y_ref[...] * (z * _sigmoid(z))
        r = lax.rsqrt(jnp.mean(v * v, axis=1, keepdims=True) + RMS_EPS)
        o_ref[...] = (v * r * g_ref[...]).astype(BF16)

    blk = pl.BlockSpec((tr, GROUP_W), lambda i, j: (i, j))
    return pl.pallas_call(
        body, name=name, grid=(t // tr, SSD_GROUPS),
        in_specs=[blk, pl.BlockSpec((tr, GROUP_W), lambda i, j: (i, zb + j)),
                  pl.BlockSpec((1, GROUP_W), lambda i, j: (0, j))],
        out_specs=blk, out_shape=_sds((t, SSD_INNER), BF16),
        compiler_params=_cp("parallel", "parallel"))(y, proj, g)


def _gnorm_bwd(dout, y, proj, g, *, name):
    t = y.shape[0]
    tr = _tile(t, 640)
    zb = OFF_Z // GROUP_W

    def body(do_ref, y_ref, z_ref, g_ref, dy_ref, dz_ref, dg_ref):
        i = pl.program_id(1)
        z = z_ref[...]
        yv = y_ref[...]
        sg = _sigmoid(z)
        sz = z * sg
        v = yv * sz
        r = lax.rsqrt(jnp.mean(v * v, axis=1, keepdims=True) + RMS_EPS)
        dov = do_ref[...].astype(F32)
        gdo = dov * g_ref[...]
        m = jnp.mean(v * gdo, axis=1, keepdims=True)
        dv = r * gdo - v * (r * r * r) * m
        dy_ref[...] = dv * sz
        dz_ref[...] = (dv * yv * (sg * (1.0 + z * (1.0 - sg)))).astype(BF16)
        pg = jnp.sum(dov * v * r, axis=0, keepdims=True)

        @pl.when(i == 0)
        def _():
            dg_ref[...] = pg

        @pl.when(i > 0)
        def _():
            dg_ref[...] += pg

    blk = pl.BlockSpec((tr, GROUP_W), lambda j, i: (i, j))
    vec = pl.BlockSpec((1, GROUP_W), lambda j, i: (0, j))
    return pl.pallas_call(
        body, name=name, grid=(SSD_GROUPS, t // tr),
        in_specs=[blk, blk, pl.BlockSpec((tr, GROUP_W), lambda j, i: (i, zb + j)), vec],
        out_specs=[blk, blk, vec],
        out_shape=[_sds((t, SSD_INNER), F32), _sds((t, SSD_INNER), BF16), _sds((1, SSD_INNER), F32)],
        compiler_params=_cp("parallel", "arbitrary"))(dout, y, proj, g)


def _mix_fwd(proj, ya, ys, *, name):
    t, d = ya.shape
    tr = _tile(t, 640)
    ba, bs = OFF_GA // d, OFF_GS // d

    def body(ga_ref, gs_ref, ya_ref, ys_ref, o_ref):
        o_ref[...] = (_sigmoid(ga_ref[...]) * ya_ref[...] + _sigmoid(gs_ref[...]) * ys_ref[...]).astype(BF16)

    blk = pl.BlockSpec((tr, d), lambda i: (i, 0))
    return pl.pallas_call(
        body, name=name, grid=(t // tr,),
        in_specs=[pl.BlockSpec((tr, d), lambda i: (i, ba)), pl.BlockSpec((tr, d), lambda i: (i, bs)), blk, blk],
        out_specs=blk, out_shape=_sds((t, d), BF16), compiler_params=_cp("parallel"))(proj, proj, ya, ys)


def _mix_bwd(dmix, proj, ya, ys, *, name):
    t, d = ya.shape
    tr = _tile(t, 640)
    ba, bs = OFF_GA // d, OFF_GS // d

    def body(dm_ref, ga_ref, gs_ref, ya_ref, ys_ref, dya_ref, dys_ref, dga_ref, dgs_ref):
        dm = dm_ref[...]
        sa = _sigmoid(ga_ref[...])
        ss = _sigmoid(gs_ref[...])
        dya_ref[...] = (sa * dm).astype(BF16)
        dys_ref[...] = (ss * dm).astype(BF16)
        dga_ref[...] = (dm * ya_ref[...] * sa * (1.0 - sa)).astype(BF16)
        dgs_ref[...] = (dm * ys_ref[...] * ss * (1.0 - ss)).astype(BF16)

    blk = pl.BlockSpec((tr, d), lambda i: (i, 0))
    return pl.pallas_call(
        body, name=name, grid=(t // tr,),
        in_specs=[blk, pl.BlockSpec((tr, d), lambda i: (i, ba)), pl.BlockSpec((tr, d), lambda i: (i, bs)), blk, blk],
        out_specs=[blk] * 4, out_shape=[_sds((t, d), BF16)] * 4,
        compiler_params=_cp("parallel"))(dmix, proj, proj, ya, ys)


def _loss_grad(h, target, *, name):
    t, d = h.shape
    tr = LANES
    assert ROW0 == tr

    def body(h_ref, t_ref, dh_ref, loss_ref):
        i = pl.program_id(0)

        @pl.when(i == 0)
        def _():
            dh_ref[...] = jnp.zeros((tr, d), F32)
            loss_ref[...] = jnp.zeros((1, LANES), F32)

        @pl.when(i > 0)
        def _():
            err = h_ref[...] - t_ref[...]
            dh_ref[...] = err * (1.0 / d)
            part = jnp.sum(jnp.sum(err * err, axis=1, keepdims=True), axis=0, keepdims=True)
            loss_ref[...] += jnp.broadcast_to(part * (0.5 / d), (1, LANES))

    blk = pl.BlockSpec((tr, d), lambda i: (i, 0))
    return pl.pallas_call(
        body, name=name, grid=(t // tr,),
        in_specs=[blk, pl.BlockSpec((tr, d), lambda i: (jnp.maximum(i - 1, 0), 0))],
        out_specs=[blk, pl.BlockSpec((1, LANES), lambda i: (0, 0))],
        out_shape=[_sds((t, d), F32), _sds((1, LANES), F32)],
        compiler_params=_cp("arbitrary"))(h, target)


def _adamw_update(gv, wv, mv, vv):
    c1 = 1.0 - ADAM_B1 ** ADAM_STEP
    c2 = 1.0 - ADAM_B2 ** ADAM_STEP
    nm = ADAM_B1 * mv + (1.0 - ADAM_B1) * gv
    nv = ADAM_B2 * vv + (1.0 - ADAM_B2) * (gv * gv)
    return -ADAM_LR * ((nm / c1) / (jnp.sqrt(nv / c2) + ADAM_EPS) + ADAM_WD * wv), nm, nv


def _as_2d(a):
    return a.reshape(1, -1) if a.ndim == 1 else a.reshape(-1, a.shape[-1])


def _adamw(g, w, m, v, *, name):
    shape = w.shape
    g2, w2, m2, v2 = (_as_2d(a) for a in (g, w, m, v))
    r, c = w2.shape
    tr = _tile(r, 256, base=8)

    def body(g_ref, w_ref, m_ref, v_ref, d_ref, nm_ref, nv_ref):
        d_ref[...], nm_ref[...], nv_ref[...] = _adamw_update(g_ref[...], w_ref[...], m_ref[...], v_ref[...])

    blk = pl.BlockSpec((tr, c), lambda i: (i, 0))
    outs = pl.pallas_call(body, name=name, grid=(r // tr,), in_specs=[blk] * 4, out_specs=[blk] * 3,
                          out_shape=[_sds((r, c), F32)] * 3, compiler_params=_cp("parallel"))(g2, w2, m2, v2)
    return [o.reshape(shape) for o in outs]


def _adamw_small(items, *, name):
    n = len(items)
    shapes = [it[1].shape for it in items]
    flat = [_as_2d(a) for it in items for a in it]

    def body(*refs):
        ins, outs = refs[:4 * n], refs[4 * n:]
        for k in range(n):
            g_ref, w_ref, m_ref, v_ref = ins[4 * k:4 * k + 4]
            d_ref, nm_ref, nv_ref = outs[3 * k:3 * k + 3]
            d_ref[...], nm_ref[...], nv_ref[...] = _adamw_update(g_ref[...], w_ref[...], m_ref[...], v_ref[...])

    out_shape = [_sds(flat[4 * k + 1].shape, F32) for k in range(n) for _ in range(3)]
    outs = pl.pallas_call(body, name=name, out_shape=out_shape,
                          compiler_params=pltpu.CompilerParams(vmem_limit_bytes=VMEM_LIMIT_BYTES))(*flat)
    return [[outs[3 * k + q].reshape(shapes[k]) for q in range(3)] for k in range(n)]


def _pair_add(g5, got, core, *, name):
    n, _, r, _ = g5.shape
    tr = _tile(r, 1024, base=8)

    def body(c_ref, a_ref, b_ref, o_ref):
        o_ref[...] = a_ref[0] + b_ref[...]

    grid_spec = pltpu.PrefetchScalarGridSpec(
        num_scalar_prefetch=1, grid=(n, r // tr),
        in_specs=[pl.BlockSpec((1, 1, tr, LANES), lambda s, i, c_ref: (s, c_ref[0], i, 0)),
                  pl.BlockSpec((1, tr, LANES), lambda s, i, c_ref: (s, i, 0))],
        out_specs=pl.BlockSpec((1, tr, LANES), lambda s, i, c_ref: (s, i, 0)))
    return pl.pallas_call(body, name=name, grid_spec=grid_spec, out_shape=_sds(got.shape, F32),
                          compiler_params=_cp("parallel", "parallel"))(core, g5, got)


def _sum_chips(q, core, *, name):
    n, r, _ = q.shape
    tr = _tile(r, 1024, base=8)

    def body(c_ref, q_ref, o_ref):
        tot = q_ref[0]
        for s in range(1, n):
            tot = tot + q_ref[s]
        o_ref[0] = tot

    grid_spec = pltpu.PrefetchScalarGridSpec(
        num_scalar_prefetch=1, grid=(r // tr,),
        in_specs=[pl.BlockSpec((n, tr, LANES), lambda i, c_ref: (0, i, 0))],
        out_specs=pl.BlockSpec((1, tr, LANES), lambda i, c_ref: (c_ref[0], i, 0)))
    return pl.pallas_call(body, name=name, grid_spec=grid_spec, out_shape=_sds((2, r, LANES), F32),
                          compiler_params=_cp("parallel"))(core, q)


_ANY = pl.BlockSpec(memory_space=pl.ANY)
_MESH = pl.DeviceIdType.MESH


def _place():
    x, y, c = lax.axis_index("x"), lax.axis_index("y"), lax.axis_index("c")
    return x, y, c, [(1 - x, y), (x, 1 - y), (1 - x, 1 - y)]


def _chip_allgather(mine, *, name):
    na = len(mine)

    def body(*refs):
        x_refs, o_refs = refs[:na], refs[na:2 * na]
        send_sems, recv_sems, local_sems = refs[2 * na:]
        x, y, c, chips = _place()
        k = 2 * x + y

        def copy(a, n, src, dst, to):
            return pltpu.make_async_remote_copy(src_ref=src, dst_ref=dst, send_sem=send_sems.at[6 * a + n],
                                                recv_sem=recv_sems.at[6 * a + n], device_id=to, device_id_type=_MESH)

        locals_ = [pltpu.make_async_copy(x_refs[a], o_refs[a].at[k], local_sems.at[a]) for a in range(na)]
        for cp in locals_:
            cp.start()
        sends = [copy(a, n, x_refs[a].at[c], o_refs[a].at[k, c], (cx, cy, c))
                 for a in range(na) for n, (cx, cy) in enumerate(chips)]
        for cp in sends:
            cp.start()
        passed = []
        for a in range(na):
            for n, (cx, cy) in enumerate(chips):
                slab = o_refs[a].at[2 * cx + cy, c]
                copy(a, n, slab, slab, (cx, cy, c)).wait_recv()
                fw = copy(a, 3 + n, slab, slab, (x, y, 1 - c))
                fw.start()
                passed.append(fw)
        for a in range(na):
            for n, (cx, cy) in enumerate(chips):
                slab = o_refs[a].at[2 * cx + cy, 1 - c]
                copy(a, 3 + n, slab, slab, (x, y, 1 - c)).wait_recv()
        for cp in sends + passed:
            cp.wait_send()
        for cp in locals_:
            cp.wait()

    return pl.pallas_call(
        body, name=name, in_specs=[_ANY] * na, out_specs=[_ANY] * na,
        out_shape=[_sds((N_CHIPS,) + a.shape, a.dtype) for a in mine],
        scratch_shapes=[pltpu.SemaphoreType.DMA((6 * na,)), pltpu.SemaphoreType.DMA((6 * na,)),
                        pltpu.SemaphoreType.DMA((na,))])(*mine)


def _sibling_swap(g5, *, name):
    n, _, r, _ = g5.shape

    def body(x_ref, o_ref, send_sems, recv_sems):
        x, y, c, _ = _place()
        cps = [pltpu.make_async_remote_copy(src_ref=x_ref.at[s, 1 - c], dst_ref=o_ref.at[s], send_sem=send_sems.at[s],
                                            recv_sem=recv_sems.at[s], device_id=(x, y, 1 - c), device_id_type=_MESH)
               for s in range(n)]
        for cp in cps:
            cp.start()
        for cp in cps:
            cp.wait()

    return pl.pallas_call(
        body, name=name, in_specs=[_ANY], out_specs=_ANY, out_shape=_sds((n, r, LANES), g5.dtype),
        scratch_shapes=[pltpu.SemaphoreType.DMA((n,)), pltpu.SemaphoreType.DMA((n,))])(g5)


_CHIP_EXCHANGE_SEMS = [pltpu.SemaphoreType.DMA((3,)), pltpu.SemaphoreType.DMA((3,)), pltpu.SemaphoreType.DMA]


def _chip_exchange_copies(h_ref, q_ref, send_sems, recv_sems, local_sem):
    x, y, c, chips = _place()
    k = 2 * x + y
    local = pltpu.make_async_copy(h_ref.at[k], q_ref.at[k], local_sem)
    sends, arrivals = [], []
    for n, (cx, cy) in enumerate(chips):
        kk = 2 * cx + cy
        mk = functools.partial(pltpu.make_async_remote_copy, src_ref=h_ref.at[kk], send_sem=send_sems.at[n],
                               recv_sem=recv_sems.at[n], device_id=(cx, cy, c), device_id_type=_MESH)
        sends.append(mk(dst_ref=q_ref.at[k]))
        arrivals.append(mk(dst_ref=q_ref.at[kk]))
    return local, sends, arrivals


def _chip_exchange_start(copies):
    local, sends, _ = copies
    local.start()
    for cp in sends:
        cp.start()


def _chip_exchange_wait(copies):
    local, sends, arrivals = copies
    for cp in arrivals:
        cp.wait_recv()
    for cp in sends:
        cp.wait_send()
    local.wait()


def _chip_exchange(h, *, name):
    def body(h_ref, q_ref, send_sems, recv_sems, local_sem):
        copies = _chip_exchange_copies(h_ref, q_ref, send_sems, recv_sems, local_sem)
        _chip_exchange_start(copies)
        _chip_exchange_wait(copies)

    return pl.pallas_call(body, name=name, in_specs=[_ANY], out_specs=_ANY, out_shape=_sds(h.shape, h.dtype),
                          scratch_shapes=_CHIP_EXCHANGE_SEMS)(h)


def _sibling_allgather(buf, *, name):
    def body(x_ref, o_ref, send_sem, recv_sem):
        x, y, c, _ = _place()
        cp = pltpu.make_async_remote_copy(src_ref=x_ref.at[c], dst_ref=o_ref.at[c], send_sem=send_sem,
                                          recv_sem=recv_sem, device_id=(x, y, 1 - c), device_id_type=_MESH)
        cp.start()
        pltpu.make_async_remote_copy(src_ref=x_ref.at[c], dst_ref=o_ref.at[1 - c], send_sem=send_sem,
                                     recv_sem=recv_sem, device_id=(x, y, 1 - c), device_id_type=_MESH).wait_recv()
        cp.wait_send()

    return pl.pallas_call(
        body, name=name, in_specs=[_ANY], out_specs=_ANY, out_shape=_sds(buf.shape, buf.dtype),
        input_output_aliases={0: 0},
        scratch_shapes=[pltpu.SemaphoreType.DMA, pltpu.SemaphoreType.DMA])(buf)


def _chip_segments(g, kind):
    if kind == "col":
        n = g.shape[-1] // N_CHIPS
        s = g.reshape(g.shape[:-1] + (N_CHIPS, n))
        return jnp.moveaxis(s, -2, 0).reshape(N_CHIPS, -1)
    k = g.shape[-2] // N_CHIPS
    s = g.reshape(g.shape[:-2] + (N_CHIPS, k, g.shape[-1]))
    return jnp.moveaxis(s, -3, 0).reshape(N_CHIPS, -1)


def _join(blocks, kind):
    if kind == "col":
        s = jnp.moveaxis(blocks, 0, -2)
        return s.reshape(s.shape[:-2] + (s.shape[-2] * s.shape[-1],))
    return blocks.reshape((blocks.shape[0] * blocks.shape[1],) + blocks.shape[2:])


def _gather_weights(local):
    mine = []
    for n, _ in SHARDED:
        a = local[n]
        if n == "w_in":
            a = jnp.swapaxes(a, 1, 2)
        if n == "meta_tokens":
            a = a.reshape(2, N_META // 2, a.shape[-1])
        mine.append(a.astype(BF16) if n in GATHER_BF16 else a)
    got = _chip_allgather(mine, name="gather_weights")
    return {n: g for (n, _), g in zip(SHARDED, got)}


def _rope_tables(t):
    half = QK_ROPE // 2
    inv_freq = 1.0 / (ROPE_THETA ** (jnp.arange(0, QK_ROPE, 2, dtype=F32) / QK_ROPE))
    pos = jnp.maximum(jnp.arange(t, dtype=F32) - PAD, 0.0)
    ang = pos[:, None] * inv_freq[None, :]
    cos, sin = jnp.cos(ang), jnp.sin(ang)
    z = jnp.zeros((t, half), F32)
    z2 = jnp.zeros((t, LANES - QK_ROPE), F32)
    return (jnp.concatenate([cos, cos, z2], axis=1), jnp.concatenate([-sin, z, z2], axis=1),
            jnp.concatenate([z, sin, z2], axis=1))


def _expand_matrix():
    lane = np.arange(SSD_INNER) // SSD_HEAD_DIM
    e = (np.arange(LANES)[:, None] == lane[None, :]).astype(np.float32)
    return jnp.asarray(e, BF16)


def _layer_weights(full, rep, i):
    w = {}
    kinds = dict(SHARDED)
    whole = lambda n: _join(full[n][:, i], kinds[n])
    wt = _join(full["w_in"][:, i], "row")
    zr = lambda n: jnp.zeros((n, D_MODEL), BF16)
    w["w_in_t"] = jnp.concatenate(
        [wt[0:1024], wt[1088:3136], wt[3136:6208], wt[6240:7264], wt[7264:8288],
         wt[1024:1088], zr(LANES - QK_ROPE), wt[6208:6240], zr(LANES - SSD_HEADS)], axis=0)
    wq = whole("w_q_b").reshape(Q_LORA, MLA_HEADS, QK_NOPE + QK_ROPE)
    w["w_q_b"] = jnp.pad(wq, ((0, 0), (0, 0), (0, QHEAD - QK_NOPE - QK_ROPE))).reshape(Q_LORA, MLA_HEADS * QHEAD)
    wkv = whole("w_kv_b").reshape(KV_LORA, MLA_HEADS, 2, QK_NOPE)
    w["w_kv_b"] = jnp.swapaxes(wkv, 1, 2).reshape(KV_LORA, 2 * MLA_HEADS * QK_NOPE)
    for n in ("w_o_attn", "w_o_ssd", "w_out", "w_down", "ssd_conv_w"):
        w[n] = whole(n)
    w_up = whole("w_up")
    w["w_up_g"] = w_up[:, :D_FF]
    w["w_up_v"] = w_up[:, D_FF:]
    ffn_w = whole("ffn_conv_w")
    w["ffn_conv_wg"] = ffn_w[:, :D_FF]
    w["ffn_conv_wv"] = ffn_w[:, D_FF:]
    row = lambda v: v.reshape(1, -1)
    w["q_norm_g"] = row(rep["q_norm_g"][i])
    w["kv_norm_g"] = row(rep["kv_norm_g"][i])
    w["ssd_conv_b"] = row(rep["ssd_conv_b"][i])
    w["dt_bias"] = row(jnp.pad(rep["dt_bias"][i], (0, LANES - SSD_HEADS)))
    a = -jnp.exp(rep["a_log"][i])
    w["a"] = a
    w["a_x"] = row(jnp.repeat(a, SSD_HEAD_DIM))
    w["d_x"] = row(jnp.repeat(rep["d_skip"][i], SSD_HEAD_DIM))
    w["ssd_norm_g"] = row(rep["ssd_norm_g"][i])
    w["ffn_conv_bg"] = row(rep["ffn_conv_b"][i][:D_FF])
    w["ffn_conv_bv"] = row(rep["ffn_conv_b"][i][D_FF:])
    for n in ("ln1_g", "ln1_b", "ln2_g", "ln2_b"):
        w[n] = row(rep[n][i])
    return w


def _layer_fwd(h, w, tabs, expand, tag):
    cos, sin_a, sin_b = tabs
    s = {"h": h}
    proj = _mm(h, w["w_in_t"], trans_b=True, name=tag + "in_proj", tn=768)
    s["proj"] = proj
    qn = _rms_fwd(proj, OFF_Q, Q_LORA, w["q_norm_g"], name=tag + "q_norm")
    q_raw = _mm(qn, w["w_q_b"], name=tag + "q_up")
    q = _rope_q_fwd(q_raw, cos, sin_a, sin_b, name=tag + "q_rope")
    kvn = _rms_fwd(proj, OFF_KV, KV_LORA, w["kv_norm_g"], name=tag + "kv_norm")
    kv = _mm(kvn, w["w_kv_b"], name=tag + "kv_up", out_dtype=BF16)
    kpe = _rope_k_fwd(proj, cos, sin_a, sin_b, name=tag + "k_rope")
    o, lse = _flash_fwd(q, kv, kpe, name=tag + "attn")
    ya = _mm(o, w["w_o_attn"], name=tag + "attn_out")
    s.update(qn=qn, q=q, kvn=kvn, kv=kv, kpe=kpe, o=o, lse=lse, ya=ya)
    xs, bm, cm, dtx = _ssd_prep_fwd(proj, w["ssd_conv_w"], w["ssd_conv_b"], w["dt_bias"], expand, name=tag + "ssd_prep")
    y, prev = _ssd_fwd(xs, dtx, bm, cm, bm.T, w["a_x"], w["d_x"], name=tag + "ssd_scan")
    yn = _gnorm_fwd(y, proj, w["ssd_norm_g"], name=tag + "ssd_norm")
    ys = _mm(yn, w["w_o_ssd"], name=tag + "ssd_out")
    s.update(xs=xs, bm=bm, cm=cm, dtx=dtx, y=y, prev=prev, yn=yn, ys=ys)
    mixed = _mix_fwd(proj, ya, ys, name=tag + "mix")
    br = _mm(mixed, w["w_out"], name=tag + "mix_out")
    pre1, h1 = _ln_fwd(h, br, w["ln1_g"], w["ln1_b"], name=tag + "ln1")
    s.update(mixed=mixed, pre1=pre1, h1=h1)
    ug = _mm(h1, w["w_up_g"], name=tag + "up_g", tn=1408)
    uv = _mm(h1, w["w_up_v"], name=tag + "up_v", tn=1408)
    act = _ffn_act_fwd(ug, uv, w["ffn_conv_wg"], w["ffn_conv_wv"], w["ffn_conv_bg"], w["ffn_conv_bv"],
                       name=tag + "ffn_act")
    ffn = _mm(act, w["w_down"], name=tag + "down")
    pre2, h2 = _ln_fwd(h1, ffn, w["ln2_g"], w["ln2_b"], name=tag + "ln2")
    s.update(ug=ug, uv=uv, act=act, pre2=pre2)
    return h2, s


def _layer_bwd(dh2, w, s, tabs, reduce_m, tag, exchange=None):
    cos, sin_a, sin_b = tabs
    g = {}
    proj = s["proj"]
    dpre2, g["ln2_g"], g["ln2_b"] = _ln_bwd(dh2, s["pre2"], w["ln2_g"], name=tag + "ln2_bwd")
    g["w_down"] = _mm_tn(s["act"], dpre2, name=tag + "down_dw")
    dact = _mm(dpre2, w["w_down"], trans_b=True, name=tag + "down_dx", tn=1408)
    dcg, dcv, dwg, dwv, dbg, dbv = _ffn_act_bwd(s["ug"], s["uv"], dact, w["ffn_conv_wg"], w["ffn_conv_wv"],
                                                w["ffn_conv_bg"], w["ffn_conv_bv"], name=tag + "ffn_act_bwd")
    g["ffn_conv_w"] = jnp.concatenate([dwg, dwv], axis=1)
    g["ffn_conv_b"] = jnp.concatenate([dbg, dbv], axis=1).reshape(-1)
    dug = _conv_bwd_input(dcg, w["ffn_conv_wg"], FFN_CONV, name=tag + "ffn_conv_bwd_g", tc=1408)
    duv = _conv_bwd_input(dcv, w["ffn_conv_wv"], FFN_CONV, name=tag + "ffn_conv_bwd_v", tc=1408)
    g["w_up"] = jnp.concatenate([_mm_tn(s["h1"], dug, name=tag + "up_g_dw", tn=1408),
                                 _mm_tn(s["h1"], duv, name=tag + "up_v_dw", tn=1408)], axis=1)
    dh1 = _mm(dug, w["w_up_g"], trans_b=True, add=dpre2, add_scale=ALPHA, name=tag + "up_g_dx")
    dh1 = _mm(duv, w["w_up_v"], trans_b=True, add=dh1, name=tag + "up_v_dx")
    dpre1, g["ln1_g"], g["ln1_b"] = _ln_bwd(dh1, s["pre1"], w["ln1_g"], name=tag + "ln1_bwd")
    g["w_out"] = _mm_tn(s["mixed"], dpre1, name=tag + "mix_out_dw")
    dmix = _mm(dpre1, w["w_out"], trans_b=True, name=tag + "mix_out_dx")
    dya, dys, dga, dgs = _mix_bwd(dmix, proj, s["ya"], s["ys"], name=tag + "mix_bwd")
    g["w_o_attn"] = _mm_tn(s["o"], dya, name=tag + "attn_out_dw")
    do = _mm(dya, w["w_o_attn"], trans_b=True, out_dtype=BF16, name=tag + "attn_out_dx")
    delta = _attn_delta(do, s["o"], name=tag + "attn_delta")
    by_tile = lambda a: a.reshape(MLA_HEADS, -1, _tile(a.shape[1], 640))
    dq, dkn, dkp, dv, *exchanged = _flash_bwd(s["q"], s["kv"], s["kpe"], do, by_tile(s["lse"]), by_tile(delta),
                                              name=tag + "attn_bwd", exchange=exchange)
    dq_raw = _rope_q_bwd(dq, cos, sin_a, sin_b, name=tag + "q_rope_bwd")
    dwq = _mm_tn(s["qn"], dq_raw, name=tag + "q_up_dw")
    g["w_q_b"] = dwq.reshape(Q_LORA, MLA_HEADS, QHEAD)[:, :, :QK_NOPE + QK_ROPE].reshape(Q_LORA, -1)
    dqn = _mm(dq_raw, w["w_q_b"], trans_b=True, out_dtype=BF16, name=tag + "q_up_dx")
    dqlat, dgq = _rms_bwd(dqn, proj, OFF_Q, Q_LORA, w["q_norm_g"], name=tag + "q_norm_bwd")
    g["q_norm_g"] = dgq.reshape(-1)
    dkv = jnp.concatenate([dkn, dv], axis=1)
    dwkv = _mm_tn(s["kvn"], dkv, name=tag + "kv_up_dw")
    g["w_kv_b"] = jnp.swapaxes(dwkv.reshape(KV_LORA, 2, MLA_HEADS, QK_NOPE), 1, 2).reshape(KV_LORA, -1)
    dkvn = _mm(dkv, w["w_kv_b"], trans_b=True, out_dtype=BF16, name=tag + "kv_up_dx")
    dkvlat, dgkv = _rms_bwd(dkvn, proj, OFF_KV, KV_LORA, w["kv_norm_g"], name=tag + "kv_norm_bwd")
    g["kv_norm_g"] = dgkv.reshape(-1)
    dkpe = _rope_k_bwd(dkp, cos, sin_a, sin_b, name=tag + "k_rope_bwd")
    g["w_o_ssd"] = _mm_tn(s["yn"], dys, name=tag + "ssd_out_dw")
    dyn = _mm(dys, w["w_o_ssd"], trans_b=True, out_dtype=BF16, name=tag + "ssd_out_dx")
    dy, dz, dgn = _gnorm_bwd(dyn, s["y"], proj, w["ssd_norm_g"], name=tag + "ssd_norm_bwd")
    g["ssd_norm_g"] = dgn.reshape(-1)
    dxs, ddtx, dbm, dcm, da_x, dd_x = _ssd_bwd(s["xs"], s["dtx"], s["bm"], s["cm"], s["cm"].T, s["prev"], dy,
                                               w["a_x"], w["d_x"], name=tag + "ssd_scan_bwd")
    g["a_log"] = da_x.reshape(SSD_HEADS, SSD_HEAD_DIM).sum(axis=1) * w["a"]
    g["d_skip"] = dd_x.reshape(SSD_HEADS, SSD_HEAD_DIM).sum(axis=1)
    dconv, ddtr, dcw, dcb, ddtb = _ssd_prep_bwd_a(proj, dxs, dbm, dcm, ddtx, w["ssd_conv_w"], w["ssd_conv_b"],
                                                  w["dt_bias"], reduce_m, name=tag + "ssd_prep_bwd")
    g["ssd_conv_w"] = dcw
    g["ssd_conv_b"] = dcb.reshape(-1)
    g["dt_bias"] = ddtb.reshape(-1)[:SSD_HEADS]
    dxbc = _conv_bwd_input(dconv, w["ssd_conv_w"], SSD_CONV, name=tag + "ssd_conv_bwd", tc=1024)
    h = s["h"]
    comps = ((dqlat, OFF_Q), (dkvlat, OFF_KV), (dz, OFF_Z), (dxbc, OFF_XBC), (dga, OFF_GA), (dgs, OFF_GS),
             (dkpe, OFF_KPE), (ddtr, OFF_DT))
    dws = {off: _mm_tn(dc, h, name=f"{tag}in_dw{n}") for n, (dc, off) in enumerate(comps)}
    with_w = lambda group: [(dc, w["w_in_t"][off:off + dc.shape[1]]) for dc, off in group]
    wide = [c for c in comps if c[1] in (OFF_Z, OFF_XBC)]
    rest = [c for c in comps if c[1] not in (OFF_Z, OFF_XBC)]
    dh = _mm_sum(with_w(wide), dpre1, add_scale=ALPHA, name=tag + "in_dx_wide")
    dh = _mm_sum(with_w(rest), dh, name=tag + "in_dx_rest")
    g["w_in"] = jnp.concatenate([dws[OFF_Q], dws[OFF_KV], dws[OFF_KPE][:QK_ROPE], dws[OFF_Z], dws[OFF_XBC],
                                 dws[OFF_DT][:SSD_HEADS], dws[OFF_GA], dws[OFF_GS]], axis=0)
    return dh, g, (exchanged[0] if exchanged else None)


def _local_step(x, target, full, rep):
    seq = x.shape[0]
    t = seq + ROW0
    tabs = _rope_tables(t)
    expand = _expand_matrix()
    reduce_m = expand.T
    meta = _join(full["meta_tokens"].reshape(N_CHIPS, N_META, -1), "col")
    xin = jnp.concatenate([jnp.zeros((PAD, D_MODEL), F32), meta, x], axis=0)
    row = lambda v: v.reshape(1, -1)
    _, h = _ln_fwd(xin, None, row(rep["emb_ln_g"]), row(rep["emb_ln_b"]), name="emb_ln")
    ws, saved = [], []
    for i in range(DEPTH):
        w = _layer_weights(full, rep, i)
        h, s = _layer_fwd(h, w, tabs, expand, f"l{i}_")
        ws.append(w)
        saved.append(s)
    dh, loss = _loss_grad(h, target, name="loss")
    dh, g1, _ = _layer_bwd(dh, ws[1], saved[1], tabs, reduce_m, "l1_")
    g4, layout1 = _pack_part(g1, outside=False)
    dh, g0, parts1 = _layer_bwd(dh, ws[0], saved[0], tabs, reduce_m, "l0_", exchange=_reduce_begin(g4, "reduce1_"))
    dxin, dg, db = _ln_bwd(dh, xin, row(rep["emb_ln_g"]), name="emb_ln_bwd")
    g0["emb_ln_g"] = dg
    g0["emb_ln_b"] = db
    g0["meta_tokens"] = dxin[PAD:ROW0]
    return loss, dxin[ROW0:], g0, parts1, layout1


def _pack_part(g, outside):
    sharded = PART_SHARDED + (OUTSIDE_SHARDED if outside else ())
    replicated = PART_REPLICATED + (OUTSIDE_REPLICATED if outside else ())
    pieces, layout, row0 = [], [], 0

    def add(name, piece):
        nonlocal row0
        pieces.append(piece)
        layout.append((name, row0, piece.shape[1]))
        row0 += piece.shape[1]

    for n, kind in sharded:
        add(n, _chip_segments(g[n], "row" if n == "w_in" else kind).reshape(N_CHIPS, -1, LANES))
    for n in replicated:
        add(n, jnp.broadcast_to(g[n].reshape(1, -1, LANES), (N_CHIPS, g[n].size // LANES, LANES)))
    tail = jnp.concatenate([g[n].reshape(-1) for n in PART_TAIL])
    tail = jnp.pad(tail, (0, LANES - tail.shape[0]))
    add("tail", jnp.broadcast_to(tail.reshape(1, 1, LANES), (N_CHIPS, 1, LANES)))
    rows = -(-row0 // REDUCE_ROW_ALIGN) * REDUCE_ROW_ALIGN
    pieces.append(jnp.zeros((N_CHIPS, rows - row0, LANES), F32))
    return jnp.concatenate(pieces, axis=1), layout


def _core_index():
    return lax.axis_index("c").astype(jnp.int32).reshape(1)


def _reduce_begin(g4, tag):
    n, r, _ = g4.shape
    g5 = g4.reshape(n, 2, r // 2, LANES)
    got = _sibling_swap(g5, name=tag + "pair_swap")
    return _pair_add(g5, got, _core_index(), name=tag + "pair_add")


def _reduce_end(parts, tag):
    half = _sum_chips(parts, _core_index(), name=tag + "chip_sum")
    both = _sibling_allgather(half, name=tag + "pair_gather")
    return both.reshape(-1, LANES)


def _unpack_part(flat, layout, shapes):
    out = {}
    for n, row0, rows in layout:
        piece = flat[row0:row0 + rows]
        if n == "tail":
            vec = piece.reshape(-1)
            for k, m in enumerate(PART_TAIL):
                out[m] = vec[k * SSD_HEADS:(k + 1) * SSD_HEADS]
        elif n == "w_in":
            out[n] = piece.reshape(shapes[n][1], shapes[n][0]).T
        else:
            out[n] = piece.reshape(shapes[n])
    return out


def kernel(x, meta_tokens, emb_ln_g, emb_ln_b, w_in, q_norm_g, w_q_b, kv_norm_g, w_kv_b, w_o_attn, ssd_conv_w, ssd_conv_b, dt_bias, a_log, d_skip, ssd_norm_g, w_o_ssd, w_out, ln1_g, ln1_b, w_up, ffn_conv_w, ffn_conv_b, w_down, ln2_g, ln2_b, loss_target, m_meta_tokens, m_emb_ln_g, m_emb_ln_b, m_w_in, m_q_norm_g, m_w_q_b, m_kv_norm_g, m_w_kv_b, m_w_o_attn, m_ssd_conv_w, m_ssd_conv_b, m_dt_bias, m_a_log, m_d_skip, m_ssd_norm_g, m_w_o_ssd, m_w_out, m_ln1_g, m_ln1_b, m_w_up, m_ffn_conv_w, m_ffn_conv_b, m_w_down, m_ln2_g, m_ln2_b, v_meta_tokens, v_emb_ln_g, v_emb_ln_b, v_w_in, v_q_norm_g, v_w_q_b, v_kv_norm_g, v_w_kv_b, v_w_o_attn, v_ssd_conv_w, v_ssd_conv_b, v_dt_bias, v_a_log, v_d_skip, v_ssd_norm_g, v_w_o_ssd, v_w_out, v_ln1_g, v_ln1_b, v_w_up, v_ffn_conv_w, v_ffn_conv_b, v_w_down, v_ln2_g, v_ln2_b):
    given = dict(locals())
    local_w = {n: given[n] for n in WEIGHTS}
    local_m = {n: given["m_" + n] for n in WEIGHTS}
    local_v = {n: given["v_" + n] for n in WEIGHTS}
    full = _gather_weights(local_w)
    rep = {n: local_w[n] for n in REPLICATED}
    loss, grad_x, g0, parts1, layout1 = _local_step(x[0], loss_target[0], full, rep)
    g4, layout0 = _pack_part(g0, outside=True)
    parts0 = _chip_exchange(_reduce_begin(g4, "reduce0_"), name="reduce0_chip_exchange")
    outside = [n for n, _ in OUTSIDE_SHARDED] + list(OUTSIDE_REPLICATED)
    shapes = {n: (local_w[n].shape if n in outside else local_w[n].shape[1:]) for n in WEIGHTS}
    by_layer = [_unpack_part(_reduce_end(parts0, "reduce0_"), layout0, shapes),
                _unpack_part(_reduce_end(parts1, "reduce1_"), layout1, shapes)]
    grad = {n: (by_layer[0][n] if n in outside else jnp.stack([p[n] for p in by_layer])) for n in WEIGHTS}
    upd = {}
    small = [n for n in WEIGHTS if n not in GATHER_BF16]
    for n in GATHER_BF16:
        upd[n] = _adamw(grad[n], local_w[n], local_m[n], local_v[n], name="adamw_" + n)
    res = _adamw_small([(grad[n], local_w[n], local_m[n], local_v[n]) for n in small], name="adamw_small")
    upd.update(zip(small, res))
    total = lax.psum(loss[0, 0], ("x", "y", "c"))
    outs = [total, grad_x[None]] + [grad[n] for n in WEIGHTS]
    for q in range(3):
        outs.extend(upd[n][q] for n in WEIGHTS)
    return tuple(outs)
```

```python
import functools
import math

import numpy as np
import jax
import jax.numpy as jnp
from jax import lax
from jax.experimental import pallas as pl
from jax.experimental.pallas import tpu as pltpu

F32 = jnp.float32
BF16 = jnp.bfloat16

D_MODEL = 1024
N_META = 16
DEPTH = 2
MLA_HEADS = 8
Q_LORA = 768
KV_LORA = 256
QK_NOPE = 128
QK_ROPE = 64
V_HEAD = 128
ROPE_THETA = 10000.0
NEG_INF = -1e30
PAD_KEY_SCORE = -1e30
SSD_INNER = 2048
SSD_HEAD_DIM = 64
SSD_HEADS = 32
SSD_GROUPS = 4
SSD_STATE = 128
SSD_CONV = 4
SSD_CONV_DIM = 3072
CHUNK = 128
D_FF = 2816
FFN_CONV = 3
LN_EPS = 1e-5
RMS_EPS = 1e-6
ALPHA = (2 * DEPTH) ** 0.25
ATTN_SCALE = (QK_NOPE + QK_ROPE) ** -0.5
LOG2E = math.log2(math.e)
LN2 = math.log(2.0)
Q_SCALE = ATTN_SCALE * LOG2E
ADAM_LR = 0.001
ADAM_B1 = 0.9
ADAM_B2 = 0.999
ADAM_EPS = 1e-08
ADAM_WD = 0.01
ADAM_STEP = 10

LANES = 128
PAD = 112
ROW0 = PAD + N_META
QHEAD = 256
GROUP_W = SSD_INNER // SSD_GROUPS
HALO = 8
VMEM_LIMIT_BYTES = 56 * 1024 * 1024
MM_VMEM_BUDGET = 40 * 1024 * 1024
N_CHIPS = 4

OFF_Q, OFF_KV, OFF_Z, OFF_XBC, OFF_GA, OFF_GS, OFF_KPE, OFF_DT = 0, 768, 1024, 3072, 6144, 7168, 8192, 8320
IN_COLS_P = 8448

NT_DIMS = (((1,), (1,)), ((), ()))
NN_DIMS = (((1,), (0,)), ((), ()))
TN_DIMS = (((0,), (0,)), ((), ()))

SHARDED = (("meta_tokens", "col"), ("w_in", "col"), ("w_q_b", "col"), ("w_kv_b", "col"), ("w_o_attn", "row"),
           ("ssd_conv_w", "col"), ("w_o_ssd", "row"), ("w_out", "row"), ("w_up", "col"), ("ffn_conv_w", "col"),
           ("w_down", "row"))
REPLICATED = ("emb_ln_g", "emb_ln_b", "q_norm_g", "kv_norm_g", "ssd_conv_b", "dt_bias", "a_log", "d_skip",
              "ssd_norm_g", "ln1_g", "ln1_b", "ffn_conv_b", "ln2_g", "ln2_b")
WEIGHTS = ("meta_tokens", "emb_ln_g", "emb_ln_b", "w_in", "q_norm_g", "w_q_b", "kv_norm_g", "w_kv_b", "w_o_attn",
           "ssd_conv_w", "ssd_conv_b", "dt_bias", "a_log", "d_skip", "ssd_norm_g", "w_o_ssd", "w_out", "ln1_g",
           "ln1_b", "w_up", "ffn_conv_w", "ffn_conv_b", "w_down", "ln2_g", "ln2_b")
GATHER_BF16 = ("w_in", "w_q_b", "w_kv_b", "w_o_attn", "w_o_ssd", "w_out", "w_up", "w_down")
GATHER_EARLY = ("meta_tokens", "w_in", "w_q_b", "w_kv_b", "ssd_conv_w")
GATHER_LATE = ("w_o_attn", "w_o_ssd", "w_out", "w_up", "ffn_conv_w", "w_down")
PART_SHARDED = tuple(item for item in SHARDED if item[0] != "meta_tokens")
PART_REPLICATED = ("q_norm_g", "kv_norm_g", "ssd_conv_b", "ssd_norm_g", "ln1_g", "ln1_b", "ffn_conv_b", "ln2_g", "ln2_b")
PART_TAIL = ("dt_bias", "a_log", "d_skip")
OUTSIDE_SHARDED = (("meta_tokens", "col"),)
OUTSIDE_REPLICATED = ("emb_ln_g", "emb_ln_b")
REDUCE_ROW_ALIGN = 512


def _tile(n, target, base=LANES):
    best = None
    d = base
    while d <= min(n, target):
        if n % d == 0:
            best = d
        d += base
    return n if best is None else best


def _cp(*sem):
    return pltpu.CompilerParams(dimension_semantics=sem, vmem_limit_bytes=VMEM_LIMIT_BYTES)


def _sds(shape, dtype):
    return jax.ShapeDtypeStruct(shape, dtype)


def _row_ids(i, tr, shape):
    return i * tr + lax.broadcasted_iota(jnp.int32, shape, 0)


def _sigmoid(x):
    return 1.0 / (1.0 + jnp.exp(-x))


def _mm_row_tile(m, tn, tk, nk, a_bytes, b_bytes, out_bytes, add_bytes):
    best = LANES
    for tm in range(LANES, m + 1, LANES):
        if m % tm:
            continue
        blocks = 2 * (tm * tk * a_bytes + tk * tn * b_bytes + tm * tn * (out_bytes + add_bytes))
        temps = tm * tn * 4 * (2 if nk > 1 else 1) + tm * tk * 2 + tk * tn * 2
        if blocks + temps <= MM_VMEM_BUDGET:
            best = tm
    return best


def _mm(a, b, *, name, trans_b=False, out_dtype=F32, add=None, add_scale=1.0, tn=1024, tk=1408):
    m, k_dim = a.shape
    n = b.shape[0] if trans_b else b.shape[1]
    tn, tk = _tile(n, tn), _tile(k_dim, tk)
    nk = k_dim // tk
    has_add = add is not None
    tm = _mm_row_tile(m, tn, tk, nk, a.dtype.itemsize, b.dtype.itemsize, jnp.dtype(out_dtype).itemsize,
                      add.dtype.itemsize if has_add else 0)
    dims = NT_DIMS if trans_b else NN_DIMS

    def body(*refs):
        a_ref, b_ref = refs[0], refs[1]
        r_ref = refs[2] if has_add else None
        o_ref = refs[3] if has_add else refs[2]
        part = lax.dot_general(a_ref[...].astype(BF16), b_ref[...].astype(BF16), dims, preferred_element_type=F32)

        def finish(r):
            if has_add:
                r = r + add_scale * r_ref[...].astype(F32)
            o_ref[...] = r.astype(out_dtype)

        if nk == 1:
            finish(part)
        else:
            acc = refs[-1]
            kk = pl.program_id(2)

            @pl.when(kk == 0)
            def _():
                acc[...] = part

            @pl.when(kk > 0)
            def _():
                acc[...] += part

            @pl.when(kk == nk - 1)
            def _():
                finish(acc[...])

    in_specs = [pl.BlockSpec((tm, tk), lambda i, j, kk: (i, kk)),
                pl.BlockSpec((tn, tk), lambda i, j, kk: (j, kk)) if trans_b
                else pl.BlockSpec((tk, tn), lambda i, j, kk: (kk, j))]
    args = [a, b]
    if has_add:
        in_specs.append(pl.BlockSpec((tm, tn), lambda i, j, kk: (i, j)))
        args.append(add)
    return pl.pallas_call(
        body, name=name, grid=(m // tm, n // tn, nk), in_specs=in_specs,
        out_specs=pl.BlockSpec((tm, tn), lambda i, j, kk: (i, j)),
        out_shape=_sds((m, n), out_dtype),
        scratch_shapes=[pltpu.VMEM((tm, tn), F32)] if nk > 1 else [],
        compiler_params=_cp("parallel", "parallel", "arbitrary"),
    )(*args)


def _mm_sum(pairs, add, *, name, add_scale=1.0, tm=640):
    m, n = add.shape
    tm = _tile(m, tm)
    npairs = len(pairs)

    def body(*refs):
        a_refs, b_refs = refs[:npairs], refs[npairs:2 * npairs]
        r_ref, o_ref = refs[2 * npairs], refs[2 * npairs + 1]
        acc = add_scale * r_ref[...]
        for a_ref, b_ref in zip(a_refs, b_refs):
            acc = acc + jnp.dot(a_ref[...].astype(BF16), b_ref[...].astype(BF16), preferred_element_type=F32)
        o_ref[...] = acc

    in_specs = ([pl.BlockSpec((tm, a.shape[1]), lambda i: (i, 0)) for a, _ in pairs]
                + [pl.BlockSpec(b.shape, lambda i: (0, 0)) for _, b in pairs]
                + [pl.BlockSpec((tm, n), lambda i: (i, 0))])
    return pl.pallas_call(
        body, name=name, grid=(m // tm,), in_specs=in_specs, out_specs=pl.BlockSpec((tm, n), lambda i: (i, 0)),
        out_shape=_sds((m, n), F32), compiler_params=_cp("parallel"),
    )(*[a for a, _ in pairs], *[b for _, b in pairs], add)


def _mm_tn(a, b, *, name, tko=1408, tn=1024, tt=640):
    t, k_dim = a.shape
    n = b.shape[1]
    tko, tn, tt = _tile(k_dim, tko), _tile(n, tn), _tile(t, tt)

    def body(a_ref, b_ref, o_ref):
        part = lax.dot_general(a_ref[...].astype(BF16), b_ref[...].astype(BF16), TN_DIMS, preferred_element_type=F32)
        tt_i = pl.program_id(2)

        @pl.when(tt_i == 0)
        def _():
            o_ref[...] = part

        @pl.when(tt_i > 0)
        def _():
            o_ref[...] += part

    return pl.pallas_call(
        body, name=name, grid=(k_dim // tko, n // tn, t // tt),
        in_specs=[pl.BlockSpec((tt, tko), lambda i, j, s: (s, i)), pl.BlockSpec((tt, tn), lambda i, j, s: (s, j))],
        out_specs=pl.BlockSpec((tko, tn), lambda i, j, s: (i, j)),
        out_shape=_sds((k_dim, n), F32),
        compiler_params=_cp("parallel", "parallel", "arbitrary"),
    )(a, b)


def _ln_fwd(h, branch, g, b, *, name):
    t, d = h.shape
    tr = _tile(t, 640)
    has_branch = branch is not None

    def body(*refs):
        if has_branch:
            h_ref, br_ref, g_ref, b_ref, pre_ref, o_ref = refs
            pre = ALPHA * h_ref[...] + br_ref[...]
            pre_ref[...] = pre
        else:
            h_ref, g_ref, b_ref, o_ref = refs
            pre = h_ref[...]
        mu = jnp.mean(pre, axis=1, keepdims=True)
        xc = pre - mu
        var = jnp.mean(xc * xc, axis=1, keepdims=True)
        y = xc * lax.rsqrt(var + LN_EPS) * g_ref[...] + b_ref[...]
        rows = _row_ids(pl.program_id(0), tr, (tr, 1))
        o_ref[...] = jnp.where(rows >= PAD, y, 0.0)

    row_spec = pl.BlockSpec((tr, d), lambda i: (i, 0))
    vec_spec = pl.BlockSpec((1, d), lambda i: (0, 0))
    if has_branch:
        return pl.pallas_call(
            body, name=name, grid=(t // tr,), in_specs=[row_spec, row_spec, vec_spec, vec_spec],
            out_specs=[row_spec, row_spec], out_shape=[_sds((t, d), F32), _sds((t, d), F32)],
            compiler_params=_cp("parallel"))(h, branch, g, b)
    out = pl.pallas_call(
        body, name=name, grid=(t // tr,), in_specs=[row_spec, vec_spec, vec_spec],
        out_specs=row_spec, out_shape=_sds((t, d), F32), compiler_params=_cp("parallel"))(h, g, b)
    return h, out


def _ln_bwd(dy, pre, g, *, name):
    t, d = pre.shape
    tr = _tile(t, 640)

    def body(dy_ref, pre_ref, g_ref, dpre_ref, dg_ref, db_ref):
        i = pl.program_id(0)
        pre_v = pre_ref[...]
        mu = jnp.mean(pre_v, axis=1, keepdims=True)
        xc = pre_v - mu
        var = jnp.mean(xc * xc, axis=1, keepdims=True)
        rstd = lax.rsqrt(var + LN_EPS)
        xhat = xc * rstd
        rows = _row_ids(i, tr, (tr, 1))
        dym = jnp.where(rows >= PAD, dy_ref[...], 0.0)
        gdy = dym * g_ref[...]
        m1 = jnp.mean(gdy, axis=1, keepdims=True)
        m2 = jnp.mean(gdy * xhat, axis=1, keepdims=True)
        dpre_ref[...] = rstd * (gdy - m1 - xhat * m2)
        pg = jnp.sum(dym * xhat, axis=0, keepdims=True)
        pb = jnp.sum(dym, axis=0, keepdims=True)

        @pl.when(i == 0)
        def _():
            dg_ref[...] = pg
            db_ref[...] = pb

        @pl.when(i > 0)
        def _():
            dg_ref[...] += pg
            db_ref[...] += pb

    row_spec = pl.BlockSpec((tr, d), lambda i: (i, 0))
    vec_spec = pl.BlockSpec((1, d), lambda i: (0, 0))
    return pl.pallas_call(
        body, name=name, grid=(t // tr,), in_specs=[row_spec, row_spec, vec_spec],
        out_specs=[row_spec, vec_spec, vec_spec],
        out_shape=[_sds((t, d), F32), _sds((1, d), F32), _sds((1, d), F32)],
        compiler_params=_cp("arbitrary"))(dy, pre, g)


def _rms_fwd(proj, col_off, width, g, *, name):
    t = proj.shape[0]
    tr = _tile(t, 640)
    cb = col_off // width

    def body(x_ref, g_ref, o_ref):
        x = x_ref[...]
        r = lax.rsqrt(jnp.mean(x * x, axis=1, keepdims=True) + RMS_EPS)
        o_ref[...] = (x * r * g_ref[...]).astype(BF16)

    return pl.pallas_call(
        body, name=name, grid=(t // tr,),
        in_specs=[pl.BlockSpec((tr, width), lambda i: (i, cb)), pl.BlockSpec((1, width), lambda i: (0, 0))],
        out_specs=pl.BlockSpec((tr, width), lambda i: (i, 0)), out_shape=_sds((t, width), BF16),
        compiler_params=_cp("parallel"))(proj, g)


def _rms_bwd(dy, proj, col_off, width, g, *, name):
    t = proj.shape[0]
    tr = _tile(t, 640)
    cb = col_off // width

    def body(dy_ref, x_ref, g_ref, dx_ref, dg_ref):
        i = pl.program_id(0)
        x = x_ref[...]
        dyv = dy_ref[...].astype(F32)
        r = lax.rsqrt(jnp.mean(x * x, axis=1, keepdims=True) + RMS_EPS)
        gdy = dyv * g_ref[...]
        m = jnp.mean(x * gdy, axis=1, keepdims=True)
        dx_ref[...] = (r * gdy - x * (r * r * r) * m).astype(BF16)
        pg = jnp.sum(dyv * x * r, axis=0, keepdims=True)

        @pl.when(i == 0)
        def _():
            dg_ref[...] = pg

        @pl.when(i > 0)
        def _():
            dg_ref[...] += pg

    return pl.pallas_call(
        body, name=name, grid=(t // tr,),
        in_specs=[pl.BlockSpec((tr, width), lambda i: (i, 0)), pl.BlockSpec((tr, width), lambda i: (i, cb)),
                  pl.BlockSpec((1, width), lambda i: (0, 0))],
        out_specs=[pl.BlockSpec((tr, width), lambda i: (i, 0)), pl.BlockSpec((1, width), lambda i: (0, 0))],
        out_shape=[_sds((t, width), BF16), _sds((1, width), F32)],
        compiler_params=_cp("arbitrary"))(dy, proj, g)


def _rope_apply(r, cos, sin_a, sin_b):
    return r * cos + pltpu.roll(r, 96, 1) * sin_a + pltpu.roll(r, 32, 1) * sin_b


def _rope_apply_t(dr, cos, sin_a, sin_b):
    return dr * cos + pltpu.roll(dr * sin_a, 32, 1) + pltpu.roll(dr * sin_b, 96, 1)


def _rope_q_fwd(q, cos, sin_a, sin_b, *, name):
    t, w = q.shape
    tr = _tile(t, 128)

    def body(q_ref, c_ref, sa_ref, sb_ref, o_ref):
        c, sa, sb = c_ref[...], sa_ref[...], sb_ref[...]
        flag = lax.broadcasted_iota(jnp.int32, (tr, LANES), 1) == QK_ROPE
        for h in range(MLA_HEADS):
            base = h * QHEAD
            o_ref[:, base:base + LANES] = (q_ref[:, base:base + LANES] * Q_SCALE).astype(BF16)
            rot = _rope_apply(q_ref[:, base + LANES:base + QHEAD], c, sa, sb)
            o_ref[:, base + LANES:base + QHEAD] = jnp.where(flag, 1.0, rot * Q_SCALE).astype(BF16)

    tab = pl.BlockSpec((tr, LANES), lambda i: (i, 0))
    row = pl.BlockSpec((tr, w), lambda i: (i, 0))
    return pl.pallas_call(body, name=name, grid=(t // tr,), in_specs=[row, tab, tab, tab], out_specs=row,
                          out_shape=_sds((t, w), BF16), compiler_params=_cp("parallel"))(q, cos, sin_a, sin_b)


def _rope_q_bwd(dq, cos, sin_a, sin_b, *, name):
    t, w = dq.shape
    tr = _tile(t, 128)

    def body(dq_ref, c_ref, sa_ref, sb_ref, o_ref):
        c, sa, sb = c_ref[...], sa_ref[...], sb_ref[...]
        for h in range(MLA_HEADS):
            base = h * QHEAD
            o_ref[:, base:base + LANES] = (dq_ref[:, base:base + LANES] * ATTN_SCALE).astype(BF16)
            d_rot = _rope_apply_t(dq_ref[:, base + LANES:base + QHEAD], c, sa, sb)
            o_ref[:, base + LANES:base + QHEAD] = (d_rot * ATTN_SCALE).astype(BF16)

    tab = pl.BlockSpec((tr, LANES), lambda i: (i, 0))
    row = pl.BlockSpec((tr, w), lambda i: (i, 0))
    return pl.pallas_call(body, name=name, grid=(t // tr,), in_specs=[row, tab, tab, tab], out_specs=row,
                          out_shape=_sds((t, w), BF16), compiler_params=_cp("parallel"))(dq, cos, sin_a, sin_b)


def _rope_k_fwd(proj, cos, sin_a, sin_b, *, name):
    t = proj.shape[0]
    tr = _tile(t, 640)
    cb = OFF_KPE // LANES

    def body(x_ref, c_ref, sa_ref, sb_ref, o_ref):
        rot = _rope_apply(x_ref[...], c_ref[...], sa_ref[...], sb_ref[...])
        rows = _row_ids(pl.program_id(0), tr, (tr, LANES))
        lane = lax.broadcasted_iota(jnp.int32, (tr, LANES), 1)
        o_ref[...] = jnp.where((lane == QK_ROPE) & (rows < PAD), PAD_KEY_SCORE, rot).astype(BF16)

    tab = pl.BlockSpec((tr, LANES), lambda i: (i, 0))
    return pl.pallas_call(body, name=name, grid=(t // tr,),
                          in_specs=[pl.BlockSpec((tr, LANES), lambda i: (i, cb)), tab, tab, tab], out_specs=tab,
                          out_shape=_sds((t, LANES), BF16), compiler_params=_cp("parallel"))(proj, cos, sin_a, sin_b)


def _rope_k_bwd(dkp, cos, sin_a, sin_b, *, name):
    nh, t, _ = dkp.shape
    tr = _tile(t, 640)

    def body(d_ref, c_ref, sa_ref, sb_ref, o_ref):
        tot = d_ref[0]
        for h in range(1, nh):
            tot = tot + d_ref[h]
        o_ref[...] = _rope_apply_t(tot, c_ref[...], sa_ref[...], sb_ref[...]).astype(BF16)

    tab = pl.BlockSpec((tr, LANES), lambda i: (i, 0))
    return pl.pallas_call(body, name=name, grid=(t // tr,),
                          in_specs=[pl.BlockSpec((nh, tr, LANES), lambda i: (0, i, 0)), tab, tab, tab], out_specs=tab,
                          out_shape=_sds((t, LANES), BF16), compiler_params=_cp("parallel"))(dkp, cos, sin_a, sin_b)


def _causal(tb, keys_first=False):
    a = lax.broadcasted_iota(jnp.int32, (tb, tb), 0)
    b = lax.broadcasted_iota(jnp.int32, (tb, tb), 1)
    return a <= b if keys_first else b <= a


def _flash_fwd(q, kv, kpe, *, name, gather=()):
    t = q.shape[0]
    nh = MLA_HEADS
    tb = _tile(t, 640)
    nb = t // tb
    na = len(gather)

    def attend(q_ref, kn_ref, v_ref, kp_ref, o_ref, lse_ref, extra):
        i = pl.program_id(1)
        qv = q_ref[...]

        def scores(j):
            r0 = pl.multiple_of(j * tb, tb)
            k = jnp.concatenate([kn_ref[pl.ds(r0, tb), :], kp_ref[pl.ds(r0, tb), :]], axis=1)
            return lax.dot_general(qv, k, NT_DIMS, preferred_element_type=F32)

        def update(s, j, state):
            m_prev, l_prev, acc = state
            m_new = jnp.maximum(m_prev, jnp.max(s, axis=1, keepdims=True))
            p = jnp.exp2(s - m_new)
            corr = jnp.exp2(m_prev - m_new)
            r0 = pl.multiple_of(j * tb, tb)
            pv = jnp.dot(p.astype(BF16), v_ref[pl.ds(r0, tb), :], preferred_element_type=F32)
            return m_new, corr * l_prev + jnp.sum(p, axis=1, keepdims=True), corr * acc + pv

        def loop(j, carry):
            s_cur, st = carry
            s_next = scores(j + 1)
            return s_next, update(s_cur, j, st)

        state = (jnp.full((tb, 1), NEG_INF, F32), jnp.zeros((tb, 1), F32), jnp.zeros((tb, V_HEAD), F32))
        s_diag, state = lax.fori_loop(0, i, loop, (scores(0), state))
        m, l, acc = update(jnp.where(_causal(tb), s_diag, NEG_INF), i, state)
        o_ref[...] = (acc / l).astype(BF16)
        lse_ref[0] = m + jnp.log2(l)

        if na:
            step = pl.program_id(0) * nb + i
            for phase, at in enumerate((0, (nh * nb) // 2, nh * nb - 1)):
                @pl.when(step == at)
                def _(phase=phase):
                    _chip_allgather_phase(phase, extra[:na], extra[na:2 * na], *extra[2 * na:])

    def body(q_ref, kn_ref, v_ref, kp_ref, *rest):
        attend(q_ref, kn_ref, v_ref, kp_ref, *rest[na:na + 2], extra=rest[:na] + rest[na + 2:])

    gather = list(gather)
    return pl.pallas_call(
        body, name=name, grid=(nh, nb),
        in_specs=[pl.BlockSpec((tb, QHEAD), lambda h, i: (i, h)),
                  pl.BlockSpec((t, LANES), lambda h, i: (0, h)),
                  pl.BlockSpec((t, LANES), lambda h, i: (0, nh + h)),
                  pl.BlockSpec((t, LANES), lambda h, i: (0, 0))] + [_ANY] * na,
        out_specs=[pl.BlockSpec((tb, V_HEAD), lambda h, i: (i, h)),
                   pl.BlockSpec((1, tb, 1), lambda h, i: (h, i, 0))] + [_ANY] * na,
        out_shape=[_sds((t, nh * V_HEAD), BF16), _sds((nh, t, 1), F32)] + _chip_allgather_shapes(gather),
        scratch_shapes=_chip_allgather_sems(na) if na else [],
        compiler_params=_cp("arbitrary", "arbitrary"))(q, kv, kv, kpe, *gather)


def _attn_delta(do, o, *, name):
    t = o.shape[0]
    nh = MLA_HEADS
    tr = _tile(t, 640)

    def body(do_ref, o_ref, d_ref):
        d_ref[0] = jnp.sum(do_ref[...].astype(F32) * o_ref[...].astype(F32), axis=1, keepdims=True)

    blk = pl.BlockSpec((tr, V_HEAD), lambda h, i: (i, h))
    return pl.pallas_call(body, name=name, grid=(nh, t // tr), in_specs=[blk, blk],
                          out_specs=pl.BlockSpec((1, tr, 1), lambda h, i: (h, i, 0)),
                          out_shape=_sds((nh, t, 1), F32), compiler_params=_cp("parallel", "parallel"))(do, o)


def _flash_bwd(q, kv, kpe, do, lse, delta, *, name, exchange=None):
    t = q.shape[0]
    nh = MLA_HEADS
    tb = lse.shape[2]
    nb = t // tb
    fused = exchange is not None

    def body(*refs):
        q_ref, do_ref, lse_ref, dl_ref, kn_ref, v_ref, kp_ref = refs[:7]
        dq_ref, dkn_ref, dkp_ref, dv_ref = refs[7 + fused:11 + fused]
        j = pl.program_id(1)

        if fused:
            copies = functools.partial(_chip_exchange_copies, refs[7], refs[11 + fused], *refs[12 + fused:])
            first = (pl.program_id(0) == 0) & (j == 0)
            last = (pl.program_id(0) == nh - 1) & (j == nb - 1)

            @pl.when(first)
            def _():
                _chip_exchange_start(copies())

        @pl.when(j == 0)
        def _():
            dq_ref[...] = jnp.zeros((t, QHEAD), F32)

        k = jnp.concatenate([kn_ref[...], kp_ref[...]], axis=1)
        v = v_ref[...]

        def tile(i, carry, masked):
            dk, dv = carry
            r0 = pl.multiple_of(i * tb, tb)
            qv = q_ref[pl.ds(r0, tb), :]
            dov = do_ref[pl.ds(r0, tb), :]
            st = lax.dot_general(k, qv, NT_DIMS, preferred_element_type=F32)
            if masked:
                st = jnp.where(_causal(tb, keys_first=True), st, NEG_INF)
            pt = jnp.exp2(st - lse_ref[0, pl.ds(i, 1), :])
            dpt = lax.dot_general(v, dov, NT_DIMS, preferred_element_type=F32)
            dst = (pt * (dpt - dl_ref[0, pl.ds(i, 1), :])).astype(BF16)
            dv = dv + jnp.dot(pt.astype(BF16), dov, preferred_element_type=F32)
            dk = dk + jnp.dot(dst, qv, preferred_element_type=F32)
            dq_ref[pl.ds(r0, tb), :] += lax.dot_general(dst, k, TN_DIMS, preferred_element_type=F32)
            return dk, dv

        carry = tile(j, (jnp.zeros((tb, QHEAD), F32), jnp.zeros((tb, V_HEAD), F32)), True)
        dk, dv = lax.fori_loop(j + 1, nb, lambda i, c: tile(i, c, False), carry)
        dkn_ref[...] = (dk[:, :LANES] * LN2).astype(BF16)
        dkp_ref[0] = dk[:, LANES:] * LN2
        dv_ref[...] = dv.astype(BF16)

        if fused:
            @pl.when(last)
            def _():
                _chip_exchange_wait(copies())

    stat = pl.BlockSpec((1, nb, tb), lambda h, j: (h, 0, 0))
    in_specs = [pl.BlockSpec((t, QHEAD), lambda h, j: (0, h)),
                pl.BlockSpec((t, V_HEAD), lambda h, j: (0, h)),
                stat, stat,
                pl.BlockSpec((tb, LANES), lambda h, j: (j, h)),
                pl.BlockSpec((tb, LANES), lambda h, j: (j, nh + h)),
                pl.BlockSpec((tb, LANES), lambda h, j: (j, 0))]
    out_specs = [pl.BlockSpec((t, QHEAD), lambda h, j: (0, h)),
                 pl.BlockSpec((tb, LANES), lambda h, j: (j, h)),
                 pl.BlockSpec((1, tb, LANES), lambda h, j: (h, j, 0)),
                 pl.BlockSpec((tb, V_HEAD), lambda h, j: (j, h))]
    out_shape = [_sds((t, nh * QHEAD), F32), _sds((t, nh * LANES), BF16), _sds((nh, t, LANES), F32),
                 _sds((t, nh * V_HEAD), BF16)]
    args = [q, do, lse, delta, kv, kv, kpe]
    scratch = []
    if fused:
        in_specs.append(_ANY)
        out_specs.append(_ANY)
        out_shape.append(_sds(exchange.shape, exchange.dtype))
        args.append(exchange)
        scratch = _CHIP_EXCHANGE_SEMS
    return pl.pallas_call(body, name=name, grid=(nh, nb), in_specs=in_specs, out_specs=out_specs, out_shape=out_shape,
                          scratch_shapes=scratch, compiler_params=_cp("arbitrary", "arbitrary"))(*args)


def _fill_prev(buf, x_ref, halo_ref, i, tr):
    buf[pl.ds(0, HALO), :] = jnp.where(i > 0, halo_ref[...], 0.0)
    buf[pl.ds(HALO, tr), :] = x_ref[...]


def _conv_prev(buf, w_ref, kw, tr):
    acc = w_ref[kw - 1:kw, :] * buf[pl.ds(HALO, tr), :]
    for k in range(kw - 1):
        acc = acc + w_ref[k:k + 1, :] * buf[pl.ds(HALO - kw + 1 + k, tr), :]
    return acc


def _conv_dw(buf, dc, kw, tr):
    rows = [jnp.sum(dc * buf[pl.ds(HALO - kw + 1 + k, tr), :], axis=0, keepdims=True) for k in range(kw)]
    return jnp.concatenate(rows, axis=0)


def _conv_next(buf, dc_ref, halo_ref, w_ref, kw, i, n_tiles, tr):
    buf[pl.ds(0, tr), :] = dc_ref[...]
    buf[pl.ds(tr, HALO), :] = jnp.where(i < n_tiles - 1, halo_ref[...], 0.0)
    acc = w_ref[kw - 1:kw, :] * buf[pl.ds(0, tr), :]
    for k in range(kw - 1):
        acc = acc + w_ref[k:k + 1, :] * buf[pl.ds(kw - 1 - k, tr), :]
    return acc


def _split3(x):
    x1 = x.astype(BF16)
    r1 = x - x1.astype(F32)
    x2 = r1.astype(BF16)
    x3 = (r1 - x2.astype(F32)).astype(BF16)
    return x1, x2, x3


def _dot3(parts, m, left):
    tot = None
    for p in parts:
        r = jnp.dot(m, p, preferred_element_type=F32) if left else jnp.dot(p, m, preferred_element_type=F32)
        tot = r if tot is None else tot + r
    return tot


def _ssd_prep_fwd(proj, conv_w, conv_b, dt_bias, expand, *, name):
    t = proj.shape[0]
    tr = _tile(t, 128)
    nt = t // tr
    hb = tr // HALO
    cw = SSD_CONV_DIM
    cb_x = OFF_XBC // cw
    cb_dt = OFF_DT // LANES

    def body(x_ref, halo_ref, dtr_ref, w_ref, b_ref, dtb_ref, e_ref, xs_ref, bm_ref, cm_ref, dtx_ref, buf):
        i = pl.program_id(0)
        _fill_prev(buf, x_ref, halo_ref, i, tr)
        conv = _conv_prev(buf, w_ref, SSD_CONV, tr) + b_ref[...]
        rows = _row_ids(i, tr, (tr, 1))
        live = rows >= PAD
        act = jnp.where(live, conv * _sigmoid(conv), 0.0)
        xs_ref[...] = act[:, :SSD_INNER]
        bm_ref[...] = act[:, SSD_INNER:SSD_INNER + GROUP_W]
        cm_ref[...] = act[:, SSD_INNER + GROUP_W:]
        dt = jnp.where(live, jax.nn.softplus(dtr_ref[...] + dtb_ref[...]), 0.0)
        dtx_ref[...] = _dot3(_split3(dt), e_ref[...], left=False)

    return pl.pallas_call(
        body, name=name, grid=(nt,),
        in_specs=[pl.BlockSpec((tr, cw), lambda i: (i, cb_x)),
                  pl.BlockSpec((HALO, cw), lambda i: (jnp.maximum(i * hb - 1, 0), cb_x)),
                  pl.BlockSpec((tr, LANES), lambda i: (i, cb_dt)),
                  pl.BlockSpec((SSD_CONV, cw), lambda i: (0, 0)),
                  pl.BlockSpec((1, cw), lambda i: (0, 0)),
                  pl.BlockSpec((1, LANES), lambda i: (0, 0)),
                  pl.BlockSpec((LANES, SSD_INNER), lambda i: (0, 0))],
        out_specs=[pl.BlockSpec((tr, SSD_INNER), lambda i: (i, 0)), pl.BlockSpec((tr, GROUP_W), lambda i: (i, 0)),
                   pl.BlockSpec((tr, GROUP_W), lambda i: (i, 0)), pl.BlockSpec((tr, SSD_INNER), lambda i: (i, 0))],
        out_shape=[_sds((t, SSD_INNER), F32), _sds((t, GROUP_W), F32), _sds((t, GROUP_W), F32),
                   _sds((t, SSD_INNER), F32)],
        scratch_shapes=[pltpu.VMEM((tr + HALO, cw), F32)],
        compiler_params=_cp("parallel"))(proj, proj, proj, conv_w, conv_b, dt_bias, expand)


def _ssd_prep_bwd_a(proj, dxs, dbm, dcm, ddtx, conv_w, conv_b, dt_bias, reduce_m, *, name):
    t = proj.shape[0]
    tr = _tile(t, 128)
    nt = t // tr
    hb = tr // HALO
    cw = SSD_CONV_DIM
    cb_x = OFF_XBC // cw
    cb_dt = OFF_DT // LANES

    def body(x_ref, halo_ref, dtr_ref, dxs_ref, dbm_ref, dcm_ref, ddtx_ref, w_ref, b_ref, dtb_ref, r_ref,
             dconv_ref, ddtr_ref, dw_ref, db_ref, ddtb_ref, buf):
        i = pl.program_id(0)
        _fill_prev(buf, x_ref, halo_ref, i, tr)
        conv = _conv_prev(buf, w_ref, SSD_CONV, tr) + b_ref[...]
        rows = _row_ids(i, tr, (tr, 1))
        live = rows >= PAD
        sg = _sigmoid(conv)
        dact = jnp.concatenate([dxs_ref[...], dbm_ref[...], dcm_ref[...]], axis=1)
        dconv = jnp.where(live, dact * (sg * (1.0 + conv * (1.0 - sg))), 0.0)
        dconv_ref[...] = dconv
        pw = _conv_dw(buf, dconv, SSD_CONV, tr)
        pb = jnp.sum(dconv, axis=0, keepdims=True)
        ddt = _dot3(_split3(ddtx_ref[...]), r_ref[...], left=False)
        ddtr = jnp.where(live, ddt * _sigmoid(dtr_ref[...] + dtb_ref[...]), 0.0)
        ddtr_ref[...] = ddtr.astype(BF16)
        pdb = jnp.sum(ddtr, axis=0, keepdims=True)

        @pl.when(i == 0)
        def _():
            dw_ref[...] = pw
            db_ref[...] = pb
            ddtb_ref[...] = pdb

        @pl.when(i > 0)
        def _():
            dw_ref[...] += pw
            db_ref[...] += pb
            ddtb_ref[...] += pdb

    return pl.pallas_call(
        body, name=name, grid=(nt,),
        in_specs=[pl.BlockSpec((tr, cw), lambda i: (i, cb_x)),
                  pl.BlockSpec((HALO, cw), lambda i: (jnp.maximum(i * hb - 1, 0), cb_x)),
                  pl.BlockSpec((tr, LANES), lambda i: (i, cb_dt)),
                  pl.BlockSpec((tr, SSD_INNER), lambda i: (i, 0)),
                  pl.BlockSpec((tr, GROUP_W), lambda i: (i, 0)),
                  pl.BlockSpec((tr, GROUP_W), lambda i: (i, 0)),
                  pl.BlockSpec((tr, SSD_INNER), lambda i: (i, 0)),
                  pl.BlockSpec((SSD_CONV, cw), lambda i: (0, 0)),
                  pl.BlockSpec((1, cw), lambda i: (0, 0)),
                  pl.BlockSpec((1, LANES), lambda i: (0, 0)),
                  pl.BlockSpec((SSD_INNER, LANES), lambda i: (0, 0))],
        out_specs=[pl.BlockSpec((tr, cw), lambda i: (i, 0)), pl.BlockSpec((tr, LANES), lambda i: (i, 0)),
                   pl.BlockSpec((SSD_CONV, cw), lambda i: (0, 0)), pl.BlockSpec((1, cw), lambda i: (0, 0)),
                   pl.BlockSpec((1, LANES), lambda i: (0, 0))],
        out_shape=[_sds((t, cw), F32), _sds((t, LANES), BF16), _sds((SSD_CONV, cw), F32), _sds((1, cw), F32),
                   _sds((1, LANES), F32)],
        scratch_shapes=[pltpu.VMEM((tr + HALO, cw), F32)],
        compiler_params=_cp("arbitrary"))(proj, proj, proj, dxs, dbm, dcm, ddtx, conv_w, conv_b, dt_bias, reduce_m)


def _conv_bwd_input(dconv, w, kw, *, name, out_dtype=BF16, tc=None):
    t, c = dconv.shape
    tr = _tile(t, 128)
    nt = t // tr
    hb = tr // HALO
    tc = _tile(c, tc or c)
    last_hb = t // HALO - 1

    def body(dc_ref, halo_ref, w_ref, o_ref, buf):
        i = pl.program_id(0)
        o_ref[...] = _conv_next(buf, dc_ref, halo_ref, w_ref, kw, i, nt, tr).astype(out_dtype)

    return pl.pallas_call(
        body, name=name, grid=(nt, c // tc),
        in_specs=[pl.BlockSpec((tr, tc), lambda i, j: (i, j)),
                  pl.BlockSpec((HALO, tc), lambda i, j: (jnp.minimum((i + 1) * hb, last_hb), j)),
                  pl.BlockSpec((kw, tc), lambda i, j: (0, j))],
        out_specs=pl.BlockSpec((tr, tc), lambda i, j: (i, j)), out_shape=_sds((t, c), out_dtype),
        scratch_shapes=[pltpu.VMEM((tr + HALO, tc), F32)],
        compiler_params=_cp("parallel", "parallel"))(dconv, dconv, w)


def _ffn_act_fwd(ug, uv, wg, wv, bg, bv, *, name):
    t, c = ug.shape
    tr = _tile(t, 128)
    hb = tr // HALO
    tc = _tile(c, 1408)

    def body(ug_ref, hg_ref, uv_ref, hv_ref, wg_ref, wv_ref, bg_ref, bv_ref, o_ref, bufg, bufv):
        i = pl.program_id(0)
        _fill_prev(bufg, ug_ref, hg_ref, i, tr)
        _fill_prev(bufv, uv_ref, hv_ref, i, tr)
        cg = _conv_prev(bufg, wg_ref, FFN_CONV, tr) + bg_ref[...]
        cv = _conv_prev(bufv, wv_ref, FFN_CONV, tr) + bv_ref[...]
        o_ref[...] = (cg * _sigmoid(cg) * cv).astype(BF16)

    blk = pl.BlockSpec((tr, tc), lambda i, j: (i, j))
    halo = pl.BlockSpec((HALO, tc), lambda i, j: (jnp.maximum(i * hb - 1, 0), j))
    wsp = pl.BlockSpec((FFN_CONV, tc), lambda i, j: (0, j))
    bsp = pl.BlockSpec((1, tc), lambda i, j: (0, j))
    return pl.pallas_call(
        body, name=name, grid=(t // tr, c // tc), in_specs=[blk, halo, blk, halo, wsp, wsp, bsp, bsp],
        out_specs=blk, out_shape=_sds((t, c), BF16),
        scratch_shapes=[pltpu.VMEM((tr + HALO, tc), F32), pltpu.VMEM((tr + HALO, tc), F32)],
        compiler_params=_cp("parallel", "parallel"))(ug, ug, uv, uv, wg, wv, bg, bv)


def _ffn_act_bwd(ug, uv, dact, wg, wv, bg, bv, *, name):
    t, c = ug.shape
    tr = _tile(t, 128)
    hb = tr // HALO
    tc = _tile(c, 1408)

    def body(ug_ref, hg_ref, uv_ref, hv_ref, da_ref, wg_ref, wv_ref, bg_ref, bv_ref,
             dcg_ref, dcv_ref, dwg_ref, dwv_ref, dbg_ref, dbv_ref, bufg, bufv):
        i = pl.program_id(1)
        _fill_prev(bufg, ug_ref, hg_ref, i, tr)
        _fill_prev(bufv, uv_ref, hv_ref, i, tr)
        cg = _conv_prev(bufg, wg_ref, FFN_CONV, tr) + bg_ref[...]
        cv = _conv_prev(bufv, wv_ref, FFN_CONV, tr) + bv_ref[...]
        sg = _sigmoid(cg)
        da = da_ref[...]
        dcg = da * cv * (sg * (1.0 + cg * (1.0 - sg)))
        dcv = da * (cg * sg)
        dcg_ref[...] = dcg
        dcv_ref[...] = dcv
        pwg = _conv_dw(bufg, dcg, FFN_CONV, tr)
        pwv = _conv_dw(bufv, dcv, FFN_CONV, tr)
        pbg = jnp.sum(dcg, axis=0, keepdims=True)
        pbv = jnp.sum(dcv, axis=0, keepdims=True)

        @pl.when(i == 0)
        def _():
            dwg_ref[...] = pwg
            dwv_ref[...] = pwv
            dbg_ref[...] = pbg
            dbv_ref[...] = pbv

        @pl.when(i > 0)
        def _():
            dwg_ref[...] += pwg
            dwv_ref[...] += pwv
            dbg_ref[...] += pbg
            dbv_ref[...] += pbv

    blk = pl.BlockSpec((tr, tc), lambda j, i: (i, j))
    halo = pl.BlockSpec((HALO, tc), lambda j, i: (jnp.maximum(i * hb - 1, 0), j))
    wsp = pl.BlockSpec((FFN_CONV, tc), lambda j, i: (0, j))
    bsp = pl.BlockSpec((1, tc), lambda j, i: (0, j))
    return pl.pallas_call(
        body, name=name, grid=(c // tc, t // tr), in_specs=[blk, halo, blk, halo, blk, wsp, wsp, bsp, bsp],
        out_specs=[blk, blk, wsp, wsp, bsp, bsp],
        out_shape=[_sds((t, c), F32), _sds((t, c), F32), _sds((FFN_CONV, c), F32), _sds((FFN_CONV, c), F32),
                   _sds((1, c), F32), _sds((1, c), F32)],
        scratch_shapes=[pltpu.VMEM((tr + HALO, tc), F32), pltpu.VMEM((tr + HALO, tc), F32)],
        compiler_params=_cp("parallel", "arbitrary"))(ug, ug, uv, uv, dact, wg, wv, bg, bv)


def _tri(lower):
    li = lax.broadcasted_iota(jnp.int32, (CHUNK, CHUNK), 0)
    si = lax.broadcasted_iota(jnp.int32, (CHUNK, CHUNK), 1)
    return li >= si if lower else li <= si


def _tri_ones(lower):
    return jnp.where(_tri(lower), 1.0, 0.0).astype(BF16)


def _decay_pair(acs, acs_t, lane0):
    col = acs[:, lane0:lane0 + 1]
    row = acs_t[lane0:lane0 + 1, :]
    low = jnp.where(_tri(True), jnp.exp(jnp.minimum(col - row, 0.0)), 0.0)
    upp = jnp.where(_tri(False), jnp.exp(jnp.minimum(row - col, 0.0)), 0.0)
    return low, upp


def _ssd_fwd(xs, dtx, bm, cm, bm_t, a_x, d_x, *, name):
    t = xs.shape[0]
    nc = t // CHUNK
    gw = GROUP_W

    def body(xs_ref, dt_ref, b_ref, c_ref, bt_ref, a_ref, d_ref, y_ref, prev_ref, h_s):
        @pl.when(pl.program_id(1) == 0)
        def _():
            h_s[...] = jnp.zeros((SSD_STATE, gw), F32)

        x = xs_ref[...]
        dt = dt_ref[...]
        acs = _dot3(_split3(dt * a_ref[...]), _tri_ones(True), left=True)
        acs_t = acs.T
        xc = x * dt
        bv = b_ref[...].astype(BF16)
        cv = c_ref[...].astype(BF16)
        cb = lax.dot_general(cv, bv, NT_DIMS, preferred_element_type=F32)
        lane = lax.broadcasted_iota(jnp.int32, (CHUNK, LANES), 1)
        pieces = []
        for pp in range(gw // LANES):
            xcp = xc[:, pp * LANES:(pp + 1) * LANES]
            acc = jnp.zeros((CHUNK, LANES), F32)
            for e in range(2):
                low, _ = _decay_pair(acs, acs_t, pp * LANES + e * SSD_HEAD_DIM)
                mine = (lane >= e * SSD_HEAD_DIM) & (lane < (e + 1) * SSD_HEAD_DIM)
                xm = jnp.where(mine, xcp, 0.0).astype(BF16)
                acc = acc + jnp.dot((cb * low).astype(BF16), xm, preferred_element_type=F32)
            pieces.append(acc)
        y_diag = jnp.concatenate(pieces, axis=1)
        h_prev = h_s[...]
        y_off = jnp.dot(cv, h_prev.astype(BF16), preferred_element_type=F32) * jnp.exp(acs)
        y_ref[...] = y_diag + y_off + d_ref[...] * x
        prev_ref[0] = h_prev
        last = acs[CHUNK - 1:CHUNK, :]
        w = jnp.exp(last - acs)
        st = jnp.dot(bt_ref[...].astype(BF16), (xc * w).astype(BF16), preferred_element_type=F32)
        h_s[...] = h_prev * jnp.exp(last) + st

    tok = pl.BlockSpec((CHUNK, gw), lambda g, c: (c, g))
    grp = pl.BlockSpec((CHUNK, SSD_STATE), lambda g, c: (c, g))
    vec = pl.BlockSpec((1, gw), lambda g, c: (0, g))
    return pl.pallas_call(
        body, name=name, grid=(SSD_GROUPS, nc),
        in_specs=[tok, tok, grp, grp, pl.BlockSpec((SSD_STATE, CHUNK), lambda g, c: (g, c)), vec, vec],
        out_specs=[tok, pl.BlockSpec((1, SSD_STATE, gw), lambda g, c: (c, 0, g))],
        out_shape=[_sds((t, SSD_INNER), F32), _sds((nc, SSD_STATE, SSD_INNER), F32)],
        scratch_shapes=[pltpu.VMEM((SSD_STATE, gw), F32)],
        compiler_params=_cp("parallel", "arbitrary"))(xs, dtx, bm, cm, bm_t, a_x, d_x)


def _ssd_bwd(xs, dtx, bm, cm, cm_t, prev, dy, a_x, d_x, *, name):
    t = xs.shape[0]
    nc = t // CHUNK
    gw = GROUP_W

    def body(xs_ref, dt_ref, b_ref, c_ref, ct_ref, prev_ref, dy_ref, a_ref, d_ref,
             dxs_ref, ddt_ref, db_ref, dc_ref, da_ref, dd_ref, g_s):
        first = pl.program_id(1) == 0

        @pl.when(first)
        def _():
            g_s[...] = jnp.zeros((SSD_STATE, gw), F32)

        x = xs_ref[...]
        dt = dt_ref[...]
        a = a_ref[...]
        dyv = dy_ref[...]
        acs = _dot3(_split3(dt * a), _tri_ones(True), left=True)
        acs_t = acs.T
        xc = x * dt
        bv = b_ref[...].astype(BF16)
        cv = c_ref[...].astype(BF16)
        cb = lax.dot_general(cv, bv, NT_DIMS, preferred_element_type=F32)
        cb_t = lax.dot_general(bv, cv, NT_DIMS, preferred_element_type=F32)
        last = acs[CHUNK - 1:CHUNK, :]
        w = jnp.exp(last - acs)
        cd = jnp.exp(last)
        p_in = prev_ref[0]
        p_b = p_in.astype(BF16)
        g_out = g_s[...]
        g_b = g_out.astype(BF16)
        dy_e = dyv * jnp.exp(acs)
        dy_eb = dy_e.astype(BF16)
        y_off_raw = jnp.dot(cv, p_b, preferred_element_type=F32)
        dacs = dy_e * y_off_raw
        d_c = lax.dot_general(dy_eb, p_b, NT_DIMS, preferred_element_type=F32)
        d_prev = jnp.dot(ct_ref[...].astype(BF16), dy_eb, preferred_element_type=F32)
        q_l = jnp.dot(bv, g_b, preferred_element_type=F32)
        dxc = w * q_l
        tw = xc * q_l * w
        dacs = dacs - tw
        d_b = lax.dot_general((xc * w).astype(BF16), g_b, NT_DIMS, preferred_element_type=F32)
        last_add = jnp.sum(tw, axis=0, keepdims=True) + cd * jnp.sum(g_out * p_in, axis=0, keepdims=True)
        g_s[...] = cd * g_out + d_prev
        lane = lax.broadcasted_iota(jnp.int32, (CHUNK, LANES), 1)
        d_cb = jnp.zeros((CHUNK, CHUNK), F32)
        d_cb_t = jnp.zeros((CHUNK, CHUNK), F32)
        dxc_pieces, dacs_pieces = [], []
        for pp in range(gw // LANES):
            xcp = xc[:, pp * LANES:(pp + 1) * LANES]
            dyp = dyv[:, pp * LANES:(pp + 1) * LANES]
            dxcp = jnp.zeros((CHUNK, LANES), F32)
            dacsp = jnp.zeros((CHUNK, LANES), F32)
            for e in range(2):
                low, upp = _decay_pair(acs, acs_t, pp * LANES + e * SSD_HEAD_DIM)
                mine = (lane >= e * SSD_HEAD_DIM) & (lane < (e + 1) * SSD_HEAD_DIM)
                m_low = cb * low
                m_upp = cb_t * upp
                dym = jnp.where(mine, dyp, 0.0).astype(BF16)
                xm = jnp.where(mine, xcp, 0.0).astype(BF16)
                dxcp = dxcp + jnp.dot(m_upp.astype(BF16), dym, preferred_element_type=F32)
                d_m = lax.dot_general(dym, xm, NT_DIMS, preferred_element_type=F32)
                d_m_t = lax.dot_general(xm, dym, NT_DIMS, preferred_element_type=F32)
                rs = jnp.sum(d_m * m_low, axis=1, keepdims=True)
                cs = jnp.sum(d_m_t * m_upp, axis=1, keepdims=True)
                dacsp = dacsp + jnp.where(lane == e * SSD_HEAD_DIM, rs - cs, 0.0)
                d_cb = d_cb + d_m * low
                d_cb_t = d_cb_t + d_m_t * upp
            dxc_pieces.append(dxcp)
            dacs_pieces.append(dacsp)
        dxc = dxc + jnp.concatenate(dxc_pieces, axis=1)
        dacs = dacs + jnp.concatenate(dacs_pieces, axis=1)
        rowi = lax.broadcasted_iota(jnp.int32, (CHUNK, gw), 0)
        dacs = dacs + jnp.where(rowi == CHUNK - 1, last_add, 0.0)
        dc_ref[...] = d_c + jnp.dot(d_cb.astype(BF16), bv, preferred_element_type=F32)
        db_ref[...] = d_b + jnp.dot(d_cb_t.astype(BF16), cv, preferred_element_type=F32)
        dda = _dot3(_split3(dacs), _tri_ones(False), left=True)
        ddt_ref[...] = dda * a + dxc * x
        dxs_ref[...] = dxc * dt + d_ref[...] * dyv
        pa = jnp.sum(dda * dt, axis=0, keepdims=True)
        pd = jnp.sum(dyv * x, axis=0, keepdims=True)

        @pl.when(first)
        def _():
            da_ref[...] = pa
            dd_ref[...] = pd

        @pl.when(jnp.logical_not(first))
        def _():
            da_ref[...] += pa
            dd_ref[...] += pd

    rc = lambda c: nc - 1 - c
    tok = pl.BlockSpec((CHUNK, gw), lambda g, c: (rc(c), g))
    grp = pl.BlockSpec((CHUNK, SSD_STATE), lambda g, c: (rc(c), g))
    vec = pl.BlockSpec((1, gw), lambda g, c: (0, g))
    return pl.pallas_call(
        body, name=name, grid=(SSD_GROUPS, nc),
        in_specs=[tok, tok, grp, grp, pl.BlockSpec((SSD_STATE, CHUNK), lambda g, c: (g, rc(c))),
                  pl.BlockSpec((1, SSD_STATE, gw), lambda g, c: (rc(c), 0, g)), tok, vec, vec],
        out_specs=[tok, tok, grp, grp, vec, vec],
        out_shape=[_sds((t, SSD_INNER), F32), _sds((t, SSD_INNER), F32), _sds((t, gw), F32), _sds((t, gw), F32),
                   _sds((1, SSD_INNER), F32), _sds((1, SSD_INNER), F32)],
        scratch_shapes=[pltpu.VMEM((SSD_STATE, gw), F32)],
        compiler_params=_cp("parallel", "arbitrary"))(xs, dtx, bm, cm, cm_t, prev, dy, a_x, d_x)


def _gnorm_fwd(y, proj, g, *, name):
    t = y.shape[0]
    tr = _tile(t, 640)
    zb = OFF_Z // GROUP_W

    def body(y_ref, z_ref, g_ref, o_ref):
        z = z_ref[...]
        v = y_ref[...] * (z * _sigmoid(z))
        r = lax.rsqrt(jnp.mean(v * v, axis=1, keepdims=True) + RMS_EPS)
        o_ref[...] = (v * r * g_ref[...]).astype(BF16)

    blk = pl.BlockSpec((tr, GROUP_W), lambda i, j: (i, j))
    return pl.pallas_call(
        body, name=name, grid=(t // tr, SSD_GROUPS),
        in_specs=[blk, pl.BlockSpec((tr, GROUP_W), lambda i, j: (i, zb + j)),
                  pl.BlockSpec((1, GROUP_W), lambda i, j: (0, j))],
        out_specs=blk, out_shape=_sds((t, SSD_INNER), BF16),
        compiler_params=_cp("parallel", "parallel"))(y, proj, g)


def _gnorm_bwd(dout, y, proj, g, *, name):
    t = y.shape[0]
    tr = _tile(t, 640)
    zb = OFF_Z // GROUP_W

    def body(do_ref, y_ref, z_ref, g_ref, dy_ref, dz_ref, dg_ref):
        i = pl.program_id(1)
        z = z_ref[...]
        yv = y_ref[...]
        sg = _sigmoid(z)
        sz = z * sg
        v = yv * sz
        r = lax.rsqrt(jnp.mean(v * v, axis=1, keepdims=True) + RMS_EPS)
        dov = do_ref[...].astype(F32)
        gdo = dov * g_ref[...]
        m = jnp.mean(v * gdo, axis=1, keepdims=True)
        dv = r * gdo - v * (r * r * r) * m
        dy_ref[...] = dv * sz
        dz_ref[...] = (dv * yv * (sg * (1.0 + z * (1.0 - sg)))).astype(BF16)
        pg = jnp.sum(dov * v * r, axis=0, keepdims=True)

        @pl.when(i == 0)
        def _():
            dg_ref[...] = pg

        @pl.when(i > 0)
        def _():
            dg_ref[...] += pg

    blk = pl.BlockSpec((tr, GROUP_W), lambda j, i: (i, j))
    vec = pl.BlockSpec((1, GROUP_W), lambda j, i: (0, j))
    return pl.pallas_call(
        body, name=name, grid=(SSD_GROUPS, t // tr),
        in_specs=[blk, blk, pl.BlockSpec((tr, GROUP_W), lambda j, i: (i, zb + j)), vec],
        out_specs=[blk, blk, vec],
        out_shape=[_sds((t, SSD_INNER), F32), _sds((t, SSD_INNER), BF16), _sds((1, SSD_INNER), F32)],
        compiler_params=_cp("parallel", "arbitrary"))(dout, y, proj, g)


def _mix_fwd(proj, ya, ys, *, name):
    t, d = ya.shape
    tr = _tile(t, 640)
    ba, bs = OFF_GA // d, OFF_GS // d

    def body(ga_ref, gs_ref, ya_ref, ys_ref, o_ref):
        o_ref[...] = (_sigmoid(ga_ref[...]) * ya_ref[...] + _sigmoid(gs_ref[...]) * ys_ref[...]).astype(BF16)

    blk = pl.BlockSpec((tr, d), lambda i: (i, 0))
    return pl.pallas_call(
        body, name=name, grid=(t // tr,),
        in_specs=[pl.BlockSpec((tr, d), lambda i: (i, ba)), pl.BlockSpec((tr, d), lambda i: (i, bs)), blk, blk],
        out_specs=blk, out_shape=_sds((t, d), BF16), compiler_params=_cp("parallel"))(proj, proj, ya, ys)


def _mix_bwd(dmix, proj, ya, ys, *, name):
    t, d = ya.shape
    tr = _tile(t, 640)
    ba, bs = OFF_GA // d, OFF_GS // d

    def body(dm_ref, ga_ref, gs_ref, ya_ref, ys_ref, dya_ref, dys_ref, dga_ref, dgs_ref):
        dm = dm_ref[...]
        sa = _sigmoid(ga_ref[...])
        ss = _sigmoid(gs_ref[...])
        dya_ref[...] = (sa * dm).astype(BF16)
        dys_ref[...] = (ss * dm).astype(BF16)
        dga_ref[...] = (dm * ya_ref[...] * sa * (1.0 - sa)).astype(BF16)
        dgs_ref[...] = (dm * ys_ref[...] * ss * (1.0 - ss)).astype(BF16)

    blk = pl.BlockSpec((tr, d), lambda i: (i, 0))
    return pl.pallas_call(
        body, name=name, grid=(t // tr,),
        in_specs=[blk, pl.BlockSpec((tr, d), lambda i: (i, ba)), pl.BlockSpec((tr, d), lambda i: (i, bs)), blk, blk],
        out_specs=[blk] * 4, out_shape=[_sds((t, d), BF16)] * 4,
        compiler_params=_cp("parallel"))(dmix, proj, proj, ya, ys)


def _loss_grad(h, target, *, name):
    t, d = h.shape
    tr = LANES
    assert ROW0 == tr

    def body(h_ref, t_ref, dh_ref, loss_ref):
        i = pl.program_id(0)

        @pl.when(i == 0)
        def _():
            dh_ref[...] = jnp.zeros((tr, d), F32)
            loss_ref[...] = jnp.zeros((1, LANES), F32)

        @pl.when(i > 0)
        def _():
            err = h_ref[...] - t_ref[...]
            dh_ref[...] = err * (1.0 / d)
            part = jnp.sum(jnp.sum(err * err, axis=1, keepdims=True), axis=0, keepdims=True)
            loss_ref[...] += jnp.broadcast_to(part * (0.5 / d), (1, LANES))

    blk = pl.BlockSpec((tr, d), lambda i: (i, 0))
    return pl.pallas_call(
        body, name=name, grid=(t // tr,),
        in_specs=[blk, pl.BlockSpec((tr, d), lambda i: (jnp.maximum(i - 1, 0), 0))],
        out_specs=[blk, pl.BlockSpec((1, LANES), lambda i: (0, 0))],
        out_shape=[_sds((t, d), F32), _sds((1, LANES), F32)],
        compiler_params=_cp("arbitrary"))(h, target)


def _adamw_update(gv, wv, mv, vv):
    c1 = 1.0 - ADAM_B1 ** ADAM_STEP
    c2 = 1.0 - ADAM_B2 ** ADAM_STEP
    nm = ADAM_B1 * mv + (1.0 - ADAM_B1) * gv
    nv = ADAM_B2 * vv + (1.0 - ADAM_B2) * (gv * gv)
    return -ADAM_LR * ((nm / c1) / (jnp.sqrt(nv / c2) + ADAM_EPS) + ADAM_WD * wv), nm, nv


def _as_2d(a):
    return a.reshape(1, -1) if a.ndim == 1 else a.reshape(-1, a.shape[-1])


def _adamw(g, w, m, v, *, name):
    shape = w.shape
    g2, w2, m2, v2 = (_as_2d(a) for a in (g, w, m, v))
    r, c = w2.shape
    tr = _tile(r, 256, base=8)

    def body(g_ref, w_ref, m_ref, v_ref, d_ref, nm_ref, nv_ref):
        d_ref[...], nm_ref[...], nv_ref[...] = _adamw_update(g_ref[...], w_ref[...], m_ref[...], v_ref[...])

    blk = pl.BlockSpec((tr, c), lambda i: (i, 0))
    outs = pl.pallas_call(body, name=name, grid=(r // tr,), in_specs=[blk] * 4, out_specs=[blk] * 3,
                          out_shape=[_sds((r, c), F32)] * 3, compiler_params=_cp("parallel"))(g2, w2, m2, v2)
    return [o.reshape(shape) for o in outs]


def _adamw_small(items, *, name):
    n = len(items)
    shapes = [it[1].shape for it in items]
    flat = [_as_2d(a) for it in items for a in it]

    def body(*refs):
        ins, outs = refs[:4 * n], refs[4 * n:]
        for k in range(n):
            g_ref, w_ref, m_ref, v_ref = ins[4 * k:4 * k + 4]
            d_ref, nm_ref, nv_ref = outs[3 * k:3 * k + 3]
            d_ref[...], nm_ref[...], nv_ref[...] = _adamw_update(g_ref[...], w_ref[...], m_ref[...], v_ref[...])

    out_shape = [_sds(flat[4 * k + 1].shape, F32) for k in range(n) for _ in range(3)]
    outs = pl.pallas_call(body, name=name, out_shape=out_shape,
                          compiler_params=pltpu.CompilerParams(vmem_limit_bytes=VMEM_LIMIT_BYTES))(*flat)
    return [[outs[3 * k + q].reshape(shapes[k]) for q in range(3)] for k in range(n)]


def _pair_add(g5, got, core, *, name):
    n, _, r, _ = g5.shape
    tr = _tile(r, 1024, base=8)

    def body(c_ref, a_ref, b_ref, o_ref):
        o_ref[...] = a_ref[0] + b_ref[...]

    grid_spec = pltpu.PrefetchScalarGridSpec(
        num_scalar_prefetch=1, grid=(n, r // tr),
        in_specs=[pl.BlockSpec((1, 1, tr, LANES), lambda s, i, c_ref: (s, c_ref[0], i, 0)),
                  pl.BlockSpec((1, tr, LANES), lambda s, i, c_ref: (s, i, 0))],
        out_specs=pl.BlockSpec((1, tr, LANES), lambda s, i, c_ref: (s, i, 0)))
    return pl.pallas_call(body, name=name, grid_spec=grid_spec, out_shape=_sds(got.shape, F32),
                          compiler_params=_cp("parallel", "parallel"))(core, g5, got)


def _sum_chips(q, core, *, name):
    n, r, _ = q.shape
    tr = _tile(r, 1024, base=8)

    def body(c_ref, q_ref, o_ref):
        tot = q_ref[0]
        for s in range(1, n):
            tot = tot + q_ref[s]
        o_ref[0] = tot

    grid_spec = pltpu.PrefetchScalarGridSpec(
        num_scalar_prefetch=1, grid=(r // tr,),
        in_specs=[pl.BlockSpec((n, tr, LANES), lambda i, c_ref: (0, i, 0))],
        out_specs=pl.BlockSpec((1, tr, LANES), lambda i, c_ref: (c_ref[0], i, 0)))
    return pl.pallas_call(body, name=name, grid_spec=grid_spec, out_shape=_sds((2, r, LANES), F32),
                          compiler_params=_cp("parallel"))(core, q)


_ANY = pl.BlockSpec(memory_space=pl.ANY)
_MESH = pl.DeviceIdType.MESH


def _place():
    x, y, c = lax.axis_index("x"), lax.axis_index("y"), lax.axis_index("c")
    return x, y, c, [(1 - x, y), (x, 1 - y), (1 - x, 1 - y)]


def _chip_allgather(mine, *, name):
    na = len(mine)

    def body(*refs):
        for phase in range(3):
            _chip_allgather_phase(phase, refs[:na], refs[na:2 * na], *refs[2 * na:])

    return pl.pallas_call(
        body, name=name, in_specs=[_ANY] * na, out_specs=[_ANY] * na,
        out_shape=_chip_allgather_shapes(mine), scratch_shapes=_chip_allgather_sems(na))(*mine)


def _chip_allgather_shapes(mine):
    return [_sds((N_CHIPS,) + a.shape, a.dtype) for a in mine]


def _chip_allgather_sems(na):
    return [pltpu.SemaphoreType.DMA((6 * na,)), pltpu.SemaphoreType.DMA((6 * na,)), pltpu.SemaphoreType.DMA((na,))]


def _chip_allgather_phase(phase, x_refs, o_refs, send_sems, recv_sems, local_sems):
    na = len(x_refs)
    x, y, c, chips = _place()
    k = 2 * x + y

    def copy(a, n, src, dst, to):
        return pltpu.make_async_remote_copy(src_ref=src, dst_ref=dst, send_sem=send_sems.at[6 * a + n],
                                            recv_sem=recv_sems.at[6 * a + n], device_id=to, device_id_type=_MESH)

    locals_ = [pltpu.make_async_copy(x_refs[a], o_refs[a].at[k], local_sems.at[a]) for a in range(na)]
    sends = [copy(a, n, x_refs[a].at[c], o_refs[a].at[k, c], (cx, cy, c))
             for a in range(na) for n, (cx, cy) in enumerate(chips)]
    landed = [(copy(a, n, o_refs[a].at[2 * cx + cy, c], o_refs[a].at[2 * cx + cy, c], (cx, cy, c)),
               copy(a, 3 + n, o_refs[a].at[2 * cx + cy, c], o_refs[a].at[2 * cx + cy, c], (x, y, 1 - c)))
              for a in range(na) for n, (cx, cy) in enumerate(chips)]
    if phase == 0:
        for cp in locals_ + sends:
            cp.start()
    elif phase == 1:
        for arrival, forward in landed:
            arrival.wait_recv()
            forward.start()
    else:
        for a in range(na):
            for n, (cx, cy) in enumerate(chips):
                slab = o_refs[a].at[2 * cx + cy, 1 - c]
                copy(a, 3 + n, slab, slab, (x, y, 1 - c)).wait_recv()
        for cp in sends + [forward for _, forward in landed]:
            cp.wait_send()
        for cp in locals_:
            cp.wait()


def _sibling_swap(g5, *, name):
    n, _, r, _ = g5.shape

    def body(x_ref, o_ref, send_sems, recv_sems):
        x, y, c, _ = _place()
        cps = [pltpu.make_async_remote_copy(src_ref=x_ref.at[s, 1 - c], dst_ref=o_ref.at[s], send_sem=send_sems.at[s],
                                            recv_sem=recv_sems.at[s], device_id=(x, y, 1 - c), device_id_type=_MESH)
               for s in range(n)]
        for cp in cps:
            cp.start()
        for cp in cps:
            cp.wait()

    return pl.pallas_call(
        body, name=name, in_specs=[_ANY], out_specs=_ANY, out_shape=_sds((n, r, LANES), g5.dtype),
        scratch_shapes=[pltpu.SemaphoreType.DMA((n,)), pltpu.SemaphoreType.DMA((n,))])(g5)


_CHIP_EXCHANGE_SEMS = [pltpu.SemaphoreType.DMA((3,)), pltpu.SemaphoreType.DMA((3,)), pltpu.SemaphoreType.DMA]


def _chip_exchange_copies(h_ref, q_ref, send_sems, recv_sems, local_sem):
    x, y, c, chips = _place()
    k = 2 * x + y
    local = pltpu.make_async_copy(h_ref.at[k], q_ref.at[k], local_sem)
    sends, arrivals = [], []
    for n, (cx, cy) in enumerate(chips):
        kk = 2 * cx + cy
        mk = functools.partial(pltpu.make_async_remote_copy, src_ref=h_ref.at[kk], send_sem=send_sems.at[n],
                               recv_sem=recv_sems.at[n], device_id=(cx, cy, c), device_id_type=_MESH)
        sends.append(mk(dst_ref=q_ref.at[k]))
        arrivals.append(mk(dst_ref=q_ref.at[kk]))
    return local, sends, arrivals


def _chip_exchange_start(copies):
    local, sends, _ = copies
    local.start()
    for cp in sends:
        cp.start()


def _chip_exchange_wait(copies):
    local, sends, arrivals = copies
    for cp in arrivals:
        cp.wait_recv()
    for cp in sends:
        cp.wait_send()
    local.wait()


def _chip_exchange(h, *, name):
    def body(h_ref, q_ref, send_sems, recv_sems, local_sem):
        copies = _chip_exchange_copies(h_ref, q_ref, send_sems, recv_sems, local_sem)
        _chip_exchange_start(copies)
        _chip_exchange_wait(copies)

    return pl.pallas_call(body, name=name, in_specs=[_ANY], out_specs=_ANY, out_shape=_sds(h.shape, h.dtype),
                          scratch_shapes=_CHIP_EXCHANGE_SEMS)(h)


def _sibling_allgather(buf, *, name):
    def body(x_ref, o_ref, send_sem, recv_sem):
        x, y, c, _ = _place()
        cp = pltpu.make_async_remote_copy(src_ref=x_ref.at[c], dst_ref=o_ref.at[c], send_sem=send_sem,
                                          recv_sem=recv_sem, device_id=(x, y, 1 - c), device_id_type=_MESH)
        cp.start()
        pltpu.make_async_remote_copy(src_ref=x_ref.at[c], dst_ref=o_ref.at[1 - c], send_sem=send_sem,
                                     recv_sem=recv_sem, device_id=(x, y, 1 - c), device_id_type=_MESH).wait_recv()
        cp.wait_send()

    return pl.pallas_call(
        body, name=name, in_specs=[_ANY], out_specs=_ANY, out_shape=_sds(buf.shape, buf.dtype),
        input_output_aliases={0: 0},
        scratch_shapes=[pltpu.SemaphoreType.DMA, pltpu.SemaphoreType.DMA])(buf)


def _chip_segments(g, kind):
    if kind == "col":
        n = g.shape[-1] // N_CHIPS
        s = g.reshape(g.shape[:-1] + (N_CHIPS, n))
        return jnp.moveaxis(s, -2, 0).reshape(N_CHIPS, -1)
    k = g.shape[-2] // N_CHIPS
    s = g.reshape(g.shape[:-2] + (N_CHIPS, k, g.shape[-1]))
    return jnp.moveaxis(s, -3, 0).reshape(N_CHIPS, -1)


def _join(blocks, kind):
    if kind == "col":
        s = jnp.moveaxis(blocks, 0, -2)
        return s.reshape(s.shape[:-2] + (s.shape[-2] * s.shape[-1],))
    return blocks.reshape((blocks.shape[0] * blocks.shape[1],) + blocks.shape[2:])


def _travel_form(local, names):
    mine = []
    for n in names:
        a = local[n]
        if n == "w_in":
            a = jnp.swapaxes(a, 1, 2)
        if n == "meta_tokens":
            a = a.reshape(2, N_META // 2, a.shape[-1])
        mine.append(a.astype(BF16) if n in GATHER_BF16 else a)
    return mine


def _rope_tables(t):
    half = QK_ROPE // 2
    inv_freq = 1.0 / (ROPE_THETA ** (jnp.arange(0, QK_ROPE, 2, dtype=F32) / QK_ROPE))
    pos = jnp.maximum(jnp.arange(t, dtype=F32) - PAD, 0.0)
    ang = pos[:, None] * inv_freq[None, :]
    cos, sin = jnp.cos(ang), jnp.sin(ang)
    z = jnp.zeros((t, half), F32)
    z2 = jnp.zeros((t, LANES - QK_ROPE), F32)
    return (jnp.concatenate([cos, cos, z2], axis=1), jnp.concatenate([-sin, z, z2], axis=1),
            jnp.concatenate([z, sin, z2], axis=1))


def _expand_matrix():
    lane = np.arange(SSD_INNER) // SSD_HEAD_DIM
    e = (np.arange(LANES)[:, None] == lane[None, :]).astype(np.float32)
    return jnp.asarray(e, BF16)


def _late_weights(full, i):
    w = {}
    kinds = dict(SHARDED)
    whole = lambda n: _join(full[n][:, i], kinds[n])
    for n in ("w_o_attn", "w_o_ssd", "w_out", "w_down"):
        w[n] = whole(n)
    w_up = whole("w_up")
    w["w_up_g"] = w_up[:, :D_FF]
    w["w_up_v"] = w_up[:, D_FF:]
    ffn_w = whole("ffn_conv_w")
    w["ffn_conv_wg"] = ffn_w[:, :D_FF]
    w["ffn_conv_wv"] = ffn_w[:, D_FF:]
    return w


def _early_weights(full, rep, i):
    w = {}
    kinds = dict(SHARDED)
    whole = lambda n: _join(full[n][:, i], kinds[n])
    wt = _join(full["w_in"][:, i], "row")
    zr = lambda n: jnp.zeros((n, D_MODEL), BF16)
    w["w_in_t"] = jnp.concatenate(
        [wt[0:1024], wt[1088:3136], wt[3136:6208], wt[6240:7264], wt[7264:8288],
         wt[1024:1088], zr(LANES - QK_ROPE), wt[6208:6240], zr(LANES - SSD_HEADS)], axis=0)
    wq = whole("w_q_b").reshape(Q_LORA, MLA_HEADS, QK_NOPE + QK_ROPE)
    w["w_q_b"] = jnp.pad(wq, ((0, 0), (0, 0), (0, QHEAD - QK_NOPE - QK_ROPE))).reshape(Q_LORA, MLA_HEADS * QHEAD)
    wkv = whole("w_kv_b").reshape(KV_LORA, MLA_HEADS, 2, QK_NOPE)
    w["w_kv_b"] = jnp.swapaxes(wkv, 1, 2).reshape(KV_LORA, 2 * MLA_HEADS * QK_NOPE)
    w["ssd_conv_w"] = whole("ssd_conv_w")
    row = lambda v: v.reshape(1, -1)
    w["q_norm_g"] = row(rep["q_norm_g"][i])
    w["kv_norm_g"] = row(rep["kv_norm_g"][i])
    w["ssd_conv_b"] = row(rep["ssd_conv_b"][i])
    w["dt_bias"] = row(jnp.pad(rep["dt_bias"][i], (0, LANES - SSD_HEADS)))
    a = -jnp.exp(rep["a_log"][i])
    w["a"] = a
    w["a_x"] = row(jnp.repeat(a, SSD_HEAD_DIM))
    w["d_x"] = row(jnp.repeat(rep["d_skip"][i], SSD_HEAD_DIM))
    w["ssd_norm_g"] = row(rep["ssd_norm_g"][i])
    w["ffn_conv_bg"] = row(rep["ffn_conv_b"][i][:D_FF])
    w["ffn_conv_bv"] = row(rep["ffn_conv_b"][i][D_FF:])
    for n in ("ln1_g", "ln1_b", "ln2_g", "ln2_b"):
        w[n] = row(rep[n][i])
    return w


def _layer_fwd(h, w, tabs, expand, layer, late):
    tag = f"l{layer}_"
    cos, sin_a, sin_b = tabs
    s = {"h": h}
    proj = _mm(h, w["w_in_t"], trans_b=True, name=tag + "in_proj", tn=768)
    s["proj"] = proj
    qn = _rms_fwd(proj, OFF_Q, Q_LORA, w["q_norm_g"], name=tag + "q_norm")
    q_raw = _mm(qn, w["w_q_b"], name=tag + "q_up")
    q = _rope_q_fwd(q_raw, cos, sin_a, sin_b, name=tag + "q_rope")
    kvn = _rms_fwd(proj, OFF_KV, KV_LORA, w["kv_norm_g"], name=tag + "kv_norm")
    kv = _mm(kvn, w["w_kv_b"], name=tag + "kv_up", out_dtype=BF16)
    kpe = _rope_k_fwd(proj, cos, sin_a, sin_b, name=tag + "k_rope")
    if isinstance(late, dict):
        o, lse = _flash_fwd(q, kv, kpe, name=tag + "attn")
    else:
        o, lse, *got = _flash_fwd(q, kv, kpe, name=tag + "attn", gather=late)
        late = dict(zip(GATHER_LATE, got))
    w.update(_late_weights(late, layer))
    ya = _mm(o, w["w_o_attn"], name=tag + "attn_out")
    s.update(qn=qn, q=q, kvn=kvn, kv=kv, kpe=kpe, o=o, lse=lse, ya=ya)
    xs, bm, cm, dtx = _ssd_prep_fwd(proj, w["ssd_conv_w"], w["ssd_conv_b"], w["dt_bias"], expand, name=tag + "ssd_prep")
    y, prev = _ssd_fwd(xs, dtx, bm, cm, bm.T, w["a_x"], w["d_x"], name=tag + "ssd_scan")
    yn = _gnorm_fwd(y, proj, w["ssd_norm_g"], name=tag + "ssd_norm")
    ys = _mm(yn, w["w_o_ssd"], name=tag + "ssd_out")
    s.update(xs=xs, bm=bm, cm=cm, dtx=dtx, y=y, prev=prev, yn=yn, ys=ys)
    mixed = _mix_fwd(proj, ya, ys, name=tag + "mix")
    br = _mm(mixed, w["w_out"], name=tag + "mix_out")
    pre1, h1 = _ln_fwd(h, br, w["ln1_g"], w["ln1_b"], name=tag + "ln1")
    s.update(mixed=mixed, pre1=pre1, h1=h1)
    ug = _mm(h1, w["w_up_g"], name=tag + "up_g", tn=1408)
    uv = _mm(h1, w["w_up_v"], name=tag + "up_v", tn=1408)
    act = _ffn_act_fwd(ug, uv, w["ffn_conv_wg"], w["ffn_conv_wv"], w["ffn_conv_bg"], w["ffn_conv_bv"],
                       name=tag + "ffn_act")
    ffn = _mm(act, w["w_down"], name=tag + "down")
    pre2, h2 = _ln_fwd(h1, ffn, w["ln2_g"], w["ln2_b"], name=tag + "ln2")
    s.update(ug=ug, uv=uv, act=act, pre2=pre2)
    return h2, s, late


def _layer_bwd(dh2, w, s, tabs, reduce_m, tag, exchange=None):
    cos, sin_a, sin_b = tabs
    g = {}
    proj = s["proj"]
    dpre2, g["ln2_g"], g["ln2_b"] = _ln_bwd(dh2, s["pre2"], w["ln2_g"], name=tag + "ln2_bwd")
    g["w_down"] = _mm_tn(s["act"], dpre2, name=tag + "down_dw")
    dact = _mm(dpre2, w["w_down"], trans_b=True, name=tag + "down_dx", tn=1408)
    dcg, dcv, dwg, dwv, dbg, dbv = _ffn_act_bwd(s["ug"], s["uv"], dact, w["ffn_conv_wg"], w["ffn_conv_wv"],
                                                w["ffn_conv_bg"], w["ffn_conv_bv"], name=tag + "ffn_act_bwd")
    g["ffn_conv_w"] = jnp.concatenate([dwg, dwv], axis=1)
    g["ffn_conv_b"] = jnp.concatenate([dbg, dbv], axis=1).reshape(-1)
    dug = _conv_bwd_input(dcg, w["ffn_conv_wg"], FFN_CONV, name=tag + "ffn_conv_bwd_g", tc=1408)
    duv = _conv_bwd_input(dcv, w["ffn_conv_wv"], FFN_CONV, name=tag + "ffn_conv_bwd_v", tc=1408)
    g["w_up"] = jnp.concatenate([_mm_tn(s["h1"], dug, name=tag + "up_g_dw", tn=1408),
                                 _mm_tn(s["h1"], duv, name=tag + "up_v_dw", tn=1408)], axis=1)
    dh1 = _mm(dug, w["w_up_g"], trans_b=True, add=dpre2, add_scale=ALPHA, name=tag + "up_g_dx")
    dh1 = _mm(duv, w["w_up_v"], trans_b=True, add=dh1, name=tag + "up_v_dx")
    dpre1, g["ln1_g"], g["ln1_b"] = _ln_bwd(dh1, s["pre1"], w["ln1_g"], name=tag + "ln1_bwd")
    g["w_out"] = _mm_tn(s["mixed"], dpre1, name=tag + "mix_out_dw")
    dmix = _mm(dpre1, w["w_out"], trans_b=True, name=tag + "mix_out_dx")
    dya, dys, dga, dgs = _mix_bwd(dmix, proj, s["ya"], s["ys"], name=tag + "mix_bwd")
    g["w_o_attn"] = _mm_tn(s["o"], dya, name=tag + "attn_out_dw")
    do = _mm(dya, w["w_o_attn"], trans_b=True, out_dtype=BF16, name=tag + "attn_out_dx")
    delta = _attn_delta(do, s["o"], name=tag + "attn_delta")
    by_tile = lambda a: a.reshape(MLA_HEADS, -1, _tile(a.shape[1], 640))
    dq, dkn, dkp, dv, *exchanged = _flash_bwd(s["q"], s["kv"], s["kpe"], do, by_tile(s["lse"]), by_tile(delta),
                                              name=tag + "attn_bwd", exchange=exchange)
    dq_raw = _rope_q_bwd(dq, cos, sin_a, sin_b, name=tag + "q_rope_bwd")
    dwq = _mm_tn(s["qn"], dq_raw, name=tag + "q_up_dw")
    g["w_q_b"] = dwq.reshape(Q_LORA, MLA_HEADS, QHEAD)[:, :, :QK_NOPE + QK_ROPE].reshape(Q_LORA, -1)
    dqn = _mm(dq_raw, w["w_q_b"], trans_b=True, out_dtype=BF16, name=tag + "q_up_dx")
    dqlat, dgq = _rms_bwd(dqn, proj, OFF_Q, Q_LORA, w["q_norm_g"], name=tag + "q_norm_bwd")
    g["q_norm_g"] = dgq.reshape(-1)
    dkv = jnp.concatenate([dkn, dv], axis=1)
    dwkv = _mm_tn(s["kvn"], dkv, name=tag + "kv_up_dw")
    g["w_kv_b"] = jnp.swapaxes(dwkv.reshape(KV_LORA, 2, MLA_HEADS, QK_NOPE), 1, 2).reshape(KV_LORA, -1)
    dkvn = _mm(dkv, w["w_kv_b"], trans_b=True, out_dtype=BF16, name=tag + "kv_up_dx")
    dkvlat, dgkv = _rms_bwd(dkvn, proj, OFF_KV, KV_LORA, w["kv_norm_g"], name=tag + "kv_norm_bwd")
    g["kv_norm_g"] = dgkv.reshape(-1)
    dkpe = _rope_k_bwd(dkp, cos, sin_a, sin_b, name=tag + "k_rope_bwd")
    g["w_o_ssd"] = _mm_tn(s["yn"], dys, name=tag + "ssd_out_dw")
    dyn = _mm(dys, w["w_o_ssd"], trans_b=True, out_dtype=BF16, name=tag + "ssd_out_dx")
    dy, dz, dgn = _gnorm_bwd(dyn, s["y"], proj, w["ssd_norm_g"], name=tag + "ssd_norm_bwd")
    g["ssd_norm_g"] = dgn.reshape(-1)
    dxs, ddtx, dbm, dcm, da_x, dd_x = _ssd_bwd(s["xs"], s["dtx"], s["bm"], s["cm"], s["cm"].T, s["prev"], dy,
                                               w["a_x"], w["d_x"], name=tag + "ssd_scan_bwd")
    g["a_log"] = da_x.reshape(SSD_HEADS, SSD_HEAD_DIM).sum(axis=1) * w["a"]
    g["d_skip"] = dd_x.reshape(SSD_HEADS, SSD_HEAD_DIM).sum(axis=1)
    dconv, ddtr, dcw, dcb, ddtb = _ssd_prep_bwd_a(proj, dxs, dbm, dcm, ddtx, w["ssd_conv_w"], w["ssd_conv_b"],
                                                  w["dt_bias"], reduce_m, name=tag + "ssd_prep_bwd")
    g["ssd_conv_w"] = dcw
    g["ssd_conv_b"] = dcb.reshape(-1)
    g["dt_bias"] = ddtb.reshape(-1)[:SSD_HEADS]
    dxbc = _conv_bwd_input(dconv, w["ssd_conv_w"], SSD_CONV, name=tag + "ssd_conv_bwd", tc=1024)
    h = s["h"]
    comps = ((dqlat, OFF_Q), (dkvlat, OFF_KV), (dz, OFF_Z), (dxbc, OFF_XBC), (dga, OFF_GA), (dgs, OFF_GS),
             (dkpe, OFF_KPE), (ddtr, OFF_DT))
    dws = {off: _mm_tn(dc, h, name=f"{tag}in_dw{n}") for n, (dc, off) in enumerate(comps)}
    with_w = lambda group: [(dc, w["w_in_t"][off:off + dc.shape[1]]) for dc, off in group]
    wide = [c for c in comps if c[1] in (OFF_Z, OFF_XBC)]
    rest = [c for c in comps if c[1] not in (OFF_Z, OFF_XBC)]
    dh = _mm_sum(with_w(wide), dpre1, add_scale=ALPHA, name=tag + "in_dx_wide")
    dh = _mm_sum(with_w(rest), dh, name=tag + "in_dx_rest")
    g["w_in"] = jnp.concatenate([dws[OFF_Q], dws[OFF_KV], dws[OFF_KPE][:QK_ROPE], dws[OFF_Z], dws[OFF_XBC],
                                 dws[OFF_DT][:SSD_HEADS], dws[OFF_GA], dws[OFF_GS]], axis=0)
    return dh, g, (exchanged[0] if exchanged else None)


def _local_step(x, target, full, late, rep):
    seq = x.shape[0]
    t = seq + ROW0
    tabs = _rope_tables(t)
    expand = _expand_matrix()
    reduce_m = expand.T
    meta = _join(full["meta_tokens"].reshape(N_CHIPS, N_META, -1), "col")
    xin = jnp.concatenate([jnp.zeros((PAD, D_MODEL), F32), meta, x], axis=0)
    row = lambda v: v.reshape(1, -1)
    _, h = _ln_fwd(xin, None, row(rep["emb_ln_g"]), row(rep["emb_ln_b"]), name="emb_ln")
    ws, saved = [], []
    for i in range(DEPTH):
        w = _early_weights(full, rep, i)
        h, s, late = _layer_fwd(h, w, tabs, expand, i, late)
        ws.append(w)
        saved.append(s)
    dh, loss = _loss_grad(h, target, name="loss")
    dh, g1, _ = _layer_bwd(dh, ws[1], saved[1], tabs, reduce_m, "l1_")
    g4, layout1 = _pack_part(g1, outside=False)
    dh, g0, parts1 = _layer_bwd(dh, ws[0], saved[0], tabs, reduce_m, "l0_", exchange=_reduce_begin(g4, "reduce1_"))
    dxin, dg, db = _ln_bwd(dh, xin, row(rep["emb_ln_g"]), name="emb_ln_bwd")
    g0["emb_ln_g"] = dg
    g0["emb_ln_b"] = db
    g0["meta_tokens"] = dxin[PAD:ROW0]
    return loss, dxin[ROW0:], g0, parts1, layout1


def _pack_part(g, outside):
    sharded = PART_SHARDED + (OUTSIDE_SHARDED if outside else ())
    replicated = PART_REPLICATED + (OUTSIDE_REPLICATED if outside else ())
    pieces, layout, row0 = [], [], 0

    def add(name, piece):
        nonlocal row0
        pieces.append(piece)
        layout.append((name, row0, piece.shape[1]))
        row0 += piece.shape[1]

    for n, kind in sharded:
        add(n, _chip_segments(g[n], "row" if n == "w_in" else kind).reshape(N_CHIPS, -1, LANES))
    for n in replicated:
        add(n, jnp.broadcast_to(g[n].reshape(1, -1, LANES), (N_CHIPS, g[n].size // LANES, LANES)))
    tail = jnp.concatenate([g[n].reshape(-1) for n in PART_TAIL])
    tail = jnp.pad(tail, (0, LANES - tail.shape[0]))
    add("tail", jnp.broadcast_to(tail.reshape(1, 1, LANES), (N_CHIPS, 1, LANES)))
    rows = -(-row0 // REDUCE_ROW_ALIGN) * REDUCE_ROW_ALIGN
    pieces.append(jnp.zeros((N_CHIPS, rows - row0, LANES), F32))
    return jnp.concatenate(pieces, axis=1), layout


def _core_index():
    return lax.axis_index("c").astype(jnp.int32).reshape(1)


def _reduce_begin(g4, tag):
    n, r, _ = g4.shape
    g5 = g4.reshape(n, 2, r // 2, LANES)
    got = _sibling_swap(g5, name=tag + "pair_swap")
    return _pair_add(g5, got, _core_index(), name=tag + "pair_add")


def _reduce_end(parts, tag):
    half = _sum_chips(parts, _core_index(), name=tag + "chip_sum")
    both = _sibling_allgather(half, name=tag + "pair_gather")
    return both.reshape(-1, LANES)


def _unpack_part(flat, layout, shapes):
    out = {}
    for n, row0, rows in layout:
        piece = flat[row0:row0 + rows]
        if n == "tail":
            vec = piece.reshape(-1)
            for k, m in enumerate(PART_TAIL):
                out[m] = vec[k * SSD_HEADS:(k + 1) * SSD_HEADS]
        elif n == "w_in":
            out[n] = piece.reshape(shapes[n][1], shapes[n][0]).T
        else:
            out[n] = piece.reshape(shapes[n])
    return out


def kernel(x, meta_tokens, emb_ln_g, emb_ln_b, w_in, q_norm_g, w_q_b, kv_norm_g, w_kv_b, w_o_attn, ssd_conv_w, ssd_conv_b, dt_bias, a_log, d_skip, ssd_norm_g, w_o_ssd, w_out, ln1_g, ln1_b, w_up, ffn_conv_w, ffn_conv_b, w_down, ln2_g, ln2_b, loss_target, m_meta_tokens, m_emb_ln_g, m_emb_ln_b, m_w_in, m_q_norm_g, m_w_q_b, m_kv_norm_g, m_w_kv_b, m_w_o_attn, m_ssd_conv_w, m_ssd_conv_b, m_dt_bias, m_a_log, m_d_skip, m_ssd_norm_g, m_w_o_ssd, m_w_out, m_ln1_g, m_ln1_b, m_w_up, m_ffn_conv_w, m_ffn_conv_b, m_w_down, m_ln2_g, m_ln2_b, v_meta_tokens, v_emb_ln_g, v_emb_ln_b, v_w_in, v_q_norm_g, v_w_q_b, v_kv_norm_g, v_w_kv_b, v_w_o_attn, v_ssd_conv_w, v_ssd_conv_b, v_dt_bias, v_a_log, v_d_skip, v_ssd_norm_g, v_w_o_ssd, v_w_out, v_ln1_g, v_ln1_b, v_w_up, v_ffn_conv_w, v_ffn_conv_b, v_w_down, v_ln2_g, v_ln2_b):
    given = dict(locals())
    local_w = {n: given[n] for n in WEIGHTS}
    local_m = {n: given["m_" + n] for n in WEIGHTS}
    local_v = {n: given["v_" + n] for n in WEIGHTS}
    full = dict(zip(GATHER_EARLY, _chip_allgather(_travel_form(local_w, GATHER_EARLY), name="gather_early")))
    rep = {n: local_w[n] for n in REPLICATED}
    loss, grad_x, g0, parts1, layout1 = _local_step(x[0], loss_target[0], full, _travel_form(local_w, GATHER_LATE), rep)
    g4, layout0 = _pack_part(g0, outside=True)
    parts0 = _chip_exchange(_reduce_begin(g4, "reduce0_"), name="reduce0_chip_exchange")
    outside = [n for n, _ in OUTSIDE_SHARDED] + list(OUTSIDE_REPLICATED)
    shapes = {n: (local_w[n].shape if n in outside else local_w[n].shape[1:]) for n in WEIGHTS}
    by_layer = [_unpack_part(_reduce_end(parts0, "reduce0_"), layout0, shapes),
                _unpack_part(_reduce_end(parts1, "reduce1_"), layout1, shapes)]
    grad = {n: (by_layer[0][n] if n in outside else jnp.stack([p[n] for p in by_layer])) for n in WEIGHTS}
    upd = {}
    small = [n for n in WEIGHTS if n not in GATHER_BF16]
    for n in GATHER_BF16:
        upd[n] = _adamw(grad[n], local_w[n], local_m[n], local_v[n], name="adamw_" + n)
    res = _adamw_small([(grad[n], local_w[n], local_m[n], local_v[n]) for n in small], name="adamw_small")
    upd.update(zip(small, res))
    total = lax.psum(loss[0, 0], ("x", "y", "c"))
    outs = [total, grad_x[None]] + [grad[n] for n in WEIGHTS]
    for q in range(3):
        outs.extend(upd[n][q] for n in WEIGHTS)
    return tuple(outs)
```

```python
import functools
import math

import numpy as np
import jax
import jax.numpy as jnp
from jax import lax
from jax.experimental import pallas as pl
from jax.experimental.pallas import tpu as pltpu

F32 = jnp.float32
BF16 = jnp.bfloat16

D_MODEL = 1024
N_META = 16
DEPTH = 2
MLA_HEADS = 8
Q_LORA = 768
KV_LORA = 256
QK_NOPE = 128
QK_ROPE = 64
V_HEAD = 128
ROPE_THETA = 10000.0
NEG_INF = -1e30
PAD_KEY_SCORE = -1e30
SSD_INNER = 2048
SSD_HEAD_DIM = 64
SSD_HEADS = 32
SSD_GROUPS = 4
SSD_STATE = 128
SSD_CONV = 4
SSD_CONV_DIM = 3072
CHUNK = 128
D_FF = 2816
FFN_CONV = 3
LN_EPS = 1e-5
RMS_EPS = 1e-6
ALPHA = (2 * DEPTH) ** 0.25
ATTN_SCALE = (QK_NOPE + QK_ROPE) ** -0.5
LOG2E = math.log2(math.e)
LN2 = math.log(2.0)
Q_SCALE = ATTN_SCALE * LOG2E
ADAM_LR = 0.001
ADAM_B1 = 0.9
ADAM_B2 = 0.999
ADAM_EPS = 1e-08
ADAM_WD = 0.01
ADAM_STEP = 10

LANES = 128
PAD = 112
ROW0 = PAD + N_META
QHEAD = 256
GROUP_W = SSD_INNER // SSD_GROUPS
HALO = 8
VMEM_LIMIT_BYTES = 56 * 1024 * 1024
MM_VMEM_BUDGET = 40 * 1024 * 1024
N_CHIPS = 4

OFF_Q, OFF_KV, OFF_Z, OFF_XBC, OFF_GA, OFF_GS, OFF_KPE, OFF_DT = 0, 768, 1024, 3072, 6144, 7168, 8192, 8320
IN_COLS_P = 8448

NT_DIMS = (((1,), (1,)), ((), ()))
NN_DIMS = (((1,), (0,)), ((), ()))
TN_DIMS = (((0,), (0,)), ((), ()))

SHARDED = (("meta_tokens", "col"), ("w_in", "col"), ("w_q_b", "col"), ("w_kv_b", "col"), ("w_o_attn", "row"),
           ("ssd_conv_w", "col"), ("w_o_ssd", "row"), ("w_out", "row"), ("w_up", "col"), ("ffn_conv_w", "col"),
           ("w_down", "row"))
REPLICATED = ("emb_ln_g", "emb_ln_b", "q_norm_g", "kv_norm_g", "ssd_conv_b", "dt_bias", "a_log", "d_skip",
              "ssd_norm_g", "ln1_g", "ln1_b", "ffn_conv_b", "ln2_g", "ln2_b")
WEIGHTS = ("meta_tokens", "emb_ln_g", "emb_ln_b", "w_in", "q_norm_g", "w_q_b", "kv_norm_g", "w_kv_b", "w_o_attn",
           "ssd_conv_w", "ssd_conv_b", "dt_bias", "a_log", "d_skip", "ssd_norm_g", "w_o_ssd", "w_out", "ln1_g",
           "ln1_b", "w_up", "ffn_conv_w", "ffn_conv_b", "w_down", "ln2_g", "ln2_b")
GATHER_BF16 = ("w_in", "w_q_b", "w_kv_b", "w_o_attn", "w_o_ssd", "w_out", "w_up", "w_down")
GATHER_EARLY = ("meta_tokens", "w_in", "w_q_b", "w_kv_b", "ssd_conv_w")
GATHER_LATE = ("w_o_attn", "w_o_ssd", "w_out", "w_up", "ffn_conv_w", "w_down")
OUTSIDE = ("meta_tokens", "emb_ln_g", "emb_ln_b")
LATE_GRADS = ("w_in", "w_q_b", "q_norm_g", "w_kv_b", "kv_norm_g")
REDUCE_ROW_ALIGN = 512


def _tile(n, target, base=LANES):
    best = None
    d = base
    while d <= min(n, target):
        if n % d == 0:
            best = d
        d += base
    return n if best is None else best


def _cp(*sem):
    return pltpu.CompilerParams(dimension_semantics=sem, vmem_limit_bytes=VMEM_LIMIT_BYTES)


def _sds(shape, dtype):
    return jax.ShapeDtypeStruct(shape, dtype)


def _row_ids(i, tr, shape):
    return i * tr + lax.broadcasted_iota(jnp.int32, shape, 0)


def _sigmoid(x):
    return 1.0 / (1.0 + jnp.exp(-x))


def _mm_tiles(m, n, tn_max, tk, nk, a_bytes, b_bytes, out_bytes, add_bytes):
    divisors = lambda size, cap: [d for d in range(min(size, cap) // LANES * LANES, 0, -LANES) if size % d == 0]
    for tm in divisors(m, m):
        for tn in divisors(n, tn_max):
            blocks = 2 * (tm * tk * a_bytes + tk * tn * b_bytes + tm * tn * (out_bytes + add_bytes))
            temps = tm * tn * 4 * (2 if nk > 1 else 1) + tm * tk * 2 + tk * tn * 2
            if blocks + temps <= MM_VMEM_BUDGET:
                return tm, tn
    return LANES, LANES


def _mm(a, b, *, name, trans_b=False, out_dtype=F32, add=None, add_scale=1.0, tn=1024, tk=1408):
    m, k_dim = a.shape
    n = b.shape[0] if trans_b else b.shape[1]
    tk = _tile(k_dim, tk)
    nk = k_dim // tk
    has_add = add is not None
    tm, tn = _mm_tiles(m, n, tn, tk, nk, a.dtype.itemsize, b.dtype.itemsize, jnp.dtype(out_dtype).itemsize,
                       add.dtype.itemsize if has_add else 0)
    dims = NT_DIMS if trans_b else NN_DIMS

    def body(*refs):
        a_ref, b_ref = refs[0], refs[1]
        r_ref = refs[2] if has_add else None
        o_ref = refs[3] if has_add else refs[2]
        part = lax.dot_general(a_ref[...].astype(BF16), b_ref[...].astype(BF16), dims, preferred_element_type=F32)

        def finish(r):
            if has_add:
                r = r + add_scale * r_ref[...].astype(F32)
            o_ref[...] = r.astype(out_dtype)

        if nk == 1:
            finish(part)
        else:
            acc = refs[-1]
            kk = pl.program_id(2)

            @pl.when(kk == 0)
            def _():
                acc[...] = part

            @pl.when(kk > 0)
            def _():
                acc[...] += part

            @pl.when(kk == nk - 1)
            def _():
                finish(acc[...])

    in_specs = [pl.BlockSpec((tm, tk), lambda i, j, kk: (i, kk)),
                pl.BlockSpec((tn, tk), lambda i, j, kk: (j, kk)) if trans_b
                else pl.BlockSpec((tk, tn), lambda i, j, kk: (kk, j))]
    args = [a, b]
    if has_add:
        in_specs.append(pl.BlockSpec((tm, tn), lambda i, j, kk: (i, j)))
        args.append(add)
    return pl.pallas_call(
        body, name=name, grid=(m // tm, n // tn, nk), in_specs=in_specs,
        out_specs=pl.BlockSpec((tm, tn), lambda i, j, kk: (i, j)),
        out_shape=_sds((m, n), out_dtype),
        scratch_shapes=[pltpu.VMEM((tm, tn), F32)] if nk > 1 else [],
        compiler_params=_cp("parallel", "parallel", "arbitrary"),
    )(*args)


def _mm_sum(pairs, add, *, name, add_scale=1.0, tm=640):
    m, n = add.shape
    tm = _tile(m, tm)
    npairs = len(pairs)

    def body(*refs):
        a_refs, b_refs = refs[:npairs], refs[npairs:2 * npairs]
        r_ref, o_ref = refs[2 * npairs], refs[2 * npairs + 1]
        acc = add_scale * r_ref[...]
        for a_ref, b_ref in zip(a_refs, b_refs):
            acc = acc + jnp.dot(a_ref[...].astype(BF16), b_ref[...].astype(BF16), preferred_element_type=F32)
        o_ref[...] = acc

    in_specs = ([pl.BlockSpec((tm, a.shape[1]), lambda i: (i, 0)) for a, _ in pairs]
                + [pl.BlockSpec(b.shape, lambda i: (0, 0)) for _, b in pairs]
                + [pl.BlockSpec((tm, n), lambda i: (i, 0))])
    return pl.pallas_call(
        body, name=name, grid=(m // tm,), in_specs=in_specs, out_specs=pl.BlockSpec((tm, n), lambda i: (i, 0)),
        out_shape=_sds((m, n), F32), compiler_params=_cp("parallel"),
    )(*[a for a, _ in pairs], *[b for _, b in pairs], add)


def _mm_tn(a, b, *, name, tko=1408, tn=1024, tt=640):
    t, k_dim = a.shape
    n = b.shape[1]
    tko, tn, tt = _tile(k_dim, tko), _tile(n, tn), _tile(t, tt)

    def body(a_ref, b_ref, o_ref):
        part = lax.dot_general(a_ref[...].astype(BF16), b_ref[...].astype(BF16), TN_DIMS, preferred_element_type=F32)
        tt_i = pl.program_id(2)

        @pl.when(tt_i == 0)
        def _():
            o_ref[...] = part

        @pl.when(tt_i > 0)
        def _():
            o_ref[...] += part

    return pl.pallas_call(
        body, name=name, grid=(k_dim // tko, n // tn, t // tt),
        in_specs=[pl.BlockSpec((tt, tko), lambda i, j, s: (s, i)), pl.BlockSpec((tt, tn), lambda i, j, s: (s, j))],
        out_specs=pl.BlockSpec((tko, tn), lambda i, j, s: (i, j)),
        out_shape=_sds((k_dim, n), F32),
        compiler_params=_cp("parallel", "parallel", "arbitrary"),
    )(a, b)


def _ln_fwd(h, branch, g, b, *, name):
    t, d = h.shape
    tr = _tile(t, 640)
    has_branch = branch is not None

    def body(*refs):
        if has_branch:
            h_ref, br_ref, g_ref, b_ref, pre_ref, o_ref = refs
            pre = ALPHA * h_ref[...] + br_ref[...]
            pre_ref[...] = pre
        else:
            h_ref, g_ref, b_ref, o_ref = refs
            pre = h_ref[...]
        mu = jnp.mean(pre, axis=1, keepdims=True)
        xc = pre - mu
        var = jnp.mean(xc * xc, axis=1, keepdims=True)
        y = xc * lax.rsqrt(var + LN_EPS) * g_ref[...] + b_ref[...]
        rows = _row_ids(pl.program_id(0), tr, (tr, 1))
        o_ref[...] = jnp.where(rows >= PAD, y, 0.0)

    row_spec = pl.BlockSpec((tr, d), lambda i: (i, 0))
    vec_spec = pl.BlockSpec((1, d), lambda i: (0, 0))
    if has_branch:
        return pl.pallas_call(
            body, name=name, grid=(t // tr,), in_specs=[row_spec, row_spec, vec_spec, vec_spec],
            out_specs=[row_spec, row_spec], out_shape=[_sds((t, d), F32), _sds((t, d), F32)],
            compiler_params=_cp("parallel"))(h, branch, g, b)
    out = pl.pallas_call(
        body, name=name, grid=(t // tr,), in_specs=[row_spec, vec_spec, vec_spec],
        out_specs=row_spec, out_shape=_sds((t, d), F32), compiler_params=_cp("parallel"))(h, g, b)
    return h, out


def _ln_bwd(dy, pre, g, *, name):
    t, d = pre.shape
    tr = _tile(t, 640)

    def body(dy_ref, pre_ref, g_ref, dpre_ref, dg_ref, db_ref):
        i = pl.program_id(0)
        pre_v = pre_ref[...]
        mu = jnp.mean(pre_v, axis=1, keepdims=True)
        xc = pre_v - mu
        var = jnp.mean(xc * xc, axis=1, keepdims=True)
        rstd = lax.rsqrt(var + LN_EPS)
        xhat = xc * rstd
        rows = _row_ids(i, tr, (tr, 1))
        dym = jnp.where(rows >= PAD, dy_ref[...], 0.0)
        gdy = dym * g_ref[...]
        m1 = jnp.mean(gdy, axis=1, keepdims=True)
        m2 = jnp.mean(gdy * xhat, axis=1, keepdims=True)
        dpre_ref[...] = rstd * (gdy - m1 - xhat * m2)
        pg = jnp.sum(dym * xhat, axis=0, keepdims=True)
        pb = jnp.sum(dym, axis=0, keepdims=True)

        @pl.when(i == 0)
        def _():
            dg_ref[...] = pg
            db_ref[...] = pb

        @pl.when(i > 0)
        def _():
            dg_ref[...] += pg
            db_ref[...] += pb

    row_spec = pl.BlockSpec((tr, d), lambda i: (i, 0))
    vec_spec = pl.BlockSpec((1, d), lambda i: (0, 0))
    return pl.pallas_call(
        body, name=name, grid=(t // tr,), in_specs=[row_spec, row_spec, vec_spec],
        out_specs=[row_spec, vec_spec, vec_spec],
        out_shape=[_sds((t, d), F32), _sds((1, d), F32), _sds((1, d), F32)],
        compiler_params=_cp("arbitrary"))(dy, pre, g)


def _rms_fwd(proj, col_off, width, g, *, name):
    t = proj.shape[0]
    tr = _tile(t, 640)
    cb = col_off // width

    def body(x_ref, g_ref, o_ref):
        x = x_ref[...]
        r = lax.rsqrt(jnp.mean(x * x, axis=1, keepdims=True) + RMS_EPS)
        o_ref[...] = (x * r * g_ref[...]).astype(BF16)

    return pl.pallas_call(
        body, name=name, grid=(t // tr,),
        in_specs=[pl.BlockSpec((tr, width), lambda i: (i, cb)), pl.BlockSpec((1, width), lambda i: (0, 0))],
        out_specs=pl.BlockSpec((tr, width), lambda i: (i, 0)), out_shape=_sds((t, width), BF16),
        compiler_params=_cp("parallel"))(proj, g)


def _rms_bwd(dy, proj, col_off, width, g, *, name):
    t = proj.shape[0]
    tr = _tile(t, 640)
    cb = col_off // width

    def body(dy_ref, x_ref, g_ref, dx_ref, dg_ref):
        i = pl.program_id(0)
        x = x_ref[...]
        dyv = dy_ref[...].astype(F32)
        r = lax.rsqrt(jnp.mean(x * x, axis=1, keepdims=True) + RMS_EPS)
        gdy = dyv * g_ref[...]
        m = jnp.mean(x * gdy, axis=1, keepdims=True)
        dx_ref[...] = (r * gdy - x * (r * r * r) * m).astype(BF16)
        pg = jnp.sum(dyv * x * r, axis=0, keepdims=True)

        @pl.when(i == 0)
        def _():
            dg_ref[...] = pg

        @pl.when(i > 0)
        def _():
            dg_ref[...] += pg

    return pl.pallas_call(
        body, name=name, grid=(t // tr,),
        in_specs=[pl.BlockSpec((tr, width), lambda i: (i, 0)), pl.BlockSpec((tr, width), lambda i: (i, cb)),
                  pl.BlockSpec((1, width), lambda i: (0, 0))],
        out_specs=[pl.BlockSpec((tr, width), lambda i: (i, 0)), pl.BlockSpec((1, width), lambda i: (0, 0))],
        out_shape=[_sds((t, width), BF16), _sds((1, width), F32)],
        compiler_params=_cp("arbitrary"))(dy, proj, g)


def _rope_apply(r, cos, sin_a, sin_b):
    return r * cos + pltpu.roll(r, 96, 1) * sin_a + pltpu.roll(r, 32, 1) * sin_b


def _rope_apply_t(dr, cos, sin_a, sin_b):
    return dr * cos + pltpu.roll(dr * sin_a, 32, 1) + pltpu.roll(dr * sin_b, 96, 1)


def _rope_q_fwd(q, cos, sin_a, sin_b, *, name):
    t, w = q.shape
    tr = _tile(t, 128)

    def body(q_ref, c_ref, sa_ref, sb_ref, o_ref):
        c, sa, sb = c_ref[...], sa_ref[...], sb_ref[...]
        flag = lax.broadcasted_iota(jnp.int32, (tr, LANES), 1) == QK_ROPE
        for h in range(MLA_HEADS):
            base = h * QHEAD
            o_ref[:, base:base + LANES] = (q_ref[:, base:base + LANES] * Q_SCALE).astype(BF16)
            rot = _rope_apply(q_ref[:, base + LANES:base + QHEAD], c, sa, sb)
            o_ref[:, base + LANES:base + QHEAD] = jnp.where(flag, 1.0, rot * Q_SCALE).astype(BF16)

    tab = pl.BlockSpec((tr, LANES), lambda i: (i, 0))
    row = pl.BlockSpec((tr, w), lambda i: (i, 0))
    return pl.pallas_call(body, name=name, grid=(t // tr,), in_specs=[row, tab, tab, tab], out_specs=row,
                          out_shape=_sds((t, w), BF16), compiler_params=_cp("parallel"))(q, cos, sin_a, sin_b)


def _rope_q_bwd(dq, cos, sin_a, sin_b, *, name):
    t, w = dq.shape
    tr = _tile(t, 128)

    def body(dq_ref, c_ref, sa_ref, sb_ref, o_ref):
        c, sa, sb = c_ref[...], sa_ref[...], sb_ref[...]
        for h in range(MLA_HEADS):
            base = h * QHEAD
            o_ref[:, base:base + LANES] = (dq_ref[:, base:base + LANES] * ATTN_SCALE).astype(BF16)
            d_rot = _rope_apply_t(dq_ref[:, base + LANES:base + QHEAD], c, sa, sb)
            o_ref[:, base + LANES:base + QHEAD] = (d_rot * ATTN_SCALE).astype(BF16)

    tab = pl.BlockSpec((tr, LANES), lambda i: (i, 0))
    row = pl.BlockSpec((tr, w), lambda i: (i, 0))
    return pl.pallas_call(body, name=name, grid=(t // tr,), in_specs=[row, tab, tab, tab], out_specs=row,
                          out_shape=_sds((t, w), BF16), compiler_params=_cp("parallel"))(dq, cos, sin_a, sin_b)


def _rope_k_fwd(proj, cos, sin_a, sin_b, *, name):
    t = proj.shape[0]
    tr = _tile(t, 640)
    cb = OFF_KPE // LANES

    def body(x_ref, c_ref, sa_ref, sb_ref, o_ref):
        rot = _rope_apply(x_ref[...], c_ref[...], sa_ref[...], sb_ref[...])
        rows = _row_ids(pl.program_id(0), tr, (tr, LANES))
        lane = lax.broadcasted_iota(jnp.int32, (tr, LANES), 1)
        o_ref[...] = jnp.where((lane == QK_ROPE) & (rows < PAD), PAD_KEY_SCORE, rot).astype(BF16)

    tab = pl.BlockSpec((tr, LANES), lambda i: (i, 0))
    return pl.pallas_call(body, name=name, grid=(t // tr,),
                          in_specs=[pl.BlockSpec((tr, LANES), lambda i: (i, cb)), tab, tab, tab], out_specs=tab,
                          out_shape=_sds((t, LANES), BF16), compiler_params=_cp("parallel"))(proj, cos, sin_a, sin_b)


def _rope_k_bwd(dkp, cos, sin_a, sin_b, *, name):
    nh, t, _ = dkp.shape
    tr = _tile(t, 640)

    def body(d_ref, c_ref, sa_ref, sb_ref, o_ref):
        tot = d_ref[0]
        for h in range(1, nh):
            tot = tot + d_ref[h]
        o_ref[...] = _rope_apply_t(tot, c_ref[...], sa_ref[...], sb_ref[...]).astype(BF16)

    tab = pl.BlockSpec((tr, LANES), lambda i: (i, 0))
    return pl.pallas_call(body, name=name, grid=(t // tr,),
                          in_specs=[pl.BlockSpec((nh, tr, LANES), lambda i: (0, i, 0)), tab, tab, tab], out_specs=tab,
                          out_shape=_sds((t, LANES), BF16), compiler_params=_cp("parallel"))(dkp, cos, sin_a, sin_b)


def _causal(tb, keys_first=False):
    a = lax.broadcasted_iota(jnp.int32, (tb, tb), 0)
    b = lax.broadcasted_iota(jnp.int32, (tb, tb), 1)
    return a <= b if keys_first else b <= a


def _flash_fwd(q, kv, kpe, *, name, gather=()):
    t = q.shape[0]
    nh = MLA_HEADS
    tb = _tile(t, 640)
    nb = t // tb
    na = len(gather)

    def attend(q_ref, kn_ref, v_ref, kp_ref, o_ref, lse_ref, extra):
        i = pl.program_id(1)
        qv = q_ref[...]

        def scores(j):
            r0 = pl.multiple_of(j * tb, tb)
            k = jnp.concatenate([kn_ref[pl.ds(r0, tb), :], kp_ref[pl.ds(r0, tb), :]], axis=1)
            return lax.dot_general(qv, k, NT_DIMS, preferred_element_type=F32)

        def update(s, j, state):
            m_prev, l_prev, acc = state
            m_new = jnp.maximum(m_prev, jnp.max(s, axis=1, keepdims=True))
            p = jnp.exp2(s - m_new)
            corr = jnp.exp2(m_prev - m_new)
            r0 = pl.multiple_of(j * tb, tb)
            pv = jnp.dot(p.astype(BF16), v_ref[pl.ds(r0, tb), :], preferred_element_type=F32)
            return m_new, corr * l_prev + jnp.sum(p, axis=1, keepdims=True), corr * acc + pv

        def loop(j, carry):
            s_cur, st = carry
            s_next = scores(j + 1)
            return s_next, update(s_cur, j, st)

        state = (jnp.full((tb, 1), NEG_INF, F32), jnp.zeros((tb, 1), F32), jnp.zeros((tb, V_HEAD), F32))
        s_diag, state = lax.fori_loop(0, i, loop, (scores(0), state))
        m, l, acc = update(jnp.where(_causal(tb), s_diag, NEG_INF), i, state)
        o_ref[...] = (acc / l).astype(BF16)
        lse_ref[0] = m + jnp.log2(l)

        if na:
            step = pl.program_id(0) * nb + i
            for phase, at in enumerate((0, (nh * nb) // 2, nh * nb - 1)):
                @pl.when(step == at)
                def _(phase=phase):
                    _chip_allgather_phase(phase, extra[:na], extra[na:2 * na], *extra[2 * na:])

    def body(q_ref, kn_ref, v_ref, kp_ref, *rest):
        attend(q_ref, kn_ref, v_ref, kp_ref, *rest[na:na + 2], extra=rest[:na] + rest[na + 2:])

    gather = list(gather)
    return pl.pallas_call(
        body, name=name, grid=(nh, nb),
        in_specs=[pl.BlockSpec((tb, QHEAD), lambda h, i: (i, h)),
                  pl.BlockSpec((t, LANES), lambda h, i: (0, h)),
                  pl.BlockSpec((t, LANES), lambda h, i: (0, nh + h)),
                  pl.BlockSpec((t, LANES), lambda h, i: (0, 0))] + [_ANY] * na,
        out_specs=[pl.BlockSpec((tb, V_HEAD), lambda h, i: (i, h)),
                   pl.BlockSpec((1, tb, 1), lambda h, i: (h, i, 0))] + [_ANY] * na,
        out_shape=[_sds((t, nh * V_HEAD), BF16), _sds((nh, t, 1), F32)] + _chip_allgather_shapes(gather),
        scratch_shapes=_chip_allgather_sems(na) if na else [],
        compiler_params=_cp("arbitrary", "arbitrary"))(q, kv, kv, kpe, *gather)


def _attn_delta(do, o, *, name):
    t = o.shape[0]
    nh = MLA_HEADS
    tr = _tile(t, 640)

    def body(do_ref, o_ref, d_ref):
        d_ref[0] = jnp.sum(do_ref[...].astype(F32) * o_ref[...].astype(F32), axis=1, keepdims=True)

    blk = pl.BlockSpec((tr, V_HEAD), lambda h, i: (i, h))
    return pl.pallas_call(body, name=name, grid=(nh, t // tr), in_specs=[blk, blk],
                          out_specs=pl.BlockSpec((1, tr, 1), lambda h, i: (h, i, 0)),
                          out_shape=_sds((nh, t, 1), F32), compiler_params=_cp("parallel", "parallel"))(do, o)


def _flash_bwd(q, kv, kpe, do, lse, delta, *, name, exchange=None):
    t = q.shape[0]
    nh = MLA_HEADS
    tb = lse.shape[2]
    nb = t // tb
    fused = exchange is not None

    def body(*refs):
        q_ref, do_ref, lse_ref, dl_ref, kn_ref, v_ref, kp_ref = refs[:7]
        dq_ref, dkn_ref, dkp_ref, dv_ref = refs[7 + fused:11 + fused]
        j = pl.program_id(1)

        if fused:
            copies = functools.partial(_chip_exchange_copies, refs[7], refs[11 + fused], *refs[12 + fused:])
            first = (pl.program_id(0) == 0) & (j == 0)
            last = (pl.program_id(0) == nh - 1) & (j == nb - 1)

            @pl.when(first)
            def _():
                _chip_exchange_start(copies())

        @pl.when(j == 0)
        def _():
            dq_ref[...] = jnp.zeros((t, QHEAD), F32)

        k = jnp.concatenate([kn_ref[...], kp_ref[...]], axis=1)
        v = v_ref[...]

        def tile(i, carry, masked):
            dk, dv = carry
            r0 = pl.multiple_of(i * tb, tb)
            qv = q_ref[pl.ds(r0, tb), :]
            dov = do_ref[pl.ds(r0, tb), :]
            st = lax.dot_general(k, qv, NT_DIMS, preferred_element_type=F32)
            if masked:
                st = jnp.where(_causal(tb, keys_first=True), st, NEG_INF)
            pt = jnp.exp2(st - lse_ref[0, pl.ds(i, 1), :])
            dpt = lax.dot_general(v, dov, NT_DIMS, preferred_element_type=F32)
            dst = (pt * (dpt - dl_ref[0, pl.ds(i, 1), :])).astype(BF16)
            dv = dv + jnp.dot(pt.astype(BF16), dov, preferred_element_type=F32)
            dk = dk + jnp.dot(dst, qv, preferred_element_type=F32)
            dq_ref[pl.ds(r0, tb), :] += lax.dot_general(dst, k, TN_DIMS, preferred_element_type=F32)
            return dk, dv

        carry = tile(j, (jnp.zeros((tb, QHEAD), F32), jnp.zeros((tb, V_HEAD), F32)), True)
        dk, dv = lax.fori_loop(j + 1, nb, lambda i, c: tile(i, c, False), carry)
        dkn_ref[...] = (dk[:, :LANES] * LN2).astype(BF16)
        dkp_ref[0] = dk[:, LANES:] * LN2
        dv_ref[...] = dv.astype(BF16)

        if fused:
            @pl.when(last)
            def _():
                _chip_exchange_wait(copies())

    stat = pl.BlockSpec((1, nb, tb), lambda h, j: (h, 0, 0))
    in_specs = [pl.BlockSpec((t, QHEAD), lambda h, j: (0, h)),
                pl.BlockSpec((t, V_HEAD), lambda h, j: (0, h)),
                stat, stat,
                pl.BlockSpec((tb, LANES), lambda h, j: (j, h)),
                pl.BlockSpec((tb, LANES), lambda h, j: (j, nh + h)),
                pl.BlockSpec((tb, LANES), lambda h, j: (j, 0))]
    out_specs = [pl.BlockSpec((t, QHEAD), lambda h, j: (0, h)),
                 pl.BlockSpec((tb, LANES), lambda h, j: (j, h)),
                 pl.BlockSpec((1, tb, LANES), lambda h, j: (h, j, 0)),
                 pl.BlockSpec((tb, V_HEAD), lambda h, j: (j, h))]
    out_shape = [_sds((t, nh * QHEAD), F32), _sds((t, nh * LANES), BF16), _sds((nh, t, LANES), F32),
                 _sds((t, nh * V_HEAD), BF16)]
    args = [q, do, lse, delta, kv, kv, kpe]
    scratch = []
    if fused:
        in_specs.append(_ANY)
        out_specs.append(_ANY)
        out_shape.append(_sds(exchange.shape, exchange.dtype))
        args.append(exchange)
        scratch = _CHIP_EXCHANGE_SEMS
    return pl.pallas_call(body, name=name, grid=(nh, nb), in_specs=in_specs, out_specs=out_specs, out_shape=out_shape,
                          scratch_shapes=scratch, compiler_params=_cp("arbitrary", "arbitrary"))(*args)


def _fill_prev(buf, x_ref, halo_ref, i, tr):
    buf[pl.ds(0, HALO), :] = jnp.where(i > 0, halo_ref[...], 0.0)
    buf[pl.ds(HALO, tr), :] = x_ref[...]


def _conv_prev(buf, w_ref, kw, tr):
    acc = w_ref[kw - 1:kw, :] * buf[pl.ds(HALO, tr), :]
    for k in range(kw - 1):
        acc = acc + w_ref[k:k + 1, :] * buf[pl.ds(HALO - kw + 1 + k, tr), :]
    return acc


def _conv_dw(buf, dc, kw, tr):
    rows = [jnp.sum(dc * buf[pl.ds(HALO - kw + 1 + k, tr), :], axis=0, keepdims=True) for k in range(kw)]
    return jnp.concatenate(rows, axis=0)


def _conv_next(buf, dc_ref, halo_ref, w_ref, kw, i, n_tiles, tr):
    buf[pl.ds(0, tr), :] = dc_ref[...]
    buf[pl.ds(tr, HALO), :] = jnp.where(i < n_tiles - 1, halo_ref[...], 0.0)
    acc = w_ref[kw - 1:kw, :] * buf[pl.ds(0, tr), :]
    for k in range(kw - 1):
        acc = acc + w_ref[k:k + 1, :] * buf[pl.ds(kw - 1 - k, tr), :]
    return acc


def _split3(x):
    x1 = x.astype(BF16)
    r1 = x - x1.astype(F32)
    x2 = r1.astype(BF16)
    x3 = (r1 - x2.astype(F32)).astype(BF16)
    return x1, x2, x3


def _dot3(parts, m, left):
    tot = None
    for p in parts:
        r = jnp.dot(m, p, preferred_element_type=F32) if left else jnp.dot(p, m, preferred_element_type=F32)
        tot = r if tot is None else tot + r
    return tot


def _ssd_prep_fwd(proj, conv_w, conv_b, dt_bias, expand, *, name):
    t = proj.shape[0]
    tr = _tile(t, 128)
    nt = t // tr
    hb = tr // HALO
    cw = SSD_CONV_DIM
    cb_x = OFF_XBC // cw
    cb_dt = OFF_DT // LANES

    def body(x_ref, halo_ref, dtr_ref, w_ref, b_ref, dtb_ref, e_ref, xs_ref, bm_ref, cm_ref, dtx_ref, buf):
        i = pl.program_id(0)
        _fill_prev(buf, x_ref, halo_ref, i, tr)
        conv = _conv_prev(buf, w_ref, SSD_CONV, tr) + b_ref[...]
        rows = _row_ids(i, tr, (tr, 1))
        live = rows >= PAD
        act = jnp.where(live, conv * _sigmoid(conv), 0.0)
        xs_ref[...] = act[:, :SSD_INNER]
        bm_ref[...] = act[:, SSD_INNER:SSD_INNER + GROUP_W]
        cm_ref[...] = act[:, SSD_INNER + GROUP_W:]
        dt = jnp.where(live, jax.nn.softplus(dtr_ref[...] + dtb_ref[...]), 0.0)
        dtx_ref[...] = _dot3(_split3(dt), e_ref[...], left=False)

    return pl.pallas_call(
        body, name=name, grid=(nt,),
        in_specs=[pl.BlockSpec((tr, cw), lambda i: (i, cb_x)),
                  pl.BlockSpec((HALO, cw), lambda i: (jnp.maximum(i * hb - 1, 0), cb_x)),
                  pl.BlockSpec((tr, LANES), lambda i: (i, cb_dt)),
                  pl.BlockSpec((SSD_CONV, cw), lambda i: (0, 0)),
                  pl.BlockSpec((1, cw), lambda i: (0, 0)),
                  pl.BlockSpec((1, LANES), lambda i: (0, 0)),
                  pl.BlockSpec((LANES, SSD_INNER), lambda i: (0, 0))],
        out_specs=[pl.BlockSpec((tr, SSD_INNER), lambda i: (i, 0)), pl.BlockSpec((tr, GROUP_W), lambda i: (i, 0)),
                   pl.BlockSpec((tr, GROUP_W), lambda i: (i, 0)), pl.BlockSpec((tr, SSD_INNER), lambda i: (i, 0))],
        out_shape=[_sds((t, SSD_INNER), F32), _sds((t, GROUP_W), F32), _sds((t, GROUP_W), F32),
                   _sds((t, SSD_INNER), F32)],
        scratch_shapes=[pltpu.VMEM((tr + HALO, cw), F32)],
        compiler_params=_cp("parallel"))(proj, proj, proj, conv_w, conv_b, dt_bias, expand)


def _ssd_prep_bwd_a(proj, dxs, dbm, dcm, ddtx, conv_w, conv_b, dt_bias, reduce_m, *, name):
    t = proj.shape[0]
    tr = _tile(t, 128)
    nt = t // tr
    hb = tr // HALO
    cw = SSD_CONV_DIM
    cb_x = OFF_XBC // cw
    cb_dt = OFF_DT // LANES

    def body(x_ref, halo_ref, dtr_ref, dxs_ref, dbm_ref, dcm_ref, ddtx_ref, w_ref, b_ref, dtb_ref, r_ref,
             dconv_ref, ddtr_ref, dw_ref, db_ref, ddtb_ref, buf):
        i = pl.program_id(0)
        _fill_prev(buf, x_ref, halo_ref, i, tr)
        conv = _conv_prev(buf, w_ref, SSD_CONV, tr) + b_ref[...]
        rows = _row_ids(i, tr, (tr, 1))
        live = rows >= PAD
        sg = _sigmoid(conv)
        dact = jnp.concatenate([dxs_ref[...], dbm_ref[...], dcm_ref[...]], axis=1)
        dconv = jnp.where(live, dact * (sg * (1.0 + conv * (1.0 - sg))), 0.0)
        dconv_ref[...] = dconv
        pw = _conv_dw(buf, dconv, SSD_CONV, tr)
        pb = jnp.sum(dconv, axis=0, keepdims=True)
        ddt = _dot3(_split3(ddtx_ref[...]), r_ref[...], left=False)
        ddtr = jnp.where(live, ddt * _sigmoid(dtr_ref[...] + dtb_ref[...]), 0.0)
        ddtr_ref[...] = ddtr.astype(BF16)
        pdb = jnp.sum(ddtr, axis=0, keepdims=True)

        @pl.when(i == 0)
        def _():
            dw_ref[...] = pw
            db_ref[...] = pb
            ddtb_ref[...] = pdb

        @pl.when(i > 0)
        def _():
            dw_ref[...] += pw
            db_ref[...] += pb
            ddtb_ref[...] += pdb

    return pl.pallas_call(
        body, name=name, grid=(nt,),
        in_specs=[pl.BlockSpec((tr, cw), lambda i: (i, cb_x)),
                  pl.BlockSpec((HALO, cw), lambda i: (jnp.maximum(i * hb - 1, 0), cb_x)),
                  pl.BlockSpec((tr, LANES), lambda i: (i, cb_dt)),
                  pl.BlockSpec((tr, SSD_INNER), lambda i: (i, 0)),
                  pl.BlockSpec((tr, GROUP_W), lambda i: (i, 0)),
                  pl.BlockSpec((tr, GROUP_W), lambda i: (i, 0)),
                  pl.BlockSpec((tr, SSD_INNER), lambda i: (i, 0)),
                  pl.BlockSpec((SSD_CONV, cw), lambda i: (0, 0)),
                  pl.BlockSpec((1, cw), lambda i: (0, 0)),
                  pl.BlockSpec((1, LANES), lambda i: (0, 0)),
                  pl.BlockSpec((SSD_INNER, LANES), lambda i: (0, 0))],
        out_specs=[pl.BlockSpec((tr, cw), lambda i: (i, 0)), pl.BlockSpec((tr, LANES), lambda i: (i, 0)),
                   pl.BlockSpec((SSD_CONV, cw), lambda i: (0, 0)), pl.BlockSpec((1, cw), lambda i: (0, 0)),
                   pl.BlockSpec((1, LANES), lambda i: (0, 0))],
        out_shape=[_sds((t, cw), F32), _sds((t, LANES), BF16), _sds((SSD_CONV, cw), F32), _sds((1, cw), F32),
                   _sds((1, LANES), F32)],
        scratch_shapes=[pltpu.VMEM((tr + HALO, cw), F32)],
        compiler_params=_cp("arbitrary"))(proj, proj, proj, dxs, dbm, dcm, ddtx, conv_w, conv_b, dt_bias, reduce_m)


def _conv_bwd_input(dconv, w, kw, *, name, out_dtype=BF16, tc=None):
    t, c = dconv.shape
    tr = _tile(t, 128)
    nt = t // tr
    hb = tr // HALO
    tc = _tile(c, tc or c)
    last_hb = t // HALO - 1

    def body(dc_ref, halo_ref, w_ref, o_ref, buf):
        i = pl.program_id(0)
        o_ref[...] = _conv_next(buf, dc_ref, halo_ref, w_ref, kw, i, nt, tr).astype(out_dtype)

    return pl.pallas_call(
        body, name=name, grid=(nt, c // tc),
        in_specs=[pl.BlockSpec((tr, tc), lambda i, j: (i, j)),
                  pl.BlockSpec((HALO, tc), lambda i, j: (jnp.minimum((i + 1) * hb, last_hb), j)),
                  pl.BlockSpec((kw, tc), lambda i, j: (0, j))],
        out_specs=pl.BlockSpec((tr, tc), lambda i, j: (i, j)), out_shape=_sds((t, c), out_dtype),
        scratch_shapes=[pltpu.VMEM((tr + HALO, tc), F32)],
        compiler_params=_cp("parallel", "parallel"))(dconv, dconv, w)


def _ffn_act_fwd(ug, uv, wg, wv, bg, bv, *, name):
    t, c = ug.shape
    tr = _tile(t, 128)
    hb = tr // HALO
    tc = _tile(c, 1408)

    def body(ug_ref, hg_ref, uv_ref, hv_ref, wg_ref, wv_ref, bg_ref, bv_ref, o_ref, bufg, bufv):
        i = pl.program_id(0)
        _fill_prev(bufg, ug_ref, hg_ref, i, tr)
        _fill_prev(bufv, uv_ref, hv_ref, i, tr)
        cg = _conv_prev(bufg, wg_ref, FFN_CONV, tr) + bg_ref[...]
        cv = _conv_prev(bufv, wv_ref, FFN_CONV, tr) + bv_ref[...]
        o_ref[...] = (cg * _sigmoid(cg) * cv).astype(BF16)

    blk = pl.BlockSpec((tr, tc), lambda i, j: (i, j))
    halo = pl.BlockSpec((HALO, tc), lambda i, j: (jnp.maximum(i * hb - 1, 0), j))
    wsp = pl.BlockSpec((FFN_CONV, tc), lambda i, j: (0, j))
    bsp = pl.BlockSpec((1, tc), lambda i, j: (0, j))
    return pl.pallas_call(
        body, name=name, grid=(t // tr, c // tc), in_specs=[blk, halo, blk, halo, wsp, wsp, bsp, bsp],
        out_specs=blk, out_shape=_sds((t, c), BF16),
        scratch_shapes=[pltpu.VMEM((tr + HALO, tc), F32), pltpu.VMEM((tr + HALO, tc), F32)],
        compiler_params=_cp("parallel", "parallel"))(ug, ug, uv, uv, wg, wv, bg, bv)


def _ffn_act_bwd(ug, uv, dact, wg, wv, bg, bv, *, name):
    t, c = ug.shape
    tr = _tile(t, 128)
    hb = tr // HALO
    tc = _tile(c, 1408)

    def body(ug_ref, hg_ref, uv_ref, hv_ref, da_ref, wg_ref, wv_ref, bg_ref, bv_ref,
             dcg_ref, dcv_ref, dwg_ref, dwv_ref, dbg_ref, dbv_ref, bufg, bufv):
        i = pl.program_id(1)
        _fill_prev(bufg, ug_ref, hg_ref, i, tr)
        _fill_prev(bufv, uv_ref, hv_ref, i, tr)
        cg = _conv_prev(bufg, wg_ref, FFN_CONV, tr) + bg_ref[...]
        cv = _conv_prev(bufv, wv_ref, FFN_CONV, tr) + bv_ref[...]
        sg = _sigmoid(cg)
        da = da_ref[...]
        dcg = da * cv * (sg * (1.0 + cg * (1.0 - sg)))
        dcv = da * (cg * sg)
        dcg_ref[...] = dcg
        dcv_ref[...] = dcv
        pwg = _conv_dw(bufg, dcg, FFN_CONV, tr)
        pwv = _conv_dw(bufv, dcv, FFN_CONV, tr)
        pbg = jnp.sum(dcg, axis=0, keepdims=True)
        pbv = jnp.sum(dcv, axis=0, keepdims=True)

        @pl.when(i == 0)
        def _():
            dwg_ref[...] = pwg
            dwv_ref[...] = pwv
            dbg_ref[...] = pbg
            dbv_ref[...] = pbv

        @pl.when(i > 0)
        def _():
            dwg_ref[...] += pwg
            dwv_ref[...] += pwv
            dbg_ref[...] += pbg
            dbv_ref[...] += pbv

    blk = pl.BlockSpec((tr, tc), lambda j, i: (i, j))
    halo = pl.BlockSpec((HALO, tc), lambda j, i: (jnp.maximum(i * hb - 1, 0), j))
    wsp = pl.BlockSpec((FFN_CONV, tc), lambda j, i: (0, j))
    bsp = pl.BlockSpec((1, tc), lambda j, i: (0, j))
    return pl.pallas_call(
        body, name=name, grid=(c // tc, t // tr), in_specs=[blk, halo, blk, halo, blk, wsp, wsp, bsp, bsp],
        out_specs=[blk, blk, wsp, wsp, bsp, bsp],
        out_shape=[_sds((t, c), F32), _sds((t, c), F32), _sds((FFN_CONV, c), F32), _sds((FFN_CONV, c), F32),
                   _sds((1, c), F32), _sds((1, c), F32)],
        scratch_shapes=[pltpu.VMEM((tr + HALO, tc), F32), pltpu.VMEM((tr + HALO, tc), F32)],
        compiler_params=_cp("parallel", "arbitrary"))(ug, ug, uv, uv, dact, wg, wv, bg, bv)


def _tri(lower):
    li = lax.broadcasted_iota(jnp.int32, (CHUNK, CHUNK), 0)
    si = lax.broadcasted_iota(jnp.int32, (CHUNK, CHUNK), 1)
    return li >= si if lower else li <= si


def _tri_ones(lower):
    return jnp.where(_tri(lower), 1.0, 0.0).astype(BF16)


def _decay_pair(acs, acs_t, lane0):
    col = acs[:, lane0:lane0 + 1]
    row = acs_t[lane0:lane0 + 1, :]
    low = jnp.where(_tri(True), jnp.exp(jnp.minimum(col - row, 0.0)), 0.0)
    upp = jnp.where(_tri(False), jnp.exp(jnp.minimum(row - col, 0.0)), 0.0)
    return low, upp


def _ssd_fwd(xs, dtx, bm, cm, bm_t, a_x, d_x, *, name):
    t = xs.shape[0]
    nc = t // CHUNK
    gw = GROUP_W

    def body(xs_ref, dt_ref, b_ref, c_ref, bt_ref, a_ref, d_ref, y_ref, prev_ref, h_s):
        @pl.when(pl.program_id(1) == 0)
        def _():
            h_s[...] = jnp.zeros((SSD_STATE, gw), F32)

        x = xs_ref[...]
        dt = dt_ref[...]
        acs = _dot3(_split3(dt * a_ref[...]), _tri_ones(True), left=True)
        acs_t = acs.T
        xc = x * dt
        bv = b_ref[...].astype(BF16)
        cv = c_ref[...].astype(BF16)
        cb = lax.dot_general(cv, bv, NT_DIMS, preferred_element_type=F32)
        lane = lax.broadcasted_iota(jnp.int32, (CHUNK, LANES), 1)
        pieces = []
        for pp in range(gw // LANES):
            xcp = xc[:, pp * LANES:(pp + 1) * LANES]
            acc = jnp.zeros((CHUNK, LANES), F32)
            for e in range(2):
                low, _ = _decay_pair(acs, acs_t, pp * LANES + e * SSD_HEAD_DIM)
                mine = (lane >= e * SSD_HEAD_DIM) & (lane < (e + 1) * SSD_HEAD_DIM)
                xm = jnp.where(mine, xcp, 0.0).astype(BF16)
                acc = acc + jnp.dot((cb * low).astype(BF16), xm, preferred_element_type=F32)
            pieces.append(acc)
        y_diag = jnp.concatenate(pieces, axis=1)
        h_prev = h_s[...]
        y_off = jnp.dot(cv, h_prev.astype(BF16), preferred_element_type=F32) * jnp.exp(acs)
        y_ref[...] = y_diag + y_off + d_ref[...] * x
        prev_ref[0] = h_prev
        last = acs[CHUNK - 1:CHUNK, :]
        w = jnp.exp(last - acs)
        st = jnp.dot(bt_ref[...].astype(BF16), (xc * w).astype(BF16), preferred_element_type=F32)
        h_s[...] = h_prev * jnp.exp(last) + st

    tok = pl.BlockSpec((CHUNK, gw), lambda g, c: (c, g))
    grp = pl.BlockSpec((CHUNK, SSD_STATE), lambda g, c: (c, g))
    vec = pl.BlockSpec((1, gw), lambda g, c: (0, g))
    return pl.pallas_call(
        body, name=name, grid=(SSD_GROUPS, nc),
        in_specs=[tok, tok, grp, grp, pl.BlockSpec((SSD_STATE, CHUNK), lambda g, c: (g, c)), vec, vec],
        out_specs=[tok, pl.BlockSpec((1, SSD_STATE, gw), lambda g, c: (c, 0, g))],
        out_shape=[_sds((t, SSD_INNER), F32), _sds((nc, SSD_STATE, SSD_INNER), F32)],
        scratch_shapes=[pltpu.VMEM((SSD_STATE, gw), F32)],
        compiler_params=_cp("parallel", "arbitrary"))(xs, dtx, bm, cm, bm_t, a_x, d_x)


def _ssd_bwd(xs, dtx, bm, cm, cm_t, prev, dy, a_x, d_x, *, name):
    t = xs.shape[0]
    nc = t // CHUNK
    gw = GROUP_W

    def body(xs_ref, dt_ref, b_ref, c_ref, ct_ref, prev_ref, dy_ref, a_ref, d_ref,
             dxs_ref, ddt_ref, db_ref, dc_ref, da_ref, dd_ref, g_s):
        first = pl.program_id(1) == 0

        @pl.when(first)
        def _():
            g_s[...] = jnp.zeros((SSD_STATE, gw), F32)

        x = xs_ref[...]
        dt = dt_ref[...]
        a = a_ref[...]
        dyv = dy_ref[...]
        acs = _dot3(_split3(dt * a), _tri_ones(True), left=True)
        acs_t = acs.T
        xc = x * dt
        bv = b_ref[...].astype(BF16)
        cv = c_ref[...].astype(BF16)
        cb = lax.dot_general(cv, bv, NT_DIMS, preferred_element_type=F32)
        cb_t = lax.dot_general(bv, cv, NT_DIMS, preferred_element_type=F32)
        last = acs[CHUNK - 1:CHUNK, :]
        w = jnp.exp(last - acs)
        cd = jnp.exp(last)
        p_in = prev_ref[0]
        p_b = p_in.astype(BF16)
        g_out = g_s[...]
        g_b = g_out.astype(BF16)
        dy_e = dyv * jnp.exp(acs)
        dy_eb = dy_e.astype(BF16)
        y_off_raw = jnp.dot(cv, p_b, preferred_element_type=F32)
        dacs = dy_e * y_off_raw
        d_c = lax.dot_general(dy_eb, p_b, NT_DIMS, preferred_element_type=F32)
        d_prev = jnp.dot(ct_ref[...].astype(BF16), dy_eb, preferred_element_type=F32)
        q_l = jnp.dot(bv, g_b, preferred_element_type=F32)
        dxc = w * q_l
        tw = xc * q_l * w
        dacs = dacs - tw
        d_b = lax.dot_general((xc * w).astype(BF16), g_b, NT_DIMS, preferred_element_type=F32)
        last_add = jnp.sum(tw, axis=0, keepdims=True) + cd * jnp.sum(g_out * p_in, axis=0, keepdims=True)
        g_s[...] = cd * g_out + d_prev
        lane = lax.broadcasted_iota(jnp.int32, (CHUNK, LANES), 1)
        d_cb = jnp.zeros((CHUNK, CHUNK), F32)
        d_cb_t = jnp.zeros((CHUNK, CHUNK), F32)
        dxc_pieces, dacs_pieces = [], []
        for pp in range(gw // LANES):
            xcp = xc[:, pp * LANES:(pp + 1) * LANES]
            dyp = dyv[:, pp * LANES:(pp + 1) * LANES]
            dxcp = jnp.zeros((CHUNK, LANES), F32)
            dacsp = jnp.zeros((CHUNK, LANES), F32)
            for e in range(2):
                low, upp = _decay_pair(acs, acs_t, pp * LANES + e * SSD_HEAD_DIM)
                mine = (lane >= e * SSD_HEAD_DIM) & (lane < (e + 1) * SSD_HEAD_DIM)
                m_low = cb * low
                m_upp = cb_t * upp
                dym = jnp.where(mine, dyp, 0.0).astype(BF16)
                xm = jnp.where(mine, xcp, 0.0).astype(BF16)
                dxcp = dxcp + jnp.dot(m_upp.astype(BF16), dym, preferred_element_type=F32)
                d_m = lax.dot_general(dym, xm, NT_DIMS, preferred_element_type=F32)
                d_m_t = lax.dot_general(xm, dym, NT_DIMS, preferred_element_type=F32)
                rs = jnp.sum(d_m * m_low, axis=1, keepdims=True)
                cs = jnp.sum(d_m_t * m_upp, axis=1, keepdims=True)
                dacsp = dacsp + jnp.where(lane == e * SSD_HEAD_DIM, rs - cs, 0.0)
                d_cb = d_cb + d_m * low
                d_cb_t = d_cb_t + d_m_t * upp
            dxc_pieces.append(dxcp)
            dacs_pieces.append(dacsp)
        dxc = dxc + jnp.concatenate(dxc_pieces, axis=1)
        dacs = dacs + jnp.concatenate(dacs_pieces, axis=1)
        rowi = lax.broadcasted_iota(jnp.int32, (CHUNK, gw), 0)
        dacs = dacs + jnp.where(rowi == CHUNK - 1, last_add, 0.0)
        dc_ref[...] = d_c + jnp.dot(d_cb.astype(BF16), bv, preferred_element_type=F32)
        db_ref[...] = d_b + jnp.dot(d_cb_t.astype(BF16), cv, preferred_element_type=F32)
        dda = _dot3(_split3(dacs), _tri_ones(False), left=True)
        ddt_ref[...] = dda * a + dxc * x
        dxs_ref[...] = dxc * dt + d_ref[...] * dyv
        pa = jnp.sum(dda * dt, axis=0, keepdims=True)
        pd = jnp.sum(dyv * x, axis=0, keepdims=True)

        @pl.when(first)
        def _():
            da_ref[...] = pa
            dd_ref[...] = pd

        @pl.when(jnp.logical_not(first))
        def _():
            da_ref[...] += pa
            dd_ref[...] += pd

    rc = lambda c: nc - 1 - c
    tok = pl.BlockSpec((CHUNK, gw), lambda g, c: (rc(c), g))
    grp = pl.BlockSpec((CHUNK, SSD_STATE), lambda g, c: (rc(c), g))
    vec = pl.BlockSpec((1, gw), lambda g, c: (0, g))
    return pl.pallas_call(
        body, name=name, grid=(SSD_GROUPS, nc),
        in_specs=[tok, tok, grp, grp, pl.BlockSpec((SSD_STATE, CHUNK), lambda g, c: (g, rc(c))),
                  pl.BlockSpec((1, SSD_STATE, gw), lambda g, c: (rc(c), 0, g)), tok, vec, vec],
        out_specs=[tok, tok, grp, grp, vec, vec],
        out_shape=[_sds((t, SSD_INNER), F32), _sds((t, SSD_INNER), F32), _sds((t, gw), F32), _sds((t, gw), F32),
                   _sds((1, SSD_INNER), F32), _sds((1, SSD_INNER), F32)],
        scratch_shapes=[pltpu.VMEM((SSD_STATE, gw), F32)],
        compiler_params=_cp("parallel", "arbitrary"))(xs, dtx, bm, cm, cm_t, prev, dy, a_x, d_x)


def _gnorm_fwd(y, proj, g, *, name):
    t = y.shape[0]
    tr = _tile(t, 640)
    zb = OFF_Z // GROUP_W

    def body(y_ref, z_ref, g_ref, o_ref):
        z = z_ref[...]
        v = y_ref[...] * (z * _sigmoid(z))
        r = lax.rsqrt(jnp.mean(v * v, axis=1, keepdims=True) + RMS_EPS)
        o_ref[...] = (v * r * g_ref[...]).astype(BF16)

    blk = pl.BlockSpec((tr, GROUP_W), lambda i, j: (i, j))
    return pl.pallas_call(
        body, name=name, grid=(t // tr, SSD_GROUPS),
        in_specs=[blk, pl.BlockSpec((tr, GROUP_W), lambda i, j: (i, zb + j)),
                  pl.BlockSpec((1, GROUP_W), lambda i, j: (0, j))],
        out_specs=blk, out_shape=_sds((t, SSD_INNER), BF16),
        compiler_params=_cp("parallel", "parallel"))(y, proj, g)


def _gnorm_bwd(dout, y, proj, g, *, name):
    t = y.shape[0]
    tr = _tile(t, 640)
    zb = OFF_Z // GROUP_W

    def body(do_ref, y_ref, z_ref, g_ref, dy_ref, dz_ref, dg_ref):
        i = pl.program_id(1)
        z = z_ref[...]
        yv = y_ref[...]
        sg = _sigmoid(z)
        sz = z * sg
        v = yv * sz
        r = lax.rsqrt(jnp.mean(v * v, axis=1, keepdims=True) + RMS_EPS)
        dov = do_ref[...].astype(F32)
        gdo = dov * g_ref[...]
        m = jnp.mean(v * gdo, axis=1, keepdims=True)
        dv = r * gdo - v * (r * r * r) * m
        dy_ref[...] = dv * sz
        dz_ref[...] = (dv * yv * (sg * (1.0 + z * (1.0 - sg)))).astype(BF16)
        pg = jnp.sum(dov * v * r, axis=0, keepdims=True)

        @pl.when(i == 0)
        def _():
            dg_ref[...] = pg

        @pl.when(i > 0)
        def _():
            dg_ref[...] += pg

    blk = pl.BlockSpec((tr, GROUP_W), lambda j, i: (i, j))
    vec = pl.BlockSpec((1, GROUP_W), lambda j, i: (0, j))
    return pl.pallas_call(
        body, name=name, grid=(SSD_GROUPS, t // tr),
        in_specs=[blk, blk, pl.BlockSpec((tr, GROUP_W), lambda j, i: (i, zb + j)), vec],
        out_specs=[blk, blk, vec],
        out_shape=[_sds((t, SSD_INNER), F32), _sds((t, SSD_INNER), BF16), _sds((1, SSD_INNER), F32)],
        compiler_params=_cp("parallel", "arbitrary"))(dout, y, proj, g)


def _mix_fwd(proj, ya, ys, *, name):
    t, d = ya.shape
    tr = _tile(t, 640)
    ba, bs = OFF_GA // d, OFF_GS // d

    def body(ga_ref, gs_ref, ya_ref, ys_ref, o_ref):
        o_ref[...] = (_sigmoid(ga_ref[...]) * ya_ref[...] + _sigmoid(gs_ref[...]) * ys_ref[...]).astype(BF16)

    blk = pl.BlockSpec((tr, d), lambda i: (i, 0))
    return pl.pallas_call(
        body, name=name, grid=(t // tr,),
        in_specs=[pl.BlockSpec((tr, d), lambda i: (i, ba)), pl.BlockSpec((tr, d), lambda i: (i, bs)), blk, blk],
        out_specs=blk, out_shape=_sds((t, d), BF16), compiler_params=_cp("parallel"))(proj, proj, ya, ys)


def _mix_bwd(dmix, proj, ya, ys, *, name):
    t, d = ya.shape
    tr = _tile(t, 640)
    ba, bs = OFF_GA // d, OFF_GS // d

    def body(dm_ref, ga_ref, gs_ref, ya_ref, ys_ref, dya_ref, dys_ref, dga_ref, dgs_ref):
        dm = dm_ref[...]
        sa = _sigmoid(ga_ref[...])
        ss = _sigmoid(gs_ref[...])
        dya_ref[...] = (sa * dm).astype(BF16)
        dys_ref[...] = (ss * dm).astype(BF16)
        dga_ref[...] = (dm * ya_ref[...] * sa * (1.0 - sa)).astype(BF16)
        dgs_ref[...] = (dm * ys_ref[...] * ss * (1.0 - ss)).astype(BF16)

    blk = pl.BlockSpec((tr, d), lambda i: (i, 0))
    return pl.pallas_call(
        body, name=name, grid=(t // tr,),
        in_specs=[blk, pl.BlockSpec((tr, d), lambda i: (i, ba)), pl.BlockSpec((tr, d), lambda i: (i, bs)), blk, blk],
        out_specs=[blk] * 4, out_shape=[_sds((t, d), BF16)] * 4,
        compiler_params=_cp("parallel"))(dmix, proj, proj, ya, ys)


def _loss_grad(h, target, *, name):
    t, d = h.shape
    tr = LANES
    assert ROW0 == tr

    def body(h_ref, t_ref, dh_ref, loss_ref):
        i = pl.program_id(0)

        @pl.when(i == 0)
        def _():
            dh_ref[...] = jnp.zeros((tr, d), F32)
            loss_ref[...] = jnp.zeros((1, LANES), F32)

        @pl.when(i > 0)
        def _():
            err = h_ref[...] - t_ref[...]
            dh_ref[...] = err * (1.0 / d)
            part = jnp.sum(jnp.sum(err * err, axis=1, keepdims=True), axis=0, keepdims=True)
            loss_ref[...] += jnp.broadcast_to(part * (0.5 / d), (1, LANES))

    blk = pl.BlockSpec((tr, d), lambda i: (i, 0))
    return pl.pallas_call(
        body, name=name, grid=(t // tr,),
        in_specs=[blk, pl.BlockSpec((tr, d), lambda i: (jnp.maximum(i - 1, 0), 0))],
        out_specs=[blk, pl.BlockSpec((1, LANES), lambda i: (0, 0))],
        out_shape=[_sds((t, d), F32), _sds((1, LANES), F32)],
        compiler_params=_cp("arbitrary"))(h, target)


def _adamw_update(gv, wv, mv, vv):
    c1 = 1.0 - ADAM_B1 ** ADAM_STEP
    c2 = 1.0 - ADAM_B2 ** ADAM_STEP
    nm = ADAM_B1 * mv + (1.0 - ADAM_B1) * gv
    nv = ADAM_B2 * vv + (1.0 - ADAM_B2) * (gv * gv)
    return -ADAM_LR * ((nm / c1) / (jnp.sqrt(nv / c2) + ADAM_EPS) + ADAM_WD * wv), nm, nv


def _as_2d(a):
    return a.reshape(1, -1) if a.ndim == 1 else a.reshape(-1, a.shape[-1])


def _adamw(g, w, m, v, *, name):
    shape = w.shape
    g2, w2, m2, v2 = (_as_2d(a) for a in (g, w, m, v))
    r, c = w2.shape
    tr = _tile(r, 256, base=8)

    def body(g_ref, w_ref, m_ref, v_ref, d_ref, nm_ref, nv_ref):
        d_ref[...], nm_ref[...], nv_ref[...] = _adamw_update(g_ref[...], w_ref[...], m_ref[...], v_ref[...])

    blk = pl.BlockSpec((tr, c), lambda i: (i, 0))
    outs = pl.pallas_call(body, name=name, grid=(r // tr,), in_specs=[blk] * 4, out_specs=[blk] * 3,
                          out_shape=[_sds((r, c), F32)] * 3, compiler_params=_cp("parallel"))(g2, w2, m2, v2)
    return [o.reshape(shape) for o in outs]


def _adamw_small(items, *, name):
    n = len(items)
    shapes = [it[1].shape for it in items]
    flat = [_as_2d(a) for it in items for a in it]

    def body(*refs):
        ins, outs = refs[:4 * n], refs[4 * n:]
        for k in range(n):
            g_ref, w_ref, m_ref, v_ref = ins[4 * k:4 * k + 4]
            d_ref, nm_ref, nv_ref = outs[3 * k:3 * k + 3]
            d_ref[...], nm_ref[...], nv_ref[...] = _adamw_update(g_ref[...], w_ref[...], m_ref[...], v_ref[...])

    out_shape = [_sds(flat[4 * k + 1].shape, F32) for k in range(n) for _ in range(3)]
    outs = pl.pallas_call(body, name=name, out_shape=out_shape,
                          compiler_params=pltpu.CompilerParams(vmem_limit_bytes=VMEM_LIMIT_BYTES))(*flat)
    return [[outs[3 * k + q].reshape(shapes[k]) for q in range(3)] for k in range(n)]


def _pair_add(g5, got, core, *, name):
    n, _, r, _ = g5.shape
    tr = _tile(r, 1024, base=8)

    def body(c_ref, a_ref, b_ref, o_ref):
        o_ref[...] = a_ref[0] + b_ref[...]

    grid_spec = pltpu.PrefetchScalarGridSpec(
        num_scalar_prefetch=1, grid=(n, r // tr),
        in_specs=[pl.BlockSpec((1, 1, tr, LANES), lambda s, i, c_ref: (s, c_ref[0], i, 0)),
                  pl.BlockSpec((1, tr, LANES), lambda s, i, c_ref: (s, i, 0))],
        out_specs=pl.BlockSpec((1, tr, LANES), lambda s, i, c_ref: (s, i, 0)))
    return pl.pallas_call(body, name=name, grid_spec=grid_spec, out_shape=_sds(got.shape, F32),
                          compiler_params=_cp("parallel", "parallel"))(core, g5, got)


def _sum_chips(q, core, *, name):
    n, r, _ = q.shape
    tr = _tile(r, 1024, base=8)

    def body(c_ref, q_ref, o_ref):
        tot = q_ref[0]
        for s in range(1, n):
            tot = tot + q_ref[s]
        o_ref[0] = tot

    grid_spec = pltpu.PrefetchScalarGridSpec(
        num_scalar_prefetch=1, grid=(r // tr,),
        in_specs=[pl.BlockSpec((n, tr, LANES), lambda i, c_ref: (0, i, 0))],
        out_specs=pl.BlockSpec((1, tr, LANES), lambda i, c_ref: (c_ref[0], i, 0)))
    return pl.pallas_call(body, name=name, grid_spec=grid_spec, out_shape=_sds((2, r, LANES), F32),
                          compiler_params=_cp("parallel"))(core, q)


_ANY = pl.BlockSpec(memory_space=pl.ANY)
_MESH = pl.DeviceIdType.MESH


def _place():
    x, y, c = lax.axis_index("x"), lax.axis_index("y"), lax.axis_index("c")
    return x, y, c, [(1 - x, y), (x, 1 - y), (1 - x, 1 - y)]


def _chip_allgather(mine, *, name):
    na = len(mine)

    def body(*refs):
        for phase in range(3):
            _chip_allgather_phase(phase, refs[:na], refs[na:2 * na], *refs[2 * na:])

    return pl.pallas_call(
        body, name=name, in_specs=[_ANY] * na, out_specs=[_ANY] * na,
        out_shape=_chip_allgather_shapes(mine), scratch_shapes=_chip_allgather_sems(na))(*mine)


def _chip_allgather_shapes(mine):
    return [_sds((N_CHIPS,) + a.shape, a.dtype) for a in mine]


def _chip_allgather_sems(na):
    return [pltpu.SemaphoreType.DMA((6 * na,)), pltpu.SemaphoreType.DMA((6 * na,)), pltpu.SemaphoreType.DMA((na,))]


def _chip_allgather_phase(phase, x_refs, o_refs, send_sems, recv_sems, local_sems):
    na = len(x_refs)
    x, y, c, chips = _place()
    k = 2 * x + y

    def copy(a, n, src, dst, to):
        return pltpu.make_async_remote_copy(src_ref=src, dst_ref=dst, send_sem=send_sems.at[6 * a + n],
                                            recv_sem=recv_sems.at[6 * a + n], device_id=to, device_id_type=_MESH)

    locals_ = [pltpu.make_async_copy(x_refs[a], o_refs[a].at[k], local_sems.at[a]) for a in range(na)]
    sends = [copy(a, n, x_refs[a].at[c], o_refs[a].at[k, c], (cx, cy, c))
             for a in range(na) for n, (cx, cy) in enumerate(chips)]
    landed = [(copy(a, n, o_refs[a].at[2 * cx + cy, c], o_refs[a].at[2 * cx + cy, c], (cx, cy, c)),
               copy(a, 3 + n, o_refs[a].at[2 * cx + cy, c], o_refs[a].at[2 * cx + cy, c], (x, y, 1 - c)))
              for a in range(na) for n, (cx, cy) in enumerate(chips)]
    if phase == 0:
        for cp in locals_ + sends:
            cp.start()
    elif phase == 1:
        for arrival, forward in landed:
            arrival.wait_recv()
            forward.start()
    else:
        for a in range(na):
            for n, (cx, cy) in enumerate(chips):
                slab = o_refs[a].at[2 * cx + cy, 1 - c]
                copy(a, 3 + n, slab, slab, (x, y, 1 - c)).wait_recv()
        for cp in sends + [forward for _, forward in landed]:
            cp.wait_send()
        for cp in locals_:
            cp.wait()


def _sibling_swap(g5, *, name):
    n, _, r, _ = g5.shape

    def body(x_ref, o_ref, send_sems, recv_sems):
        x, y, c, _ = _place()
        cps = [pltpu.make_async_remote_copy(src_ref=x_ref.at[s, 1 - c], dst_ref=o_ref.at[s], send_sem=send_sems.at[s],
                                            recv_sem=recv_sems.at[s], device_id=(x, y, 1 - c), device_id_type=_MESH)
               for s in range(n)]
        for cp in cps:
            cp.start()
        for cp in cps:
            cp.wait()

    return pl.pallas_call(
        body, name=name, in_specs=[_ANY], out_specs=_ANY, out_shape=_sds((n, r, LANES), g5.dtype),
        scratch_shapes=[pltpu.SemaphoreType.DMA((n,)), pltpu.SemaphoreType.DMA((n,))])(g5)


_CHIP_EXCHANGE_SEMS = [pltpu.SemaphoreType.DMA((3,)), pltpu.SemaphoreType.DMA((3,)), pltpu.SemaphoreType.DMA]


def _chip_exchange_copies(h_ref, q_ref, send_sems, recv_sems, local_sem):
    x, y, c, chips = _place()
    k = 2 * x + y
    local = pltpu.make_async_copy(h_ref.at[k], q_ref.at[k], local_sem)
    sends, arrivals = [], []
    for n, (cx, cy) in enumerate(chips):
        kk = 2 * cx + cy
        mk = functools.partial(pltpu.make_async_remote_copy, src_ref=h_ref.at[kk], send_sem=send_sems.at[n],
                               recv_sem=recv_sems.at[n], device_id=(cx, cy, c), device_id_type=_MESH)
        sends.append(mk(dst_ref=q_ref.at[k]))
        arrivals.append(mk(dst_ref=q_ref.at[kk]))
    return local, sends, arrivals


def _chip_exchange_start(copies):
    local, sends, _ = copies
    local.start()
    for cp in sends:
        cp.start()


def _chip_exchange_wait(copies):
    local, sends, arrivals = copies
    for cp in arrivals:
        cp.wait_recv()
    for cp in sends:
        cp.wait_send()
    local.wait()


def _chip_exchange(h, *, name):
    def body(h_ref, q_ref, send_sems, recv_sems, local_sem):
        copies = _chip_exchange_copies(h_ref, q_ref, send_sems, recv_sems, local_sem)
        _chip_exchange_start(copies)
        _chip_exchange_wait(copies)

    return pl.pallas_call(body, name=name, in_specs=[_ANY], out_specs=_ANY, out_shape=_sds(h.shape, h.dtype),
                          scratch_shapes=_CHIP_EXCHANGE_SEMS)(h)


def _sibling_allgather(buf, *, name):
    def body(x_ref, o_ref, send_sem, recv_sem):
        x, y, c, _ = _place()
        cp = pltpu.make_async_remote_copy(src_ref=x_ref.at[c], dst_ref=o_ref.at[c], send_sem=send_sem,
                                          recv_sem=recv_sem, device_id=(x, y, 1 - c), device_id_type=_MESH)
        cp.start()
        pltpu.make_async_remote_copy(src_ref=x_ref.at[c], dst_ref=o_ref.at[1 - c], send_sem=send_sem,
                                     recv_sem=recv_sem, device_id=(x, y, 1 - c), device_id_type=_MESH).wait_recv()
        cp.wait_send()

    return pl.pallas_call(
        body, name=name, in_specs=[_ANY], out_specs=_ANY, out_shape=_sds(buf.shape, buf.dtype),
        input_output_aliases={0: 0},
        scratch_shapes=[pltpu.SemaphoreType.DMA, pltpu.SemaphoreType.DMA])(buf)


def _chip_segments(g, kind):
    if kind == "col":
        n = g.shape[-1] // N_CHIPS
        s = g.reshape(g.shape[:-1] + (N_CHIPS, n))
        return jnp.moveaxis(s, -2, 0).reshape(N_CHIPS, -1)
    k = g.shape[-2] // N_CHIPS
    s = g.reshape(g.shape[:-2] + (N_CHIPS, k, g.shape[-1]))
    return jnp.moveaxis(s, -3, 0).reshape(N_CHIPS, -1)


def _join(blocks, kind):
    if kind == "col":
        s = jnp.moveaxis(blocks, 0, -2)
        return s.reshape(s.shape[:-2] + (s.shape[-2] * s.shape[-1],))
    return blocks.reshape((blocks.shape[0] * blocks.shape[1],) + blocks.shape[2:])


def _travel_form(local, names):
    mine = []
    for n in names:
        a = local[n]
        if n == "w_in":
            a = jnp.swapaxes(a, 1, 2)
        if n == "meta_tokens":
            a = a.reshape(2, N_META // 2, a.shape[-1])
        mine.append(a.astype(BF16) if n in GATHER_BF16 else a)
    return mine


def _rope_tables(t):
    half = QK_ROPE // 2
    inv_freq = 1.0 / (ROPE_THETA ** (jnp.arange(0, QK_ROPE, 2, dtype=F32) / QK_ROPE))
    pos = jnp.maximum(jnp.arange(t, dtype=F32) - PAD, 0.0)
    ang = pos[:, None] * inv_freq[None, :]
    cos, sin = jnp.cos(ang), jnp.sin(ang)
    z = jnp.zeros((t, half), F32)
    z2 = jnp.zeros((t, LANES - QK_ROPE), F32)
    return (jnp.concatenate([cos, cos, z2], axis=1), jnp.concatenate([-sin, z, z2], axis=1),
            jnp.concatenate([z, sin, z2], axis=1))


def _expand_matrix():
    lane = np.arange(SSD_INNER) // SSD_HEAD_DIM
    e = (np.arange(LANES)[:, None] == lane[None, :]).astype(np.float32)
    return jnp.asarray(e, BF16)


def _late_weights(full, i):
    w = {}
    kinds = dict(SHARDED)
    whole = lambda n: _join(full[n][:, i], kinds[n])
    for n in ("w_o_attn", "w_o_ssd", "w_out", "w_down"):
        w[n] = whole(n)
    w_up = whole("w_up")
    w["w_up_g"] = w_up[:, :D_FF]
    w["w_up_v"] = w_up[:, D_FF:]
    ffn_w = whole("ffn_conv_w")
    w["ffn_conv_wg"] = ffn_w[:, :D_FF]
    w["ffn_conv_wv"] = ffn_w[:, D_FF:]
    return w


def _early_weights(full, rep, i):
    w = {}
    kinds = dict(SHARDED)
    whole = lambda n: _join(full[n][:, i], kinds[n])
    wt = _join(full["w_in"][:, i], "row")
    zr = lambda n: jnp.zeros((n, D_MODEL), BF16)
    w["w_in_t"] = jnp.concatenate(
        [wt[0:1024], wt[1088:3136], wt[3136:6208], wt[6240:7264], wt[7264:8288],
         wt[1024:1088], zr(LANES - QK_ROPE), wt[6208:6240], zr(LANES - SSD_HEADS)], axis=0)
    wq = whole("w_q_b").reshape(Q_LORA, MLA_HEADS, QK_NOPE + QK_ROPE)
    w["w_q_b"] = jnp.pad(wq, ((0, 0), (0, 0), (0, QHEAD - QK_NOPE - QK_ROPE))).reshape(Q_LORA, MLA_HEADS * QHEAD)
    wkv = whole("w_kv_b").reshape(KV_LORA, MLA_HEADS, 2, QK_NOPE)
    w["w_kv_b"] = jnp.swapaxes(wkv, 1, 2).reshape(KV_LORA, 2 * MLA_HEADS * QK_NOPE)
    w["ssd_conv_w"] = whole("ssd_conv_w")
    row = lambda v: v.reshape(1, -1)
    w["q_norm_g"] = row(rep["q_norm_g"][i])
    w["kv_norm_g"] = row(rep["kv_norm_g"][i])
    w["ssd_conv_b"] = row(rep["ssd_conv_b"][i])
    w["dt_bias"] = row(jnp.pad(rep["dt_bias"][i], (0, LANES - SSD_HEADS)))
    a = -jnp.exp(rep["a_log"][i])
    w["a"] = a
    w["a_x"] = row(jnp.repeat(a, SSD_HEAD_DIM))
    w["d_x"] = row(jnp.repeat(rep["d_skip"][i], SSD_HEAD_DIM))
    w["ssd_norm_g"] = row(rep["ssd_norm_g"][i])
    w["ffn_conv_bg"] = row(rep["ffn_conv_b"][i][:D_FF])
    w["ffn_conv_bv"] = row(rep["ffn_conv_b"][i][D_FF:])
    for n in ("ln1_g", "ln1_b", "ln2_g", "ln2_b"):
        w[n] = row(rep[n][i])
    return w


def _layer_fwd(h, w, tabs, expand, layer, late):
    tag = f"l{layer}_"
    cos, sin_a, sin_b = tabs
    s = {"h": h}
    proj = _mm(h, w["w_in_t"], trans_b=True, name=tag + "in_proj", tn=768)
    s["proj"] = proj
    qn = _rms_fwd(proj, OFF_Q, Q_LORA, w["q_norm_g"], name=tag + "q_norm")
    q_raw = _mm(qn, w["w_q_b"], name=tag + "q_up")
    q = _rope_q_fwd(q_raw, cos, sin_a, sin_b, name=tag + "q_rope")
    kvn = _rms_fwd(proj, OFF_KV, KV_LORA, w["kv_norm_g"], name=tag + "kv_norm")
    kv = _mm(kvn, w["w_kv_b"], name=tag + "kv_up", out_dtype=BF16)
    kpe = _rope_k_fwd(proj, cos, sin_a, sin_b, name=tag + "k_rope")
    if isinstance(late, dict):
        o, lse = _flash_fwd(q, kv, kpe, name=tag + "attn")
    else:
        o, lse, *got = _flash_fwd(q, kv, kpe, name=tag + "attn", gather=late)
        late = dict(zip(GATHER_LATE, got))
    w.update(_late_weights(late, layer))
    ya = _mm(o, w["w_o_attn"], name=tag + "attn_out")
    s.update(qn=qn, q=q, kvn=kvn, kv=kv, kpe=kpe, o=o, lse=lse, ya=ya)
    xs, bm, cm, dtx = _ssd_prep_fwd(proj, w["ssd_conv_w"], w["ssd_conv_b"], w["dt_bias"], expand, name=tag + "ssd_prep")
    y, prev = _ssd_fwd(xs, dtx, bm, cm, bm.T, w["a_x"], w["d_x"], name=tag + "ssd_scan")
    yn = _gnorm_fwd(y, proj, w["ssd_norm_g"], name=tag + "ssd_norm")
    ys = _mm(yn, w["w_o_ssd"], name=tag + "ssd_out")
    s.update(xs=xs, bm=bm, cm=cm, dtx=dtx, y=y, prev=prev, yn=yn, ys=ys)
    mixed = _mix_fwd(proj, ya, ys, name=tag + "mix")
    br = _mm(mixed, w["w_out"], name=tag + "mix_out")
    pre1, h1 = _ln_fwd(h, br, w["ln1_g"], w["ln1_b"], name=tag + "ln1")
    s.update(mixed=mixed, pre1=pre1, h1=h1)
    ug = _mm(h1, w["w_up_g"], name=tag + "up_g", tn=1408)
    uv = _mm(h1, w["w_up_v"], name=tag + "up_v", tn=1408)
    act = _ffn_act_fwd(ug, uv, w["ffn_conv_wg"], w["ffn_conv_wv"], w["ffn_conv_bg"], w["ffn_conv_bv"],
                       name=tag + "ffn_act")
    ffn = _mm(act, w["w_down"], name=tag + "down")
    pre2, h2 = _ln_fwd(h1, ffn, w["ln2_g"], w["ln2_b"], name=tag + "ln2")
    s.update(ug=ug, uv=uv, act=act, pre2=pre2)
    return h2, s, late


def _layer_bwd(dh2, w, s, tabs, reduce_m, tag, begin_exchange=None):
    cos, sin_a, sin_b = tabs
    g = {}
    proj = s["proj"]
    dpre2, g["ln2_g"], g["ln2_b"] = _ln_bwd(dh2, s["pre2"], w["ln2_g"], name=tag + "ln2_bwd")
    g["w_down"] = _mm_tn(s["act"], dpre2, name=tag + "down_dw")
    dact = _mm(dpre2, w["w_down"], trans_b=True, name=tag + "down_dx", tn=1408)
    dcg, dcv, dwg, dwv, dbg, dbv = _ffn_act_bwd(s["ug"], s["uv"], dact, w["ffn_conv_wg"], w["ffn_conv_wv"],
                                                w["ffn_conv_bg"], w["ffn_conv_bv"], name=tag + "ffn_act_bwd")
    g["ffn_conv_w"] = jnp.concatenate([dwg, dwv], axis=1)
    g["ffn_conv_b"] = jnp.concatenate([dbg, dbv], axis=1).reshape(-1)
    dug = _conv_bwd_input(dcg, w["ffn_conv_wg"], FFN_CONV, name=tag + "ffn_conv_bwd_g", tc=1408)
    duv = _conv_bwd_input(dcv, w["ffn_conv_wv"], FFN_CONV, name=tag + "ffn_conv_bwd_v", tc=1408)
    g["w_up"] = jnp.concatenate([_mm_tn(s["h1"], dug, name=tag + "up_g_dw", tn=1408),
                                 _mm_tn(s["h1"], duv, name=tag + "up_v_dw", tn=1408)], axis=1)
    dh1 = _mm(dug, w["w_up_g"], trans_b=True, add=dpre2, add_scale=ALPHA, name=tag + "up_g_dx")
    dh1 = _mm(duv, w["w_up_v"], trans_b=True, add=dh1, name=tag + "up_v_dx")
    dpre1, g["ln1_g"], g["ln1_b"] = _ln_bwd(dh1, s["pre1"], w["ln1_g"], name=tag + "ln1_bwd")
    g["w_out"] = _mm_tn(s["mixed"], dpre1, name=tag + "mix_out_dw")
    dmix = _mm(dpre1, w["w_out"], trans_b=True, name=tag + "mix_out_dx")
    dya, dys, dga, dgs = _mix_bwd(dmix, proj, s["ya"], s["ys"], name=tag + "mix_bwd")
    g["w_o_ssd"] = _mm_tn(s["yn"], dys, name=tag + "ssd_out_dw")
    dyn = _mm(dys, w["w_o_ssd"], trans_b=True, out_dtype=BF16, name=tag + "ssd_out_dx")
    dy, dz, dgn = _gnorm_bwd(dyn, s["y"], proj, w["ssd_norm_g"], name=tag + "ssd_norm_bwd")
    g["ssd_norm_g"] = dgn.reshape(-1)
    dxs, ddtx, dbm, dcm, da_x, dd_x = _ssd_bwd(s["xs"], s["dtx"], s["bm"], s["cm"], s["cm"].T, s["prev"], dy,
                                               w["a_x"], w["d_x"], name=tag + "ssd_scan_bwd")
    g["a_log"] = da_x.reshape(SSD_HEADS, SSD_HEAD_DIM).sum(axis=1) * w["a"]
    g["d_skip"] = dd_x.reshape(SSD_HEADS, SSD_HEAD_DIM).sum(axis=1)
    dconv, ddtr, dcw, dcb, ddtb = _ssd_prep_bwd_a(proj, dxs, dbm, dcm, ddtx, w["ssd_conv_w"], w["ssd_conv_b"],
                                                  w["dt_bias"], reduce_m, name=tag + "ssd_prep_bwd")
    g["ssd_conv_w"] = dcw
    g["ssd_conv_b"] = dcb.reshape(-1)
    g["dt_bias"] = ddtb.reshape(-1)[:SSD_HEADS]
    dxbc = _conv_bwd_input(dconv, w["ssd_conv_w"], SSD_CONV, name=tag + "ssd_conv_bwd", tc=1024)
    g["w_o_attn"] = _mm_tn(s["o"], dya, name=tag + "attn_out_dw")
    exchange = begin_exchange(dict(g)) if begin_exchange else None
    do = _mm(dya, w["w_o_attn"], trans_b=True, out_dtype=BF16, name=tag + "attn_out_dx")
    delta = _attn_delta(do, s["o"], name=tag + "attn_delta")
    by_tile = lambda a: a.reshape(MLA_HEADS, -1, _tile(a.shape[1], 640))
    dq, dkn, dkp, dv, *exchanged = _flash_bwd(s["q"], s["kv"], s["kpe"], do, by_tile(s["lse"]), by_tile(delta),
                                              name=tag + "attn_bwd", exchange=exchange)
    dq_raw = _rope_q_bwd(dq, cos, sin_a, sin_b, name=tag + "q_rope_bwd")
    dwq = _mm_tn(s["qn"], dq_raw, name=tag + "q_up_dw")
    g["w_q_b"] = dwq.reshape(Q_LORA, MLA_HEADS, QHEAD)[:, :, :QK_NOPE + QK_ROPE].reshape(Q_LORA, -1)
    dqn = _mm(dq_raw, w["w_q_b"], trans_b=True, out_dtype=BF16, name=tag + "q_up_dx")
    dqlat, dgq = _rms_bwd(dqn, proj, OFF_Q, Q_LORA, w["q_norm_g"], name=tag + "q_norm_bwd")
    g["q_norm_g"] = dgq.reshape(-1)
    dkv = jnp.concatenate([dkn, dv], axis=1)
    dwkv = _mm_tn(s["kvn"], dkv, name=tag + "kv_up_dw")
    g["w_kv_b"] = jnp.swapaxes(dwkv.reshape(KV_LORA, 2, MLA_HEADS, QK_NOPE), 1, 2).reshape(KV_LORA, -1)
    dkvn = _mm(dkv, w["w_kv_b"], trans_b=True, out_dtype=BF16, name=tag + "kv_up_dx")
    dkvlat, dgkv = _rms_bwd(dkvn, proj, OFF_KV, KV_LORA, w["kv_norm_g"], name=tag + "kv_norm_bwd")
    g["kv_norm_g"] = dgkv.reshape(-1)
    dkpe = _rope_k_bwd(dkp, cos, sin_a, sin_b, name=tag + "k_rope_bwd")
    h = s["h"]
    comps = ((dqlat, OFF_Q), (dkvlat, OFF_KV), (dz, OFF_Z), (dxbc, OFF_XBC), (dga, OFF_GA), (dgs, OFF_GS),
             (dkpe, OFF_KPE), (ddtr, OFF_DT))
    dws = {off: _mm_tn(dc, h, name=f"{tag}in_dw{n}") for n, (dc, off) in enumerate(comps)}
    with_w = lambda group: [(dc, w["w_in_t"][off:off + dc.shape[1]]) for dc, off in group]
    wide = [c for c in comps if c[1] in (OFF_Z, OFF_XBC)]
    rest = [c for c in comps if c[1] not in (OFF_Z, OFF_XBC)]
    dh = _mm_sum(with_w(wide), dpre1, add_scale=ALPHA, name=tag + "in_dx_wide")
    dh = _mm_sum(with_w(rest), dh, name=tag + "in_dx_rest")
    g["w_in"] = jnp.concatenate([dws[OFF_Q], dws[OFF_KV], dws[OFF_KPE][:QK_ROPE], dws[OFF_Z], dws[OFF_XBC],
                                 dws[OFF_DT][:SSD_HEADS], dws[OFF_GA], dws[OFF_GS]], axis=0)
    return dh, g, (exchanged[0] if exchanged else None)


def _local_step(x, target, full, late, rep):
    seq = x.shape[0]
    t = seq + ROW0
    tabs = _rope_tables(t)
    expand = _expand_matrix()
    reduce_m = expand.T
    meta = _join(full["meta_tokens"].reshape(N_CHIPS, N_META, -1), "col")
    xin = jnp.concatenate([jnp.zeros((PAD, D_MODEL), F32), meta, x], axis=0)
    row = lambda v: v.reshape(1, -1)
    _, h = _ln_fwd(xin, None, row(rep["emb_ln_g"]), row(rep["emb_ln_b"]), name="emb_ln")
    ws, saved = [], []
    for i in range(DEPTH):
        w = _early_weights(full, rep, i)
        h, s, late = _layer_fwd(h, w, tabs, expand, i, late)
        ws.append(w)
        saved.append(s)
    dh, loss = _loss_grad(h, target, name="loss")
    dh, g1, _ = _layer_bwd(dh, ws[1], saved[1], tabs, reduce_m, "l1_")
    layouts = []

    def begin_exchange(g0_so_far):
        g4, layout = _pack_part([((1, n), g) for n, g in g1.items()] + [((0, n), g) for n, g in g0_so_far.items()])
        layouts.append(layout)
        return _reduce_begin(g4, "reduce1_")

    dh, g0, parts1 = _layer_bwd(dh, ws[0], saved[0], tabs, reduce_m, "l0_", begin_exchange=begin_exchange)
    dxin, dg, db = _ln_bwd(dh, xin, row(rep["emb_ln_g"]), name="emb_ln_bwd")
    rest = [((0, n), g0[n]) for n in LATE_GRADS]
    rest += [((0, "emb_ln_g"), dg), ((0, "emb_ln_b"), db), ((0, "meta_tokens"), dxin[PAD:ROW0])]
    return loss, dxin[ROW0:], rest, parts1, layouts[0]


def _pack_part(entries):
    kinds = dict(SHARDED)
    pieces, layout, tail, row0 = [], [], [], 0

    def add(key, piece):
        nonlocal row0
        pieces.append(piece)
        layout.append((key, row0, piece.shape[1]))
        row0 += piece.shape[1]

    for key, g in entries:
        if key[1] in kinds:
            kind = "row" if key[1] == "w_in" else kinds[key[1]]
            add(key, _chip_segments(g, kind).reshape(N_CHIPS, -1, LANES))
    for key, g in entries:
        if key[1] not in kinds and g.size % LANES == 0:
            add(key, jnp.broadcast_to(g.reshape(1, -1, LANES), (N_CHIPS, g.size // LANES, LANES)))
        elif key[1] not in kinds:
            tail.append((key, g.reshape(-1)))
    if tail:
        vec = jnp.concatenate([g for _, g in tail])
        vec = jnp.pad(vec, (0, -vec.shape[0] % LANES)).reshape(1, -1, LANES)
        add(("tail", tuple((key, g.shape[0]) for key, g in tail)), jnp.broadcast_to(vec, (N_CHIPS,) + vec.shape[1:]))
    rows = -(-row0 // REDUCE_ROW_ALIGN) * REDUCE_ROW_ALIGN
    pieces.append(jnp.zeros((N_CHIPS, rows - row0, LANES), F32))
    return jnp.concatenate(pieces, axis=1), layout


def _core_index():
    return lax.axis_index("c").astype(jnp.int32).reshape(1)


def _reduce_begin(g4, tag):
    n, r, _ = g4.shape
    g5 = g4.reshape(n, 2, r // 2, LANES)
    got = _sibling_swap(g5, name=tag + "pair_swap")
    return _pair_add(g5, got, _core_index(), name=tag + "pair_add")


def _reduce_end(parts, tag):
    half = _sum_chips(parts, _core_index(), name=tag + "chip_sum")
    both = _sibling_allgather(half, name=tag + "pair_gather")
    return both.reshape(-1, LANES)


def _unpack_part(flat, layout, shapes):
    out = {}
    for key, row0, rows in layout:
        piece = flat[row0:row0 + rows]
        if key[0] == "tail":
            vec, off = piece.reshape(-1), 0
            for sub, size in key[1]:
                out[sub] = vec[off:off + size]
                off += size
        elif key[1] == "w_in":
            out[key] = piece.reshape(shapes[key[1]][1], shapes[key[1]][0]).T
        else:
            out[key] = piece.reshape(shapes[key[1]])
    return out


def kernel(x, meta_tokens, emb_ln_g, emb_ln_b, w_in, q_norm_g, w_q_b, kv_norm_g, w_kv_b, w_o_attn, ssd_conv_w, ssd_conv_b, dt_bias, a_log, d_skip, ssd_norm_g, w_o_ssd, w_out, ln1_g, ln1_b, w_up, ffn_conv_w, ffn_conv_b, w_down, ln2_g, ln2_b, loss_target, m_meta_tokens, m_emb_ln_g, m_emb_ln_b, m_w_in, m_q_norm_g, m_w_q_b, m_kv_norm_g, m_w_kv_b, m_w_o_attn, m_ssd_conv_w, m_ssd_conv_b, m_dt_bias, m_a_log, m_d_skip, m_ssd_norm_g, m_w_o_ssd, m_w_out, m_ln1_g, m_ln1_b, m_w_up, m_ffn_conv_w, m_ffn_conv_b, m_w_down, m_ln2_g, m_ln2_b, v_meta_tokens, v_emb_ln_g, v_emb_ln_b, v_w_in, v_q_norm_g, v_w_q_b, v_kv_norm_g, v_w_kv_b, v_w_o_attn, v_ssd_conv_w, v_ssd_conv_b, v_dt_bias, v_a_log, v_d_skip, v_ssd_norm_g, v_w_o_ssd, v_w_out, v_ln1_g, v_ln1_b, v_w_up, v_ffn_conv_w, v_ffn_conv_b, v_w_down, v_ln2_g, v_ln2_b):
    given = dict(locals())
    local_w = {n: given[n] for n in WEIGHTS}
    local_m = {n: given["m_" + n] for n in WEIGHTS}
    local_v = {n: given["v_" + n] for n in WEIGHTS}
    full = dict(zip(GATHER_EARLY, _chip_allgather(_travel_form(local_w, GATHER_EARLY), name="gather_early")))
    rep = {n: local_w[n] for n in REPLICATED}
    loss, grad_x, rest, parts1, layout1 = _local_step(x[0], loss_target[0], full, _travel_form(local_w, GATHER_LATE), rep)
    g4, layout0 = _pack_part(rest)
    parts0 = _chip_exchange(_reduce_begin(g4, "reduce0_"), name="reduce0_chip_exchange")
    shapes = {n: (local_w[n].shape if n in OUTSIDE else local_w[n].shape[1:]) for n in WEIGHTS}
    summed = _unpack_part(_reduce_end(parts0, "reduce0_"), layout0, shapes)
    summed.update(_unpack_part(_reduce_end(parts1, "reduce1_"), layout1, shapes))
    grad = {n: (summed[0, n] if n in OUTSIDE else jnp.stack([summed[i, n] for i in range(DEPTH)])) for n in WEIGHTS}
    upd = {}
    small = [n for n in WEIGHTS if n not in GATHER_BF16]
    for n in GATHER_BF16:
        upd[n] = _adamw(grad[n], local_w[n], local_m[n], local_v[n], name="adamw_" + n)
    res = _adamw_small([(grad[n], local_w[n], local_m[n], local_v[n]) for n in small], name="adamw_small")
    upd.update(zip(small, res))
    total = lax.psum(loss[0, 0], ("x", "y", "c"))
    outs = [total, grad_x[None]] + [grad[n] for n in WEIGHTS]
    for q in range(3):
        outs.extend(upd[n][q] for n in WEIGHTS)
    return tuple(outs)
```

```python
import functools
import math

import numpy as np
import jax
import jax.numpy as jnp
from jax import lax
from jax.experimental import pallas as pl
from jax.experimental.pallas import tpu as pltpu

F32 = jnp.float32
BF16 = jnp.bfloat16

D_MODEL = 1024
N_META = 16
DEPTH = 2
MLA_HEADS = 8
Q_LORA = 768
KV_LORA = 256
QK_NOPE = 128
QK_ROPE = 64
V_HEAD = 128
ROPE_THETA = 10000.0
NEG_INF = -1e30
PAD_KEY_SCORE = -1e30
SSD_INNER = 2048
SSD_HEAD_DIM = 64
SSD_HEADS = 32
SSD_GROUPS = 4
SSD_STATE = 128
SSD_CONV = 4
SSD_CONV_DIM = 3072
CHUNK = 128
D_FF = 2816
FFN_CONV = 3
LN_EPS = 1e-5
RMS_EPS = 1e-6
ALPHA = (2 * DEPTH) ** 0.25
ATTN_SCALE = (QK_NOPE + QK_ROPE) ** -0.5
LOG2E = math.log2(math.e)
LN2 = math.log(2.0)
Q_SCALE = ATTN_SCALE * LOG2E
ADAM_LR = 0.001
ADAM_B1 = 0.9
ADAM_B2 = 0.999
ADAM_EPS = 1e-08
ADAM_WD = 0.01
ADAM_STEP = 10

LANES = 128
PAD = 112
ROW0 = PAD + N_META
QHEAD = 256
GROUP_W = SSD_INNER // SSD_GROUPS
HALO = 8
VMEM_LIMIT_BYTES = 56 * 1024 * 1024
MM_VMEM_BUDGET = 40 * 1024 * 1024
N_CHIPS = 4

OFF_Q, OFF_KV, OFF_Z, OFF_XBC, OFF_GA, OFF_GS, OFF_KPE, OFF_DT = 0, 768, 1024, 3072, 6144, 7168, 8192, 8320
IN_COLS_P = 8448

NT_DIMS = (((1,), (1,)), ((), ()))
NN_DIMS = (((1,), (0,)), ((), ()))
TN_DIMS = (((0,), (0,)), ((), ()))

SHARDED = (("meta_tokens", "col"), ("w_in", "col"), ("w_q_b", "col"), ("w_kv_b", "col"), ("w_o_attn", "row"),
           ("ssd_conv_w", "col"), ("w_o_ssd", "row"), ("w_out", "row"), ("w_up", "col"), ("ffn_conv_w", "col"),
           ("w_down", "row"))
REPLICATED = ("emb_ln_g", "emb_ln_b", "q_norm_g", "kv_norm_g", "ssd_conv_b", "dt_bias", "a_log", "d_skip",
              "ssd_norm_g", "ln1_g", "ln1_b", "ffn_conv_b", "ln2_g", "ln2_b")
WEIGHTS = ("meta_tokens", "emb_ln_g", "emb_ln_b", "w_in", "q_norm_g", "w_q_b", "kv_norm_g", "w_kv_b", "w_o_attn",
           "ssd_conv_w", "ssd_conv_b", "dt_bias", "a_log", "d_skip", "ssd_norm_g", "w_o_ssd", "w_out", "ln1_g",
           "ln1_b", "w_up", "ffn_conv_w", "ffn_conv_b", "w_down", "ln2_g", "ln2_b")
GATHER_BF16 = ("w_in", "w_q_b", "w_kv_b", "w_o_attn", "w_o_ssd", "w_out", "w_up", "w_down")
GATHER_EARLY = ("meta_tokens", "w_in", "w_q_b", "w_kv_b", "ssd_conv_w")
GATHER_LATE = ("w_o_attn", "w_o_ssd", "w_out", "w_up", "ffn_conv_w", "w_down")
OUTSIDE = ("meta_tokens", "emb_ln_g", "emb_ln_b")
LATE_GRADS = ("w_in", "w_q_b", "q_norm_g", "w_kv_b", "kv_norm_g")
REDUCE_ROW_ALIGN = 512


def _tile(n, target, base=LANES):
    best = None
    d = base
    while d <= min(n, target):
        if n % d == 0:
            best = d
        d += base
    return n if best is None else best


def _cp(*sem):
    return pltpu.CompilerParams(dimension_semantics=sem, vmem_limit_bytes=VMEM_LIMIT_BYTES)


def _sds(shape, dtype):
    return jax.ShapeDtypeStruct(shape, dtype)


def _row_ids(i, tr, shape):
    return i * tr + lax.broadcasted_iota(jnp.int32, shape, 0)


def _sigmoid(x):
    return 1.0 / (1.0 + jnp.exp(-x))


def _mm_tiles(m, n, tn_max, tk, nk, a_bytes, b_bytes, out_bytes, add_bytes):
    divisors = lambda size, cap: [d for d in range(min(size, cap) // LANES * LANES, 0, -LANES) if size % d == 0]
    for tm in divisors(m, m):
        for tn in divisors(n, tn_max):
            blocks = 2 * (tm * tk * a_bytes + tk * tn * b_bytes + tm * tn * (out_bytes + add_bytes))
            temps = tm * tn * 4 * (2 if nk > 1 else 1) + tm * tk * 2 + tk * tn * 2
            if blocks + temps <= MM_VMEM_BUDGET:
                return tm, tn
    return LANES, LANES


def _mm(a, b, *, name, trans_b=False, out_dtype=F32, add=None, add_scale=1.0, tn=1024, tk=1408):
    m, k_dim = a.shape
    n = b.shape[0] if trans_b else b.shape[1]
    tk = _tile(k_dim, tk)
    nk = k_dim // tk
    has_add = add is not None
    tm, tn = _mm_tiles(m, n, tn, tk, nk, a.dtype.itemsize, b.dtype.itemsize, jnp.dtype(out_dtype).itemsize,
                       add.dtype.itemsize if has_add else 0)
    dims = NT_DIMS if trans_b else NN_DIMS

    def body(*refs):
        a_ref, b_ref = refs[0], refs[1]
        r_ref = refs[2] if has_add else None
        o_ref = refs[3] if has_add else refs[2]
        part = lax.dot_general(a_ref[...].astype(BF16), b_ref[...].astype(BF16), dims, preferred_element_type=F32)

        def finish(r):
            if has_add:
                r = r + add_scale * r_ref[...].astype(F32)
            o_ref[...] = r.astype(out_dtype)

        if nk == 1:
            finish(part)
        else:
            acc = refs[-1]
            kk = pl.program_id(2)

            @pl.when(kk == 0)
            def _():
                acc[...] = part

            @pl.when(kk > 0)
            def _():
                acc[...] += part

            @pl.when(kk == nk - 1)
            def _():
                finish(acc[...])

    in_specs = [pl.BlockSpec((tm, tk), lambda i, j, kk: (i, kk)),
                pl.BlockSpec((tn, tk), lambda i, j, kk: (j, kk)) if trans_b
                else pl.BlockSpec((tk, tn), lambda i, j, kk: (kk, j))]
    args = [a, b]
    if has_add:
        in_specs.append(pl.BlockSpec((tm, tn), lambda i, j, kk: (i, j)))
        args.append(add)
    return pl.pallas_call(
        body, name=name, grid=(m // tm, n // tn, nk), in_specs=in_specs,
        out_specs=pl.BlockSpec((tm, tn), lambda i, j, kk: (i, j)),
        out_shape=_sds((m, n), out_dtype),
        scratch_shapes=[pltpu.VMEM((tm, tn), F32)] if nk > 1 else [],
        compiler_params=_cp("parallel", "parallel", "arbitrary"),
    )(*args)


def _mm_sum(pairs, add, *, name, add_scale=1.0, tm=640):
    m, n = add.shape
    tm = _tile(m, tm)
    npairs = len(pairs)

    def body(*refs):
        a_refs, b_refs = refs[:npairs], refs[npairs:2 * npairs]
        r_ref, o_ref = refs[2 * npairs], refs[2 * npairs + 1]
        acc = add_scale * r_ref[...]
        for a_ref, b_ref in zip(a_refs, b_refs):
            acc = acc + jnp.dot(a_ref[...].astype(BF16), b_ref[...].astype(BF16), preferred_element_type=F32)
        o_ref[...] = acc

    in_specs = ([pl.BlockSpec((tm, a.shape[1]), lambda i: (i, 0)) for a, _ in pairs]
                + [pl.BlockSpec(b.shape, lambda i: (0, 0)) for _, b in pairs]
                + [pl.BlockSpec((tm, n), lambda i: (i, 0))])
    return pl.pallas_call(
        body, name=name, grid=(m // tm,), in_specs=in_specs, out_specs=pl.BlockSpec((tm, n), lambda i: (i, 0)),
        out_shape=_sds((m, n), F32), compiler_params=_cp("parallel"),
    )(*[a for a, _ in pairs], *[b for _, b in pairs], add)


def _mm_tn(a, b, *, name, tko=1408, tn=1024, tt=640):
    t, k_dim = a.shape
    n = b.shape[1]
    tko, tn, tt = _tile(k_dim, tko), _tile(n, tn), _tile(t, tt)

    def body(a_ref, b_ref, o_ref):
        part = lax.dot_general(a_ref[...].astype(BF16), b_ref[...].astype(BF16), TN_DIMS, preferred_element_type=F32)
        tt_i = pl.program_id(2)

        @pl.when(tt_i == 0)
        def _():
            o_ref[...] = part

        @pl.when(tt_i > 0)
        def _():
            o_ref[...] += part

    return pl.pallas_call(
        body, name=name, grid=(k_dim // tko, n // tn, t // tt),
        in_specs=[pl.BlockSpec((tt, tko), lambda i, j, s: (s, i)), pl.BlockSpec((tt, tn), lambda i, j, s: (s, j))],
        out_specs=pl.BlockSpec((tko, tn), lambda i, j, s: (i, j)),
        out_shape=_sds((k_dim, n), F32),
        compiler_params=_cp("parallel", "parallel", "arbitrary"),
    )(a, b)


def _ln_fwd(h, branch, g, b, *, name):
    t, d = h.shape
    tr = _tile(t, 640)
    has_branch = branch is not None

    def body(*refs):
        if has_branch:
            h_ref, br_ref, g_ref, b_ref, pre_ref, o_ref, ob_ref = refs
            pre = ALPHA * h_ref[...] + br_ref[...]
            pre_ref[...] = pre
        else:
            h_ref, g_ref, b_ref, o_ref, ob_ref = refs
            pre = h_ref[...]
        mu = jnp.mean(pre, axis=1, keepdims=True)
        xc = pre - mu
        var = jnp.mean(xc * xc, axis=1, keepdims=True)
        y = xc * lax.rsqrt(var + LN_EPS) * g_ref[...] + b_ref[...]
        rows = _row_ids(pl.program_id(0), tr, (tr, 1))
        y = jnp.where(rows >= PAD, y, 0.0)
        o_ref[...] = y
        ob_ref[...] = y.astype(BF16)

    row_spec = pl.BlockSpec((tr, d), lambda i: (i, 0))
    vec_spec = pl.BlockSpec((1, d), lambda i: (0, 0))
    if has_branch:
        return pl.pallas_call(
            body, name=name, grid=(t // tr,), in_specs=[row_spec, row_spec, vec_spec, vec_spec],
            out_specs=[row_spec] * 3, out_shape=[_sds((t, d), F32), _sds((t, d), F32), _sds((t, d), BF16)],
            compiler_params=_cp("parallel"))(h, branch, g, b)
    out, out_b = pl.pallas_call(
        body, name=name, grid=(t // tr,), in_specs=[row_spec, vec_spec, vec_spec],
        out_specs=[row_spec] * 2, out_shape=[_sds((t, d), F32), _sds((t, d), BF16)],
        compiler_params=_cp("parallel"))(h, g, b)
    return h, out, out_b


def _ln_bwd(dy, pre, g, *, name):
    t, d = pre.shape
    tr = _tile(t, 640)

    def body(dy_ref, pre_ref, g_ref, dpre_ref, dpre_b_ref, dg_ref, db_ref):
        i = pl.program_id(0)
        pre_v = pre_ref[...]
        mu = jnp.mean(pre_v, axis=1, keepdims=True)
        xc = pre_v - mu
        var = jnp.mean(xc * xc, axis=1, keepdims=True)
        rstd = lax.rsqrt(var + LN_EPS)
        xhat = xc * rstd
        rows = _row_ids(i, tr, (tr, 1))
        dym = jnp.where(rows >= PAD, dy_ref[...], 0.0)
        gdy = dym * g_ref[...]
        m1 = jnp.mean(gdy, axis=1, keepdims=True)
        m2 = jnp.mean(gdy * xhat, axis=1, keepdims=True)
        dpre = rstd * (gdy - m1 - xhat * m2)
        dpre_ref[...] = dpre
        dpre_b_ref[...] = dpre.astype(BF16)
        pg = jnp.sum(dym * xhat, axis=0, keepdims=True)
        pb = jnp.sum(dym, axis=0, keepdims=True)

        @pl.when(i == 0)
        def _():
            dg_ref[...] = pg
            db_ref[...] = pb

        @pl.when(i > 0)
        def _():
            dg_ref[...] += pg
            db_ref[...] += pb

    row_spec = pl.BlockSpec((tr, d), lambda i: (i, 0))
    vec_spec = pl.BlockSpec((1, d), lambda i: (0, 0))
    return pl.pallas_call(
        body, name=name, grid=(t // tr,), in_specs=[row_spec, row_spec, vec_spec],
        out_specs=[row_spec, row_spec, vec_spec, vec_spec],
        out_shape=[_sds((t, d), F32), _sds((t, d), BF16), _sds((1, d), F32), _sds((1, d), F32)],
        compiler_params=_cp("arbitrary"))(dy, pre, g)


def _rms_fwd(proj, col_off, width, g, *, name):
    t = proj.shape[0]
    tr = _tile(t, 640)
    cb = col_off // width

    def body(x_ref, g_ref, o_ref):
        x = x_ref[...]
        r = lax.rsqrt(jnp.mean(x * x, axis=1, keepdims=True) + RMS_EPS)
        o_ref[...] = (x * r * g_ref[...]).astype(BF16)

    return pl.pallas_call(
        body, name=name, grid=(t // tr,),
        in_specs=[pl.BlockSpec((tr, width), lambda i: (i, cb)), pl.BlockSpec((1, width), lambda i: (0, 0))],
        out_specs=pl.BlockSpec((tr, width), lambda i: (i, 0)), out_shape=_sds((t, width), BF16),
        compiler_params=_cp("parallel"))(proj, g)


def _rms_bwd(dy, proj, col_off, width, g, *, name):
    t = proj.shape[0]
    tr = _tile(t, 640)
    cb = col_off // width

    def body(dy_ref, x_ref, g_ref, dx_ref, dg_ref):
        i = pl.program_id(0)
        x = x_ref[...]
        dyv = dy_ref[...].astype(F32)
        r = lax.rsqrt(jnp.mean(x * x, axis=1, keepdims=True) + RMS_EPS)
        gdy = dyv * g_ref[...]
        m = jnp.mean(x * gdy, axis=1, keepdims=True)
        dx_ref[...] = (r * gdy - x * (r * r * r) * m).astype(BF16)
        pg = jnp.sum(dyv * x * r, axis=0, keepdims=True)

        @pl.when(i == 0)
        def _():
            dg_ref[...] = pg

        @pl.when(i > 0)
        def _():
            dg_ref[...] += pg

    return pl.pallas_call(
        body, name=name, grid=(t // tr,),
        in_specs=[pl.BlockSpec((tr, width), lambda i: (i, 0)), pl.BlockSpec((tr, width), lambda i: (i, cb)),
                  pl.BlockSpec((1, width), lambda i: (0, 0))],
        out_specs=[pl.BlockSpec((tr, width), lambda i: (i, 0)), pl.BlockSpec((1, width), lambda i: (0, 0))],
        out_shape=[_sds((t, width), BF16), _sds((1, width), F32)],
        compiler_params=_cp("arbitrary"))(dy, proj, g)


def _rope_apply(r, cos, sin_a, sin_b):
    return r * cos + pltpu.roll(r, 96, 1) * sin_a + pltpu.roll(r, 32, 1) * sin_b


def _rope_apply_t(dr, cos, sin_a, sin_b):
    return dr * cos + pltpu.roll(dr * sin_a, 32, 1) + pltpu.roll(dr * sin_b, 96, 1)


def _rope_q_fwd(q, cos, sin_a, sin_b, *, name):
    t, w = q.shape
    tr = _tile(t, 128)

    def body(q_ref, c_ref, sa_ref, sb_ref, o_ref):
        c, sa, sb = c_ref[...], sa_ref[...], sb_ref[...]
        flag = lax.broadcasted_iota(jnp.int32, (tr, LANES), 1) == QK_ROPE
        for h in range(MLA_HEADS):
            base = h * QHEAD
            o_ref[:, base:base + LANES] = (q_ref[:, base:base + LANES] * Q_SCALE).astype(BF16)
            rot = _rope_apply(q_ref[:, base + LANES:base + QHEAD], c, sa, sb)
            o_ref[:, base + LANES:base + QHEAD] = jnp.where(flag, 1.0, rot * Q_SCALE).astype(BF16)

    tab = pl.BlockSpec((tr, LANES), lambda i: (i, 0))
    row = pl.BlockSpec((tr, w), lambda i: (i, 0))
    return pl.pallas_call(body, name=name, grid=(t // tr,), in_specs=[row, tab, tab, tab], out_specs=row,
                          out_shape=_sds((t, w), BF16), compiler_params=_cp("parallel"))(q, cos, sin_a, sin_b)


def _rope_q_bwd(dq, cos, sin_a, sin_b, *, name):
    t, w = dq.shape
    tr = _tile(t, 128)

    def body(dq_ref, c_ref, sa_ref, sb_ref, o_ref):
        c, sa, sb = c_ref[...], sa_ref[...], sb_ref[...]
        for h in range(MLA_HEADS):
            base = h * QHEAD
            o_ref[:, base:base + LANES] = (dq_ref[:, base:base + LANES] * ATTN_SCALE).astype(BF16)
            d_rot = _rope_apply_t(dq_ref[:, base + LANES:base + QHEAD], c, sa, sb)
            o_ref[:, base + LANES:base + QHEAD] = (d_rot * ATTN_SCALE).astype(BF16)

    tab = pl.BlockSpec((tr, LANES), lambda i: (i, 0))
    row = pl.BlockSpec((tr, w), lambda i: (i, 0))
    return pl.pallas_call(body, name=name, grid=(t // tr,), in_specs=[row, tab, tab, tab], out_specs=row,
                          out_shape=_sds((t, w), BF16), compiler_params=_cp("parallel"))(dq, cos, sin_a, sin_b)


def _rope_k_fwd(proj, cos, sin_a, sin_b, *, name):
    t = proj.shape[0]
    tr = _tile(t, 640)
    cb = OFF_KPE // LANES

    def body(x_ref, c_ref, sa_ref, sb_ref, o_ref):
        rot = _rope_apply(x_ref[...], c_ref[...], sa_ref[...], sb_ref[...])
        rows = _row_ids(pl.program_id(0), tr, (tr, LANES))
        lane = lax.broadcasted_iota(jnp.int32, (tr, LANES), 1)
        o_ref[...] = jnp.where((lane == QK_ROPE) & (rows < PAD), PAD_KEY_SCORE, rot).astype(BF16)

    tab = pl.BlockSpec((tr, LANES), lambda i: (i, 0))
    return pl.pallas_call(body, name=name, grid=(t // tr,),
                          in_specs=[pl.BlockSpec((tr, LANES), lambda i: (i, cb)), tab, tab, tab], out_specs=tab,
                          out_shape=_sds((t, LANES), BF16), compiler_params=_cp("parallel"))(proj, cos, sin_a, sin_b)


def _rope_k_bwd(dkp, cos, sin_a, sin_b, *, name):
    nh, t, _ = dkp.shape
    tr = _tile(t, 640)

    def body(d_ref, c_ref, sa_ref, sb_ref, o_ref):
        tot = d_ref[0]
        for h in range(1, nh):
            tot = tot + d_ref[h]
        o_ref[...] = _rope_apply_t(tot, c_ref[...], sa_ref[...], sb_ref[...]).astype(BF16)

    tab = pl.BlockSpec((tr, LANES), lambda i: (i, 0))
    return pl.pallas_call(body, name=name, grid=(t // tr,),
                          in_specs=[pl.BlockSpec((nh, tr, LANES), lambda i: (0, i, 0)), tab, tab, tab], out_specs=tab,
                          out_shape=_sds((t, LANES), BF16), compiler_params=_cp("parallel"))(dkp, cos, sin_a, sin_b)


def _causal(tb, keys_first=False):
    a = lax.broadcasted_iota(jnp.int32, (tb, tb), 0)
    b = lax.broadcasted_iota(jnp.int32, (tb, tb), 1)
    return a <= b if keys_first else b <= a


def _flash_fwd(q, kv, kpe, *, name, gather=()):
    t = q.shape[0]
    nh = MLA_HEADS
    tb = _tile(t, 640)
    nb = t // tb
    na = len(gather)

    def attend(q_ref, kn_ref, v_ref, kp_ref, o_ref, lse_ref, extra):
        i = pl.program_id(1)
        qv = q_ref[...]

        def scores(j):
            r0 = pl.multiple_of(j * tb, tb)
            k = jnp.concatenate([kn_ref[pl.ds(r0, tb), :], kp_ref[pl.ds(r0, tb), :]], axis=1)
            return lax.dot_general(qv, k, NT_DIMS, preferred_element_type=F32)

        def update(s, j, state):
            m_prev, l_prev, acc = state
            m_new = jnp.maximum(m_prev, jnp.max(s, axis=1, keepdims=True))
            p = jnp.exp2(s - m_new)
            corr = jnp.exp2(m_prev - m_new)
            r0 = pl.multiple_of(j * tb, tb)
            pv = jnp.dot(p.astype(BF16), v_ref[pl.ds(r0, tb), :], preferred_element_type=F32)
            return m_new, corr * l_prev + jnp.sum(p, axis=1, keepdims=True), corr * acc + pv

        def loop(j, carry):
            s_cur, st = carry
            s_next = scores(j + 1)
            return s_next, update(s_cur, j, st)

        state = (jnp.full((tb, 1), NEG_INF, F32), jnp.zeros((tb, 1), F32), jnp.zeros((tb, V_HEAD), F32))
        s_diag, state = lax.fori_loop(0, i, loop, (scores(0), state))
        m, l, acc = update(jnp.where(_causal(tb), s_diag, NEG_INF), i, state)
        o_ref[...] = (acc / l).astype(BF16)
        lse_ref[0] = m + jnp.log2(l)

        if na:
            step = pl.program_id(0) * nb + i
            for phase, at in enumerate((0, (nh * nb) // 2, nh * nb - 1)):
                @pl.when(step == at)
                def _(phase=phase):
                    _chip_allgather_phase(phase, extra[:na], extra[na:2 * na], *extra[2 * na:])

    def body(q_ref, kn_ref, v_ref, kp_ref, *rest):
        attend(q_ref, kn_ref, v_ref, kp_ref, *rest[na:na + 2], extra=rest[:na] + rest[na + 2:])

    gather = list(gather)
    return pl.pallas_call(
        body, name=name, grid=(nh, nb),
        in_specs=[pl.BlockSpec((tb, QHEAD), lambda h, i: (i, h)),
                  pl.BlockSpec((t, LANES), lambda h, i: (0, h)),
                  pl.BlockSpec((t, LANES), lambda h, i: (0, nh + h)),
                  pl.BlockSpec((t, LANES), lambda h, i: (0, 0))] + [_ANY] * na,
        out_specs=[pl.BlockSpec((tb, V_HEAD), lambda h, i: (i, h)),
                   pl.BlockSpec((1, tb, 1), lambda h, i: (h, i, 0))] + [_ANY] * na,
        out_shape=[_sds((t, nh * V_HEAD), BF16), _sds((nh, t, 1), F32)] + _chip_allgather_shapes(gather),
        scratch_shapes=_chip_allgather_sems(na) if na else [],
        compiler_params=_cp("arbitrary", "arbitrary"))(q, kv, kv, kpe, *gather)


def _attn_delta(do, o, *, name):
    t = o.shape[0]
    nh = MLA_HEADS
    tr = _tile(t, 640)

    def body(do_ref, o_ref, d_ref):
        d_ref[0] = jnp.sum(do_ref[...].astype(F32) * o_ref[...].astype(F32), axis=1, keepdims=True)

    blk = pl.BlockSpec((tr, V_HEAD), lambda h, i: (i, h))
    return pl.pallas_call(body, name=name, grid=(nh, t // tr), in_specs=[blk, blk],
                          out_specs=pl.BlockSpec((1, tr, 1), lambda h, i: (h, i, 0)),
                          out_shape=_sds((nh, t, 1), F32), compiler_params=_cp("parallel", "parallel"))(do, o)


def _flash_bwd(q, kv, kpe, do, lse, delta, *, name, exchange=None):
    t = q.shape[0]
    nh = MLA_HEADS
    tb = lse.shape[2]
    nb = t // tb
    fused = exchange is not None

    def body(*refs):
        q_ref, do_ref, lse_ref, dl_ref, kn_ref, v_ref, kp_ref = refs[:7]
        dq_ref, dkn_ref, dkp_ref, dv_ref = refs[7 + fused:11 + fused]
        j = pl.program_id(1)

        if fused:
            copies = functools.partial(_chip_exchange_copies, refs[7], refs[11 + fused], *refs[12 + fused:])
            first = (pl.program_id(0) == 0) & (j == 0)
            last = (pl.program_id(0) == nh - 1) & (j == nb - 1)

            @pl.when(first)
            def _():
                _chip_exchange_start(copies())

        @pl.when(j == 0)
        def _():
            dq_ref[...] = jnp.zeros((t, QHEAD), F32)

        k = jnp.concatenate([kn_ref[...], kp_ref[...]], axis=1)
        v = v_ref[...]

        def tile(i, carry, masked):
            dk, dv = carry
            r0 = pl.multiple_of(i * tb, tb)
            qv = q_ref[pl.ds(r0, tb), :]
            dov = do_ref[pl.ds(r0, tb), :]
            st = lax.dot_general(k, qv, NT_DIMS, preferred_element_type=F32)
            if masked:
                st = jnp.where(_causal(tb, keys_first=True), st, NEG_INF)
            pt = jnp.exp2(st - lse_ref[0, pl.ds(i, 1), :])
            dpt = lax.dot_general(v, dov, NT_DIMS, preferred_element_type=F32)
            dst = (pt * (dpt - dl_ref[0, pl.ds(i, 1), :])).astype(BF16)
            dv = dv + jnp.dot(pt.astype(BF16), dov, preferred_element_type=F32)
            dk = dk + jnp.dot(dst, qv, preferred_element_type=F32)
            dq_ref[pl.ds(r0, tb), :] += lax.dot_general(dst, k, TN_DIMS, preferred_element_type=F32)
            return dk, dv

        carry = tile(j, (jnp.zeros((tb, QHEAD), F32), jnp.zeros((tb, V_HEAD), F32)), True)
        dk, dv = lax.fori_loop(j + 1, nb, lambda i, c: tile(i, c, False), carry)
        dkn_ref[...] = (dk[:, :LANES] * LN2).astype(BF16)
        dkp_ref[0] = dk[:, LANES:] * LN2
        dv_ref[...] = dv.astype(BF16)

        if fused:
            @pl.when(last)
            def _():
                _chip_exchange_wait(copies())

    stat = pl.BlockSpec((1, nb, tb), lambda h, j: (h, 0, 0))
    in_specs = [pl.BlockSpec((t, QHEAD), lambda h, j: (0, h)),
                pl.BlockSpec((t, V_HEAD), lambda h, j: (0, h)),
                stat, stat,
                pl.BlockSpec((tb, LANES), lambda h, j: (j, h)),
                pl.BlockSpec((tb, LANES), lambda h, j: (j, nh + h)),
                pl.BlockSpec((tb, LANES), lambda h, j: (j, 0))]
    out_specs = [pl.BlockSpec((t, QHEAD), lambda h, j: (0, h)),
                 pl.BlockSpec((tb, LANES), lambda h, j: (j, h)),
                 pl.BlockSpec((1, tb, LANES), lambda h, j: (h, j, 0)),
                 pl.BlockSpec((tb, V_HEAD), lambda h, j: (j, h))]
    out_shape = [_sds((t, nh * QHEAD), F32), _sds((t, nh * LANES), BF16), _sds((nh, t, LANES), F32),
                 _sds((t, nh * V_HEAD), BF16)]
    args = [q, do, lse, delta, kv, kv, kpe]
    scratch = []
    if fused:
        in_specs.append(_ANY)
        out_specs.append(_ANY)
        out_shape.append(_sds(exchange.shape, exchange.dtype))
        args.append(exchange)
        scratch = _CHIP_EXCHANGE_SEMS
    return pl.pallas_call(body, name=name, grid=(nh, nb), in_specs=in_specs, out_specs=out_specs, out_shape=out_shape,
                          scratch_shapes=scratch, compiler_params=_cp("arbitrary", "arbitrary"))(*args)


def _fill_prev(buf, x_ref, halo_ref, i, tr):
    buf[pl.ds(0, HALO), :] = jnp.where(i > 0, halo_ref[...], 0.0)
    buf[pl.ds(HALO, tr), :] = x_ref[...]


def _conv_prev(buf, w_ref, kw, tr):
    acc = w_ref[kw - 1:kw, :] * buf[pl.ds(HALO, tr), :]
    for k in range(kw - 1):
        acc = acc + w_ref[k:k + 1, :] * buf[pl.ds(HALO - kw + 1 + k, tr), :]
    return acc


def _conv_dw(buf, dc, kw, tr):
    rows = [jnp.sum(dc * buf[pl.ds(HALO - kw + 1 + k, tr), :], axis=0, keepdims=True) for k in range(kw)]
    return jnp.concatenate(rows, axis=0)


def _conv_next(buf, dc_ref, halo_ref, w_ref, kw, i, n_tiles, tr):
    buf[pl.ds(0, tr), :] = dc_ref[...]
    buf[pl.ds(tr, HALO), :] = jnp.where(i < n_tiles - 1, halo_ref[...], 0.0)
    acc = w_ref[kw - 1:kw, :] * buf[pl.ds(0, tr), :]
    for k in range(kw - 1):
        acc = acc + w_ref[k:k + 1, :] * buf[pl.ds(kw - 1 - k, tr), :]
    return acc


def _split3(x):
    x1 = x.astype(BF16)
    r1 = x - x1.astype(F32)
    x2 = r1.astype(BF16)
    x3 = (r1 - x2.astype(F32)).astype(BF16)
    return x1, x2, x3


def _dot3(parts, m, left):
    tot = None
    for p in parts:
        r = jnp.dot(m, p, preferred_element_type=F32) if left else jnp.dot(p, m, preferred_element_type=F32)
        tot = r if tot is None else tot + r
    return tot


def _ssd_prep_fwd(proj, conv_w, conv_b, dt_bias, expand, *, name):
    t = proj.shape[0]
    tr = _tile(t, 128)
    nt = t // tr
    hb = tr // HALO
    cw = SSD_CONV_DIM
    cb_x = OFF_XBC // cw
    cb_dt = OFF_DT // LANES

    def body(x_ref, halo_ref, dtr_ref, w_ref, b_ref, dtb_ref, e_ref, xs_ref, bm_ref, cm_ref, dtx_ref, buf):
        i = pl.program_id(0)
        _fill_prev(buf, x_ref, halo_ref, i, tr)
        conv = _conv_prev(buf, w_ref, SSD_CONV, tr) + b_ref[...]
        rows = _row_ids(i, tr, (tr, 1))
        live = rows >= PAD
        act = jnp.where(live, conv * _sigmoid(conv), 0.0)
        xs_ref[...] = act[:, :SSD_INNER]
        bm_ref[...] = act[:, SSD_INNER:SSD_INNER + GROUP_W]
        cm_ref[...] = act[:, SSD_INNER + GROUP_W:]
        dt = jnp.where(live, jax.nn.softplus(dtr_ref[...] + dtb_ref[...]), 0.0)
        dtx_ref[...] = _dot3(_split3(dt), e_ref[...], left=False)

    return pl.pallas_call(
        body, name=name, grid=(nt,),
        in_specs=[pl.BlockSpec((tr, cw), lambda i: (i, cb_x)),
                  pl.BlockSpec((HALO, cw), lambda i: (jnp.maximum(i * hb - 1, 0), cb_x)),
                  pl.BlockSpec((tr, LANES), lambda i: (i, cb_dt)),
                  pl.BlockSpec((SSD_CONV, cw), lambda i: (0, 0)),
                  pl.BlockSpec((1, cw), lambda i: (0, 0)),
                  pl.BlockSpec((1, LANES), lambda i: (0, 0)),
                  pl.BlockSpec((LANES, SSD_INNER), lambda i: (0, 0))],
        out_specs=[pl.BlockSpec((tr, SSD_INNER), lambda i: (i, 0)), pl.BlockSpec((tr, GROUP_W), lambda i: (i, 0)),
                   pl.BlockSpec((tr, GROUP_W), lambda i: (i, 0)), pl.BlockSpec((tr, SSD_INNER), lambda i: (i, 0))],
        out_shape=[_sds((t, SSD_INNER), F32), _sds((t, GROUP_W), F32), _sds((t, GROUP_W), F32),
                   _sds((t, SSD_INNER), F32)],
        scratch_shapes=[pltpu.VMEM((tr + HALO, cw), F32)],
        compiler_params=_cp("parallel"))(proj, proj, proj, conv_w, conv_b, dt_bias, expand)


def _ssd_prep_bwd_a(proj, dxs, dbm, dcm, ddtx, conv_w, conv_b, dt_bias, reduce_m, *, name):
    t = proj.shape[0]
    tr = _tile(t, 128)
    nt = t // tr
    hb = tr // HALO
    cw = SSD_CONV_DIM
    cb_x = OFF_XBC // cw
    cb_dt = OFF_DT // LANES

    def body(x_ref, halo_ref, dtr_ref, dxs_ref, dbm_ref, dcm_ref, ddtx_ref, w_ref, b_ref, dtb_ref, r_ref,
             dconv_ref, ddtr_ref, dw_ref, db_ref, ddtb_ref, buf):
        i = pl.program_id(0)
        _fill_prev(buf, x_ref, halo_ref, i, tr)
        conv = _conv_prev(buf, w_ref, SSD_CONV, tr) + b_ref[...]
        rows = _row_ids(i, tr, (tr, 1))
        live = rows >= PAD
        sg = _sigmoid(conv)
        dact = jnp.concatenate([dxs_ref[...], dbm_ref[...], dcm_ref[...]], axis=1)
        dconv = jnp.where(live, dact * (sg * (1.0 + conv * (1.0 - sg))), 0.0)
        dconv_ref[...] = dconv
        pw = _conv_dw(buf, dconv, SSD_CONV, tr)
        pb = jnp.sum(dconv, axis=0, keepdims=True)
        ddt = _dot3(_split3(ddtx_ref[...]), r_ref[...], left=False)
        ddtr = jnp.where(live, ddt * _sigmoid(dtr_ref[...] + dtb_ref[...]), 0.0)
        ddtr_ref[...] = ddtr.astype(BF16)
        pdb = jnp.sum(ddtr, axis=0, keepdims=True)

        @pl.when(i == 0)
        def _():
            dw_ref[...] = pw
            db_ref[...] = pb
            ddtb_ref[...] = pdb

        @pl.when(i > 0)
        def _():
            dw_ref[...] += pw
            db_ref[...] += pb
            ddtb_ref[...] += pdb

    return pl.pallas_call(
        body, name=name, grid=(nt,),
        in_specs=[pl.BlockSpec((tr, cw), lambda i: (i, cb_x)),
                  pl.BlockSpec((HALO, cw), lambda i: (jnp.maximum(i * hb - 1, 0), cb_x)),
                  pl.BlockSpec((tr, LANES), lambda i: (i, cb_dt)),
                  pl.BlockSpec((tr, SSD_INNER), lambda i: (i, 0)),
                  pl.BlockSpec((tr, GROUP_W), lambda i: (i, 0)),
                  pl.BlockSpec((tr, GROUP_W), lambda i: (i, 0)),
                  pl.BlockSpec((tr, SSD_INNER), lambda i: (i, 0)),
                  pl.BlockSpec((SSD_CONV, cw), lambda i: (0, 0)),
                  pl.BlockSpec((1, cw), lambda i: (0, 0)),
                  pl.BlockSpec((1, LANES), lambda i: (0, 0)),
                  pl.BlockSpec((SSD_INNER, LANES), lambda i: (0, 0))],
        out_specs=[pl.BlockSpec((tr, cw), lambda i: (i, 0)), pl.BlockSpec((tr, LANES), lambda i: (i, 0)),
                   pl.BlockSpec((SSD_CONV, cw), lambda i: (0, 0)), pl.BlockSpec((1, cw), lambda i: (0, 0)),
                   pl.BlockSpec((1, LANES), lambda i: (0, 0))],
        out_shape=[_sds((t, cw), F32), _sds((t, LANES), BF16), _sds((SSD_CONV, cw), F32), _sds((1, cw), F32),
                   _sds((1, LANES), F32)],
        scratch_shapes=[pltpu.VMEM((tr + HALO, cw), F32)],
        compiler_params=_cp("arbitrary"))(proj, proj, proj, dxs, dbm, dcm, ddtx, conv_w, conv_b, dt_bias, reduce_m)


def _conv_bwd_input(dconv, w, kw, *, name, out_dtype=BF16, tc=None):
    t, c = dconv.shape
    tr = _tile(t, 128)
    nt = t // tr
    hb = tr // HALO
    tc = _tile(c, tc or c)
    last_hb = t // HALO - 1

    def body(dc_ref, halo_ref, w_ref, o_ref, buf):
        i = pl.program_id(0)
        o_ref[...] = _conv_next(buf, dc_ref, halo_ref, w_ref, kw, i, nt, tr).astype(out_dtype)

    return pl.pallas_call(
        body, name=name, grid=(nt, c // tc),
        in_specs=[pl.BlockSpec((tr, tc), lambda i, j: (i, j)),
                  pl.BlockSpec((HALO, tc), lambda i, j: (jnp.minimum((i + 1) * hb, last_hb), j)),
                  pl.BlockSpec((kw, tc), lambda i, j: (0, j))],
        out_specs=pl.BlockSpec((tr, tc), lambda i, j: (i, j)), out_shape=_sds((t, c), out_dtype),
        scratch_shapes=[pltpu.VMEM((tr + HALO, tc), F32)],
        compiler_params=_cp("parallel", "parallel"))(dconv, dconv, w)


def _ffn_act_fwd(ug, uv, wg, wv, bg, bv, *, name):
    t, c = ug.shape
    tr = _tile(t, 128)
    hb = tr // HALO
    tc = _tile(c, 1408)

    def body(ug_ref, hg_ref, uv_ref, hv_ref, wg_ref, wv_ref, bg_ref, bv_ref, o_ref, bufg, bufv):
        i = pl.program_id(0)
        _fill_prev(bufg, ug_ref, hg_ref, i, tr)
        _fill_prev(bufv, uv_ref, hv_ref, i, tr)
        cg = _conv_prev(bufg, wg_ref, FFN_CONV, tr) + bg_ref[...]
        cv = _conv_prev(bufv, wv_ref, FFN_CONV, tr) + bv_ref[...]
        o_ref[...] = (cg * _sigmoid(cg) * cv).astype(BF16)

    blk = pl.BlockSpec((tr, tc), lambda i, j: (i, j))
    halo = pl.BlockSpec((HALO, tc), lambda i, j: (jnp.maximum(i * hb - 1, 0), j))
    wsp = pl.BlockSpec((FFN_CONV, tc), lambda i, j: (0, j))
    bsp = pl.BlockSpec((1, tc), lambda i, j: (0, j))
    return pl.pallas_call(
        body, name=name, grid=(t // tr, c // tc), in_specs=[blk, halo, blk, halo, wsp, wsp, bsp, bsp],
        out_specs=blk, out_shape=_sds((t, c), BF16),
        scratch_shapes=[pltpu.VMEM((tr + HALO, tc), F32), pltpu.VMEM((tr + HALO, tc), F32)],
        compiler_params=_cp("parallel", "parallel"))(ug, ug, uv, uv, wg, wv, bg, bv)


def _ffn_act_bwd(ug, uv, dact, wg, wv, bg, bv, *, name):
    t, c = ug.shape
    tr = _tile(t, 128)
    hb = tr // HALO
    tc = _tile(c, 1408)

    def body(ug_ref, hg_ref, uv_ref, hv_ref, da_ref, wg_ref, wv_ref, bg_ref, bv_ref,
             dcg_ref, dcv_ref, dwg_ref, dwv_ref, dbg_ref, dbv_ref, bufg, bufv):
        i = pl.program_id(1)
        _fill_prev(bufg, ug_ref, hg_ref, i, tr)
        _fill_prev(bufv, uv_ref, hv_ref, i, tr)
        cg = _conv_prev(bufg, wg_ref, FFN_CONV, tr) + bg_ref[...]
        cv = _conv_prev(bufv, wv_ref, FFN_CONV, tr) + bv_ref[...]
        sg = _sigmoid(cg)
        da = da_ref[...]
        dcg = da * cv * (sg * (1.0 + cg * (1.0 - sg)))
        dcv = da * (cg * sg)
        dcg_ref[...] = dcg
        dcv_ref[...] = dcv
        pwg = _conv_dw(bufg, dcg, FFN_CONV, tr)
        pwv = _conv_dw(bufv, dcv, FFN_CONV, tr)
        pbg = jnp.sum(dcg, axis=0, keepdims=True)
        pbv = jnp.sum(dcv, axis=0, keepdims=True)

        @pl.when(i == 0)
        def _():
            dwg_ref[...] = pwg
            dwv_ref[...] = pwv
            dbg_ref[...] = pbg
            dbv_ref[...] = pbv

        @pl.when(i > 0)
        def _():
            dwg_ref[...] += pwg
            dwv_ref[...] += pwv
            dbg_ref[...] += pbg
            dbv_ref[...] += pbv

    blk = pl.BlockSpec((tr, tc), lambda j, i: (i, j))
    halo = pl.BlockSpec((HALO, tc), lambda j, i: (jnp.maximum(i * hb - 1, 0), j))
    wsp = pl.BlockSpec((FFN_CONV, tc), lambda j, i: (0, j))
    bsp = pl.BlockSpec((1, tc), lambda j, i: (0, j))
    return pl.pallas_call(
        body, name=name, grid=(c // tc, t // tr), in_specs=[blk, halo, blk, halo, blk, wsp, wsp, bsp, bsp],
        out_specs=[blk, blk, wsp, wsp, bsp, bsp],
        out_shape=[_sds((t, c), F32), _sds((t, c), F32), _sds((FFN_CONV, c), F32), _sds((FFN_CONV, c), F32),
                   _sds((1, c), F32), _sds((1, c), F32)],
        scratch_shapes=[pltpu.VMEM((tr + HALO, tc), F32), pltpu.VMEM((tr + HALO, tc), F32)],
        compiler_params=_cp("parallel", "arbitrary"))(ug, ug, uv, uv, dact, wg, wv, bg, bv)


def _tri(lower):
    li = lax.broadcasted_iota(jnp.int32, (CHUNK, CHUNK), 0)
    si = lax.broadcasted_iota(jnp.int32, (CHUNK, CHUNK), 1)
    return li >= si if lower else li <= si


def _tri_ones(lower):
    return jnp.where(_tri(lower), 1.0, 0.0).astype(BF16)


def _decay_pair(acs, acs_t, lane0):
    col = acs[:, lane0:lane0 + 1]
    row = acs_t[lane0:lane0 + 1, :]
    low = jnp.where(_tri(True), jnp.exp(jnp.minimum(col - row, 0.0)), 0.0)
    upp = jnp.where(_tri(False), jnp.exp(jnp.minimum(row - col, 0.0)), 0.0)
    return low, upp


def _ssd_fwd(xs, dtx, bm, cm, bm_t, a_x, d_x, *, name):
    t = xs.shape[0]
    nc = t // CHUNK
    gw = GROUP_W

    def body(xs_ref, dt_ref, b_ref, c_ref, bt_ref, a_ref, d_ref, y_ref, prev_ref, h_s):
        @pl.when(pl.program_id(1) == 0)
        def _():
            h_s[...] = jnp.zeros((SSD_STATE, gw), F32)

        x = xs_ref[...]
        dt = dt_ref[...]
        acs = _dot3(_split3(dt * a_ref[...]), _tri_ones(True), left=True)
        acs_t = acs.T
        xc = x * dt
        bv = b_ref[...].astype(BF16)
        cv = c_ref[...].astype(BF16)
        cb = lax.dot_general(cv, bv, NT_DIMS, preferred_element_type=F32)
        lane = lax.broadcasted_iota(jnp.int32, (CHUNK, LANES), 1)
        pieces = []
        for pp in range(gw // LANES):
            xcp = xc[:, pp * LANES:(pp + 1) * LANES]
            acc = jnp.zeros((CHUNK, LANES), F32)
            for e in range(2):
                low, _ = _decay_pair(acs, acs_t, pp * LANES + e * SSD_HEAD_DIM)
                mine = (lane >= e * SSD_HEAD_DIM) & (lane < (e + 1) * SSD_HEAD_DIM)
                xm = jnp.where(mine, xcp, 0.0).astype(BF16)
                acc = acc + jnp.dot((cb * low).astype(BF16), xm, preferred_element_type=F32)
            pieces.append(acc)
        y_diag = jnp.concatenate(pieces, axis=1)
        h_prev = h_s[...]
        y_off = jnp.dot(cv, h_prev.astype(BF16), preferred_element_type=F32) * jnp.exp(acs)
        y_ref[...] = y_diag + y_off + d_ref[...] * x
        prev_ref[0] = h_prev
        last = acs[CHUNK - 1:CHUNK, :]
        w = jnp.exp(last - acs)
        st = jnp.dot(bt_ref[...].astype(BF16), (xc * w).astype(BF16), preferred_element_type=F32)
        h_s[...] = h_prev * jnp.exp(last) + st

    tok = pl.BlockSpec((CHUNK, gw), lambda g, c: (c, g))
    grp = pl.BlockSpec((CHUNK, SSD_STATE), lambda g, c: (c, g))
    vec = pl.BlockSpec((1, gw), lambda g, c: (0, g))
    return pl.pallas_call(
        body, name=name, grid=(SSD_GROUPS, nc),
        in_specs=[tok, tok, grp, grp, pl.BlockSpec((SSD_STATE, CHUNK), lambda g, c: (g, c)), vec, vec],
        out_specs=[tok, pl.BlockSpec((1, SSD_STATE, gw), lambda g, c: (c, 0, g))],
        out_shape=[_sds((t, SSD_INNER), F32), _sds((nc, SSD_STATE, SSD_INNER), F32)],
        scratch_shapes=[pltpu.VMEM((SSD_STATE, gw), F32)],
        compiler_params=_cp("parallel", "arbitrary"))(xs, dtx, bm, cm, bm_t, a_x, d_x)


def _ssd_bwd(xs, dtx, bm, cm, cm_t, prev, dy, a_x, d_x, *, name):
    t = xs.shape[0]
    nc = t // CHUNK
    gw = GROUP_W

    def body(xs_ref, dt_ref, b_ref, c_ref, ct_ref, prev_ref, dy_ref, a_ref, d_ref,
             dxs_ref, ddt_ref, db_ref, dc_ref, da_ref, dd_ref, g_s):
        first = pl.program_id(1) == 0

        @pl.when(first)
        def _():
            g_s[...] = jnp.zeros((SSD_STATE, gw), F32)

        x = xs_ref[...]
        dt = dt_ref[...]
        a = a_ref[...]
        dyv = dy_ref[...]
        acs = _dot3(_split3(dt * a), _tri_ones(True), left=True)
        acs_t = acs.T
        xc = x * dt
        bv = b_ref[...].astype(BF16)
        cv = c_ref[...].astype(BF16)
        cb = lax.dot_general(cv, bv, NT_DIMS, preferred_element_type=F32)
        cb_t = lax.dot_general(bv, cv, NT_DIMS, preferred_element_type=F32)
        last = acs[CHUNK - 1:CHUNK, :]
        w = jnp.exp(last - acs)
        cd = jnp.exp(last)
        p_in = prev_ref[0]
        p_b = p_in.astype(BF16)
        g_out = g_s[...]
        g_b = g_out.astype(BF16)
        dy_e = dyv * jnp.exp(acs)
        dy_eb = dy_e.astype(BF16)
        y_off_raw = jnp.dot(cv, p_b, preferred_element_type=F32)
        dacs = dy_e * y_off_raw
        d_c = lax.dot_general(dy_eb, p_b, NT_DIMS, preferred_element_type=F32)
        d_prev = jnp.dot(ct_ref[...].astype(BF16), dy_eb, preferred_element_type=F32)
        q_l = jnp.dot(bv, g_b, preferred_element_type=F32)
        dxc = w * q_l
        tw = xc * q_l * w
        dacs = dacs - tw
        d_b = lax.dot_general((xc * w).astype(BF16), g_b, NT_DIMS, preferred_element_type=F32)
        last_add = jnp.sum(tw, axis=0, keepdims=True) + cd * jnp.sum(g_out * p_in, axis=0, keepdims=True)
        g_s[...] = cd * g_out + d_prev
        lane = lax.broadcasted_iota(jnp.int32, (CHUNK, LANES), 1)
        d_cb = jnp.zeros((CHUNK, CHUNK), F32)
        d_cb_t = jnp.zeros((CHUNK, CHUNK), F32)
        dxc_pieces, dacs_pieces = [], []
        for pp in range(gw // LANES):
            xcp = xc[:, pp * LANES:(pp + 1) * LANES]
            dyp = dyv[:, pp * LANES:(pp + 1) * LANES]
            dxcp = jnp.zeros((CHUNK, LANES), F32)
            dacsp = jnp.zeros((CHUNK, LANES), F32)
            for e in range(2):
                low, upp = _decay_pair(acs, acs_t, pp * LANES + e * SSD_HEAD_DIM)
                mine = (lane >= e * SSD_HEAD_DIM) & (lane < (e + 1) * SSD_HEAD_DIM)
                m_low = cb * low
                m_upp = cb_t * upp
                dym = jnp.where(mine, dyp, 0.0).astype(BF16)
                xm = jnp.where(mine, xcp, 0.0).astype(BF16)
                dxcp = dxcp + jnp.dot(m_upp.astype(BF16), dym, preferred_element_type=F32)
                d_m = lax.dot_general(dym, xm, NT_DIMS, preferred_element_type=F32)
                d_m_t = lax.dot_general(xm, dym, NT_DIMS, preferred_element_type=F32)
                rs = jnp.sum(d_m * m_low, axis=1, keepdims=True)
                cs = jnp.sum(d_m_t * m_upp, axis=1, keepdims=True)
                dacsp = dacsp + jnp.where(lane == e * SSD_HEAD_DIM, rs - cs, 0.0)
                d_cb = d_cb + d_m * low
                d_cb_t = d_cb_t + d_m_t * upp
            dxc_pieces.append(dxcp)
            dacs_pieces.append(dacsp)
        dxc = dxc + jnp.concatenate(dxc_pieces, axis=1)
        dacs = dacs + jnp.concatenate(dacs_pieces, axis=1)
        rowi = lax.broadcasted_iota(jnp.int32, (CHUNK, gw), 0)
        dacs = dacs + jnp.where(rowi == CHUNK - 1, last_add, 0.0)
        dc_ref[...] = d_c + jnp.dot(d_cb.astype(BF16), bv, preferred_element_type=F32)
        db_ref[...] = d_b + jnp.dot(d_cb_t.astype(BF16), cv, preferred_element_type=F32)
        dda = _dot3(_split3(dacs), _tri_ones(False), left=True)
        ddt_ref[...] = dda * a + dxc * x
        dxs_ref[...] = dxc * dt + d_ref[...] * dyv
        pa = jnp.sum(dda * dt, axis=0, keepdims=True)
        pd = jnp.sum(dyv * x, axis=0, keepdims=True)

        @pl.when(first)
        def _():
            da_ref[...] = pa
            dd_ref[...] = pd

        @pl.when(jnp.logical_not(first))
        def _():
            da_ref[...] += pa
            dd_ref[...] += pd

    rc = lambda c: nc - 1 - c
    tok = pl.BlockSpec((CHUNK, gw), lambda g, c: (rc(c), g))
    grp = pl.BlockSpec((CHUNK, SSD_STATE), lambda g, c: (rc(c), g))
    vec = pl.BlockSpec((1, gw), lambda g, c: (0, g))
    return pl.pallas_call(
        body, name=name, grid=(SSD_GROUPS, nc),
        in_specs=[tok, tok, grp, grp, pl.BlockSpec((SSD_STATE, CHUNK), lambda g, c: (g, rc(c))),
                  pl.BlockSpec((1, SSD_STATE, gw), lambda g, c: (rc(c), 0, g)), tok, vec, vec],
        out_specs=[tok, tok, grp, grp, vec, vec],
        out_shape=[_sds((t, SSD_INNER), F32), _sds((t, SSD_INNER), F32), _sds((t, gw), F32), _sds((t, gw), F32),
                   _sds((1, SSD_INNER), F32), _sds((1, SSD_INNER), F32)],
        scratch_shapes=[pltpu.VMEM((SSD_STATE, gw), F32)],
        compiler_params=_cp("parallel", "arbitrary"))(xs, dtx, bm, cm, cm_t, prev, dy, a_x, d_x)


def _gnorm_fwd(y, proj, g, *, name):
    t = y.shape[0]
    tr = _tile(t, 640)
    zb = OFF_Z // GROUP_W

    def body(y_ref, z_ref, g_ref, o_ref):
        z = z_ref[...]
        v = y_ref[...] * (z * _sigmoid(z))
        r = lax.rsqrt(jnp.mean(v * v, axis=1, keepdims=True) + RMS_EPS)
        o_ref[...] = (v * r * g_ref[...]).astype(BF16)

    blk = pl.BlockSpec((tr, GROUP_W), lambda i, j: (i, j))
    return pl.pallas_call(
        body, name=name, grid=(t // tr, SSD_GROUPS),
        in_specs=[blk, pl.BlockSpec((tr, GROUP_W), lambda i, j: (i, zb + j)),
                  pl.BlockSpec((1, GROUP_W), lambda i, j: (0, j))],
        out_specs=blk, out_shape=_sds((t, SSD_INNER), BF16),
        compiler_params=_cp("parallel", "parallel"))(y, proj, g)


def _gnorm_bwd(dout, y, proj, g, *, name):
    t = y.shape[0]
    tr = _tile(t, 640)
    zb = OFF_Z // GROUP_W

    def body(do_ref, y_ref, z_ref, g_ref, dy_ref, dz_ref, dg_ref):
        i = pl.program_id(1)
        z = z_ref[...]
        yv = y_ref[...]
        sg = _sigmoid(z)
        sz = z * sg
        v = yv * sz
        r = lax.rsqrt(jnp.mean(v * v, axis=1, keepdims=True) + RMS_EPS)
        dov = do_ref[...].astype(F32)
        gdo = dov * g_ref[...]
        m = jnp.mean(v * gdo, axis=1, keepdims=True)
        dv = r * gdo - v * (r * r * r) * m
        dy_ref[...] = dv * sz
        dz_ref[...] = (dv * yv * (sg * (1.0 + z * (1.0 - sg)))).astype(BF16)
        pg = jnp.sum(dov * v * r, axis=0, keepdims=True)

        @pl.when(i == 0)
        def _():
            dg_ref[...] = pg

        @pl.when(i > 0)
        def _():
            dg_ref[...] += pg

    blk = pl.BlockSpec((tr, GROUP_W), lambda j, i: (i, j))
    vec = pl.BlockSpec((1, GROUP_W), lambda j, i: (0, j))
    return pl.pallas_call(
        body, name=name, grid=(SSD_GROUPS, t // tr),
        in_specs=[blk, blk, pl.BlockSpec((tr, GROUP_W), lambda j, i: (i, zb + j)), vec],
        out_specs=[blk, blk, vec],
        out_shape=[_sds((t, SSD_INNER), F32), _sds((t, SSD_INNER), BF16), _sds((1, SSD_INNER), F32)],
        compiler_params=_cp("parallel", "arbitrary"))(dout, y, proj, g)


def _mix_fwd(proj, ya, ys, *, name):
    t, d = ya.shape
    tr = _tile(t, 640)
    ba, bs = OFF_GA // d, OFF_GS // d

    def body(ga_ref, gs_ref, ya_ref, ys_ref, o_ref):
        o_ref[...] = (_sigmoid(ga_ref[...]) * ya_ref[...] + _sigmoid(gs_ref[...]) * ys_ref[...]).astype(BF16)

    blk = pl.BlockSpec((tr, d), lambda i: (i, 0))
    return pl.pallas_call(
        body, name=name, grid=(t // tr,),
        in_specs=[pl.BlockSpec((tr, d), lambda i: (i, ba)), pl.BlockSpec((tr, d), lambda i: (i, bs)), blk, blk],
        out_specs=blk, out_shape=_sds((t, d), BF16), compiler_params=_cp("parallel"))(proj, proj, ya, ys)


def _mix_bwd(dmix, proj, ya, ys, *, name):
    t, d = ya.shape
    tr = _tile(t, 640)
    ba, bs = OFF_GA // d, OFF_GS // d

    def body(dm_ref, ga_ref, gs_ref, ya_ref, ys_ref, dya_ref, dys_ref, dga_ref, dgs_ref):
        dm = dm_ref[...]
        sa = _sigmoid(ga_ref[...])
        ss = _sigmoid(gs_ref[...])
        dya_ref[...] = (sa * dm).astype(BF16)
        dys_ref[...] = (ss * dm).astype(BF16)
        dga_ref[...] = (dm * ya_ref[...] * sa * (1.0 - sa)).astype(BF16)
        dgs_ref[...] = (dm * ys_ref[...] * ss * (1.0 - ss)).astype(BF16)

    blk = pl.BlockSpec((tr, d), lambda i: (i, 0))
    return pl.pallas_call(
        body, name=name, grid=(t // tr,),
        in_specs=[blk, pl.BlockSpec((tr, d), lambda i: (i, ba)), pl.BlockSpec((tr, d), lambda i: (i, bs)), blk, blk],
        out_specs=[blk] * 4, out_shape=[_sds((t, d), BF16)] * 4,
        compiler_params=_cp("parallel"))(dmix, proj, proj, ya, ys)


def _loss_grad(h, target, *, name):
    t, d = h.shape
    tr = LANES
    assert ROW0 == tr

    def body(h_ref, t_ref, dh_ref, loss_ref):
        i = pl.program_id(0)

        @pl.when(i == 0)
        def _():
            dh_ref[...] = jnp.zeros((tr, d), F32)
            loss_ref[...] = jnp.zeros((1, LANES), F32)

        @pl.when(i > 0)
        def _():
            err = h_ref[...] - t_ref[...]
            dh_ref[...] = err * (1.0 / d)
            part = jnp.sum(jnp.sum(err * err, axis=1, keepdims=True), axis=0, keepdims=True)
            loss_ref[...] += jnp.broadcast_to(part * (0.5 / d), (1, LANES))

    blk = pl.BlockSpec((tr, d), lambda i: (i, 0))
    return pl.pallas_call(
        body, name=name, grid=(t // tr,),
        in_specs=[blk, pl.BlockSpec((tr, d), lambda i: (jnp.maximum(i - 1, 0), 0))],
        out_specs=[blk, pl.BlockSpec((1, LANES), lambda i: (0, 0))],
        out_shape=[_sds((t, d), F32), _sds((1, LANES), F32)],
        compiler_params=_cp("arbitrary"))(h, target)


def _adamw_update(gv, wv, mv, vv):
    c1 = 1.0 - ADAM_B1 ** ADAM_STEP
    c2 = 1.0 - ADAM_B2 ** ADAM_STEP
    nm = ADAM_B1 * mv + (1.0 - ADAM_B1) * gv
    nv = ADAM_B2 * vv + (1.0 - ADAM_B2) * (gv * gv)
    return -ADAM_LR * ((nm / c1) / (jnp.sqrt(nv / c2) + ADAM_EPS) + ADAM_WD * wv), nm, nv


def _as_2d(a):
    return a.reshape(1, -1) if a.ndim == 1 else a.reshape(-1, a.shape[-1])


def _adamw(g, w, m, v, *, name):
    shape = w.shape
    g2, w2, m2, v2 = (_as_2d(a) for a in (g, w, m, v))
    r, c = w2.shape
    tr = _tile(r, 256, base=8)

    def body(g_ref, w_ref, m_ref, v_ref, d_ref, nm_ref, nv_ref):
        d_ref[...], nm_ref[...], nv_ref[...] = _adamw_update(g_ref[...], w_ref[...], m_ref[...], v_ref[...])

    blk = pl.BlockSpec((tr, c), lambda i: (i, 0))
    outs = pl.pallas_call(body, name=name, grid=(r // tr,), in_specs=[blk] * 4, out_specs=[blk] * 3,
                          out_shape=[_sds((r, c), F32)] * 3, compiler_params=_cp("parallel"))(g2, w2, m2, v2)
    return [o.reshape(shape) for o in outs]


def _adamw_small(items, *, name):
    n = len(items)
    shapes = [it[1].shape for it in items]
    flat = [_as_2d(a) for it in items for a in it]

    def body(*refs):
        ins, outs = refs[:4 * n], refs[4 * n:]
        for k in range(n):
            g_ref, w_ref, m_ref, v_ref = ins[4 * k:4 * k + 4]
            d_ref, nm_ref, nv_ref = outs[3 * k:3 * k + 3]
            d_ref[...], nm_ref[...], nv_ref[...] = _adamw_update(g_ref[...], w_ref[...], m_ref[...], v_ref[...])

    out_shape = [_sds(flat[4 * k + 1].shape, F32) for k in range(n) for _ in range(3)]
    outs = pl.pallas_call(body, name=name, out_shape=out_shape,
                          compiler_params=pltpu.CompilerParams(vmem_limit_bytes=VMEM_LIMIT_BYTES))(*flat)
    return [[outs[3 * k + q].reshape(shapes[k]) for q in range(3)] for k in range(n)]


def _pair_add(g5, got, core, *, name):
    n, _, r, _ = g5.shape
    tr = _tile(r, 1024, base=8)

    def body(c_ref, a_ref, b_ref, o_ref):
        o_ref[...] = a_ref[0] + b_ref[...]

    grid_spec = pltpu.PrefetchScalarGridSpec(
        num_scalar_prefetch=1, grid=(n, r // tr),
        in_specs=[pl.BlockSpec((1, 1, tr, LANES), lambda s, i, c_ref: (s, c_ref[0], i, 0)),
                  pl.BlockSpec((1, tr, LANES), lambda s, i, c_ref: (s, i, 0))],
        out_specs=pl.BlockSpec((1, tr, LANES), lambda s, i, c_ref: (s, i, 0)))
    return pl.pallas_call(body, name=name, grid_spec=grid_spec, out_shape=_sds(got.shape, F32),
                          compiler_params=_cp("parallel", "parallel"))(core, g5, got)


def _sum_chips(q, core, *, name):
    n, r, _ = q.shape
    tr = _tile(r, 1024, base=8)

    def body(c_ref, q_ref, o_ref):
        tot = q_ref[0]
        for s in range(1, n):
            tot = tot + q_ref[s]
        o_ref[0] = tot

    grid_spec = pltpu.PrefetchScalarGridSpec(
        num_scalar_prefetch=1, grid=(r // tr,),
        in_specs=[pl.BlockSpec((n, tr, LANES), lambda i, c_ref: (0, i, 0))],
        out_specs=pl.BlockSpec((1, tr, LANES), lambda i, c_ref: (c_ref[0], i, 0)))
    return pl.pallas_call(body, name=name, grid_spec=grid_spec, out_shape=_sds((2, r, LANES), F32),
                          compiler_params=_cp("parallel"))(core, q)


_ANY = pl.BlockSpec(memory_space=pl.ANY)
_MESH = pl.DeviceIdType.MESH


def _place():
    x, y, c = lax.axis_index("x"), lax.axis_index("y"), lax.axis_index("c")
    return x, y, c, [(1 - x, y), (x, 1 - y), (1 - x, 1 - y)]


def _chip_allgather(mine, *, name):
    na = len(mine)

    def body(*refs):
        for phase in range(3):
            _chip_allgather_phase(phase, refs[:na], refs[na:2 * na], *refs[2 * na:])

    return pl.pallas_call(
        body, name=name, in_specs=[_ANY] * na, out_specs=[_ANY] * na,
        out_shape=_chip_allgather_shapes(mine), scratch_shapes=_chip_allgather_sems(na))(*mine)


def _chip_allgather_shapes(mine):
    return [_sds((N_CHIPS,) + a.shape, a.dtype) for a in mine]


def _chip_allgather_sems(na):
    return [pltpu.SemaphoreType.DMA((6 * na,)), pltpu.SemaphoreType.DMA((6 * na,)), pltpu.SemaphoreType.DMA((na,))]


def _chip_allgather_phase(phase, x_refs, o_refs, send_sems, recv_sems, local_sems):
    na = len(x_refs)
    x, y, c, chips = _place()
    k = 2 * x + y

    def copy(a, n, src, dst, to):
        return pltpu.make_async_remote_copy(src_ref=src, dst_ref=dst, send_sem=send_sems.at[6 * a + n],
                                            recv_sem=recv_sems.at[6 * a + n], device_id=to, device_id_type=_MESH)

    locals_ = [pltpu.make_async_copy(x_refs[a], o_refs[a].at[k], local_sems.at[a]) for a in range(na)]
    sends = [copy(a, n, x_refs[a].at[c], o_refs[a].at[k, c], (cx, cy, c))
             for a in range(na) for n, (cx, cy) in enumerate(chips)]
    landed = [(copy(a, n, o_refs[a].at[2 * cx + cy, c], o_refs[a].at[2 * cx + cy, c], (cx, cy, c)),
               copy(a, 3 + n, o_refs[a].at[2 * cx + cy, c], o_refs[a].at[2 * cx + cy, c], (x, y, 1 - c)))
              for a in range(na) for n, (cx, cy) in enumerate(chips)]
    if phase == 0:
        for cp in locals_ + sends:
            cp.start()
    elif phase == 1:
        for arrival, forward in landed:
            arrival.wait_recv()
            forward.start()
    else:
        for a in range(na):
            for n, (cx, cy) in enumerate(chips):
                slab = o_refs[a].at[2 * cx + cy, 1 - c]
                copy(a, 3 + n, slab, slab, (x, y, 1 - c)).wait_recv()
        for cp in sends + [forward for _, forward in landed]:
            cp.wait_send()
        for cp in locals_:
            cp.wait()


def _sibling_swap(g5, *, name):
    n, _, r, _ = g5.shape

    def body(x_ref, o_ref, send_sems, recv_sems):
        x, y, c, _ = _place()
        cps = [pltpu.make_async_remote_copy(src_ref=x_ref.at[s, 1 - c], dst_ref=o_ref.at[s], send_sem=send_sems.at[s],
                                            recv_sem=recv_sems.at[s], device_id=(x, y, 1 - c), device_id_type=_MESH)
               for s in range(n)]
        for cp in cps:
            cp.start()
        for cp in cps:
            cp.wait()

    return pl.pallas_call(
        body, name=name, in_specs=[_ANY], out_specs=_ANY, out_shape=_sds((n, r, LANES), g5.dtype),
        scratch_shapes=[pltpu.SemaphoreType.DMA((n,)), pltpu.SemaphoreType.DMA((n,))])(g5)


_CHIP_EXCHANGE_SEMS = [pltpu.SemaphoreType.DMA((3,)), pltpu.SemaphoreType.DMA((3,)), pltpu.SemaphoreType.DMA]


def _chip_exchange_copies(h_ref, q_ref, send_sems, recv_sems, local_sem):
    x, y, c, chips = _place()
    k = 2 * x + y
    local = pltpu.make_async_copy(h_ref.at[k], q_ref.at[k], local_sem)
    sends, arrivals = [], []
    for n, (cx, cy) in enumerate(chips):
        kk = 2 * cx + cy
        mk = functools.partial(pltpu.make_async_remote_copy, src_ref=h_ref.at[kk], send_sem=send_sems.at[n],
                               recv_sem=recv_sems.at[n], device_id=(cx, cy, c), device_id_type=_MESH)
        sends.append(mk(dst_ref=q_ref.at[k]))
        arrivals.append(mk(dst_ref=q_ref.at[kk]))
    return local, sends, arrivals


def _chip_exchange_start(copies):
    local, sends, _ = copies
    local.start()
    for cp in sends:
        cp.start()


def _chip_exchange_wait(copies):
    local, sends, arrivals = copies
    for cp in arrivals:
        cp.wait_recv()
    for cp in sends:
        cp.wait_send()
    local.wait()


def _chip_exchange(h, *, name):
    def body(h_ref, q_ref, send_sems, recv_sems, local_sem):
        copies = _chip_exchange_copies(h_ref, q_ref, send_sems, recv_sems, local_sem)
        _chip_exchange_start(copies)
        _chip_exchange_wait(copies)

    return pl.pallas_call(body, name=name, in_specs=[_ANY], out_specs=_ANY, out_shape=_sds(h.shape, h.dtype),
                          scratch_shapes=_CHIP_EXCHANGE_SEMS)(h)


def _sibling_allgather(buf, *, name):
    def body(x_ref, o_ref, send_sem, recv_sem):
        x, y, c, _ = _place()
        cp = pltpu.make_async_remote_copy(src_ref=x_ref.at[c], dst_ref=o_ref.at[c], send_sem=send_sem,
                                          recv_sem=recv_sem, device_id=(x, y, 1 - c), device_id_type=_MESH)
        cp.start()
        pltpu.make_async_remote_copy(src_ref=x_ref.at[c], dst_ref=o_ref.at[1 - c], send_sem=send_sem,
                                     recv_sem=recv_sem, device_id=(x, y, 1 - c), device_id_type=_MESH).wait_recv()
        cp.wait_send()

    return pl.pallas_call(
        body, name=name, in_specs=[_ANY], out_specs=_ANY, out_shape=_sds(buf.shape, buf.dtype),
        input_output_aliases={0: 0},
        scratch_shapes=[pltpu.SemaphoreType.DMA, pltpu.SemaphoreType.DMA])(buf)


def _chip_segments(g, kind):
    if kind == "col":
        n = g.shape[-1] // N_CHIPS
        s = g.reshape(g.shape[:-1] + (N_CHIPS, n))
        return jnp.moveaxis(s, -2, 0).reshape(N_CHIPS, -1)
    k = g.shape[-2] // N_CHIPS
    s = g.reshape(g.shape[:-2] + (N_CHIPS, k, g.shape[-1]))
    return jnp.moveaxis(s, -3, 0).reshape(N_CHIPS, -1)


def _join(blocks, kind):
    if kind == "col":
        s = jnp.moveaxis(blocks, 0, -2)
        return s.reshape(s.shape[:-2] + (s.shape[-2] * s.shape[-1],))
    return blocks.reshape((blocks.shape[0] * blocks.shape[1],) + blocks.shape[2:])


def _travel_form(local, names):
    mine = []
    for n in names:
        a = local[n]
        if n == "w_in":
            a = jnp.swapaxes(a, 1, 2)
        if n == "meta_tokens":
            a = a.reshape(2, N_META // 2, a.shape[-1])
        mine.append(a.astype(BF16) if n in GATHER_BF16 else a)
    return mine


def _rope_tables(t):
    half = QK_ROPE // 2
    inv_freq = 1.0 / (ROPE_THETA ** (jnp.arange(0, QK_ROPE, 2, dtype=F32) / QK_ROPE))
    pos = jnp.maximum(jnp.arange(t, dtype=F32) - PAD, 0.0)
    ang = pos[:, None] * inv_freq[None, :]
    cos, sin = jnp.cos(ang), jnp.sin(ang)
    z = jnp.zeros((t, half), F32)
    z2 = jnp.zeros((t, LANES - QK_ROPE), F32)
    return (jnp.concatenate([cos, cos, z2], axis=1), jnp.concatenate([-sin, z, z2], axis=1),
            jnp.concatenate([z, sin, z2], axis=1))


def _expand_matrix():
    lane = np.arange(SSD_INNER) // SSD_HEAD_DIM
    e = (np.arange(LANES)[:, None] == lane[None, :]).astype(np.float32)
    return jnp.asarray(e, BF16)


def _late_weights(full, i):
    w = {}
    kinds = dict(SHARDED)
    whole = lambda n: _join(full[n][:, i], kinds[n])
    for n in ("w_o_attn", "w_o_ssd", "w_out", "w_down"):
        w[n] = whole(n)
    w_up = whole("w_up")
    w["w_up_g"] = w_up[:, :D_FF]
    w["w_up_v"] = w_up[:, D_FF:]
    ffn_w = whole("ffn_conv_w")
    w["ffn_conv_wg"] = ffn_w[:, :D_FF]
    w["ffn_conv_wv"] = ffn_w[:, D_FF:]
    return w


def _early_weights(full, rep, i):
    w = {}
    kinds = dict(SHARDED)
    whole = lambda n: _join(full[n][:, i], kinds[n])
    wt = _join(full["w_in"][:, i], "row")
    zr = lambda n: jnp.zeros((n, D_MODEL), BF16)
    w["w_in_t"] = jnp.concatenate(
        [wt[0:1024], wt[1088:3136], wt[3136:6208], wt[6240:7264], wt[7264:8288],
         wt[1024:1088], zr(LANES - QK_ROPE), wt[6208:6240], zr(LANES - SSD_HEADS)], axis=0)
    wq = whole("w_q_b").reshape(Q_LORA, MLA_HEADS, QK_NOPE + QK_ROPE)
    w["w_q_b"] = jnp.pad(wq, ((0, 0), (0, 0), (0, QHEAD - QK_NOPE - QK_ROPE))).reshape(Q_LORA, MLA_HEADS * QHEAD)
    wkv = whole("w_kv_b").reshape(KV_LORA, MLA_HEADS, 2, QK_NOPE)
    w["w_kv_b"] = jnp.swapaxes(wkv, 1, 2).reshape(KV_LORA, 2 * MLA_HEADS * QK_NOPE)
    w["ssd_conv_w"] = whole("ssd_conv_w")
    row = lambda v: v.reshape(1, -1)
    w["q_norm_g"] = row(rep["q_norm_g"][i])
    w["kv_norm_g"] = row(rep["kv_norm_g"][i])
    w["ssd_conv_b"] = row(rep["ssd_conv_b"][i])
    w["dt_bias"] = row(jnp.pad(rep["dt_bias"][i], (0, LANES - SSD_HEADS)))
    a = -jnp.exp(rep["a_log"][i])
    w["a"] = a
    w["a_x"] = row(jnp.repeat(a, SSD_HEAD_DIM))
    w["d_x"] = row(jnp.repeat(rep["d_skip"][i], SSD_HEAD_DIM))
    w["ssd_norm_g"] = row(rep["ssd_norm_g"][i])
    w["ffn_conv_bg"] = row(rep["ffn_conv_b"][i][:D_FF])
    w["ffn_conv_bv"] = row(rep["ffn_conv_b"][i][D_FF:])
    for n in ("ln1_g", "ln1_b", "ln2_g", "ln2_b"):
        w[n] = row(rep[n][i])
    return w


def _layer_fwd(h, hb, w, tabs, expand, layer, late):
    tag = f"l{layer}_"
    cos, sin_a, sin_b = tabs
    s = {"h": h, "hb": hb}
    proj = _mm(hb, w["w_in_t"], trans_b=True, name=tag + "in_proj", tn=768)
    s["proj"] = proj
    qn = _rms_fwd(proj, OFF_Q, Q_LORA, w["q_norm_g"], name=tag + "q_norm")
    q_raw = _mm(qn, w["w_q_b"], name=tag + "q_up")
    q = _rope_q_fwd(q_raw, cos, sin_a, sin_b, name=tag + "q_rope")
    kvn = _rms_fwd(proj, OFF_KV, KV_LORA, w["kv_norm_g"], name=tag + "kv_norm")
    kv = _mm(kvn, w["w_kv_b"], name=tag + "kv_up", out_dtype=BF16)
    kpe = _rope_k_fwd(proj, cos, sin_a, sin_b, name=tag + "k_rope")
    if isinstance(late, dict):
        o, lse = _flash_fwd(q, kv, kpe, name=tag + "attn")
    else:
        o, lse, *got = _flash_fwd(q, kv, kpe, name=tag + "attn", gather=late)
        late = dict(zip(GATHER_LATE, got))
    w.update(_late_weights(late, layer))
    ya = _mm(o, w["w_o_attn"], name=tag + "attn_out")
    s.update(qn=qn, q=q, kvn=kvn, kv=kv, kpe=kpe, o=o, lse=lse, ya=ya)
    xs, bm, cm, dtx = _ssd_prep_fwd(proj, w["ssd_conv_w"], w["ssd_conv_b"], w["dt_bias"], expand, name=tag + "ssd_prep")
    y, prev = _ssd_fwd(xs, dtx, bm, cm, bm.T, w["a_x"], w["d_x"], name=tag + "ssd_scan")
    yn = _gnorm_fwd(y, proj, w["ssd_norm_g"], name=tag + "ssd_norm")
    ys = _mm(yn, w["w_o_ssd"], name=tag + "ssd_out")
    s.update(xs=xs, bm=bm, cm=cm, dtx=dtx, y=y, prev=prev, yn=yn, ys=ys)
    mixed = _mix_fwd(proj, ya, ys, name=tag + "mix")
    br = _mm(mixed, w["w_out"], name=tag + "mix_out")
    pre1, h1, h1b = _ln_fwd(h, br, w["ln1_g"], w["ln1_b"], name=tag + "ln1")
    s.update(mixed=mixed, pre1=pre1, h1b=h1b)
    ug = _mm(h1b, w["w_up_g"], name=tag + "up_g", tn=1408)
    uv = _mm(h1b, w["w_up_v"], name=tag + "up_v", tn=1408)
    act = _ffn_act_fwd(ug, uv, w["ffn_conv_wg"], w["ffn_conv_wv"], w["ffn_conv_bg"], w["ffn_conv_bv"],
                       name=tag + "ffn_act")
    ffn = _mm(act, w["w_down"], name=tag + "down")
    pre2, h2, h2b = _ln_fwd(h1, ffn, w["ln2_g"], w["ln2_b"], name=tag + "ln2")
    s.update(ug=ug, uv=uv, act=act, pre2=pre2)
    return h2, h2b, s, late


def _layer_bwd(dh2, w, s, tabs, reduce_m, tag, begin_exchange=None):
    cos, sin_a, sin_b = tabs
    g = {}
    proj = s["proj"]
    dpre2, dpre2b, g["ln2_g"], g["ln2_b"] = _ln_bwd(dh2, s["pre2"], w["ln2_g"], name=tag + "ln2_bwd")
    g["w_down"] = _mm_tn(s["act"], dpre2b, name=tag + "down_dw")
    dact = _mm(dpre2b, w["w_down"], trans_b=True, name=tag + "down_dx", tn=1408)
    dcg, dcv, dwg, dwv, dbg, dbv = _ffn_act_bwd(s["ug"], s["uv"], dact, w["ffn_conv_wg"], w["ffn_conv_wv"],
                                                w["ffn_conv_bg"], w["ffn_conv_bv"], name=tag + "ffn_act_bwd")
    g["ffn_conv_w"] = jnp.concatenate([dwg, dwv], axis=1)
    g["ffn_conv_b"] = jnp.concatenate([dbg, dbv], axis=1).reshape(-1)
    dug = _conv_bwd_input(dcg, w["ffn_conv_wg"], FFN_CONV, name=tag + "ffn_conv_bwd_g", tc=1408)
    duv = _conv_bwd_input(dcv, w["ffn_conv_wv"], FFN_CONV, name=tag + "ffn_conv_bwd_v", tc=1408)
    g["w_up"] = jnp.concatenate([_mm_tn(s["h1b"], dug, name=tag + "up_g_dw", tn=1408),
                                 _mm_tn(s["h1b"], duv, name=tag + "up_v_dw", tn=1408)], axis=1)
    dh1 = _mm(dug, w["w_up_g"], trans_b=True, add=dpre2, add_scale=ALPHA, name=tag + "up_g_dx")
    dh1 = _mm(duv, w["w_up_v"], trans_b=True, add=dh1, name=tag + "up_v_dx")
    dpre1, dpre1b, g["ln1_g"], g["ln1_b"] = _ln_bwd(dh1, s["pre1"], w["ln1_g"], name=tag + "ln1_bwd")
    g["w_out"] = _mm_tn(s["mixed"], dpre1b, name=tag + "mix_out_dw")
    dmix = _mm(dpre1b, w["w_out"], trans_b=True, name=tag + "mix_out_dx")
    dya, dys, dga, dgs = _mix_bwd(dmix, proj, s["ya"], s["ys"], name=tag + "mix_bwd")
    g["w_o_ssd"] = _mm_tn(s["yn"], dys, name=tag + "ssd_out_dw")
    dyn = _mm(dys, w["w_o_ssd"], trans_b=True, out_dtype=BF16, name=tag + "ssd_out_dx")
    dy, dz, dgn = _gnorm_bwd(dyn, s["y"], proj, w["ssd_norm_g"], name=tag + "ssd_norm_bwd")
    g["ssd_norm_g"] = dgn.reshape(-1)
    dxs, ddtx, dbm, dcm, da_x, dd_x = _ssd_bwd(s["xs"], s["dtx"], s["bm"], s["cm"], s["cm"].T, s["prev"], dy,
                                               w["a_x"], w["d_x"], name=tag + "ssd_scan_bwd")
    g["a_log"] = da_x.reshape(SSD_HEADS, SSD_HEAD_DIM).sum(axis=1) * w["a"]
    g["d_skip"] = dd_x.reshape(SSD_HEADS, SSD_HEAD_DIM).sum(axis=1)
    dconv, ddtr, dcw, dcb, ddtb = _ssd_prep_bwd_a(proj, dxs, dbm, dcm, ddtx, w["ssd_conv_w"], w["ssd_conv_b"],
                                                  w["dt_bias"], reduce_m, name=tag + "ssd_prep_bwd")
    g["ssd_conv_w"] = dcw
    g["ssd_conv_b"] = dcb.reshape(-1)
    g["dt_bias"] = ddtb.reshape(-1)[:SSD_HEADS]
    dxbc = _conv_bwd_input(dconv, w["ssd_conv_w"], SSD_CONV, name=tag + "ssd_conv_bwd", tc=1024)
    g["w_o_attn"] = _mm_tn(s["o"], dya, name=tag + "attn_out_dw")
    exchange = begin_exchange(dict(g)) if begin_exchange else None
    do = _mm(dya, w["w_o_attn"], trans_b=True, out_dtype=BF16, name=tag + "attn_out_dx")
    delta = _attn_delta(do, s["o"], name=tag + "attn_delta")
    by_tile = lambda a: a.reshape(MLA_HEADS, -1, _tile(a.shape[1], 640))
    dq, dkn, dkp, dv, *exchanged = _flash_bwd(s["q"], s["kv"], s["kpe"], do, by_tile(s["lse"]), by_tile(delta),
                                              name=tag + "attn_bwd", exchange=exchange)
    dq_raw = _rope_q_bwd(dq, cos, sin_a, sin_b, name=tag + "q_rope_bwd")
    dwq = _mm_tn(s["qn"], dq_raw, name=tag + "q_up_dw")
    g["w_q_b"] = dwq.reshape(Q_LORA, MLA_HEADS, QHEAD)[:, :, :QK_NOPE + QK_ROPE].reshape(Q_LORA, -1)
    dqn = _mm(dq_raw, w["w_q_b"], trans_b=True, out_dtype=BF16, name=tag + "q_up_dx")
    dqlat, dgq = _rms_bwd(dqn, proj, OFF_Q, Q_LORA, w["q_norm_g"], name=tag + "q_norm_bwd")
    g["q_norm_g"] = dgq.reshape(-1)
    dkv = jnp.concatenate([dkn, dv], axis=1)
    dwkv = _mm_tn(s["kvn"], dkv, name=tag + "kv_up_dw")
    g["w_kv_b"] = jnp.swapaxes(dwkv.reshape(KV_LORA, 2, MLA_HEADS, QK_NOPE), 1, 2).reshape(KV_LORA, -1)
    dkvn = _mm(dkv, w["w_kv_b"], trans_b=True, out_dtype=BF16, name=tag + "kv_up_dx")
    dkvlat, dgkv = _rms_bwd(dkvn, proj, OFF_KV, KV_LORA, w["kv_norm_g"], name=tag + "kv_norm_bwd")
    g["kv_norm_g"] = dgkv.reshape(-1)
    dkpe = _rope_k_bwd(dkp, cos, sin_a, sin_b, name=tag + "k_rope_bwd")
    h = s["hb"]
    comps = ((dqlat, OFF_Q), (dkvlat, OFF_KV), (dz, OFF_Z), (dxbc, OFF_XBC), (dga, OFF_GA), (dgs, OFF_GS),
             (dkpe, OFF_KPE), (ddtr, OFF_DT))
    dws = {off: _mm_tn(dc, h, name=f"{tag}in_dw{n}") for n, (dc, off) in enumerate(comps)}
    with_w = lambda group: [(dc, w["w_in_t"][off:off + dc.shape[1]]) for dc, off in group]
    wide = [c for c in comps if c[1] in (OFF_Z, OFF_XBC)]
    rest = [c for c in comps if c[1] not in (OFF_Z, OFF_XBC)]
    dh = _mm_sum(with_w(wide), dpre1, add_scale=ALPHA, name=tag + "in_dx_wide")
    dh = _mm_sum(with_w(rest), dh, name=tag + "in_dx_rest")
    g["w_in"] = jnp.concatenate([dws[OFF_Q], dws[OFF_KV], dws[OFF_KPE][:QK_ROPE], dws[OFF_Z], dws[OFF_XBC],
                                 dws[OFF_DT][:SSD_HEADS], dws[OFF_GA], dws[OFF_GS]], axis=0)
    return dh, g, (exchanged[0] if exchanged else None)


def _local_step(x, target, full, late, rep):
    seq = x.shape[0]
    t = seq + ROW0
    tabs = _rope_tables(t)
    expand = _expand_matrix()
    reduce_m = expand.T
    meta = _join(full["meta_tokens"].reshape(N_CHIPS, N_META, -1), "col")
    xin = jnp.concatenate([jnp.zeros((PAD, D_MODEL), F32), meta, x], axis=0)
    row = lambda v: v.reshape(1, -1)
    _, h, hb = _ln_fwd(xin, None, row(rep["emb_ln_g"]), row(rep["emb_ln_b"]), name="emb_ln")
    ws, saved = [], []
    for i in range(DEPTH):
        w = _early_weights(full, rep, i)
        h, hb, s, late = _layer_fwd(h, hb, w, tabs, expand, i, late)
        ws.append(w)
        saved.append(s)
    dh, loss = _loss_grad(h, target, name="loss")
    dh, g1, _ = _layer_bwd(dh, ws[1], saved[1], tabs, reduce_m, "l1_")
    layouts = []

    def begin_exchange(g0_so_far):
        g4, layout = _pack_part([((1, n), g) for n, g in g1.items()] + [((0, n), g) for n, g in g0_so_far.items()])
        layouts.append(layout)
        return _reduce_begin(g4, "reduce1_")

    dh, g0, parts1 = _layer_bwd(dh, ws[0], saved[0], tabs, reduce_m, "l0_", begin_exchange=begin_exchange)
    dxin, _, dg, db = _ln_bwd(dh, xin, row(rep["emb_ln_g"]), name="emb_ln_bwd")
    rest = [((0, n), g0[n]) for n in LATE_GRADS]
    rest += [((0, "emb_ln_g"), dg), ((0, "emb_ln_b"), db), ((0, "meta_tokens"), dxin[PAD:ROW0])]
    return loss, dxin[ROW0:], rest, parts1, layouts[0]


def _pack_part(entries):
    kinds = dict(SHARDED)
    pieces, layout, tail, row0 = [], [], [], 0

    def add(key, piece):
        nonlocal row0
        pieces.append(piece)
        layout.append((key, row0, piece.shape[1]))
        row0 += piece.shape[1]

    for key, g in entries:
        if key[1] in kinds:
            kind = "row" if key[1] == "w_in" else kinds[key[1]]
            add(key, _chip_segments(g, kind).reshape(N_CHIPS, -1, LANES))
    for key, g in entries:
        if key[1] not in kinds and g.size % LANES == 0:
            add(key, jnp.broadcast_to(g.reshape(1, -1, LANES), (N_CHIPS, g.size // LANES, LANES)))
        elif key[1] not in kinds:
            tail.append((key, g.reshape(-1)))
    if tail:
        vec = jnp.concatenate([g for _, g in tail])
        vec = jnp.pad(vec, (0, -vec.shape[0] % LANES)).reshape(1, -1, LANES)
        add(("tail", tuple((key, g.shape[0]) for key, g in tail)), jnp.broadcast_to(vec, (N_CHIPS,) + vec.shape[1:]))
    rows = -(-row0 // REDUCE_ROW_ALIGN) * REDUCE_ROW_ALIGN
    pieces.append(jnp.zeros((N_CHIPS, rows - row0, LANES), F32))
    return jnp.concatenate(pieces, axis=1), layout


def _core_index():
    return lax.axis_index("c").astype(jnp.int32).reshape(1)


def _reduce_begin(g4, tag):
    n, r, _ = g4.shape
    g5 = g4.reshape(n, 2, r // 2, LANES)
    got = _sibling_swap(g5, name=tag + "pair_swap")
    return _pair_add(g5, got, _core_index(), name=tag + "pair_add")


def _reduce_end(parts, tag):
    half = _sum_chips(parts, _core_index(), name=tag + "chip_sum")
    both = _sibling_allgather(half, name=tag + "pair_gather")
    return both.reshape(-1, LANES)


def _unpack_part(flat, layout, shapes):
    out = {}
    for key, row0, rows in layout:
        piece = flat[row0:row0 + rows]
        if key[0] == "tail":
            vec, off = piece.reshape(-1), 0
            for sub, size in key[1]:
                out[sub] = vec[off:off + size]
                off += size
        elif key[1] == "w_in":
            out[key] = piece.reshape(shapes[key[1]][1], shapes[key[1]][0]).T
        else:
            out[key] = piece.reshape(shapes[key[1]])
    return out


def kernel(x, meta_tokens, emb_ln_g, emb_ln_b, w_in, q_norm_g, w_q_b, kv_norm_g, w_kv_b, w_o_attn, ssd_conv_w, ssd_conv_b, dt_bias, a_log, d_skip, ssd_norm_g, w_o_ssd, w_out, ln1_g, ln1_b, w_up, ffn_conv_w, ffn_conv_b, w_down, ln2_g, ln2_b, loss_target, m_meta_tokens, m_emb_ln_g, m_emb_ln_b, m_w_in, m_q_norm_g, m_w_q_b, m_kv_norm_g, m_w_kv_b, m_w_o_attn, m_ssd_conv_w, m_ssd_conv_b, m_dt_bias, m_a_log, m_d_skip, m_ssd_norm_g, m_w_o_ssd, m_w_out, m_ln1_g, m_ln1_b, m_w_up, m_ffn_conv_w, m_ffn_conv_b, m_w_down, m_ln2_g, m_ln2_b, v_meta_tokens, v_emb_ln_g, v_emb_ln_b, v_w_in, v_q_norm_g, v_w_q_b, v_kv_norm_g, v_w_kv_b, v_w_o_attn, v_ssd_conv_w, v_ssd_conv_b, v_dt_bias, v_a_log, v_d_skip, v_ssd_norm_g, v_w_o_ssd, v_w_out, v_ln1_g, v_ln1_b, v_w_up, v_ffn_conv_w, v_ffn_conv_b, v_w_down, v_ln2_g, v_ln2_b):
    given = dict(locals())
    local_w = {n: given[n] for n in WEIGHTS}
    local_m = {n: given["m_" + n] for n in WEIGHTS}
    local_v = {n: given["v_" + n] for n in WEIGHTS}
    full = dict(zip(GATHER_EARLY, _chip_allgather(_travel_form(local_w, GATHER_EARLY), name="gather_early")))
    rep = {n: local_w[n] for n in REPLICATED}
    loss, grad_x, rest, parts1, layout1 = _local_step(x[0], loss_target[0], full, _travel_form(local_w, GATHER_LATE), rep)
    g4, layout0 = _pack_part(rest)
    parts0 = _chip_exchange(_reduce_begin(g4, "reduce0_"), name="reduce0_chip_exchange")
    shapes = {n: (local_w[n].shape if n in OUTSIDE else local_w[n].shape[1:]) for n in WEIGHTS}
    summed = _unpack_part(_reduce_end(parts0, "reduce0_"), layout0, shapes)
    summed.update(_unpack_part(_reduce_end(parts1, "reduce1_"), layout1, shapes))
    grad = {n: (summed[0, n] if n in OUTSIDE else jnp.stack([summed[i, n] for i in range(DEPTH)])) for n in WEIGHTS}
    upd = {}
    small = [n for n in WEIGHTS if n not in GATHER_BF16]
    for n in GATHER_BF16:
        upd[n] = _adamw(grad[n], local_w[n], local_m[n], local_v[n], name="adamw_" + n)
    res = _adamw_small([(grad[n], local_w[n], local_m[n], local_v[n]) for n in small], name="adamw_small")
    upd.update(zip(small, res))
    total = lax.psum(loss[0, 0], ("x", "y", "c"))
    outs = [total, grad_x[None]] + [grad[n] for n in WEIGHTS]
    for q in range(3):
        outs.extend(upd[n][q] for n in WEIGHTS)
    return tuple(outs)
```

```python
import functools
import math

import numpy as np
import jax
import jax.numpy as jnp
from jax import lax
from jax.experimental import pallas as pl
from jax.experimental.pallas import tpu as pltpu

F32 = jnp.float32
BF16 = jnp.bfloat16

D_MODEL = 1024
N_META = 16
DEPTH = 2
MLA_HEADS = 8
Q_LORA = 768
KV_LORA = 256
QK_NOPE = 128
QK_ROPE = 64
V_HEAD = 128
ROPE_THETA = 10000.0
NEG_INF = -1e30
PAD_KEY_SCORE = -1e30
SSD_INNER = 2048
SSD_HEAD_DIM = 64
SSD_HEADS = 32
SSD_GROUPS = 4
SSD_STATE = 128
SSD_CONV = 4
SSD_CONV_DIM = 3072
CHUNK = 128
D_FF = 2816
FFN_CONV = 3
LN_EPS = 1e-5
RMS_EPS = 1e-6
ALPHA = (2 * DEPTH) ** 0.25
ATTN_SCALE = (QK_NOPE + QK_ROPE) ** -0.5
LOG2E = math.log2(math.e)
LN2 = math.log(2.0)
Q_SCALE = ATTN_SCALE * LOG2E
ADAM_LR = 0.001
ADAM_B1 = 0.9
ADAM_B2 = 0.999
ADAM_EPS = 1e-08
ADAM_WD = 0.01
ADAM_STEP = 10

LANES = 128
PAD = 112
ROW0 = PAD + N_META
QHEAD = 256
GROUP_W = SSD_INNER // SSD_GROUPS
HALO = 8
VMEM_LIMIT_BYTES = 56 * 1024 * 1024
MM_VMEM_BUDGET = 46 * 1024 * 1024
MM_MAX_ROW_TILE = 1664
N_CHIPS = 4

OFF_Q, OFF_KV, OFF_Z, OFF_XBC, OFF_GA, OFF_GS, OFF_KPE, OFF_DT = 0, 768, 1024, 3072, 6144, 7168, 8192, 8320
IN_COLS_P = 8448

NT_DIMS = (((1,), (1,)), ((), ()))
NN_DIMS = (((1,), (0,)), ((), ()))
TN_DIMS = (((0,), (0,)), ((), ()))

SHARDED = (("meta_tokens", "col"), ("w_in", "col"), ("w_q_b", "col"), ("w_kv_b", "col"), ("w_o_attn", "row"),
           ("ssd_conv_w", "col"), ("w_o_ssd", "row"), ("w_out", "row"), ("w_up", "col"), ("ffn_conv_w", "col"),
           ("w_down", "row"))
REPLICATED = ("emb_ln_g", "emb_ln_b", "q_norm_g", "kv_norm_g", "ssd_conv_b", "dt_bias", "a_log", "d_skip",
              "ssd_norm_g", "ln1_g", "ln1_b", "ffn_conv_b", "ln2_g", "ln2_b")
WEIGHTS = ("meta_tokens", "emb_ln_g", "emb_ln_b", "w_in", "q_norm_g", "w_q_b", "kv_norm_g", "w_kv_b", "w_o_attn",
           "ssd_conv_w", "ssd_conv_b", "dt_bias", "a_log", "d_skip", "ssd_norm_g", "w_o_ssd", "w_out", "ln1_g",
           "ln1_b", "w_up", "ffn_conv_w", "ffn_conv_b", "w_down", "ln2_g", "ln2_b")
GATHER_BF16 = ("w_in", "w_q_b", "w_kv_b", "w_o_attn", "w_o_ssd", "w_out", "w_up", "w_down")
GATHER_EARLY = ("meta_tokens", "w_in", "w_q_b", "w_kv_b", "ssd_conv_w")
GATHER_LATE = ("w_o_attn", "w_o_ssd", "w_out", "w_up", "ffn_conv_w", "w_down")
OUTSIDE = ("meta_tokens", "emb_ln_g", "emb_ln_b")
LATE_GRADS = ("w_in", "w_q_b", "q_norm_g", "w_kv_b", "kv_norm_g")
TRANSPOSED_GRADS = ("w_in", "w_up")
REDUCE_ROW_ALIGN = 512


def _tile(n, target, base=LANES):
    best = None
    d = base
    while d <= min(n, target):
        if n % d == 0:
            best = d
        d += base
    return n if best is None else best


def _cp(*sem):
    return pltpu.CompilerParams(dimension_semantics=sem, vmem_limit_bytes=VMEM_LIMIT_BYTES)


def _sds(shape, dtype):
    return jax.ShapeDtypeStruct(shape, dtype)


def _row_ids(i, tr, shape):
    return i * tr + lax.broadcasted_iota(jnp.int32, shape, 0)


def _sigmoid(x):
    return 1.0 / (1.0 + jnp.exp(-x))


def _mm_tiles(m, n, tn_max, tk, nk, a_bytes, b_bytes, out_bytes, add_bytes):
    divisors = lambda size, cap: [d for d in range(min(size, cap) // LANES * LANES, 0, -LANES) if size % d == 0]
    for tm in divisors(m, MM_MAX_ROW_TILE):
        for tn in divisors(n, tn_max):
            blocks = 2 * (tm * tk * a_bytes + tk * tn * b_bytes + tm * tn * (out_bytes + add_bytes))
            temps = tm * tn * 4 * (2 if nk > 1 else 1) + tm * tk * 2 + tk * tn * 2
            if blocks + temps <= MM_VMEM_BUDGET:
                return tm, tn
    return LANES, LANES


def _mm(a, b, *, name, trans_b=False, out_dtype=F32, add=None, add_scale=1.0, tn=1024, tk=1408):
    m, k_dim = a.shape
    n = b.shape[0] if trans_b else b.shape[1]
    tk = _tile(k_dim, tk)
    nk = k_dim // tk
    has_add = add is not None
    tm, tn = _mm_tiles(m, n, tn, tk, nk, a.dtype.itemsize, b.dtype.itemsize, jnp.dtype(out_dtype).itemsize,
                       add.dtype.itemsize if has_add else 0)
    dims = NT_DIMS if trans_b else NN_DIMS

    def body(*refs):
        a_ref, b_ref = refs[0], refs[1]
        r_ref = refs[2] if has_add else None
        o_ref = refs[3] if has_add else refs[2]
        part = lax.dot_general(a_ref[...].astype(BF16), b_ref[...].astype(BF16), dims, preferred_element_type=F32)

        def finish(r):
            if has_add:
                r = r + add_scale * r_ref[...].astype(F32)
            o_ref[...] = r.astype(out_dtype)

        if nk == 1:
            finish(part)
        else:
            acc = refs[-1]
            kk = pl.program_id(2)

            @pl.when(kk == 0)
            def _():
                acc[...] = part

            @pl.when(kk > 0)
            def _():
                acc[...] += part

            @pl.when(kk == nk - 1)
            def _():
                finish(acc[...])

    in_specs = [pl.BlockSpec((tm, tk), lambda i, j, kk: (i, kk)),
                pl.BlockSpec((tn, tk), lambda i, j, kk: (j, kk)) if trans_b
                else pl.BlockSpec((tk, tn), lambda i, j, kk: (kk, j))]
    args = [a, b]
    if has_add:
        in_specs.append(pl.BlockSpec((tm, tn), lambda i, j, kk: (i, j)))
        args.append(add)
    return pl.pallas_call(
        body, name=name, grid=(m // tm, n // tn, nk), in_specs=in_specs,
        out_specs=pl.BlockSpec((tm, tn), lambda i, j, kk: (i, j)),
        out_shape=_sds((m, n), out_dtype),
        scratch_shapes=[pltpu.VMEM((tm, tn), F32)] if nk > 1 else [],
        compiler_params=_cp("parallel", "parallel", "arbitrary"),
    )(*args)


def _mm_sum(pairs, add, *, name, add_scale=1.0, tm=640):
    m, n = add.shape
    tm = _tile(m, tm)
    npairs = len(pairs)

    def body(*refs):
        a_refs, b_refs = refs[:npairs], refs[npairs:2 * npairs]
        r_ref, o_ref = refs[2 * npairs], refs[2 * npairs + 1]
        acc = add_scale * r_ref[...]
        for a_ref, b_ref in zip(a_refs, b_refs):
            acc = acc + jnp.dot(a_ref[...].astype(BF16), b_ref[...].astype(BF16), preferred_element_type=F32)
        o_ref[...] = acc

    in_specs = ([pl.BlockSpec((tm, a.shape[1]), lambda i: (i, 0)) for a, _ in pairs]
                + [pl.BlockSpec(b.shape, lambda i: (0, 0)) for _, b in pairs]
                + [pl.BlockSpec((tm, n), lambda i: (i, 0))])
    return pl.pallas_call(
        body, name=name, grid=(m // tm,), in_specs=in_specs, out_specs=pl.BlockSpec((tm, n), lambda i: (i, 0)),
        out_shape=_sds((m, n), F32), compiler_params=_cp("parallel"),
    )(*[a for a, _ in pairs], *[b for _, b in pairs], add)


def _mm_tn(a, b, *, name, tko=1408, tn=1024, tt=640):
    t, k_dim = a.shape
    n = b.shape[1]
    tko, tn, tt = _tile(k_dim, tko), _tile(n, tn), _tile(t, tt)

    def body(a_ref, b_ref, o_ref):
        part = lax.dot_general(a_ref[...].astype(BF16), b_ref[...].astype(BF16), TN_DIMS, preferred_element_type=F32)
        tt_i = pl.program_id(2)

        @pl.when(tt_i == 0)
        def _():
            o_ref[...] = part

        @pl.when(tt_i > 0)
        def _():
            o_ref[...] += part

    return pl.pallas_call(
        body, name=name, grid=(k_dim // tko, n // tn, t // tt),
        in_specs=[pl.BlockSpec((tt, tko), lambda i, j, s: (s, i)), pl.BlockSpec((tt, tn), lambda i, j, s: (s, j))],
        out_specs=pl.BlockSpec((tko, tn), lambda i, j, s: (i, j)),
        out_shape=_sds((k_dim, n), F32),
        compiler_params=_cp("parallel", "parallel", "arbitrary"),
    )(a, b)


def _ln_fwd(h, branch, g, b, *, name):
    t, d = h.shape
    tr = _tile(t, 640)
    has_branch = branch is not None

    def body(*refs):
        if has_branch:
            h_ref, br_ref, g_ref, b_ref, pre_ref, o_ref, ob_ref = refs
            pre = ALPHA * h_ref[...] + br_ref[...]
            pre_ref[...] = pre
        else:
            h_ref, g_ref, b_ref, o_ref, ob_ref = refs
            pre = h_ref[...]
        mu = jnp.mean(pre, axis=1, keepdims=True)
        xc = pre - mu
        var = jnp.mean(xc * xc, axis=1, keepdims=True)
        y = xc * lax.rsqrt(var + LN_EPS) * g_ref[...] + b_ref[...]
        rows = _row_ids(pl.program_id(0), tr, (tr, 1))
        y = jnp.where(rows >= PAD, y, 0.0)
        o_ref[...] = y
        ob_ref[...] = y.astype(BF16)

    row_spec = pl.BlockSpec((tr, d), lambda i: (i, 0))
    vec_spec = pl.BlockSpec((1, d), lambda i: (0, 0))
    if has_branch:
        return pl.pallas_call(
            body, name=name, grid=(t // tr,), in_specs=[row_spec, row_spec, vec_spec, vec_spec],
            out_specs=[row_spec] * 3, out_shape=[_sds((t, d), F32), _sds((t, d), F32), _sds((t, d), BF16)],
            compiler_params=_cp("parallel"))(h, branch, g, b)
    out, out_b = pl.pallas_call(
        body, name=name, grid=(t // tr,), in_specs=[row_spec, vec_spec, vec_spec],
        out_specs=[row_spec] * 2, out_shape=[_sds((t, d), F32), _sds((t, d), BF16)],
        compiler_params=_cp("parallel"))(h, g, b)
    return h, out, out_b


def _ln_bwd(dy, pre, g, *, name):
    t, d = pre.shape
    tr = _tile(t, 640)

    def body(dy_ref, pre_ref, g_ref, dpre_ref, dpre_b_ref, dg_ref, db_ref):
        i = pl.program_id(0)
        pre_v = pre_ref[...]
        mu = jnp.mean(pre_v, axis=1, keepdims=True)
        xc = pre_v - mu
        var = jnp.mean(xc * xc, axis=1, keepdims=True)
        rstd = lax.rsqrt(var + LN_EPS)
        xhat = xc * rstd
        rows = _row_ids(i, tr, (tr, 1))
        dym = jnp.where(rows >= PAD, dy_ref[...], 0.0)
        gdy = dym * g_ref[...]
        m1 = jnp.mean(gdy, axis=1, keepdims=True)
        m2 = jnp.mean(gdy * xhat, axis=1, keepdims=True)
        dpre = rstd * (gdy - m1 - xhat * m2)
        dpre_ref[...] = dpre
        dpre_b_ref[...] = dpre.astype(BF16)
        pg = jnp.sum(dym * xhat, axis=0, keepdims=True)
        pb = jnp.sum(dym, axis=0, keepdims=True)

        @pl.when(i == 0)
        def _():
            dg_ref[...] = pg
            db_ref[...] = pb

        @pl.when(i > 0)
        def _():
            dg_ref[...] += pg
            db_ref[...] += pb

    row_spec = pl.BlockSpec((tr, d), lambda i: (i, 0))
    vec_spec = pl.BlockSpec((1, d), lambda i: (0, 0))
    return pl.pallas_call(
        body, name=name, grid=(t // tr,), in_specs=[row_spec, row_spec, vec_spec],
        out_specs=[row_spec, row_spec, vec_spec, vec_spec],
        out_shape=[_sds((t, d), F32), _sds((t, d), BF16), _sds((1, d), F32), _sds((1, d), F32)],
        compiler_params=_cp("arbitrary"))(dy, pre, g)


def _rms_fwd(proj, col_off, width, g, *, name):
    t = proj.shape[0]
    tr = _tile(t, 640)
    cb = col_off // width

    def body(x_ref, g_ref, o_ref):
        x = x_ref[...]
        r = lax.rsqrt(jnp.mean(x * x, axis=1, keepdims=True) + RMS_EPS)
        o_ref[...] = (x * r * g_ref[...]).astype(BF16)

    return pl.pallas_call(
        body, name=name, grid=(t // tr,),
        in_specs=[pl.BlockSpec((tr, width), lambda i: (i, cb)), pl.BlockSpec((1, width), lambda i: (0, 0))],
        out_specs=pl.BlockSpec((tr, width), lambda i: (i, 0)), out_shape=_sds((t, width), BF16),
        compiler_params=_cp("parallel"))(proj, g)


def _rms_bwd(dy, proj, col_off, width, g, *, name):
    t = proj.shape[0]
    tr = _tile(t, 640)
    cb = col_off // width

    def body(dy_ref, x_ref, g_ref, dx_ref, dg_ref):
        i = pl.program_id(0)
        x = x_ref[...]
        dyv = dy_ref[...].astype(F32)
        r = lax.rsqrt(jnp.mean(x * x, axis=1, keepdims=True) + RMS_EPS)
        gdy = dyv * g_ref[...]
        m = jnp.mean(x * gdy, axis=1, keepdims=True)
        dx_ref[...] = (r * gdy - x * (r * r * r) * m).astype(BF16)
        pg = jnp.sum(dyv * x * r, axis=0, keepdims=True)

        @pl.when(i == 0)
        def _():
            dg_ref[...] = pg

        @pl.when(i > 0)
        def _():
            dg_ref[...] += pg

    return pl.pallas_call(
        body, name=name, grid=(t // tr,),
        in_specs=[pl.BlockSpec((tr, width), lambda i: (i, 0)), pl.BlockSpec((tr, width), lambda i: (i, cb)),
                  pl.BlockSpec((1, width), lambda i: (0, 0))],
        out_specs=[pl.BlockSpec((tr, width), lambda i: (i, 0)), pl.BlockSpec((1, width), lambda i: (0, 0))],
        out_shape=[_sds((t, width), BF16), _sds((1, width), F32)],
        compiler_params=_cp("arbitrary"))(dy, proj, g)


def _rope_apply(r, cos, sin_a, sin_b):
    return r * cos + pltpu.roll(r, 96, 1) * sin_a + pltpu.roll(r, 32, 1) * sin_b


def _rope_apply_t(dr, cos, sin_a, sin_b):
    return dr * cos + pltpu.roll(dr * sin_a, 32, 1) + pltpu.roll(dr * sin_b, 96, 1)


def _rope_q_fwd(q, cos, sin_a, sin_b, *, name):
    t, w = q.shape
    tr = _tile(t, 128)

    def body(q_ref, c_ref, sa_ref, sb_ref, o_ref):
        c, sa, sb = c_ref[...], sa_ref[...], sb_ref[...]
        flag = lax.broadcasted_iota(jnp.int32, (tr, LANES), 1) == QK_ROPE
        for h in range(MLA_HEADS):
            base = h * QHEAD
            o_ref[:, base:base + LANES] = (q_ref[:, base:base + LANES] * Q_SCALE).astype(BF16)
            rot = _rope_apply(q_ref[:, base + LANES:base + QHEAD], c, sa, sb)
            o_ref[:, base + LANES:base + QHEAD] = jnp.where(flag, 1.0, rot * Q_SCALE).astype(BF16)

    tab = pl.BlockSpec((tr, LANES), lambda i: (i, 0))
    row = pl.BlockSpec((tr, w), lambda i: (i, 0))
    return pl.pallas_call(body, name=name, grid=(t // tr,), in_specs=[row, tab, tab, tab], out_specs=row,
                          out_shape=_sds((t, w), BF16), compiler_params=_cp("parallel"))(q, cos, sin_a, sin_b)


def _rope_q_bwd(dq, cos, sin_a, sin_b, *, name):
    t, w = dq.shape
    tr = _tile(t, 128)

    def body(dq_ref, c_ref, sa_ref, sb_ref, o_ref):
        c, sa, sb = c_ref[...], sa_ref[...], sb_ref[...]
        for h in range(MLA_HEADS):
            base = h * QHEAD
            o_ref[:, base:base + LANES] = (dq_ref[:, base:base + LANES] * ATTN_SCALE).astype(BF16)
            d_rot = _rope_apply_t(dq_ref[:, base + LANES:base + QHEAD], c, sa, sb)
            o_ref[:, base + LANES:base + QHEAD] = (d_rot * ATTN_SCALE).astype(BF16)

    tab = pl.BlockSpec((tr, LANES), lambda i: (i, 0))
    row = pl.BlockSpec((tr, w), lambda i: (i, 0))
    return pl.pallas_call(body, name=name, grid=(t // tr,), in_specs=[row, tab, tab, tab], out_specs=row,
                          out_shape=_sds((t, w), BF16), compiler_params=_cp("parallel"))(dq, cos, sin_a, sin_b)


def _rope_k_fwd(proj, cos, sin_a, sin_b, *, name):
    t = proj.shape[0]
    tr = _tile(t, 640)
    cb = OFF_KPE // LANES

    def body(x_ref, c_ref, sa_ref, sb_ref, o_ref):
        rot = _rope_apply(x_ref[...], c_ref[...], sa_ref[...], sb_ref[...])
        rows = _row_ids(pl.program_id(0), tr, (tr, LANES))
        lane = lax.broadcasted_iota(jnp.int32, (tr, LANES), 1)
        o_ref[...] = jnp.where((lane == QK_ROPE) & (rows < PAD), PAD_KEY_SCORE, rot).astype(BF16)

    tab = pl.BlockSpec((tr, LANES), lambda i: (i, 0))
    return pl.pallas_call(body, name=name, grid=(t // tr,),
                          in_specs=[pl.BlockSpec((tr, LANES), lambda i: (i, cb)), tab, tab, tab], out_specs=tab,
                          out_shape=_sds((t, LANES), BF16), compiler_params=_cp("parallel"))(proj, cos, sin_a, sin_b)


def _rope_k_bwd(dkp, cos, sin_a, sin_b, *, name):
    nh, t, _ = dkp.shape
    tr = _tile(t, 640)

    def body(d_ref, c_ref, sa_ref, sb_ref, o_ref):
        tot = d_ref[0]
        for h in range(1, nh):
            tot = tot + d_ref[h]
        o_ref[...] = _rope_apply_t(tot, c_ref[...], sa_ref[...], sb_ref[...]).astype(BF16)

    tab = pl.BlockSpec((tr, LANES), lambda i: (i, 0))
    return pl.pallas_call(body, name=name, grid=(t // tr,),
                          in_specs=[pl.BlockSpec((nh, tr, LANES), lambda i: (0, i, 0)), tab, tab, tab], out_specs=tab,
                          out_shape=_sds((t, LANES), BF16), compiler_params=_cp("parallel"))(dkp, cos, sin_a, sin_b)


def _causal(tb, keys_first=False):
    a = lax.broadcasted_iota(jnp.int32, (tb, tb), 0)
    b = lax.broadcasted_iota(jnp.int32, (tb, tb), 1)
    return a <= b if keys_first else b <= a


def _flash_fwd(q, kv, kpe, *, name, gather=()):
    t = q.shape[0]
    nh = MLA_HEADS
    tb = _tile(t, 640)
    nb = t // tb
    na = len(gather)

    def attend(q_ref, kn_ref, v_ref, kp_ref, o_ref, lse_ref, extra):
        i = pl.program_id(1)
        qv = q_ref[...]

        def scores(j):
            r0 = pl.multiple_of(j * tb, tb)
            k = jnp.concatenate([kn_ref[pl.ds(r0, tb), :], kp_ref[pl.ds(r0, tb), :]], axis=1)
            return lax.dot_general(qv, k, NT_DIMS, preferred_element_type=F32)

        def update(s, j, state):
            m_prev, l_prev, acc = state
            m_new = jnp.maximum(m_prev, jnp.max(s, axis=1, keepdims=True))
            p = jnp.exp2(s - m_new)
            corr = jnp.exp2(m_prev - m_new)
            r0 = pl.multiple_of(j * tb, tb)
            pv = jnp.dot(p.astype(BF16), v_ref[pl.ds(r0, tb), :], preferred_element_type=F32)
            return m_new, corr * l_prev + jnp.sum(p, axis=1, keepdims=True), corr * acc + pv

        def loop(j, carry):
            s_cur, st = carry
            s_next = scores(j + 1)
            return s_next, update(s_cur, j, st)

        state = (jnp.full((tb, 1), NEG_INF, F32), jnp.zeros((tb, 1), F32), jnp.zeros((tb, V_HEAD), F32))
        s_diag, state = lax.fori_loop(0, i, loop, (scores(0), state))
        m, l, acc = update(jnp.where(_causal(tb), s_diag, NEG_INF), i, state)
        o_ref[...] = (acc / l).astype(BF16)
        lse_ref[0] = m + jnp.log2(l)

        if na:
            step = pl.program_id(0) * nb + i
            for phase, at in enumerate((0, (nh * nb) // 2, nh * nb - 1)):
                @pl.when(step == at)
                def _(phase=phase):
                    _chip_allgather_phase(phase, extra[:na], extra[na:2 * na], *extra[2 * na:])

    def body(q_ref, kn_ref, v_ref, kp_ref, *rest):
        attend(q_ref, kn_ref, v_ref, kp_ref, *rest[na:na + 2], extra=rest[:na] + rest[na + 2:])

    gather = list(gather)
    return pl.pallas_call(
        body, name=name, grid=(nh, nb),
        in_specs=[pl.BlockSpec((tb, QHEAD), lambda h, i: (i, h)),
                  pl.BlockSpec((t, LANES), lambda h, i: (0, h)),
                  pl.BlockSpec((t, LANES), lambda h, i: (0, nh + h)),
                  pl.BlockSpec((t, LANES), lambda h, i: (0, 0))] + [_ANY] * na,
        out_specs=[pl.BlockSpec((tb, V_HEAD), lambda h, i: (i, h)),
                   pl.BlockSpec((1, tb, 1), lambda h, i: (h, i, 0))] + [_ANY] * na,
        out_shape=[_sds((t, nh * V_HEAD), BF16), _sds((nh, t, 1), F32)] + _chip_allgather_shapes(gather),
        scratch_shapes=_chip_allgather_sems(na) if na else [],
        compiler_params=_cp("arbitrary", "arbitrary"))(q, kv, kv, kpe, *gather)


def _attn_delta(do, o, *, name):
    t = o.shape[0]
    nh = MLA_HEADS
    tr = _tile(t, 640)

    def body(do_ref, o_ref, d_ref):
        d_ref[0] = jnp.sum(do_ref[...].astype(F32) * o_ref[...].astype(F32), axis=1, keepdims=True)

    blk = pl.BlockSpec((tr, V_HEAD), lambda h, i: (i, h))
    return pl.pallas_call(body, name=name, grid=(nh, t // tr), in_specs=[blk, blk],
                          out_specs=pl.BlockSpec((1, tr, 1), lambda h, i: (h, i, 0)),
                          out_shape=_sds((nh, t, 1), F32), compiler_params=_cp("parallel", "parallel"))(do, o)


def _flash_bwd(q, kv, kpe, do, lse, delta, *, name, exchange=None):
    t = q.shape[0]
    nh = MLA_HEADS
    tb = lse.shape[2]
    nb = t // tb
    fused = exchange is not None

    def body(*refs):
        q_ref, do_ref, lse_ref, dl_ref, kn_ref, v_ref, kp_ref = refs[:7]
        dq_ref, dkn_ref, dkp_ref, dv_ref = refs[7 + fused:11 + fused]
        j = pl.program_id(1)

        if fused:
            copies = functools.partial(_chip_exchange_copies, refs[7], refs[11 + fused], *refs[12 + fused:])
            first = (pl.program_id(0) == 0) & (j == 0)
            last = (pl.program_id(0) == nh - 1) & (j == nb - 1)

            @pl.when(first)
            def _():
                _chip_exchange_start(copies())

        @pl.when(j == 0)
        def _():
            dq_ref[...] = jnp.zeros((t, QHEAD), F32)

        k = jnp.concatenate([kn_ref[...], kp_ref[...]], axis=1)
        v = v_ref[...]

        def tile(i, carry, masked):
            dk, dv = carry
            r0 = pl.multiple_of(i * tb, tb)
            qv = q_ref[pl.ds(r0, tb), :]
            dov = do_ref[pl.ds(r0, tb), :]
            st = lax.dot_general(k, qv, NT_DIMS, preferred_element_type=F32)
            if masked:
                st = jnp.where(_causal(tb, keys_first=True), st, NEG_INF)
            pt = jnp.exp2(st - lse_ref[0, pl.ds(i, 1), :])
            dpt = lax.dot_general(v, dov, NT_DIMS, preferred_element_type=F32)
            dst = (pt * (dpt - dl_ref[0, pl.ds(i, 1), :])).astype(BF16)
            dv = dv + jnp.dot(pt.astype(BF16), dov, preferred_element_type=F32)
            dk = dk + jnp.dot(dst, qv, preferred_element_type=F32)
            dq_ref[pl.ds(r0, tb), :] += lax.dot_general(dst, k, TN_DIMS, preferred_element_type=F32)
            return dk, dv

        carry = tile(j, (jnp.zeros((tb, QHEAD), F32), jnp.zeros((tb, V_HEAD), F32)), True)
        dk, dv = lax.fori_loop(j + 1, nb, lambda i, c: tile(i, c, False), carry)
        dkn_ref[...] = (dk[:, :LANES] * LN2).astype(BF16)
        dkp_ref[0] = dk[:, LANES:] * LN2
        dv_ref[...] = dv.astype(BF16)

        if fused:
            @pl.when(last)
            def _():
                _chip_exchange_wait(copies())

    stat = pl.BlockSpec((1, nb, tb), lambda h, j: (h, 0, 0))
    in_specs = [pl.BlockSpec((t, QHEAD), lambda h, j: (0, h)),
                pl.BlockSpec((t, V_HEAD), lambda h, j: (0, h)),
                stat, stat,
                pl.BlockSpec((tb, LANES), lambda h, j: (j, h)),
                pl.BlockSpec((tb, LANES), lambda h, j: (j, nh + h)),
                pl.BlockSpec((tb, LANES), lambda h, j: (j, 0))]
    out_specs = [pl.BlockSpec((t, QHEAD), lambda h, j: (0, h)),
                 pl.BlockSpec((tb, LANES), lambda h, j: (j, h)),
                 pl.BlockSpec((1, tb, LANES), lambda h, j: (h, j, 0)),
                 pl.BlockSpec((tb, V_HEAD), lambda h, j: (j, h))]
    out_shape = [_sds((t, nh * QHEAD), F32), _sds((t, nh * LANES), BF16), _sds((nh, t, LANES), F32),
                 _sds((t, nh * V_HEAD), BF16)]
    args = [q, do, lse, delta, kv, kv, kpe]
    scratch = []
    if fused:
        in_specs.append(_ANY)
        out_specs.append(_ANY)
        out_shape.append(_sds(exchange.shape, exchange.dtype))
        args.append(exchange)
        scratch = _CHIP_EXCHANGE_SEMS
    return pl.pallas_call(body, name=name, grid=(nh, nb), in_specs=in_specs, out_specs=out_specs, out_shape=out_shape,
                          scratch_shapes=scratch, compiler_params=_cp("arbitrary", "arbitrary"))(*args)


def _fill_prev(buf, x_ref, halo_ref, i, tr):
    buf[pl.ds(0, HALO), :] = jnp.where(i > 0, halo_ref[...], 0.0)
    buf[pl.ds(HALO, tr), :] = x_ref[...]


def _conv_prev(buf, w_ref, kw, tr):
    acc = w_ref[kw - 1:kw, :] * buf[pl.ds(HALO, tr), :]
    for k in range(kw - 1):
        acc = acc + w_ref[k:k + 1, :] * buf[pl.ds(HALO - kw + 1 + k, tr), :]
    return acc


def _conv_dw(buf, dc, kw, tr):
    rows = [jnp.sum(dc * buf[pl.ds(HALO - kw + 1 + k, tr), :], axis=0, keepdims=True) for k in range(kw)]
    return jnp.concatenate(rows, axis=0)


def _conv_next(buf, dc_ref, halo_ref, w_ref, kw, i, n_tiles, tr):
    buf[pl.ds(0, tr), :] = dc_ref[...]
    buf[pl.ds(tr, HALO), :] = jnp.where(i < n_tiles - 1, halo_ref[...], 0.0)
    acc = w_ref[kw - 1:kw, :] * buf[pl.ds(0, tr), :]
    for k in range(kw - 1):
        acc = acc + w_ref[k:k + 1, :] * buf[pl.ds(kw - 1 - k, tr), :]
    return acc


def _split3(x):
    x1 = x.astype(BF16)
    r1 = x - x1.astype(F32)
    x2 = r1.astype(BF16)
    x3 = (r1 - x2.astype(F32)).astype(BF16)
    return x1, x2, x3


def _dot3(parts, m, left):
    tot = None
    for p in parts:
        r = jnp.dot(m, p, preferred_element_type=F32) if left else jnp.dot(p, m, preferred_element_type=F32)
        tot = r if tot is None else tot + r
    return tot


def _ssd_prep_fwd(proj, conv_w, conv_b, dt_bias, expand, *, name):
    t = proj.shape[0]
    tr = _tile(t, 128)
    nt = t // tr
    hb = tr // HALO
    cw = SSD_CONV_DIM
    cb_x = OFF_XBC // cw
    cb_dt = OFF_DT // LANES

    def body(x_ref, halo_ref, dtr_ref, w_ref, b_ref, dtb_ref, e_ref, xs_ref, bm_ref, cm_ref, dtx_ref, buf):
        i = pl.program_id(0)
        _fill_prev(buf, x_ref, halo_ref, i, tr)
        conv = _conv_prev(buf, w_ref, SSD_CONV, tr) + b_ref[...]
        rows = _row_ids(i, tr, (tr, 1))
        live = rows >= PAD
        act = jnp.where(live, conv * _sigmoid(conv), 0.0)
        xs_ref[...] = act[:, :SSD_INNER]
        bm_ref[...] = act[:, SSD_INNER:SSD_INNER + GROUP_W]
        cm_ref[...] = act[:, SSD_INNER + GROUP_W:]
        dt = jnp.where(live, jax.nn.softplus(dtr_ref[...] + dtb_ref[...]), 0.0)
        dtx_ref[...] = _dot3(_split3(dt), e_ref[...], left=False)

    return pl.pallas_call(
        body, name=name, grid=(nt,),
        in_specs=[pl.BlockSpec((tr, cw), lambda i: (i, cb_x)),
                  pl.BlockSpec((HALO, cw), lambda i: (jnp.maximum(i * hb - 1, 0), cb_x)),
                  pl.BlockSpec((tr, LANES), lambda i: (i, cb_dt)),
                  pl.BlockSpec((SSD_CONV, cw), lambda i: (0, 0)),
                  pl.BlockSpec((1, cw), lambda i: (0, 0)),
                  pl.BlockSpec((1, LANES), lambda i: (0, 0)),
                  pl.BlockSpec((LANES, SSD_INNER), lambda i: (0, 0))],
        out_specs=[pl.BlockSpec((tr, SSD_INNER), lambda i: (i, 0)), pl.BlockSpec((tr, GROUP_W), lambda i: (i, 0)),
                   pl.BlockSpec((tr, GROUP_W), lambda i: (i, 0)), pl.BlockSpec((tr, SSD_INNER), lambda i: (i, 0))],
        out_shape=[_sds((t, SSD_INNER), F32), _sds((t, GROUP_W), F32), _sds((t, GROUP_W), F32),
                   _sds((t, SSD_INNER), F32)],
        scratch_shapes=[pltpu.VMEM((tr + HALO, cw), F32)],
        compiler_params=_cp("parallel"))(proj, proj, proj, conv_w, conv_b, dt_bias, expand)


def _ssd_prep_bwd_a(proj, dxs, dbm, dcm, ddtx, conv_w, conv_b, dt_bias, reduce_m, *, name):
    t = proj.shape[0]
    tr = _tile(t, 128)
    nt = t // tr
    hb = tr // HALO
    cw = SSD_CONV_DIM
    cb_x = OFF_XBC // cw
    cb_dt = OFF_DT // LANES

    def body(x_ref, halo_ref, dtr_ref, dxs_ref, dbm_ref, dcm_ref, ddtx_ref, w_ref, b_ref, dtb_ref, r_ref,
             dconv_ref, ddtr_ref, dw_ref, db_ref, ddtb_ref, buf):
        i = pl.program_id(0)
        _fill_prev(buf, x_ref, halo_ref, i, tr)
        conv = _conv_prev(buf, w_ref, SSD_CONV, tr) + b_ref[...]
        rows = _row_ids(i, tr, (tr, 1))
        live = rows >= PAD
        sg = _sigmoid(conv)
        dact = jnp.concatenate([dxs_ref[...], dbm_ref[...], dcm_ref[...]], axis=1)
        dconv = jnp.where(live, dact * (sg * (1.0 + conv * (1.0 - sg))), 0.0)
        dconv_ref[...] = dconv
        pw = _conv_dw(buf, dconv, SSD_CONV, tr)
        pb = jnp.sum(dconv, axis=0, keepdims=True)
        ddt = _dot3(_split3(ddtx_ref[...]), r_ref[...], left=False)
        ddtr = jnp.where(live, ddt * _sigmoid(dtr_ref[...] + dtb_ref[...]), 0.0)
        ddtr_ref[...] = ddtr.astype(BF16)
        pdb = jnp.sum(ddtr, axis=0, keepdims=True)

        @pl.when(i == 0)
        def _():
            dw_ref[...] = pw
            db_ref[...] = pb
            ddtb_ref[...] = pdb

        @pl.when(i > 0)
        def _():
            dw_ref[...] += pw
            db_ref[...] += pb
            ddtb_ref[...] += pdb

    return pl.pallas_call(
        body, name=name, grid=(nt,),
        in_specs=[pl.BlockSpec((tr, cw), lambda i: (i, cb_x)),
                  pl.BlockSpec((HALO, cw), lambda i: (jnp.maximum(i * hb - 1, 0), cb_x)),
                  pl.BlockSpec((tr, LANES), lambda i: (i, cb_dt)),
                  pl.BlockSpec((tr, SSD_INNER), lambda i: (i, 0)),
                  pl.BlockSpec((tr, GROUP_W), lambda i: (i, 0)),
                  pl.BlockSpec((tr, GROUP_W), lambda i: (i, 0)),
                  pl.BlockSpec((tr, SSD_INNER), lambda i: (i, 0)),
                  pl.BlockSpec((SSD_CONV, cw), lambda i: (0, 0)),
                  pl.BlockSpec((1, cw), lambda i: (0, 0)),
                  pl.BlockSpec((1, LANES), lambda i: (0, 0)),
                  pl.BlockSpec((SSD_INNER, LANES), lambda i: (0, 0))],
        out_specs=[pl.BlockSpec((tr, cw), lambda i: (i, 0)), pl.BlockSpec((tr, LANES), lambda i: (i, 0)),
                   pl.BlockSpec((SSD_CONV, cw), lambda i: (0, 0)), pl.BlockSpec((1, cw), lambda i: (0, 0)),
                   pl.BlockSpec((1, LANES), lambda i: (0, 0))],
        out_shape=[_sds((t, cw), F32), _sds((t, LANES), BF16), _sds((SSD_CONV, cw), F32), _sds((1, cw), F32),
                   _sds((1, LANES), F32)],
        scratch_shapes=[pltpu.VMEM((tr + HALO, cw), F32)],
        compiler_params=_cp("arbitrary"))(proj, proj, proj, dxs, dbm, dcm, ddtx, conv_w, conv_b, dt_bias, reduce_m)


def _conv_bwd_input(dconv, w, kw, *, name, out_dtype=BF16, tc=None):
    t, c = dconv.shape
    tr = _tile(t, 128)
    nt = t // tr
    hb = tr // HALO
    tc = _tile(c, tc or c)
    last_hb = t // HALO - 1

    def body(dc_ref, halo_ref, w_ref, o_ref, buf):
        i = pl.program_id(0)
        o_ref[...] = _conv_next(buf, dc_ref, halo_ref, w_ref, kw, i, nt, tr).astype(out_dtype)

    return pl.pallas_call(
        body, name=name, grid=(nt, c // tc),
        in_specs=[pl.BlockSpec((tr, tc), lambda i, j: (i, j)),
                  pl.BlockSpec((HALO, tc), lambda i, j: (jnp.minimum((i + 1) * hb, last_hb), j)),
                  pl.BlockSpec((kw, tc), lambda i, j: (0, j))],
        out_specs=pl.BlockSpec((tr, tc), lambda i, j: (i, j)), out_shape=_sds((t, c), out_dtype),
        scratch_shapes=[pltpu.VMEM((tr + HALO, tc), F32)],
        compiler_params=_cp("parallel", "parallel"))(dconv, dconv, w)


def _ffn_act_fwd(ug, uv, wg, wv, bg, bv, *, name):
    t, c = ug.shape
    tr = _tile(t, 128)
    hb = tr // HALO
    tc = _tile(c, 1408)

    def body(ug_ref, hg_ref, uv_ref, hv_ref, wg_ref, wv_ref, bg_ref, bv_ref, o_ref, bufg, bufv):
        i = pl.program_id(0)
        _fill_prev(bufg, ug_ref, hg_ref, i, tr)
        _fill_prev(bufv, uv_ref, hv_ref, i, tr)
        cg = _conv_prev(bufg, wg_ref, FFN_CONV, tr) + bg_ref[...]
        cv = _conv_prev(bufv, wv_ref, FFN_CONV, tr) + bv_ref[...]
        o_ref[...] = (cg * _sigmoid(cg) * cv).astype(BF16)

    blk = pl.BlockSpec((tr, tc), lambda i, j: (i, j))
    halo = pl.BlockSpec((HALO, tc), lambda i, j: (jnp.maximum(i * hb - 1, 0), j))
    wsp = pl.BlockSpec((FFN_CONV, tc), lambda i, j: (0, j))
    bsp = pl.BlockSpec((1, tc), lambda i, j: (0, j))
    return pl.pallas_call(
        body, name=name, grid=(t // tr, c // tc), in_specs=[blk, halo, blk, halo, wsp, wsp, bsp, bsp],
        out_specs=blk, out_shape=_sds((t, c), BF16),
        scratch_shapes=[pltpu.VMEM((tr + HALO, tc), F32), pltpu.VMEM((tr + HALO, tc), F32)],
        compiler_params=_cp("parallel", "parallel"))(ug, ug, uv, uv, wg, wv, bg, bv)


def _ffn_act_bwd(ug, uv, dact, wg, wv, bg, bv, *, name):
    t, c = ug.shape
    tr = _tile(t, 128)
    hb = tr // HALO
    tc = _tile(c, 1408)

    def body(ug_ref, hg_ref, uv_ref, hv_ref, da_ref, wg_ref, wv_ref, bg_ref, bv_ref,
             dcg_ref, dcv_ref, dwg_ref, dwv_ref, dbg_ref, dbv_ref, bufg, bufv):
        i = pl.program_id(1)
        _fill_prev(bufg, ug_ref, hg_ref, i, tr)
        _fill_prev(bufv, uv_ref, hv_ref, i, tr)
        cg = _conv_prev(bufg, wg_ref, FFN_CONV, tr) + bg_ref[...]
        cv = _conv_prev(bufv, wv_ref, FFN_CONV, tr) + bv_ref[...]
        sg = _sigmoid(cg)
        da = da_ref[...]
        dcg = da * cv * (sg * (1.0 + cg * (1.0 - sg)))
        dcv = da * (cg * sg)
        dcg_ref[...] = dcg
        dcv_ref[...] = dcv
        pwg = _conv_dw(bufg, dcg, FFN_CONV, tr)
        pwv = _conv_dw(bufv, dcv, FFN_CONV, tr)
        pbg = jnp.sum(dcg, axis=0, keepdims=True)
        pbv = jnp.sum(dcv, axis=0, keepdims=True)

        @pl.when(i == 0)
        def _():
            dwg_ref[...] = pwg
            dwv_ref[...] = pwv
            dbg_ref[...] = pbg
            dbv_ref[...] = pbv

        @pl.when(i > 0)
        def _():
            dwg_ref[...] += pwg
            dwv_ref[...] += pwv
            dbg_ref[...] += pbg
            dbv_ref[...] += pbv

    blk = pl.BlockSpec((tr, tc), lambda j, i: (i, j))
    halo = pl.BlockSpec((HALO, tc), lambda j, i: (jnp.maximum(i * hb - 1, 0), j))
    wsp = pl.BlockSpec((FFN_CONV, tc), lambda j, i: (0, j))
    bsp = pl.BlockSpec((1, tc), lambda j, i: (0, j))
    return pl.pallas_call(
        body, name=name, grid=(c // tc, t // tr), in_specs=[blk, halo, blk, halo, blk, wsp, wsp, bsp, bsp],
        out_specs=[blk, blk, wsp, wsp, bsp, bsp],
        out_shape=[_sds((t, c), F32), _sds((t, c), F32), _sds((FFN_CONV, c), F32), _sds((FFN_CONV, c), F32),
                   _sds((1, c), F32), _sds((1, c), F32)],
        scratch_shapes=[pltpu.VMEM((tr + HALO, tc), F32), pltpu.VMEM((tr + HALO, tc), F32)],
        compiler_params=_cp("parallel", "arbitrary"))(ug, ug, uv, uv, dact, wg, wv, bg, bv)


def _tri(lower):
    li = lax.broadcasted_iota(jnp.int32, (CHUNK, CHUNK), 0)
    si = lax.broadcasted_iota(jnp.int32, (CHUNK, CHUNK), 1)
    return li >= si if lower else li <= si


def _tri_ones(lower):
    return jnp.where(_tri(lower), 1.0, 0.0).astype(BF16)


def _decay_pair(acs, acs_t, lane0):
    col = acs[:, lane0:lane0 + 1]
    row = acs_t[lane0:lane0 + 1, :]
    low = jnp.where(_tri(True), jnp.exp(jnp.minimum(col - row, 0.0)), 0.0)
    upp = jnp.where(_tri(False), jnp.exp(jnp.minimum(row - col, 0.0)), 0.0)
    return low, upp


def _ssd_fwd(xs, dtx, bm, cm, bm_t, a_x, d_x, *, name):
    t = xs.shape[0]
    nc = t // CHUNK
    gw = GROUP_W

    def body(xs_ref, dt_ref, b_ref, c_ref, bt_ref, a_ref, d_ref, y_ref, prev_ref, h_s):
        @pl.when(pl.program_id(1) == 0)
        def _():
            h_s[...] = jnp.zeros((SSD_STATE, gw), F32)

        x = xs_ref[...]
        dt = dt_ref[...]
        acs = _dot3(_split3(dt * a_ref[...]), _tri_ones(True), left=True)
        acs_t = acs.T
        xc = x * dt
        bv = b_ref[...].astype(BF16)
        cv = c_ref[...].astype(BF16)
        cb = lax.dot_general(cv, bv, NT_DIMS, preferred_element_type=F32)
        lane = lax.broadcasted_iota(jnp.int32, (CHUNK, LANES), 1)
        pieces = []
        for pp in range(gw // LANES):
            xcp = xc[:, pp * LANES:(pp + 1) * LANES]
            acc = jnp.zeros((CHUNK, LANES), F32)
            for e in range(2):
                low, _ = _decay_pair(acs, acs_t, pp * LANES + e * SSD_HEAD_DIM)
                mine = (lane >= e * SSD_HEAD_DIM) & (lane < (e + 1) * SSD_HEAD_DIM)
                xm = jnp.where(mine, xcp, 0.0).astype(BF16)
                acc = acc + jnp.dot((cb * low).astype(BF16), xm, preferred_element_type=F32)
            pieces.append(acc)
        y_diag = jnp.concatenate(pieces, axis=1)
        h_prev = h_s[...]
        y_off = jnp.dot(cv, h_prev.astype(BF16), preferred_element_type=F32) * jnp.exp(acs)
        y_ref[...] = y_diag + y_off + d_ref[...] * x
        prev_ref[0] = h_prev
        last = acs[CHUNK - 1:CHUNK, :]
        w = jnp.exp(last - acs)
        st = jnp.dot(bt_ref[...].astype(BF16), (xc * w).astype(BF16), preferred_element_type=F32)
        h_s[...] = h_prev * jnp.exp(last) + st

    tok = pl.BlockSpec((CHUNK, gw), lambda g, c: (c, g))
    grp = pl.BlockSpec((CHUNK, SSD_STATE), lambda g, c: (c, g))
    vec = pl.BlockSpec((1, gw), lambda g, c: (0, g))
    return pl.pallas_call(
        body, name=name, grid=(SSD_GROUPS, nc),
        in_specs=[tok, tok, grp, grp, pl.BlockSpec((SSD_STATE, CHUNK), lambda g, c: (g, c)), vec, vec],
        out_specs=[tok, pl.BlockSpec((1, SSD_STATE, gw), lambda g, c: (c, 0, g))],
        out_shape=[_sds((t, SSD_INNER), F32), _sds((nc, SSD_STATE, SSD_INNER), F32)],
        scratch_shapes=[pltpu.VMEM((SSD_STATE, gw), F32)],
        compiler_params=_cp("parallel", "arbitrary"))(xs, dtx, bm, cm, bm_t, a_x, d_x)


def _ssd_bwd(xs, dtx, bm, cm, cm_t, prev, dy, a_x, d_x, *, name):
    t = xs.shape[0]
    nc = t // CHUNK
    gw = GROUP_W

    def body(xs_ref, dt_ref, b_ref, c_ref, ct_ref, prev_ref, dy_ref, a_ref, d_ref,
             dxs_ref, ddt_ref, db_ref, dc_ref, da_ref, dd_ref, g_s):
        first = pl.program_id(1) == 0

        @pl.when(first)
        def _():
            g_s[...] = jnp.zeros((SSD_STATE, gw), F32)

        x = xs_ref[...]
        dt = dt_ref[...]
        a = a_ref[...]
        dyv = dy_ref[...]
        acs = _dot3(_split3(dt * a), _tri_ones(True), left=True)
        acs_t = acs.T
        xc = x * dt
        bv = b_ref[...].astype(BF16)
        cv = c_ref[...].astype(BF16)
        cb = lax.dot_general(cv, bv, NT_DIMS, preferred_element_type=F32)
        cb_t = lax.dot_general(bv, cv, NT_DIMS, preferred_element_type=F32)
        last = acs[CHUNK - 1:CHUNK, :]
        w = jnp.exp(last - acs)
        cd = jnp.exp(last)
        p_in = prev_ref[0]
        p_b = p_in.astype(BF16)
        g_out = g_s[...]
        g_b = g_out.astype(BF16)
        dy_e = dyv * jnp.exp(acs)
        dy_eb = dy_e.astype(BF16)
        y_off_raw = jnp.dot(cv, p_b, preferred_element_type=F32)
        dacs = dy_e * y_off_raw
        d_c = lax.dot_general(dy_eb, p_b, NT_DIMS, preferred_element_type=F32)
        d_prev = jnp.dot(ct_ref[...].astype(BF16), dy_eb, preferred_element_type=F32)
        q_l = jnp.dot(bv, g_b, preferred_element_type=F32)
        dxc = w * q_l
        tw = xc * q_l * w
        dacs = dacs - tw
        d_b = lax.dot_general((xc * w).astype(BF16), g_b, NT_DIMS, preferred_element_type=F32)
        last_add = jnp.sum(tw, axis=0, keepdims=True) + cd * jnp.sum(g_out * p_in, axis=0, keepdims=True)
        g_s[...] = cd * g_out + d_prev
        lane = lax.broadcasted_iota(jnp.int32, (CHUNK, LANES), 1)
        d_cb = jnp.zeros((CHUNK, CHUNK), F32)
        d_cb_t = jnp.zeros((CHUNK, CHUNK), F32)
        dxc_pieces, dacs_pieces = [], []
        for pp in range(gw // LANES):
            xcp = xc[:, pp * LANES:(pp + 1) * LANES]
            dyp = dyv[:, pp * LANES:(pp + 1) * LANES]
            dxcp = jnp.zeros((CHUNK, LANES), F32)
            dacsp = jnp.zeros((CHUNK, LANES), F32)
            for e in range(2):
                low, upp = _decay_pair(acs, acs_t, pp * LANES + e * SSD_HEAD_DIM)
                mine = (lane >= e * SSD_HEAD_DIM) & (lane < (e + 1) * SSD_HEAD_DIM)
                m_low = cb * low
                m_upp = cb_t * upp
                dym = jnp.where(mine, dyp, 0.0).astype(BF16)
                xm = jnp.where(mine, xcp, 0.0).astype(BF16)
                dxcp = dxcp + jnp.dot(m_upp.astype(BF16), dym, preferred_element_type=F32)
                d_m = lax.dot_general(dym, xm, NT_DIMS, preferred_element_type=F32)
                d_m_t = lax.dot_general(xm, dym, NT_DIMS, preferred_element_type=F32)
                rs = jnp.sum(d_m * m_low, axis=1, keepdims=True)
                cs = jnp.sum(d_m_t * m_upp, axis=1, keepdims=True)
                dacsp = dacsp + jnp.where(lane == e * SSD_HEAD_DIM, rs - cs, 0.0)
                d_cb = d_cb + d_m * low
                d_cb_t = d_cb_t + d_m_t * upp
            dxc_pieces.append(dxcp)
            dacs_pieces.append(dacsp)
        dxc = dxc + jnp.concatenate(dxc_pieces, axis=1)
        dacs = dacs + jnp.concatenate(dacs_pieces, axis=1)
        rowi = lax.broadcasted_iota(jnp.int32, (CHUNK, gw), 0)
        dacs = dacs + jnp.where(rowi == CHUNK - 1, last_add, 0.0)
        dc_ref[...] = d_c + jnp.dot(d_cb.astype(BF16), bv, preferred_element_type=F32)
        db_ref[...] = d_b + jnp.dot(d_cb_t.astype(BF16), cv, preferred_element_type=F32)
        dda = _dot3(_split3(dacs), _tri_ones(False), left=True)
        ddt_ref[...] = dda * a + dxc * x
        dxs_ref[...] = dxc * dt + d_ref[...] * dyv
        pa = jnp.sum(dda * dt, axis=0, keepdims=True)
        pd = jnp.sum(dyv * x, axis=0, keepdims=True)

        @pl.when(first)
        def _():
            da_ref[...] = pa
            dd_ref[...] = pd

        @pl.when(jnp.logical_not(first))
        def _():
            da_ref[...] += pa
            dd_ref[...] += pd

    rc = lambda c: nc - 1 - c
    tok = pl.BlockSpec((CHUNK, gw), lambda g, c: (rc(c), g))
    grp = pl.BlockSpec((CHUNK, SSD_STATE), lambda g, c: (rc(c), g))
    vec = pl.BlockSpec((1, gw), lambda g, c: (0, g))
    return pl.pallas_call(
        body, name=name, grid=(SSD_GROUPS, nc),
        in_specs=[tok, tok, grp, grp, pl.BlockSpec((SSD_STATE, CHUNK), lambda g, c: (g, rc(c))),
                  pl.BlockSpec((1, SSD_STATE, gw), lambda g, c: (rc(c), 0, g)), tok, vec, vec],
        out_specs=[tok, tok, grp, grp, vec, vec],
        out_shape=[_sds((t, SSD_INNER), F32), _sds((t, SSD_INNER), F32), _sds((t, gw), F32), _sds((t, gw), F32),
                   _sds((1, SSD_INNER), F32), _sds((1, SSD_INNER), F32)],
        scratch_shapes=[pltpu.VMEM((SSD_STATE, gw), F32)],
        compiler_params=_cp("parallel", "arbitrary"))(xs, dtx, bm, cm, cm_t, prev, dy, a_x, d_x)


def _gnorm_fwd(y, proj, g, *, name):
    t = y.shape[0]
    tr = _tile(t, 640)
    zb = OFF_Z // GROUP_W

    def body(y_ref, z_ref, g_ref, o_ref):
        z = z_ref[...]
        v = y_ref[...] * (z * _sigmoid(z))
        r = lax.rsqrt(jnp.mean(v * v, axis=1, keepdims=True) + RMS_EPS)
        o_ref[...] = (v * r * g_ref[...]).astype(BF16)

    blk = pl.BlockSpec((tr, GROUP_W), lambda i, j: (i, j))
    return pl.pallas_call(
        body, name=name, grid=(t // tr, SSD_GROUPS),
        in_specs=[blk, pl.BlockSpec((tr, GROUP_W), lambda i, j: (i, zb + j)),
                  pl.BlockSpec((1, GROUP_W), lambda i, j: (0, j))],
        out_specs=blk, out_shape=_sds((t, SSD_INNER), BF16),
        compiler_params=_cp("parallel", "parallel"))(y, proj, g)


def _gnorm_bwd(dout, y, proj, g, *, name):
    t = y.shape[0]
    tr = _tile(t, 640)
    zb = OFF_Z // GROUP_W

    def body(do_ref, y_ref, z_ref, g_ref, dy_ref, dz_ref, dg_ref):
        i = pl.program_id(1)
        z = z_ref[...]
        yv = y_ref[...]
        sg = _sigmoid(z)
        sz = z * sg
        v = yv * sz
        r = lax.rsqrt(jnp.mean(v * v, axis=1, keepdims=True) + RMS_EPS)
        dov = do_ref[...].astype(F32)
        gdo = dov * g_ref[...]
        m = jnp.mean(v * gdo, axis=1, keepdims=True)
        dv = r * gdo - v * (r * r * r) * m
        dy_ref[...] = dv * sz
        dz_ref[...] = (dv * yv * (sg * (1.0 + z * (1.0 - sg)))).astype(BF16)
        pg = jnp.sum(dov * v * r, axis=0, keepdims=True)

        @pl.when(i == 0)
        def _():
            dg_ref[...] = pg

        @pl.when(i > 0)
        def _():
            dg_ref[...] += pg

    blk = pl.BlockSpec((tr, GROUP_W), lambda j, i: (i, j))
    vec = pl.BlockSpec((1, GROUP_W), lambda j, i: (0, j))
    return pl.pallas_call(
        body, name=name, grid=(SSD_GROUPS, t // tr),
        in_specs=[blk, blk, pl.BlockSpec((tr, GROUP_W), lambda j, i: (i, zb + j)), vec],
        out_specs=[blk, blk, vec],
        out_shape=[_sds((t, SSD_INNER), F32), _sds((t, SSD_INNER), BF16), _sds((1, SSD_INNER), F32)],
        compiler_params=_cp("parallel", "arbitrary"))(dout, y, proj, g)


def _mix_fwd(proj, ya, ys, *, name):
    t, d = ya.shape
    tr = _tile(t, 640)
    ba, bs = OFF_GA // d, OFF_GS // d

    def body(ga_ref, gs_ref, ya_ref, ys_ref, o_ref):
        o_ref[...] = (_sigmoid(ga_ref[...]) * ya_ref[...] + _sigmoid(gs_ref[...]) * ys_ref[...]).astype(BF16)

    blk = pl.BlockSpec((tr, d), lambda i: (i, 0))
    return pl.pallas_call(
        body, name=name, grid=(t // tr,),
        in_specs=[pl.BlockSpec((tr, d), lambda i: (i, ba)), pl.BlockSpec((tr, d), lambda i: (i, bs)), blk, blk],
        out_specs=blk, out_shape=_sds((t, d), BF16), compiler_params=_cp("parallel"))(proj, proj, ya, ys)


def _mix_bwd(dmix, proj, ya, ys, *, name):
    t, d = ya.shape
    tr = _tile(t, 640)
    ba, bs = OFF_GA // d, OFF_GS // d

    def body(dm_ref, ga_ref, gs_ref, ya_ref, ys_ref, dya_ref, dys_ref, dga_ref, dgs_ref):
        dm = dm_ref[...]
        sa = _sigmoid(ga_ref[...])
        ss = _sigmoid(gs_ref[...])
        dya_ref[...] = (sa * dm).astype(BF16)
        dys_ref[...] = (ss * dm).astype(BF16)
        dga_ref[...] = (dm * ya_ref[...] * sa * (1.0 - sa)).astype(BF16)
        dgs_ref[...] = (dm * ys_ref[...] * ss * (1.0 - ss)).astype(BF16)

    blk = pl.BlockSpec((tr, d), lambda i: (i, 0))
    return pl.pallas_call(
        body, name=name, grid=(t // tr,),
        in_specs=[blk, pl.BlockSpec((tr, d), lambda i: (i, ba)), pl.BlockSpec((tr, d), lambda i: (i, bs)), blk, blk],
        out_specs=[blk] * 4, out_shape=[_sds((t, d), BF16)] * 4,
        compiler_params=_cp("parallel"))(dmix, proj, proj, ya, ys)


def _loss_grad(h, target, *, name):
    t, d = h.shape
    tr = LANES
    assert ROW0 == tr

    def body(h_ref, t_ref, dh_ref, loss_ref):
        i = pl.program_id(0)

        @pl.when(i == 0)
        def _():
            dh_ref[...] = jnp.zeros((tr, d), F32)
            loss_ref[...] = jnp.zeros((1, LANES), F32)

        @pl.when(i > 0)
        def _():
            err = h_ref[...] - t_ref[...]
            dh_ref[...] = err * (1.0 / d)
            part = jnp.sum(jnp.sum(err * err, axis=1, keepdims=True), axis=0, keepdims=True)
            loss_ref[...] += jnp.broadcast_to(part * (0.5 / d), (1, LANES))

    blk = pl.BlockSpec((tr, d), lambda i: (i, 0))
    return pl.pallas_call(
        body, name=name, grid=(t // tr,),
        in_specs=[blk, pl.BlockSpec((tr, d), lambda i: (jnp.maximum(i - 1, 0), 0))],
        out_specs=[blk, pl.BlockSpec((1, LANES), lambda i: (0, 0))],
        out_shape=[_sds((t, d), F32), _sds((1, LANES), F32)],
        compiler_params=_cp("arbitrary"))(h, target)


def _adamw_update(gv, wv, mv, vv):
    c1 = 1.0 - ADAM_B1 ** ADAM_STEP
    c2 = 1.0 - ADAM_B2 ** ADAM_STEP
    nm = ADAM_B1 * mv + (1.0 - ADAM_B1) * gv
    nv = ADAM_B2 * vv + (1.0 - ADAM_B2) * (gv * gv)
    return -ADAM_LR * ((nm / c1) / (jnp.sqrt(nv / c2) + ADAM_EPS) + ADAM_WD * wv), nm, nv


def _as_2d(a):
    return a.reshape(1, -1) if a.ndim == 1 else a.reshape(-1, a.shape[-1])


def _adamw(g, w, m, v, *, name):
    shape = w.shape
    g2, w2, m2, v2 = (_as_2d(a) for a in (g, w, m, v))
    r, c = w2.shape
    tr = _tile(r, 256, base=8)

    def body(g_ref, w_ref, m_ref, v_ref, d_ref, nm_ref, nv_ref):
        d_ref[...], nm_ref[...], nv_ref[...] = _adamw_update(g_ref[...], w_ref[...], m_ref[...], v_ref[...])

    blk = pl.BlockSpec((tr, c), lambda i: (i, 0))
    outs = pl.pallas_call(body, name=name, grid=(r // tr,), in_specs=[blk] * 4, out_specs=[blk] * 3,
                          out_shape=[_sds((r, c), F32)] * 3, compiler_params=_cp("parallel"))(g2, w2, m2, v2)
    return [o.reshape(shape) for o in outs]


def _adamw_small(items, *, name):
    n = len(items)
    shapes = [it[1].shape for it in items]
    flat = [_as_2d(a) for it in items for a in it]

    def body(*refs):
        ins, outs = refs[:4 * n], refs[4 * n:]
        for k in range(n):
            g_ref, w_ref, m_ref, v_ref = ins[4 * k:4 * k + 4]
            d_ref, nm_ref, nv_ref = outs[3 * k:3 * k + 3]
            d_ref[...], nm_ref[...], nv_ref[...] = _adamw_update(g_ref[...], w_ref[...], m_ref[...], v_ref[...])

    out_shape = [_sds(flat[4 * k + 1].shape, F32) for k in range(n) for _ in range(3)]
    outs = pl.pallas_call(body, name=name, out_shape=out_shape,
                          compiler_params=pltpu.CompilerParams(vmem_limit_bytes=VMEM_LIMIT_BYTES))(*flat)
    return [[outs[3 * k + q].reshape(shapes[k]) for q in range(3)] for k in range(n)]


def _pair_add(g5, got, core, *, name):
    n, _, r, _ = g5.shape
    tr = _tile(r, 1024, base=8)

    def body(c_ref, a_ref, b_ref, o_ref):
        o_ref[...] = a_ref[0] + b_ref[...]

    grid_spec = pltpu.PrefetchScalarGridSpec(
        num_scalar_prefetch=1, grid=(n, r // tr),
        in_specs=[pl.BlockSpec((1, 1, tr, LANES), lambda s, i, c_ref: (s, c_ref[0], i, 0)),
                  pl.BlockSpec((1, tr, LANES), lambda s, i, c_ref: (s, i, 0))],
        out_specs=pl.BlockSpec((1, tr, LANES), lambda s, i, c_ref: (s, i, 0)))
    return pl.pallas_call(body, name=name, grid_spec=grid_spec, out_shape=_sds(got.shape, F32),
                          compiler_params=_cp("parallel", "parallel"))(core, g5, got)


def _sum_chips(q, core, *, name):
    n, r, _ = q.shape
    tr = _tile(r, 1024, base=8)

    def body(c_ref, q_ref, o_ref):
        tot = q_ref[0]
        for s in range(1, n):
            tot = tot + q_ref[s]
        o_ref[0] = tot

    grid_spec = pltpu.PrefetchScalarGridSpec(
        num_scalar_prefetch=1, grid=(r // tr,),
        in_specs=[pl.BlockSpec((n, tr, LANES), lambda i, c_ref: (0, i, 0))],
        out_specs=pl.BlockSpec((1, tr, LANES), lambda i, c_ref: (c_ref[0], i, 0)))
    return pl.pallas_call(body, name=name, grid_spec=grid_spec, out_shape=_sds((2, r, LANES), F32),
                          compiler_params=_cp("parallel"))(core, q)


_ANY = pl.BlockSpec(memory_space=pl.ANY)
_MESH = pl.DeviceIdType.MESH


def _place():
    x, y, c = lax.axis_index("x"), lax.axis_index("y"), lax.axis_index("c")
    return x, y, c, [(1 - x, y), (x, 1 - y), (1 - x, 1 - y)]


def _chip_allgather(mine, *, name):
    na = len(mine)

    def body(*refs):
        for phase in range(3):
            _chip_allgather_phase(phase, refs[:na], refs[na:2 * na], *refs[2 * na:])

    return pl.pallas_call(
        body, name=name, in_specs=[_ANY] * na, out_specs=[_ANY] * na,
        out_shape=_chip_allgather_shapes(mine), scratch_shapes=_chip_allgather_sems(na))(*mine)


def _chip_allgather_shapes(mine):
    return [_sds((N_CHIPS,) + a.shape, a.dtype) for a in mine]


def _chip_allgather_sems(na):
    return [pltpu.SemaphoreType.DMA((6 * na,)), pltpu.SemaphoreType.DMA((6 * na,)), pltpu.SemaphoreType.DMA((na,))]


def _chip_allgather_phase(phase, x_refs, o_refs, send_sems, recv_sems, local_sems):
    na = len(x_refs)
    x, y, c, chips = _place()
    k = 2 * x + y

    def copy(a, n, src, dst, to):
        return pltpu.make_async_remote_copy(src_ref=src, dst_ref=dst, send_sem=send_sems.at[6 * a + n],
                                            recv_sem=recv_sems.at[6 * a + n], device_id=to, device_id_type=_MESH)

    locals_ = [pltpu.make_async_copy(x_refs[a], o_refs[a].at[k], local_sems.at[a]) for a in range(na)]
    sends = [copy(a, n, x_refs[a].at[c], o_refs[a].at[k, c], (cx, cy, c))
             for a in range(na) for n, (cx, cy) in enumerate(chips)]
    landed = [(copy(a, n, o_refs[a].at[2 * cx + cy, c], o_refs[a].at[2 * cx + cy, c], (cx, cy, c)),
               copy(a, 3 + n, o_refs[a].at[2 * cx + cy, c], o_refs[a].at[2 * cx + cy, c], (x, y, 1 - c)))
              for a in range(na) for n, (cx, cy) in enumerate(chips)]
    if phase == 0:
        for cp in locals_ + sends:
            cp.start()
    elif phase == 1:
        for arrival, forward in landed:
            arrival.wait_recv()
            forward.start()
    else:
        for a in range(na):
            for n, (cx, cy) in enumerate(chips):
                slab = o_refs[a].at[2 * cx + cy, 1 - c]
                copy(a, 3 + n, slab, slab, (x, y, 1 - c)).wait_recv()
        for cp in sends + [forward for _, forward in landed]:
            cp.wait_send()
        for cp in locals_:
            cp.wait()


def _sibling_swap(g5, *, name):
    n, _, r, _ = g5.shape

    def body(x_ref, o_ref, send_sems, recv_sems):
        x, y, c, _ = _place()
        cps = [pltpu.make_async_remote_copy(src_ref=x_ref.at[s, 1 - c], dst_ref=o_ref.at[s], send_sem=send_sems.at[s],
                                            recv_sem=recv_sems.at[s], device_id=(x, y, 1 - c), device_id_type=_MESH)
               for s in range(n)]
        for cp in cps:
            cp.start()
        for cp in cps:
            cp.wait()

    return pl.pallas_call(
        body, name=name, in_specs=[_ANY], out_specs=_ANY, out_shape=_sds((n, r, LANES), g5.dtype),
        scratch_shapes=[pltpu.SemaphoreType.DMA((n,)), pltpu.SemaphoreType.DMA((n,))])(g5)


_CHIP_EXCHANGE_SEMS = [pltpu.SemaphoreType.DMA((3,)), pltpu.SemaphoreType.DMA((3,)), pltpu.SemaphoreType.DMA]


def _chip_exchange_copies(h_ref, q_ref, send_sems, recv_sems, local_sem):
    x, y, c, chips = _place()
    k = 2 * x + y
    local = pltpu.make_async_copy(h_ref.at[k], q_ref.at[k], local_sem)
    sends, arrivals = [], []
    for n, (cx, cy) in enumerate(chips):
        kk = 2 * cx + cy
        mk = functools.partial(pltpu.make_async_remote_copy, src_ref=h_ref.at[kk], send_sem=send_sems.at[n],
                               recv_sem=recv_sems.at[n], device_id=(cx, cy, c), device_id_type=_MESH)
        sends.append(mk(dst_ref=q_ref.at[k]))
        arrivals.append(mk(dst_ref=q_ref.at[kk]))
    return local, sends, arrivals


def _chip_exchange_start(copies):
    local, sends, _ = copies
    local.start()
    for cp in sends:
        cp.start()


def _chip_exchange_wait(copies):
    local, sends, arrivals = copies
    for cp in arrivals:
        cp.wait_recv()
    for cp in sends:
        cp.wait_send()
    local.wait()


def _chip_exchange(h, *, name):
    def body(h_ref, q_ref, send_sems, recv_sems, local_sem):
        copies = _chip_exchange_copies(h_ref, q_ref, send_sems, recv_sems, local_sem)
        _chip_exchange_start(copies)
        _chip_exchange_wait(copies)

    return pl.pallas_call(body, name=name, in_specs=[_ANY], out_specs=_ANY, out_shape=_sds(h.shape, h.dtype),
                          scratch_shapes=_CHIP_EXCHANGE_SEMS)(h)


def _sibling_allgather(buf, *, name):
    def body(x_ref, o_ref, send_sem, recv_sem):
        x, y, c, _ = _place()
        cp = pltpu.make_async_remote_copy(src_ref=x_ref.at[c], dst_ref=o_ref.at[c], send_sem=send_sem,
                                          recv_sem=recv_sem, device_id=(x, y, 1 - c), device_id_type=_MESH)
        cp.start()
        pltpu.make_async_remote_copy(src_ref=x_ref.at[c], dst_ref=o_ref.at[1 - c], send_sem=send_sem,
                                     recv_sem=recv_sem, device_id=(x, y, 1 - c), device_id_type=_MESH).wait_recv()
        cp.wait_send()

    return pl.pallas_call(
        body, name=name, in_specs=[_ANY], out_specs=_ANY, out_shape=_sds(buf.shape, buf.dtype),
        input_output_aliases={0: 0},
        scratch_shapes=[pltpu.SemaphoreType.DMA, pltpu.SemaphoreType.DMA])(buf)


def _chip_segments(g, kind):
    if kind == "col":
        n = g.shape[-1] // N_CHIPS
        s = g.reshape(g.shape[:-1] + (N_CHIPS, n))
        return jnp.moveaxis(s, -2, 0).reshape(N_CHIPS, -1)
    k = g.shape[-2] // N_CHIPS
    s = g.reshape(g.shape[:-2] + (N_CHIPS, k, g.shape[-1]))
    return jnp.moveaxis(s, -3, 0).reshape(N_CHIPS, -1)


def _join(blocks, kind):
    if kind == "col":
        s = jnp.moveaxis(blocks, 0, -2)
        return s.reshape(s.shape[:-2] + (s.shape[-2] * s.shape[-1],))
    return blocks.reshape((blocks.shape[0] * blocks.shape[1],) + blocks.shape[2:])


def _travel_form(local, names):
    mine = []
    for n in names:
        a = local[n]
        if n == "w_in":
            a = jnp.swapaxes(a, 1, 2)
        if n == "meta_tokens":
            a = a.reshape(2, N_META // 2, a.shape[-1])
        mine.append(a.astype(BF16) if n in GATHER_BF16 else a)
    return mine


def _rope_tables(t):
    half = QK_ROPE // 2
    inv_freq = 1.0 / (ROPE_THETA ** (jnp.arange(0, QK_ROPE, 2, dtype=F32) / QK_ROPE))
    pos = jnp.maximum(jnp.arange(t, dtype=F32) - PAD, 0.0)
    ang = pos[:, None] * inv_freq[None, :]
    cos, sin = jnp.cos(ang), jnp.sin(ang)
    z = jnp.zeros((t, half), F32)
    z2 = jnp.zeros((t, LANES - QK_ROPE), F32)
    return (jnp.concatenate([cos, cos, z2], axis=1), jnp.concatenate([-sin, z, z2], axis=1),
            jnp.concatenate([z, sin, z2], axis=1))


def _expand_matrix():
    lane = np.arange(SSD_INNER) // SSD_HEAD_DIM
    e = (np.arange(LANES)[:, None] == lane[None, :]).astype(np.float32)
    return jnp.asarray(e, BF16)


def _late_weights(full, i):
    w = {}
    kinds = dict(SHARDED)
    whole = lambda n: _join(full[n][:, i], kinds[n])
    for n in ("w_o_attn", "w_o_ssd", "w_out", "w_down"):
        w[n] = whole(n)
    w_up = whole("w_up")
    w["w_up_g"] = w_up[:, :D_FF]
    w["w_up_v"] = w_up[:, D_FF:]
    ffn_w = whole("ffn_conv_w")
    w["ffn_conv_wg"] = ffn_w[:, :D_FF]
    w["ffn_conv_wv"] = ffn_w[:, D_FF:]
    return w


def _early_weights(full, rep, i):
    w = {}
    kinds = dict(SHARDED)
    whole = lambda n: _join(full[n][:, i], kinds[n])
    wt = _join(full["w_in"][:, i], "row")
    zr = lambda n: jnp.zeros((n, D_MODEL), BF16)
    w["w_in_t"] = jnp.concatenate(
        [wt[0:1024], wt[1088:3136], wt[3136:6208], wt[6240:7264], wt[7264:8288],
         wt[1024:1088], zr(LANES - QK_ROPE), wt[6208:6240], zr(LANES - SSD_HEADS)], axis=0)
    wq = whole("w_q_b").reshape(Q_LORA, MLA_HEADS, QK_NOPE + QK_ROPE)
    w["w_q_b"] = jnp.pad(wq, ((0, 0), (0, 0), (0, QHEAD - QK_NOPE - QK_ROPE))).reshape(Q_LORA, MLA_HEADS * QHEAD)
    wkv = whole("w_kv_b").reshape(KV_LORA, MLA_HEADS, 2, QK_NOPE)
    w["w_kv_b"] = jnp.swapaxes(wkv, 1, 2).reshape(KV_LORA, 2 * MLA_HEADS * QK_NOPE)
    w["ssd_conv_w"] = whole("ssd_conv_w")
    row = lambda v: v.reshape(1, -1)
    w["q_norm_g"] = row(rep["q_norm_g"][i])
    w["kv_norm_g"] = row(rep["kv_norm_g"][i])
    w["ssd_conv_b"] = row(rep["ssd_conv_b"][i])
    w["dt_bias"] = row(jnp.pad(rep["dt_bias"][i], (0, LANES - SSD_HEADS)))
    a = -jnp.exp(rep["a_log"][i])
    w["a"] = a
    w["a_x"] = row(jnp.repeat(a, SSD_HEAD_DIM))
    w["d_x"] = row(jnp.repeat(rep["d_skip"][i], SSD_HEAD_DIM))
    w["ssd_norm_g"] = row(rep["ssd_norm_g"][i])
    w["ffn_conv_bg"] = row(rep["ffn_conv_b"][i][:D_FF])
    w["ffn_conv_bv"] = row(rep["ffn_conv_b"][i][D_FF:])
    for n in ("ln1_g", "ln1_b", "ln2_g", "ln2_b"):
        w[n] = row(rep[n][i])
    return w


def _layer_fwd(h, hb, w, tabs, expand, layer, late):
    tag = f"l{layer}_"
    cos, sin_a, sin_b = tabs
    s = {"h": h, "hb": hb}
    proj = _mm(hb, w["w_in_t"], trans_b=True, name=tag + "in_proj", tn=768)
    s["proj"] = proj
    qn = _rms_fwd(proj, OFF_Q, Q_LORA, w["q_norm_g"], name=tag + "q_norm")
    q_raw = _mm(qn, w["w_q_b"], name=tag + "q_up")
    q = _rope_q_fwd(q_raw, cos, sin_a, sin_b, name=tag + "q_rope")
    kvn = _rms_fwd(proj, OFF_KV, KV_LORA, w["kv_norm_g"], name=tag + "kv_norm")
    kv = _mm(kvn, w["w_kv_b"], name=tag + "kv_up", out_dtype=BF16)
    kpe = _rope_k_fwd(proj, cos, sin_a, sin_b, name=tag + "k_rope")
    if isinstance(late, dict):
        o, lse = _flash_fwd(q, kv, kpe, name=tag + "attn")
    else:
        o, lse, *got = _flash_fwd(q, kv, kpe, name=tag + "attn", gather=late)
        late = dict(zip(GATHER_LATE, got))
    w.update(_late_weights(late, layer))
    ya = _mm(o, w["w_o_attn"], name=tag + "attn_out")
    s.update(qn=qn, q=q, kvn=kvn, kv=kv, kpe=kpe, o=o, lse=lse, ya=ya)
    xs, bm, cm, dtx = _ssd_prep_fwd(proj, w["ssd_conv_w"], w["ssd_conv_b"], w["dt_bias"], expand, name=tag + "ssd_prep")
    y, prev = _ssd_fwd(xs, dtx, bm, cm, bm.T, w["a_x"], w["d_x"], name=tag + "ssd_scan")
    yn = _gnorm_fwd(y, proj, w["ssd_norm_g"], name=tag + "ssd_norm")
    ys = _mm(yn, w["w_o_ssd"], name=tag + "ssd_out")
    s.update(xs=xs, bm=bm, cm=cm, dtx=dtx, y=y, prev=prev, yn=yn, ys=ys)
    mixed = _mix_fwd(proj, ya, ys, name=tag + "mix")
    br = _mm(mixed, w["w_out"], name=tag + "mix_out")
    pre1, h1, h1b = _ln_fwd(h, br, w["ln1_g"], w["ln1_b"], name=tag + "ln1")
    s.update(mixed=mixed, pre1=pre1, h1b=h1b)
    ug = _mm(h1b, w["w_up_g"], name=tag + "up_g", tn=1408)
    uv = _mm(h1b, w["w_up_v"], name=tag + "up_v", tn=1408)
    act = _ffn_act_fwd(ug, uv, w["ffn_conv_wg"], w["ffn_conv_wv"], w["ffn_conv_bg"], w["ffn_conv_bv"],
                       name=tag + "ffn_act")
    ffn = _mm(act, w["w_down"], name=tag + "down")
    pre2, h2, h2b = _ln_fwd(h1, ffn, w["ln2_g"], w["ln2_b"], name=tag + "ln2")
    s.update(ug=ug, uv=uv, act=act, pre2=pre2)
    return h2, h2b, s, late


def _layer_bwd(dh2, w, s, tabs, reduce_m, tag, begin_exchange=None):
    cos, sin_a, sin_b = tabs
    g = {}
    proj = s["proj"]
    dpre2, dpre2b, g["ln2_g"], g["ln2_b"] = _ln_bwd(dh2, s["pre2"], w["ln2_g"], name=tag + "ln2_bwd")
    g["w_down"] = _mm_tn(s["act"], dpre2b, name=tag + "down_dw")
    dact = _mm(dpre2b, w["w_down"], trans_b=True, name=tag + "down_dx", tn=1408)
    dcg, dcv, dwg, dwv, dbg, dbv = _ffn_act_bwd(s["ug"], s["uv"], dact, w["ffn_conv_wg"], w["ffn_conv_wv"],
                                                w["ffn_conv_bg"], w["ffn_conv_bv"], name=tag + "ffn_act_bwd")
    g["ffn_conv_w"] = jnp.concatenate([dwg, dwv], axis=1)
    g["ffn_conv_b"] = jnp.concatenate([dbg, dbv], axis=1).reshape(-1)
    dug = _conv_bwd_input(dcg, w["ffn_conv_wg"], FFN_CONV, name=tag + "ffn_conv_bwd_g", tc=1408)
    duv = _conv_bwd_input(dcv, w["ffn_conv_wv"], FFN_CONV, name=tag + "ffn_conv_bwd_v", tc=1408)
    g["w_up"] = jnp.concatenate([_mm_tn(dug, s["h1b"], name=tag + "up_g_dw"),
                                 _mm_tn(duv, s["h1b"], name=tag + "up_v_dw")], axis=0)
    dh1 = _mm(dug, w["w_up_g"], trans_b=True, add=dpre2, add_scale=ALPHA, name=tag + "up_g_dx")
    dh1 = _mm(duv, w["w_up_v"], trans_b=True, add=dh1, name=tag + "up_v_dx")
    dpre1, dpre1b, g["ln1_g"], g["ln1_b"] = _ln_bwd(dh1, s["pre1"], w["ln1_g"], name=tag + "ln1_bwd")
    g["w_out"] = _mm_tn(s["mixed"], dpre1b, name=tag + "mix_out_dw")
    dmix = _mm(dpre1b, w["w_out"], trans_b=True, name=tag + "mix_out_dx")
    dya, dys, dga, dgs = _mix_bwd(dmix, proj, s["ya"], s["ys"], name=tag + "mix_bwd")
    g["w_o_ssd"] = _mm_tn(s["yn"], dys, name=tag + "ssd_out_dw")
    dyn = _mm(dys, w["w_o_ssd"], trans_b=True, out_dtype=BF16, name=tag + "ssd_out_dx")
    dy, dz, dgn = _gnorm_bwd(dyn, s["y"], proj, w["ssd_norm_g"], name=tag + "ssd_norm_bwd")
    g["ssd_norm_g"] = dgn.reshape(-1)
    dxs, ddtx, dbm, dcm, da_x, dd_x = _ssd_bwd(s["xs"], s["dtx"], s["bm"], s["cm"], s["cm"].T, s["prev"], dy,
                                               w["a_x"], w["d_x"], name=tag + "ssd_scan_bwd")
    g["a_log"] = da_x.reshape(SSD_HEADS, SSD_HEAD_DIM).sum(axis=1) * w["a"]
    g["d_skip"] = dd_x.reshape(SSD_HEADS, SSD_HEAD_DIM).sum(axis=1)
    dconv, ddtr, dcw, dcb, ddtb = _ssd_prep_bwd_a(proj, dxs, dbm, dcm, ddtx, w["ssd_conv_w"], w["ssd_conv_b"],
                                                  w["dt_bias"], reduce_m, name=tag + "ssd_prep_bwd")
    g["ssd_conv_w"] = dcw
    g["ssd_conv_b"] = dcb.reshape(-1)
    g["dt_bias"] = ddtb.reshape(-1)[:SSD_HEADS]
    dxbc = _conv_bwd_input(dconv, w["ssd_conv_w"], SSD_CONV, name=tag + "ssd_conv_bwd", tc=1024)
    g["w_o_attn"] = _mm_tn(s["o"], dya, name=tag + "attn_out_dw")
    exchange = begin_exchange(dict(g)) if begin_exchange else None
    do = _mm(dya, w["w_o_attn"], trans_b=True, out_dtype=BF16, name=tag + "attn_out_dx")
    delta = _attn_delta(do, s["o"], name=tag + "attn_delta")
    by_tile = lambda a: a.reshape(MLA_HEADS, -1, _tile(a.shape[1], 640))
    dq, dkn, dkp, dv, *exchanged = _flash_bwd(s["q"], s["kv"], s["kpe"], do, by_tile(s["lse"]), by_tile(delta),
                                              name=tag + "attn_bwd", exchange=exchange)
    dq_raw = _rope_q_bwd(dq, cos, sin_a, sin_b, name=tag + "q_rope_bwd")
    dwq = _mm_tn(s["qn"], dq_raw, name=tag + "q_up_dw")
    g["w_q_b"] = dwq.reshape(Q_LORA, MLA_HEADS, QHEAD)[:, :, :QK_NOPE + QK_ROPE].reshape(Q_LORA, -1)
    dqn = _mm(dq_raw, w["w_q_b"], trans_b=True, out_dtype=BF16, name=tag + "q_up_dx")
    dqlat, dgq = _rms_bwd(dqn, proj, OFF_Q, Q_LORA, w["q_norm_g"], name=tag + "q_norm_bwd")
    g["q_norm_g"] = dgq.reshape(-1)
    dkv = jnp.concatenate([dkn, dv], axis=1)
    dwkv = _mm_tn(s["kvn"], dkv, name=tag + "kv_up_dw")
    g["w_kv_b"] = jnp.swapaxes(dwkv.reshape(KV_LORA, 2, MLA_HEADS, QK_NOPE), 1, 2).reshape(KV_LORA, -1)
    dkvn = _mm(dkv, w["w_kv_b"], trans_b=True, out_dtype=BF16, name=tag + "kv_up_dx")
    dkvlat, dgkv = _rms_bwd(dkvn, proj, OFF_KV, KV_LORA, w["kv_norm_g"], name=tag + "kv_norm_bwd")
    g["kv_norm_g"] = dgkv.reshape(-1)
    dkpe = _rope_k_bwd(dkp, cos, sin_a, sin_b, name=tag + "k_rope_bwd")
    h = s["hb"]
    comps = ((dqlat, OFF_Q), (dkvlat, OFF_KV), (dz, OFF_Z), (dxbc, OFF_XBC), (dga, OFF_GA), (dgs, OFF_GS),
             (dkpe, OFF_KPE), (ddtr, OFF_DT))
    dws = {off: _mm_tn(dc, h, name=f"{tag}in_dw{n}") for n, (dc, off) in enumerate(comps)}
    with_w = lambda group: [(dc, w["w_in_t"][off:off + dc.shape[1]]) for dc, off in group]
    wide = [c for c in comps if c[1] in (OFF_Z, OFF_XBC)]
    rest = [c for c in comps if c[1] not in (OFF_Z, OFF_XBC)]
    dh = _mm_sum(with_w(wide), dpre1, add_scale=ALPHA, name=tag + "in_dx_wide")
    dh = _mm_sum(with_w(rest), dh, name=tag + "in_dx_rest")
    g["w_in"] = jnp.concatenate([dws[OFF_Q], dws[OFF_KV], dws[OFF_KPE][:QK_ROPE], dws[OFF_Z], dws[OFF_XBC],
                                 dws[OFF_DT][:SSD_HEADS], dws[OFF_GA], dws[OFF_GS]], axis=0)
    return dh, g, (exchanged[0] if exchanged else None)


def _local_step(x, target, full, late, rep):
    seq = x.shape[0]
    t = seq + ROW0
    tabs = _rope_tables(t)
    expand = _expand_matrix()
    reduce_m = expand.T
    meta = _join(full["meta_tokens"].reshape(N_CHIPS, N_META, -1), "col")
    xin = jnp.concatenate([jnp.zeros((PAD, D_MODEL), F32), meta, x], axis=0)
    row = lambda v: v.reshape(1, -1)
    _, h, hb = _ln_fwd(xin, None, row(rep["emb_ln_g"]), row(rep["emb_ln_b"]), name="emb_ln")
    ws, saved = [], []
    for i in range(DEPTH):
        w = _early_weights(full, rep, i)
        h, hb, s, late = _layer_fwd(h, hb, w, tabs, expand, i, late)
        ws.append(w)
        saved.append(s)
    dh, loss = _loss_grad(h, target, name="loss")
    dh, g1, _ = _layer_bwd(dh, ws[1], saved[1], tabs, reduce_m, "l1_")
    layouts = []

    def begin_exchange(g0_so_far):
        g4, layout = _pack_part([((1, n), g) for n, g in g1.items()] + [((0, n), g) for n, g in g0_so_far.items()])
        layouts.append(layout)
        return _reduce_begin(g4, "reduce1_")

    dh, g0, parts1 = _layer_bwd(dh, ws[0], saved[0], tabs, reduce_m, "l0_", begin_exchange=begin_exchange)
    dxin, _, dg, db = _ln_bwd(dh, xin, row(rep["emb_ln_g"]), name="emb_ln_bwd")
    rest = [((0, n), g0[n]) for n in LATE_GRADS]
    rest += [((0, "emb_ln_g"), dg), ((0, "emb_ln_b"), db), ((0, "meta_tokens"), dxin[PAD:ROW0])]
    return loss, dxin[ROW0:], rest, parts1, layouts[0]


def _pack_part(entries):
    kinds = dict(SHARDED)
    pieces, layout, tail, row0 = [], [], [], 0

    def add(key, piece):
        nonlocal row0
        pieces.append(piece)
        layout.append((key, row0, piece.shape[1]))
        row0 += piece.shape[1]

    for key, g in entries:
        if key[1] in kinds:
            kind = "row" if key[1] in TRANSPOSED_GRADS else kinds[key[1]]
            add(key, _chip_segments(g, kind).reshape(N_CHIPS, -1, LANES))
    for key, g in entries:
        if key[1] not in kinds and g.size % LANES == 0:
            add(key, jnp.broadcast_to(g.reshape(1, -1, LANES), (N_CHIPS, g.size // LANES, LANES)))
        elif key[1] not in kinds:
            tail.append((key, g.reshape(-1)))
    if tail:
        vec = jnp.concatenate([g for _, g in tail])
        vec = jnp.pad(vec, (0, -vec.shape[0] % LANES)).reshape(1, -1, LANES)
        add(("tail", tuple((key, g.shape[0]) for key, g in tail)), jnp.broadcast_to(vec, (N_CHIPS,) + vec.shape[1:]))
    rows = -(-row0 // REDUCE_ROW_ALIGN) * REDUCE_ROW_ALIGN
    pieces.append(jnp.zeros((N_CHIPS, rows - row0, LANES), F32))
    return jnp.concatenate(pieces, axis=1), layout


def _core_index():
    return lax.axis_index("c").astype(jnp.int32).reshape(1)


def _reduce_begin(g4, tag):
    n, r, _ = g4.shape
    g5 = g4.reshape(n, 2, r // 2, LANES)
    got = _sibling_swap(g5, name=tag + "pair_swap")
    return _pair_add(g5, got, _core_index(), name=tag + "pair_add")


def _reduce_end(parts, tag):
    half = _sum_chips(parts, _core_index(), name=tag + "chip_sum")
    both = _sibling_allgather(half, name=tag + "pair_gather")
    return both.reshape(-1, LANES)


def _unpack_part(flat, layout, shapes):
    out = {}
    for key, row0, rows in layout:
        piece = flat[row0:row0 + rows]
        if key[0] == "tail":
            vec, off = piece.reshape(-1), 0
            for sub, size in key[1]:
                out[sub] = vec[off:off + size]
                off += size
        elif key[1] in TRANSPOSED_GRADS:
            out[key] = piece.reshape(shapes[key[1]][1], shapes[key[1]][0]).T
        else:
            out[key] = piece.reshape(shapes[key[1]])
    return out


def kernel(x, meta_tokens, emb_ln_g, emb_ln_b, w_in, q_norm_g, w_q_b, kv_norm_g, w_kv_b, w_o_attn, ssd_conv_w, ssd_conv_b, dt_bias, a_log, d_skip, ssd_norm_g, w_o_ssd, w_out, ln1_g, ln1_b, w_up, ffn_conv_w, ffn_conv_b, w_down, ln2_g, ln2_b, loss_target, m_meta_tokens, m_emb_ln_g, m_emb_ln_b, m_w_in, m_q_norm_g, m_w_q_b, m_kv_norm_g, m_w_kv_b, m_w_o_attn, m_ssd_conv_w, m_ssd_conv_b, m_dt_bias, m_a_log, m_d_skip, m_ssd_norm_g, m_w_o_ssd, m_w_out, m_ln1_g, m_ln1_b, m_w_up, m_ffn_conv_w, m_ffn_conv_b, m_w_down, m_ln2_g, m_ln2_b, v_meta_tokens, v_emb_ln_g, v_emb_ln_b, v_w_in, v_q_norm_g, v_w_q_b, v_kv_norm_g, v_w_kv_b, v_w_o_attn, v_ssd_conv_w, v_ssd_conv_b, v_dt_bias, v_a_log, v_d_skip, v_ssd_norm_g, v_w_o_ssd, v_w_out, v_ln1_g, v_ln1_b, v_w_up, v_ffn_conv_w, v_ffn_conv_b, v_w_down, v_ln2_g, v_ln2_b):
    given = dict(locals())
    local_w = {n: given[n] for n in WEIGHTS}
    local_m = {n: given["m_" + n] for n in WEIGHTS}
    local_v = {n: given["v_" + n] for n in WEIGHTS}
    full = dict(zip(GATHER_EARLY, _chip_allgather(_travel_form(local_w, GATHER_EARLY), name="gather_early")))
    rep = {n: local_w[n] for n in REPLICATED}
    loss, grad_x, rest, parts1, layout1 = _local_step(x[0], loss_target[0], full, _travel_form(local_w, GATHER_LATE), rep)
    g4, layout0 = _pack_part(rest)
    parts0 = _chip_exchange(_reduce_begin(g4, "reduce0_"), name="reduce0_chip_exchange")
    shapes = {n: (local_w[n].shape if n in OUTSIDE else local_w[n].shape[1:]) for n in WEIGHTS}
    summed = _unpack_part(_reduce_end(parts0, "reduce0_"), layout0, shapes)
    summed.update(_unpack_part(_reduce_end(parts1, "reduce1_"), layout1, shapes))
    grad = {n: (summed[0, n] if n in OUTSIDE else jnp.stack([summed[i, n] for i in range(DEPTH)])) for n in WEIGHTS}
    upd = {}
    small = [n for n in WEIGHTS if n not in GATHER_BF16]
    for n in GATHER_BF16:
        upd[n] = _adamw(grad[n], local_w[n], local_m[n], local_v[n], name="adamw_" + n)
    res = _adamw_small([(grad[n], local_w[n], local_m[n], local_v[n]) for n in small], name="adamw_small")
    upd.update(zip(small, res))
    total = lax.psum(loss[0, 0], ("x", "y", "c"))
    outs = [total, grad_x[None]] + [grad[n] for n in WEIGHTS]
    for q in range(3):
        outs.extend(upd[n][q] for n in WEIGHTS)
    return tuple(outs)
```

```python
import functools
import math

import numpy as np
import jax
import jax.numpy as jnp
from jax import lax
from jax.experimental import pallas as pl
from jax.experimental.pallas import tpu as pltpu

F32 = jnp.float32
BF16 = jnp.bfloat16

D_MODEL = 1024
N_META = 16
DEPTH = 2
MLA_HEADS = 8
Q_LORA = 768
KV_LORA = 256
QK_NOPE = 128
QK_ROPE = 64
V_HEAD = 128
ROPE_THETA = 10000.0
NEG_INF = -1e30
PAD_KEY_SCORE = -1e30
SSD_INNER = 2048
SSD_HEAD_DIM = 64
SSD_HEADS = 32
SSD_GROUPS = 4
SSD_STATE = 128
SSD_CONV = 4
SSD_CONV_DIM = 3072
CHUNK = 128
D_FF = 2816
FFN_CONV = 3
LN_EPS = 1e-5
RMS_EPS = 1e-6
ALPHA = (2 * DEPTH) ** 0.25
ATTN_SCALE = (QK_NOPE + QK_ROPE) ** -0.5
LOG2E = math.log2(math.e)
LN2 = math.log(2.0)
Q_SCALE = ATTN_SCALE * LOG2E
ADAM_LR = 0.001
ADAM_B1 = 0.9
ADAM_B2 = 0.999
ADAM_EPS = 1e-08
ADAM_WD = 0.01
ADAM_STEP = 10

LANES = 128
PAD = 112
ROW0 = PAD + N_META
QHEAD = 256
GROUP_W = SSD_INNER // SSD_GROUPS
HALO = 8
VMEM_LIMIT_BYTES = 56 * 1024 * 1024
MM_VMEM_BUDGET = 46 * 1024 * 1024
MM_MAX_ROW_TILE = 1664
N_CHIPS = 4

OFF_Q, OFF_KV, OFF_Z, OFF_XBC, OFF_GA, OFF_GS, OFF_KPE, OFF_DT = 0, 768, 1024, 3072, 6144, 7168, 8192, 8320
IN_COLS_P = 8448

NT_DIMS = (((1,), (1,)), ((), ()))
NN_DIMS = (((1,), (0,)), ((), ()))
TN_DIMS = (((0,), (0,)), ((), ()))

SHARDED = (("meta_tokens", "col"), ("w_in", "col"), ("w_q_b", "col"), ("w_kv_b", "col"), ("w_o_attn", "row"),
           ("ssd_conv_w", "col"), ("w_o_ssd", "row"), ("w_out", "row"), ("w_up", "col"), ("ffn_conv_w", "col"),
           ("w_down", "row"))
REPLICATED = ("emb_ln_g", "emb_ln_b", "q_norm_g", "kv_norm_g", "ssd_conv_b", "dt_bias", "a_log", "d_skip",
              "ssd_norm_g", "ln1_g", "ln1_b", "ffn_conv_b", "ln2_g", "ln2_b")
WEIGHTS = ("meta_tokens", "emb_ln_g", "emb_ln_b", "w_in", "q_norm_g", "w_q_b", "kv_norm_g", "w_kv_b", "w_o_attn",
           "ssd_conv_w", "ssd_conv_b", "dt_bias", "a_log", "d_skip", "ssd_norm_g", "w_o_ssd", "w_out", "ln1_g",
           "ln1_b", "w_up", "ffn_conv_w", "ffn_conv_b", "w_down", "ln2_g", "ln2_b")
GATHER_BF16 = ("w_in", "w_q_b", "w_kv_b", "w_o_attn", "w_o_ssd", "w_out", "w_up", "w_down")
GATHER_EARLY = ("meta_tokens", "w_in", "w_q_b", "w_kv_b", "ssd_conv_w")
GATHER_LATE = ("w_o_attn", "w_o_ssd", "w_out", "w_up", "ffn_conv_w", "w_down")
OUTSIDE = ("meta_tokens", "emb_ln_g", "emb_ln_b")
LATE_GRADS = ("w_in", "w_q_b", "q_norm_g", "w_kv_b", "kv_norm_g")
TRANSPOSED_GRADS = ("w_in", "w_up")
REDUCE_ROW_ALIGN = 512


def _tile(n, target, base=LANES):
    best = None
    d = base
    while d <= min(n, target):
        if n % d == 0:
            best = d
        d += base
    return n if best is None else best


def _cp(*sem):
    return pltpu.CompilerParams(dimension_semantics=sem, vmem_limit_bytes=VMEM_LIMIT_BYTES)


def _sds(shape, dtype):
    return jax.ShapeDtypeStruct(shape, dtype)


def _row_ids(i, tr, shape):
    return i * tr + lax.broadcasted_iota(jnp.int32, shape, 0)


def _sigmoid(x):
    return 1.0 / (1.0 + jnp.exp(-x))


def _mm_tiles(m, n, tn_max, tk, nk, a_bytes, b_bytes, out_bytes, add_bytes):
    divisors = lambda size, cap: [d for d in range(min(size, cap) // LANES * LANES, 0, -LANES) if size % d == 0]
    for tm in divisors(m, MM_MAX_ROW_TILE):
        for tn in divisors(n, tn_max):
            blocks = 2 * (tm * tk * a_bytes + tk * tn * b_bytes + tm * tn * (out_bytes + add_bytes))
            temps = tm * tn * 4 * (2 if nk > 1 else 1) + tm * tk * 2 + tk * tn * 2
            if blocks + temps <= MM_VMEM_BUDGET:
                return tm, tn
    return LANES, LANES


def _mm(a, b, *, name, trans_b=False, out_dtype=F32, add=None, add_scale=1.0, tn=1024, tk=1408):
    m, k_dim = a.shape
    n = b.shape[0] if trans_b else b.shape[1]
    tk = _tile(k_dim, tk)
    nk = k_dim // tk
    has_add = add is not None
    tm, tn = _mm_tiles(m, n, tn, tk, nk, a.dtype.itemsize, b.dtype.itemsize, jnp.dtype(out_dtype).itemsize,
                       add.dtype.itemsize if has_add else 0)
    dims = NT_DIMS if trans_b else NN_DIMS

    def body(*refs):
        a_ref, b_ref = refs[0], refs[1]
        r_ref = refs[2] if has_add else None
        o_ref = refs[3] if has_add else refs[2]
        part = lax.dot_general(a_ref[...].astype(BF16), b_ref[...].astype(BF16), dims, preferred_element_type=F32)

        def finish(r):
            if has_add:
                r = r + add_scale * r_ref[...].astype(F32)
            o_ref[...] = r.astype(out_dtype)

        if nk == 1:
            finish(part)
        else:
            acc = refs[-1]
            kk = pl.program_id(2)

            @pl.when(kk == 0)
            def _():
                acc[...] = part

            @pl.when(kk > 0)
            def _():
                acc[...] += part

            @pl.when(kk == nk - 1)
            def _():
                finish(acc[...])

    in_specs = [pl.BlockSpec((tm, tk), lambda i, j, kk: (i, kk)),
                pl.BlockSpec((tn, tk), lambda i, j, kk: (j, kk)) if trans_b
                else pl.BlockSpec((tk, tn), lambda i, j, kk: (kk, j))]
    args = [a, b]
    if has_add:
        in_specs.append(pl.BlockSpec((tm, tn), lambda i, j, kk: (i, j)))
        args.append(add)
    return pl.pallas_call(
        body, name=name, grid=(m // tm, n // tn, nk), in_specs=in_specs,
        out_specs=pl.BlockSpec((tm, tn), lambda i, j, kk: (i, j)),
        out_shape=_sds((m, n), out_dtype),
        scratch_shapes=[pltpu.VMEM((tm, tn), F32)] if nk > 1 else [],
        compiler_params=_cp("parallel", "parallel", "arbitrary"),
    )(*args)


def _mm_sum(pairs, add, *, name, add_scale=1.0, tm=640):
    m, n = add.shape
    tm = _tile(m, tm)
    npairs = len(pairs)

    def body(*refs):
        a_refs, b_refs = refs[:npairs], refs[npairs:2 * npairs]
        r_ref, o_ref = refs[2 * npairs], refs[2 * npairs + 1]
        acc = add_scale * r_ref[...]
        for a_ref, b_ref in zip(a_refs, b_refs):
            acc = acc + jnp.dot(a_ref[...].astype(BF16), b_ref[...].astype(BF16), preferred_element_type=F32)
        o_ref[...] = acc

    in_specs = ([pl.BlockSpec((tm, a.shape[1]), lambda i: (i, 0)) for a, _ in pairs]
                + [pl.BlockSpec(b.shape, lambda i: (0, 0)) for _, b in pairs]
                + [pl.BlockSpec((tm, n), lambda i: (i, 0))])
    return pl.pallas_call(
        body, name=name, grid=(m // tm,), in_specs=in_specs, out_specs=pl.BlockSpec((tm, n), lambda i: (i, 0)),
        out_shape=_sds((m, n), F32), compiler_params=_cp("parallel"),
    )(*[a for a, _ in pairs], *[b for _, b in pairs], add)


def _mm_tn(a, b, *, name, tko=1408, tn=1024, tt=640):
    t, k_dim = a.shape
    n = b.shape[1]
    tko, tn, tt = _tile(k_dim, tko), _tile(n, tn), _tile(t, tt)

    def body(a_ref, b_ref, o_ref):
        part = lax.dot_general(a_ref[...].astype(BF16), b_ref[...].astype(BF16), TN_DIMS, preferred_element_type=F32)
        tt_i = pl.program_id(2)

        @pl.when(tt_i == 0)
        def _():
            o_ref[...] = part

        @pl.when(tt_i > 0)
        def _():
            o_ref[...] += part

    return pl.pallas_call(
        body, name=name, grid=(k_dim // tko, n // tn, t // tt),
        in_specs=[pl.BlockSpec((tt, tko), lambda i, j, s: (s, i)), pl.BlockSpec((tt, tn), lambda i, j, s: (s, j))],
        out_specs=pl.BlockSpec((tko, tn), lambda i, j, s: (i, j)),
        out_shape=_sds((k_dim, n), F32),
        compiler_params=_cp("parallel", "parallel", "arbitrary"),
    )(a, b)


def _ln_fwd(h, branch, g, b, *, name):
    t, d = h.shape
    tr = _tile(t, 640)
    has_branch = branch is not None

    def body(*refs):
        if has_branch:
            h_ref, br_ref, g_ref, b_ref, pre_ref, o_ref, ob_ref = refs
            pre = ALPHA * h_ref[...] + br_ref[...]
            pre_ref[...] = pre
        else:
            h_ref, g_ref, b_ref, o_ref, ob_ref = refs
            pre = h_ref[...]
        mu = jnp.mean(pre, axis=1, keepdims=True)
        xc = pre - mu
        var = jnp.mean(xc * xc, axis=1, keepdims=True)
        y = xc * lax.rsqrt(var + LN_EPS) * g_ref[...] + b_ref[...]
        rows = _row_ids(pl.program_id(0), tr, (tr, 1))
        y = jnp.where(rows >= PAD, y, 0.0)
        o_ref[...] = y
        ob_ref[...] = y.astype(BF16)

    row_spec = pl.BlockSpec((tr, d), lambda i: (i, 0))
    vec_spec = pl.BlockSpec((1, d), lambda i: (0, 0))
    if has_branch:
        return pl.pallas_call(
            body, name=name, grid=(t // tr,), in_specs=[row_spec, row_spec, vec_spec, vec_spec],
            out_specs=[row_spec] * 3, out_shape=[_sds((t, d), F32), _sds((t, d), F32), _sds((t, d), BF16)],
            compiler_params=_cp("parallel"))(h, branch, g, b)
    out, out_b = pl.pallas_call(
        body, name=name, grid=(t // tr,), in_specs=[row_spec, vec_spec, vec_spec],
        out_specs=[row_spec] * 2, out_shape=[_sds((t, d), F32), _sds((t, d), BF16)],
        compiler_params=_cp("parallel"))(h, g, b)
    return h, out, out_b


def _ln_bwd(dy, pre, g, *, name):
    t, d = pre.shape
    tr = _tile(t, 640)

    def body(dy_ref, pre_ref, g_ref, dpre_ref, dpre_b_ref, dg_ref, db_ref):
        i = pl.program_id(0)
        pre_v = pre_ref[...]
        mu = jnp.mean(pre_v, axis=1, keepdims=True)
        xc = pre_v - mu
        var = jnp.mean(xc * xc, axis=1, keepdims=True)
        rstd = lax.rsqrt(var + LN_EPS)
        xhat = xc * rstd
        rows = _row_ids(i, tr, (tr, 1))
        dym = jnp.where(rows >= PAD, dy_ref[...], 0.0)
        gdy = dym * g_ref[...]
        m1 = jnp.mean(gdy, axis=1, keepdims=True)
        m2 = jnp.mean(gdy * xhat, axis=1, keepdims=True)
        dpre = rstd * (gdy - m1 - xhat * m2)
        dpre_ref[...] = dpre
        dpre_b_ref[...] = dpre.astype(BF16)
        pg = jnp.sum(dym * xhat, axis=0, keepdims=True)
        pb = jnp.sum(dym, axis=0, keepdims=True)

        @pl.when(i == 0)
        def _():
            dg_ref[...] = pg
            db_ref[...] = pb

        @pl.when(i > 0)
        def _():
            dg_ref[...] += pg
            db_ref[...] += pb

    row_spec = pl.BlockSpec((tr, d), lambda i: (i, 0))
    vec_spec = pl.BlockSpec((1, d), lambda i: (0, 0))
    return pl.pallas_call(
        body, name=name, grid=(t // tr,), in_specs=[row_spec, row_spec, vec_spec],
        out_specs=[row_spec, row_spec, vec_spec, vec_spec],
        out_shape=[_sds((t, d), F32), _sds((t, d), BF16), _sds((1, d), F32), _sds((1, d), F32)],
        compiler_params=_cp("arbitrary"))(dy, pre, g)


def _rms_fwd(proj, col_off, width, g, *, name):
    t = proj.shape[0]
    tr = _tile(t, 640)
    cb = col_off // width

    def body(x_ref, g_ref, o_ref):
        x = x_ref[...]
        r = lax.rsqrt(jnp.mean(x * x, axis=1, keepdims=True) + RMS_EPS)
        o_ref[...] = (x * r * g_ref[...]).astype(BF16)

    return pl.pallas_call(
        body, name=name, grid=(t // tr,),
        in_specs=[pl.BlockSpec((tr, width), lambda i: (i, cb)), pl.BlockSpec((1, width), lambda i: (0, 0))],
        out_specs=pl.BlockSpec((tr, width), lambda i: (i, 0)), out_shape=_sds((t, width), BF16),
        compiler_params=_cp("parallel"))(proj, g)


def _rms_bwd(dy, proj, col_off, width, g, *, name):
    t = proj.shape[0]
    tr = _tile(t, 640)
    cb = col_off // width

    def body(dy_ref, x_ref, g_ref, dx_ref, dg_ref):
        i = pl.program_id(0)
        x = x_ref[...]
        dyv = dy_ref[...].astype(F32)
        r = lax.rsqrt(jnp.mean(x * x, axis=1, keepdims=True) + RMS_EPS)
        gdy = dyv * g_ref[...]
        m = jnp.mean(x * gdy, axis=1, keepdims=True)
        dx_ref[...] = (r * gdy - x * (r * r * r) * m).astype(BF16)
        pg = jnp.sum(dyv * x * r, axis=0, keepdims=True)

        @pl.when(i == 0)
        def _():
            dg_ref[...] = pg

        @pl.when(i > 0)
        def _():
            dg_ref[...] += pg

    return pl.pallas_call(
        body, name=name, grid=(t // tr,),
        in_specs=[pl.BlockSpec((tr, width), lambda i: (i, 0)), pl.BlockSpec((tr, width), lambda i: (i, cb)),
                  pl.BlockSpec((1, width), lambda i: (0, 0))],
        out_specs=[pl.BlockSpec((tr, width), lambda i: (i, 0)), pl.BlockSpec((1, width), lambda i: (0, 0))],
        out_shape=[_sds((t, width), BF16), _sds((1, width), F32)],
        compiler_params=_cp("arbitrary"))(dy, proj, g)


def _rope_apply(r, cos, sin_a, sin_b):
    return r * cos + pltpu.roll(r, 96, 1) * sin_a + pltpu.roll(r, 32, 1) * sin_b


def _rope_apply_t(dr, cos, sin_a, sin_b):
    return dr * cos + pltpu.roll(dr * sin_a, 32, 1) + pltpu.roll(dr * sin_b, 96, 1)


def _rope_q_fwd(q, cos, sin_a, sin_b, *, name):
    t, w = q.shape
    tr = _tile(t, 128)

    def body(q_ref, c_ref, sa_ref, sb_ref, o_ref):
        c, sa, sb = c_ref[...], sa_ref[...], sb_ref[...]
        flag = lax.broadcasted_iota(jnp.int32, (tr, LANES), 1) == QK_ROPE
        for h in range(MLA_HEADS):
            base = h * QHEAD
            o_ref[:, base:base + LANES] = (q_ref[:, base:base + LANES] * Q_SCALE).astype(BF16)
            rot = _rope_apply(q_ref[:, base + LANES:base + QHEAD], c, sa, sb)
            o_ref[:, base + LANES:base + QHEAD] = jnp.where(flag, 1.0, rot * Q_SCALE).astype(BF16)

    tab = pl.BlockSpec((tr, LANES), lambda i: (i, 0))
    row = pl.BlockSpec((tr, w), lambda i: (i, 0))
    return pl.pallas_call(body, name=name, grid=(t // tr,), in_specs=[row, tab, tab, tab], out_specs=row,
                          out_shape=_sds((t, w), BF16), compiler_params=_cp("parallel"))(q, cos, sin_a, sin_b)


def _rope_q_bwd(dq, cos, sin_a, sin_b, *, name):
    t, w = dq.shape
    tr = _tile(t, 128)

    def body(dq_ref, c_ref, sa_ref, sb_ref, o_ref):
        c, sa, sb = c_ref[...], sa_ref[...], sb_ref[...]
        for h in range(MLA_HEADS):
            base = h * QHEAD
            o_ref[:, base:base + LANES] = (dq_ref[:, base:base + LANES] * ATTN_SCALE).astype(BF16)
            d_rot = _rope_apply_t(dq_ref[:, base + LANES:base + QHEAD], c, sa, sb)
            o_ref[:, base + LANES:base + QHEAD] = (d_rot * ATTN_SCALE).astype(BF16)

    tab = pl.BlockSpec((tr, LANES), lambda i: (i, 0))
    row = pl.BlockSpec((tr, w), lambda i: (i, 0))
    return pl.pallas_call(body, name=name, grid=(t // tr,), in_specs=[row, tab, tab, tab], out_specs=row,
                          out_shape=_sds((t, w), BF16), compiler_params=_cp("parallel"))(dq, cos, sin_a, sin_b)


def _rope_k_fwd(proj, cos, sin_a, sin_b, *, name):
    t = proj.shape[0]
    tr = _tile(t, 640)
    cb = OFF_KPE // LANES

    def body(x_ref, c_ref, sa_ref, sb_ref, o_ref):
        rot = _rope_apply(x_ref[...], c_ref[...], sa_ref[...], sb_ref[...])
        rows = _row_ids(pl.program_id(0), tr, (tr, LANES))
        lane = lax.broadcasted_iota(jnp.int32, (tr, LANES), 1)
        o_ref[...] = jnp.where((lane == QK_ROPE) & (rows < PAD), PAD_KEY_SCORE, rot).astype(BF16)

    tab = pl.BlockSpec((tr, LANES), lambda i: (i, 0))
    return pl.pallas_call(body, name=name, grid=(t // tr,),
                          in_specs=[pl.BlockSpec((tr, LANES), lambda i: (i, cb)), tab, tab, tab], out_specs=tab,
                          out_shape=_sds((t, LANES), BF16), compiler_params=_cp("parallel"))(proj, cos, sin_a, sin_b)


def _rope_k_bwd(dkp, cos, sin_a, sin_b, *, name):
    nh, t, _ = dkp.shape
    tr = _tile(t, 640)

    def body(d_ref, c_ref, sa_ref, sb_ref, o_ref):
        tot = d_ref[0]
        for h in range(1, nh):
            tot = tot + d_ref[h]
        o_ref[...] = _rope_apply_t(tot, c_ref[...], sa_ref[...], sb_ref[...]).astype(BF16)

    tab = pl.BlockSpec((tr, LANES), lambda i: (i, 0))
    return pl.pallas_call(body, name=name, grid=(t // tr,),
                          in_specs=[pl.BlockSpec((nh, tr, LANES), lambda i: (0, i, 0)), tab, tab, tab], out_specs=tab,
                          out_shape=_sds((t, LANES), BF16), compiler_params=_cp("parallel"))(dkp, cos, sin_a, sin_b)


def _causal(tb, keys_first=False):
    a = lax.broadcasted_iota(jnp.int32, (tb, tb), 0)
    b = lax.broadcasted_iota(jnp.int32, (tb, tb), 1)
    return a <= b if keys_first else b <= a


def _flash_fwd(q, kv, kpe, *, name, gather=()):
    t = q.shape[0]
    nh = MLA_HEADS
    tb = _tile(t, 640)
    nb = t // tb
    na = len(gather)

    def attend(q_ref, kn_ref, v_ref, kp_ref, o_ref, lse_ref, extra):
        i = pl.program_id(1)
        qv = q_ref[...]

        def scores(j):
            r0 = pl.multiple_of(j * tb, tb)
            k = jnp.concatenate([kn_ref[pl.ds(r0, tb), :], kp_ref[pl.ds(r0, tb), :]], axis=1)
            return lax.dot_general(qv, k, NT_DIMS, preferred_element_type=F32)

        def update(s, j, state):
            m_prev, l_prev, acc = state
            m_new = jnp.maximum(m_prev, jnp.max(s, axis=1, keepdims=True))
            p = jnp.exp2(s - m_new)
            corr = jnp.exp2(m_prev - m_new)
            r0 = pl.multiple_of(j * tb, tb)
            pv = jnp.dot(p.astype(BF16), v_ref[pl.ds(r0, tb), :], preferred_element_type=F32)
            return m_new, corr * l_prev + jnp.sum(p, axis=1, keepdims=True), corr * acc + pv

        def loop(j, carry):
            s_cur, st = carry
            s_next = scores(j + 1)
            return s_next, update(s_cur, j, st)

        state = (jnp.full((tb, 1), NEG_INF, F32), jnp.zeros((tb, 1), F32), jnp.zeros((tb, V_HEAD), F32))
        s_diag, state = lax.fori_loop(0, i, loop, (scores(0), state))
        m, l, acc = update(jnp.where(_causal(tb), s_diag, NEG_INF), i, state)
        o_ref[...] = (acc / l).astype(BF16)
        lse_ref[0] = m + jnp.log2(l)

        if na:
            step = pl.program_id(0) * nb + i
            for phase, at in enumerate((0, (nh * nb) // 2, nh * nb - 1)):
                @pl.when(step == at)
                def _(phase=phase):
                    _chip_allgather_phase(phase, extra[:na], extra[na:2 * na], *extra[2 * na:])

    def body(q_ref, kn_ref, v_ref, kp_ref, *rest):
        attend(q_ref, kn_ref, v_ref, kp_ref, *rest[na:na + 2], extra=rest[:na] + rest[na + 2:])

    gather = list(gather)
    return pl.pallas_call(
        body, name=name, grid=(nh, nb),
        in_specs=[pl.BlockSpec((tb, QHEAD), lambda h, i: (i, h)),
                  pl.BlockSpec((t, LANES), lambda h, i: (0, h)),
                  pl.BlockSpec((t, LANES), lambda h, i: (0, nh + h)),
                  pl.BlockSpec((t, LANES), lambda h, i: (0, 0))] + [_ANY] * na,
        out_specs=[pl.BlockSpec((tb, V_HEAD), lambda h, i: (i, h)),
                   pl.BlockSpec((1, tb, 1), lambda h, i: (h, i, 0))] + [_ANY] * na,
        out_shape=[_sds((t, nh * V_HEAD), BF16), _sds((nh, t, 1), F32)] + _chip_allgather_shapes(gather),
        scratch_shapes=_chip_allgather_sems(na) if na else [],
        compiler_params=_cp("arbitrary", "arbitrary"))(q, kv, kv, kpe, *gather)


def _attn_delta(do, o, *, name):
    t = o.shape[0]
    nh = MLA_HEADS
    tr = _tile(t, 640)

    def body(do_ref, o_ref, d_ref):
        d_ref[0] = jnp.sum(do_ref[...].astype(F32) * o_ref[...].astype(F32), axis=1, keepdims=True)

    blk = pl.BlockSpec((tr, V_HEAD), lambda h, i: (i, h))
    return pl.pallas_call(body, name=name, grid=(nh, t // tr), in_specs=[blk, blk],
                          out_specs=pl.BlockSpec((1, tr, 1), lambda h, i: (h, i, 0)),
                          out_shape=_sds((nh, t, 1), F32), compiler_params=_cp("parallel", "parallel"))(do, o)


def _flash_bwd(q, kv, kpe, do, lse, delta, *, name, exchange=None):
    t = q.shape[0]
    nh = MLA_HEADS
    tb = lse.shape[2]
    nb = t // tb
    fused = exchange is not None

    def body(*refs):
        q_ref, do_ref, lse_ref, dl_ref, kn_ref, v_ref, kp_ref = refs[:7]
        dq_ref, dkn_ref, dkp_ref, dv_ref = refs[7 + fused:11 + fused]
        j = pl.program_id(1)

        if fused:
            copies = functools.partial(_chip_exchange_copies, refs[7], refs[11 + fused], *refs[12 + fused:])
            first = (pl.program_id(0) == 0) & (j == 0)
            last = (pl.program_id(0) == nh - 1) & (j == nb - 1)

            @pl.when(first)
            def _():
                _chip_exchange_start(copies())

        @pl.when(j == 0)
        def _():
            dq_ref[...] = jnp.zeros((t, QHEAD), F32)

        k = jnp.concatenate([kn_ref[...], kp_ref[...]], axis=1)
        v = v_ref[...]

        def tile(i, carry, masked):
            dk, dv = carry
            r0 = pl.multiple_of(i * tb, tb)
            qv = q_ref[pl.ds(r0, tb), :]
            dov = do_ref[pl.ds(r0, tb), :]
            st = lax.dot_general(k, qv, NT_DIMS, preferred_element_type=F32)
            if masked:
                st = jnp.where(_causal(tb, keys_first=True), st, NEG_INF)
            pt = jnp.exp2(st - lse_ref[0, pl.ds(i, 1), :])
            dpt = lax.dot_general(v, dov, NT_DIMS, preferred_element_type=F32)
            dst = (pt * (dpt - dl_ref[0, pl.ds(i, 1), :])).astype(BF16)
            dv = dv + jnp.dot(pt.astype(BF16), dov, preferred_element_type=F32)
            dk = dk + jnp.dot(dst, qv, preferred_element_type=F32)
            dq_ref[pl.ds(r0, tb), :] += lax.dot_general(dst, k, TN_DIMS, preferred_element_type=F32)
            return dk, dv

        carry = tile(j, (jnp.zeros((tb, QHEAD), F32), jnp.zeros((tb, V_HEAD), F32)), True)
        dk, dv = lax.fori_loop(j + 1, nb, lambda i, c: tile(i, c, False), carry)
        dkn_ref[...] = (dk[:, :LANES] * LN2).astype(BF16)
        dkp_ref[0] = dk[:, LANES:] * LN2
        dv_ref[...] = dv.astype(BF16)

        if fused:
            @pl.when(last)
            def _():
                _chip_exchange_wait(copies())

    stat = pl.BlockSpec((1, nb, tb), lambda h, j: (h, 0, 0))
    in_specs = [pl.BlockSpec((t, QHEAD), lambda h, j: (0, h)),
                pl.BlockSpec((t, V_HEAD), lambda h, j: (0, h)),
                stat, stat,
                pl.BlockSpec((tb, LANES), lambda h, j: (j, h)),
                pl.BlockSpec((tb, LANES), lambda h, j: (j, nh + h)),
                pl.BlockSpec((tb, LANES), lambda h, j: (j, 0))]
    out_specs = [pl.BlockSpec((t, QHEAD), lambda h, j: (0, h)),
                 pl.BlockSpec((tb, LANES), lambda h, j: (j, h)),
                 pl.BlockSpec((1, tb, LANES), lambda h, j: (h, j, 0)),
                 pl.BlockSpec((tb, V_HEAD), lambda h, j: (j, h))]
    out_shape = [_sds((t, nh * QHEAD), F32), _sds((t, nh * LANES), BF16), _sds((nh, t, LANES), F32),
                 _sds((t, nh * V_HEAD), BF16)]
    args = [q, do, lse, delta, kv, kv, kpe]
    scratch = []
    if fused:
        in_specs.append(_ANY)
        out_specs.append(_ANY)
        out_shape.append(_sds(exchange.shape, exchange.dtype))
        args.append(exchange)
        scratch = _CHIP_EXCHANGE_SEMS
    return pl.pallas_call(body, name=name, grid=(nh, nb), in_specs=in_specs, out_specs=out_specs, out_shape=out_shape,
                          scratch_shapes=scratch, compiler_params=_cp("arbitrary", "arbitrary"))(*args)


def _fill_prev(buf, x_ref, halo_ref, i, tr):
    buf[pl.ds(0, HALO), :] = jnp.where(i > 0, halo_ref[...], 0.0)
    buf[pl.ds(HALO, tr), :] = x_ref[...]


def _conv_prev(buf, w_ref, kw, tr):
    acc = w_ref[kw - 1:kw, :] * buf[pl.ds(HALO, tr), :]
    for k in range(kw - 1):
        acc = acc + w_ref[k:k + 1, :] * buf[pl.ds(HALO - kw + 1 + k, tr), :]
    return acc


def _conv_dw(buf, dc, kw, tr):
    rows = [jnp.sum(dc * buf[pl.ds(HALO - kw + 1 + k, tr), :], axis=0, keepdims=True) for k in range(kw)]
    return jnp.concatenate(rows, axis=0)


def _conv_next(buf, dc_ref, halo_ref, w_ref, kw, i, n_tiles, tr):
    buf[pl.ds(0, tr), :] = dc_ref[...]
    buf[pl.ds(tr, HALO), :] = jnp.where(i < n_tiles - 1, halo_ref[...], 0.0)
    acc = w_ref[kw - 1:kw, :] * buf[pl.ds(0, tr), :]
    for k in range(kw - 1):
        acc = acc + w_ref[k:k + 1, :] * buf[pl.ds(kw - 1 - k, tr), :]
    return acc


def _split3(x):
    x1 = x.astype(BF16)
    r1 = x - x1.astype(F32)
    x2 = r1.astype(BF16)
    x3 = (r1 - x2.astype(F32)).astype(BF16)
    return x1, x2, x3


def _dot3(parts, m, left):
    tot = None
    for p in parts:
        r = jnp.dot(m, p, preferred_element_type=F32) if left else jnp.dot(p, m, preferred_element_type=F32)
        tot = r if tot is None else tot + r
    return tot


def _ssd_prep_fwd(proj, conv_w, conv_b, dt_bias, expand, *, name):
    t = proj.shape[0]
    tr = _tile(t, 128)
    nt = t // tr
    hb = tr // HALO
    cw = SSD_CONV_DIM
    cb_x = OFF_XBC // cw
    cb_dt = OFF_DT // LANES

    def body(x_ref, halo_ref, dtr_ref, w_ref, b_ref, dtb_ref, e_ref, xs_ref, bm_ref, cm_ref, dtx_ref, buf):
        i = pl.program_id(0)
        _fill_prev(buf, x_ref, halo_ref, i, tr)
        conv = _conv_prev(buf, w_ref, SSD_CONV, tr) + b_ref[...]
        rows = _row_ids(i, tr, (tr, 1))
        live = rows >= PAD
        act = jnp.where(live, conv * _sigmoid(conv), 0.0)
        xs_ref[...] = act[:, :SSD_INNER]
        bm_ref[...] = act[:, SSD_INNER:SSD_INNER + GROUP_W]
        cm_ref[...] = act[:, SSD_INNER + GROUP_W:]
        dt = jnp.where(live, jax.nn.softplus(dtr_ref[...] + dtb_ref[...]), 0.0)
        dtx_ref[...] = _dot3(_split3(dt), e_ref[...], left=False)

    return pl.pallas_call(
        body, name=name, grid=(nt,),
        in_specs=[pl.BlockSpec((tr, cw), lambda i: (i, cb_x)),
                  pl.BlockSpec((HALO, cw), lambda i: (jnp.maximum(i * hb - 1, 0), cb_x)),
                  pl.BlockSpec((tr, LANES), lambda i: (i, cb_dt)),
                  pl.BlockSpec((SSD_CONV, cw), lambda i: (0, 0)),
                  pl.BlockSpec((1, cw), lambda i: (0, 0)),
                  pl.BlockSpec((1, LANES), lambda i: (0, 0)),
                  pl.BlockSpec((LANES, SSD_INNER), lambda i: (0, 0))],
        out_specs=[pl.BlockSpec((tr, SSD_INNER), lambda i: (i, 0)), pl.BlockSpec((tr, GROUP_W), lambda i: (i, 0)),
                   pl.BlockSpec((tr, GROUP_W), lambda i: (i, 0)), pl.BlockSpec((tr, SSD_INNER), lambda i: (i, 0))],
        out_shape=[_sds((t, SSD_INNER), F32), _sds((t, GROUP_W), F32), _sds((t, GROUP_W), F32),
                   _sds((t, SSD_INNER), F32)],
        scratch_shapes=[pltpu.VMEM((tr + HALO, cw), F32)],
        compiler_params=_cp("parallel"))(proj, proj, proj, conv_w, conv_b, dt_bias, expand)


def _ssd_prep_bwd_a(proj, dxs, dbm, dcm, ddtx, conv_w, conv_b, dt_bias, reduce_m, *, name):
    t = proj.shape[0]
    tr = _tile(t, 128)
    nt = t // tr
    hb = tr // HALO
    cw = SSD_CONV_DIM
    cb_x = OFF_XBC // cw
    cb_dt = OFF_DT // LANES

    def body(x_ref, halo_ref, dtr_ref, dxs_ref, dbm_ref, dcm_ref, ddtx_ref, w_ref, b_ref, dtb_ref, r_ref,
             dconv_ref, ddtr_ref, dw_ref, db_ref, ddtb_ref, buf):
        i = pl.program_id(0)
        _fill_prev(buf, x_ref, halo_ref, i, tr)
        conv = _conv_prev(buf, w_ref, SSD_CONV, tr) + b_ref[...]
        rows = _row_ids(i, tr, (tr, 1))
        live = rows >= PAD
        sg = _sigmoid(conv)
        dact = jnp.concatenate([dxs_ref[...], dbm_ref[...], dcm_ref[...]], axis=1)
        dconv = jnp.where(live, dact * (sg * (1.0 + conv * (1.0 - sg))), 0.0)
        dconv_ref[...] = dconv
        pw = _conv_dw(buf, dconv, SSD_CONV, tr)
        pb = jnp.sum(dconv, axis=0, keepdims=True)
        ddt = _dot3(_split3(ddtx_ref[...]), r_ref[...], left=False)
        ddtr = jnp.where(live, ddt * _sigmoid(dtr_ref[...] + dtb_ref[...]), 0.0)
        ddtr_ref[...] = ddtr.astype(BF16)
        pdb = jnp.sum(ddtr, axis=0, keepdims=True)

        @pl.when(i == 0)
        def _():
            dw_ref[...] = pw
            db_ref[...] = pb
            ddtb_ref[...] = pdb

        @pl.when(i > 0)
        def _():
            dw_ref[...] += pw
            db_ref[...] += pb
            ddtb_ref[...] += pdb

    return pl.pallas_call(
        body, name=name, grid=(nt,),
        in_specs=[pl.BlockSpec((tr, cw), lambda i: (i, cb_x)),
                  pl.BlockSpec((HALO, cw), lambda i: (jnp.maximum(i * hb - 1, 0), cb_x)),
                  pl.BlockSpec((tr, LANES), lambda i: (i, cb_dt)),
                  pl.BlockSpec((tr, SSD_INNER), lambda i: (i, 0)),
                  pl.BlockSpec((tr, GROUP_W), lambda i: (i, 0)),
                  pl.BlockSpec((tr, GROUP_W), lambda i: (i, 0)),
                  pl.BlockSpec((tr, SSD_INNER), lambda i: (i, 0)),
                  pl.BlockSpec((SSD_CONV, cw), lambda i: (0, 0)),
                  pl.BlockSpec((1, cw), lambda i: (0, 0)),
                  pl.BlockSpec((1, LANES), lambda i: (0, 0)),
                  pl.BlockSpec((SSD_INNER, LANES), lambda i: (0, 0))],
        out_specs=[pl.BlockSpec((tr, cw), lambda i: (i, 0)), pl.BlockSpec((tr, LANES), lambda i: (i, 0)),
                   pl.BlockSpec((SSD_CONV, cw), lambda i: (0, 0)), pl.BlockSpec((1, cw), lambda i: (0, 0)),
                   pl.BlockSpec((1, LANES), lambda i: (0, 0))],
        out_shape=[_sds((t, cw), F32), _sds((t, LANES), BF16), _sds((SSD_CONV, cw), F32), _sds((1, cw), F32),
                   _sds((1, LANES), F32)],
        scratch_shapes=[pltpu.VMEM((tr + HALO, cw), F32)],
        compiler_params=_cp("arbitrary"))(proj, proj, proj, dxs, dbm, dcm, ddtx, conv_w, conv_b, dt_bias, reduce_m)


def _conv_bwd_input(dconv, w, kw, *, name, out_dtype=BF16, tc=None):
    t, c = dconv.shape
    tr = _tile(t, 640)
    nt = t // tr
    hb = tr // HALO
    tc = _tile(c, tc or c)
    last_hb = t // HALO - 1

    def body(dc_ref, halo_ref, w_ref, o_ref, buf):
        i = pl.program_id(0)
        o_ref[...] = _conv_next(buf, dc_ref, halo_ref, w_ref, kw, i, nt, tr).astype(out_dtype)

    return pl.pallas_call(
        body, name=name, grid=(nt, c // tc),
        in_specs=[pl.BlockSpec((tr, tc), lambda i, j: (i, j)),
                  pl.BlockSpec((HALO, tc), lambda i, j: (jnp.minimum((i + 1) * hb, last_hb), j)),
                  pl.BlockSpec((kw, tc), lambda i, j: (0, j))],
        out_specs=pl.BlockSpec((tr, tc), lambda i, j: (i, j)), out_shape=_sds((t, c), out_dtype),
        scratch_shapes=[pltpu.VMEM((tr + HALO, tc), F32)],
        compiler_params=_cp("parallel", "parallel"))(dconv, dconv, w)


def _ffn_act_fwd(ug, uv, wg, wv, bg, bv, *, name):
    t, c = ug.shape
    tr = _tile(t, 128)
    hb = tr // HALO
    tc = _tile(c, 1408)

    def body(ug_ref, hg_ref, uv_ref, hv_ref, wg_ref, wv_ref, bg_ref, bv_ref, o_ref, bufg, bufv):
        i = pl.program_id(0)
        _fill_prev(bufg, ug_ref, hg_ref, i, tr)
        _fill_prev(bufv, uv_ref, hv_ref, i, tr)
        cg = _conv_prev(bufg, wg_ref, FFN_CONV, tr) + bg_ref[...]
        cv = _conv_prev(bufv, wv_ref, FFN_CONV, tr) + bv_ref[...]
        o_ref[...] = (cg * _sigmoid(cg) * cv).astype(BF16)

    blk = pl.BlockSpec((tr, tc), lambda i, j: (i, j))
    halo = pl.BlockSpec((HALO, tc), lambda i, j: (jnp.maximum(i * hb - 1, 0), j))
    wsp = pl.BlockSpec((FFN_CONV, tc), lambda i, j: (0, j))
    bsp = pl.BlockSpec((1, tc), lambda i, j: (0, j))
    return pl.pallas_call(
        body, name=name, grid=(t // tr, c // tc), in_specs=[blk, halo, blk, halo, wsp, wsp, bsp, bsp],
        out_specs=blk, out_shape=_sds((t, c), BF16),
        scratch_shapes=[pltpu.VMEM((tr + HALO, tc), F32), pltpu.VMEM((tr + HALO, tc), F32)],
        compiler_params=_cp("parallel", "parallel"))(ug, ug, uv, uv, wg, wv, bg, bv)


def _ffn_act_bwd(ug, uv, dact, wg, wv, bg, bv, *, name):
    t, c = ug.shape
    tr = _tile(t, 128)
    hb = tr // HALO
    tc = _tile(c, 1408)

    def body(ug_ref, hg_ref, uv_ref, hv_ref, da_ref, wg_ref, wv_ref, bg_ref, bv_ref,
             dcg_ref, dcv_ref, dwg_ref, dwv_ref, dbg_ref, dbv_ref, bufg, bufv):
        i = pl.program_id(1)
        _fill_prev(bufg, ug_ref, hg_ref, i, tr)
        _fill_prev(bufv, uv_ref, hv_ref, i, tr)
        cg = _conv_prev(bufg, wg_ref, FFN_CONV, tr) + bg_ref[...]
        cv = _conv_prev(bufv, wv_ref, FFN_CONV, tr) + bv_ref[...]
        sg = _sigmoid(cg)
        da = da_ref[...]
        dcg = da * cv * (sg * (1.0 + cg * (1.0 - sg)))
        dcv = da * (cg * sg)
        dcg_ref[...] = dcg
        dcv_ref[...] = dcv
        pwg = _conv_dw(bufg, dcg, FFN_CONV, tr)
        pwv = _conv_dw(bufv, dcv, FFN_CONV, tr)
        pbg = jnp.sum(dcg, axis=0, keepdims=True)
        pbv = jnp.sum(dcv, axis=0, keepdims=True)

        @pl.when(i == 0)
        def _():
            dwg_ref[...] = pwg
            dwv_ref[...] = pwv
            dbg_ref[...] = pbg
            dbv_ref[...] = pbv

        @pl.when(i > 0)
        def _():
            dwg_ref[...] += pwg
            dwv_ref[...] += pwv
            dbg_ref[...] += pbg
            dbv_ref[...] += pbv

    blk = pl.BlockSpec((tr, tc), lambda j, i: (i, j))
    halo = pl.BlockSpec((HALO, tc), lambda j, i: (jnp.maximum(i * hb - 1, 0), j))
    wsp = pl.BlockSpec((FFN_CONV, tc), lambda j, i: (0, j))
    bsp = pl.BlockSpec((1, tc), lambda j, i: (0, j))
    return pl.pallas_call(
        body, name=name, grid=(c // tc, t // tr), in_specs=[blk, halo, blk, halo, blk, wsp, wsp, bsp, bsp],
        out_specs=[blk, blk, wsp, wsp, bsp, bsp],
        out_shape=[_sds((t, c), F32), _sds((t, c), F32), _sds((FFN_CONV, c), F32), _sds((FFN_CONV, c), F32),
                   _sds((1, c), F32), _sds((1, c), F32)],
        scratch_shapes=[pltpu.VMEM((tr + HALO, tc), F32), pltpu.VMEM((tr + HALO, tc), F32)],
        compiler_params=_cp("parallel", "arbitrary"))(ug, ug, uv, uv, dact, wg, wv, bg, bv)


def _tri(lower):
    li = lax.broadcasted_iota(jnp.int32, (CHUNK, CHUNK), 0)
    si = lax.broadcasted_iota(jnp.int32, (CHUNK, CHUNK), 1)
    return li >= si if lower else li <= si


def _tri_ones(lower):
    return jnp.where(_tri(lower), 1.0, 0.0).astype(BF16)


def _decay_pair(acs, acs_t, lane0):
    col = acs[:, lane0:lane0 + 1]
    row = acs_t[lane0:lane0 + 1, :]
    low = jnp.where(_tri(True), jnp.exp(jnp.minimum(col - row, 0.0)), 0.0)
    upp = jnp.where(_tri(False), jnp.exp(jnp.minimum(row - col, 0.0)), 0.0)
    return low, upp


def _ssd_fwd(xs, dtx, bm, cm, bm_t, a_x, d_x, *, name):
    t = xs.shape[0]
    nc = t // CHUNK
    gw = GROUP_W

    def body(xs_ref, dt_ref, b_ref, c_ref, bt_ref, a_ref, d_ref, y_ref, prev_ref, h_s):
        @pl.when(pl.program_id(1) == 0)
        def _():
            h_s[...] = jnp.zeros((SSD_STATE, gw), F32)

        x = xs_ref[...]
        dt = dt_ref[...]
        acs = _dot3(_split3(dt * a_ref[...]), _tri_ones(True), left=True)
        acs_t = acs.T
        xc = x * dt
        bv = b_ref[...].astype(BF16)
        cv = c_ref[...].astype(BF16)
        cb = lax.dot_general(cv, bv, NT_DIMS, preferred_element_type=F32)
        lane = lax.broadcasted_iota(jnp.int32, (CHUNK, LANES), 1)
        pieces = []
        for pp in range(gw // LANES):
            xcp = xc[:, pp * LANES:(pp + 1) * LANES]
            acc = jnp.zeros((CHUNK, LANES), F32)
            for e in range(2):
                low, _ = _decay_pair(acs, acs_t, pp * LANES + e * SSD_HEAD_DIM)
                mine = (lane >= e * SSD_HEAD_DIM) & (lane < (e + 1) * SSD_HEAD_DIM)
                xm = jnp.where(mine, xcp, 0.0).astype(BF16)
                acc = acc + jnp.dot((cb * low).astype(BF16), xm, preferred_element_type=F32)
            pieces.append(acc)
        y_diag = jnp.concatenate(pieces, axis=1)
        h_prev = h_s[...]
        y_off = jnp.dot(cv, h_prev.astype(BF16), preferred_element_type=F32) * jnp.exp(acs)
        y_ref[...] = y_diag + y_off + d_ref[...] * x
        prev_ref[0] = h_prev
        last = acs[CHUNK - 1:CHUNK, :]
        w = jnp.exp(last - acs)
        st = jnp.dot(bt_ref[...].astype(BF16), (xc * w).astype(BF16), preferred_element_type=F32)
        h_s[...] = h_prev * jnp.exp(last) + st

    tok = pl.BlockSpec((CHUNK, gw), lambda g, c: (c, g))
    grp = pl.BlockSpec((CHUNK, SSD_STATE), lambda g, c: (c, g))
    vec = pl.BlockSpec((1, gw), lambda g, c: (0, g))
    return pl.pallas_call(
        body, name=name, grid=(SSD_GROUPS, nc),
        in_specs=[tok, tok, grp, grp, pl.BlockSpec((SSD_STATE, CHUNK), lambda g, c: (g, c)), vec, vec],
        out_specs=[tok, pl.BlockSpec((1, SSD_STATE, gw), lambda g, c: (c, 0, g))],
        out_shape=[_sds((t, SSD_INNER), F32), _sds((nc, SSD_STATE, SSD_INNER), F32)],
        scratch_shapes=[pltpu.VMEM((SSD_STATE, gw), F32)],
        compiler_params=_cp("parallel", "arbitrary"))(xs, dtx, bm, cm, bm_t, a_x, d_x)


def _ssd_bwd(xs, dtx, bm, cm, cm_t, prev, dy, a_x, d_x, *, name):
    t = xs.shape[0]
    nc = t // CHUNK
    gw = GROUP_W

    def body(xs_ref, dt_ref, b_ref, c_ref, ct_ref, prev_ref, dy_ref, a_ref, d_ref,
             dxs_ref, ddt_ref, db_ref, dc_ref, da_ref, dd_ref, g_s):
        first = pl.program_id(1) == 0

        @pl.when(first)
        def _():
            g_s[...] = jnp.zeros((SSD_STATE, gw), F32)

        x = xs_ref[...]
        dt = dt_ref[...]
        a = a_ref[...]
        dyv = dy_ref[...]
        acs = _dot3(_split3(dt * a), _tri_ones(True), left=True)
        acs_t = acs.T
        xc = x * dt
        bv = b_ref[...].astype(BF16)
        cv = c_ref[...].astype(BF16)
        cb = lax.dot_general(cv, bv, NT_DIMS, preferred_element_type=F32)
        cb_t = lax.dot_general(bv, cv, NT_DIMS, preferred_element_type=F32)
        last = acs[CHUNK - 1:CHUNK, :]
        w = jnp.exp(last - acs)
        cd = jnp.exp(last)
        p_in = prev_ref[0]
        p_b = p_in.astype(BF16)
        g_out = g_s[...]
        g_b = g_out.astype(BF16)
        dy_e = dyv * jnp.exp(acs)
        dy_eb = dy_e.astype(BF16)
        y_off_raw = jnp.dot(cv, p_b, preferred_element_type=F32)
        dacs = dy_e * y_off_raw
        d_c = lax.dot_general(dy_eb, p_b, NT_DIMS, preferred_element_type=F32)
        d_prev = jnp.dot(ct_ref[...].astype(BF16), dy_eb, preferred_element_type=F32)
        q_l = jnp.dot(bv, g_b, preferred_element_type=F32)
        dxc = w * q_l
        tw = xc * q_l * w
        dacs = dacs - tw
        d_b = lax.dot_general((xc * w).astype(BF16), g_b, NT_DIMS, preferred_element_type=F32)
        last_add = jnp.sum(tw, axis=0, keepdims=True) + cd * jnp.sum(g_out * p_in, axis=0, keepdims=True)
        g_s[...] = cd * g_out + d_prev
        lane = lax.broadcasted_iota(jnp.int32, (CHUNK, LANES), 1)
        d_cb = jnp.zeros((CHUNK, CHUNK), F32)
        d_cb_t = jnp.zeros((CHUNK, CHUNK), F32)
        dxc_pieces, dacs_pieces = [], []
        for pp in range(gw // LANES):
            xcp = xc[:, pp * LANES:(pp + 1) * LANES]
            dyp = dyv[:, pp * LANES:(pp + 1) * LANES]
            dxcp = jnp.zeros((CHUNK, LANES), F32)
            dacsp = jnp.zeros((CHUNK, LANES), F32)
            for e in range(2):
                low, upp = _decay_pair(acs, acs_t, pp * LANES + e * SSD_HEAD_DIM)
                mine = (lane >= e * SSD_HEAD_DIM) & (lane < (e + 1) * SSD_HEAD_DIM)
                m_low = cb * low
                m_upp = cb_t * upp
                dym = jnp.where(mine, dyp, 0.0).astype(BF16)
                xm = jnp.where(mine, xcp, 0.0).astype(BF16)
                dxcp = dxcp + jnp.dot(m_upp.astype(BF16), dym, preferred_element_type=F32)
                d_m = lax.dot_general(dym, xm, NT_DIMS, preferred_element_type=F32)
                d_m_t = lax.dot_general(xm, dym, NT_DIMS, preferred_element_type=F32)
                rs = jnp.sum(d_m * m_low, axis=1, keepdims=True)
                cs = jnp.sum(d_m_t * m_upp, axis=1, keepdims=True)
                dacsp = dacsp + jnp.where(lane == e * SSD_HEAD_DIM, rs - cs, 0.0)
                d_cb = d_cb + d_m * low
                d_cb_t = d_cb_t + d_m_t * upp
            dxc_pieces.append(dxcp)
            dacs_pieces.append(dacsp)
        dxc = dxc + jnp.concatenate(dxc_pieces, axis=1)
        dacs = dacs + jnp.concatenate(dacs_pieces, axis=1)
        rowi = lax.broadcasted_iota(jnp.int32, (CHUNK, gw), 0)
        dacs = dacs + jnp.where(rowi == CHUNK - 1, last_add, 0.0)
        dc_ref[...] = d_c + jnp.dot(d_cb.astype(BF16), bv, preferred_element_type=F32)
        db_ref[...] = d_b + jnp.dot(d_cb_t.astype(BF16), cv, preferred_element_type=F32)
        dda = _dot3(_split3(dacs), _tri_ones(False), left=True)
        ddt_ref[...] = dda * a + dxc * x
        dxs_ref[...] = dxc * dt + d_ref[...] * dyv
        pa = jnp.sum(dda * dt, axis=0, keepdims=True)
        pd = jnp.sum(dyv * x, axis=0, keepdims=True)

        @pl.when(first)
        def _():
            da_ref[...] = pa
            dd_ref[...] = pd

        @pl.when(jnp.logical_not(first))
        def _():
            da_ref[...] += pa
            dd_ref[...] += pd

    rc = lambda c: nc - 1 - c
    tok = pl.BlockSpec((CHUNK, gw), lambda g, c: (rc(c), g))
    grp = pl.BlockSpec((CHUNK, SSD_STATE), lambda g, c: (rc(c), g))
    vec = pl.BlockSpec((1, gw), lambda g, c: (0, g))
    return pl.pallas_call(
        body, name=name, grid=(SSD_GROUPS, nc),
        in_specs=[tok, tok, grp, grp, pl.BlockSpec((SSD_STATE, CHUNK), lambda g, c: (g, rc(c))),
                  pl.BlockSpec((1, SSD_STATE, gw), lambda g, c: (rc(c), 0, g)), tok, vec, vec],
        out_specs=[tok, tok, grp, grp, vec, vec],
        out_shape=[_sds((t, SSD_INNER), F32), _sds((t, SSD_INNER), F32), _sds((t, gw), F32), _sds((t, gw), F32),
                   _sds((1, SSD_INNER), F32), _sds((1, SSD_INNER), F32)],
        scratch_shapes=[pltpu.VMEM((SSD_STATE, gw), F32)],
        compiler_params=_cp("parallel", "arbitrary"))(xs, dtx, bm, cm, cm_t, prev, dy, a_x, d_x)


def _gnorm_fwd(y, proj, g, *, name):
    t = y.shape[0]
    tr = _tile(t, 640)
    zb = OFF_Z // GROUP_W

    def body(y_ref, z_ref, g_ref, o_ref):
        z = z_ref[...]
        v = y_ref[...] * (z * _sigmoid(z))
        r = lax.rsqrt(jnp.mean(v * v, axis=1, keepdims=True) + RMS_EPS)
        o_ref[...] = (v * r * g_ref[...]).astype(BF16)

    blk = pl.BlockSpec((tr, GROUP_W), lambda i, j: (i, j))
    return pl.pallas_call(
        body, name=name, grid=(t // tr, SSD_GROUPS),
        in_specs=[blk, pl.BlockSpec((tr, GROUP_W), lambda i, j: (i, zb + j)),
                  pl.BlockSpec((1, GROUP_W), lambda i, j: (0, j))],
        out_specs=blk, out_shape=_sds((t, SSD_INNER), BF16),
        compiler_params=_cp("parallel", "parallel"))(y, proj, g)


def _gnorm_bwd(dout, y, proj, g, *, name):
    t = y.shape[0]
    tr = _tile(t, 640)
    zb = OFF_Z // GROUP_W

    def body(do_ref, y_ref, z_ref, g_ref, dy_ref, dz_ref, dg_ref):
        i = pl.program_id(1)
        z = z_ref[...]
        yv = y_ref[...]
        sg = _sigmoid(z)
        sz = z * sg
        v = yv * sz
        r = lax.rsqrt(jnp.mean(v * v, axis=1, keepdims=True) + RMS_EPS)
        dov = do_ref[...].astype(F32)
        gdo = dov * g_ref[...]
        m = jnp.mean(v * gdo, axis=1, keepdims=True)
        dv = r * gdo - v * (r * r * r) * m
        dy_ref[...] = dv * sz
        dz_ref[...] = (dv * yv * (sg * (1.0 + z * (1.0 - sg)))).astype(BF16)
        pg = jnp.sum(dov * v * r, axis=0, keepdims=True)

        @pl.when(i == 0)
        def _():
            dg_ref[...] = pg

        @pl.when(i > 0)
        def _():
            dg_ref[...] += pg

    blk = pl.BlockSpec((tr, GROUP_W), lambda j, i: (i, j))
    vec = pl.BlockSpec((1, GROUP_W), lambda j, i: (0, j))
    return pl.pallas_call(
        body, name=name, grid=(SSD_GROUPS, t // tr),
        in_specs=[blk, blk, pl.BlockSpec((tr, GROUP_W), lambda j, i: (i, zb + j)), vec],
        out_specs=[blk, blk, vec],
        out_shape=[_sds((t, SSD_INNER), F32), _sds((t, SSD_INNER), BF16), _sds((1, SSD_INNER), F32)],
        compiler_params=_cp("parallel", "arbitrary"))(dout, y, proj, g)


def _mix_fwd(proj, ya, ys, *, name):
    t, d = ya.shape
    tr = _tile(t, 640)
    ba, bs = OFF_GA // d, OFF_GS // d

    def body(ga_ref, gs_ref, ya_ref, ys_ref, o_ref):
        o_ref[...] = (_sigmoid(ga_ref[...]) * ya_ref[...] + _sigmoid(gs_ref[...]) * ys_ref[...]).astype(BF16)

    blk = pl.BlockSpec((tr, d), lambda i: (i, 0))
    return pl.pallas_call(
        body, name=name, grid=(t // tr,),
        in_specs=[pl.BlockSpec((tr, d), lambda i: (i, ba)), pl.BlockSpec((tr, d), lambda i: (i, bs)), blk, blk],
        out_specs=blk, out_shape=_sds((t, d), BF16), compiler_params=_cp("parallel"))(proj, proj, ya, ys)


def _mix_bwd(dmix, proj, ya, ys, *, name):
    t, d = ya.shape
    tr = _tile(t, 640)
    ba, bs = OFF_GA // d, OFF_GS // d

    def body(dm_ref, ga_ref, gs_ref, ya_ref, ys_ref, dya_ref, dys_ref, dga_ref, dgs_ref):
        dm = dm_ref[...]
        sa = _sigmoid(ga_ref[...])
        ss = _sigmoid(gs_ref[...])
        dya_ref[...] = (sa * dm).astype(BF16)
        dys_ref[...] = (ss * dm).astype(BF16)
        dga_ref[...] = (dm * ya_ref[...] * sa * (1.0 - sa)).astype(BF16)
        dgs_ref[...] = (dm * ys_ref[...] * ss * (1.0 - ss)).astype(BF16)

    blk = pl.BlockSpec((tr, d), lambda i: (i, 0))
    return pl.pallas_call(
        body, name=name, grid=(t // tr,),
        in_specs=[blk, pl.BlockSpec((tr, d), lambda i: (i, ba)), pl.BlockSpec((tr, d), lambda i: (i, bs)), blk, blk],
        out_specs=[blk] * 4, out_shape=[_sds((t, d), BF16)] * 4,
        compiler_params=_cp("parallel"))(dmix, proj, proj, ya, ys)


def _loss_grad(h, target, *, name):
    t, d = h.shape
    tr = LANES
    assert ROW0 == tr

    def body(h_ref, t_ref, dh_ref, loss_ref):
        i = pl.program_id(0)

        @pl.when(i == 0)
        def _():
            dh_ref[...] = jnp.zeros((tr, d), F32)
            loss_ref[...] = jnp.zeros((1, LANES), F32)

        @pl.when(i > 0)
        def _():
            err = h_ref[...] - t_ref[...]
            dh_ref[...] = err * (1.0 / d)
            part = jnp.sum(jnp.sum(err * err, axis=1, keepdims=True), axis=0, keepdims=True)
            loss_ref[...] += jnp.broadcast_to(part * (0.5 / d), (1, LANES))

    blk = pl.BlockSpec((tr, d), lambda i: (i, 0))
    return pl.pallas_call(
        body, name=name, grid=(t // tr,),
        in_specs=[blk, pl.BlockSpec((tr, d), lambda i: (jnp.maximum(i - 1, 0), 0))],
        out_specs=[blk, pl.BlockSpec((1, LANES), lambda i: (0, 0))],
        out_shape=[_sds((t, d), F32), _sds((1, LANES), F32)],
        compiler_params=_cp("arbitrary"))(h, target)


def _adamw_update(gv, wv, mv, vv):
    c1 = 1.0 - ADAM_B1 ** ADAM_STEP
    c2 = 1.0 - ADAM_B2 ** ADAM_STEP
    nm = ADAM_B1 * mv + (1.0 - ADAM_B1) * gv
    nv = ADAM_B2 * vv + (1.0 - ADAM_B2) * (gv * gv)
    return -ADAM_LR * ((nm / c1) / (jnp.sqrt(nv / c2) + ADAM_EPS) + ADAM_WD * wv), nm, nv


def _as_2d(a):
    return a.reshape(1, -1) if a.ndim == 1 else a.reshape(-1, a.shape[-1])


def _adamw(g, w, m, v, *, name):
    shape = w.shape
    g2, w2, m2, v2 = (_as_2d(a) for a in (g, w, m, v))
    r, c = w2.shape
    tr = _tile(r, 256, base=8)

    def body(g_ref, w_ref, m_ref, v_ref, d_ref, nm_ref, nv_ref):
        d_ref[...], nm_ref[...], nv_ref[...] = _adamw_update(g_ref[...], w_ref[...], m_ref[...], v_ref[...])

    blk = pl.BlockSpec((tr, c), lambda i: (i, 0))
    outs = pl.pallas_call(body, name=name, grid=(r // tr,), in_specs=[blk] * 4, out_specs=[blk] * 3,
                          out_shape=[_sds((r, c), F32)] * 3, compiler_params=_cp("parallel"))(g2, w2, m2, v2)
    return [o.reshape(shape) for o in outs]


def _adamw_small(items, *, name):
    n = len(items)
    shapes = [it[1].shape for it in items]
    flat = [_as_2d(a) for it in items for a in it]

    def body(*refs):
        ins, outs = refs[:4 * n], refs[4 * n:]
        for k in range(n):
            g_ref, w_ref, m_ref, v_ref = ins[4 * k:4 * k + 4]
            d_ref, nm_ref, nv_ref = outs[3 * k:3 * k + 3]
            d_ref[...], nm_ref[...], nv_ref[...] = _adamw_update(g_ref[...], w_ref[...], m_ref[...], v_ref[...])

    out_shape = [_sds(flat[4 * k + 1].shape, F32) for k in range(n) for _ in range(3)]
    outs = pl.pallas_call(body, name=name, out_shape=out_shape,
                          compiler_params=pltpu.CompilerParams(vmem_limit_bytes=VMEM_LIMIT_BYTES))(*flat)
    return [[outs[3 * k + q].reshape(shapes[k]) for q in range(3)] for k in range(n)]


def _pair_add(g5, got, core, *, name):
    n, _, r, _ = g5.shape
    tr = _tile(r, 1024, base=8)

    def body(c_ref, a_ref, b_ref, o_ref):
        o_ref[...] = a_ref[0] + b_ref[...]

    grid_spec = pltpu.PrefetchScalarGridSpec(
        num_scalar_prefetch=1, grid=(n, r // tr),
        in_specs=[pl.BlockSpec((1, 1, tr, LANES), lambda s, i, c_ref: (s, c_ref[0], i, 0)),
                  pl.BlockSpec((1, tr, LANES), lambda s, i, c_ref: (s, i, 0))],
        out_specs=pl.BlockSpec((1, tr, LANES), lambda s, i, c_ref: (s, i, 0)))
    return pl.pallas_call(body, name=name, grid_spec=grid_spec, out_shape=_sds(got.shape, F32),
                          compiler_params=_cp("parallel", "parallel"))(core, g5, got)


def _sum_chips(q, core, *, name):
    n, r, _ = q.shape
    tr = _tile(r, 1024, base=8)

    def body(c_ref, q_ref, o_ref):
        tot = q_ref[0]
        for s in range(1, n):
            tot = tot + q_ref[s]
        o_ref[0] = tot

    grid_spec = pltpu.PrefetchScalarGridSpec(
        num_scalar_prefetch=1, grid=(r // tr,),
        in_specs=[pl.BlockSpec((n, tr, LANES), lambda i, c_ref: (0, i, 0))],
        out_specs=pl.BlockSpec((1, tr, LANES), lambda i, c_ref: (c_ref[0], i, 0)))
    return pl.pallas_call(body, name=name, grid_spec=grid_spec, out_shape=_sds((2, r, LANES), F32),
                          compiler_params=_cp("parallel"))(core, q)


_ANY = pl.BlockSpec(memory_space=pl.ANY)
_MESH = pl.DeviceIdType.MESH


def _place():
    x, y, c = lax.axis_index("x"), lax.axis_index("y"), lax.axis_index("c")
    return x, y, c, [(1 - x, y), (x, 1 - y), (1 - x, 1 - y)]


def _chip_allgather(mine, *, name):
    na = len(mine)

    def body(*refs):
        for phase in range(3):
            _chip_allgather_phase(phase, refs[:na], refs[na:2 * na], *refs[2 * na:])

    return pl.pallas_call(
        body, name=name, in_specs=[_ANY] * na, out_specs=[_ANY] * na,
        out_shape=_chip_allgather_shapes(mine), scratch_shapes=_chip_allgather_sems(na))(*mine)


def _chip_allgather_shapes(mine):
    return [_sds((N_CHIPS,) + a.shape, a.dtype) for a in mine]


def _chip_allgather_sems(na):
    return [pltpu.SemaphoreType.DMA((6 * na,)), pltpu.SemaphoreType.DMA((6 * na,)), pltpu.SemaphoreType.DMA((na,))]


def _chip_allgather_phase(phase, x_refs, o_refs, send_sems, recv_sems, local_sems):
    na = len(x_refs)
    x, y, c, chips = _place()
    k = 2 * x + y

    def copy(a, n, src, dst, to):
        return pltpu.make_async_remote_copy(src_ref=src, dst_ref=dst, send_sem=send_sems.at[6 * a + n],
                                            recv_sem=recv_sems.at[6 * a + n], device_id=to, device_id_type=_MESH)

    locals_ = [pltpu.make_async_copy(x_refs[a], o_refs[a].at[k], local_sems.at[a]) for a in range(na)]
    sends = [copy(a, n, x_refs[a].at[c], o_refs[a].at[k, c], (cx, cy, c))
             for a in range(na) for n, (cx, cy) in enumerate(chips)]
    landed = [(copy(a, n, o_refs[a].at[2 * cx + cy, c], o_refs[a].at[2 * cx + cy, c], (cx, cy, c)),
               copy(a, 3 + n, o_refs[a].at[2 * cx + cy, c], o_refs[a].at[2 * cx + cy, c], (x, y, 1 - c)))
              for a in range(na) for n, (cx, cy) in enumerate(chips)]
    if phase == 0:
        for cp in locals_ + sends:
            cp.start()
    elif phase == 1:
        for arrival, forward in landed:
            arrival.wait_recv()
            forward.start()
    else:
        for a in range(na):
            for n, (cx, cy) in enumerate(chips):
                slab = o_refs[a].at[2 * cx + cy, 1 - c]
                copy(a, 3 + n, slab, slab, (x, y, 1 - c)).wait_recv()
        for cp in sends + [forward for _, forward in landed]:
            cp.wait_send()
        for cp in locals_:
            cp.wait()


def _sibling_swap(g5, *, name):
    n, _, r, _ = g5.shape

    def body(x_ref, o_ref, send_sems, recv_sems):
        x, y, c, _ = _place()
        cps = [pltpu.make_async_remote_copy(src_ref=x_ref.at[s, 1 - c], dst_ref=o_ref.at[s], send_sem=send_sems.at[s],
                                            recv_sem=recv_sems.at[s], device_id=(x, y, 1 - c), device_id_type=_MESH)
               for s in range(n)]
        for cp in cps:
            cp.start()
        for cp in cps:
            cp.wait()

    return pl.pallas_call(
        body, name=name, in_specs=[_ANY], out_specs=_ANY, out_shape=_sds((n, r, LANES), g5.dtype),
        scratch_shapes=[pltpu.SemaphoreType.DMA((n,)), pltpu.SemaphoreType.DMA((n,))])(g5)


_CHIP_EXCHANGE_SEMS = [pltpu.SemaphoreType.DMA((3,)), pltpu.SemaphoreType.DMA((3,)), pltpu.SemaphoreType.DMA]


def _chip_exchange_copies(h_ref, q_ref, send_sems, recv_sems, local_sem):
    x, y, c, chips = _place()
    k = 2 * x + y
    local = pltpu.make_async_copy(h_ref.at[k], q_ref.at[k], local_sem)
    sends, arrivals = [], []
    for n, (cx, cy) in enumerate(chips):
        kk = 2 * cx + cy
        mk = functools.partial(pltpu.make_async_remote_copy, src_ref=h_ref.at[kk], send_sem=send_sems.at[n],
                               recv_sem=recv_sems.at[n], device_id=(cx, cy, c), device_id_type=_MESH)
        sends.append(mk(dst_ref=q_ref.at[k]))
        arrivals.append(mk(dst_ref=q_ref.at[kk]))
    return local, sends, arrivals


def _chip_exchange_start(copies):
    local, sends, _ = copies
    local.start()
    for cp in sends:
        cp.start()


def _chip_exchange_wait(copies):
    local, sends, arrivals = copies
    for cp in arrivals:
        cp.wait_recv()
    for cp in sends:
        cp.wait_send()
    local.wait()


def _chip_exchange(h, *, name):
    def body(h_ref, q_ref, send_sems, recv_sems, local_sem):
        copies = _chip_exchange_copies(h_ref, q_ref, send_sems, recv_sems, local_sem)
        _chip_exchange_start(copies)
        _chip_exchange_wait(copies)

    return pl.pallas_call(body, name=name, in_specs=[_ANY], out_specs=_ANY, out_shape=_sds(h.shape, h.dtype),
                          scratch_shapes=_CHIP_EXCHANGE_SEMS)(h)


def _sibling_allgather(buf, *, name):
    def body(x_ref, o_ref, send_sem, recv_sem):
        x, y, c, _ = _place()
        cp = pltpu.make_async_remote_copy(src_ref=x_ref.at[c], dst_ref=o_ref.at[c], send_sem=send_sem,
                                          recv_sem=recv_sem, device_id=(x, y, 1 - c), device_id_type=_MESH)
        cp.start()
        pltpu.make_async_remote_copy(src_ref=x_ref.at[c], dst_ref=o_ref.at[1 - c], send_sem=send_sem,
                                     recv_sem=recv_sem, device_id=(x, y, 1 - c), device_id_type=_MESH).wait_recv()
        cp.wait_send()

    return pl.pallas_call(
        body, name=name, in_specs=[_ANY], out_specs=_ANY, out_shape=_sds(buf.shape, buf.dtype),
        input_output_aliases={0: 0},
        scratch_shapes=[pltpu.SemaphoreType.DMA, pltpu.SemaphoreType.DMA])(buf)


def _chip_segments(g, kind):
    if kind == "col":
        n = g.shape[-1] // N_CHIPS
        s = g.reshape(g.shape[:-1] + (N_CHIPS, n))
        return jnp.moveaxis(s, -2, 0).reshape(N_CHIPS, -1)
    k = g.shape[-2] // N_CHIPS
    s = g.reshape(g.shape[:-2] + (N_CHIPS, k, g.shape[-1]))
    return jnp.moveaxis(s, -3, 0).reshape(N_CHIPS, -1)


def _join(blocks, kind):
    if kind == "col":
        s = jnp.moveaxis(blocks, 0, -2)
        return s.reshape(s.shape[:-2] + (s.shape[-2] * s.shape[-1],))
    return blocks.reshape((blocks.shape[0] * blocks.shape[1],) + blocks.shape[2:])


def _travel_form(local, names):
    mine = []
    for n in names:
        a = local[n]
        if n == "w_in":
            a = jnp.swapaxes(a, 1, 2)
        if n == "meta_tokens":
            a = a.reshape(2, N_META // 2, a.shape[-1])
        mine.append(a.astype(BF16) if n in GATHER_BF16 else a)
    return mine


def _rope_tables(t):
    half = QK_ROPE // 2
    inv_freq = 1.0 / (ROPE_THETA ** (jnp.arange(0, QK_ROPE, 2, dtype=F32) / QK_ROPE))
    pos = jnp.maximum(jnp.arange(t, dtype=F32) - PAD, 0.0)
    ang = pos[:, None] * inv_freq[None, :]
    cos, sin = jnp.cos(ang), jnp.sin(ang)
    z = jnp.zeros((t, half), F32)
    z2 = jnp.zeros((t, LANES - QK_ROPE), F32)
    return (jnp.concatenate([cos, cos, z2], axis=1), jnp.concatenate([-sin, z, z2], axis=1),
            jnp.concatenate([z, sin, z2], axis=1))


def _expand_matrix():
    lane = np.arange(SSD_INNER) // SSD_HEAD_DIM
    e = (np.arange(LANES)[:, None] == lane[None, :]).astype(np.float32)
    return jnp.asarray(e, BF16)


def _late_weights(full, i):
    w = {}
    kinds = dict(SHARDED)
    whole = lambda n: _join(full[n][:, i], kinds[n])
    for n in ("w_o_attn", "w_o_ssd", "w_out", "w_down"):
        w[n] = whole(n)
    w_up = whole("w_up")
    w["w_up_g"] = w_up[:, :D_FF]
    w["w_up_v"] = w_up[:, D_FF:]
    ffn_w = whole("ffn_conv_w")
    w["ffn_conv_wg"] = ffn_w[:, :D_FF]
    w["ffn_conv_wv"] = ffn_w[:, D_FF:]
    return w


def _early_weights(full, rep, i):
    w = {}
    kinds = dict(SHARDED)
    whole = lambda n: _join(full[n][:, i], kinds[n])
    wt = _join(full["w_in"][:, i], "row")
    zr = lambda n: jnp.zeros((n, D_MODEL), BF16)
    w["w_in_t"] = jnp.concatenate(
        [wt[0:1024], wt[1088:3136], wt[3136:6208], wt[6240:7264], wt[7264:8288],
         wt[1024:1088], zr(LANES - QK_ROPE), wt[6208:6240], zr(LANES - SSD_HEADS)], axis=0)
    wq = whole("w_q_b").reshape(Q_LORA, MLA_HEADS, QK_NOPE + QK_ROPE)
    w["w_q_b"] = jnp.pad(wq, ((0, 0), (0, 0), (0, QHEAD - QK_NOPE - QK_ROPE))).reshape(Q_LORA, MLA_HEADS * QHEAD)
    wkv = whole("w_kv_b").reshape(KV_LORA, MLA_HEADS, 2, QK_NOPE)
    w["w_kv_b"] = jnp.swapaxes(wkv, 1, 2).reshape(KV_LORA, 2 * MLA_HEADS * QK_NOPE)
    w["ssd_conv_w"] = whole("ssd_conv_w")
    row = lambda v: v.reshape(1, -1)
    w["q_norm_g"] = row(rep["q_norm_g"][i])
    w["kv_norm_g"] = row(rep["kv_norm_g"][i])
    w["ssd_conv_b"] = row(rep["ssd_conv_b"][i])
    w["dt_bias"] = row(jnp.pad(rep["dt_bias"][i], (0, LANES - SSD_HEADS)))
    a = -jnp.exp(rep["a_log"][i])
    w["a"] = a
    w["a_x"] = row(jnp.repeat(a, SSD_HEAD_DIM))
    w["d_x"] = row(jnp.repeat(rep["d_skip"][i], SSD_HEAD_DIM))
    w["ssd_norm_g"] = row(rep["ssd_norm_g"][i])
    w["ffn_conv_bg"] = row(rep["ffn_conv_b"][i][:D_FF])
    w["ffn_conv_bv"] = row(rep["ffn_conv_b"][i][D_FF:])
    for n in ("ln1_g", "ln1_b", "ln2_g", "ln2_b"):
        w[n] = row(rep[n][i])
    return w


def _layer_fwd(h, hb, w, tabs, expand, layer, late):
    tag = f"l{layer}_"
    cos, sin_a, sin_b = tabs
    s = {"h": h, "hb": hb}
    proj = _mm(hb, w["w_in_t"], trans_b=True, name=tag + "in_proj", tn=768)
    s["proj"] = proj
    qn = _rms_fwd(proj, OFF_Q, Q_LORA, w["q_norm_g"], name=tag + "q_norm")
    q_raw = _mm(qn, w["w_q_b"], name=tag + "q_up")
    q = _rope_q_fwd(q_raw, cos, sin_a, sin_b, name=tag + "q_rope")
    kvn = _rms_fwd(proj, OFF_KV, KV_LORA, w["kv_norm_g"], name=tag + "kv_norm")
    kv = _mm(kvn, w["w_kv_b"], name=tag + "kv_up", out_dtype=BF16)
    kpe = _rope_k_fwd(proj, cos, sin_a, sin_b, name=tag + "k_rope")
    if isinstance(late, dict):
        o, lse = _flash_fwd(q, kv, kpe, name=tag + "attn")
    else:
        o, lse, *got = _flash_fwd(q, kv, kpe, name=tag + "attn", gather=late)
        late = dict(zip(GATHER_LATE, got))
    w.update(_late_weights(late, layer))
    ya = _mm(o, w["w_o_attn"], name=tag + "attn_out")
    s.update(qn=qn, q=q, kvn=kvn, kv=kv, kpe=kpe, o=o, lse=lse, ya=ya)
    xs, bm, cm, dtx = _ssd_prep_fwd(proj, w["ssd_conv_w"], w["ssd_conv_b"], w["dt_bias"], expand, name=tag + "ssd_prep")
    y, prev = _ssd_fwd(xs, dtx, bm, cm, bm.T, w["a_x"], w["d_x"], name=tag + "ssd_scan")
    yn = _gnorm_fwd(y, proj, w["ssd_norm_g"], name=tag + "ssd_norm")
    ys = _mm(yn, w["w_o_ssd"], name=tag + "ssd_out")
    s.update(xs=xs, bm=bm, cm=cm, dtx=dtx, y=y, prev=prev, yn=yn, ys=ys)
    mixed = _mix_fwd(proj, ya, ys, name=tag + "mix")
    br = _mm(mixed, w["w_out"], name=tag + "mix_out")
    pre1, h1, h1b = _ln_fwd(h, br, w["ln1_g"], w["ln1_b"], name=tag + "ln1")
    s.update(mixed=mixed, pre1=pre1, h1b=h1b)
    ug = _mm(h1b, w["w_up_g"], name=tag + "up_g", tn=1408)
    uv = _mm(h1b, w["w_up_v"], name=tag + "up_v", tn=1408)
    act = _ffn_act_fwd(ug, uv, w["ffn_conv_wg"], w["ffn_conv_wv"], w["ffn_conv_bg"], w["ffn_conv_bv"],
                       name=tag + "ffn_act")
    ffn = _mm(act, w["w_down"], name=tag + "down")
    pre2, h2, h2b = _ln_fwd(h1, ffn, w["ln2_g"], w["ln2_b"], name=tag + "ln2")
    s.update(ug=ug, uv=uv, act=act, pre2=pre2)
    return h2, h2b, s, late


def _layer_bwd(dh2, w, s, tabs, reduce_m, tag, begin_exchange=None):
    cos, sin_a, sin_b = tabs
    g = {}
    proj = s["proj"]
    dpre2, dpre2b, g["ln2_g"], g["ln2_b"] = _ln_bwd(dh2, s["pre2"], w["ln2_g"], name=tag + "ln2_bwd")
    g["w_down"] = _mm_tn(s["act"], dpre2b, name=tag + "down_dw")
    dact = _mm(dpre2b, w["w_down"], trans_b=True, name=tag + "down_dx", tn=1408)
    dcg, dcv, dwg, dwv, dbg, dbv = _ffn_act_bwd(s["ug"], s["uv"], dact, w["ffn_conv_wg"], w["ffn_conv_wv"],
                                                w["ffn_conv_bg"], w["ffn_conv_bv"], name=tag + "ffn_act_bwd")
    g["ffn_conv_w"] = jnp.concatenate([dwg, dwv], axis=1)
    g["ffn_conv_b"] = jnp.concatenate([dbg, dbv], axis=1).reshape(-1)
    dug = _conv_bwd_input(dcg, w["ffn_conv_wg"], FFN_CONV, name=tag + "ffn_conv_bwd_g", tc=1408)
    duv = _conv_bwd_input(dcv, w["ffn_conv_wv"], FFN_CONV, name=tag + "ffn_conv_bwd_v", tc=1408)
    g["w_up"] = jnp.concatenate([_mm_tn(dug, s["h1b"], name=tag + "up_g_dw"),
                                 _mm_tn(duv, s["h1b"], name=tag + "up_v_dw")], axis=0)
    dh1 = _mm(dug, w["w_up_g"], trans_b=True, add=dpre2, add_scale=ALPHA, name=tag + "up_g_dx")
    dh1 = _mm(duv, w["w_up_v"], trans_b=True, add=dh1, name=tag + "up_v_dx")
    dpre1, dpre1b, g["ln1_g"], g["ln1_b"] = _ln_bwd(dh1, s["pre1"], w["ln1_g"], name=tag + "ln1_bwd")
    g["w_out"] = _mm_tn(s["mixed"], dpre1b, name=tag + "mix_out_dw")
    dmix = _mm(dpre1b, w["w_out"], trans_b=True, name=tag + "mix_out_dx")
    dya, dys, dga, dgs = _mix_bwd(dmix, proj, s["ya"], s["ys"], name=tag + "mix_bwd")
    g["w_o_ssd"] = _mm_tn(s["yn"], dys, name=tag + "ssd_out_dw")
    dyn = _mm(dys, w["w_o_ssd"], trans_b=True, out_dtype=BF16, name=tag + "ssd_out_dx")
    dy, dz, dgn = _gnorm_bwd(dyn, s["y"], proj, w["ssd_norm_g"], name=tag + "ssd_norm_bwd")
    g["ssd_norm_g"] = dgn.reshape(-1)
    dxs, ddtx, dbm, dcm, da_x, dd_x = _ssd_bwd(s["xs"], s["dtx"], s["bm"], s["cm"], s["cm"].T, s["prev"], dy,
                                               w["a_x"], w["d_x"], name=tag + "ssd_scan_bwd")
    g["a_log"] = da_x.reshape(SSD_HEADS, SSD_HEAD_DIM).sum(axis=1) * w["a"]
    g["d_skip"] = dd_x.reshape(SSD_HEADS, SSD_HEAD_DIM).sum(axis=1)
    dconv, ddtr, dcw, dcb, ddtb = _ssd_prep_bwd_a(proj, dxs, dbm, dcm, ddtx, w["ssd_conv_w"], w["ssd_conv_b"],
                                                  w["dt_bias"], reduce_m, name=tag + "ssd_prep_bwd")
    g["ssd_conv_w"] = dcw
    g["ssd_conv_b"] = dcb.reshape(-1)
    g["dt_bias"] = ddtb.reshape(-1)[:SSD_HEADS]
    dxbc = _conv_bwd_input(dconv, w["ssd_conv_w"], SSD_CONV, name=tag + "ssd_conv_bwd", tc=1024)
    g["w_o_attn"] = _mm_tn(s["o"], dya, name=tag + "attn_out_dw")
    exchange = begin_exchange(dict(g)) if begin_exchange else None
    do = _mm(dya, w["w_o_attn"], trans_b=True, out_dtype=BF16, name=tag + "attn_out_dx")
    delta = _attn_delta(do, s["o"], name=tag + "attn_delta")
    by_tile = lambda a: a.reshape(MLA_HEADS, -1, _tile(a.shape[1], 640))
    dq, dkn, dkp, dv, *exchanged = _flash_bwd(s["q"], s["kv"], s["kpe"], do, by_tile(s["lse"]), by_tile(delta),
                                              name=tag + "attn_bwd", exchange=exchange)
    dq_raw = _rope_q_bwd(dq, cos, sin_a, sin_b, name=tag + "q_rope_bwd")
    dwq = _mm_tn(s["qn"], dq_raw, name=tag + "q_up_dw")
    g["w_q_b"] = dwq.reshape(Q_LORA, MLA_HEADS, QHEAD)[:, :, :QK_NOPE + QK_ROPE].reshape(Q_LORA, -1)
    dqn = _mm(dq_raw, w["w_q_b"], trans_b=True, out_dtype=BF16, name=tag + "q_up_dx")
    dqlat, dgq = _rms_bwd(dqn, proj, OFF_Q, Q_LORA, w["q_norm_g"], name=tag + "q_norm_bwd")
    g["q_norm_g"] = dgq.reshape(-1)
    dkv = jnp.concatenate([dkn, dv], axis=1)
    dwkv = _mm_tn(s["kvn"], dkv, name=tag + "kv_up_dw")
    g["w_kv_b"] = jnp.swapaxes(dwkv.reshape(KV_LORA, 2, MLA_HEADS, QK_NOPE), 1, 2).reshape(KV_LORA, -1)
    dkvn = _mm(dkv, w["w_kv_b"], trans_b=True, out_dtype=BF16, name=tag + "kv_up_dx")
    dkvlat, dgkv = _rms_bwd(dkvn, proj, OFF_KV, KV_LORA, w["kv_norm_g"], name=tag + "kv_norm_bwd")
    g["kv_norm_g"] = dgkv.reshape(-1)
    dkpe = _rope_k_bwd(dkp, cos, sin_a, sin_b, name=tag + "k_rope_bwd")
    h = s["hb"]
    comps = ((dqlat, OFF_Q), (dkvlat, OFF_KV), (dz, OFF_Z), (dxbc, OFF_XBC), (dga, OFF_GA), (dgs, OFF_GS),
             (dkpe, OFF_KPE), (ddtr, OFF_DT))
    dws = {off: _mm_tn(dc, h, name=f"{tag}in_dw{n}") for n, (dc, off) in enumerate(comps)}
    with_w = lambda group: [(dc, w["w_in_t"][off:off + dc.shape[1]]) for dc, off in group]
    wide = [c for c in comps if c[1] in (OFF_Z, OFF_XBC)]
    rest = [c for c in comps if c[1] not in (OFF_Z, OFF_XBC)]
    dh = _mm_sum(with_w(wide), dpre1, add_scale=ALPHA, name=tag + "in_dx_wide")
    dh = _mm_sum(with_w(rest), dh, name=tag + "in_dx_rest")
    g["w_in"] = jnp.concatenate([dws[OFF_Q], dws[OFF_KV], dws[OFF_KPE][:QK_ROPE], dws[OFF_Z], dws[OFF_XBC],
                                 dws[OFF_DT][:SSD_HEADS], dws[OFF_GA], dws[OFF_GS]], axis=0)
    return dh, g, (exchanged[0] if exchanged else None)


def _local_step(x, target, full, late, rep):
    seq = x.shape[0]
    t = seq + ROW0
    tabs = _rope_tables(t)
    expand = _expand_matrix()
    reduce_m = expand.T
    meta = _join(full["meta_tokens"].reshape(N_CHIPS, N_META, -1), "col")
    xin = jnp.concatenate([jnp.zeros((PAD, D_MODEL), F32), meta, x], axis=0)
    row = lambda v: v.reshape(1, -1)
    _, h, hb = _ln_fwd(xin, None, row(rep["emb_ln_g"]), row(rep["emb_ln_b"]), name="emb_ln")
    ws, saved = [], []
    for i in range(DEPTH):
        w = _early_weights(full, rep, i)
        h, hb, s, late = _layer_fwd(h, hb, w, tabs, expand, i, late)
        ws.append(w)
        saved.append(s)
    dh, loss = _loss_grad(h, target, name="loss")
    dh, g1, _ = _layer_bwd(dh, ws[1], saved[1], tabs, reduce_m, "l1_")
    layouts = []

    def begin_exchange(g0_so_far):
        g4, layout = _pack_part([((1, n), g) for n, g in g1.items()] + [((0, n), g) for n, g in g0_so_far.items()])
        layouts.append(layout)
        return _reduce_begin(g4, "reduce1_")

    dh, g0, parts1 = _layer_bwd(dh, ws[0], saved[0], tabs, reduce_m, "l0_", begin_exchange=begin_exchange)
    dxin, _, dg, db = _ln_bwd(dh, xin, row(rep["emb_ln_g"]), name="emb_ln_bwd")
    rest = [((0, n), g0[n]) for n in LATE_GRADS]
    rest += [((0, "emb_ln_g"), dg), ((0, "emb_ln_b"), db), ((0, "meta_tokens"), dxin[PAD:ROW0])]
    return loss, dxin[ROW0:], rest, parts1, layouts[0]


def _pack_part(entries):
    kinds = dict(SHARDED)
    pieces, layout, tail, row0 = [], [], [], 0

    def add(key, piece):
        nonlocal row0
        pieces.append(piece)
        layout.append((key, row0, piece.shape[1]))
        row0 += piece.shape[1]

    for key, g in entries:
        if key[1] in kinds:
            kind = "row" if key[1] in TRANSPOSED_GRADS else kinds[key[1]]
            add(key, _chip_segments(g, kind).reshape(N_CHIPS, -1, LANES))
    for key, g in entries:
        if key[1] not in kinds and g.size % LANES == 0:
            add(key, jnp.broadcast_to(g.reshape(1, -1, LANES), (N_CHIPS, g.size // LANES, LANES)))
        elif key[1] not in kinds:
            tail.append((key, g.reshape(-1)))
    if tail:
        vec = jnp.concatenate([g for _, g in tail])
        vec = jnp.pad(vec, (0, -vec.shape[0] % LANES)).reshape(1, -1, LANES)
        add(("tail", tuple((key, g.shape[0]) for key, g in tail)), jnp.broadcast_to(vec, (N_CHIPS,) + vec.shape[1:]))
    rows = -(-row0 // REDUCE_ROW_ALIGN) * REDUCE_ROW_ALIGN
    pieces.append(jnp.zeros((N_CHIPS, rows - row0, LANES), F32))
    return jnp.concatenate(pieces, axis=1), layout


def _core_index():
    return lax.axis_index("c").astype(jnp.int32).reshape(1)


def _reduce_begin(g4, tag):
    n, r, _ = g4.shape
    g5 = g4.reshape(n, 2, r // 2, LANES)
    got = _sibling_swap(g5, name=tag + "pair_swap")
    return _pair_add(g5, got, _core_index(), name=tag + "pair_add")


def _reduce_end(parts, tag):
    half = _sum_chips(parts, _core_index(), name=tag + "chip_sum")
    both = _sibling_allgather(half, name=tag + "pair_gather")
    return both.reshape(-1, LANES)


def _unpack_part(flat, layout, shapes):
    out = {}
    for key, row0, rows in layout:
        piece = flat[row0:row0 + rows]
        if key[0] == "tail":
            vec, off = piece.reshape(-1), 0
            for sub, size in key[1]:
                out[sub] = vec[off:off + size]
                off += size
        elif key[1] in TRANSPOSED_GRADS:
            out[key] = piece.reshape(shapes[key[1]][1], shapes[key[1]][0]).T
        else:
            out[key] = piece.reshape(shapes[key[1]])
    return out


def kernel(x, meta_tokens, emb_ln_g, emb_ln_b, w_in, q_norm_g, w_q_b, kv_norm_g, w_kv_b, w_o_attn, ssd_conv_w, ssd_conv_b, dt_bias, a_log, d_skip, ssd_norm_g, w_o_ssd, w_out, ln1_g, ln1_b, w_up, ffn_conv_w, ffn_conv_b, w_down, ln2_g, ln2_b, loss_target, m_meta_tokens, m_emb_ln_g, m_emb_ln_b, m_w_in, m_q_norm_g, m_w_q_b, m_kv_norm_g, m_w_kv_b, m_w_o_attn, m_ssd_conv_w, m_ssd_conv_b, m_dt_bias, m_a_log, m_d_skip, m_ssd_norm_g, m_w_o_ssd, m_w_out, m_ln1_g, m_ln1_b, m_w_up, m_ffn_conv_w, m_ffn_conv_b, m_w_down, m_ln2_g, m_ln2_b, v_meta_tokens, v_emb_ln_g, v_emb_ln_b, v_w_in, v_q_norm_g, v_w_q_b, v_kv_norm_g, v_w_kv_b, v_w_o_attn, v_ssd_conv_w, v_ssd_conv_b, v_dt_bias, v_a_log, v_d_skip, v_ssd_norm_g, v_w_o_ssd, v_w_out, v_ln1_g, v_ln1_b, v_w_up, v_ffn_conv_w, v_ffn_conv_b, v_w_down, v_ln2_g, v_ln2_b):
    given = dict(locals())
    local_w = {n: given[n] for n in WEIGHTS}
    local_m = {n: given["m_" + n] for n in WEIGHTS}
    local_v = {n: given["v_" + n] for n in WEIGHTS}
    full = dict(zip(GATHER_EARLY, _chip_allgather(_travel_form(local_w, GATHER_EARLY), name="gather_early")))
    rep = {n: local_w[n] for n in REPLICATED}
    loss, grad_x, rest, parts1, layout1 = _local_step(x[0], loss_target[0], full, _travel_form(local_w, GATHER_LATE), rep)
    g4, layout0 = _pack_part(rest)
    parts0 = _chip_exchange(_reduce_begin(g4, "reduce0_"), name="reduce0_chip_exchange")
    shapes = {n: (local_w[n].shape if n in OUTSIDE else local_w[n].shape[1:]) for n in WEIGHTS}
    summed = _unpack_part(_reduce_end(parts0, "reduce0_"), layout0, shapes)
    summed.update(_unpack_part(_reduce_end(parts1, "reduce1_"), layout1, shapes))
    grad = {n: (summed[0, n] if n in OUTSIDE else jnp.stack([summed[i, n] for i in range(DEPTH)])) for n in WEIGHTS}
    upd = {}
    small = [n for n in WEIGHTS if n not in GATHER_BF16]
    for n in GATHER_BF16:
        upd[n] = _adamw(grad[n], local_w[n], local_m[n], local_v[n], name="adamw_" + n)
    res = _adamw_small([(grad[n], local_w[n], local_m[n], local_v[n]) for n in small], name="adamw_small")
    upd.update(zip(small, res))
    total = lax.psum(loss[0, 0], ("x", "y", "c"))
    outs = [total, grad_x[None]] + [grad[n] for n in WEIGHTS]
    for q in range(3):
        outs.extend(upd[n][q] for n in WEIGHTS)
    return tuple(outs)
```

```python
import functools
import math

import numpy as np
import jax
import jax.numpy as jnp
from jax import lax
from jax.experimental import pallas as pl
from jax.experimental.pallas import tpu as pltpu

F32 = jnp.float32
BF16 = jnp.bfloat16

D_MODEL = 1024
N_META = 16
DEPTH = 2
MLA_HEADS = 8
Q_LORA = 768
KV_LORA = 256
QK_NOPE = 128
QK_ROPE = 64
V_HEAD = 128
ROPE_THETA = 10000.0
NEG_INF = -1e30
PAD_KEY_SCORE = -1e30
SSD_INNER = 2048
SSD_HEAD_DIM = 64
SSD_HEADS = 32
SSD_GROUPS = 4
SSD_STATE = 128
SSD_CONV = 4
SSD_CONV_DIM = 3072
CHUNK = 128
D_FF = 2816
FFN_CONV = 3
LN_EPS = 1e-5
RMS_EPS = 1e-6
ALPHA = (2 * DEPTH) ** 0.25
ATTN_SCALE = (QK_NOPE + QK_ROPE) ** -0.5
LOG2E = math.log2(math.e)
LN2 = math.log(2.0)
Q_SCALE = ATTN_SCALE * LOG2E
ADAM_LR = 0.001
ADAM_B1 = 0.9
ADAM_B2 = 0.999
ADAM_EPS = 1e-08
ADAM_WD = 0.01
ADAM_STEP = 10

LANES = 128
PAD = 112
ROW0 = PAD + N_META
QHEAD = 256
GROUP_W = SSD_INNER // SSD_GROUPS
HALO = 8
VMEM_LIMIT_BYTES = 56 * 1024 * 1024
MM_VMEM_BUDGET = 46 * 1024 * 1024
MM_MAX_ROW_TILE = 1664
N_CHIPS = 4

OFF_Q, OFF_KV, OFF_Z, OFF_XBC, OFF_GA, OFF_GS, OFF_KPE, OFF_DT = 0, 768, 1024, 3072, 6144, 7168, 8192, 8320
IN_COLS_P = 8448

NT_DIMS = (((1,), (1,)), ((), ()))
NN_DIMS = (((1,), (0,)), ((), ()))
TN_DIMS = (((0,), (0,)), ((), ()))

SHARDED = (("meta_tokens", "col"), ("w_in", "col"), ("w_q_b", "col"), ("w_kv_b", "col"), ("w_o_attn", "row"),
           ("ssd_conv_w", "col"), ("w_o_ssd", "row"), ("w_out", "row"), ("w_up", "col"), ("ffn_conv_w", "col"),
           ("w_down", "row"))
REPLICATED = ("emb_ln_g", "emb_ln_b", "q_norm_g", "kv_norm_g", "ssd_conv_b", "dt_bias", "a_log", "d_skip",
              "ssd_norm_g", "ln1_g", "ln1_b", "ffn_conv_b", "ln2_g", "ln2_b")
WEIGHTS = ("meta_tokens", "emb_ln_g", "emb_ln_b", "w_in", "q_norm_g", "w_q_b", "kv_norm_g", "w_kv_b", "w_o_attn",
           "ssd_conv_w", "ssd_conv_b", "dt_bias", "a_log", "d_skip", "ssd_norm_g", "w_o_ssd", "w_out", "ln1_g",
           "ln1_b", "w_up", "ffn_conv_w", "ffn_conv_b", "w_down", "ln2_g", "ln2_b")
GATHER_BF16 = ("w_in", "w_q_b", "w_kv_b", "w_o_attn", "w_o_ssd", "w_out", "w_up", "w_down")
GATHER_EARLY = ("meta_tokens", "w_in", "w_q_b", "w_kv_b", "ssd_conv_w")
GATHER_LATE = ("w_o_attn", "w_o_ssd", "w_out", "w_up", "ffn_conv_w", "w_down")
OUTSIDE = ("meta_tokens", "emb_ln_g", "emb_ln_b")
LATE_GRADS = ("w_in", "w_q_b", "q_norm_g", "w_kv_b", "kv_norm_g")
TRANSPOSED_GRADS = ("w_in", "w_up")
REDUCE_ROW_ALIGN = 512


def _tile(n, target, base=LANES):
    best = None
    d = base
    while d <= min(n, target):
        if n % d == 0:
            best = d
        d += base
    return n if best is None else best


def _cp(*sem):
    return pltpu.CompilerParams(dimension_semantics=sem, vmem_limit_bytes=VMEM_LIMIT_BYTES)


def _sds(shape, dtype):
    return jax.ShapeDtypeStruct(shape, dtype)


def _row_ids(i, tr, shape):
    return i * tr + lax.broadcasted_iota(jnp.int32, shape, 0)


def _sigmoid(x):
    return 1.0 / (1.0 + jnp.exp(-x))


def _mm_tiles(m, n, tn_max, tk, nk, a_bytes, b_bytes, out_bytes, add_bytes):
    divisors = lambda size, cap: [d for d in range(min(size, cap) // LANES * LANES, 0, -LANES) if size % d == 0]
    for tm in divisors(m, MM_MAX_ROW_TILE):
        for tn in divisors(n, tn_max):
            blocks = 2 * (tm * tk * a_bytes + tk * tn * b_bytes + tm * tn * (out_bytes + add_bytes))
            temps = tm * tn * 4 * (2 if nk > 1 else 1) + tm * tk * 2 + tk * tn * 2
            if blocks + temps <= MM_VMEM_BUDGET:
                return tm, tn
    return LANES, LANES


def _mm(a, b, *, name, trans_b=False, out_dtype=F32, add=None, add_scale=1.0, tn=1024, tk=1408):
    m, k_dim = a.shape
    n = b.shape[0] if trans_b else b.shape[1]
    tk = _tile(k_dim, tk)
    nk = k_dim // tk
    has_add = add is not None
    tm, tn = _mm_tiles(m, n, tn, tk, nk, a.dtype.itemsize, b.dtype.itemsize, jnp.dtype(out_dtype).itemsize,
                       add.dtype.itemsize if has_add else 0)
    dims = NT_DIMS if trans_b else NN_DIMS

    def body(*refs):
        a_ref, b_ref = refs[0], refs[1]
        r_ref = refs[2] if has_add else None
        o_ref = refs[3] if has_add else refs[2]
        part = lax.dot_general(a_ref[...].astype(BF16), b_ref[...].astype(BF16), dims, preferred_element_type=F32)

        def finish(r):
            if has_add:
                r = r + add_scale * r_ref[...].astype(F32)
            o_ref[...] = r.astype(out_dtype)

        if nk == 1:
            finish(part)
        else:
            acc = refs[-1]
            kk = pl.program_id(2)

            @pl.when(kk == 0)
            def _():
                acc[...] = part

            @pl.when(kk > 0)
            def _():
                acc[...] += part

            @pl.when(kk == nk - 1)
            def _():
                finish(acc[...])

    in_specs = [pl.BlockSpec((tm, tk), lambda i, j, kk: (i, kk)),
                pl.BlockSpec((tn, tk), lambda i, j, kk: (j, kk)) if trans_b
                else pl.BlockSpec((tk, tn), lambda i, j, kk: (kk, j))]
    args = [a, b]
    if has_add:
        in_specs.append(pl.BlockSpec((tm, tn), lambda i, j, kk: (i, j)))
        args.append(add)
    return pl.pallas_call(
        body, name=name, grid=(m // tm, n // tn, nk), in_specs=in_specs,
        out_specs=pl.BlockSpec((tm, tn), lambda i, j, kk: (i, j)),
        out_shape=_sds((m, n), out_dtype),
        scratch_shapes=[pltpu.VMEM((tm, tn), F32)] if nk > 1 else [],
        compiler_params=_cp("parallel", "parallel", "arbitrary"),
    )(*args)


def _mm_sum(pairs, add, *, name, add_scale=1.0, tm=640):
    m, n = add.shape
    tm = _tile(m, tm)
    npairs = len(pairs)

    def body(*refs):
        a_refs, b_refs = refs[:npairs], refs[npairs:2 * npairs]
        r_ref, o_ref = refs[2 * npairs], refs[2 * npairs + 1]
        acc = add_scale * r_ref[...]
        for a_ref, b_ref in zip(a_refs, b_refs):
            acc = acc + jnp.dot(a_ref[...].astype(BF16), b_ref[...].astype(BF16), preferred_element_type=F32)
        o_ref[...] = acc

    in_specs = ([pl.BlockSpec((tm, a.shape[1]), lambda i: (i, 0)) for a, _ in pairs]
                + [pl.BlockSpec(b.shape, lambda i: (0, 0)) for _, b in pairs]
                + [pl.BlockSpec((tm, n), lambda i: (i, 0))])
    return pl.pallas_call(
        body, name=name, grid=(m // tm,), in_specs=in_specs, out_specs=pl.BlockSpec((tm, n), lambda i: (i, 0)),
        out_shape=_sds((m, n), F32), compiler_params=_cp("parallel"),
    )(*[a for a, _ in pairs], *[b for _, b in pairs], add)


def _mm_tn(a, b, *, name, tko=1408, tn=1024, tt=640):
    t, k_dim = a.shape
    n = b.shape[1]
    tko, tn, tt = _tile(k_dim, tko), _tile(n, tn), _tile(t, tt)

    def body(a_ref, b_ref, o_ref):
        part = lax.dot_general(a_ref[...].astype(BF16), b_ref[...].astype(BF16), TN_DIMS, preferred_element_type=F32)
        tt_i = pl.program_id(2)

        @pl.when(tt_i == 0)
        def _():
            o_ref[...] = part

        @pl.when(tt_i > 0)
        def _():
            o_ref[...] += part

    return pl.pallas_call(
        body, name=name, grid=(k_dim // tko, n // tn, t // tt),
        in_specs=[pl.BlockSpec((tt, tko), lambda i, j, s: (s, i)), pl.BlockSpec((tt, tn), lambda i, j, s: (s, j))],
        out_specs=pl.BlockSpec((tko, tn), lambda i, j, s: (i, j)),
        out_shape=_sds((k_dim, n), F32),
        compiler_params=_cp("parallel", "parallel", "arbitrary"),
    )(a, b)


def _ln_fwd(h, branch, g, b, *, name):
    t, d = h.shape
    tr = _tile(t, 640)
    has_branch = branch is not None

    def body(*refs):
        if has_branch:
            h_ref, br_ref, g_ref, b_ref, pre_ref, o_ref, ob_ref = refs
            pre = ALPHA * h_ref[...] + br_ref[...]
            pre_ref[...] = pre
        else:
            h_ref, g_ref, b_ref, o_ref, ob_ref = refs
            pre = h_ref[...]
        mu = jnp.mean(pre, axis=1, keepdims=True)
        xc = pre - mu
        var = jnp.mean(xc * xc, axis=1, keepdims=True)
        y = xc * lax.rsqrt(var + LN_EPS) * g_ref[...] + b_ref[...]
        rows = _row_ids(pl.program_id(0), tr, (tr, 1))
        y = jnp.where(rows >= PAD, y, 0.0)
        o_ref[...] = y
        ob_ref[...] = y.astype(BF16)

    row_spec = pl.BlockSpec((tr, d), lambda i: (i, 0))
    vec_spec = pl.BlockSpec((1, d), lambda i: (0, 0))
    if has_branch:
        return pl.pallas_call(
            body, name=name, grid=(t // tr,), in_specs=[row_spec, row_spec, vec_spec, vec_spec],
            out_specs=[row_spec] * 3, out_shape=[_sds((t, d), F32), _sds((t, d), F32), _sds((t, d), BF16)],
            compiler_params=_cp("parallel"))(h, branch, g, b)
    out, out_b = pl.pallas_call(
        body, name=name, grid=(t // tr,), in_specs=[row_spec, vec_spec, vec_spec],
        out_specs=[row_spec] * 2, out_shape=[_sds((t, d), F32), _sds((t, d), BF16)],
        compiler_params=_cp("parallel"))(h, g, b)
    return h, out, out_b


def _ln_bwd(dy, pre, g, *, name):
    t, d = pre.shape
    tr = _tile(t, 640)

    def body(dy_ref, pre_ref, g_ref, dpre_ref, dpre_b_ref, dg_ref, db_ref):
        i = pl.program_id(0)
        pre_v = pre_ref[...]
        mu = jnp.mean(pre_v, axis=1, keepdims=True)
        xc = pre_v - mu
        var = jnp.mean(xc * xc, axis=1, keepdims=True)
        rstd = lax.rsqrt(var + LN_EPS)
        xhat = xc * rstd
        rows = _row_ids(i, tr, (tr, 1))
        dym = jnp.where(rows >= PAD, dy_ref[...], 0.0)
        gdy = dym * g_ref[...]
        m1 = jnp.mean(gdy, axis=1, keepdims=True)
        m2 = jnp.mean(gdy * xhat, axis=1, keepdims=True)
        dpre = rstd * (gdy - m1 - xhat * m2)
        dpre_ref[...] = dpre
        dpre_b_ref[...] = dpre.astype(BF16)
        pg = jnp.sum(dym * xhat, axis=0, keepdims=True)
        pb = jnp.sum(dym, axis=0, keepdims=True)

        @pl.when(i == 0)
        def _():
            dg_ref[...] = pg
            db_ref[...] = pb

        @pl.when(i > 0)
        def _():
            dg_ref[...] += pg
            db_ref[...] += pb

    row_spec = pl.BlockSpec((tr, d), lambda i: (i, 0))
    vec_spec = pl.BlockSpec((1, d), lambda i: (0, 0))
    return pl.pallas_call(
        body, name=name, grid=(t // tr,), in_specs=[row_spec, row_spec, vec_spec],
        out_specs=[row_spec, row_spec, vec_spec, vec_spec],
        out_shape=[_sds((t, d), F32), _sds((t, d), BF16), _sds((1, d), F32), _sds((1, d), F32)],
        compiler_params=_cp("arbitrary"))(dy, pre, g)


def _rms_fwd(proj, col_off, width, g, *, name):
    t = proj.shape[0]
    tr = _tile(t, 640)
    cb = col_off // width

    def body(x_ref, g_ref, o_ref):
        x = x_ref[...]
        r = lax.rsqrt(jnp.mean(x * x, axis=1, keepdims=True) + RMS_EPS)
        o_ref[...] = (x * r * g_ref[...]).astype(BF16)

    return pl.pallas_call(
        body, name=name, grid=(t // tr,),
        in_specs=[pl.BlockSpec((tr, width), lambda i: (i, cb)), pl.BlockSpec((1, width), lambda i: (0, 0))],
        out_specs=pl.BlockSpec((tr, width), lambda i: (i, 0)), out_shape=_sds((t, width), BF16),
        compiler_params=_cp("parallel"))(proj, g)


def _rms_bwd(dy, proj, col_off, width, g, *, name):
    t = proj.shape[0]
    tr = _tile(t, 640)
    cb = col_off // width

    def body(dy_ref, x_ref, g_ref, dx_ref, dg_ref):
        i = pl.program_id(0)
        x = x_ref[...]
        dyv = dy_ref[...].astype(F32)
        r = lax.rsqrt(jnp.mean(x * x, axis=1, keepdims=True) + RMS_EPS)
        gdy = dyv * g_ref[...]
        m = jnp.mean(x * gdy, axis=1, keepdims=True)
        dx_ref[...] = (r * gdy - x * (r * r * r) * m).astype(BF16)
        pg = jnp.sum(dyv * x * r, axis=0, keepdims=True)

        @pl.when(i == 0)
        def _():
            dg_ref[...] = pg

        @pl.when(i > 0)
        def _():
            dg_ref[...] += pg

    return pl.pallas_call(
        body, name=name, grid=(t // tr,),
        in_specs=[pl.BlockSpec((tr, width), lambda i: (i, 0)), pl.BlockSpec((tr, width), lambda i: (i, cb)),
                  pl.BlockSpec((1, width), lambda i: (0, 0))],
        out_specs=[pl.BlockSpec((tr, width), lambda i: (i, 0)), pl.BlockSpec((1, width), lambda i: (0, 0))],
        out_shape=[_sds((t, width), BF16), _sds((1, width), F32)],
        compiler_params=_cp("arbitrary"))(dy, proj, g)


def _rope_apply(r, cos, sin_a, sin_b):
    return r * cos + pltpu.roll(r, 96, 1) * sin_a + pltpu.roll(r, 32, 1) * sin_b


def _rope_apply_t(dr, cos, sin_a, sin_b):
    return dr * cos + pltpu.roll(dr * sin_a, 32, 1) + pltpu.roll(dr * sin_b, 96, 1)


def _rope_q_fwd(q, cos, sin_a, sin_b, *, name):
    t, w = q.shape
    tr = _tile(t, 640)

    def body(q_ref, c_ref, sa_ref, sb_ref, o_ref):
        c, sa, sb = c_ref[...], sa_ref[...], sb_ref[...]
        flag = lax.broadcasted_iota(jnp.int32, (tr, LANES), 1) == QK_ROPE
        for h in range(MLA_HEADS):
            base = h * QHEAD
            o_ref[:, base:base + LANES] = (q_ref[:, base:base + LANES] * Q_SCALE).astype(BF16)
            rot = _rope_apply(q_ref[:, base + LANES:base + QHEAD], c, sa, sb)
            o_ref[:, base + LANES:base + QHEAD] = jnp.where(flag, 1.0, rot * Q_SCALE).astype(BF16)

    tab = pl.BlockSpec((tr, LANES), lambda i: (i, 0))
    row = pl.BlockSpec((tr, w), lambda i: (i, 0))
    return pl.pallas_call(body, name=name, grid=(t // tr,), in_specs=[row, tab, tab, tab], out_specs=row,
                          out_shape=_sds((t, w), BF16), compiler_params=_cp("parallel"))(q, cos, sin_a, sin_b)


def _rope_q_bwd(dq, cos, sin_a, sin_b, *, name):
    t, w = dq.shape
    tr = _tile(t, 640)

    def body(dq_ref, c_ref, sa_ref, sb_ref, o_ref):
        c, sa, sb = c_ref[...], sa_ref[...], sb_ref[...]
        for h in range(MLA_HEADS):
            base = h * QHEAD
            o_ref[:, base:base + LANES] = (dq_ref[:, base:base + LANES] * ATTN_SCALE).astype(BF16)
            d_rot = _rope_apply_t(dq_ref[:, base + LANES:base + QHEAD], c, sa, sb)
            o_ref[:, base + LANES:base + QHEAD] = (d_rot * ATTN_SCALE).astype(BF16)

    tab = pl.BlockSpec((tr, LANES), lambda i: (i, 0))
    row = pl.BlockSpec((tr, w), lambda i: (i, 0))
    return pl.pallas_call(body, name=name, grid=(t // tr,), in_specs=[row, tab, tab, tab], out_specs=row,
                          out_shape=_sds((t, w), BF16), compiler_params=_cp("parallel"))(dq, cos, sin_a, sin_b)


def _rope_k_fwd(proj, cos, sin_a, sin_b, *, name):
    t = proj.shape[0]
    tr = _tile(t, 640)
    cb = OFF_KPE // LANES

    def body(x_ref, c_ref, sa_ref, sb_ref, o_ref):
        rot = _rope_apply(x_ref[...], c_ref[...], sa_ref[...], sb_ref[...])
        rows = _row_ids(pl.program_id(0), tr, (tr, LANES))
        lane = lax.broadcasted_iota(jnp.int32, (tr, LANES), 1)
        o_ref[...] = jnp.where((lane == QK_ROPE) & (rows < PAD), PAD_KEY_SCORE, rot).astype(BF16)

    tab = pl.BlockSpec((tr, LANES), lambda i: (i, 0))
    return pl.pallas_call(body, name=name, grid=(t // tr,),
                          in_specs=[pl.BlockSpec((tr, LANES), lambda i: (i, cb)), tab, tab, tab], out_specs=tab,
                          out_shape=_sds((t, LANES), BF16), compiler_params=_cp("parallel"))(proj, cos, sin_a, sin_b)


def _rope_k_bwd(dkp, cos, sin_a, sin_b, *, name):
    nh, t, _ = dkp.shape
    tr = _tile(t, 640)

    def body(d_ref, c_ref, sa_ref, sb_ref, o_ref):
        tot = d_ref[0]
        for h in range(1, nh):
            tot = tot + d_ref[h]
        o_ref[...] = _rope_apply_t(tot, c_ref[...], sa_ref[...], sb_ref[...]).astype(BF16)

    tab = pl.BlockSpec((tr, LANES), lambda i: (i, 0))
    return pl.pallas_call(body, name=name, grid=(t // tr,),
                          in_specs=[pl.BlockSpec((nh, tr, LANES), lambda i: (0, i, 0)), tab, tab, tab], out_specs=tab,
                          out_shape=_sds((t, LANES), BF16), compiler_params=_cp("parallel"))(dkp, cos, sin_a, sin_b)


def _causal(tb, keys_first=False):
    a = lax.broadcasted_iota(jnp.int32, (tb, tb), 0)
    b = lax.broadcasted_iota(jnp.int32, (tb, tb), 1)
    return a <= b if keys_first else b <= a


def _flash_fwd(q, kv, kpe, *, name, gather=()):
    t = q.shape[0]
    nh = MLA_HEADS
    tb = _tile(t, 640)
    nb = t // tb
    na = len(gather)

    def attend(q_ref, kn_ref, v_ref, kp_ref, o_ref, lse_ref, extra):
        i = pl.program_id(1)
        qv = q_ref[...]

        def scores(j):
            r0 = pl.multiple_of(j * tb, tb)
            k = jnp.concatenate([kn_ref[pl.ds(r0, tb), :], kp_ref[pl.ds(r0, tb), :]], axis=1)
            return lax.dot_general(qv, k, NT_DIMS, preferred_element_type=F32)

        def update(s, j, state):
            m_prev, l_prev, acc = state
            m_new = jnp.maximum(m_prev, jnp.max(s, axis=1, keepdims=True))
            p = jnp.exp2(s - m_new)
            corr = jnp.exp2(m_prev - m_new)
            r0 = pl.multiple_of(j * tb, tb)
            pv = jnp.dot(p.astype(BF16), v_ref[pl.ds(r0, tb), :], preferred_element_type=F32)
            return m_new, corr * l_prev + jnp.sum(p, axis=1, keepdims=True), corr * acc + pv

        def loop(j, carry):
            s_cur, st = carry
            s_next = scores(j + 1)
            return s_next, update(s_cur, j, st)

        state = (jnp.full((tb, 1), NEG_INF, F32), jnp.zeros((tb, 1), F32), jnp.zeros((tb, V_HEAD), F32))
        s_diag, state = lax.fori_loop(0, i, loop, (scores(0), state))
        m, l, acc = update(jnp.where(_causal(tb), s_diag, NEG_INF), i, state)
        o_ref[...] = (acc / l).astype(BF16)
        lse_ref[0] = m + jnp.log2(l)

        if na:
            step = pl.program_id(0) * nb + i
            for phase, at in enumerate((0, (nh * nb) // 2, nh * nb - 1)):
                @pl.when(step == at)
                def _(phase=phase):
                    _chip_allgather_phase(phase, extra[:na], extra[na:2 * na], *extra[2 * na:])

    def body(q_ref, kn_ref, v_ref, kp_ref, *rest):
        attend(q_ref, kn_ref, v_ref, kp_ref, *rest[na:na + 2], extra=rest[:na] + rest[na + 2:])

    gather = list(gather)
    return pl.pallas_call(
        body, name=name, grid=(nh, nb),
        in_specs=[pl.BlockSpec((tb, QHEAD), lambda h, i: (i, h)),
                  pl.BlockSpec((t, LANES), lambda h, i: (0, h)),
                  pl.BlockSpec((t, LANES), lambda h, i: (0, nh + h)),
                  pl.BlockSpec((t, LANES), lambda h, i: (0, 0))] + [_ANY] * na,
        out_specs=[pl.BlockSpec((tb, V_HEAD), lambda h, i: (i, h)),
                   pl.BlockSpec((1, tb, 1), lambda h, i: (h, i, 0))] + [_ANY] * na,
        out_shape=[_sds((t, nh * V_HEAD), BF16), _sds((nh, t, 1), F32)] + _chip_allgather_shapes(gather),
        scratch_shapes=_chip_allgather_sems(na) if na else [],
        compiler_params=_cp("arbitrary", "arbitrary"))(q, kv, kv, kpe, *gather)


def _attn_delta(do, o, *, name):
    t = o.shape[0]
    nh = MLA_HEADS
    tr = _tile(t, 640)

    def body(do_ref, o_ref, d_ref):
        d_ref[0] = jnp.sum(do_ref[...].astype(F32) * o_ref[...].astype(F32), axis=1, keepdims=True)

    blk = pl.BlockSpec((tr, V_HEAD), lambda h, i: (i, h))
    return pl.pallas_call(body, name=name, grid=(nh, t // tr), in_specs=[blk, blk],
                          out_specs=pl.BlockSpec((1, tr, 1), lambda h, i: (h, i, 0)),
                          out_shape=_sds((nh, t, 1), F32), compiler_params=_cp("parallel", "parallel"))(do, o)


def _flash_bwd(q, kv, kpe, do, lse, delta, *, name, exchange=None):
    t = q.shape[0]
    nh = MLA_HEADS
    tb = lse.shape[2]
    nb = t // tb
    fused = exchange is not None

    def body(*refs):
        q_ref, do_ref, lse_ref, dl_ref, kn_ref, v_ref, kp_ref = refs[:7]
        dq_ref, dkn_ref, dkp_ref, dv_ref = refs[7 + fused:11 + fused]
        j = pl.program_id(1)

        if fused:
            copies = functools.partial(_chip_exchange_copies, refs[7], refs[11 + fused], *refs[12 + fused:])
            first = (pl.program_id(0) == 0) & (j == 0)
            last = (pl.program_id(0) == nh - 1) & (j == nb - 1)

            @pl.when(first)
            def _():
                _chip_exchange_start(copies())

        @pl.when(j == 0)
        def _():
            dq_ref[...] = jnp.zeros((t, QHEAD), F32)

        k = jnp.concatenate([kn_ref[...], kp_ref[...]], axis=1)
        v = v_ref[...]

        def tile(i, carry, masked):
            dk, dv = carry
            r0 = pl.multiple_of(i * tb, tb)
            qv = q_ref[pl.ds(r0, tb), :]
            dov = do_ref[pl.ds(r0, tb), :]
            st = lax.dot_general(k, qv, NT_DIMS, preferred_element_type=F32)
            if masked:
                st = jnp.where(_causal(tb, keys_first=True), st, NEG_INF)
            pt = jnp.exp2(st - lse_ref[0, pl.ds(i, 1), :])
            dpt = lax.dot_general(v, dov, NT_DIMS, preferred_element_type=F32)
            dst = (pt * (dpt - dl_ref[0, pl.ds(i, 1), :])).astype(BF16)
            dv = dv + jnp.dot(pt.astype(BF16), dov, preferred_element_type=F32)
            dk = dk + jnp.dot(dst, qv, preferred_element_type=F32)
            dq_ref[pl.ds(r0, tb), :] += lax.dot_general(dst, k, TN_DIMS, preferred_element_type=F32)
            return dk, dv

        carry = tile(j, (jnp.zeros((tb, QHEAD), F32), jnp.zeros((tb, V_HEAD), F32)), True)
        dk, dv = lax.fori_loop(j + 1, nb, lambda i, c: tile(i, c, False), carry)
        dkn_ref[...] = (dk[:, :LANES] * LN2).astype(BF16)
        dkp_ref[0] = dk[:, LANES:] * LN2
        dv_ref[...] = dv.astype(BF16)

        if fused:
            @pl.when(last)
            def _():
                _chip_exchange_wait(copies())

    stat = pl.BlockSpec((1, nb, tb), lambda h, j: (h, 0, 0))
    in_specs = [pl.BlockSpec((t, QHEAD), lambda h, j: (0, h)),
                pl.BlockSpec((t, V_HEAD), lambda h, j: (0, h)),
                stat, stat,
                pl.BlockSpec((tb, LANES), lambda h, j: (j, h)),
                pl.BlockSpec((tb, LANES), lambda h, j: (j, nh + h)),
                pl.BlockSpec((tb, LANES), lambda h, j: (j, 0))]
    out_specs = [pl.BlockSpec((t, QHEAD), lambda h, j: (0, h)),
                 pl.BlockSpec((tb, LANES), lambda h, j: (j, h)),
                 pl.BlockSpec((1, tb, LANES), lambda h, j: (h, j, 0)),
                 pl.BlockSpec((tb, V_HEAD), lambda h, j: (j, h))]
    out_shape = [_sds((t, nh * QHEAD), F32), _sds((t, nh * LANES), BF16), _sds((nh, t, LANES), F32),
                 _sds((t, nh * V_HEAD), BF16)]
    args = [q, do, lse, delta, kv, kv, kpe]
    scratch = []
    if fused:
        in_specs.append(_ANY)
        out_specs.append(_ANY)
        out_shape.append(_sds(exchange.shape, exchange.dtype))
        args.append(exchange)
        scratch = _CHIP_EXCHANGE_SEMS
    return pl.pallas_call(body, name=name, grid=(nh, nb), in_specs=in_specs, out_specs=out_specs, out_shape=out_shape,
                          scratch_shapes=scratch, compiler_params=_cp("arbitrary", "arbitrary"))(*args)


def _fill_prev(buf, x_ref, halo_ref, i, tr):
    buf[pl.ds(0, HALO), :] = jnp.where(i > 0, halo_ref[...], 0.0)
    buf[pl.ds(HALO, tr), :] = x_ref[...]


def _conv_prev(buf, w_ref, kw, tr):
    acc = w_ref[kw - 1:kw, :] * buf[pl.ds(HALO, tr), :]
    for k in range(kw - 1):
        acc = acc + w_ref[k:k + 1, :] * buf[pl.ds(HALO - kw + 1 + k, tr), :]
    return acc


def _conv_dw(buf, dc, kw, tr):
    rows = [jnp.sum(dc * buf[pl.ds(HALO - kw + 1 + k, tr), :], axis=0, keepdims=True) for k in range(kw)]
    return jnp.concatenate(rows, axis=0)


def _conv_next(buf, dc_ref, halo_ref, w_ref, kw, i, n_tiles, tr):
    buf[pl.ds(0, tr), :] = dc_ref[...]
    buf[pl.ds(tr, HALO), :] = jnp.where(i < n_tiles - 1, halo_ref[...], 0.0)
    acc = w_ref[kw - 1:kw, :] * buf[pl.ds(0, tr), :]
    for k in range(kw - 1):
        acc = acc + w_ref[k:k + 1, :] * buf[pl.ds(kw - 1 - k, tr), :]
    return acc


def _split3(x):
    x1 = x.astype(BF16)
    r1 = x - x1.astype(F32)
    x2 = r1.astype(BF16)
    x3 = (r1 - x2.astype(F32)).astype(BF16)
    return x1, x2, x3


def _dot3(parts, m, left):
    tot = None
    for p in parts:
        r = jnp.dot(m, p, preferred_element_type=F32) if left else jnp.dot(p, m, preferred_element_type=F32)
        tot = r if tot is None else tot + r
    return tot


def _ssd_prep_fwd(proj, conv_w, conv_b, dt_bias, expand, *, name):
    t = proj.shape[0]
    tr = _tile(t, 128)
    nt = t // tr
    hb = tr // HALO
    cw = SSD_CONV_DIM
    cb_x = OFF_XBC // cw
    cb_dt = OFF_DT // LANES

    def body(x_ref, halo_ref, dtr_ref, w_ref, b_ref, dtb_ref, e_ref, xs_ref, bm_ref, cm_ref, dtx_ref, buf):
        i = pl.program_id(0)
        _fill_prev(buf, x_ref, halo_ref, i, tr)
        conv = _conv_prev(buf, w_ref, SSD_CONV, tr) + b_ref[...]
        rows = _row_ids(i, tr, (tr, 1))
        live = rows >= PAD
        act = jnp.where(live, conv * _sigmoid(conv), 0.0)
        xs_ref[...] = act[:, :SSD_INNER]
        bm_ref[...] = act[:, SSD_INNER:SSD_INNER + GROUP_W]
        cm_ref[...] = act[:, SSD_INNER + GROUP_W:]
        dt = jnp.where(live, jax.nn.softplus(dtr_ref[...] + dtb_ref[...]), 0.0)
        dtx_ref[...] = _dot3(_split3(dt), e_ref[...], left=False)

    return pl.pallas_call(
        body, name=name, grid=(nt,),
        in_specs=[pl.BlockSpec((tr, cw), lambda i: (i, cb_x)),
                  pl.BlockSpec((HALO, cw), lambda i: (jnp.maximum(i * hb - 1, 0), cb_x)),
                  pl.BlockSpec((tr, LANES), lambda i: (i, cb_dt)),
                  pl.BlockSpec((SSD_CONV, cw), lambda i: (0, 0)),
                  pl.BlockSpec((1, cw), lambda i: (0, 0)),
                  pl.BlockSpec((1, LANES), lambda i: (0, 0)),
                  pl.BlockSpec((LANES, SSD_INNER), lambda i: (0, 0))],
        out_specs=[pl.BlockSpec((tr, SSD_INNER), lambda i: (i, 0)), pl.BlockSpec((tr, GROUP_W), lambda i: (i, 0)),
                   pl.BlockSpec((tr, GROUP_W), lambda i: (i, 0)), pl.BlockSpec((tr, SSD_INNER), lambda i: (i, 0))],
        out_shape=[_sds((t, SSD_INNER), F32), _sds((t, GROUP_W), F32), _sds((t, GROUP_W), F32),
                   _sds((t, SSD_INNER), F32)],
        scratch_shapes=[pltpu.VMEM((tr + HALO, cw), F32)],
        compiler_params=_cp("parallel"))(proj, proj, proj, conv_w, conv_b, dt_bias, expand)


def _ssd_prep_bwd_a(proj, dxs, dbm, dcm, ddtx, conv_w, conv_b, dt_bias, reduce_m, *, name):
    t = proj.shape[0]
    tr = _tile(t, 128)
    nt = t // tr
    hb = tr // HALO
    cw = SSD_CONV_DIM
    cb_x = OFF_XBC // cw
    cb_dt = OFF_DT // LANES

    def body(x_ref, halo_ref, dtr_ref, dxs_ref, dbm_ref, dcm_ref, ddtx_ref, w_ref, b_ref, dtb_ref, r_ref,
             dconv_ref, ddtr_ref, dw_ref, db_ref, ddtb_ref, buf):
        i = pl.program_id(0)
        _fill_prev(buf, x_ref, halo_ref, i, tr)
        conv = _conv_prev(buf, w_ref, SSD_CONV, tr) + b_ref[...]
        rows = _row_ids(i, tr, (tr, 1))
        live = rows >= PAD
        sg = _sigmoid(conv)
        dact = jnp.concatenate([dxs_ref[...], dbm_ref[...], dcm_ref[...]], axis=1)
        dconv = jnp.where(live, dact * (sg * (1.0 + conv * (1.0 - sg))), 0.0)
        dconv_ref[...] = dconv
        pw = _conv_dw(buf, dconv, SSD_CONV, tr)
        pb = jnp.sum(dconv, axis=0, keepdims=True)
        ddt = _dot3(_split3(ddtx_ref[...]), r_ref[...], left=False)
        ddtr = jnp.where(live, ddt * _sigmoid(dtr_ref[...] + dtb_ref[...]), 0.0)
        ddtr_ref[...] = ddtr.astype(BF16)
        pdb = jnp.sum(ddtr, axis=0, keepdims=True)

        @pl.when(i == 0)
        def _():
            dw_ref[...] = pw
            db_ref[...] = pb
            ddtb_ref[...] = pdb

        @pl.when(i > 0)
        def _():
            dw_ref[...] += pw
            db_ref[...] += pb
            ddtb_ref[...] += pdb

    return pl.pallas_call(
        body, name=name, grid=(nt,),
        in_specs=[pl.BlockSpec((tr, cw), lambda i: (i, cb_x)),
                  pl.BlockSpec((HALO, cw), lambda i: (jnp.maximum(i * hb - 1, 0), cb_x)),
                  pl.BlockSpec((tr, LANES), lambda i: (i, cb_dt)),
                  pl.BlockSpec((tr, SSD_INNER), lambda i: (i, 0)),
                  pl.BlockSpec((tr, GROUP_W), lambda i: (i, 0)),
                  pl.BlockSpec((tr, GROUP_W), lambda i: (i, 0)),
                  pl.BlockSpec((tr, SSD_INNER), lambda i: (i, 0)),
                  pl.BlockSpec((SSD_CONV, cw), lambda i: (0, 0)),
                  pl.BlockSpec((1, cw), lambda i: (0, 0)),
                  pl.BlockSpec((1, LANES), lambda i: (0, 0)),
                  pl.BlockSpec((SSD_INNER, LANES), lambda i: (0, 0))],
        out_specs=[pl.BlockSpec((tr, cw), lambda i: (i, 0)), pl.BlockSpec((tr, LANES), lambda i: (i, 0)),
                   pl.BlockSpec((SSD_CONV, cw), lambda i: (0, 0)), pl.BlockSpec((1, cw), lambda i: (0, 0)),
                   pl.BlockSpec((1, LANES), lambda i: (0, 0))],
        out_shape=[_sds((t, cw), F32), _sds((t, LANES), BF16), _sds((SSD_CONV, cw), F32), _sds((1, cw), F32),
                   _sds((1, LANES), F32)],
        scratch_shapes=[pltpu.VMEM((tr + HALO, cw), F32)],
        compiler_params=_cp("arbitrary"))(proj, proj, proj, dxs, dbm, dcm, ddtx, conv_w, conv_b, dt_bias, reduce_m)


def _conv_bwd_input(dconv, w, kw, *, name, out_dtype=BF16, tc=None):
    t, c = dconv.shape
    tr = _tile(t, 640)
    nt = t // tr
    hb = tr // HALO
    tc = _tile(c, tc or c)
    last_hb = t // HALO - 1

    def body(dc_ref, halo_ref, w_ref, o_ref, buf):
        i = pl.program_id(0)
        o_ref[...] = _conv_next(buf, dc_ref, halo_ref, w_ref, kw, i, nt, tr).astype(out_dtype)

    return pl.pallas_call(
        body, name=name, grid=(nt, c // tc),
        in_specs=[pl.BlockSpec((tr, tc), lambda i, j: (i, j)),
                  pl.BlockSpec((HALO, tc), lambda i, j: (jnp.minimum((i + 1) * hb, last_hb), j)),
                  pl.BlockSpec((kw, tc), lambda i, j: (0, j))],
        out_specs=pl.BlockSpec((tr, tc), lambda i, j: (i, j)), out_shape=_sds((t, c), out_dtype),
        scratch_shapes=[pltpu.VMEM((tr + HALO, tc), F32)],
        compiler_params=_cp("parallel", "parallel"))(dconv, dconv, w)


def _ffn_act_fwd(ug, uv, wg, wv, bg, bv, *, name):
    t, c = ug.shape
    tr = _tile(t, 640)
    hb = tr // HALO
    tc = _tile(c, 1408)

    def body(ug_ref, hg_ref, uv_ref, hv_ref, wg_ref, wv_ref, bg_ref, bv_ref, o_ref, bufg, bufv):
        i = pl.program_id(0)
        _fill_prev(bufg, ug_ref, hg_ref, i, tr)
        _fill_prev(bufv, uv_ref, hv_ref, i, tr)
        cg = _conv_prev(bufg, wg_ref, FFN_CONV, tr) + bg_ref[...]
        cv = _conv_prev(bufv, wv_ref, FFN_CONV, tr) + bv_ref[...]
        o_ref[...] = (cg * _sigmoid(cg) * cv).astype(BF16)

    blk = pl.BlockSpec((tr, tc), lambda i, j: (i, j))
    halo = pl.BlockSpec((HALO, tc), lambda i, j: (jnp.maximum(i * hb - 1, 0), j))
    wsp = pl.BlockSpec((FFN_CONV, tc), lambda i, j: (0, j))
    bsp = pl.BlockSpec((1, tc), lambda i, j: (0, j))
    return pl.pallas_call(
        body, name=name, grid=(t // tr, c // tc), in_specs=[blk, halo, blk, halo, wsp, wsp, bsp, bsp],
        out_specs=blk, out_shape=_sds((t, c), BF16),
        scratch_shapes=[pltpu.VMEM((tr + HALO, tc), F32), pltpu.VMEM((tr + HALO, tc), F32)],
        compiler_params=_cp("parallel", "parallel"))(ug, ug, uv, uv, wg, wv, bg, bv)


def _ffn_act_bwd(ug, uv, dact, wg, wv, bg, bv, *, name):
    t, c = ug.shape
    tr = _tile(t, 128)
    hb = tr // HALO
    tc = _tile(c, 1408)

    def body(ug_ref, hg_ref, uv_ref, hv_ref, da_ref, wg_ref, wv_ref, bg_ref, bv_ref,
             dcg_ref, dcv_ref, dwg_ref, dwv_ref, dbg_ref, dbv_ref, bufg, bufv):
        i = pl.program_id(1)
        _fill_prev(bufg, ug_ref, hg_ref, i, tr)
        _fill_prev(bufv, uv_ref, hv_ref, i, tr)
        cg = _conv_prev(bufg, wg_ref, FFN_CONV, tr) + bg_ref[...]
        cv = _conv_prev(bufv, wv_ref, FFN_CONV, tr) + bv_ref[...]
        sg = _sigmoid(cg)
        da = da_ref[...]
        dcg = da * cv * (sg * (1.0 + cg * (1.0 - sg)))
        dcv = da * (cg * sg)
        dcg_ref[...] = dcg
        dcv_ref[...] = dcv
        pwg = _conv_dw(bufg, dcg, FFN_CONV, tr)
        pwv = _conv_dw(bufv, dcv, FFN_CONV, tr)
        pbg = jnp.sum(dcg, axis=0, keepdims=True)
        pbv = jnp.sum(dcv, axis=0, keepdims=True)

        @pl.when(i == 0)
        def _():
            dwg_ref[...] = pwg
            dwv_ref[...] = pwv
            dbg_ref[...] = pbg
            dbv_ref[...] = pbv

        @pl.when(i > 0)
        def _():
            dwg_ref[...] += pwg
            dwv_ref[...] += pwv
            dbg_ref[...] += pbg
            dbv_ref[...] += pbv

    blk = pl.BlockSpec((tr, tc), lambda j, i: (i, j))
    halo = pl.BlockSpec((HALO, tc), lambda j, i: (jnp.maximum(i * hb - 1, 0), j))
    wsp = pl.BlockSpec((FFN_CONV, tc), lambda j, i: (0, j))
    bsp = pl.BlockSpec((1, tc), lambda j, i: (0, j))
    return pl.pallas_call(
        body, name=name, grid=(c // tc, t // tr), in_specs=[blk, halo, blk, halo, blk, wsp, wsp, bsp, bsp],
        out_specs=[blk, blk, wsp, wsp, bsp, bsp],
        out_shape=[_sds((t, c), F32), _sds((t, c), F32), _sds((FFN_CONV, c), F32), _sds((FFN_CONV, c), F32),
                   _sds((1, c), F32), _sds((1, c), F32)],
        scratch_shapes=[pltpu.VMEM((tr + HALO, tc), F32), pltpu.VMEM((tr + HALO, tc), F32)],
        compiler_params=_cp("parallel", "arbitrary"))(ug, ug, uv, uv, dact, wg, wv, bg, bv)


def _tri(lower):
    li = lax.broadcasted_iota(jnp.int32, (CHUNK, CHUNK), 0)
    si = lax.broadcasted_iota(jnp.int32, (CHUNK, CHUNK), 1)
    return li >= si if lower else li <= si


def _tri_ones(lower):
    return jnp.where(_tri(lower), 1.0, 0.0).astype(BF16)


def _decay_pair(acs, acs_t, lane0):
    col = acs[:, lane0:lane0 + 1]
    row = acs_t[lane0:lane0 + 1, :]
    low = jnp.where(_tri(True), jnp.exp(jnp.minimum(col - row, 0.0)), 0.0)
    upp = jnp.where(_tri(False), jnp.exp(jnp.minimum(row - col, 0.0)), 0.0)
    return low, upp


def _ssd_fwd(xs, dtx, bm, cm, bm_t, a_x, d_x, *, name):
    t = xs.shape[0]
    nc = t // CHUNK
    gw = GROUP_W

    def body(xs_ref, dt_ref, b_ref, c_ref, bt_ref, a_ref, d_ref, y_ref, prev_ref, h_s):
        @pl.when(pl.program_id(1) == 0)
        def _():
            h_s[...] = jnp.zeros((SSD_STATE, gw), F32)

        x = xs_ref[...]
        dt = dt_ref[...]
        acs = _dot3(_split3(dt * a_ref[...]), _tri_ones(True), left=True)
        acs_t = acs.T
        xc = x * dt
        bv = b_ref[...].astype(BF16)
        cv = c_ref[...].astype(BF16)
        cb = lax.dot_general(cv, bv, NT_DIMS, preferred_element_type=F32)
        lane = lax.broadcasted_iota(jnp.int32, (CHUNK, LANES), 1)
        pieces = []
        for pp in range(gw // LANES):
            xcp = xc[:, pp * LANES:(pp + 1) * LANES]
            acc = jnp.zeros((CHUNK, LANES), F32)
            for e in range(2):
                low, _ = _decay_pair(acs, acs_t, pp * LANES + e * SSD_HEAD_DIM)
                mine = (lane >= e * SSD_HEAD_DIM) & (lane < (e + 1) * SSD_HEAD_DIM)
                xm = jnp.where(mine, xcp, 0.0).astype(BF16)
                acc = acc + jnp.dot((cb * low).astype(BF16), xm, preferred_element_type=F32)
            pieces.append(acc)
        y_diag = jnp.concatenate(pieces, axis=1)
        h_prev = h_s[...]
        y_off = jnp.dot(cv, h_prev.astype(BF16), preferred_element_type=F32) * jnp.exp(acs)
        y_ref[...] = y_diag + y_off + d_ref[...] * x
        prev_ref[0] = h_prev
        last = acs[CHUNK - 1:CHUNK, :]
        w = jnp.exp(last - acs)
        st = jnp.dot(bt_ref[...].astype(BF16), (xc * w).astype(BF16), preferred_element_type=F32)
        h_s[...] = h_prev * jnp.exp(last) + st

    tok = pl.BlockSpec((CHUNK, gw), lambda g, c: (c, g))
    grp = pl.BlockSpec((CHUNK, SSD_STATE), lambda g, c: (c, g))
    vec = pl.BlockSpec((1, gw), lambda g, c: (0, g))
    return pl.pallas_call(
        body, name=name, grid=(SSD_GROUPS, nc),
        in_specs=[tok, tok, grp, grp, pl.BlockSpec((SSD_STATE, CHUNK), lambda g, c: (g, c)), vec, vec],
        out_specs=[tok, pl.BlockSpec((1, SSD_STATE, gw), lambda g, c: (c, 0, g))],
        out_shape=[_sds((t, SSD_INNER), F32), _sds((nc, SSD_STATE, SSD_INNER), F32)],
        scratch_shapes=[pltpu.VMEM((SSD_STATE, gw), F32)],
        compiler_params=_cp("parallel", "arbitrary"))(xs, dtx, bm, cm, bm_t, a_x, d_x)


def _ssd_bwd(xs, dtx, bm, cm, cm_t, prev, dy, a_x, d_x, *, name):
    t = xs.shape[0]
    nc = t // CHUNK
    gw = GROUP_W

    def body(xs_ref, dt_ref, b_ref, c_ref, ct_ref, prev_ref, dy_ref, a_ref, d_ref,
             dxs_ref, ddt_ref, db_ref, dc_ref, da_ref, dd_ref, g_s):
        first = pl.program_id(1) == 0

        @pl.when(first)
        def _():
            g_s[...] = jnp.zeros((SSD_STATE, gw), F32)

        x = xs_ref[...]
        dt = dt_ref[...]
        a = a_ref[...]
        dyv = dy_ref[...]
        acs = _dot3(_split3(dt * a), _tri_ones(True), left=True)
        acs_t = acs.T
        xc = x * dt
        bv = b_ref[...].astype(BF16)
        cv = c_ref[...].astype(BF16)
        cb = lax.dot_general(cv, bv, NT_DIMS, preferred_element_type=F32)
        cb_t = lax.dot_general(bv, cv, NT_DIMS, preferred_element_type=F32)
        last = acs[CHUNK - 1:CHUNK, :]
        w = jnp.exp(last - acs)
        cd = jnp.exp(last)
        p_in = prev_ref[0]
        p_b = p_in.astype(BF16)
        g_out = g_s[...]
        g_b = g_out.astype(BF16)
        dy_e = dyv * jnp.exp(acs)
        dy_eb = dy_e.astype(BF16)
        y_off_raw = jnp.dot(cv, p_b, preferred_element_type=F32)
        dacs = dy_e * y_off_raw
        d_c = lax.dot_general(dy_eb, p_b, NT_DIMS, preferred_element_type=F32)
        d_prev = jnp.dot(ct_ref[...].astype(BF16), dy_eb, preferred_element_type=F32)
        q_l = jnp.dot(bv, g_b, preferred_element_type=F32)
        dxc = w * q_l
        tw = xc * q_l * w
        dacs = dacs - tw
        d_b = lax.dot_general((xc * w).astype(BF16), g_b, NT_DIMS, preferred_element_type=F32)
        last_add = jnp.sum(tw, axis=0, keepdims=True) + cd * jnp.sum(g_out * p_in, axis=0, keepdims=True)
        g_s[...] = cd * g_out + d_prev
        lane = lax.broadcasted_iota(jnp.int32, (CHUNK, LANES), 1)
        d_cb = jnp.zeros((CHUNK, CHUNK), F32)
        d_cb_t = jnp.zeros((CHUNK, CHUNK), F32)
        dxc_pieces, dacs_pieces = [], []
        for pp in range(gw // LANES):
            xcp = xc[:, pp * LANES:(pp + 1) * LANES]
            dyp = dyv[:, pp * LANES:(pp + 1) * LANES]
            dxcp = jnp.zeros((CHUNK, LANES), F32)
            dacsp = jnp.zeros((CHUNK, LANES), F32)
            for e in range(2):
                low, upp = _decay_pair(acs, acs_t, pp * LANES + e * SSD_HEAD_DIM)
                mine = (lane >= e * SSD_HEAD_DIM) & (lane < (e + 1) * SSD_HEAD_DIM)
                m_low = cb * low
                m_upp = cb_t * upp
                dym = jnp.where(mine, dyp, 0.0).astype(BF16)
                xm = jnp.where(mine, xcp, 0.0).astype(BF16)
                dxcp = dxcp + jnp.dot(m_upp.astype(BF16), dym, preferred_element_type=F32)
                d_m = lax.dot_general(dym, xm, NT_DIMS, preferred_element_type=F32)
                d_m_t = lax.dot_general(xm, dym, NT_DIMS, preferred_element_type=F32)
                rs = jnp.sum(d_m * m_low, axis=1, keepdims=True)
                cs = jnp.sum(d_m_t * m_upp, axis=1, keepdims=True)
                dacsp = dacsp + jnp.where(lane == e * SSD_HEAD_DIM, rs - cs, 0.0)
                d_cb = d_cb + d_m * low
                d_cb_t = d_cb_t + d_m_t * upp
            dxc_pieces.append(dxcp)
            dacs_pieces.append(dacsp)
        dxc = dxc + jnp.concatenate(dxc_pieces, axis=1)
        dacs = dacs + jnp.concatenate(dacs_pieces, axis=1)
        rowi = lax.broadcasted_iota(jnp.int32, (CHUNK, gw), 0)
        dacs = dacs + jnp.where(rowi == CHUNK - 1, last_add, 0.0)
        dc_ref[...] = d_c + jnp.dot(d_cb.astype(BF16), bv, preferred_element_type=F32)
        db_ref[...] = d_b + jnp.dot(d_cb_t.astype(BF16), cv, preferred_element_type=F32)
        dda = _dot3(_split3(dacs), _tri_ones(False), left=True)
        ddt_ref[...] = dda * a + dxc * x
        dxs_ref[...] = dxc * dt + d_ref[...] * dyv
        pa = jnp.sum(dda * dt, axis=0, keepdims=True)
        pd = jnp.sum(dyv * x, axis=0, keepdims=True)

        @pl.when(first)
        def _():
            da_ref[...] = pa
            dd_ref[...] = pd

        @pl.when(jnp.logical_not(first))
        def _():
            da_ref[...] += pa
            dd_ref[...] += pd

    rc = lambda c: nc - 1 - c
    tok = pl.BlockSpec((CHUNK, gw), lambda g, c: (rc(c), g))
    grp = pl.BlockSpec((CHUNK, SSD_STATE), lambda g, c: (rc(c), g))
    vec = pl.BlockSpec((1, gw), lambda g, c: (0, g))
    return pl.pallas_call(
        body, name=name, grid=(SSD_GROUPS, nc),
        in_specs=[tok, tok, grp, grp, pl.BlockSpec((SSD_STATE, CHUNK), lambda g, c: (g, rc(c))),
                  pl.BlockSpec((1, SSD_STATE, gw), lambda g, c: (rc(c), 0, g)), tok, vec, vec],
        out_specs=[tok, tok, grp, grp, vec, vec],
        out_shape=[_sds((t, SSD_INNER), F32), _sds((t, SSD_INNER), F32), _sds((t, gw), F32), _sds((t, gw), F32),
                   _sds((1, SSD_INNER), F32), _sds((1, SSD_INNER), F32)],
        scratch_shapes=[pltpu.VMEM((SSD_STATE, gw), F32)],
        compiler_params=_cp("parallel", "arbitrary"))(xs, dtx, bm, cm, cm_t, prev, dy, a_x, d_x)


def _gnorm_fwd(y, proj, g, *, name):
    t = y.shape[0]
    tr = _tile(t, 640)
    zb = OFF_Z // GROUP_W

    def body(y_ref, z_ref, g_ref, o_ref):
        z = z_ref[...]
        v = y_ref[...] * (z * _sigmoid(z))
        r = lax.rsqrt(jnp.mean(v * v, axis=1, keepdims=True) + RMS_EPS)
        o_ref[...] = (v * r * g_ref[...]).astype(BF16)

    blk = pl.BlockSpec((tr, GROUP_W), lambda i, j: (i, j))
    return pl.pallas_call(
        body, name=name, grid=(t // tr, SSD_GROUPS),
        in_specs=[blk, pl.BlockSpec((tr, GROUP_W), lambda i, j: (i, zb + j)),
                  pl.BlockSpec((1, GROUP_W), lambda i, j: (0, j))],
        out_specs=blk, out_shape=_sds((t, SSD_INNER), BF16),
        compiler_params=_cp("parallel", "parallel"))(y, proj, g)


def _gnorm_bwd(dout, y, proj, g, *, name):
    t = y.shape[0]
    tr = _tile(t, 640)
    zb = OFF_Z // GROUP_W

    def body(do_ref, y_ref, z_ref, g_ref, dy_ref, dz_ref, dg_ref):
        i = pl.program_id(1)
        z = z_ref[...]
        yv = y_ref[...]
        sg = _sigmoid(z)
        sz = z * sg
        v = yv * sz
        r = lax.rsqrt(jnp.mean(v * v, axis=1, keepdims=True) + RMS_EPS)
        dov = do_ref[...].astype(F32)
        gdo = dov * g_ref[...]
        m = jnp.mean(v * gdo, axis=1, keepdims=True)
        dv = r * gdo - v * (r * r * r) * m
        dy_ref[...] = dv * sz
        dz_ref[...] = (dv * yv * (sg * (1.0 + z * (1.0 - sg)))).astype(BF16)
        pg = jnp.sum(dov * v * r, axis=0, keepdims=True)

        @pl.when(i == 0)
        def _():
            dg_ref[...] = pg

        @pl.when(i > 0)
        def _():
            dg_ref[...] += pg

    blk = pl.BlockSpec((tr, GROUP_W), lambda j, i: (i, j))
    vec = pl.BlockSpec((1, GROUP_W), lambda j, i: (0, j))
    return pl.pallas_call(
        body, name=name, grid=(SSD_GROUPS, t // tr),
        in_specs=[blk, blk, pl.BlockSpec((tr, GROUP_W), lambda j, i: (i, zb + j)), vec],
        out_specs=[blk, blk, vec],
        out_shape=[_sds((t, SSD_INNER), F32), _sds((t, SSD_INNER), BF16), _sds((1, SSD_INNER), F32)],
        compiler_params=_cp("parallel", "arbitrary"))(dout, y, proj, g)


def _mix_fwd(proj, ya, ys, *, name):
    t, d = ya.shape
    tr = _tile(t, 640)
    ba, bs = OFF_GA // d, OFF_GS // d

    def body(ga_ref, gs_ref, ya_ref, ys_ref, o_ref):
        o_ref[...] = (_sigmoid(ga_ref[...]) * ya_ref[...] + _sigmoid(gs_ref[...]) * ys_ref[...]).astype(BF16)

    blk = pl.BlockSpec((tr, d), lambda i: (i, 0))
    return pl.pallas_call(
        body, name=name, grid=(t // tr,),
        in_specs=[pl.BlockSpec((tr, d), lambda i: (i, ba)), pl.BlockSpec((tr, d), lambda i: (i, bs)), blk, blk],
        out_specs=blk, out_shape=_sds((t, d), BF16), compiler_params=_cp("parallel"))(proj, proj, ya, ys)


def _mix_bwd(dmix, proj, ya, ys, *, name):
    t, d = ya.shape
    tr = _tile(t, 640)
    ba, bs = OFF_GA // d, OFF_GS // d

    def body(dm_ref, ga_ref, gs_ref, ya_ref, ys_ref, dya_ref, dys_ref, dga_ref, dgs_ref):
        dm = dm_ref[...]
        sa = _sigmoid(ga_ref[...])
        ss = _sigmoid(gs_ref[...])
        dya_ref[...] = (sa * dm).astype(BF16)
        dys_ref[...] = (ss * dm).astype(BF16)
        dga_ref[...] = (dm * ya_ref[...] * sa * (1.0 - sa)).astype(BF16)
        dgs_ref[...] = (dm * ys_ref[...] * ss * (1.0 - ss)).astype(BF16)

    blk = pl.BlockSpec((tr, d), lambda i: (i, 0))
    return pl.pallas_call(
        body, name=name, grid=(t // tr,),
        in_specs=[blk, pl.BlockSpec((tr, d), lambda i: (i, ba)), pl.BlockSpec((tr, d), lambda i: (i, bs)), blk, blk],
        out_specs=[blk] * 4, out_shape=[_sds((t, d), BF16)] * 4,
        compiler_params=_cp("parallel"))(dmix, proj, proj, ya, ys)


def _loss_grad(h, target, *, name):
    t, d = h.shape
    tr = LANES
    assert ROW0 == tr

    def body(h_ref, t_ref, dh_ref, loss_ref):
        i = pl.program_id(0)

        @pl.when(i == 0)
        def _():
            dh_ref[...] = jnp.zeros((tr, d), F32)
            loss_ref[...] = jnp.zeros((1, LANES), F32)

        @pl.when(i > 0)
        def _():
            err = h_ref[...] - t_ref[...]
            dh_ref[...] = err * (1.0 / d)
            part = jnp.sum(jnp.sum(err * err, axis=1, keepdims=True), axis=0, keepdims=True)
            loss_ref[...] += jnp.broadcast_to(part * (0.5 / d), (1, LANES))

    blk = pl.BlockSpec((tr, d), lambda i: (i, 0))
    return pl.pallas_call(
        body, name=name, grid=(t // tr,),
        in_specs=[blk, pl.BlockSpec((tr, d), lambda i: (jnp.maximum(i - 1, 0), 0))],
        out_specs=[blk, pl.BlockSpec((1, LANES), lambda i: (0, 0))],
        out_shape=[_sds((t, d), F32), _sds((1, LANES), F32)],
        compiler_params=_cp("arbitrary"))(h, target)


def _adamw_update(gv, wv, mv, vv):
    c1 = 1.0 - ADAM_B1 ** ADAM_STEP
    c2 = 1.0 - ADAM_B2 ** ADAM_STEP
    nm = ADAM_B1 * mv + (1.0 - ADAM_B1) * gv
    nv = ADAM_B2 * vv + (1.0 - ADAM_B2) * (gv * gv)
    return -ADAM_LR * ((nm / c1) / (jnp.sqrt(nv / c2) + ADAM_EPS) + ADAM_WD * wv), nm, nv


def _as_2d(a):
    return a.reshape(1, -1) if a.ndim == 1 else a.reshape(-1, a.shape[-1])


def _adamw(g, w, m, v, *, name):
    shape = w.shape
    g2, w2, m2, v2 = (_as_2d(a) for a in (g, w, m, v))
    r, c = w2.shape
    tr = _tile(r, 256, base=8)

    def body(g_ref, w_ref, m_ref, v_ref, d_ref, nm_ref, nv_ref):
        d_ref[...], nm_ref[...], nv_ref[...] = _adamw_update(g_ref[...], w_ref[...], m_ref[...], v_ref[...])

    blk = pl.BlockSpec((tr, c), lambda i: (i, 0))
    outs = pl.pallas_call(body, name=name, grid=(r // tr,), in_specs=[blk] * 4, out_specs=[blk] * 3,
                          out_shape=[_sds((r, c), F32)] * 3, compiler_params=_cp("parallel"))(g2, w2, m2, v2)
    return [o.reshape(shape) for o in outs]


def _adamw_small(items, *, name):
    n = len(items)
    shapes = [it[1].shape for it in items]
    flat = [_as_2d(a) for it in items for a in it]

    def body(*refs):
        ins, outs = refs[:4 * n], refs[4 * n:]
        for k in range(n):
            g_ref, w_ref, m_ref, v_ref = ins[4 * k:4 * k + 4]
            d_ref, nm_ref, nv_ref = outs[3 * k:3 * k + 3]
            d_ref[...], nm_ref[...], nv_ref[...] = _adamw_update(g_ref[...], w_ref[...], m_ref[...], v_ref[...])

    out_shape = [_sds(flat[4 * k + 1].shape, F32) for k in range(n) for _ in range(3)]
    outs = pl.pallas_call(body, name=name, out_shape=out_shape,
                          compiler_params=pltpu.CompilerParams(vmem_limit_bytes=VMEM_LIMIT_BYTES))(*flat)
    return [[outs[3 * k + q].reshape(shapes[k]) for q in range(3)] for k in range(n)]


def _pair_add(g5, got, core, *, name):
    n, _, r, _ = g5.shape
    tr = _tile(r, 1024, base=8)

    def body(c_ref, a_ref, b_ref, o_ref):
        o_ref[...] = a_ref[0] + b_ref[...]

    grid_spec = pltpu.PrefetchScalarGridSpec(
        num_scalar_prefetch=1, grid=(n, r // tr),
        in_specs=[pl.BlockSpec((1, 1, tr, LANES), lambda s, i, c_ref: (s, c_ref[0], i, 0)),
                  pl.BlockSpec((1, tr, LANES), lambda s, i, c_ref: (s, i, 0))],
        out_specs=pl.BlockSpec((1, tr, LANES), lambda s, i, c_ref: (s, i, 0)))
    return pl.pallas_call(body, name=name, grid_spec=grid_spec, out_shape=_sds(got.shape, F32),
                          compiler_params=_cp("parallel", "parallel"))(core, g5, got)


def _sum_chips(q, core, *, name):
    n, r, _ = q.shape
    tr = _tile(r, 1024, base=8)

    def body(c_ref, q_ref, o_ref):
        tot = q_ref[0]
        for s in range(1, n):
            tot = tot + q_ref[s]
        o_ref[0] = tot

    grid_spec = pltpu.PrefetchScalarGridSpec(
        num_scalar_prefetch=1, grid=(r // tr,),
        in_specs=[pl.BlockSpec((n, tr, LANES), lambda i, c_ref: (0, i, 0))],
        out_specs=pl.BlockSpec((1, tr, LANES), lambda i, c_ref: (c_ref[0], i, 0)))
    return pl.pallas_call(body, name=name, grid_spec=grid_spec, out_shape=_sds((2, r, LANES), F32),
                          compiler_params=_cp("parallel"))(core, q)


_ANY = pl.BlockSpec(memory_space=pl.ANY)
_MESH = pl.DeviceIdType.MESH


def _place():
    x, y, c = lax.axis_index("x"), lax.axis_index("y"), lax.axis_index("c")
    return x, y, c, [(1 - x, y), (x, 1 - y), (1 - x, 1 - y)]


def _chip_allgather(mine, *, name):
    na = len(mine)

    def body(*refs):
        for phase in range(3):
            _chip_allgather_phase(phase, refs[:na], refs[na:2 * na], *refs[2 * na:])

    return pl.pallas_call(
        body, name=name, in_specs=[_ANY] * na, out_specs=[_ANY] * na,
        out_shape=_chip_allgather_shapes(mine), scratch_shapes=_chip_allgather_sems(na))(*mine)


def _chip_allgather_shapes(mine):
    return [_sds((N_CHIPS,) + a.shape, a.dtype) for a in mine]


def _chip_allgather_sems(na):
    return [pltpu.SemaphoreType.DMA((6 * na,)), pltpu.SemaphoreType.DMA((6 * na,)), pltpu.SemaphoreType.DMA((na,))]


def _chip_allgather_phase(phase, x_refs, o_refs, send_sems, recv_sems, local_sems):
    na = len(x_refs)
    x, y, c, chips = _place()
    k = 2 * x + y

    def copy(a, n, src, dst, to):
        return pltpu.make_async_remote_copy(src_ref=src, dst_ref=dst, send_sem=send_sems.at[6 * a + n],
                                            recv_sem=recv_sems.at[6 * a + n], device_id=to, device_id_type=_MESH)

    locals_ = [pltpu.make_async_copy(x_refs[a], o_refs[a].at[k], local_sems.at[a]) for a in range(na)]
    sends = [copy(a, n, x_refs[a].at[c], o_refs[a].at[k, c], (cx, cy, c))
             for a in range(na) for n, (cx, cy) in enumerate(chips)]
    landed = [(copy(a, n, o_refs[a].at[2 * cx + cy, c], o_refs[a].at[2 * cx + cy, c], (cx, cy, c)),
               copy(a, 3 + n, o_refs[a].at[2 * cx + cy, c], o_refs[a].at[2 * cx + cy, c], (x, y, 1 - c)))
              for a in range(na) for n, (cx, cy) in enumerate(chips)]
    if phase == 0:
        for cp in locals_ + sends:
            cp.start()
    elif phase == 1:
        for arrival, forward in landed:
            arrival.wait_recv()
            forward.start()
    else:
        for a in range(na):
            for n, (cx, cy) in enumerate(chips):
                slab = o_refs[a].at[2 * cx + cy, 1 - c]
                copy(a, 3 + n, slab, slab, (x, y, 1 - c)).wait_recv()
        for cp in sends + [forward for _, forward in landed]:
            cp.wait_send()
        for cp in locals_:
            cp.wait()


def _sibling_swap(g5, *, name):
    n, _, r, _ = g5.shape

    def body(x_ref, o_ref, send_sems, recv_sems):
        x, y, c, _ = _place()
        cps = [pltpu.make_async_remote_copy(src_ref=x_ref.at[s, 1 - c], dst_ref=o_ref.at[s], send_sem=send_sems.at[s],
                                            recv_sem=recv_sems.at[s], device_id=(x, y, 1 - c), device_id_type=_MESH)
               for s in range(n)]
        for cp in cps:
            cp.start()
        for cp in cps:
            cp.wait()

    return pl.pallas_call(
        body, name=name, in_specs=[_ANY], out_specs=_ANY, out_shape=_sds((n, r, LANES), g5.dtype),
        scratch_shapes=[pltpu.SemaphoreType.DMA((n,)), pltpu.SemaphoreType.DMA((n,))])(g5)


_CHIP_EXCHANGE_SEMS = [pltpu.SemaphoreType.DMA((3,)), pltpu.SemaphoreType.DMA((3,)), pltpu.SemaphoreType.DMA]


def _chip_exchange_copies(h_ref, q_ref, send_sems, recv_sems, local_sem):
    x, y, c, chips = _place()
    k = 2 * x + y
    local = pltpu.make_async_copy(h_ref.at[k], q_ref.at[k], local_sem)
    sends, arrivals = [], []
    for n, (cx, cy) in enumerate(chips):
        kk = 2 * cx + cy
        mk = functools.partial(pltpu.make_async_remote_copy, src_ref=h_ref.at[kk], send_sem=send_sems.at[n],
                               recv_sem=recv_sems.at[n], device_id=(cx, cy, c), device_id_type=_MESH)
        sends.append(mk(dst_ref=q_ref.at[k]))
        arrivals.append(mk(dst_ref=q_ref.at[kk]))
    return local, sends, arrivals


def _chip_exchange_start(copies):
    local, sends, _ = copies
    local.start()
    for cp in sends:
        cp.start()


def _chip_exchange_wait(copies):
    local, sends, arrivals = copies
    for cp in arrivals:
        cp.wait_recv()
    for cp in sends:
        cp.wait_send()
    local.wait()


def _chip_exchange(h, *, name):
    def body(h_ref, q_ref, send_sems, recv_sems, local_sem):
        copies = _chip_exchange_copies(h_ref, q_ref, send_sems, recv_sems, local_sem)
        _chip_exchange_start(copies)
        _chip_exchange_wait(copies)

    return pl.pallas_call(body, name=name, in_specs=[_ANY], out_specs=_ANY, out_shape=_sds(h.shape, h.dtype),
                          scratch_shapes=_CHIP_EXCHANGE_SEMS)(h)


def _sibling_allgather(buf, *, name):
    def body(x_ref, o_ref, send_sem, recv_sem):
        x, y, c, _ = _place()
        cp = pltpu.make_async_remote_copy(src_ref=x_ref.at[c], dst_ref=o_ref.at[c], send_sem=send_sem,
                                          recv_sem=recv_sem, device_id=(x, y, 1 - c), device_id_type=_MESH)
        cp.start()
        pltpu.make_async_remote_copy(src_ref=x_ref.at[c], dst_ref=o_ref.at[1 - c], send_sem=send_sem,
                                     recv_sem=recv_sem, device_id=(x, y, 1 - c), device_id_type=_MESH).wait_recv()
        cp.wait_send()

    return pl.pallas_call(
        body, name=name, in_specs=[_ANY], out_specs=_ANY, out_shape=_sds(buf.shape, buf.dtype),
        input_output_aliases={0: 0},
        scratch_shapes=[pltpu.SemaphoreType.DMA, pltpu.SemaphoreType.DMA])(buf)


def _chip_segments(g, kind):
    if kind == "col":
        n = g.shape[-1] // N_CHIPS
        s = g.reshape(g.shape[:-1] + (N_CHIPS, n))
        return jnp.moveaxis(s, -2, 0).reshape(N_CHIPS, -1)
    k = g.shape[-2] // N_CHIPS
    s = g.reshape(g.shape[:-2] + (N_CHIPS, k, g.shape[-1]))
    return jnp.moveaxis(s, -3, 0).reshape(N_CHIPS, -1)


def _join(blocks, kind):
    if kind == "col":
        s = jnp.moveaxis(blocks, 0, -2)
        return s.reshape(s.shape[:-2] + (s.shape[-2] * s.shape[-1],))
    return blocks.reshape((blocks.shape[0] * blocks.shape[1],) + blocks.shape[2:])


def _travel_form(local, names):
    mine = []
    for n in names:
        a = local[n]
        if n == "w_in":
            a = jnp.swapaxes(a, 1, 2)
        if n == "meta_tokens":
            a = a.reshape(2, N_META // 2, a.shape[-1])
        mine.append(a.astype(BF16) if n in GATHER_BF16 else a)
    return mine


def _rope_tables(t):
    half = QK_ROPE // 2
    inv_freq = 1.0 / (ROPE_THETA ** (jnp.arange(0, QK_ROPE, 2, dtype=F32) / QK_ROPE))
    pos = jnp.maximum(jnp.arange(t, dtype=F32) - PAD, 0.0)
    ang = pos[:, None] * inv_freq[None, :]
    cos, sin = jnp.cos(ang), jnp.sin(ang)
    z = jnp.zeros((t, half), F32)
    z2 = jnp.zeros((t, LANES - QK_ROPE), F32)
    return (jnp.concatenate([cos, cos, z2], axis=1), jnp.concatenate([-sin, z, z2], axis=1),
            jnp.concatenate([z, sin, z2], axis=1))


def _expand_matrix():
    lane = np.arange(SSD_INNER) // SSD_HEAD_DIM
    e = (np.arange(LANES)[:, None] == lane[None, :]).astype(np.float32)
    return jnp.asarray(e, BF16)


def _late_weights(full, i):
    w = {}
    kinds = dict(SHARDED)
    whole = lambda n: _join(full[n][:, i], kinds[n])
    for n in ("w_o_attn", "w_o_ssd", "w_out", "w_down"):
        w[n] = whole(n)
    w_up = whole("w_up")
    w["w_up_g"] = w_up[:, :D_FF]
    w["w_up_v"] = w_up[:, D_FF:]
    ffn_w = whole("ffn_conv_w")
    w["ffn_conv_wg"] = ffn_w[:, :D_FF]
    w["ffn_conv_wv"] = ffn_w[:, D_FF:]
    return w


def _early_weights(full, rep, i):
    w = {}
    kinds = dict(SHARDED)
    whole = lambda n: _join(full[n][:, i], kinds[n])
    wt = _join(full["w_in"][:, i], "row")
    zr = lambda n: jnp.zeros((n, D_MODEL), BF16)
    w["w_in_t"] = jnp.concatenate(
        [wt[0:1024], wt[1088:3136], wt[3136:6208], wt[6240:7264], wt[7264:8288],
         wt[1024:1088], zr(LANES - QK_ROPE), wt[6208:6240], zr(LANES - SSD_HEADS)], axis=0)
    wq = whole("w_q_b").reshape(Q_LORA, MLA_HEADS, QK_NOPE + QK_ROPE)
    w["w_q_b"] = jnp.pad(wq, ((0, 0), (0, 0), (0, QHEAD - QK_NOPE - QK_ROPE))).reshape(Q_LORA, MLA_HEADS * QHEAD)
    wkv = whole("w_kv_b").reshape(KV_LORA, MLA_HEADS, 2, QK_NOPE)
    w["w_kv_b"] = jnp.swapaxes(wkv, 1, 2).reshape(KV_LORA, 2 * MLA_HEADS * QK_NOPE)
    w["ssd_conv_w"] = whole("ssd_conv_w")
    row = lambda v: v.reshape(1, -1)
    w["q_norm_g"] = row(rep["q_norm_g"][i])
    w["kv_norm_g"] = row(rep["kv_norm_g"][i])
    w["ssd_conv_b"] = row(rep["ssd_conv_b"][i])
    w["dt_bias"] = row(jnp.pad(rep["dt_bias"][i], (0, LANES - SSD_HEADS)))
    a = -jnp.exp(rep["a_log"][i])
    w["a"] = a
    w["a_x"] = row(jnp.repeat(a, SSD_HEAD_DIM))
    w["d_x"] = row(jnp.repeat(rep["d_skip"][i], SSD_HEAD_DIM))
    w["ssd_norm_g"] = row(rep["ssd_norm_g"][i])
    w["ffn_conv_bg"] = row(rep["ffn_conv_b"][i][:D_FF])
    w["ffn_conv_bv"] = row(rep["ffn_conv_b"][i][D_FF:])
    for n in ("ln1_g", "ln1_b", "ln2_g", "ln2_b"):
        w[n] = row(rep[n][i])
    return w


def _layer_fwd(h, hb, w, tabs, expand, layer, late):
    tag = f"l{layer}_"
    cos, sin_a, sin_b = tabs
    s = {"h": h, "hb": hb}
    proj = _mm(hb, w["w_in_t"], trans_b=True, name=tag + "in_proj", tn=768)
    s["proj"] = proj
    qn = _rms_fwd(proj, OFF_Q, Q_LORA, w["q_norm_g"], name=tag + "q_norm")
    q_raw = _mm(qn, w["w_q_b"], name=tag + "q_up")
    q = _rope_q_fwd(q_raw, cos, sin_a, sin_b, name=tag + "q_rope")
    kvn = _rms_fwd(proj, OFF_KV, KV_LORA, w["kv_norm_g"], name=tag + "kv_norm")
    kv = _mm(kvn, w["w_kv_b"], name=tag + "kv_up", out_dtype=BF16)
    kpe = _rope_k_fwd(proj, cos, sin_a, sin_b, name=tag + "k_rope")
    if isinstance(late, dict):
        o, lse = _flash_fwd(q, kv, kpe, name=tag + "attn")
    else:
        o, lse, *got = _flash_fwd(q, kv, kpe, name=tag + "attn", gather=late)
        late = dict(zip(GATHER_LATE, got))
    w.update(_late_weights(late, layer))
    ya = _mm(o, w["w_o_attn"], name=tag + "attn_out")
    s.update(qn=qn, q=q, kvn=kvn, kv=kv, kpe=kpe, o=o, lse=lse, ya=ya)
    xs, bm, cm, dtx = _ssd_prep_fwd(proj, w["ssd_conv_w"], w["ssd_conv_b"], w["dt_bias"], expand, name=tag + "ssd_prep")
    y, prev = _ssd_fwd(xs, dtx, bm, cm, bm.T, w["a_x"], w["d_x"], name=tag + "ssd_scan")
    yn = _gnorm_fwd(y, proj, w["ssd_norm_g"], name=tag + "ssd_norm")
    ys = _mm(yn, w["w_o_ssd"], name=tag + "ssd_out")
    s.update(xs=xs, bm=bm, cm=cm, dtx=dtx, y=y, prev=prev, yn=yn, ys=ys)
    mixed = _mix_fwd(proj, ya, ys, name=tag + "mix")
    br = _mm(mixed, w["w_out"], name=tag + "mix_out")
    pre1, h1, h1b = _ln_fwd(h, br, w["ln1_g"], w["ln1_b"], name=tag + "ln1")
    s.update(mixed=mixed, pre1=pre1, h1b=h1b)
    ug = _mm(h1b, w["w_up_g"], name=tag + "up_g", tn=1408)
    uv = _mm(h1b, w["w_up_v"], name=tag + "up_v", tn=1408)
    act = _ffn_act_fwd(ug, uv, w["ffn_conv_wg"], w["ffn_conv_wv"], w["ffn_conv_bg"], w["ffn_conv_bv"],
                       name=tag + "ffn_act")
    ffn = _mm(act, w["w_down"], name=tag + "down")
    pre2, h2, h2b = _ln_fwd(h1, ffn, w["ln2_g"], w["ln2_b"], name=tag + "ln2")
    s.update(ug=ug, uv=uv, act=act, pre2=pre2)
    return h2, h2b, s, late


def _layer_bwd(dh2, w, s, tabs, reduce_m, tag, begin_exchange=None):
    cos, sin_a, sin_b = tabs
    g = {}
    proj = s["proj"]
    dpre2, dpre2b, g["ln2_g"], g["ln2_b"] = _ln_bwd(dh2, s["pre2"], w["ln2_g"], name=tag + "ln2_bwd")
    g["w_down"] = _mm_tn(s["act"], dpre2b, name=tag + "down_dw")
    dact = _mm(dpre2b, w["w_down"], trans_b=True, name=tag + "down_dx", tn=1408)
    dcg, dcv, dwg, dwv, dbg, dbv = _ffn_act_bwd(s["ug"], s["uv"], dact, w["ffn_conv_wg"], w["ffn_conv_wv"],
                                                w["ffn_conv_bg"], w["ffn_conv_bv"], name=tag + "ffn_act_bwd")
    g["ffn_conv_w"] = jnp.concatenate([dwg, dwv], axis=1)
    g["ffn_conv_b"] = jnp.concatenate([dbg, dbv], axis=1).reshape(-1)
    dug = _conv_bwd_input(dcg, w["ffn_conv_wg"], FFN_CONV, name=tag + "ffn_conv_bwd_g", tc=1408)
    duv = _conv_bwd_input(dcv, w["ffn_conv_wv"], FFN_CONV, name=tag + "ffn_conv_bwd_v", tc=1408)
    g["w_up"] = jnp.concatenate([_mm_tn(dug, s["h1b"], name=tag + "up_g_dw"),
                                 _mm_tn(duv, s["h1b"], name=tag + "up_v_dw")], axis=0)
    dh1 = _mm(dug, w["w_up_g"], trans_b=True, add=dpre2, add_scale=ALPHA, name=tag + "up_g_dx")
    dh1 = _mm(duv, w["w_up_v"], trans_b=True, add=dh1, name=tag + "up_v_dx")
    dpre1, dpre1b, g["ln1_g"], g["ln1_b"] = _ln_bwd(dh1, s["pre1"], w["ln1_g"], name=tag + "ln1_bwd")
    g["w_out"] = _mm_tn(s["mixed"], dpre1b, name=tag + "mix_out_dw")
    dmix = _mm(dpre1b, w["w_out"], trans_b=True, name=tag + "mix_out_dx")
    dya, dys, dga, dgs = _mix_bwd(dmix, proj, s["ya"], s["ys"], name=tag + "mix_bwd")
    g["w_o_ssd"] = _mm_tn(s["yn"], dys, name=tag + "ssd_out_dw")
    dyn = _mm(dys, w["w_o_ssd"], trans_b=True, out_dtype=BF16, name=tag + "ssd_out_dx")
    dy, dz, dgn = _gnorm_bwd(dyn, s["y"], proj, w["ssd_norm_g"], name=tag + "ssd_norm_bwd")
    g["ssd_norm_g"] = dgn.reshape(-1)
    dxs, ddtx, dbm, dcm, da_x, dd_x = _ssd_bwd(s["xs"], s["dtx"], s["bm"], s["cm"], s["cm"].T, s["prev"], dy,
                                               w["a_x"], w["d_x"], name=tag + "ssd_scan_bwd")
    g["a_log"] = da_x.reshape(SSD_HEADS, SSD_HEAD_DIM).sum(axis=1) * w["a"]
    g["d_skip"] = dd_x.reshape(SSD_HEADS, SSD_HEAD_DIM).sum(axis=1)
    dconv, ddtr, dcw, dcb, ddtb = _ssd_prep_bwd_a(proj, dxs, dbm, dcm, ddtx, w["ssd_conv_w"], w["ssd_conv_b"],
                                                  w["dt_bias"], reduce_m, name=tag + "ssd_prep_bwd")
    g["ssd_conv_w"] = dcw
    g["ssd_conv_b"] = dcb.reshape(-1)
    g["dt_bias"] = ddtb.reshape(-1)[:SSD_HEADS]
    dxbc = _conv_bwd_input(dconv, w["ssd_conv_w"], SSD_CONV, name=tag + "ssd_conv_bwd", tc=1024)
    g["w_o_attn"] = _mm_tn(s["o"], dya, name=tag + "attn_out_dw")
    exchange = begin_exchange(dict(g)) if begin_exchange else None
    do = _mm(dya, w["w_o_attn"], trans_b=True, out_dtype=BF16, name=tag + "attn_out_dx")
    delta = _attn_delta(do, s["o"], name=tag + "attn_delta")
    by_tile = lambda a: a.reshape(MLA_HEADS, -1, _tile(a.shape[1], 640))
    dq, dkn, dkp, dv, *exchanged = _flash_bwd(s["q"], s["kv"], s["kpe"], do, by_tile(s["lse"]), by_tile(delta),
                                              name=tag + "attn_bwd", exchange=exchange)
    dq_raw = _rope_q_bwd(dq, cos, sin_a, sin_b, name=tag + "q_rope_bwd")
    dwq = _mm_tn(s["qn"], dq_raw, name=tag + "q_up_dw")
    g["w_q_b"] = dwq.reshape(Q_LORA, MLA_HEADS, QHEAD)[:, :, :QK_NOPE + QK_ROPE].reshape(Q_LORA, -1)
    dqn = _mm(dq_raw, w["w_q_b"], trans_b=True, out_dtype=BF16, name=tag + "q_up_dx")
    dqlat, dgq = _rms_bwd(dqn, proj, OFF_Q, Q_LORA, w["q_norm_g"], name=tag + "q_norm_bwd")
    g["q_norm_g"] = dgq.reshape(-1)
    dkv = jnp.concatenate([dkn, dv], axis=1)
    dwkv = _mm_tn(s["kvn"], dkv, name=tag + "kv_up_dw")
    g["w_kv_b"] = jnp.swapaxes(dwkv.reshape(KV_LORA, 2, MLA_HEADS, QK_NOPE), 1, 2).reshape(KV_LORA, -1)
    dkvn = _mm(dkv, w["w_kv_b"], trans_b=True, out_dtype=BF16, name=tag + "kv_up_dx")
    dkvlat, dgkv = _rms_bwd(dkvn, proj, OFF_KV, KV_LORA, w["kv_norm_g"], name=tag + "kv_norm_bwd")
    g["kv_norm_g"] = dgkv.reshape(-1)
    dkpe = _rope_k_bwd(dkp, cos, sin_a, sin_b, name=tag + "k_rope_bwd")
    h = s["hb"]
    comps = ((dqlat, OFF_Q), (dkvlat, OFF_KV), (dz, OFF_Z), (dxbc, OFF_XBC), (dga, OFF_GA), (dgs, OFF_GS),
             (dkpe, OFF_KPE), (ddtr, OFF_DT))
    dws = {off: _mm_tn(dc, h, name=f"{tag}in_dw{n}") for n, (dc, off) in enumerate(comps)}
    with_w = lambda group: [(dc, w["w_in_t"][off:off + dc.shape[1]]) for dc, off in group]
    wide = [c for c in comps if c[1] in (OFF_Z, OFF_XBC)]
    rest = [c for c in comps if c[1] not in (OFF_Z, OFF_XBC)]
    dh = _mm_sum(with_w(wide), dpre1, add_scale=ALPHA, name=tag + "in_dx_wide")
    dh = _mm_sum(with_w(rest), dh, name=tag + "in_dx_rest")
    g["w_in"] = jnp.concatenate([dws[OFF_Q], dws[OFF_KV], dws[OFF_KPE][:QK_ROPE], dws[OFF_Z], dws[OFF_XBC],
                                 dws[OFF_DT][:SSD_HEADS], dws[OFF_GA], dws[OFF_GS]], axis=0)
    return dh, g, (exchanged[0] if exchanged else None)


def _local_step(x, target, full, late, rep):
    seq = x.shape[0]
    t = seq + ROW0
    tabs = _rope_tables(t)
    expand = _expand_matrix()
    reduce_m = expand.T
    meta = _join(full["meta_tokens"].reshape(N_CHIPS, N_META, -1), "col")
    xin = jnp.concatenate([jnp.zeros((PAD, D_MODEL), F32), meta, x], axis=0)
    row = lambda v: v.reshape(1, -1)
    _, h, hb = _ln_fwd(xin, None, row(rep["emb_ln_g"]), row(rep["emb_ln_b"]), name="emb_ln")
    ws, saved = [], []
    for i in range(DEPTH):
        w = _early_weights(full, rep, i)
        h, hb, s, late = _layer_fwd(h, hb, w, tabs, expand, i, late)
        ws.append(w)
        saved.append(s)
    dh, loss = _loss_grad(h, target, name="loss")
    dh, g1, _ = _layer_bwd(dh, ws[1], saved[1], tabs, reduce_m, "l1_")
    layouts = []

    def begin_exchange(g0_so_far):
        g4, layout = _pack_part([((1, n), g) for n, g in g1.items()] + [((0, n), g) for n, g in g0_so_far.items()])
        layouts.append(layout)
        return _reduce_begin(g4, "reduce1_")

    dh, g0, parts1 = _layer_bwd(dh, ws[0], saved[0], tabs, reduce_m, "l0_", begin_exchange=begin_exchange)
    dxin, _, dg, db = _ln_bwd(dh, xin, row(rep["emb_ln_g"]), name="emb_ln_bwd")
    rest = [((0, n), g0[n]) for n in LATE_GRADS]
    rest += [((0, "emb_ln_g"), dg), ((0, "emb_ln_b"), db), ((0, "meta_tokens"), dxin[PAD:ROW0])]
    return loss, dxin[ROW0:], rest, parts1, layouts[0]


def _pack_part(entries):
    kinds = dict(SHARDED)
    pieces, layout, tail, row0 = [], [], [], 0

    def add(key, piece):
        nonlocal row0
        pieces.append(piece)
        layout.append((key, row0, piece.shape[1]))
        row0 += piece.shape[1]

    for key, g in entries:
        if key[1] in kinds:
            kind = "row" if key[1] in TRANSPOSED_GRADS else kinds[key[1]]
            add(key, _chip_segments(g, kind).reshape(N_CHIPS, -1, LANES))
    for key, g in entries:
        if key[1] not in kinds and g.size % LANES == 0:
            add(key, jnp.broadcast_to(g.reshape(1, -1, LANES), (N_CHIPS, g.size // LANES, LANES)))
        elif key[1] not in kinds:
            tail.append((key, g.reshape(-1)))
    if tail:
        vec = jnp.concatenate([g for _, g in tail])
        vec = jnp.pad(vec, (0, -vec.shape[0] % LANES)).reshape(1, -1, LANES)
        add(("tail", tuple((key, g.shape[0]) for key, g in tail)), jnp.broadcast_to(vec, (N_CHIPS,) + vec.shape[1:]))
    rows = -(-row0 // REDUCE_ROW_ALIGN) * REDUCE_ROW_ALIGN
    pieces.append(jnp.zeros((N_CHIPS, rows - row0, LANES), F32))
    return jnp.concatenate(pieces, axis=1), layout


def _core_index():
    return lax.axis_index("c").astype(jnp.int32).reshape(1)


def _reduce_begin(g4, tag):
    n, r, _ = g4.shape
    g5 = g4.reshape(n, 2, r // 2, LANES)
    got = _sibling_swap(g5, name=tag + "pair_swap")
    return _pair_add(g5, got, _core_index(), name=tag + "pair_add")


def _reduce_end(parts, tag):
    half = _sum_chips(parts, _core_index(), name=tag + "chip_sum")
    both = _sibling_allgather(half, name=tag + "pair_gather")
    return both.reshape(-1, LANES)


def _unpack_part(flat, layout, shapes):
    out = {}
    for key, row0, rows in layout:
        piece = flat[row0:row0 + rows]
        if key[0] == "tail":
            vec, off = piece.reshape(-1), 0
            for sub, size in key[1]:
                out[sub] = vec[off:off + size]
                off += size
        elif key[1] in TRANSPOSED_GRADS:
            out[key] = piece.reshape(shapes[key[1]][1], shapes[key[1]][0]).T
        else:
            out[key] = piece.reshape(shapes[key[1]])
    return out


def kernel(x, meta_tokens, emb_ln_g, emb_ln_b, w_in, q_norm_g, w_q_b, kv_norm_g, w_kv_b, w_o_attn, ssd_conv_w, ssd_conv_b, dt_bias, a_log, d_skip, ssd_norm_g, w_o_ssd, w_out, ln1_g, ln1_b, w_up, ffn_conv_w, ffn_conv_b, w_down, ln2_g, ln2_b, loss_target, m_meta_tokens, m_emb_ln_g, m_emb_ln_b, m_w_in, m_q_norm_g, m_w_q_b, m_kv_norm_g, m_w_kv_b, m_w_o_attn, m_ssd_conv_w, m_ssd_conv_b, m_dt_bias, m_a_log, m_d_skip, m_ssd_norm_g, m_w_o_ssd, m_w_out, m_ln1_g, m_ln1_b, m_w_up, m_ffn_conv_w, m_ffn_conv_b, m_w_down, m_ln2_g, m_ln2_b, v_meta_tokens, v_emb_ln_g, v_emb_ln_b, v_w_in, v_q_norm_g, v_w_q_b, v_kv_norm_g, v_w_kv_b, v_w_o_attn, v_ssd_conv_w, v_ssd_conv_b, v_dt_bias, v_a_log, v_d_skip, v_ssd_norm_g, v_w_o_ssd, v_w_out, v_ln1_g, v_ln1_b, v_w_up, v_ffn_conv_w, v_ffn_conv_b, v_w_down, v_ln2_g, v_ln2_b):
    given = dict(locals())
    local_w = {n: given[n] for n in WEIGHTS}
    local_m = {n: given["m_" + n] for n in WEIGHTS}
    local_v = {n: given["v_" + n] for n in WEIGHTS}
    full = dict(zip(GATHER_EARLY, _chip_allgather(_travel_form(local_w, GATHER_EARLY), name="gather_early")))
    rep = {n: local_w[n] for n in REPLICATED}
    loss, grad_x, rest, parts1, layout1 = _local_step(x[0], loss_target[0], full, _travel_form(local_w, GATHER_LATE), rep)
    g4, layout0 = _pack_part(rest)
    parts0 = _chip_exchange(_reduce_begin(g4, "reduce0_"), name="reduce0_chip_exchange")
    shapes = {n: (local_w[n].shape if n in OUTSIDE else local_w[n].shape[1:]) for n in WEIGHTS}
    summed = _unpack_part(_reduce_end(parts0, "reduce0_"), layout0, shapes)
    summed.update(_unpack_part(_reduce_end(parts1, "reduce1_"), layout1, shapes))
    grad = {n: (summed[0, n] if n in OUTSIDE else jnp.stack([summed[i, n] for i in range(DEPTH)])) for n in WEIGHTS}
    upd = {}
    small = [n for n in WEIGHTS if n not in GATHER_BF16]
    for n in GATHER_BF16:
        upd[n] = _adamw(grad[n], local_w[n], local_m[n], local_v[n], name="adamw_" + n)
    res = _adamw_small([(grad[n], local_w[n], local_m[n], local_v[n]) for n in small], name="adamw_small")
    upd.update(zip(small, res))
    total = lax.psum(loss[0, 0], ("x", "y", "c"))
    outs = [total, grad_x[None]] + [grad[n] for n in WEIGHTS]
    for q in range(3):
        outs.extend(upd[n][q] for n in WEIGHTS)
    return tuple(outs)
```
